```python
import math
import jax, jax.numpy as jnp
from jax import lax
import numpy as np

D_MODEL = 1024
BATCH = 16
SEQ = 2048
DEPTH = 2

D_MIX = D_MODEL
D_ATTN = D_MIX // 2
D_CONV = D_MIX - D_ATTN
HEAD_DIM = 64
N_HEADS = D_ATTN // HEAD_DIM
CONV_GROUP = 64
N_CONV_GROUPS = D_CONV // CONV_GROUP
DILATED_CONFIGS = ((128, 1), (512, 4), (2048, 16))
BAND_BLOCK = 128
CONV_WIDTH = 3
FFN_CONV_WIDTH = 3
D_FF = 2816
EPS = 1e-6

kernel_name = "hybrid_dilated_attn_shortconv_convffn"


def rmsnorm(x, g):
    xf = x.astype(jnp.float32)
    y = xf * lax.rsqrt(jnp.mean(xf * xf, axis=-1, keepdims=True) + EPS)
    return (y * g.astype(jnp.float32)).astype(x.dtype)


def group_rmsnorm(x, g, n_groups):
    shp = x.shape
    xg = x.reshape(*shp[:-1], n_groups, shp[-1] // n_groups)
    return rmsnorm(xg, g.reshape(n_groups, -1)).reshape(shp)


def causal_dwconv(u, w):
    K = w.shape[0]
    S = u.shape[1]
    up = jnp.pad(u, ((0, 0), (K - 1, 0), (0, 0)))
    return sum(up[:, k:k + S] * w[k].astype(u.dtype) for k in range(K))


def alibi_slopes(n):
    return 2.0 ** (-8.0 * jnp.arange(1, n + 1, dtype=jnp.float32) / n)


def dilated_branch(q, k, v, slopes, window, dilation):
    B, H, S, hd = q.shape
    span = window // dilation
    L = S // dilation
    nb = -(-L // BAND_BLOCK)
    Lp = nb * BAND_BLOCK

    def to_blocks(t):
        t = t.reshape(B, H, L, dilation, hd).transpose(0, 1, 3, 2, 4)
        t = jnp.pad(t, ((0, 0), (0, 0), (0, 0), (0, Lp - L), (0, 0)))
        return t.reshape(B, H, dilation, nb, BAND_BLOCK, hd)

    def with_prev(t):
        prev = jnp.pad(t, ((0, 0), (0, 0), (0, 0), (1, 0), (0, 0), (0, 0)))[:, :, :, :-1]
        return jnp.concatenate([prev, t], axis=-2)

    qb, kb, vb = to_blocks(q), to_blocks(k), to_blocks(v)
    kk, vv = with_prev(kb), with_prev(vb)
    s = jnp.einsum('bhrnqd,bhrnkd->bhrnqk', qb, kk)
    i = jnp.arange(BAND_BLOCK)[:, None]
    j = jnp.arange(2 * BAND_BLOCK)[None, :]
    dist = BAND_BLOCK + i - j
    blk = jnp.arange(nb)[:, None, None]
    valid = (dist >= 0) & (dist <= span) & ((blk > 0) | (j >= BAND_BLOCK))
    bias = -(slopes * dilation).reshape(1, H, 1, 1, 1, 1) * dist.astype(jnp.float32)
    s = jnp.where(valid, s + bias, -jnp.inf)
    m = jnp.max(s, axis=-1)
    p = jnp.exp(s - m[..., None])
    l = jnp.sum(p, axis=-1)
    o = jnp.einsum('bhrnqk,bhrnkd->bhrnqd', p, vv)

    def from_blocks(t):
        rest = t.shape[5:]
        t = t.reshape(B, H, dilation, Lp, *rest)[:, :, :, :L]
        t = jnp.moveaxis(t, 2, 3)
        return t.reshape(B, H, S, *rest)

    return from_blocks(o), from_blocks(m), from_blocks(l)


def dilated_attention(q, k, v):
    slopes = alibi_slopes(q.shape[1])
    outs = [dilated_branch(q, k, v, slopes, w, d) for (w, d) in DILATED_CONFIGS]
    m_max = jnp.max(jnp.stack([m for (_, m, _) in outs]), axis=0)
    num = sum(o * jnp.exp(m - m_max)[..., None] for (o, m, _) in outs)
    den = sum(l * jnp.exp(m - m_max) for (_, m, l) in outs)
    return num / den[..., None]


def hybrid_mixer(h, w_in, mix_conv_w, attn_out_g, conv_out_g, w_out):
    B, S, _ = h.shape
    proj = h @ w_in
    q, k, v, gate_b, gate_c, u = jnp.split(
        proj, [D_ATTN, 2 * D_ATTN, 3 * D_ATTN, 3 * D_ATTN + D_CONV, 3 * D_ATTN + 2 * D_CONV], axis=-1)

    def heads(t):
        return t.reshape(B, S, N_HEADS, HEAD_DIM).transpose(0, 2, 1, 3).astype(jnp.float32)

    attn = dilated_attention(heads(q) * (HEAD_DIM ** -0.5), heads(k), heads(v))
    attn = attn.transpose(0, 2, 1, 3).reshape(B, S, D_ATTN).astype(h.dtype)
    attn = group_rmsnorm(attn, attn_out_g, N_HEADS)

    y = gate_b * causal_dwconv(gate_c * u, mix_conv_w)
    y = group_rmsnorm(y, conv_out_g, N_CONV_GROUPS)

    return jnp.concatenate([attn, y], axis=-1) @ w_out


def conv_glu_ffn(h, ffn_up, ffn_conv_w, ffn_down):
    up = causal_dwconv(h @ ffn_up, ffn_conv_w)
    gate, val = jnp.split(up, 2, axis=-1)
    return (jax.nn.silu(gate) * val) @ ffn_down


def _fwd_setup_inputs(seed: int = 0) -> dict:
    key = jax.random.key(seed)
    ks = jax.random.split(key, 12)
    f32 = jnp.float32
    n = jax.random.normal
    d_in = 3 * D_ATTN + 3 * D_CONV
    return {
        "x": n(ks[0], (BATCH, SEQ, D_MODEL), f32),
        "norm1_g": 1.0 + 0.02 * n(ks[1], (DEPTH, D_MODEL), f32),
        "w_in": n(ks[2], (DEPTH, D_MODEL, d_in), f32) * D_MODEL ** -0.5,
        "mix_conv_w": n(ks[3], (DEPTH, CONV_WIDTH, D_CONV), f32) * CONV_WIDTH ** -0.5,
        "attn_out_g": 1.0 + 0.02 * n(ks[4], (DEPTH, D_ATTN), f32),
        "conv_out_g": 1.0 + 0.02 * n(ks[5], (DEPTH, D_CONV), f32),
        "w_out": n(ks[6], (DEPTH, D_MIX, D_MODEL), f32) * D_MIX ** -0.5,
        "norm2_g": 1.0 + 0.02 * n(ks[7], (DEPTH, D_MODEL), f32),
        "ffn_up": n(ks[8], (DEPTH, D_MODEL, 2 * D_FF), f32) * D_MODEL ** -0.5,
        "ffn_conv_w": n(ks[9], (DEPTH, FFN_CONV_WIDTH, 2 * D_FF), f32) * FFN_CONV_WIDTH ** -0.5,
        "ffn_down": n(ks[10], (DEPTH, D_FF, D_MODEL), f32) * D_FF ** -0.5,
        "final_norm_g": 1.0 + 0.02 * n(ks[11], (D_MODEL,), f32),
    }


def _fwd_reference(x, norm1_g, w_in, mix_conv_w, attn_out_g, conv_out_g, w_out,
              norm2_g, ffn_up, ffn_conv_w, ffn_down, final_norm_g):
    for layer in range(DEPTH):
        h = rmsnorm(x, norm1_g[layer])
        x = x + hybrid_mixer(h, w_in[layer], mix_conv_w[layer], attn_out_g[layer],
                             conv_out_g[layer], w_out[layer])
        h = rmsnorm(x, norm2_g[layer])
        x = x + conv_glu_ffn(h, ffn_up[layer], ffn_conv_w[layer], ffn_down[layer])
    return rmsnorm(x, final_norm_g)


import jax as _jax
import jax.numpy as _jnp

TWIN_FORMAT = 'train_step'
FWD_PARAMS = ['x', 'norm1_g', 'w_in', 'mix_conv_w', 'attn_out_g', 'conv_out_g', 'w_out', 'norm2_g', 'ffn_up', 'ffn_conv_w', 'ffn_down', 'final_norm_g']
TWIN_WEIGHTS = ['norm1_g', 'w_in', 'mix_conv_w', 'attn_out_g', 'conv_out_g', 'w_out', 'norm2_g', 'ffn_up', 'ffn_conv_w', 'ffn_down', 'final_norm_g']
TWIN_DIFF_INPUT = 'x'
TWIN_INPUTS = ['x', 'norm1_g', 'w_in', 'mix_conv_w', 'attn_out_g', 'conv_out_g', 'w_out', 'norm2_g', 'ffn_up', 'ffn_conv_w', 'ffn_down', 'final_norm_g', 'loss_target', 'm_norm1_g', 'm_w_in', 'm_mix_conv_w', 'm_attn_out_g', 'm_conv_out_g', 'm_w_out', 'm_norm2_g', 'm_ffn_up', 'm_ffn_conv_w', 'm_ffn_down', 'm_final_norm_g', 'v_norm1_g', 'v_w_in', 'v_mix_conv_w', 'v_attn_out_g', 'v_conv_out_g', 'v_w_out', 'v_norm2_g', 'v_ffn_up', 'v_ffn_conv_w', 'v_ffn_down', 'v_final_norm_g']
TWIN_OUTPUTS = ['loss', 'grad_x', 'grad_norm1_g', 'grad_w_in', 'grad_mix_conv_w', 'grad_attn_out_g', 'grad_conv_out_g', 'grad_w_out', 'grad_norm2_g', 'grad_ffn_up', 'grad_ffn_conv_w', 'grad_ffn_down', 'grad_final_norm_g', 'delta_norm1_g', 'delta_w_in', 'delta_mix_conv_w', 'delta_attn_out_g', 'delta_conv_out_g', 'delta_w_out', 'delta_norm2_g', 'delta_ffn_up', 'delta_ffn_conv_w', 'delta_ffn_down', 'delta_final_norm_g', 'new_m_norm1_g', 'new_m_w_in', 'new_m_mix_conv_w', 'new_m_attn_out_g', 'new_m_conv_out_g', 'new_m_w_out', 'new_m_norm2_g', 'new_m_ffn_up', 'new_m_ffn_conv_w', 'new_m_ffn_down', 'new_m_final_norm_g', 'new_v_norm1_g', 'new_v_w_in', 'new_v_mix_conv_w', 'new_v_attn_out_g', 'new_v_conv_out_g', 'new_v_w_out', 'new_v_norm2_g', 'new_v_ffn_up', 'new_v_ffn_conv_w', 'new_v_ffn_down', 'new_v_final_norm_g']
TWIN_LEAF_KINDS = {'loss': 'loss', 'grad_x': 'grad_x', 'grad_norm1_g': 'grad_w', 'grad_w_in': 'grad_w', 'grad_mix_conv_w': 'grad_w', 'grad_attn_out_g': 'grad_w', 'grad_conv_out_g': 'grad_w', 'grad_w_out': 'grad_w', 'grad_norm2_g': 'grad_w', 'grad_ffn_up': 'grad_w', 'grad_ffn_conv_w': 'grad_w', 'grad_ffn_down': 'grad_w', 'grad_final_norm_g': 'grad_w', 'delta_norm1_g': 'delta_w', 'delta_w_in': 'delta_w', 'delta_mix_conv_w': 'delta_w', 'delta_attn_out_g': 'delta_w', 'delta_conv_out_g': 'delta_w', 'delta_w_out': 'delta_w', 'delta_norm2_g': 'delta_w', 'delta_ffn_up': 'delta_w', 'delta_ffn_conv_w': 'delta_w', 'delta_ffn_down': 'delta_w', 'delta_final_norm_g': 'delta_w', 'new_m_norm1_g': 'new_m', 'new_m_w_in': 'new_m', 'new_m_mix_conv_w': 'new_m', 'new_m_attn_out_g': 'new_m', 'new_m_conv_out_g': 'new_m', 'new_m_w_out': 'new_m', 'new_m_norm2_g': 'new_m', 'new_m_ffn_up': 'new_m', 'new_m_ffn_conv_w': 'new_m', 'new_m_ffn_down': 'new_m', 'new_m_final_norm_g': 'new_m', 'new_v_norm1_g': 'new_v', 'new_v_w_in': 'new_v', 'new_v_mix_conv_w': 'new_v', 'new_v_attn_out_g': 'new_v', 'new_v_conv_out_g': 'new_v', 'new_v_w_out': 'new_v', 'new_v_norm2_g': 'new_v', 'new_v_ffn_up': 'new_v', 'new_v_ffn_conv_w': 'new_v', 'new_v_ffn_down': 'new_v', 'new_v_final_norm_g': 'new_v'}


def _forward(args):
    return _fwd_reference(*[args[k] for k in FWD_PARAMS])


def _output_shape():
    out = _jax.eval_shape(lambda: _forward(_fwd_setup_inputs(0)))
    return out.shape, out.dtype

N_MICROBATCH = 1
ADAM_LR = 0.001
ADAM_B1 = 0.9
ADAM_B2 = 0.999
ADAM_EPS = 1e-08
ADAM_WD = 0.01
ADAM_STEP = 10
PER_EXAMPLE_BATCH_AXIS = {'x': 0, 'loss_target': 0}
SHARED_INPUTS = []
_WEIGHT_DTYPES = {'norm1_g': _jnp.float32, 'w_in': _jnp.float32, 'mix_conv_w': _jnp.float32, 'attn_out_g': _jnp.float32, 'conv_out_g': _jnp.float32, 'w_out': _jnp.float32, 'norm2_g': _jnp.float32, 'ffn_up': _jnp.float32, 'ffn_conv_w': _jnp.float32, 'ffn_down': _jnp.float32, 'final_norm_g': _jnp.float32}
MOMENT_SCALE = {'norm1_g': 2.332389e-01, 'w_in': 1.349766e-01, 'mix_conv_w': 1.528080e-01, 'attn_out_g': 1.481559e-01, 'conv_out_g': 1.424055e-01, 'w_out': 1.424379e-01, 'norm2_g': 1.061733e-01, 'ffn_up': 4.523846e-02, 'ffn_conv_w': 4.532266e-02, 'ffn_down': 7.412599e-02, 'final_norm_g': 3.198735e+01}


def _to_microbatches(a, axis):
    t = _jnp.moveaxis(a, axis, 0)
    t = t.reshape((N_MICROBATCH, t.shape[0] // N_MICROBATCH) + t.shape[1:])
    return _jnp.moveaxis(t, 1, axis + 1)


def setup_inputs(seed: int = 0) -> dict:
    inp = _fwd_setup_inputs(seed)
    key = _jax.random.fold_in(_jax.random.key(seed), 7919)
    shape, _ = _output_shape()
    out = dict(inp)
    out["loss_target"] = _jax.random.normal(_jax.random.fold_in(key, 0), shape, _jnp.float32)
    for i, name in enumerate(TWIN_WEIGHTS):
        w = inp[name].astype(_jnp.float32)
        if MOMENT_SCALE is None:
            s = _jnp.sqrt(_jnp.mean(_jnp.square(w)) + 1e-30)
        else:
            s = MOMENT_SCALE[name]
        km, kv = _jax.random.split(_jax.random.fold_in(key, i + 1))
        out[name] = w
        out["m_" + name] = s * _jax.random.normal(km, w.shape, _jnp.float32)
        out["v_" + name] = (s * s) * _jax.random.uniform(kv, w.shape, _jnp.float32, 0.5, 1.5)
    if N_MICROBATCH > 1:
        for name, axis in PER_EXAMPLE_BATCH_AXIS.items():
            out[name] = _to_microbatches(out[name], axis)
    return {'x': out['x'], 'norm1_g': out['norm1_g'], 'w_in': out['w_in'], 'mix_conv_w': out['mix_conv_w'], 'attn_out_g': out['attn_out_g'], 'conv_out_g': out['conv_out_g'], 'w_out': out['w_out'], 'norm2_g': out['norm2_g'], 'ffn_up': out['ffn_up'], 'ffn_conv_w': out['ffn_conv_w'], 'ffn_down': out['ffn_down'], 'final_norm_g': out['final_norm_g'], 'loss_target': out['loss_target'], 'm_norm1_g': out['m_norm1_g'], 'm_w_in': out['m_w_in'], 'm_mix_conv_w': out['m_mix_conv_w'], 'm_attn_out_g': out['m_attn_out_g'], 'm_conv_out_g': out['m_conv_out_g'], 'm_w_out': out['m_w_out'], 'm_norm2_g': out['m_norm2_g'], 'm_ffn_up': out['m_ffn_up'], 'm_ffn_conv_w': out['m_ffn_conv_w'], 'm_ffn_down': out['m_ffn_down'], 'm_final_norm_g': out['m_final_norm_g'], 'v_norm1_g': out['v_norm1_g'], 'v_w_in': out['v_w_in'], 'v_mix_conv_w': out['v_mix_conv_w'], 'v_attn_out_g': out['v_attn_out_g'], 'v_conv_out_g': out['v_conv_out_g'], 'v_w_out': out['v_w_out'], 'v_norm2_g': out['v_norm2_g'], 'v_ffn_up': out['v_ffn_up'], 'v_ffn_conv_w': out['v_ffn_conv_w'], 'v_ffn_down': out['v_ffn_down'], 'v_final_norm_g': out['v_final_norm_g']}


def _loss(weights, diff, rest, loss_target):
    with _jax.named_scope("forward"):
        args = {**rest, TWIN_DIFF_INPUT: diff, **{k: w.astype(_WEIGHT_DTYPES[k]) for k, w in weights.items()}}
        y = _forward(args)
    with _jax.named_scope("loss_head"):
        err = _jnp.square(y.astype(_jnp.float32) - loss_target)
        return 0.5 * _jnp.sum(_jnp.mean(err, axis=-1)) if err.ndim else 0.5 * err


def _adamw(w, g, m, v):
    m = ADAM_B1 * m + (1.0 - ADAM_B1) * g
    v = ADAM_B2 * v + (1.0 - ADAM_B2) * _jnp.square(g)
    m_hat = m / (1.0 - ADAM_B1 ** ADAM_STEP)
    v_hat = v / (1.0 - ADAM_B2 ** ADAM_STEP)
    delta = -ADAM_LR * (m_hat / (_jnp.sqrt(v_hat) + ADAM_EPS) + ADAM_WD * w)
    return delta, m, v


def reference(x, norm1_g, w_in, mix_conv_w, attn_out_g, conv_out_g, w_out, norm2_g, ffn_up, ffn_conv_w, ffn_down, final_norm_g, loss_target, m_norm1_g, m_w_in, m_mix_conv_w, m_attn_out_g, m_conv_out_g, m_w_out, m_norm2_g, m_ffn_up, m_ffn_conv_w, m_ffn_down, m_final_norm_g, v_norm1_g, v_w_in, v_mix_conv_w, v_attn_out_g, v_conv_out_g, v_w_out, v_norm2_g, v_ffn_up, v_ffn_conv_w, v_ffn_down, v_final_norm_g):
    given = dict(x=x, norm1_g=norm1_g, w_in=w_in, mix_conv_w=mix_conv_w, attn_out_g=attn_out_g, conv_out_g=conv_out_g, w_out=w_out, norm2_g=norm2_g, ffn_up=ffn_up, ffn_conv_w=ffn_conv_w, ffn_down=ffn_down, final_norm_g=final_norm_g, loss_target=loss_target, m_norm1_g=m_norm1_g, m_w_in=m_w_in, m_mix_conv_w=m_mix_conv_w, m_attn_out_g=m_attn_out_g, m_conv_out_g=m_conv_out_g, m_w_out=m_w_out, m_norm2_g=m_norm2_g, m_ffn_up=m_ffn_up, m_ffn_conv_w=m_ffn_conv_w, m_ffn_down=m_ffn_down, m_final_norm_g=m_final_norm_g, v_norm1_g=v_norm1_g, v_w_in=v_w_in, v_mix_conv_w=v_mix_conv_w, v_attn_out_g=v_attn_out_g, v_conv_out_g=v_conv_out_g, v_w_out=v_w_out, v_norm2_g=v_norm2_g, v_ffn_up=v_ffn_up, v_ffn_conv_w=v_ffn_conv_w, v_ffn_down=v_ffn_down, v_final_norm_g=v_final_norm_g)
    weights = {n: given[n] for n in TWIN_WEIGHTS}
    shared = {n: given[n] for n in SHARED_INPUTS}
    per_example = {n: given[n] for n in ['x']}
    grad_fn = _jax.value_and_grad(_loss, argnums=(0, 1))

    def one_microbatch(ex, loss_target):
        ex = dict(ex)
        diff = ex.pop(TWIN_DIFF_INPUT)
        return grad_fn(weights, diff, {**shared, **ex}, loss_target)

    if N_MICROBATCH == 1:
        loss, (grad_w, grad_x) = one_microbatch(per_example, given["loss_target"])
    else:
        def body(carry, xs):
            loss_sum, grad_sum = carry
            l_k, (gw_k, gx_k) = one_microbatch(xs[0], xs[1])
            with _jax.named_scope("update"):
                return (loss_sum + l_k, _jax.tree.map(_jnp.add, grad_sum, gw_k)), gx_k

        init = (_jnp.zeros((), _jnp.float32), _jax.tree.map(_jnp.zeros_like, weights))
        (loss, grad_w), grad_x = _jax.lax.scan(body, init, (per_example, given["loss_target"]))
    with _jax.named_scope("update"):
        delta_w, new_m, new_v = {}, {}, {}
        for n in TWIN_WEIGHTS:
            delta_w[n], new_m[n], new_v[n] = _adamw(weights[n], grad_w[n], given["m_" + n], given["v_" + n])
    return (loss, grad_x, *[grad_w[n] for n in TWIN_WEIGHTS], *[delta_w[n] for n in TWIN_WEIGHTS],
            *[new_m[n] for n in TWIN_WEIGHTS], *[new_v[n] for n in TWIN_WEIGHTS])
```

```python
import functools
import math

import jax
import jax.numpy as jnp
from jax import lax
from jax.experimental import pallas as pl
from jax.experimental.pallas import tpu as pltpu

F32 = jnp.float32
BF16 = jnp.bfloat16
EPS = 1e-6
GROUP = 64
LANES = 128
BAND = 128
DILATIONS = (1, 4, 16)
NEG = -1e30
MIB = 1024 * 1024
MESH_ID = pl.DeviceIdType.MESH

ADAM_LR = 0.001
ADAM_B1 = 0.9
ADAM_B2 = 0.999
ADAM_EPS = 1e-08
ADAM_WD = 0.01
ADAM_STEP = 10


def _params(sem=None, vmem_mb=48):
    return pltpu.CompilerParams(dimension_semantics=sem, vmem_limit_bytes=vmem_mb * MIB)


def _nt(a, b):
    return lax.dot_general(a, b, (((1,), (1,)), ((), ())), preferred_element_type=F32)


def _tn(a, b):
    return lax.dot_general(a, b, (((0,), (0,)), ((), ())), preferred_element_type=F32)


def _seg_sum(x, is_a):
    s_a = jnp.sum(jnp.where(is_a, x, 0.0), axis=-1, keepdims=True)
    s_b = jnp.sum(jnp.where(is_a, 0.0, x), axis=-1, keepdims=True)
    return jnp.where(is_a, s_a, s_b)


def _lane_is_a():
    return lax.broadcasted_iota(jnp.int32, (1, LANES), 1) < GROUP


def _norm_proj(x, g, w, groups, tm, chunk, name):
    T, D = x.shape
    N = w.shape[1]
    assert sum(p * c for p, c, _ in groups) == N and T % tm == 0

    def body(x_ref, g_ref, w_ref, h_ref, *out_refs):
        xv = x_ref[...]
        rstd = lax.rsqrt(jnp.mean(xv * xv, axis=-1, keepdims=True) + EPS)
        h = ((xv * rstd) * g_ref[...]).astype(BF16)
        h_ref[...] = h
        col = 0
        for (pieces, width, dtype), o_ref in zip(groups, out_refs):
            for p in range(pieces):
                for c0 in range(0, width, chunk):
                    acc = jnp.dot(h, w_ref[:, col + c0:col + c0 + chunk], preferred_element_type=F32)
                    o_ref[p, :, c0:c0 + chunk] = acc.astype(dtype)
                col += width

    out_shape = [jax.ShapeDtypeStruct((T, D), BF16)]
    out_specs = [pl.BlockSpec((tm, D), lambda i: (i, 0))]
    for pieces, width, dtype in groups:
        assert width % chunk == 0
        out_shape.append(jax.ShapeDtypeStruct((pieces, T, width), dtype))
        out_specs.append(pl.BlockSpec((pieces, tm, width), lambda i: (0, i, 0)))
    return pl.pallas_call(
        body, grid=(T // tm,), name=name,
        in_specs=[pl.BlockSpec((tm, D), lambda i: (i, 0)),
                  pl.BlockSpec((1, D), lambda i: (0, 0)),
                  pl.BlockSpec((D, N), lambda i: (0, 0))],
        out_specs=out_specs, out_shape=out_shape,
        compiler_params=_params(("parallel",), 56),
    )(x, g, w)


def _proj_residual(pieces3, w, x, tm, name):
    P, T, C = pieces3.shape
    D = w.shape[1]

    def body(a_ref, w_ref, x_ref, o_ref):
        acc = x_ref[...]
        for p in range(P):
            acc = acc + jnp.dot(a_ref[p], w_ref[p * C:(p + 1) * C, :], preferred_element_type=F32)
        o_ref[...] = acc

    return pl.pallas_call(
        body, grid=(T // tm,), name=name,
        in_specs=[pl.BlockSpec((P, tm, C), lambda i: (0, i, 0)),
                  pl.BlockSpec((P * C, D), lambda i: (0, 0)),
                  pl.BlockSpec((tm, D), lambda i: (i, 0))],
        out_specs=pl.BlockSpec((tm, D), lambda i: (i, 0)),
        out_shape=jax.ShapeDtypeStruct((T, D), F32),
        compiler_params=_params(("parallel",)),
    )(pieces3, w, x)


def _grad_through_weight(dy, w, pieces, width, out_dtype, tm, chunk, name):
    T, D = dy.shape

    def body(dy_ref, w_ref, dyb_ref, o_ref):
        dyb = dy_ref[...].astype(BF16)
        dyb_ref[...] = dyb
        for p in range(pieces):
            for c0 in range(0, width, chunk):
                r0 = p * width + c0
                o_ref[p, :, c0:c0 + chunk] = _nt(dyb, w_ref[r0:r0 + chunk, :]).astype(out_dtype)

    return pl.pallas_call(
        body, grid=(T // tm,), name=name,
        in_specs=[pl.BlockSpec((tm, D), lambda i: (i, 0)),
                  pl.BlockSpec((pieces * width, D), lambda i: (0, 0))],
        out_specs=[pl.BlockSpec((tm, D), lambda i: (i, 0)),
                   pl.BlockSpec((pieces, tm, width), lambda i: (0, i, 0))],
        out_shape=[jax.ShapeDtypeStruct((T, D), BF16),
                   jax.ShapeDtypeStruct((pieces, T, width), out_dtype)],
        compiler_params=_params(("parallel",)),
    )(dy, w)


def _grad_through_proj_norm(dp3, w, x, g, dx_in, tm, name):
    P, T, C = dp3.shape
    D = w.shape[0]

    def body(dp_ref, w_ref, x_ref, g_ref, dxin_ref, dx_ref, dg_ref):
        dh = _nt(dp_ref[0], w_ref[:, 0:C])
        for p in range(1, P):
            dh = dh + _nt(dp_ref[p], w_ref[:, p * C:(p + 1) * C])
        xv = x_ref[...]
        rstd = lax.rsqrt(jnp.mean(xv * xv, axis=-1, keepdims=True) + EPS)
        xn = xv * rstd
        a = dh * g_ref[...]
        dx_ref[...] = dxin_ref[...] + rstd * (a - xn * jnp.mean(a * xn, axis=-1, keepdims=True))
        part = jnp.sum(dh * xn, axis=0, keepdims=True)

        @pl.when(pl.program_id(0) == 0)
        def _():
            dg_ref[...] = part

        @pl.when(pl.program_id(0) != 0)
        def _():
            dg_ref[...] += part

    return pl.pallas_call(
        body, grid=(T // tm,), name=name,
        in_specs=[pl.BlockSpec((P, tm, C), lambda i: (0, i, 0)),
                  pl.BlockSpec((D, P * C), lambda i: (0, 0)),
                  pl.BlockSpec((tm, D), lambda i: (i, 0)),
                  pl.BlockSpec((1, D), lambda i: (0, 0)),
                  pl.BlockSpec((tm, D), lambda i: (i, 0))],
        out_specs=[pl.BlockSpec((tm, D), lambda i: (i, 0)),
                   pl.BlockSpec((1, D), lambda i: (0, 0))],
        out_shape=[jax.ShapeDtypeStruct((T, D), F32), jax.ShapeDtypeStruct((1, D), F32)],
        compiler_params=_params(("arbitrary",), 56),
    )(dp3, w, x, g, dx_in)


def _weight_grad(a3, g3, ta, tg, tt, name):
    PA, T, CA = a3.shape
    PG, _, CG = g3.shape
    na, ng, nt = CA // ta, CG // tg, T // tt
    assert CA % ta == 0 and CG % tg == 0 and T % tt == 0

    def body(a_ref, g_ref, o_ref, acc_ref):
        t = pl.program_id(2)
        part = _tn(a_ref[...], g_ref[...])

        @pl.when(t == 0)
        def _():
            acc_ref[...] = part

        @pl.when(t != 0)
        def _():
            acc_ref[...] += part

        @pl.when(t == nt - 1)
        def _():
            o_ref[...] = acc_ref[...].astype(o_ref.dtype)

    return pl.pallas_call(
        body, grid=(PA * na, PG * ng, nt), name=name,
        in_specs=[pl.BlockSpec((None, tt, ta), lambda i, j, t: (i // na, t, i % na)),
                  pl.BlockSpec((None, tt, tg), lambda i, j, t: (j // ng, t, j % ng))],
        out_specs=pl.BlockSpec((ta, tg), lambda i, j, t: (i, j)),
        out_shape=jax.ShapeDtypeStruct((PA * CA, PG * CG), BF16),
        scratch_shapes=[pltpu.VMEM((ta, tg), F32)],
        compiler_params=_params(("parallel", "parallel", "arbitrary"), 56),
    )(a3, g3)


def _bias_tables(bm_ref, pair, n_heads):
    ii = lax.broadcasted_iota(jnp.int32, (BAND, 2 * BAND), 0)
    jj = lax.broadcasted_iota(jnp.int32, (BAND, 2 * BAND), 1)
    dist = BAND + ii - jj
    valid = (dist >= 0) & (dist <= BAND)
    distf = dist.astype(F32)
    for hh in range(2):
        head = (2 * pair + hh + 1).astype(F32)
        slope = jnp.exp(jnp.full((1, 1), -8.0 / n_heads * math.log(2.0), F32) * head)
        for bi, d in enumerate(DILATIONS):
            bm_ref[hh * len(DILATIONS) + bi] = jnp.where(valid, -(slope * d) * distf, NEG)


def _gather_residues(dst_ref, src, d, S, convert):
    L = S // d
    for r in range(d):
        rows = pl.ds(r, L, stride=d) if d > 1 else slice(None)
        dst_ref[r * L:(r + 1) * L, :] = convert(src(rows))


def _block_rows(t, d, S):
    nb = S // (BAND * d)
    n = t % nb
    has_prev = jnp.minimum(n, 1)
    cur = pl.ds(pl.multiple_of(t * BAND, BAND), BAND)
    prev = pl.ds(pl.multiple_of((t - has_prev) * BAND, BAND), BAND)
    return cur, prev, has_prev


def _first_block_penalty(has_prev):
    jrow = lax.broadcasted_iota(jnp.int32, (1, 2 * BAND), 1)
    pen = jnp.where(has_prev == 0, NEG, 0.0).astype(F32)
    return jnp.where(jrow < BAND, pen, 0.0)


def _attn_fwd(qkv3, gain, mix_shape_pieces, S, n_heads, name):
    _, T, C = qkv3.shape
    B, P = T // S, C // LANES
    NBLK = S // BAND
    scale = GROUP ** -0.5
    nbr = len(DILATIONS)
    RC = 256

    def body(qkv_ref, g_ref, o_ref, lse_ref, an_ref, qs, ks, vs, op, mp, lp, ob, mb, lb, bm):
        pair = pl.program_id(1)
        is_a = _lane_is_a()
        _bias_tables(bm, pair, n_heads)

        for bi, d in enumerate(DILATIONS):
            nb = S // (BAND * d)
            _gather_residues(qs, lambda rows: qkv_ref.at[0][rows, :], d, S, lambda v: (v * scale).astype(BF16))
            _gather_residues(ks, lambda rows: qkv_ref.at[1][rows, :], d, S, lambda v: v.astype(BF16))
            _gather_residues(vs, lambda rows: qkv_ref.at[2][rows, :], d, S, lambda v: v.astype(BF16))
            o_dst, m_dst, l_dst = (ob.at[bi], mb.at[bi], lb.at[bi]) if d == 1 else (op, mp, lp)

            def block(t, carry, bi=bi, d=d, nb=nb, o_dst=o_dst, m_dst=m_dst, l_dst=l_dst):
                cur, prev, has_prev = _block_rows(t, d, S)
                q = qs[cur, :]
                if nb > 1:
                    kc = jnp.concatenate([ks[prev, :], ks[cur, :]], axis=0)
                    vc = jnp.concatenate([vs[prev, :], vs[cur, :]], axis=0)
                    pen = _first_block_penalty(has_prev)
                else:
                    kc, vc = ks[cur, :], vs[cur, :]
                outs = []
                for hh in range(2):
                    qm = jnp.where(is_a == (hh == 0), q, jnp.zeros_like(q))
                    s = _nt(qm, kc)
                    if nb > 1:
                        s = s + bm[hh * nbr + bi] + pen
                    else:
                        s = s + bm[hh * nbr + bi, :, BAND:2 * BAND]
                    m = jnp.max(s, axis=-1, keepdims=True)
                    e = jnp.exp(s - m)
                    l = jnp.sum(e, axis=-1, keepdims=True)
                    pv = jnp.dot(e.astype(BF16), vc, preferred_element_type=F32)
                    outs.append((pv, m, l))
                o_dst[cur, :] = jnp.where(is_a, outs[0][0], outs[1][0])
                m_dst[cur, :] = jnp.where(is_a, outs[0][1], outs[1][1])
                l_dst[cur, :] = jnp.where(is_a, outs[0][2], outs[1][2])
                return carry

            lax.fori_loop(0, NBLK, block, 0)
            if d > 1:
                L = S // d
                for r in range(d):
                    rows = pl.ds(r, L, stride=d)
                    ob.at[bi][rows, :] = op[r * L:(r + 1) * L, :]
                    mb.at[bi][rows, :] = mp[r * L:(r + 1) * L, :]
                    lb.at[bi][rows, :] = lp[r * L:(r + 1) * L, :]

        def finish(ci, carry):
            rs = pl.ds(pl.multiple_of(ci * RC, RC), RC)
            ms = [mb[bi, rs, :] for bi in range(nbr)]
            mmax = functools.reduce(jnp.maximum, ms)
            ws = [jnp.exp(m - mmax) for m in ms]
            num = sum(ob[bi, rs, :] * ws[bi] for bi in range(nbr))
            den = sum(lb[bi, rs, :] * ws[bi] for bi in range(nbr))
            o = num / den
            o_ref[rs, :] = o
            lse_ref[rs, :] = mmax + jnp.log(den)
            rstd = lax.rsqrt(_seg_sum(o * o, is_a) * (1.0 / GROUP) + EPS)
            an_ref[rs, :] = ((o * rstd) * g_ref[...]).astype(BF16)
            return carry

        lax.fori_loop(0, S // RC, finish, 0)

    seq = pl.BlockSpec((S, LANES), lambda b, p: (b, p))
    return pl.pallas_call(
        body, grid=(B, P), name=name,
        in_specs=[pl.BlockSpec((3, S, LANES), lambda b, p: (0, b, p)),
                  pl.BlockSpec((1, LANES), lambda b, p: (0, p))],
        out_specs=[seq, seq, pl.BlockSpec((None, S, LANES), lambda b, p: (0, b, p))],
        out_shape=[jax.ShapeDtypeStruct((T, C), F32), jax.ShapeDtypeStruct((T, C), F32),
                   jax.ShapeDtypeStruct((mix_shape_pieces, T, C), BF16)],
        scratch_shapes=[pltpu.VMEM((S, LANES), BF16)] * 3 + [pltpu.VMEM((S, LANES), F32)] * 3
        + [pltpu.VMEM((nbr, S, LANES), F32)] * 3 + [pltpu.VMEM((2 * nbr, BAND, 2 * BAND), F32)],
        compiler_params=_params(("parallel", "parallel")),
    )(qkv3, gain)


def _attn_bwd(qkv3, o, lse, dmix3, gain, dproj_pieces, S, n_heads, name):
    _, T, C = qkv3.shape
    B, P = T // S, C // LANES
    NBLK = S // BAND
    scale = GROUP ** -0.5
    nbr = len(DILATIONS)
    RC = 256

    def body(qkv_ref, o_ref, lse_ref, dn_ref, g_ref, dqkv_ref, dg_ref,
             do_n, dd_n, qs, ks, vs, dos, lses, dds, dqp, dkp, dvp, dqn, dkn, dvn, bm):
        pair = pl.program_id(0)
        b = pl.program_id(1)
        is_a = _lane_is_a()
        _bias_tables(bm, pair, n_heads)

        def prologue(ci, dg_acc):
            rs = pl.ds(pl.multiple_of(ci * RC, RC), RC)
            ov = o_ref[rs, :]
            dn = dn_ref[rs, :]
            rstd = lax.rsqrt(_seg_sum(ov * ov, is_a) * (1.0 / GROUP) + EPS)
            on = ov * rstd
            a = dn * g_ref[...]
            do = rstd * (a - on * (_seg_sum(a * on, is_a) * (1.0 / GROUP)))
            do_n[rs, :] = do
            dd_n[rs, :] = _seg_sum(do * ov, is_a)
            zero = jnp.zeros((RC, LANES), F32)
            dqn[rs, :] = zero
            dkn[rs, :] = zero
            dvn[rs, :] = zero
            return dg_acc + jnp.sum(dn * on, axis=0, keepdims=True)

        dg_part = lax.fori_loop(0, S // RC, prologue, jnp.zeros((1, LANES), F32))

        @pl.when(b == 0)
        def _():
            dg_ref[...] = dg_part

        @pl.when(b != 0)
        def _():
            dg_ref[...] += dg_part

        for bi, d in enumerate(DILATIONS):
            nb = S // (BAND * d)
            L = S // d
            _gather_residues(qs, lambda rows: qkv_ref.at[0][rows, :], d, S, lambda v: (v * scale).astype(BF16))
            _gather_residues(ks, lambda rows: qkv_ref.at[1][rows, :], d, S, lambda v: v.astype(BF16))
            _gather_residues(vs, lambda rows: qkv_ref.at[2][rows, :], d, S, lambda v: v.astype(BF16))
            _gather_residues(dos, lambda rows: do_n[rows, :], d, S, lambda v: v.astype(BF16))
            _gather_residues(lses, lambda rows: lse_ref[rows, :], d, S, lambda v: v)
            _gather_residues(dds, lambda rows: dd_n[rows, :], d, S, lambda v: v)
            dkp[...] = jnp.zeros((S, LANES), F32)
            dvp[...] = jnp.zeros((S, LANES), F32)

            def block(t, carry, bi=bi, d=d, nb=nb):
                cur, prev, has_prev = _block_rows(t, d, S)
                q = qs[cur, :]
                do = dos[cur, :]
                lse_t = lses[cur, :]
                dd_t = dds[cur, :]
                if nb > 1:
                    kc = jnp.concatenate([ks[prev, :], ks[cur, :]], axis=0)
                    vc = jnp.concatenate([vs[prev, :], vs[cur, :]], axis=0)
                    pen = _first_block_penalty(has_prev)
                else:
                    kc, vc = ks[cur, :], vs[cur, :]
                dq = None
                dk = None
                dv = None
                for hh in range(2):
                    mine = is_a == (hh == 0)
                    qm = jnp.where(mine, q, jnp.zeros_like(q))
                    dom = jnp.where(mine, do, jnp.zeros_like(do))
                    c0 = hh * GROUP
                    s = _nt(qm, kc)
                    if nb > 1:
                        s = s + bm[hh * nbr + bi] + pen
                    else:
                        s = s + bm[hh * nbr + bi, :, BAND:2 * BAND]
                    p = jnp.exp(s - lse_t[:, c0:c0 + 1])
                    dp = _nt(dom, vc)
                    ds = (p * (dp - dd_t[:, c0:c0 + 1])).astype(BF16)
                    pb = p.astype(BF16)
                    dq_h = jnp.dot(ds, kc, preferred_element_type=F32)
                    dk_h = _tn(ds, qm)
                    dv_h = _tn(pb, dom)
                    dq = dq_h if dq is None else jnp.where(is_a, dq, dq_h)
                    dk = dk_h if dk is None else dk + dk_h
                    dv = dv_h if dv is None else dv + dv_h
                dqp[cur, :] = dq
                if nb > 1:
                    dkp[prev, :] += dk[0:BAND, :]
                    dvp[prev, :] += dv[0:BAND, :]
                    dkp[cur, :] += dk[BAND:2 * BAND, :]
                    dvp[cur, :] += dv[BAND:2 * BAND, :]
                else:
                    dkp[cur, :] += dk
                    dvp[cur, :] += dv
                return carry

            lax.fori_loop(0, NBLK, block, 0)
            for r in range(d):
                rows = pl.ds(r, L, stride=d) if d > 1 else slice(None)
                dqn[rows, :] += dqp[r * L:(r + 1) * L, :]
                dkn[rows, :] += dkp[r * L:(r + 1) * L, :]
                dvn[rows, :] += dvp[r * L:(r + 1) * L, :]

        dqkv_ref[0] = (dqn[...] * scale).astype(BF16)
        dqkv_ref[1] = dkn[...].astype(BF16)
        dqkv_ref[2] = dvn[...].astype(BF16)

    seq = pl.BlockSpec((S, LANES), lambda p, b: (b, p))
    f32_seq = pltpu.VMEM((S, LANES), F32)
    bf_seq = pltpu.VMEM((S, LANES), BF16)
    return pl.pallas_call(
        body, grid=(P, B), name=name,
        in_specs=[pl.BlockSpec((3, S, LANES), lambda p, b: (0, b, p)), seq, seq,
                  pl.BlockSpec((None, S, LANES), lambda p, b: (0, b, p)),
                  pl.BlockSpec((1, LANES), lambda p, b: (0, p))],
        out_specs=[pl.BlockSpec((3, S, LANES), lambda p, b: (0, b, p)),
                   pl.BlockSpec((1, LANES), lambda p, b: (0, p))],
        out_shape=[jax.ShapeDtypeStruct((dproj_pieces, T, C), BF16), jax.ShapeDtypeStruct((1, C), F32)],
        scratch_shapes=[f32_seq, f32_seq, bf_seq, bf_seq, bf_seq, bf_seq, f32_seq, f32_seq,
                        f32_seq, f32_seq, f32_seq, f32_seq, f32_seq, f32_seq,
                        pltpu.VMEM((2 * nbr, BAND, 2 * BAND), F32)],
        compiler_params=_params(("parallel", "arbitrary")),
    )(qkv3, o, lse, dmix3, gain)


def _delay(x, k, row):
    return jnp.where(row >= k, pltpu.roll(x, k, 0), 0.0)


def _advance(x, k, row, S):
    return jnp.where(row < S - k, pltpu.roll(x, S - k, 0), 0.0)


def _conv3(x, w, row):
    return (w[0:1, :] * _delay(x, 2, row) + w[1:2, :] * _delay(x, 1, row)) + w[2:3, :] * x


def _conv3_grads(dz, x, w, row, S):
    dx = (w[2:3, :] * dz + w[1:2, :] * _advance(dz, 1, row, S)) + w[0:1, :] * _advance(dz, 2, row, S)
    dw = jnp.concatenate([jnp.sum(dz * _delay(x, 2, row), axis=0, keepdims=True),
                          jnp.sum(dz * _delay(x, 1, row), axis=0, keepdims=True),
                          jnp.sum(dz * x, axis=0, keepdims=True)], axis=0)
    return dx, dw


def _mix_conv_fwd(cv3, taps, gain, mix, S, name):
    _, T, C = cv3.shape
    B, P = T // S, C // LANES

    def body(cv_ref, w_ref, g_ref, mix_hbm, y_ref):
        del mix_hbm
        row = lax.broadcasted_iota(jnp.int32, (S, 1), 0)
        is_a = _lane_is_a()
        gb = cv_ref[0].astype(F32)
        c = cv_ref[1].astype(F32) * cv_ref[2].astype(F32)
        y = gb * _conv3(c, w_ref[...], row)
        rstd = lax.rsqrt(_seg_sum(y * y, is_a) * (1.0 / GROUP) + EPS)
        y_ref[...] = ((y * rstd) * g_ref[...]).astype(BF16)

    return pl.pallas_call(
        body, grid=(B, P), name=name,
        in_specs=[pl.BlockSpec((3, S, LANES), lambda b, p: (0, b, p)),
                  pl.BlockSpec((3, LANES), lambda b, p: (0, p)),
                  pl.BlockSpec((1, LANES), lambda b, p: (0, p)),
                  pl.BlockSpec(memory_space=pl.ANY)],
        out_specs=pl.BlockSpec((None, S, LANES), lambda b, p: (1, b, p)),
        out_shape=jax.ShapeDtypeStruct(mix.shape, mix.dtype),
        input_output_aliases={3: 0},
        compiler_params=_params(("parallel", "parallel")),
    )(cv3, taps, gain, mix)


def _mix_conv_bwd(cv3, dmix3, taps, gain, dproj, S, name):
    _, T, C = cv3.shape
    B, P = T // S, C // LANES

    def body(cv_ref, dn_ref, w_ref, g_ref, dproj_hbm, dcv_ref, dw_ref, dg_ref):
        del dproj_hbm
        b = pl.program_id(1)
        row = lax.broadcasted_iota(jnp.int32, (S, 1), 0)
        is_a = _lane_is_a()
        w = w_ref[...]
        gb = cv_ref[0].astype(F32)
        gc = cv_ref[1].astype(F32)
        u = cv_ref[2].astype(F32)
        c = gc * u
        z = _conv3(c, w, row)
        y = gb * z
        rstd = lax.rsqrt(_seg_sum(y * y, is_a) * (1.0 / GROUP) + EPS)
        yn = y * rstd
        dn = dn_ref[...]
        a = dn * g_ref[...]
        dy = rstd * (a - yn * (_seg_sum(a * yn, is_a) * (1.0 / GROUP)))
        dg = jnp.sum(dn * yn, axis=0, keepdims=True)
        dc, dw = _conv3_grads(dy * gb, c, w, row, S)
        dcv_ref[0] = (dy * z).astype(BF16)
        dcv_ref[1] = (dc * u).astype(BF16)
        dcv_ref[2] = (dc * gc).astype(BF16)

        @pl.when(b == 0)
        def _():
            dw_ref[...] = dw
            dg_ref[...] = dg

        @pl.when(b != 0)
        def _():
            dw_ref[...] += dw
            dg_ref[...] += dg

    return pl.pallas_call(
        body, grid=(P, B), name=name,
        in_specs=[pl.BlockSpec((3, S, LANES), lambda p, b: (0, b, p)),
                  pl.BlockSpec((None, S, LANES), lambda p, b: (1, b, p)),
                  pl.BlockSpec((3, LANES), lambda p, b: (0, p)),
                  pl.BlockSpec((1, LANES), lambda p, b: (0, p)),
                  pl.BlockSpec(memory_space=pl.ANY)],
        out_specs=[pl.BlockSpec((3, S, LANES), lambda p, b: (1, b, p)),
                   pl.BlockSpec((3, LANES), lambda p, b: (0, p)),
                   pl.BlockSpec((1, LANES), lambda p, b: (0, p))],
        out_shape=[jax.ShapeDtypeStruct(dproj.shape, dproj.dtype),
                   jax.ShapeDtypeStruct((3, C), F32), jax.ShapeDtypeStruct((1, C), F32)],
        input_output_aliases={4: 0},
        compiler_params=_params(("parallel", "arbitrary")),
    )(cv3, dmix3, taps, gain, dproj)


def _sigmoid(x):
    return 1.0 / (1.0 + jnp.exp(-x))


def _ffn_act_fwd(up3, taps, S, name):
    _, T, Fd = up3.shape
    B, P = T // S, Fd // LANES

    def body(up_ref, wg_ref, wv_ref, act_ref):
        row = lax.broadcasted_iota(jnp.int32, (S, 1), 0)
        cg = _conv3(up_ref[0].astype(F32), wg_ref[...], row)
        cv = _conv3(up_ref[1].astype(F32), wv_ref[...], row)
        act_ref[...] = ((cg * _sigmoid(cg)) * cv).astype(BF16)

    return pl.pallas_call(
        body, grid=(B, P), name=name,
        in_specs=[pl.BlockSpec((2, S, LANES), lambda b, p: (0, b, p)),
                  pl.BlockSpec((3, LANES), lambda b, p: (0, p)),
                  pl.BlockSpec((3, LANES), lambda b, p: (0, P + p))],
        out_specs=pl.BlockSpec((S, LANES), lambda b, p: (b, p)),
        out_shape=jax.ShapeDtypeStruct((T, Fd), BF16),
        compiler_params=_params(("parallel", "parallel")),
    )(up3, taps, taps)


def _ffn_act_bwd(up3, dact3, taps, S, name):
    _, T, Fd = up3.shape
    B, P = T // S, Fd // LANES

    def body(up_ref, da_ref, wg_ref, wv_ref, dup_ref, dwg_ref, dwv_ref):
        b = pl.program_id(1)
        row = lax.broadcasted_iota(jnp.int32, (S, 1), 0)
        ug = up_ref[0].astype(F32)
        uv = up_ref[1].astype(F32)
        wg = wg_ref[...]
        wv = wv_ref[...]
        cg = _conv3(ug, wg, row)
        cv = _conv3(uv, wv, row)
        sg = _sigmoid(cg)
        da = da_ref[...].astype(F32)
        dcg = (da * cv) * (sg * (1.0 + cg * (1.0 - sg)))
        dcv = da * (cg * sg)
        dug, dwg = _conv3_grads(dcg, ug, wg, row, S)
        duv, dwv = _conv3_grads(dcv, uv, wv, row, S)
        dup_ref[0] = dug.astype(BF16)
        dup_ref[1] = duv.astype(BF16)

        @pl.when(b == 0)
        def _():
            dwg_ref[...] = dwg
            dwv_ref[...] = dwv

        @pl.when(b != 0)
        def _():
            dwg_ref[...] += dwg
            dwv_ref[...] += dwv

    tap_out = pl.BlockSpec((3, LANES), lambda p, b: (0, p))
    return pl.pallas_call(
        body, grid=(P, B), name=name,
        in_specs=[pl.BlockSpec((2, S, LANES), lambda p, b: (0, b, p)),
                  pl.BlockSpec((None, S, LANES), lambda p, b: (0, b, p)),
                  pl.BlockSpec((3, LANES), lambda p, b: (0, p)),
                  pl.BlockSpec((3, LANES), lambda p, b: (0, P + p))],
        out_specs=[pl.BlockSpec((2, S, LANES), lambda p, b: (0, b, p)), tap_out, tap_out],
        out_shape=[jax.ShapeDtypeStruct((2, T, Fd), BF16),
                   jax.ShapeDtypeStruct((3, Fd), F32), jax.ShapeDtypeStruct((3, Fd), F32)],
        compiler_params=_params(("parallel", "arbitrary")),
    )(up3, dact3, taps, taps)


def _final_norm_loss(x, g, target, tm, name):
    T, D = x.shape

    def body(x_ref, g_ref, t_ref, dx_ref, dg_ref, loss_ref):
        xv = x_ref[...]
        rstd = lax.rsqrt(jnp.mean(xv * xv, axis=-1, keepdims=True) + EPS)
        xn = xv * rstd
        err = xn * g_ref[...] - t_ref[...]
        part = 0.5 * jnp.sum(jnp.mean(err * err, axis=-1, keepdims=True), axis=0, keepdims=True)
        dy = err * (1.0 / D)
        a = dy * g_ref[...]
        dx_ref[...] = rstd * (a - xn * jnp.mean(a * xn, axis=-1, keepdims=True))
        dg = jnp.sum(dy * xn, axis=0, keepdims=True)
        lpart = jnp.broadcast_to(part, (1, LANES))

        @pl.when(pl.program_id(0) == 0)
        def _():
            dg_ref[...] = dg
            loss_ref[...] = lpart

        @pl.when(pl.program_id(0) != 0)
        def _():
            dg_ref[...] += dg
            loss_ref[...] += lpart

    row = pl.BlockSpec((tm, D), lambda i: (i, 0))
    return pl.pallas_call(
        body, grid=(T // tm,), name=name,
        in_specs=[row, pl.BlockSpec((1, D), lambda i: (0, 0)), row],
        out_specs=[row, pl.BlockSpec((1, D), lambda i: (0, 0)), pl.BlockSpec((1, LANES), lambda i: (0, 0))],
        out_shape=[jax.ShapeDtypeStruct((T, D), F32), jax.ShapeDtypeStruct((1, D), F32),
                   jax.ShapeDtypeStruct((1, LANES), F32)],
        compiler_params=_params(("arbitrary",)),
    )(x, g, target)


def _row_tile(rows, cols, budget_elems=512 * 1024):
    tr = rows
    while tr * cols > budget_elems and tr % 16 == 0:
        tr //= 2
    return tr


def _sum_slots(stack, order, out_dtype, name):
    n, R, Cc = stack.shape
    tr = _row_tile(R, Cc)

    def body(s_ref, o_ref):
        acc = s_ref[order[0]].astype(F32)
        for k in order[1:]:
            acc = acc + s_ref[k].astype(F32)
        o_ref[...] = acc.astype(out_dtype)

    return pl.pallas_call(
        body, grid=(R // tr,), name=name,
        in_specs=[pl.BlockSpec((n, tr, Cc), lambda i: (0, i, 0))],
        out_specs=pl.BlockSpec((tr, Cc), lambda i: (i, 0)),
        out_shape=jax.ShapeDtypeStruct((R, Cc), out_dtype),
        compiler_params=_params(("parallel",)),
    )(stack)


def _adamw(w, g, m, v, name):
    R, Cc = w.shape
    tr = _row_tile(R, Cc, 256 * 1024)

    def body(w_ref, g_ref, m_ref, v_ref, d_ref, nm_ref, nv_ref):
        gv = g_ref[...]
        nm = ADAM_B1 * m_ref[...] + (1.0 - ADAM_B1) * gv
        nv = ADAM_B2 * v_ref[...] + (1.0 - ADAM_B2) * (gv * gv)
        m_hat = nm / (1.0 - ADAM_B1 ** ADAM_STEP)
        v_hat = nv / (1.0 - ADAM_B2 ** ADAM_STEP)
        d_ref[...] = -ADAM_LR * (m_hat / (jnp.sqrt(v_hat) + ADAM_EPS) + ADAM_WD * w_ref[...])
        nm_ref[...] = nm
        nv_ref[...] = nv

    blk = pl.BlockSpec((tr, Cc), lambda i: (i, 0))
    shp = jax.ShapeDtypeStruct((R, Cc), F32)
    return pl.pallas_call(
        body, grid=(R // tr,), name=name,
        in_specs=[blk] * 4, out_specs=[blk] * 3, out_shape=[shp] * 3,
        compiler_params=_params(("parallel",)),
    )(w, g, m, v)


def _position():
    x, y, c = lax.axis_index("x"), lax.axis_index("y"), lax.axis_index("c")
    chips = [(1 - x, y), (x, 1 - y), (1 - x, 1 - y)]
    return x, y, c, chips


def _shard_window(ref, col_sharded, shard, half, lead=()):
    K, N = ref.shape[-2:]
    if col_sharded:
        rows = slice(None) if half is None else pl.ds(pl.multiple_of(half * (K // 2), 16), K // 2)
        cols = slice(None) if shard is None else pl.ds(pl.multiple_of(shard * (N // 4), LANES), N // 4)
    else:
        rows = slice(None) if shard is None else pl.ds(pl.multiple_of(shard * (K // 4), 16), K // 4)
        cols = slice(None) if half is None else pl.ds(pl.multiple_of(half * (N // 2), LANES), N // 2)
    return ref.at[(*lead, rows, cols)]


COL_SHARDED = (True, False, True, False)


def _gather_weights(shards):
    nw = len(shards)
    full_shapes = []
    for sh, colwise in zip(shards, COL_SHARDED):
        Lh, K, N = sh.shape
        full_shapes.append(jax.ShapeDtypeStruct((Lh, K, N * 4) if colwise else (Lh, K * 4, N), sh.dtype))

    def body(*refs):
        own, full = refs[:nw], refs[nw:2 * nw]
        send_sems, recv_sems, local_sems = refs[2 * nw:]
        x, y, c, chips = _position()
        me = 2 * x + y
        all_l = (slice(None),)

        def first(i, j, chip, to_me):
            src_chip = (2 * chip[0] + chip[1]) if to_me else me
            return pltpu.make_async_remote_copy(
                src_ref=_shard_window(own[i], COL_SHARDED[i], None, c, all_l) if not to_me
                else _shard_window(full[i], COL_SHARDED[i], src_chip, c, all_l),
                dst_ref=_shard_window(full[i], COL_SHARDED[i], src_chip, c, all_l),
                send_sem=send_sems.at[i * 3 + j], recv_sem=recv_sems.at[i * 3 + j],
                device_id=(chip[0], chip[1], c), device_id_type=MESH_ID)

        def second(i, j, chip, to_me):
            half = (1 - c) if to_me else c
            win = _shard_window(full[i], COL_SHARDED[i], 2 * chip[0] + chip[1], half, all_l)
            k = (nw + i) * 3 + j
            return pltpu.make_async_remote_copy(
                src_ref=win, dst_ref=win, send_sem=send_sems.at[k], recv_sem=recv_sems.at[k],
                device_id=(x, y, 1 - c), device_id_type=MESH_ID)

        local = [pltpu.make_async_copy(own[i], _shard_window(full[i], COL_SHARDED[i], me, None, all_l),
                                       local_sems.at[i]) for i in range(nw)]
        for cp in local:
            cp.start()
        sends = [first(i, j, chip, False) for i in range(nw) for j, chip in enumerate(chips)]
        for cp in sends:
            cp.start()
        passes = []
        for i in range(nw):
            for j, chip in enumerate(chips):
                first(i, j, chip, True).wait_recv()
                cp = second(i, j, chip, False)
                cp.start()
                passes.append(cp)
        for i in range(nw):
            for j, chip in enumerate(chips):
                second(i, j, chip, True).wait_recv()
        for cp in sends + passes:
            cp.wait_send()
        for cp in local:
            cp.wait()

    any_spec = pl.BlockSpec(memory_space=pl.ANY)
    return pl.pallas_call(
        body, name="gather_weights",
        in_specs=[any_spec] * nw, out_specs=[any_spec] * nw, out_shape=full_shapes,
        scratch_shapes=[pltpu.SemaphoreType.DMA((2 * nw * 3,)), pltpu.SemaphoreType.DMA((2 * nw * 3,)),
                        pltpu.SemaphoreType.DMA((nw,))],
    )(*shards)


def _half_shape(shape, colwise):
    K, N = shape
    return (K // 2, N) if colwise else (K, N // 2)


def _piece_shape(shape, colwise):
    K, N = shape
    return (K // 2, N // 4) if colwise else (K // 4, N // 2)


def _exchange_halves(grads, colwise):
    n = len(grads)
    out_shapes = [jax.ShapeDtypeStruct((2, *_half_shape(g.shape, cw)), g.dtype) for g, cw in zip(grads, colwise)]

    def body(*refs):
        g_refs, out = refs[:n], refs[n:2 * n]
        send_sems, recv_sems, local_sems = refs[2 * n:]
        x, y, c, _ = _position()
        local = [pltpu.make_async_copy(_shard_window(g_refs[i], colwise[i], None, c), out[i].at[0], local_sems.at[i])
                 for i in range(n)]
        remote = [pltpu.make_async_remote_copy(
            src_ref=_shard_window(g_refs[i], colwise[i], None, 1 - c), dst_ref=out[i].at[1],
            send_sem=send_sems.at[i], recv_sem=recv_sems.at[i],
            device_id=(x, y, 1 - c), device_id_type=MESH_ID) for i in range(n)]
        for cp in local + remote:
            cp.start()
        for cp in remote:
            cp.wait()
        for cp in local:
            cp.wait()

    any_spec = pl.BlockSpec(memory_space=pl.ANY)
    return pl.pallas_call(
        body, name="grad_exchange_halves",
        in_specs=[any_spec] * n, out_specs=[any_spec] * n, out_shape=out_shapes,
        scratch_shapes=[pltpu.SemaphoreType.DMA((n,)), pltpu.SemaphoreType.DMA((n,)), pltpu.SemaphoreType.DMA((n,))],
    )(*grads)


def _scatter_to_owners(chip_sums, colwise):
    n = len(chip_sums)
    shapes = [_piece_shape((g.shape[0] * 2, g.shape[1]) if cw else (g.shape[0], g.shape[1] * 2), cw)
              for g, cw in zip(chip_sums, colwise)]
    out_shapes = [jax.ShapeDtypeStruct((4, *s), g.dtype) for s, g in zip(shapes, chip_sums)]

    def piece(ref, cw, shard):
        K, N = ref.shape
        if cw:
            return ref.at[:, pl.ds(pl.multiple_of(shard * (N // 4), LANES), N // 4)]
        return ref.at[pl.ds(pl.multiple_of(shard * (K // 4), 16), K // 4), :]

    def body(*refs):
        g_refs, out = refs[:n], refs[n:2 * n]
        send_sems, recv_sems, local_sems = refs[2 * n:]
        x, y, c, chips = _position()
        me = 2 * x + y
        local = [pltpu.make_async_copy(piece(g_refs[i], colwise[i], me), out[i].at[3], local_sems.at[i])
                 for i in range(n)]
        remote = [pltpu.make_async_remote_copy(
            src_ref=piece(g_refs[i], colwise[i], 2 * chip[0] + chip[1]), dst_ref=out[i].at[j],
            send_sem=send_sems.at[i * 3 + j], recv_sem=recv_sems.at[i * 3 + j],
            device_id=(chip[0], chip[1], c), device_id_type=MESH_ID)
            for i in range(n) for j, chip in enumerate(chips)]
        for cp in local + remote:
            cp.start()
        for cp in remote:
            cp.wait()
        for cp in local:
            cp.wait()

    any_spec = pl.BlockSpec(memory_space=pl.ANY)
    return pl.pallas_call(
        body, name="grad_scatter_to_owners",
        in_specs=[any_spec] * n, out_specs=[any_spec] * n, out_shape=out_shapes,
        scratch_shapes=[pltpu.SemaphoreType.DMA((3 * n,)), pltpu.SemaphoreType.DMA((3 * n,)),
                        pltpu.SemaphoreType.DMA((n,))],
    )(*chip_sums)


def _share_with_sibling(halves, colwise, n_layers):
    nw = len(halves) // n_layers
    out_shapes = []
    for wi in range(nw):
        K, N = halves[wi * n_layers].shape
        out_shapes.append(jax.ShapeDtypeStruct((n_layers, K * 2, N) if colwise[wi] else (n_layers, K, N * 2), F32))
    n = len(halves)

    def window(ref, cw, layer, half):
        _, K, N = ref.shape
        if cw:
            return ref.at[layer, pl.ds(pl.multiple_of(half * (K // 2), 8), K // 2), :]
        return ref.at[layer, :, pl.ds(pl.multiple_of(half * (N // 2), LANES), N // 2)]

    def body(*refs):
        h_refs, out = refs[:n], refs[n:n + nw]
        send_sems, recv_sems, local_sems = refs[n + nw:]
        x, y, c, _ = _position()
        local, remote = [], []
        for wi in range(nw):
            for layer in range(n_layers):
                k = wi * n_layers + layer
                win = window(out[wi], colwise[wi], layer, c)
                local.append(pltpu.make_async_copy(h_refs[k], win, local_sems.at[k]))
                remote.append(pltpu.make_async_remote_copy(
                    src_ref=h_refs[k], dst_ref=win, send_sem=send_sems.at[k], recv_sem=recv_sems.at[k],
                    device_id=(x, y, 1 - c), device_id_type=MESH_ID))
        for cp in local + remote:
            cp.start()
        for wi in range(nw):
            for layer in range(n_layers):
                k = wi * n_layers + layer
                pltpu.make_async_remote_copy(
                    src_ref=h_refs[k], dst_ref=window(out[wi], colwise[wi], layer, 1 - c),
                    send_sem=send_sems.at[k], recv_sem=recv_sems.at[k],
                    device_id=(x, y, 1 - c), device_id_type=MESH_ID).wait_recv()
        for cp in remote:
            cp.wait_send()
        for cp in local:
            cp.wait()

    any_spec = pl.BlockSpec(memory_space=pl.ANY)
    return pl.pallas_call(
        body, name="grad_share_with_sibling",
        in_specs=[any_spec] * n, out_specs=[any_spec] * nw, out_shape=out_shapes,
        scratch_shapes=[pltpu.SemaphoreType.DMA((n,)), pltpu.SemaphoreType.DMA((n,)), pltpu.SemaphoreType.DMA((n,))],
    )(*halves)


def _all_reduce_small(pack, name):
    R, Cc = pack.shape
    n_dev = 8

    def body(p_ref, o_ref, buf, send_sems, recv_sems):
        x, y, c, _ = _position()
        me = 4 * x + 2 * y + c
        buf[me] = p_ref[...]

        def peer(k):
            px = 1 - x if k & 4 else x
            py = 1 - y if k & 2 else y
            pc = 1 - c if k & 1 else c
            return px, py, pc

        def copy(k, incoming):
            px, py, pc = peer(k)
            slot = (4 * px + 2 * py + pc) if incoming else me
            return pltpu.make_async_remote_copy(
                src_ref=p_ref, dst_ref=buf.at[slot], send_sem=send_sems.at[k], recv_sem=recv_sems.at[k],
                device_id=(px, py, pc), device_id_type=MESH_ID)

        for k in range(1, n_dev):
            copy(k, False).start()
        for k in range(1, n_dev):
            copy(k, True).wait_recv()
        for k in range(1, n_dev):
            copy(k, False).wait_send()
        acc = buf[0]
        for j in range(1, n_dev):
            acc = acc + buf[j]
        o_ref[...] = acc

    vmem = pl.BlockSpec(memory_space=pltpu.VMEM)
    return pl.pallas_call(
        body, name=name,
        in_specs=[vmem], out_specs=vmem, out_shape=jax.ShapeDtypeStruct((R, Cc), F32),
        scratch_shapes=[pltpu.VMEM((n_dev, R, Cc), F32), pltpu.SemaphoreType.DMA((n_dev,)),
                        pltpu.SemaphoreType.DMA((n_dev,))],
    )(pack)


def _local_forward_backward(x2, target2, S, layers, final_g, tm=512):
    T, D = x2.shape
    C = D // 2
    n_heads = C // GROUP
    saved = []
    xc = x2
    for li, lw in enumerate(layers):
        Fd = lw["ffn_down"].shape[0]
        h1, qkv3, cv3 = _norm_proj(xc, lw["norm1"], lw["w_in"], ((3, C, F32), (3, C, BF16)), tm, min(C, 512),
                                   f"l{li}_norm_in_proj")
        o, lse, mix = _attn_fwd(qkv3, lw["attn_g"], 2, S, n_heads, f"l{li}_attn_fwd")
        mix = _mix_conv_fwd(cv3, lw["taps"], lw["conv_g"], mix, S, f"l{li}_mix_conv_fwd")
        x_mid = _proj_residual(mix, lw["w_out"], xc, tm, f"l{li}_out_proj")
        h2, up3 = _norm_proj(x_mid, lw["norm2"], lw["ffn_up"], ((2, Fd, BF16),), tm // 2, 256, f"l{li}_norm_ffn_up")
        act = _ffn_act_fwd(up3, lw["ffn_taps"], S, f"l{li}_ffn_act_fwd")
        x_out = _proj_residual(act.reshape(1, T, Fd), lw["ffn_down"], x_mid, tm, f"l{li}_ffn_down")
        saved.append(dict(x_in=xc, h1=h1, qkv3=qkv3, cv3=cv3, o=o, lse=lse, mix=mix, x_mid=x_mid, h2=h2, up3=up3,
                          act=act))
        xc = x_out

    dx, d_final_g, loss_part = _final_norm_loss(xc, final_g, target2, tm, "final_norm_loss")

    big, small = [None] * len(layers), [None] * len(layers)
    for li in reversed(range(len(layers))):
        lw, sv = layers[li], saved[li]
        Fd = lw["ffn_down"].shape[0]
        dxb, dact3 = _grad_through_weight(dx, lw["ffn_down"], 1, Fd, BF16, tm, 256, f"l{li}_d_act")
        d_ffn_down = _weight_grad(sv["act"].reshape(1, T, Fd), dxb.reshape(1, T, D), Fd // 2, D, 1024,
                                  f"l{li}_d_ffn_down")
        dup3, d_taps_g, d_taps_v = _ffn_act_bwd(sv["up3"], dact3, lw["ffn_taps"], S, f"l{li}_ffn_act_bwd")
        d_ffn_up = _weight_grad(sv["h2"].reshape(1, T, D), dup3, D, Fd // 2, 1024, f"l{li}_d_ffn_up")
        dx_mid, d_norm2 = _grad_through_proj_norm(dup3, lw["ffn_up"], sv["x_mid"], lw["norm2"], dx, tm // 2,
                                                  f"l{li}_d_norm2")
        dxmb, dmix3 = _grad_through_weight(dx_mid, lw["w_out"], 2, C, F32, tm, min(C, 512), f"l{li}_d_mix")
        d_w_out = _weight_grad(sv["mix"], dxmb.reshape(1, T, D), C, D, 1024, f"l{li}_d_w_out")
        dproj, d_attn_g = _attn_bwd(sv["qkv3"], sv["o"], sv["lse"], dmix3, lw["attn_g"], 6, S, n_heads,
                                    f"l{li}_attn_bwd")
        dproj, d_taps, d_conv_g = _mix_conv_bwd(sv["cv3"], dmix3, lw["taps"], lw["conv_g"], dproj, S,
                                                f"l{li}_mix_conv_bwd")
        d_w_in = _weight_grad(sv["h1"].reshape(1, T, D), dproj, D, C, 1024, f"l{li}_d_w_in")
        dx, d_norm1 = _grad_through_proj_norm(dproj, lw["w_in"], sv["x_in"], lw["norm1"], dx_mid, tm,
                                              f"l{li}_d_norm1")
        big[li] = (d_w_in, d_w_out, d_ffn_up, d_ffn_down)
        small[li] = dict(norm1=d_norm1, taps=d_taps, attn_g=d_attn_g, conv_g=d_conv_g, norm2=d_norm2,
                         ffn_taps=jnp.concatenate([d_taps_g, d_taps_v], axis=1))
    return loss_part, dx, big, small, d_final_g


SMALL_ORDER = ("norm1", "attn_g", "conv_g", "norm2", "taps", "ffn_taps")


def _pack_small(small, d_final_g):
    parts = [small[li][k].reshape(-1) for li in range(len(small)) for k in SMALL_ORDER] + [d_final_g.reshape(-1)]
    return jnp.concatenate(parts).reshape(-1, LANES)


def _unpack_small(pack, small, d_final_g):
    flat = pack.reshape(-1)
    out, pos = [dict() for _ in small], 0
    for li in range(len(small)):
        for k in SMALL_ORDER:
            n = small[li][k].size
            out[li][k] = flat[pos:pos + n].reshape(small[li][k].shape)
            pos += n
    return out, flat[pos:pos + d_final_g.size]


def kernel(x, norm1_g, w_in, mix_conv_w, attn_out_g, conv_out_g, w_out, norm2_g, ffn_up, ffn_conv_w, ffn_down, final_norm_g, loss_target, m_norm1_g, m_w_in, m_mix_conv_w, m_attn_out_g, m_conv_out_g, m_w_out, m_norm2_g, m_ffn_up, m_ffn_conv_w, m_ffn_down, m_final_norm_g, v_norm1_g, v_w_in, v_mix_conv_w, v_attn_out_g, v_conv_out_g, v_w_out, v_norm2_g, v_ffn_up, v_ffn_conv_w, v_ffn_down, v_final_norm_g):
    Bl, S, D = x.shape
    L = w_in.shape[0]
    T = Bl * S
    shard = 2 * lax.axis_index("x") + lax.axis_index("y")

    big_w = (w_in, w_out, ffn_up, ffn_down)
    full = _gather_weights([w.astype(BF16) for w in big_w])

    taps_w, ftaps_w = mix_conv_w.shape[2], ffn_conv_w.shape[2]
    taps_full = jnp.zeros((L, 3, 4 * taps_w), F32)
    taps_full = lax.dynamic_update_slice(taps_full, mix_conv_w, (0, 0, shard * taps_w))
    ftaps_full = jnp.zeros((L, 3, 4 * ftaps_w), F32)
    ftaps_full = lax.dynamic_update_slice(ftaps_full, ffn_conv_w, (0, 0, shard * ftaps_w))
    tap_pack = jnp.concatenate([taps_full.reshape(-1), ftaps_full.reshape(-1)]).reshape(-1, LANES)
    tap_pack = _all_reduce_small(tap_pack * 0.5, "all_gather_taps")
    n_taps = taps_full.size
    taps_full = tap_pack.reshape(-1)[:n_taps].reshape(taps_full.shape)
    ftaps_full = tap_pack.reshape(-1)[n_taps:].reshape(ftaps_full.shape)

    layers = [dict(w_in=full[0][l], w_out=full[1][l], ffn_up=full[2][l], ffn_down=full[3][l],
                   norm1=norm1_g[l:l + 1], taps=taps_full[l], attn_g=attn_out_g[l:l + 1],
                   conv_g=conv_out_g[l:l + 1], norm2=norm2_g[l:l + 1], ffn_taps=ftaps_full[l]) for l in range(L)]

    loss_part, dx, big, small, d_final_g = _local_forward_backward(
        x.reshape(T, D), loss_target.reshape(T, D), S, layers, final_norm_g.reshape(1, D))
    loss = lax.psum(loss_part[0, 0], ("x", "y", "c"))

    grads = [big[l][wi] for wi in range(4) for l in range(L)]
    colwise = [COL_SHARDED[wi] for wi in range(4) for _ in range(L)]
    stacks = _exchange_halves(grads, colwise)
    chip_sums = [_sum_slots(s, (0, 1), BF16, f"chip_sum_{k}") for k, s in enumerate(stacks)]
    stacks = _scatter_to_owners(chip_sums, colwise)
    halves = [_sum_slots(s, (3, 0, 1, 2), F32, f"owner_sum_{k}") for k, s in enumerate(stacks)]
    g_big = _share_with_sibling(halves, COL_SHARDED, L)

    pack = _all_reduce_small(_pack_small(small, d_final_g), "all_reduce_small_grads")
    g_small, g_final = _unpack_small(pack, small, d_final_g)

    def stacked(key):
        return jnp.stack([g_small[l][key].reshape(g_small[l][key].shape[-2:] if key.endswith("taps") else (-1,))
                          for l in range(L)])

    g_norm1, g_attn, g_conv, g_norm2 = stacked("norm1"), stacked("attn_g"), stacked("conv_g"), stacked("norm2")
    g_taps = lax.dynamic_slice(stacked("taps"), (0, 0, shard * taps_w), (L, 3, taps_w))
    g_ftaps = lax.dynamic_slice(stacked("ffn_taps"), (0, 0, shard * ftaps_w), (L, 3, ftaps_w))

    grads_out = dict(norm1_g=g_norm1, w_in=g_big[0], mix_conv_w=g_taps, attn_out_g=g_attn, conv_out_g=g_conv,
                     w_out=g_big[1], norm2_g=g_norm2, ffn_up=g_big[2], ffn_conv_w=g_ftaps, ffn_down=g_big[3],
                     final_norm_g=g_final)
    weights = dict(norm1_g=norm1_g, w_in=w_in, mix_conv_w=mix_conv_w, attn_out_g=attn_out_g, conv_out_g=conv_out_g,
                   w_out=w_out, norm2_g=norm2_g, ffn_up=ffn_up, ffn_conv_w=ffn_conv_w, ffn_down=ffn_down,
                   final_norm_g=final_norm_g)
    ms = dict(norm1_g=m_norm1_g, w_in=m_w_in, mix_conv_w=m_mix_conv_w, attn_out_g=m_attn_out_g,
              conv_out_g=m_conv_out_g, w_out=m_w_out, norm2_g=m_norm2_g, ffn_up=m_ffn_up, ffn_conv_w=m_ffn_conv_w,
              ffn_down=m_ffn_down, final_norm_g=m_final_norm_g)
    vs = dict(norm1_g=v_norm1_g, w_in=v_w_in, mix_conv_w=v_mix_conv_w, attn_out_g=v_attn_out_g,
              conv_out_g=v_conv_out_g, w_out=v_w_out, norm2_g=v_norm2_g, ffn_up=v_ffn_up, ffn_conv_w=v_ffn_conv_w,
              ffn_down=v_ffn_down, final_norm_g=v_final_norm_g)
    names = list(weights)
    big_names = ("w_in", "w_out", "ffn_up", "ffn_down")
    small_names = [n for n in names if n not in big_names]
    delta, new_m, new_v = {}, {}, {}
    for n in big_names:
        shp = weights[n].shape
        two_d = (shp[0] * shp[1], shp[2])
        d_, m_, v_ = _adamw(weights[n].reshape(two_d), grads_out[n].reshape(two_d), ms[n].reshape(two_d),
                            vs[n].reshape(two_d), f"adamw_{n}")
        delta[n], new_m[n], new_v[n] = d_.reshape(shp), m_.reshape(shp), v_.reshape(shp)

    def packed(tree):
        return jnp.concatenate([tree[n].reshape(-1) for n in small_names]).reshape(-1, LANES)

    d_, m_, v_ = _adamw(packed(weights), packed(grads_out), packed(ms), packed(vs), "adamw_small")
    pos = 0
    for n in small_names:
        size, shp = weights[n].size, weights[n].shape
        delta[n] = d_.reshape(-1)[pos:pos + size].reshape(shp)
        new_m[n] = m_.reshape(-1)[pos:pos + size].reshape(shp)
        new_v[n] = v_.reshape(-1)[pos:pos + size].reshape(shp)
        pos += size

    return (loss, dx.reshape(Bl, S, D), *[grads_out[n] for n in names], *[delta[n] for n in names],
            *[new_m[n] for n in names], *[new_v[n] for n in names])
```

```python
import functools
import math

import jax
import jax.numpy as jnp
from jax import lax
from jax.experimental import pallas as pl
from jax.experimental.pallas import tpu as pltpu

F32 = jnp.float32
BF16 = jnp.bfloat16
EPS = 1e-6
GROUP = 64
LANES = 128
BAND = 128
DILATIONS = (1, 4, 16)
NEG = -1e30
MIB = 1024 * 1024
MESH_ID = pl.DeviceIdType.MESH

ADAM_LR = 0.001
ADAM_B1 = 0.9
ADAM_B2 = 0.999
ADAM_EPS = 1e-08
ADAM_WD = 0.01
ADAM_STEP = 10


def _params(sem=None, vmem_mb=48):
    return pltpu.CompilerParams(dimension_semantics=sem, vmem_limit_bytes=vmem_mb * MIB)


def _nt(a, b):
    return lax.dot_general(a, b, (((1,), (1,)), ((), ())), preferred_element_type=F32)


def _tn(a, b):
    return lax.dot_general(a, b, (((0,), (0,)), ((), ())), preferred_element_type=F32)


def _seg_sum(x, is_a):
    s_a = jnp.sum(jnp.where(is_a, x, 0.0), axis=-1, keepdims=True)
    s_b = jnp.sum(jnp.where(is_a, 0.0, x), axis=-1, keepdims=True)
    return jnp.where(is_a, s_a, s_b)


def _lane_is_a():
    return lax.broadcasted_iota(jnp.int32, (1, LANES), 1) < GROUP


def _norm_proj(x, g, w3, layer, groups, tm, chunk, name):
    T, D = x.shape
    N = w3.shape[2]
    assert sum(p * c for p, c, _ in groups) == N and T % tm == 0

    def body(x_ref, g_ref, w_ref, h_ref, *out_refs):
        xv = x_ref[...]
        rstd = lax.rsqrt(jnp.mean(xv * xv, axis=-1, keepdims=True) + EPS)
        h = ((xv * rstd) * g_ref[...]).astype(BF16)
        h_ref[...] = h
        col = 0
        for (pieces, width, dtype), o_ref in zip(groups, out_refs):
            for p in range(pieces):
                for c0 in range(0, width, chunk):
                    acc = jnp.dot(h, w_ref[:, col + c0:col + c0 + chunk], preferred_element_type=F32)
                    o_ref[p, :, c0:c0 + chunk] = acc.astype(dtype)
                col += width

    out_shape = [jax.ShapeDtypeStruct((T, D), BF16)]
    out_specs = [pl.BlockSpec((tm, D), lambda i: (i, 0))]
    for pieces, width, dtype in groups:
        assert width % chunk == 0
        out_shape.append(jax.ShapeDtypeStruct((pieces, T, width), dtype))
        out_specs.append(pl.BlockSpec((pieces, tm, width), lambda i: (0, i, 0)))
    return pl.pallas_call(
        body, grid=(T // tm,), name=name,
        in_specs=[pl.BlockSpec((tm, D), lambda i: (i, 0)),
                  pl.BlockSpec((1, D), lambda i: (0, 0)),
                  pl.BlockSpec((None, D, N), lambda i: (layer, 0, 0))],
        out_specs=out_specs, out_shape=out_shape,
        compiler_params=_params(("parallel",), 56),
    )(x, g, w3)


def _proj_residual(pieces3, w3, layer, x, tm, name):
    P, T, C = pieces3.shape
    D = w3.shape[2]

    def body(a_ref, w_ref, x_ref, o_ref):
        acc = x_ref[...]
        for p in range(P):
            acc = acc + jnp.dot(a_ref[p], w_ref[p * C:(p + 1) * C, :], preferred_element_type=F32)
        o_ref[...] = acc

    return pl.pallas_call(
        body, grid=(T // tm,), name=name,
        in_specs=[pl.BlockSpec((P, tm, C), lambda i: (0, i, 0)),
                  pl.BlockSpec((None, P * C, D), lambda i: (layer, 0, 0)),
                  pl.BlockSpec((tm, D), lambda i: (i, 0))],
        out_specs=pl.BlockSpec((tm, D), lambda i: (i, 0)),
        out_shape=jax.ShapeDtypeStruct((T, D), F32),
        compiler_params=_params(("parallel",)),
    )(pieces3, w3, x)


def _grad_through_weight(dy, w3, layer, pieces, width, out_dtype, tm, chunk, name):
    T, D = dy.shape

    def body(dy_ref, w_ref, dyb_ref, o_ref):
        dyb = dy_ref[...].astype(BF16)
        dyb_ref[...] = dyb
        for p in range(pieces):
            for c0 in range(0, width, chunk):
                r0 = p * width + c0
                o_ref[p, :, c0:c0 + chunk] = _nt(dyb, w_ref[r0:r0 + chunk, :]).astype(out_dtype)

    return pl.pallas_call(
        body, grid=(T // tm,), name=name,
        in_specs=[pl.BlockSpec((tm, D), lambda i: (i, 0)),
                  pl.BlockSpec((None, pieces * width, D), lambda i: (layer, 0, 0))],
        out_specs=[pl.BlockSpec((tm, D), lambda i: (i, 0)),
                   pl.BlockSpec((pieces, tm, width), lambda i: (0, i, 0))],
        out_shape=[jax.ShapeDtypeStruct((T, D), BF16),
                   jax.ShapeDtypeStruct((pieces, T, width), out_dtype)],
        compiler_params=_params(("parallel",)),
    )(dy, w3)


def _grad_through_proj_norm(dp3, w3, layer, x, g, dx_in, tm, name):
    P, T, C = dp3.shape
    D = w3.shape[1]

    def body(dp_ref, w_ref, x_ref, g_ref, dxin_ref, dx_ref, dg_ref):
        dh = _nt(dp_ref[0], w_ref[:, 0:C])
        for p in range(1, P):
            dh = dh + _nt(dp_ref[p], w_ref[:, p * C:(p + 1) * C])
        xv = x_ref[...]
        rstd = lax.rsqrt(jnp.mean(xv * xv, axis=-1, keepdims=True) + EPS)
        xn = xv * rstd
        a = dh * g_ref[...]
        dx_ref[...] = dxin_ref[...] + rstd * (a - xn * jnp.mean(a * xn, axis=-1, keepdims=True))
        part = jnp.sum(dh * xn, axis=0, keepdims=True)

        @pl.when(pl.program_id(0) == 0)
        def _():
            dg_ref[...] = part

        @pl.when(pl.program_id(0) != 0)
        def _():
            dg_ref[...] += part

    return pl.pallas_call(
        body, grid=(T // tm,), name=name,
        in_specs=[pl.BlockSpec((P, tm, C), lambda i: (0, i, 0)),
                  pl.BlockSpec((None, D, P * C), lambda i: (layer, 0, 0)),
                  pl.BlockSpec((tm, D), lambda i: (i, 0)),
                  pl.BlockSpec((1, D), lambda i: (0, 0)),
                  pl.BlockSpec((tm, D), lambda i: (i, 0))],
        out_specs=[pl.BlockSpec((tm, D), lambda i: (i, 0)),
                   pl.BlockSpec((1, D), lambda i: (0, 0))],
        out_shape=[jax.ShapeDtypeStruct((T, D), F32), jax.ShapeDtypeStruct((1, D), F32)],
        compiler_params=_params(("arbitrary",), 56),
    )(dp3, w3, x, g, dx_in)


def _weight_grad(a3, g3, ta, tg, tt, layer, n_layers, prev, name):
    PA, T, CA = a3.shape
    PG, _, CG = g3.shape
    na, ng, nt = CA // ta, CG // tg, T // tt
    assert CA % ta == 0 and CG % tg == 0 and T % tt == 0

    def body(a_ref, g_ref, *rest):
        o_ref, acc_ref = rest[-2:]
        t = pl.program_id(2)
        part = _tn(a_ref[...], g_ref[...])

        @pl.when(t == 0)
        def _():
            acc_ref[...] = part

        @pl.when(t != 0)
        def _():
            acc_ref[...] += part

        @pl.when(t == nt - 1)
        def _():
            o_ref[...] = acc_ref[...].astype(o_ref.dtype)

    in_specs = [pl.BlockSpec((None, tt, ta), lambda i, j, t: (i // na, t, i % na)),
                pl.BlockSpec((None, tt, tg), lambda i, j, t: (j // ng, t, j % ng))]
    operands = [a3, g3]
    if prev is not None:
        in_specs.append(pl.BlockSpec(memory_space=pl.ANY))
        operands.append(prev)
    return pl.pallas_call(
        body, grid=(PA * na, PG * ng, nt), name=name,
        in_specs=in_specs,
        out_specs=pl.BlockSpec((None, ta, tg), lambda i, j, t: (layer, i, j)),
        out_shape=jax.ShapeDtypeStruct((n_layers, PA * CA, PG * CG), BF16),
        scratch_shapes=[pltpu.VMEM((ta, tg), F32)],
        input_output_aliases={} if prev is None else {2: 0},
        compiler_params=_params(("parallel", "parallel", "arbitrary"), 56),
    )(*operands)


def _bias_tables(bm_ref, pair, n_heads):
    ii = lax.broadcasted_iota(jnp.int32, (BAND, 2 * BAND), 0)
    jj = lax.broadcasted_iota(jnp.int32, (BAND, 2 * BAND), 1)
    dist = BAND + ii - jj
    valid = (dist >= 0) & (dist <= BAND)
    distf = dist.astype(F32)
    for hh in range(2):
        head = (2 * pair + hh + 1).astype(F32)
        slope = jnp.exp(jnp.full((1, 1), -8.0 / n_heads * math.log(2.0), F32) * head)
        for bi, d in enumerate(DILATIONS):
            bm_ref[hh * len(DILATIONS) + bi] = jnp.where(valid, -(slope * d) * distf, NEG)


def _gather_residues(dst_ref, src, d, S, convert):
    L = S // d
    for r in range(d):
        rows = pl.ds(r, L, stride=d) if d > 1 else slice(None)
        dst_ref[r * L:(r + 1) * L, :] = convert(src(rows))


def _block_rows(t, d, S):
    nb = S // (BAND * d)
    n = t % nb
    has_prev = jnp.minimum(n, 1)
    cur = pl.ds(pl.multiple_of(t * BAND, BAND), BAND)
    prev = pl.ds(pl.multiple_of((t - has_prev) * BAND, BAND), BAND)
    return cur, prev, has_prev


def _first_block_penalty(has_prev):
    jrow = lax.broadcasted_iota(jnp.int32, (1, 2 * BAND), 1)
    pen = jnp.where(has_prev == 0, NEG, 0.0).astype(F32)
    return jnp.where(jrow < BAND, pen, 0.0)


def _attn_fwd(qkv3, gain, mix_shape_pieces, S, n_heads, name):
    _, T, C = qkv3.shape
    B, P = T // S, C // LANES
    NBLK = S // BAND
    scale = GROUP ** -0.5
    nbr = len(DILATIONS)
    RC = 256

    def body(qkv_ref, g_ref, o_ref, lse_ref, an_ref, qs, ks, vs, op, mp, lp, ob, mb, lb, bm):
        pair = pl.program_id(1)
        is_a = _lane_is_a()
        _bias_tables(bm, pair, n_heads)

        for bi, d in enumerate(DILATIONS):
            nb = S // (BAND * d)
            _gather_residues(qs, lambda rows: qkv_ref.at[0][rows, :], d, S, lambda v: (v * scale).astype(BF16))
            _gather_residues(ks, lambda rows: qkv_ref.at[1][rows, :], d, S, lambda v: v.astype(BF16))
            _gather_residues(vs, lambda rows: qkv_ref.at[2][rows, :], d, S, lambda v: v.astype(BF16))
            o_dst, m_dst, l_dst = (ob.at[bi], mb.at[bi], lb.at[bi]) if d == 1 else (op, mp, lp)

            def block(t, carry, bi=bi, d=d, nb=nb, o_dst=o_dst, m_dst=m_dst, l_dst=l_dst):
                cur, prev, has_prev = _block_rows(t, d, S)
                q = qs[cur, :]
                if nb > 1:
                    kc = jnp.concatenate([ks[prev, :], ks[cur, :]], axis=0)
                    vc = jnp.concatenate([vs[prev, :], vs[cur, :]], axis=0)
                    pen = _first_block_penalty(has_prev)
                else:
                    kc, vc = ks[cur, :], vs[cur, :]
                outs = []
                for hh in range(2):
                    qm = jnp.where(is_a == (hh == 0), q, jnp.zeros_like(q))
                    s = _nt(qm, kc)
                    if nb > 1:
                        s = s + bm[hh * nbr + bi] + pen
                    else:
                        s = s + bm[hh * nbr + bi, :, BAND:2 * BAND]
                    m = jnp.max(s, axis=-1, keepdims=True)
                    e = jnp.exp(s - m)
                    l = jnp.sum(e, axis=-1, keepdims=True)
                    pv = jnp.dot(e.astype(BF16), vc, preferred_element_type=F32)
                    outs.append((pv, m, l))
                o_dst[cur, :] = jnp.where(is_a, outs[0][0], outs[1][0])
                m_dst[cur, :] = jnp.where(is_a, outs[0][1], outs[1][1])
                l_dst[cur, :] = jnp.where(is_a, outs[0][2], outs[1][2])
                return carry

            lax.fori_loop(0, NBLK, block, 0, unroll=4)
            if d > 1:
                L = S // d
                for r in range(d):
                    rows = pl.ds(r, L, stride=d)
                    ob.at[bi][rows, :] = op[r * L:(r + 1) * L, :]
                    mb.at[bi][rows, :] = mp[r * L:(r + 1) * L, :]
                    lb.at[bi][rows, :] = lp[r * L:(r + 1) * L, :]

        def finish(ci, carry):
            rs = pl.ds(pl.multiple_of(ci * RC, RC), RC)
            ms = [mb[bi, rs, :] for bi in range(nbr)]
            mmax = functools.reduce(jnp.maximum, ms)
            ws = [jnp.exp(m - mmax) for m in ms]
            num = sum(ob[bi, rs, :] * ws[bi] for bi in range(nbr))
            den = sum(lb[bi, rs, :] * ws[bi] for bi in range(nbr))
            o = num / den
            o_ref[rs, :] = o
            lse_ref[rs, :] = mmax + jnp.log(den)
            rstd = lax.rsqrt(_seg_sum(o * o, is_a) * (1.0 / GROUP) + EPS)
            an_ref[rs, :] = ((o * rstd) * g_ref[...]).astype(BF16)
            return carry

        lax.fori_loop(0, S // RC, finish, 0)

    seq = pl.BlockSpec((S, LANES), lambda b, p: (b, p))
    return pl.pallas_call(
        body, grid=(B, P), name=name,
        in_specs=[pl.BlockSpec((3, S, LANES), lambda b, p: (0, b, p)),
                  pl.BlockSpec((1, LANES), lambda b, p: (0, p))],
        out_specs=[seq, seq, pl.BlockSpec((None, S, LANES), lambda b, p: (0, b, p))],
        out_shape=[jax.ShapeDtypeStruct((T, C), F32), jax.ShapeDtypeStruct((T, C), F32),
                   jax.ShapeDtypeStruct((mix_shape_pieces, T, C), BF16)],
        scratch_shapes=[pltpu.VMEM((S, LANES), BF16)] * 3 + [pltpu.VMEM((S, LANES), F32)] * 3
        + [pltpu.VMEM((nbr, S, LANES), F32)] * 3 + [pltpu.VMEM((2 * nbr, BAND, 2 * BAND), F32)],
        compiler_params=_params(("parallel", "parallel")),
    )(qkv3, gain)


def _attn_bwd(qkv3, o, lse, dmix3, gain, dproj_pieces, S, n_heads, name):
    _, T, C = qkv3.shape
    B, P = T // S, C // LANES
    NBLK = S // BAND
    scale = GROUP ** -0.5
    nbr = len(DILATIONS)
    RC = 256

    def body(qkv_ref, o_ref, lse_ref, dn_ref, g_ref, dqkv_ref, dg_ref,
             do_n, dd_n, qs, ks, vs, dos, lses, dds, dqp, dkp, dvp, dqn, dkn, dvn, bm):
        pair = pl.program_id(0)
        b = pl.program_id(1)
        is_a = _lane_is_a()
        _bias_tables(bm, pair, n_heads)

        def prologue(ci, dg_acc):
            rs = pl.ds(pl.multiple_of(ci * RC, RC), RC)
            ov = o_ref[rs, :]
            dn = dn_ref[rs, :]
            rstd = lax.rsqrt(_seg_sum(ov * ov, is_a) * (1.0 / GROUP) + EPS)
            on = ov * rstd
            a = dn * g_ref[...]
            do = rstd * (a - on * (_seg_sum(a * on, is_a) * (1.0 / GROUP)))
            do_n[rs, :] = do
            dd_n[rs, :] = _seg_sum(do * ov, is_a)
            zero = jnp.zeros((RC, LANES), F32)
            dqn[rs, :] = zero
            dkn[rs, :] = zero
            dvn[rs, :] = zero
            return dg_acc + jnp.sum(dn * on, axis=0, keepdims=True)

        dg_part = lax.fori_loop(0, S // RC, prologue, jnp.zeros((1, LANES), F32))

        @pl.when(b == 0)
        def _():
            dg_ref[...] = dg_part

        @pl.when(b != 0)
        def _():
            dg_ref[...] += dg_part

        for bi, d in enumerate(DILATIONS):
            nb = S // (BAND * d)
            L = S // d
            _gather_residues(qs, lambda rows: qkv_ref.at[0][rows, :], d, S, lambda v: (v * scale).astype(BF16))
            _gather_residues(ks, lambda rows: qkv_ref.at[1][rows, :], d, S, lambda v: v.astype(BF16))
            _gather_residues(vs, lambda rows: qkv_ref.at[2][rows, :], d, S, lambda v: v.astype(BF16))
            _gather_residues(dos, lambda rows: do_n[rows, :], d, S, lambda v: v.astype(BF16))
            _gather_residues(lses, lambda rows: lse_ref[rows, :], d, S, lambda v: v)
            _gather_residues(dds, lambda rows: dd_n[rows, :], d, S, lambda v: v)
            dkp[...] = jnp.zeros((S, LANES), F32)
            dvp[...] = jnp.zeros((S, LANES), F32)

            def block(t, carry, bi=bi, d=d, nb=nb):
                cur, prev, has_prev = _block_rows(t, d, S)
                q = qs[cur, :]
                do = dos[cur, :]
                lse_t = lses[cur, :]
                dd_t = dds[cur, :]
                if nb > 1:
                    kc = jnp.concatenate([ks[prev, :], ks[cur, :]], axis=0)
                    vc = jnp.concatenate([vs[prev, :], vs[cur, :]], axis=0)
                    pen = _first_block_penalty(has_prev)
                else:
                    kc, vc = ks[cur, :], vs[cur, :]
                dq = None
                dk = None
                dv = None
                for hh in range(2):
                    mine = is_a == (hh == 0)
                    qm = jnp.where(mine, q, jnp.zeros_like(q))
                    dom = jnp.where(mine, do, jnp.zeros_like(do))
                    c0 = hh * GROUP
                    s = _nt(qm, kc)
                    if nb > 1:
                        s = s + bm[hh * nbr + bi] + pen
                    else:
                        s = s + bm[hh * nbr + bi, :, BAND:2 * BAND]
                    p = jnp.exp(s - lse_t[:, c0:c0 + 1])
                    dp = _nt(dom, vc)
                    ds = (p * (dp - dd_t[:, c0:c0 + 1])).astype(BF16)
                    pb = p.astype(BF16)
                    dq_h = jnp.dot(ds, kc, preferred_element_type=F32)
                    dk_h = _tn(ds, qm)
                    dv_h = _tn(pb, dom)
                    dq = dq_h if dq is None else jnp.where(is_a, dq, dq_h)
                    dk = dk_h if dk is None else dk + dk_h
                    dv = dv_h if dv is None else dv + dv_h
                dqp[cur, :] = dq
                if nb > 1:
                    dkp[prev, :] += dk[0:BAND, :]
                    dvp[prev, :] += dv[0:BAND, :]
                    dkp[cur, :] += dk[BAND:2 * BAND, :]
                    dvp[cur, :] += dv[BAND:2 * BAND, :]
                else:
                    dkp[cur, :] += dk
                    dvp[cur, :] += dv
                return carry

            lax.fori_loop(0, NBLK, block, 0, unroll=2)
            for r in range(d):
                rows = pl.ds(r, L, stride=d) if d > 1 else slice(None)
                dqn[rows, :] += dqp[r * L:(r + 1) * L, :]
                dkn[rows, :] += dkp[r * L:(r + 1) * L, :]
                dvn[rows, :] += dvp[r * L:(r + 1) * L, :]

        dqkv_ref[0] = (dqn[...] * scale).astype(BF16)
        dqkv_ref[1] = dkn[...].astype(BF16)
        dqkv_ref[2] = dvn[...].astype(BF16)

    seq = pl.BlockSpec((S, LANES), lambda p, b: (b, p))
    f32_seq = pltpu.VMEM((S, LANES), F32)
    bf_seq = pltpu.VMEM((S, LANES), BF16)
    return pl.pallas_call(
        body, grid=(P, B), name=name,
        in_specs=[pl.BlockSpec((3, S, LANES), lambda p, b: (0, b, p)), seq, seq,
                  pl.BlockSpec((None, S, LANES), lambda p, b: (0, b, p)),
                  pl.BlockSpec((1, LANES), lambda p, b: (0, p))],
        out_specs=[pl.BlockSpec((3, S, LANES), lambda p, b: (0, b, p)),
                   pl.BlockSpec((1, LANES), lambda p, b: (0, p))],
        out_shape=[jax.ShapeDtypeStruct((dproj_pieces, T, C), BF16), jax.ShapeDtypeStruct((1, C), F32)],
        scratch_shapes=[f32_seq, f32_seq, bf_seq, bf_seq, bf_seq, bf_seq, f32_seq, f32_seq,
                        f32_seq, f32_seq, f32_seq, f32_seq, f32_seq, f32_seq,
                        pltpu.VMEM((2 * nbr, BAND, 2 * BAND), F32)],
        compiler_params=_params(("parallel", "arbitrary")),
    )(qkv3, o, lse, dmix3, gain)


def _delay(x, k, row):
    return jnp.where(row >= k, pltpu.roll(x, k, 0), 0.0)


def _advance(x, k, row, S):
    return jnp.where(row < S - k, pltpu.roll(x, S - k, 0), 0.0)


def _conv3(x, w, row):
    return (w[0:1, :] * _delay(x, 2, row) + w[1:2, :] * _delay(x, 1, row)) + w[2:3, :] * x


def _conv3_grads(dz, x, w, row, S):
    dx = (w[2:3, :] * dz + w[1:2, :] * _advance(dz, 1, row, S)) + w[0:1, :] * _advance(dz, 2, row, S)
    dw = jnp.concatenate([jnp.sum(dz * _delay(x, 2, row), axis=0, keepdims=True),
                          jnp.sum(dz * _delay(x, 1, row), axis=0, keepdims=True),
                          jnp.sum(dz * x, axis=0, keepdims=True)], axis=0)
    return dx, dw


def _mix_conv_fwd(cv3, taps, gain, mix, S, name):
    _, T, C = cv3.shape
    B, P = T // S, C // LANES

    def body(cv_ref, w_ref, g_ref, mix_hbm, y_ref):
        del mix_hbm
        row = lax.broadcasted_iota(jnp.int32, (S, 1), 0)
        is_a = _lane_is_a()
        gb = cv_ref[0].astype(F32)
        c = cv_ref[1].astype(F32) * cv_ref[2].astype(F32)
        y = gb * _conv3(c, w_ref[...], row)
        rstd = lax.rsqrt(_seg_sum(y * y, is_a) * (1.0 / GROUP) + EPS)
        y_ref[...] = ((y * rstd) * g_ref[...]).astype(BF16)

    return pl.pallas_call(
        body, grid=(B, P), name=name,
        in_specs=[pl.BlockSpec((3, S, LANES), lambda b, p: (0, b, p)),
                  pl.BlockSpec((3, LANES), lambda b, p: (0, p)),
                  pl.BlockSpec((1, LANES), lambda b, p: (0, p)),
                  pl.BlockSpec(memory_space=pl.ANY)],
        out_specs=pl.BlockSpec((None, S, LANES), lambda b, p: (1, b, p)),
        out_shape=jax.ShapeDtypeStruct(mix.shape, mix.dtype),
        input_output_aliases={3: 0},
        compiler_params=_params(("parallel", "parallel")),
    )(cv3, taps, gain, mix)


def _mix_conv_bwd(cv3, dmix3, taps, gain, dproj, S, name):
    _, T, C = cv3.shape
    B, P = T // S, C // LANES

    def body(cv_ref, dn_ref, w_ref, g_ref, dproj_hbm, dcv_ref, dw_ref, dg_ref):
        del dproj_hbm
        b = pl.program_id(1)
        row = lax.broadcasted_iota(jnp.int32, (S, 1), 0)
        is_a = _lane_is_a()
        w = w_ref[...]
        gb = cv_ref[0].astype(F32)
        gc = cv_ref[1].astype(F32)
        u = cv_ref[2].astype(F32)
        c = gc * u
        z = _conv3(c, w, row)
        y = gb * z
        rstd = lax.rsqrt(_seg_sum(y * y, is_a) * (1.0 / GROUP) + EPS)
        yn = y * rstd
        dn = dn_ref[...]
        a = dn * g_ref[...]
        dy = rstd * (a - yn * (_seg_sum(a * yn, is_a) * (1.0 / GROUP)))
        dg = jnp.sum(dn * yn, axis=0, keepdims=True)
        dc, dw = _conv3_grads(dy * gb, c, w, row, S)
        dcv_ref[0] = (dy * z).astype(BF16)
        dcv_ref[1] = (dc * u).astype(BF16)
        dcv_ref[2] = (dc * gc).astype(BF16)

        @pl.when(b == 0)
        def _():
            dw_ref[...] = dw
            dg_ref[...] = dg

        @pl.when(b != 0)
        def _():
            dw_ref[...] += dw
            dg_ref[...] += dg

    return pl.pallas_call(
        body, grid=(P, B), name=name,
        in_specs=[pl.BlockSpec((3, S, LANES), lambda p, b: (0, b, p)),
                  pl.BlockSpec((None, S, LANES), lambda p, b: (1, b, p)),
                  pl.BlockSpec((3, LANES), lambda p, b: (0, p)),
                  pl.BlockSpec((1, LANES), lambda p, b: (0, p)),
                  pl.BlockSpec(memory_space=pl.ANY)],
        out_specs=[pl.BlockSpec((3, S, LANES), lambda p, b: (1, b, p)),
                   pl.BlockSpec((3, LANES), lambda p, b: (0, p)),
                   pl.BlockSpec((1, LANES), lambda p, b: (0, p))],
        out_shape=[jax.ShapeDtypeStruct(dproj.shape, dproj.dtype),
                   jax.ShapeDtypeStruct((3, C), F32), jax.ShapeDtypeStruct((1, C), F32)],
        input_output_aliases={4: 0},
        compiler_params=_params(("parallel", "arbitrary")),
    )(cv3, dmix3, taps, gain, dproj)


def _sigmoid(x):
    return 1.0 / (1.0 + jnp.exp(-x))


def _ffn_act_fwd(up3, taps, S, name):
    _, T, Fd = up3.shape
    B, P = T // S, Fd // LANES

    def body(up_ref, wg_ref, wv_ref, act_ref):
        row = lax.broadcasted_iota(jnp.int32, (S, 1), 0)
        cg = _conv3(up_ref[0].astype(F32), wg_ref[...], row)
        cv = _conv3(up_ref[1].astype(F32), wv_ref[...], row)
        act_ref[...] = ((cg * _sigmoid(cg)) * cv).astype(BF16)

    return pl.pallas_call(
        body, grid=(B, P), name=name,
        in_specs=[pl.BlockSpec((2, S, LANES), lambda b, p: (0, b, p)),
                  pl.BlockSpec((3, LANES), lambda b, p: (0, p)),
                  pl.BlockSpec((3, LANES), lambda b, p: (0, P + p))],
        out_specs=pl.BlockSpec((S, LANES), lambda b, p: (b, p)),
        out_shape=jax.ShapeDtypeStruct((T, Fd), BF16),
        compiler_params=_params(("parallel", "parallel")),
    )(up3, taps, taps)


def _ffn_act_bwd(up3, dact3, taps, S, name):
    _, T, Fd = up3.shape
    B, P = T // S, Fd // LANES

    def body(up_ref, da_ref, wg_ref, wv_ref, dup_ref, dwg_ref, dwv_ref):
        b = pl.program_id(1)
        row = lax.broadcasted_iota(jnp.int32, (S, 1), 0)
        ug = up_ref[0].astype(F32)
        uv = up_ref[1].astype(F32)
        wg = wg_ref[...]
        wv = wv_ref[...]
        cg = _conv3(ug, wg, row)
        cv = _conv3(uv, wv, row)
        sg = _sigmoid(cg)
        da = da_ref[...].astype(F32)
        dcg = (da * cv) * (sg * (1.0 + cg * (1.0 - sg)))
        dcv = da * (cg * sg)
        dug, dwg = _conv3_grads(dcg, ug, wg, row, S)
        duv, dwv = _conv3_grads(dcv, uv, wv, row, S)
        dup_ref[0] = dug.astype(BF16)
        dup_ref[1] = duv.astype(BF16)

        @pl.when(b == 0)
        def _():
            dwg_ref[...] = dwg
            dwv_ref[...] = dwv

        @pl.when(b != 0)
        def _():
            dwg_ref[...] += dwg
            dwv_ref[...] += dwv

    tap_out = pl.BlockSpec((3, LANES), lambda p, b: (0, p))
    return pl.pallas_call(
        body, grid=(P, B), name=name,
        in_specs=[pl.BlockSpec((2, S, LANES), lambda p, b: (0, b, p)),
                  pl.BlockSpec((None, S, LANES), lambda p, b: (0, b, p)),
                  pl.BlockSpec((3, LANES), lambda p, b: (0, p)),
                  pl.BlockSpec((3, LANES), lambda p, b: (0, P + p))],
        out_specs=[pl.BlockSpec((2, S, LANES), lambda p, b: (0, b, p)), tap_out, tap_out],
        out_shape=[jax.ShapeDtypeStruct((2, T, Fd), BF16),
                   jax.ShapeDtypeStruct((3, Fd), F32), jax.ShapeDtypeStruct((3, Fd), F32)],
        compiler_params=_params(("parallel", "arbitrary")),
    )(up3, dact3, taps, taps)


def _final_norm_loss(x, g, target, tm, name):
    T, D = x.shape

    def body(x_ref, g_ref, t_ref, dx_ref, dg_ref, loss_ref):
        xv = x_ref[...]
        rstd = lax.rsqrt(jnp.mean(xv * xv, axis=-1, keepdims=True) + EPS)
        xn = xv * rstd
        err = xn * g_ref[...] - t_ref[...]
        part = 0.5 * jnp.sum(jnp.mean(err * err, axis=-1, keepdims=True), axis=0, keepdims=True)
        dy = err * (1.0 / D)
        a = dy * g_ref[...]
        dx_ref[...] = rstd * (a - xn * jnp.mean(a * xn, axis=-1, keepdims=True))
        dg = jnp.sum(dy * xn, axis=0, keepdims=True)
        lpart = jnp.broadcast_to(part, (1, LANES))

        @pl.when(pl.program_id(0) == 0)
        def _():
            dg_ref[...] = dg
            loss_ref[...] = lpart

        @pl.when(pl.program_id(0) != 0)
        def _():
            dg_ref[...] += dg
            loss_ref[...] += lpart

    row = pl.BlockSpec((tm, D), lambda i: (i, 0))
    return pl.pallas_call(
        body, grid=(T // tm,), name=name,
        in_specs=[row, pl.BlockSpec((1, D), lambda i: (0, 0)), row],
        out_specs=[row, pl.BlockSpec((1, D), lambda i: (0, 0)), pl.BlockSpec((1, LANES), lambda i: (0, 0))],
        out_shape=[jax.ShapeDtypeStruct((T, D), F32), jax.ShapeDtypeStruct((1, D), F32),
                   jax.ShapeDtypeStruct((1, LANES), F32)],
        compiler_params=_params(("arbitrary",)),
    )(x, g, target)


def _row_tile(rows, cols, budget_elems=512 * 1024):
    tr = rows
    while tr * cols > budget_elems and tr % 32 == 0:
        tr //= 2
    return tr


def _prefetch_call(body, grid, in_specs, out_specs, out_shape, name, sem, aliases=None):
    return pl.pallas_call(
        body, name=name, out_shape=out_shape,
        grid_spec=pltpu.PrefetchScalarGridSpec(num_scalar_prefetch=1, grid=grid, in_specs=in_specs,
                                               out_specs=out_specs),
        input_output_aliases=aliases or {},
        compiler_params=_params(sem))


def _cast_into_full(w, colwise, where, name):
    L, K, N = w.shape
    tr = _row_tile(K, N)
    nrb = K // tr
    full_shape = (L, K, 4 * N) if colwise else (L, 4 * K, N)

    def body(where_ref, w_ref, o_ref):
        del where_ref
        o_ref[...] = w_ref[...].astype(BF16)

    if colwise:
        out_map = lambda l, i, wh: (l, i, wh[0])
    else:
        out_map = lambda l, i, wh: (l, wh[0] * nrb + i, 0)
    return _prefetch_call(
        body, (L, nrb), [pl.BlockSpec((None, tr, N), lambda l, i, wh: (l, i, 0))],
        pl.BlockSpec((None, tr, N), out_map), jax.ShapeDtypeStruct(full_shape, BF16), name,
        ("parallel", "parallel"))(where, w)


def _chip_sum(g3, other, colwise, where, name):
    L, K, N = g3.shape
    hk, hn = (K // 2, N) if colwise else (K, N // 2)
    tr = _row_tile(hk, hn)
    nrb = hk // tr

    def body(where_ref, g_ref, o_ref, s_ref):
        del where_ref
        s_ref[...] = (g_ref[...].astype(F32) + o_ref[...].astype(F32)).astype(BF16)

    if colwise:
        g_map = lambda l, i, wh: (l, wh[1] * nrb + i, 0)
    else:
        g_map = lambda l, i, wh: (l, i, wh[1])
    blk = pl.BlockSpec((None, tr, hn), lambda l, i, wh: (l, i, 0))
    return _prefetch_call(
        body, (L, nrb), [pl.BlockSpec((None, tr, hn), g_map), blk], blk,
        jax.ShapeDtypeStruct((L, hk, hn), BF16), name, ("parallel", "parallel"))(where, g3, other)


def _owner_sum(chip_sum, received, colwise, where, name):
    L, hk, hn = chip_sum.shape
    pk, pn = (hk, hn // 4) if colwise else (hk // 4, hn)
    tr = _row_tile(pk, pn)
    nrb = pk // tr
    shard_shape = (L, 2 * pk, pn) if colwise else (L, pk, 2 * pn)

    def body(where_ref, own_ref, rec_ref, o_ref):
        del where_ref
        acc = own_ref[...].astype(F32)
        for j in range(3):
            acc = acc + rec_ref[j].astype(F32)
        o_ref[...] = acc

    if colwise:
        own_map = lambda l, i, wh: (l, i, wh[0])
        out_map = lambda l, i, wh: (l, wh[1] * nrb + i, 0)
    else:
        own_map = lambda l, i, wh: (l, wh[0] * nrb + i, 0)
        out_map = lambda l, i, wh: (l, i, wh[1])
    return _prefetch_call(
        body, (L, nrb),
        [pl.BlockSpec((None, tr, pn), own_map),
         pl.BlockSpec((3, None, tr, pn), lambda l, i, wh: (0, l, i, 0))],
        pl.BlockSpec((None, tr, pn), out_map), jax.ShapeDtypeStruct(shard_shape, F32), name,
        ("parallel", "parallel"))(where, chip_sum, received)


def _adamw(w, g, m, v, name):
    R, Cc = w.shape
    tr = _row_tile(R, Cc, 256 * 1024)

    def body(w_ref, g_ref, m_ref, v_ref, d_ref, nm_ref, nv_ref):
        gv = g_ref[...]
        nm = ADAM_B1 * m_ref[...] + (1.0 - ADAM_B1) * gv
        nv = ADAM_B2 * v_ref[...] + (1.0 - ADAM_B2) * (gv * gv)
        m_hat = nm / (1.0 - ADAM_B1 ** ADAM_STEP)
        v_hat = nv / (1.0 - ADAM_B2 ** ADAM_STEP)
        d_ref[...] = -ADAM_LR * (m_hat / (jnp.sqrt(v_hat) + ADAM_EPS) + ADAM_WD * w_ref[...])
        nm_ref[...] = nm
        nv_ref[...] = nv

    blk = pl.BlockSpec((tr, Cc), lambda i: (i, 0))
    shp = jax.ShapeDtypeStruct((R, Cc), F32)
    return pl.pallas_call(
        body, grid=(R // tr,), name=name,
        in_specs=[blk] * 4, out_specs=[blk] * 3, out_shape=[shp] * 3,
        compiler_params=_params(("parallel",)),
    )(w, g, m, v)


COL_SHARDED = (True, False, True, False)
ANY = pl.BlockSpec(memory_space=pl.ANY)


def _position():
    x, y, c = lax.axis_index("x"), lax.axis_index("y"), lax.axis_index("c")
    chips = [(1 - x, y), (x, 1 - y), (1 - x, 1 - y)]
    return x, y, c, chips


def _span(index, size, align):
    return pl.ds(pl.multiple_of(index * size, align), size)


def _window(ref, colwise, shard, half, shards=4):
    _, K, N = ref.shape
    rows = cols = slice(None)
    if colwise:
        if half is not None:
            rows = _span(half, K // 2, 16)
        if shard is not None:
            cols = _span(shard, N // shards, LANES)
    else:
        if shard is not None:
            rows = _span(shard, K // shards, 16)
        if half is not None:
            cols = _span(half, N // 2, LANES)
    return ref.at[:, rows, cols]


def _gather_weights(fulls):
    nw = len(fulls)

    def body(*refs):
        full = refs[nw:2 * nw]
        send_sems, recv_sems = refs[2 * nw:]
        x, y, c, chips = _position()
        me = 2 * x + y

        def over_ici(i, j, chip, origin):
            win = _window(full[i], COL_SHARDED[i], origin, c)
            return pltpu.make_async_remote_copy(
                src_ref=win, dst_ref=win, send_sem=send_sems.at[i * 3 + j], recv_sem=recv_sems.at[i * 3 + j],
                device_id=(chip[0], chip[1], c), device_id_type=MESH_ID)

        def to_sibling(i, j, chip, half):
            win = _window(full[i], COL_SHARDED[i], 2 * chip[0] + chip[1], half)
            k = (nw + i) * 3 + j
            return pltpu.make_async_remote_copy(
                src_ref=win, dst_ref=win, send_sem=send_sems.at[k], recv_sem=recv_sems.at[k],
                device_id=(x, y, 1 - c), device_id_type=MESH_ID)

        sends = [over_ici(i, j, chip, me) for i in range(nw) for j, chip in enumerate(chips)]
        for cp in sends:
            cp.start()
        passes = []
        for i in range(nw):
            for j, chip in enumerate(chips):
                over_ici(i, j, chip, 2 * chip[0] + chip[1]).wait_recv()
                cp = to_sibling(i, j, chip, c)
                cp.start()
                passes.append(cp)
        for i in range(nw):
            for j, chip in enumerate(chips):
                to_sibling(i, j, chip, 1 - c).wait_recv()
        for cp in sends + passes:
            cp.wait_send()

    return pl.pallas_call(
        body, name="gather_weights",
        in_specs=[ANY] * nw, out_specs=[ANY] * nw,
        out_shape=[jax.ShapeDtypeStruct(f.shape, f.dtype) for f in fulls],
        input_output_aliases={i: i for i in range(nw)},
        scratch_shapes=[pltpu.SemaphoreType.DMA((2 * nw * 3,)), pltpu.SemaphoreType.DMA((2 * nw * 3,))],
    )(*fulls)


def _exchange_halves(grads):
    n = len(grads)
    out_shapes = []
    for g, cw in zip(grads, COL_SHARDED):
        L, K, N = g.shape
        out_shapes.append(jax.ShapeDtypeStruct((L, K // 2, N) if cw else (L, K, N // 2), g.dtype))

    def body(*refs):
        g_refs, out = refs[:n], refs[n:2 * n]
        send_sems, recv_sems = refs[2 * n:]
        x, y, c, _ = _position()
        copies = [pltpu.make_async_remote_copy(
            src_ref=_window(g_refs[i], COL_SHARDED[i], None, 1 - c), dst_ref=out[i],
            send_sem=send_sems.at[i], recv_sem=recv_sems.at[i],
            device_id=(x, y, 1 - c), device_id_type=MESH_ID) for i in range(n)]
        for cp in copies:
            cp.start()
        for cp in copies:
            cp.wait()

    return pl.pallas_call(
        body, name="grad_exchange_halves",
        in_specs=[ANY] * n, out_specs=[ANY] * n, out_shape=out_shapes,
        scratch_shapes=[pltpu.SemaphoreType.DMA((n,)), pltpu.SemaphoreType.DMA((n,))],
    )(*grads)


def _scatter_to_owners(chip_sums):
    n = len(chip_sums)
    out_shapes = []
    for g, cw in zip(chip_sums, COL_SHARDED):
        L, hk, hn = g.shape
        out_shapes.append(jax.ShapeDtypeStruct((3, L, hk, hn // 4) if cw else (3, L, hk // 4, hn), g.dtype))

    def body(*refs):
        g_refs, out = refs[:n], refs[n:2 * n]
        send_sems, recv_sems = refs[2 * n:]
        x, y, c, chips = _position()
        copies = [pltpu.make_async_remote_copy(
            src_ref=_window(g_refs[i], COL_SHARDED[i], 2 * chip[0] + chip[1], None), dst_ref=out[i].at[j],
            send_sem=send_sems.at[i * 3 + j], recv_sem=recv_sems.at[i * 3 + j],
            device_id=(chip[0], chip[1], c), device_id_type=MESH_ID)
            for i in range(n) for j, chip in enumerate(chips)]
        for cp in copies:
            cp.start()
        for cp in copies:
            cp.wait()

    return pl.pallas_call(
        body, name="grad_scatter_to_owners",
        in_specs=[ANY] * n, out_specs=[ANY] * n, out_shape=out_shapes,
        scratch_shapes=[pltpu.SemaphoreType.DMA((3 * n,)), pltpu.SemaphoreType.DMA((3 * n,))],
    )(*chip_sums)


def _share_with_sibling(shards):
    n = len(shards)

    def body(*refs):
        out = refs[n:2 * n]
        send_sems, recv_sems = refs[2 * n:]
        x, y, c, _ = _position()

        def copy(i, half):
            win = _window(out[i], COL_SHARDED[i], None, half)
            return pltpu.make_async_remote_copy(
                src_ref=win, dst_ref=win, send_sem=send_sems.at[i], recv_sem=recv_sems.at[i],
                device_id=(x, y, 1 - c), device_id_type=MESH_ID)

        for i in range(n):
            copy(i, c).start()
        for i in range(n):
            copy(i, 1 - c).wait_recv()
        for i in range(n):
            copy(i, c).wait_send()

    return pl.pallas_call(
        body, name="grad_share_with_sibling",
        in_specs=[ANY] * n, out_specs=[ANY] * n,
        out_shape=[jax.ShapeDtypeStruct(s.shape, s.dtype) for s in shards],
        input_output_aliases={i: i for i in range(n)},
        scratch_shapes=[pltpu.SemaphoreType.DMA((n,)), pltpu.SemaphoreType.DMA((n,))],
    )(*shards)


def _all_reduce_small(pack, name):
    R, Cc = pack.shape
    n_dev = 8

    def body(p_ref, o_ref, buf, send_sems, recv_sems):
        x, y, c, _ = _position()
        me = 4 * x + 2 * y + c
        buf[me] = p_ref[...]

        def peer(k):
            px = 1 - x if k & 4 else x
            py = 1 - y if k & 2 else y
            pc = 1 - c if k & 1 else c
            return px, py, pc

        def copy(k, incoming):
            px, py, pc = peer(k)
            slot = (4 * px + 2 * py + pc) if incoming else me
            return pltpu.make_async_remote_copy(
                src_ref=p_ref, dst_ref=buf.at[slot], send_sem=send_sems.at[k], recv_sem=recv_sems.at[k],
                device_id=(px, py, pc), device_id_type=MESH_ID)

        for k in range(1, n_dev):
            copy(k, False).start()
        for k in range(1, n_dev):
            copy(k, True).wait_recv()
        for k in range(1, n_dev):
            copy(k, False).wait_send()
        acc = buf[0]
        for j in range(1, n_dev):
            acc = acc + buf[j]
        o_ref[...] = acc

    vmem = pl.BlockSpec(memory_space=pltpu.VMEM)
    return pl.pallas_call(
        body, name=name,
        in_specs=[vmem], out_specs=vmem, out_shape=jax.ShapeDtypeStruct((R, Cc), F32),
        scratch_shapes=[pltpu.VMEM((n_dev, R, Cc), F32), pltpu.SemaphoreType.DMA((n_dev,)),
                        pltpu.SemaphoreType.DMA((n_dev,))],
    )(pack)


def _local_forward_backward(x2, target2, S, big_w, layers, final_g, tm=512):
    T, D = x2.shape
    C = D // 2
    n_heads = C // GROUP
    n_layers = len(layers)
    w_in, w_out, ffn_up, ffn_down = big_w
    Fd = ffn_down.shape[1]
    saved = []
    xc = x2
    for li, lw in enumerate(layers):
        h1, qkv3, cv3 = _norm_proj(xc, lw["norm1"], w_in, li, ((3, C, F32), (3, C, BF16)), tm, min(C, 512),
                                   f"l{li}_norm_in_proj")
        o, lse, mix = _attn_fwd(qkv3, lw["attn_g"], 2, S, n_heads, f"l{li}_attn_fwd")
        mix = _mix_conv_fwd(cv3, lw["taps"], lw["conv_g"], mix, S, f"l{li}_mix_conv_fwd")
        x_mid = _proj_residual(mix, w_out, li, xc, tm, f"l{li}_out_proj")
        h2, up3 = _norm_proj(x_mid, lw["norm2"], ffn_up, li, ((2, Fd, BF16),), tm // 2, 256, f"l{li}_norm_ffn_up")
        act = _ffn_act_fwd(up3, lw["ffn_taps"], S, f"l{li}_ffn_act_fwd")
        x_out = _proj_residual(act.reshape(1, T, Fd), ffn_down, li, x_mid, tm, f"l{li}_ffn_down")
        saved.append(dict(x_in=xc, h1=h1, qkv3=qkv3, cv3=cv3, o=o, lse=lse, mix=mix, x_mid=x_mid, h2=h2, up3=up3,
                          act=act))
        xc = x_out

    dx, d_final_g, loss_part = _final_norm_loss(xc, final_g, target2, tm, "final_norm_loss")

    d_w_in = d_w_out = d_ffn_up = d_ffn_down = None
    small = [None] * n_layers
    for li in reversed(range(n_layers)):
        lw, sv = layers[li], saved[li]
        dxb, dact3 = _grad_through_weight(dx, ffn_down, li, 1, Fd, BF16, tm, 256, f"l{li}_d_act")
        d_ffn_down = _weight_grad(sv["act"].reshape(1, T, Fd), dxb.reshape(1, T, D), Fd // 2, D, 1024,
                                  li, n_layers, d_ffn_down, f"l{li}_d_ffn_down")
        dup3, d_taps_g, d_taps_v = _ffn_act_bwd(sv["up3"], dact3, lw["ffn_taps"], S, f"l{li}_ffn_act_bwd")
        d_ffn_up = _weight_grad(sv["h2"].reshape(1, T, D), dup3, D, Fd // 2, 1024, li, n_layers, d_ffn_up,
                                f"l{li}_d_ffn_up")
        dx_mid, d_norm2 = _grad_through_proj_norm(dup3, ffn_up, li, sv["x_mid"], lw["norm2"], dx, tm // 2,
                                                  f"l{li}_d_norm2")
        dxmb, dmix3 = _grad_through_weight(dx_mid, w_out, li, 2, C, F32, tm, min(C, 512), f"l{li}_d_mix")
        d_w_out = _weight_grad(sv["mix"], dxmb.reshape(1, T, D), C, D, 1024, li, n_layers, d_w_out,
                               f"l{li}_d_w_out")
        dproj, d_attn_g = _attn_bwd(sv["qkv3"], sv["o"], sv["lse"], dmix3, lw["attn_g"], 6, S, n_heads,
                                    f"l{li}_attn_bwd")
        dproj, d_taps, d_conv_g = _mix_conv_bwd(sv["cv3"], dmix3, lw["taps"], lw["conv_g"], dproj, S,
                                                f"l{li}_mix_conv_bwd")
        d_w_in = _weight_grad(sv["h1"].reshape(1, T, D), dproj, D, C, 1024, li, n_layers, d_w_in,
                              f"l{li}_d_w_in")
        dx, d_norm1 = _grad_through_proj_norm(dproj, w_in, li, sv["x_in"], lw["norm1"], dx_mid, tm,
                                              f"l{li}_d_norm1")
        small[li] = dict(norm1=d_norm1, taps=d_taps, attn_g=d_attn_g, conv_g=d_conv_g, norm2=d_norm2,
                         ffn_taps=jnp.concatenate([d_taps_g, d_taps_v], axis=1))
    return loss_part, dx, (d_w_in, d_w_out, d_ffn_up, d_ffn_down), small, d_final_g


SMALL_ORDER = ("norm1", "attn_g", "conv_g", "norm2", "taps", "ffn_taps")


def _pack_small(small, d_final_g):
    parts = [small[li][k].reshape(-1) for li in range(len(small)) for k in SMALL_ORDER] + [d_final_g.reshape(-1)]
    return jnp.concatenate(parts).reshape(-1, LANES)


def _unpack_small(pack, small, d_final_g):
    flat = pack.reshape(-1)
    out, pos = [dict() for _ in small], 0
    for li in range(len(small)):
        for k in SMALL_ORDER:
            n = small[li][k].size
            out[li][k] = flat[pos:pos + n].reshape(small[li][k].shape)
            pos += n
    return out, flat[pos:pos + d_final_g.size]


def kernel(x, norm1_g, w_in, mix_conv_w, attn_out_g, conv_out_g, w_out, norm2_g, ffn_up, ffn_conv_w, ffn_down, final_norm_g, loss_target, m_norm1_g, m_w_in, m_mix_conv_w, m_attn_out_g, m_conv_out_g, m_w_out, m_norm2_g, m_ffn_up, m_ffn_conv_w, m_ffn_down, m_final_norm_g, v_norm1_g, v_w_in, v_mix_conv_w, v_attn_out_g, v_conv_out_g, v_w_out, v_norm2_g, v_ffn_up, v_ffn_conv_w, v_ffn_down, v_final_norm_g):
    Bl, S, D = x.shape
    L = w_in.shape[0]
    T = Bl * S
    shard = 2 * lax.axis_index("x") + lax.axis_index("y")
    where = jnp.stack([shard, lax.axis_index("c")]).astype(jnp.int32)
    big_names = ("w_in", "w_out", "ffn_up", "ffn_down")

    fulls = [_cast_into_full(w, cw, where, f"cast_{n}")
             for w, cw, n in zip((w_in, w_out, ffn_up, ffn_down), COL_SHARDED, big_names)]
    fulls = _gather_weights(fulls)

    taps_w, ftaps_w = mix_conv_w.shape[2], ffn_conv_w.shape[2]
    taps_full = jnp.zeros((L, 3, 4 * taps_w), F32)
    taps_full = lax.dynamic_update_slice(taps_full, mix_conv_w, (0, 0, shard * taps_w))
    ftaps_full = jnp.zeros((L, 3, 4 * ftaps_w), F32)
    ftaps_full = lax.dynamic_update_slice(ftaps_full, ffn_conv_w, (0, 0, shard * ftaps_w))
    tap_pack = jnp.concatenate([taps_full.reshape(-1), ftaps_full.reshape(-1)]).reshape(-1, LANES)
    tap_pack = _all_reduce_small(tap_pack * 0.5, "all_gather_taps")
    n_taps = taps_full.size
    taps_full = tap_pack.reshape(-1)[:n_taps].reshape(taps_full.shape)
    ftaps_full = tap_pack.reshape(-1)[n_taps:].reshape(ftaps_full.shape)

    layers = [dict(norm1=norm1_g[l:l + 1], taps=taps_full[l], attn_g=attn_out_g[l:l + 1],
                   conv_g=conv_out_g[l:l + 1], norm2=norm2_g[l:l + 1], ffn_taps=ftaps_full[l]) for l in range(L)]

    loss_part, dx, big, small, d_final_g = _local_forward_backward(
        x.reshape(T, D), loss_target.reshape(T, D), S, fulls, layers, final_norm_g.reshape(1, D))
    loss = lax.psum(loss_part[0, 0], ("x", "y", "c"))

    others = _exchange_halves(big)
    chip_sums = [_chip_sum(g, o, cw, where, f"chip_sum_{n}")
                 for g, o, cw, n in zip(big, others, COL_SHARDED, big_names)]
    received = _scatter_to_owners(chip_sums)
    g_big = [_owner_sum(s, r, cw, where, f"owner_sum_{n}")
             for s, r, cw, n in zip(chip_sums, received, COL_SHARDED, big_names)]
    g_big = _share_with_sibling(g_big)

    pack = _all_reduce_small(_pack_small(small, d_final_g), "all_reduce_small_grads")
    g_small, g_final = _unpack_small(pack, small, d_final_g)

    def stacked(key):
        return jnp.stack([g_small[l][key].reshape(g_small[l][key].shape[-2:] if key.endswith("taps") else (-1,))
                          for l in range(L)])

    g_norm1, g_attn, g_conv, g_norm2 = stacked("norm1"), stacked("attn_g"), stacked("conv_g"), stacked("norm2")
    g_taps = lax.dynamic_slice(stacked("taps"), (0, 0, shard * taps_w), (L, 3, taps_w))
    g_ftaps = lax.dynamic_slice(stacked("ffn_taps"), (0, 0, shard * ftaps_w), (L, 3, ftaps_w))

    grads_out = dict(norm1_g=g_norm1, w_in=g_big[0], mix_conv_w=g_taps, attn_out_g=g_attn, conv_out_g=g_conv,
                     w_out=g_big[1], norm2_g=g_norm2, ffn_up=g_big[2], ffn_conv_w=g_ftaps, ffn_down=g_big[3],
                     final_norm_g=g_final)
    weights = dict(norm1_g=norm1_g, w_in=w_in, mix_conv_w=mix_conv_w, attn_out_g=attn_out_g, conv_out_g=conv_out_g,
                   w_out=w_out, norm2_g=norm2_g, ffn_up=ffn_up, ffn_conv_w=ffn_conv_w, ffn_down=ffn_down,
                   final_norm_g=final_norm_g)
    ms = dict(norm1_g=m_norm1_g, w_in=m_w_in, mix_conv_w=m_mix_conv_w, attn_out_g=m_attn_out_g,
              conv_out_g=m_conv_out_g, w_out=m_w_out, norm2_g=m_norm2_g, ffn_up=m_ffn_up, ffn_conv_w=m_ffn_conv_w,
              ffn_down=m_ffn_down, final_norm_g=m_final_norm_g)
    vs = dict(norm1_g=v_norm1_g, w_in=v_w_in, mix_conv_w=v_mix_conv_w, attn_out_g=v_attn_out_g,
              conv_out_g=v_conv_out_g, w_out=v_w_out, norm2_g=v_norm2_g, ffn_up=v_ffn_up, ffn_conv_w=v_ffn_conv_w,
              ffn_down=v_ffn_down, final_norm_g=v_final_norm_g)
    names = list(weights)
    small_names = [n for n in names if n not in big_names]
    delta, new_m, new_v = {}, {}, {}
    for n in big_names:
        shp = weights[n].shape
        two_d = (shp[0] * shp[1], shp[2])
        d_, m_, v_ = _adamw(weights[n].reshape(two_d), grads_out[n].reshape(two_d), ms[n].reshape(two_d),
                            vs[n].reshape(two_d), f"adamw_{n}")
        delta[n], new_m[n], new_v[n] = d_.reshape(shp), m_.reshape(shp), v_.reshape(shp)

    def packed(tree):
        return jnp.concatenate([tree[n].reshape(-1) for n in small_names]).reshape(-1, LANES)

    d_, m_, v_ = _adamw(packed(weights), packed(grads_out), packed(ms), packed(vs), "adamw_small")
    pos = 0
    for n in small_names:
        size, shp = weights[n].size, weights[n].shape
        delta[n] = d_.reshape(-1)[pos:pos + size].reshape(shp)
        new_m[n] = m_.reshape(-1)[pos:pos + size].reshape(shp)
        new_v[n] = v_.reshape(-1)[pos:pos + size].reshape(shp)
        pos += size

    return (loss, dx.reshape(Bl, S, D), *[grads_out[n] for n in names], *[delta[n] for n in names],
            *[new_m[n] for n in names], *[new_v[n] for n in names])
```

```python
import functools
import math

import jax
import jax.numpy as jnp
from jax import lax
from jax.experimental import pallas as pl
from jax.experimental.pallas import tpu as pltpu

F32 = jnp.float32
BF16 = jnp.bfloat16
EPS = 1e-6
GROUP = 64
LANES = 128
BAND = 128
DILATIONS = (1, 4, 16)
NEG = -1e30
MIB = 1024 * 1024
MESH_ID = pl.DeviceIdType.MESH

ADAM_LR = 0.001
ADAM_B1 = 0.9
ADAM_B2 = 0.999
ADAM_EPS = 1e-08
ADAM_WD = 0.01
ADAM_STEP = 10


def _params(sem=None, vmem_mb=48):
    return pltpu.CompilerParams(dimension_semantics=sem, vmem_limit_bytes=vmem_mb * MIB)


def _nt(a, b):
    return lax.dot_general(a, b, (((1,), (1,)), ((), ())), preferred_element_type=F32)


def _tn(a, b):
    return lax.dot_general(a, b, (((0,), (0,)), ((), ())), preferred_element_type=F32)


def _seg_sum(x, is_a):
    s_a = jnp.sum(jnp.where(is_a, x, 0.0), axis=-1, keepdims=True)
    s_b = jnp.sum(jnp.where(is_a, 0.0, x), axis=-1, keepdims=True)
    return jnp.where(is_a, s_a, s_b)


def _lane_is_a():
    return lax.broadcasted_iota(jnp.int32, (1, LANES), 1) < GROUP


def _norm_proj(x, g, w3, layer, groups, tm, chunk, name):
    T, D = x.shape
    N = w3.shape[2]
    assert sum(p * c for p, c, _ in groups) == N and T % tm == 0

    def body(x_ref, g_ref, w_ref, h_ref, *out_refs):
        xv = x_ref[...]
        rstd = lax.rsqrt(jnp.mean(xv * xv, axis=-1, keepdims=True) + EPS)
        h = ((xv * rstd) * g_ref[...]).astype(BF16)
        h_ref[...] = h
        col = 0
        for (pieces, width, dtype), o_ref in zip(groups, out_refs):
            for p in range(pieces):
                for c0 in range(0, width, chunk):
                    acc = jnp.dot(h, w_ref[:, col + c0:col + c0 + chunk], preferred_element_type=F32)
                    o_ref[p, :, c0:c0 + chunk] = acc.astype(dtype)
                col += width

    out_shape = [jax.ShapeDtypeStruct((T, D), BF16)]
    out_specs = [pl.BlockSpec((tm, D), lambda i: (i, 0))]
    for pieces, width, dtype in groups:
        assert width % chunk == 0
        out_shape.append(jax.ShapeDtypeStruct((pieces, T, width), dtype))
        out_specs.append(pl.BlockSpec((pieces, tm, width), lambda i: (0, i, 0)))
    return pl.pallas_call(
        body, grid=(T // tm,), name=name,
        in_specs=[pl.BlockSpec((tm, D), lambda i: (i, 0)),
                  pl.BlockSpec((1, D), lambda i: (0, 0)),
                  pl.BlockSpec((None, D, N), lambda i: (layer, 0, 0))],
        out_specs=out_specs, out_shape=out_shape,
        compiler_params=_params(("parallel",), 56),
    )(x, g, w3)


def _proj_residual(pieces3, w3, layer, x, tm, name):
    P, T, C = pieces3.shape
    D = w3.shape[2]

    def body(a_ref, w_ref, x_ref, o_ref):
        acc = x_ref[...]
        for p in range(P):
            acc = acc + jnp.dot(a_ref[p], w_ref[p * C:(p + 1) * C, :], preferred_element_type=F32)
        o_ref[...] = acc

    return pl.pallas_call(
        body, grid=(T // tm,), name=name,
        in_specs=[pl.BlockSpec((P, tm, C), lambda i: (0, i, 0)),
                  pl.BlockSpec((None, P * C, D), lambda i: (layer, 0, 0)),
                  pl.BlockSpec((tm, D), lambda i: (i, 0))],
        out_specs=pl.BlockSpec((tm, D), lambda i: (i, 0)),
        out_shape=jax.ShapeDtypeStruct((T, D), F32),
        compiler_params=_params(("parallel",)),
    )(pieces3, w3, x)


def _grad_through_weight(dy, w3, layer, pieces, width, out_dtype, tm, chunk, name):
    T, D = dy.shape

    def body(dy_ref, w_ref, dyb_ref, o_ref):
        dyb = dy_ref[...].astype(BF16)
        dyb_ref[...] = dyb
        for p in range(pieces):
            for c0 in range(0, width, chunk):
                r0 = p * width + c0
                o_ref[p, :, c0:c0 + chunk] = _nt(dyb, w_ref[r0:r0 + chunk, :]).astype(out_dtype)

    return pl.pallas_call(
        body, grid=(T // tm,), name=name,
        in_specs=[pl.BlockSpec((tm, D), lambda i: (i, 0)),
                  pl.BlockSpec((None, pieces * width, D), lambda i: (layer, 0, 0))],
        out_specs=[pl.BlockSpec((tm, D), lambda i: (i, 0)),
                   pl.BlockSpec((pieces, tm, width), lambda i: (0, i, 0))],
        out_shape=[jax.ShapeDtypeStruct((T, D), BF16),
                   jax.ShapeDtypeStruct((pieces, T, width), out_dtype)],
        compiler_params=_params(("parallel",)),
    )(dy, w3)


def _grad_through_proj_norm(dp3, w3, layer, x, g, dx_in, tm, name):
    P, T, C = dp3.shape
    D = w3.shape[1]

    def body(dp_ref, w_ref, x_ref, g_ref, dxin_ref, dx_ref, dg_ref):
        dh = _nt(dp_ref[0], w_ref[:, 0:C])
        for p in range(1, P):
            dh = dh + _nt(dp_ref[p], w_ref[:, p * C:(p + 1) * C])
        xv = x_ref[...]
        rstd = lax.rsqrt(jnp.mean(xv * xv, axis=-1, keepdims=True) + EPS)
        xn = xv * rstd
        a = dh * g_ref[...]
        dx_ref[...] = dxin_ref[...] + rstd * (a - xn * jnp.mean(a * xn, axis=-1, keepdims=True))
        part = jnp.sum(dh * xn, axis=0, keepdims=True)

        @pl.when(pl.program_id(0) == 0)
        def _():
            dg_ref[...] = part

        @pl.when(pl.program_id(0) != 0)
        def _():
            dg_ref[...] += part

    return pl.pallas_call(
        body, grid=(T // tm,), name=name,
        in_specs=[pl.BlockSpec((P, tm, C), lambda i: (0, i, 0)),
                  pl.BlockSpec((None, D, P * C), lambda i: (layer, 0, 0)),
                  pl.BlockSpec((tm, D), lambda i: (i, 0)),
                  pl.BlockSpec((1, D), lambda i: (0, 0)),
                  pl.BlockSpec((tm, D), lambda i: (i, 0))],
        out_specs=[pl.BlockSpec((tm, D), lambda i: (i, 0)),
                   pl.BlockSpec((1, D), lambda i: (0, 0))],
        out_shape=[jax.ShapeDtypeStruct((T, D), F32), jax.ShapeDtypeStruct((1, D), F32)],
        compiler_params=_params(("arbitrary",), 56),
    )(dp3, w3, x, g, dx_in)


def _weight_grad(a3, g3, ta, tg, tt, layer, n_layers, prev, name):
    PA, T, CA = a3.shape
    PG, _, CG = g3.shape
    na, ng, nt = CA // ta, CG // tg, T // tt
    assert CA % ta == 0 and CG % tg == 0 and T % tt == 0

    def body(a_ref, g_ref, *rest):
        o_ref, acc_ref = rest[-2:]
        t = pl.program_id(2)
        part = _tn(a_ref[...], g_ref[...])

        @pl.when(t == 0)
        def _():
            acc_ref[...] = part

        @pl.when(t != 0)
        def _():
            acc_ref[...] += part

        @pl.when(t == nt - 1)
        def _():
            o_ref[...] = acc_ref[...].astype(o_ref.dtype)

    in_specs = [pl.BlockSpec((None, tt, ta), lambda i, j, t: (i // na, t, i % na)),
                pl.BlockSpec((None, tt, tg), lambda i, j, t: (j // ng, t, j % ng))]
    operands = [a3, g3]
    if prev is not None:
        in_specs.append(pl.BlockSpec(memory_space=pl.ANY))
        operands.append(prev)
    return pl.pallas_call(
        body, grid=(PA * na, PG * ng, nt), name=name,
        in_specs=in_specs,
        out_specs=pl.BlockSpec((None, ta, tg), lambda i, j, t: (layer, i, j)),
        out_shape=jax.ShapeDtypeStruct((n_layers, PA * CA, PG * CG), BF16),
        scratch_shapes=[pltpu.VMEM((ta, tg), F32)],
        input_output_aliases={} if prev is None else {2: 0},
        compiler_params=_params(("parallel", "parallel", "arbitrary"), 56),
    )(*operands)


def _bias_tables(bm_ref, pair, n_heads):
    ii = lax.broadcasted_iota(jnp.int32, (BAND, 2 * BAND), 0)
    jj = lax.broadcasted_iota(jnp.int32, (BAND, 2 * BAND), 1)
    dist = BAND + ii - jj
    valid = (dist >= 0) & (dist <= BAND)
    distf = dist.astype(F32)
    for hh in range(2):
        head = (2 * pair + hh + 1).astype(F32)
        slope = jnp.exp(jnp.full((1, 1), -8.0 / n_heads * math.log(2.0), F32) * head)
        for bi, d in enumerate(DILATIONS):
            bm_ref[hh * len(DILATIONS) + bi] = jnp.where(valid, -(slope * d) * distf, NEG)


def _gather_residues(dst_ref, src, d, S, convert):
    L = S // d
    for r in range(d):
        rows = pl.ds(r, L, stride=d) if d > 1 else slice(None)
        dst_ref[r * L:(r + 1) * L, :] = convert(src(rows))


def _block_rows(t, d, S):
    nb = S // (BAND * d)
    n = t % nb
    has_prev = jnp.minimum(n, 1)
    cur = pl.ds(pl.multiple_of(t * BAND, BAND), BAND)
    prev = pl.ds(pl.multiple_of((t - has_prev) * BAND, BAND), BAND)
    return cur, prev, has_prev


def _first_block_penalty(has_prev):
    jrow = lax.broadcasted_iota(jnp.int32, (1, 2 * BAND), 1)
    pen = jnp.where(has_prev == 0, NEG, 0.0).astype(F32)
    return jnp.where(jrow < BAND, pen, 0.0)


def _attn_fwd(qkv3, gain, mix_shape_pieces, S, n_heads, name):
    _, T, C = qkv3.shape
    B, P = T // S, C // LANES
    NBLK = S // BAND
    scale = GROUP ** -0.5
    nbr = len(DILATIONS)
    RC = 256

    def body(qkv_ref, g_ref, o_ref, lse_ref, an_ref, qs, ks, vs, op, mp, lp, ob, mb, lb, bm):
        pair = pl.program_id(1)
        is_a = _lane_is_a()
        _bias_tables(bm, pair, n_heads)

        for bi, d in enumerate(DILATIONS):
            nb = S // (BAND * d)
            _gather_residues(qs, lambda rows: qkv_ref.at[0][rows, :], d, S, lambda v: (v * scale).astype(BF16))
            _gather_residues(ks, lambda rows: qkv_ref.at[1][rows, :], d, S, lambda v: v.astype(BF16))
            _gather_residues(vs, lambda rows: qkv_ref.at[2][rows, :], d, S, lambda v: v.astype(BF16))
            o_dst, m_dst, l_dst = (ob.at[bi], mb.at[bi], lb.at[bi]) if d == 1 else (op, mp, lp)

            def block(t, carry, bi=bi, d=d, nb=nb, o_dst=o_dst, m_dst=m_dst, l_dst=l_dst):
                cur, prev, has_prev = _block_rows(t, d, S)
                q = qs[cur, :]
                kc = jnp.concatenate([ks[prev, :], ks[cur, :]], axis=0)
                vc = jnp.concatenate([vs[prev, :], vs[cur, :]], axis=0)
                pen = _first_block_penalty(has_prev)
                outs = []
                for hh in range(2):
                    qm = jnp.where(is_a == (hh == 0), q, jnp.zeros_like(q))
                    s = _nt(qm, kc) + bm[hh * nbr + bi] + pen
                    m = jnp.max(s, axis=-1, keepdims=True)
                    e = jnp.exp(s - m)
                    l = jnp.sum(e, axis=-1, keepdims=True)
                    pv = jnp.dot(e.astype(BF16), vc, preferred_element_type=F32)
                    outs.append((pv, m, l))
                o_dst[cur, :] = jnp.where(is_a, outs[0][0], outs[1][0])
                m_dst[cur, :] = jnp.where(is_a, outs[0][1], outs[1][1])
                l_dst[cur, :] = jnp.where(is_a, outs[0][2], outs[1][2])
                return carry

            lax.fori_loop(0, NBLK, block, 0, unroll=8)
            if d > 1:
                L = S // d
                for r in range(d):
                    rows = pl.ds(r, L, stride=d)
                    ob.at[bi][rows, :] = op[r * L:(r + 1) * L, :]
                    mb.at[bi][rows, :] = mp[r * L:(r + 1) * L, :]
                    lb.at[bi][rows, :] = lp[r * L:(r + 1) * L, :]

        def finish(ci, carry):
            rs = pl.ds(pl.multiple_of(ci * RC, RC), RC)
            ms = [mb[bi, rs, :] for bi in range(nbr)]
            mmax = functools.reduce(jnp.maximum, ms)
            ws = [jnp.exp(m - mmax) for m in ms]
            num = sum(ob[bi, rs, :] * ws[bi] for bi in range(nbr))
            den = sum(lb[bi, rs, :] * ws[bi] for bi in range(nbr))
            o = num / den
            o_ref[rs, :] = o
            lse_ref[rs, :] = mmax + jnp.log(den)
            rstd = lax.rsqrt(_seg_sum(o * o, is_a) * (1.0 / GROUP) + EPS)
            an_ref[rs, :] = ((o * rstd) * g_ref[...]).astype(BF16)
            return carry

        lax.fori_loop(0, S // RC, finish, 0)

    seq = pl.BlockSpec((S, LANES), lambda b, p: (b, p))
    return pl.pallas_call(
        body, grid=(B, P), name=name,
        in_specs=[pl.BlockSpec((3, S, LANES), lambda b, p: (0, b, p)),
                  pl.BlockSpec((1, LANES), lambda b, p: (0, p))],
        out_specs=[seq, seq, pl.BlockSpec((None, S, LANES), lambda b, p: (0, b, p))],
        out_shape=[jax.ShapeDtypeStruct((T, C), F32), jax.ShapeDtypeStruct((T, C), F32),
                   jax.ShapeDtypeStruct((mix_shape_pieces, T, C), BF16)],
        scratch_shapes=[pltpu.VMEM((S, LANES), BF16)] * 3 + [pltpu.VMEM((S, LANES), F32)] * 3
        + [pltpu.VMEM((nbr, S, LANES), F32)] * 3 + [pltpu.VMEM((2 * nbr, BAND, 2 * BAND), F32)],
        compiler_params=_params(("parallel", "parallel")),
    )(qkv3, gain)


def _attn_bwd(qkv3, o, lse, dmix3, gain, dproj_pieces, S, n_heads, name):
    _, T, C = qkv3.shape
    B, P = T // S, C // LANES
    NBLK = S // BAND
    scale = GROUP ** -0.5
    nbr = len(DILATIONS)
    RC = 256

    def body(qkv_ref, o_ref, lse_ref, dn_ref, g_ref, dqkv_ref, dg_ref,
             do_n, dd_n, qs, ks, vs, dos, lses, dds, dqp, dkp, dvp, dqn, dkn, dvn, bm):
        pair = pl.program_id(0)
        b = pl.program_id(1)
        is_a = _lane_is_a()
        _bias_tables(bm, pair, n_heads)

        def prologue(ci, dg_acc):
            rs = pl.ds(pl.multiple_of(ci * RC, RC), RC)
            ov = o_ref[rs, :]
            dn = dn_ref[rs, :]
            rstd = lax.rsqrt(_seg_sum(ov * ov, is_a) * (1.0 / GROUP) + EPS)
            on = ov * rstd
            a = dn * g_ref[...]
            do = rstd * (a - on * (_seg_sum(a * on, is_a) * (1.0 / GROUP)))
            do_n[rs, :] = do
            dd_n[rs, :] = _seg_sum(do * ov, is_a)
            zero = jnp.zeros((RC, LANES), F32)
            dqn[rs, :] = zero
            dkn[rs, :] = zero
            dvn[rs, :] = zero
            return dg_acc + jnp.sum(dn * on, axis=0, keepdims=True)

        dg_part = lax.fori_loop(0, S // RC, prologue, jnp.zeros((1, LANES), F32))

        @pl.when(b == 0)
        def _():
            dg_ref[...] = dg_part

        @pl.when(b != 0)
        def _():
            dg_ref[...] += dg_part

        for bi, d in enumerate(DILATIONS):
            nb = S // (BAND * d)
            L = S // d
            _gather_residues(qs, lambda rows: qkv_ref.at[0][rows, :], d, S, lambda v: (v * scale).astype(BF16))
            _gather_residues(ks, lambda rows: qkv_ref.at[1][rows, :], d, S, lambda v: v.astype(BF16))
            _gather_residues(vs, lambda rows: qkv_ref.at[2][rows, :], d, S, lambda v: v.astype(BF16))
            _gather_residues(dos, lambda rows: do_n[rows, :], d, S, lambda v: v.astype(BF16))
            _gather_residues(lses, lambda rows: lse_ref[rows, :], d, S, lambda v: v)
            _gather_residues(dds, lambda rows: dd_n[rows, :], d, S, lambda v: v)
            dkp[...] = jnp.zeros((S, LANES), F32)
            dvp[...] = jnp.zeros((S, LANES), F32)

            def block(t, carry, bi=bi, d=d, nb=nb):
                cur, prev, has_prev = _block_rows(t, d, S)
                q = qs[cur, :]
                do = dos[cur, :]
                lse_t = lses[cur, :]
                dd_t = dds[cur, :]
                if nb > 1:
                    kc = jnp.concatenate([ks[prev, :], ks[cur, :]], axis=0)
                    vc = jnp.concatenate([vs[prev, :], vs[cur, :]], axis=0)
                    pen = _first_block_penalty(has_prev)
                else:
                    kc, vc = ks[cur, :], vs[cur, :]
                dq = None
                dk = None
                dv = None
                for hh in range(2):
                    mine = is_a == (hh == 0)
                    qm = jnp.where(mine, q, jnp.zeros_like(q))
                    dom = jnp.where(mine, do, jnp.zeros_like(do))
                    c0 = hh * GROUP
                    s = _nt(qm, kc)
                    if nb > 1:
                        s = s + bm[hh * nbr + bi] + pen
                    else:
                        s = s + bm[hh * nbr + bi, :, BAND:2 * BAND]
                    p = jnp.exp(s - lse_t[:, c0:c0 + 1])
                    dp = _nt(dom, vc)
                    ds = (p * (dp - dd_t[:, c0:c0 + 1])).astype(BF16)
                    pb = p.astype(BF16)
                    dq_h = jnp.dot(ds, kc, preferred_element_type=F32)
                    dk_h = _tn(ds, qm)
                    dv_h = _tn(pb, dom)
                    dq = dq_h if dq is None else jnp.where(is_a, dq, dq_h)
                    dk = dk_h if dk is None else dk + dk_h
                    dv = dv_h if dv is None else dv + dv_h
                dqp[cur, :] = dq
                if nb > 1:
                    dkp[prev, :] += dk[0:BAND, :]
                    dvp[prev, :] += dv[0:BAND, :]
                    dkp[cur, :] += dk[BAND:2 * BAND, :]
                    dvp[cur, :] += dv[BAND:2 * BAND, :]
                else:
                    dkp[cur, :] += dk
                    dvp[cur, :] += dv
                return carry

            lax.fori_loop(0, NBLK, block, 0, unroll=4)
            for r in range(d):
                rows = pl.ds(r, L, stride=d) if d > 1 else slice(None)
                dqn[rows, :] += dqp[r * L:(r + 1) * L, :]
                dkn[rows, :] += dkp[r * L:(r + 1) * L, :]
                dvn[rows, :] += dvp[r * L:(r + 1) * L, :]

        dqkv_ref[0] = (dqn[...] * scale).astype(BF16)
        dqkv_ref[1] = dkn[...].astype(BF16)
        dqkv_ref[2] = dvn[...].astype(BF16)

    seq = pl.BlockSpec((S, LANES), lambda p, b: (b, p))
    f32_seq = pltpu.VMEM((S, LANES), F32)
    bf_seq = pltpu.VMEM((S, LANES), BF16)
    return pl.pallas_call(
        body, grid=(P, B), name=name,
        in_specs=[pl.BlockSpec((3, S, LANES), lambda p, b: (0, b, p)), seq, seq,
                  pl.BlockSpec((None, S, LANES), lambda p, b: (0, b, p)),
                  pl.BlockSpec((1, LANES), lambda p, b: (0, p))],
        out_specs=[pl.BlockSpec((3, S, LANES), lambda p, b: (0, b, p)),
                   pl.BlockSpec((1, LANES), lambda p, b: (0, p))],
        out_shape=[jax.ShapeDtypeStruct((dproj_pieces, T, C), BF16), jax.ShapeDtypeStruct((1, C), F32)],
        scratch_shapes=[f32_seq, f32_seq, bf_seq, bf_seq, bf_seq, bf_seq, f32_seq, f32_seq,
                        f32_seq, f32_seq, f32_seq, f32_seq, f32_seq, f32_seq,
                        pltpu.VMEM((2 * nbr, BAND, 2 * BAND), F32)],
        compiler_params=_params(("parallel", "arbitrary")),
    )(qkv3, o, lse, dmix3, gain)


def _delay(x, k, row):
    return jnp.where(row >= k, pltpu.roll(x, k, 0), 0.0)


def _advance(x, k, row, S):
    return jnp.where(row < S - k, pltpu.roll(x, S - k, 0), 0.0)


def _conv3(x, w, row):
    return (w[0:1, :] * _delay(x, 2, row) + w[1:2, :] * _delay(x, 1, row)) + w[2:3, :] * x


def _conv3_grads(dz, x, w, row, S):
    dx = (w[2:3, :] * dz + w[1:2, :] * _advance(dz, 1, row, S)) + w[0:1, :] * _advance(dz, 2, row, S)
    dw = jnp.concatenate([jnp.sum(dz * _delay(x, 2, row), axis=0, keepdims=True),
                          jnp.sum(dz * _delay(x, 1, row), axis=0, keepdims=True),
                          jnp.sum(dz * x, axis=0, keepdims=True)], axis=0)
    return dx, dw


def _mix_conv_fwd(cv3, taps, gain, mix, S, name):
    _, T, C = cv3.shape
    B, P = T // S, C // LANES

    def body(cv_ref, w_ref, g_ref, mix_hbm, y_ref):
        del mix_hbm
        row = lax.broadcasted_iota(jnp.int32, (S, 1), 0)
        is_a = _lane_is_a()
        gb = cv_ref[0].astype(F32)
        c = cv_ref[1].astype(F32) * cv_ref[2].astype(F32)
        y = gb * _conv3(c, w_ref[...], row)
        rstd = lax.rsqrt(_seg_sum(y * y, is_a) * (1.0 / GROUP) + EPS)
        y_ref[...] = ((y * rstd) * g_ref[...]).astype(BF16)

    return pl.pallas_call(
        body, grid=(B, P), name=name,
        in_specs=[pl.BlockSpec((3, S, LANES), lambda b, p: (0, b, p)),
                  pl.BlockSpec((3, LANES), lambda b, p: (0, p)),
                  pl.BlockSpec((1, LANES), lambda b, p: (0, p)),
                  pl.BlockSpec(memory_space=pl.ANY)],
        out_specs=pl.BlockSpec((None, S, LANES), lambda b, p: (1, b, p)),
        out_shape=jax.ShapeDtypeStruct(mix.shape, mix.dtype),
        input_output_aliases={3: 0},
        compiler_params=_params(("parallel", "parallel")),
    )(cv3, taps, gain, mix)


def _mix_conv_bwd(cv3, dmix3, taps, gain, dproj, S, name):
    _, T, C = cv3.shape
    B, P = T // S, C // LANES

    def body(cv_ref, dn_ref, w_ref, g_ref, dproj_hbm, dcv_ref, dw_ref, dg_ref):
        del dproj_hbm
        b = pl.program_id(1)
        row = lax.broadcasted_iota(jnp.int32, (S, 1), 0)
        is_a = _lane_is_a()
        w = w_ref[...]
        gb = cv_ref[0].astype(F32)
        gc = cv_ref[1].astype(F32)
        u = cv_ref[2].astype(F32)
        c = gc * u
        z = _conv3(c, w, row)
        y = gb * z
        rstd = lax.rsqrt(_seg_sum(y * y, is_a) * (1.0 / GROUP) + EPS)
        yn = y * rstd
        dn = dn_ref[...]
        a = dn * g_ref[...]
        dy = rstd * (a - yn * (_seg_sum(a * yn, is_a) * (1.0 / GROUP)))
        dg = jnp.sum(dn * yn, axis=0, keepdims=True)
        dc, dw = _conv3_grads(dy * gb, c, w, row, S)
        dcv_ref[0] = (dy * z).astype(BF16)
        dcv_ref[1] = (dc * u).astype(BF16)
        dcv_ref[2] = (dc * gc).astype(BF16)

        @pl.when(b == 0)
        def _():
            dw_ref[...] = dw
            dg_ref[...] = dg

        @pl.when(b != 0)
        def _():
            dw_ref[...] += dw
            dg_ref[...] += dg

    return pl.pallas_call(
        body, grid=(P, B), name=name,
        in_specs=[pl.BlockSpec((3, S, LANES), lambda p, b: (0, b, p)),
                  pl.BlockSpec((None, S, LANES), lambda p, b: (1, b, p)),
                  pl.BlockSpec((3, LANES), lambda p, b: (0, p)),
                  pl.BlockSpec((1, LANES), lambda p, b: (0, p)),
                  pl.BlockSpec(memory_space=pl.ANY)],
        out_specs=[pl.BlockSpec((3, S, LANES), lambda p, b: (1, b, p)),
                   pl.BlockSpec((3, LANES), lambda p, b: (0, p)),
                   pl.BlockSpec((1, LANES), lambda p, b: (0, p))],
        out_shape=[jax.ShapeDtypeStruct(dproj.shape, dproj.dtype),
                   jax.ShapeDtypeStruct((3, C), F32), jax.ShapeDtypeStruct((1, C), F32)],
        input_output_aliases={4: 0},
        compiler_params=_params(("parallel", "arbitrary")),
    )(cv3, dmix3, taps, gain, dproj)


def _sigmoid(x):
    return 1.0 / (1.0 + jnp.exp(-x))


def _ffn_act_fwd(up3, taps, S, name):
    _, T, Fd = up3.shape
    B, P = T // S, Fd // LANES

    def body(up_ref, wg_ref, wv_ref, act_ref):
        row = lax.broadcasted_iota(jnp.int32, (S, 1), 0)
        cg = _conv3(up_ref[0].astype(F32), wg_ref[...], row)
        cv = _conv3(up_ref[1].astype(F32), wv_ref[...], row)
        act_ref[...] = ((cg * _sigmoid(cg)) * cv).astype(BF16)

    return pl.pallas_call(
        body, grid=(B, P), name=name,
        in_specs=[pl.BlockSpec((2, S, LANES), lambda b, p: (0, b, p)),
                  pl.BlockSpec((3, LANES), lambda b, p: (0, p)),
                  pl.BlockSpec((3, LANES), lambda b, p: (0, P + p))],
        out_specs=pl.BlockSpec((S, LANES), lambda b, p: (b, p)),
        out_shape=jax.ShapeDtypeStruct((T, Fd), BF16),
        compiler_params=_params(("parallel", "parallel")),
    )(up3, taps, taps)


def _ffn_act_bwd(up3, dact3, taps, S, name):
    _, T, Fd = up3.shape
    B, P = T // S, Fd // LANES

    def body(up_ref, da_ref, wg_ref, wv_ref, dup_ref, dwg_ref, dwv_ref):
        b = pl.program_id(1)
        row = lax.broadcasted_iota(jnp.int32, (S, 1), 0)
        ug = up_ref[0].astype(F32)
        uv = up_ref[1].astype(F32)
        wg = wg_ref[...]
        wv = wv_ref[...]
        cg = _conv3(ug, wg, row)
        cv = _conv3(uv, wv, row)
        sg = _sigmoid(cg)
        da = da_ref[...].astype(F32)
        dcg = (da * cv) * (sg * (1.0 + cg * (1.0 - sg)))
        dcv = da * (cg * sg)
        dug, dwg = _conv3_grads(dcg, ug, wg, row, S)
        duv, dwv = _conv3_grads(dcv, uv, wv, row, S)
        dup_ref[0] = dug.astype(BF16)
        dup_ref[1] = duv.astype(BF16)

        @pl.when(b == 0)
        def _():
            dwg_ref[...] = dwg
            dwv_ref[...] = dwv

        @pl.when(b != 0)
        def _():
            dwg_ref[...] += dwg
            dwv_ref[...] += dwv

    tap_out = pl.BlockSpec((3, LANES), lambda p, b: (0, p))
    return pl.pallas_call(
        body, grid=(P, B), name=name,
        in_specs=[pl.BlockSpec((2, S, LANES), lambda p, b: (0, b, p)),
                  pl.BlockSpec((None, S, LANES), lambda p, b: (0, b, p)),
                  pl.BlockSpec((3, LANES), lambda p, b: (0, p)),
                  pl.BlockSpec((3, LANES), lambda p, b: (0, P + p))],
        out_specs=[pl.BlockSpec((2, S, LANES), lambda p, b: (0, b, p)), tap_out, tap_out],
        out_shape=[jax.ShapeDtypeStruct((2, T, Fd), BF16),
                   jax.ShapeDtypeStruct((3, Fd), F32), jax.ShapeDtypeStruct((3, Fd), F32)],
        compiler_params=_params(("parallel", "arbitrary")),
    )(up3, dact3, taps, taps)


def _final_norm_loss(x, g, target, tm, name):
    T, D = x.shape

    def body(x_ref, g_ref, t_ref, dx_ref, dg_ref, loss_ref):
        xv = x_ref[...]
        rstd = lax.rsqrt(jnp.mean(xv * xv, axis=-1, keepdims=True) + EPS)
        xn = xv * rstd
        err = xn * g_ref[...] - t_ref[...]
        part = 0.5 * jnp.sum(jnp.mean(err * err, axis=-1, keepdims=True), axis=0, keepdims=True)
        dy = err * (1.0 / D)
        a = dy * g_ref[...]
        dx_ref[...] = rstd * (a - xn * jnp.mean(a * xn, axis=-1, keepdims=True))
        dg = jnp.sum(dy * xn, axis=0, keepdims=True)
        lpart = jnp.broadcast_to(part, (1, LANES))

        @pl.when(pl.program_id(0) == 0)
        def _():
            dg_ref[...] = dg
            loss_ref[...] = lpart

        @pl.when(pl.program_id(0) != 0)
        def _():
            dg_ref[...] += dg
            loss_ref[...] += lpart

    row = pl.BlockSpec((tm, D), lambda i: (i, 0))
    return pl.pallas_call(
        body, grid=(T // tm,), name=name,
        in_specs=[row, pl.BlockSpec((1, D), lambda i: (0, 0)), row],
        out_specs=[row, pl.BlockSpec((1, D), lambda i: (0, 0)), pl.BlockSpec((1, LANES), lambda i: (0, 0))],
        out_shape=[jax.ShapeDtypeStruct((T, D), F32), jax.ShapeDtypeStruct((1, D), F32),
                   jax.ShapeDtypeStruct((1, LANES), F32)],
        compiler_params=_params(("arbitrary",)),
    )(x, g, target)


def _row_tile(rows, cols, budget_elems=512 * 1024):
    tr = rows
    while tr * cols > budget_elems and tr % 32 == 0:
        tr //= 2
    return tr


def _prefetch_call(body, grid, in_specs, out_specs, out_shape, name, sem, aliases=None):
    return pl.pallas_call(
        body, name=name, out_shape=out_shape,
        grid_spec=pltpu.PrefetchScalarGridSpec(num_scalar_prefetch=1, grid=grid, in_specs=in_specs,
                                               out_specs=out_specs),
        input_output_aliases=aliases or {},
        compiler_params=_params(sem))


def _cast_into_full(w, layer, colwise, where, name):
    _, K, N = w.shape
    tr = _row_tile(K, N)
    nrb = K // tr
    full_shape = (1, K, 4 * N) if colwise else (1, 4 * K, N)

    def body(where_ref, w_ref, o_ref):
        del where_ref
        o_ref[...] = w_ref[...].astype(BF16)

    if colwise:
        out_map = lambda i, wh: (0, i, wh[0])
    else:
        out_map = lambda i, wh: (0, wh[0] * nrb + i, 0)
    return _prefetch_call(
        body, (nrb,), [pl.BlockSpec((None, tr, N), lambda i, wh: (layer, i, 0))],
        pl.BlockSpec((None, tr, N), out_map), jax.ShapeDtypeStruct(full_shape, BF16), name,
        ("parallel",))(where, w)


def _chip_sum(g3, other, colwise, where, name):
    L, K, N = g3.shape
    hk, hn = (K // 2, N) if colwise else (K, N // 2)
    tr = _row_tile(hk, hn)
    nrb = hk // tr

    def body(where_ref, g_ref, o_ref, s_ref):
        del where_ref
        s_ref[...] = (g_ref[...].astype(F32) + o_ref[...].astype(F32)).astype(BF16)

    if colwise:
        g_map = lambda l, i, wh: (l, wh[1] * nrb + i, 0)
    else:
        g_map = lambda l, i, wh: (l, i, wh[1])
    blk = pl.BlockSpec((None, tr, hn), lambda l, i, wh: (l, i, 0))
    return _prefetch_call(
        body, (L, nrb), [pl.BlockSpec((None, tr, hn), g_map), blk], blk,
        jax.ShapeDtypeStruct((L, hk, hn), BF16), name, ("parallel", "parallel"))(where, g3, other)


def _owner_sum(chip_sum, received, colwise, where, name):
    L, hk, hn = chip_sum.shape
    pk, pn = (hk, hn // 4) if colwise else (hk // 4, hn)
    tr = _row_tile(pk, pn)
    nrb = pk // tr
    shard_shape = (L, 2 * pk, pn) if colwise else (L, pk, 2 * pn)

    def body(where_ref, own_ref, rec_ref, o_ref):
        del where_ref
        acc = own_ref[...].astype(F32)
        for j in range(3):
            acc = acc + rec_ref[j].astype(F32)
        o_ref[...] = acc

    if colwise:
        own_map = lambda l, i, wh: (l, i, wh[0])
        out_map = lambda l, i, wh: (l, wh[1] * nrb + i, 0)
    else:
        own_map = lambda l, i, wh: (l, wh[0] * nrb + i, 0)
        out_map = lambda l, i, wh: (l, i, wh[1])
    return _prefetch_call(
        body, (L, nrb),
        [pl.BlockSpec((None, tr, pn), own_map),
         pl.BlockSpec((3, None, tr, pn), lambda l, i, wh: (0, l, i, 0))],
        pl.BlockSpec((None, tr, pn), out_map), jax.ShapeDtypeStruct(shard_shape, F32), name,
        ("parallel", "parallel"))(where, chip_sum, received)


def _adamw(w, g, m, v, name):
    R, Cc = w.shape
    tr = _row_tile(R, Cc, 256 * 1024)

    def body(w_ref, g_ref, m_ref, v_ref, d_ref, nm_ref, nv_ref):
        gv = g_ref[...]
        nm = ADAM_B1 * m_ref[...] + (1.0 - ADAM_B1) * gv
        nv = ADAM_B2 * v_ref[...] + (1.0 - ADAM_B2) * (gv * gv)
        m_hat = nm / (1.0 - ADAM_B1 ** ADAM_STEP)
        v_hat = nv / (1.0 - ADAM_B2 ** ADAM_STEP)
        d_ref[...] = -ADAM_LR * (m_hat / (jnp.sqrt(v_hat) + ADAM_EPS) + ADAM_WD * w_ref[...])
        nm_ref[...] = nm
        nv_ref[...] = nv

    blk = pl.BlockSpec((tr, Cc), lambda i: (i, 0))
    shp = jax.ShapeDtypeStruct((R, Cc), F32)
    return pl.pallas_call(
        body, grid=(R // tr,), name=name,
        in_specs=[blk] * 4, out_specs=[blk] * 3, out_shape=[shp] * 3,
        compiler_params=_params(("parallel",)),
    )(w, g, m, v)


COL_SHARDED = (True, False, True, False)
ANY = pl.BlockSpec(memory_space=pl.ANY)


def _position():
    x, y, c = lax.axis_index("x"), lax.axis_index("y"), lax.axis_index("c")
    chips = [(1 - x, y), (x, 1 - y), (1 - x, 1 - y)]
    return x, y, c, chips


def _span(index, size, align):
    return pl.ds(pl.multiple_of(index * size, align), size)


def _window(ref, colwise, shard, half, shards=4):
    _, K, N = ref.shape
    rows = cols = slice(None)
    if colwise:
        if half is not None:
            rows = _span(half, K // 2, 16)
        if shard is not None:
            cols = _span(shard, N // shards, LANES)
    else:
        if shard is not None:
            rows = _span(shard, K // shards, 16)
        if half is not None:
            cols = _span(half, N // 2, LANES)
    return ref.at[:, rows, cols]


HBM = pl.BlockSpec(memory_space=pltpu.HBM)
SEMAPHORES = pl.BlockSpec(memory_space=pltpu.SEMAPHORE)


def _gather_start(fulls, colwise, group_sizes):
    n = len(fulls)
    n_groups = len(group_sizes)

    def body(*refs):
        ins = refs[:n]
        sems = refs[n:n + 2 * n_groups]
        x, y, c, chips = _position()
        me = 2 * x + y
        i = 0
        for g, size in enumerate(group_sizes):
            for a in range(size):
                win = _window(ins[i], colwise[i], me, None)
                for j, chip in enumerate(chips):
                    pltpu.make_async_remote_copy(
                        src_ref=win, dst_ref=win, send_sem=sems[2 * g].at[a * 3 + j],
                        recv_sem=sems[2 * g + 1].at[a * 3 + j],
                        device_id=(chip[0], chip[1], c), device_id_type=MESH_ID).start()
                i += 1

    sem_shapes = []
    for size in group_sizes:
        sem_shapes += [pltpu.SemaphoreType.DMA((3 * size,)), pltpu.SemaphoreType.DMA((3 * size,))]
    outs = pl.pallas_call(
        body, name="gather_start",
        in_specs=[HBM] * n, out_specs=[SEMAPHORES] * (2 * n_groups) + [HBM] * n,
        out_shape=sem_shapes + [pltpu.HBM(f.shape, f.dtype) for f in fulls],
        input_output_aliases={i: 2 * n_groups + i for i in range(n)},
        compiler_params=pltpu.CompilerParams(has_side_effects=pltpu.SideEffectType.DATAFLOW_SIDE_EFFECTING),
    )(*[pltpu.with_memory_space_constraint(f, pltpu.HBM) for f in fulls])
    sems = [(outs[2 * g], outs[2 * g + 1]) for g in range(n_groups)]
    return sems, list(outs[2 * n_groups:])


def _gather_wait(in_flight, colwise, sems, after, name):
    n = len(in_flight)

    def body(*refs):
        ins = refs[:n]
        send_sems, recv_sems = refs[n], refs[n + 1]
        x, y, c, chips = _position()
        me = 2 * x + y
        for a in range(n):
            for j, chip in enumerate(chips):
                cp = pltpu.make_async_remote_copy(
                    src_ref=_window(ins[a], colwise[a], me, None),
                    dst_ref=_window(ins[a], colwise[a], 2 * chip[0] + chip[1], None),
                    send_sem=send_sems.at[a * 3 + j], recv_sem=recv_sems.at[a * 3 + j],
                    device_id=(chip[0], chip[1], c), device_id_type=MESH_ID)
                cp.wait_send()
                cp.wait_recv()

    operands = list(in_flight) + list(sems)
    in_specs = [HBM] * n + [SEMAPHORES] * 2
    if after is not None:
        operands.append(after)
        in_specs.append(ANY)
    outs = pl.pallas_call(
        body, name=name,
        in_specs=in_specs, out_specs=[HBM] * n,
        out_shape=[pltpu.HBM(f.shape, f.dtype) for f in in_flight],
        input_output_aliases={i: i for i in range(n)},
        compiler_params=pltpu.CompilerParams(has_side_effects=pltpu.SideEffectType.DATAFLOW_SIDE_EFFECTING),
    )(*operands)
    return list(outs)


def _exchange_halves(grads):
    n = len(grads)
    out_shapes = []
    for g, cw in zip(grads, COL_SHARDED):
        L, K, N = g.shape
        out_shapes.append(jax.ShapeDtypeStruct((L, K // 2, N) if cw else (L, K, N // 2), g.dtype))

    def body(*refs):
        g_refs, out = refs[:n], refs[n:2 * n]
        send_sems, recv_sems = refs[2 * n:]
        x, y, c, _ = _position()
        copies = [pltpu.make_async_remote_copy(
            src_ref=_window(g_refs[i], COL_SHARDED[i], None, 1 - c), dst_ref=out[i],
            send_sem=send_sems.at[i], recv_sem=recv_sems.at[i],
            device_id=(x, y, 1 - c), device_id_type=MESH_ID) for i in range(n)]
        for cp in copies:
            cp.start()
        for cp in copies:
            cp.wait()

    return pl.pallas_call(
        body, name="grad_exchange_halves",
        in_specs=[ANY] * n, out_specs=[ANY] * n, out_shape=out_shapes,
        scratch_shapes=[pltpu.SemaphoreType.DMA((n,)), pltpu.SemaphoreType.DMA((n,))],
    )(*grads)


def _scatter_to_owners(chip_sums):
    n = len(chip_sums)
    out_shapes = []
    for g, cw in zip(chip_sums, COL_SHARDED):
        L, hk, hn = g.shape
        out_shapes.append(jax.ShapeDtypeStruct((3, L, hk, hn // 4) if cw else (3, L, hk // 4, hn), g.dtype))

    def body(*refs):
        g_refs, out = refs[:n], refs[n:2 * n]
        send_sems, recv_sems = refs[2 * n:]
        x, y, c, chips = _position()
        copies = [pltpu.make_async_remote_copy(
            src_ref=_window(g_refs[i], COL_SHARDED[i], 2 * chip[0] + chip[1], None), dst_ref=out[i].at[j],
            send_sem=send_sems.at[i * 3 + j], recv_sem=recv_sems.at[i * 3 + j],
            device_id=(chip[0], chip[1], c), device_id_type=MESH_ID)
            for i in range(n) for j, chip in enumerate(chips)]
        for cp in copies:
            cp.start()
        for cp in copies:
            cp.wait()

    return pl.pallas_call(
        body, name="grad_scatter_to_owners",
        in_specs=[ANY] * n, out_specs=[ANY] * n, out_shape=out_shapes,
        scratch_shapes=[pltpu.SemaphoreType.DMA((3 * n,)), pltpu.SemaphoreType.DMA((3 * n,))],
    )(*chip_sums)


def _share_with_sibling(shards):
    n = len(shards)

    def body(*refs):
        out = refs[n:2 * n]
        send_sems, recv_sems = refs[2 * n:]
        x, y, c, _ = _position()

        def copy(i, half):
            win = _window(out[i], COL_SHARDED[i], None, half)
            return pltpu.make_async_remote_copy(
                src_ref=win, dst_ref=win, send_sem=send_sems.at[i], recv_sem=recv_sems.at[i],
                device_id=(x, y, 1 - c), device_id_type=MESH_ID)

        for i in range(n):
            copy(i, c).start()
        for i in range(n):
            copy(i, 1 - c).wait_recv()
        for i in range(n):
            copy(i, c).wait_send()

    return pl.pallas_call(
        body, name="grad_share_with_sibling",
        in_specs=[ANY] * n, out_specs=[ANY] * n,
        out_shape=[jax.ShapeDtypeStruct(s.shape, s.dtype) for s in shards],
        input_output_aliases={i: i for i in range(n)},
        scratch_shapes=[pltpu.SemaphoreType.DMA((n,)), pltpu.SemaphoreType.DMA((n,))],
    )(*shards)


def _all_reduce_small(pack, name):
    R, Cc = pack.shape
    n_dev = 8

    def body(p_ref, o_ref, buf, send_sems, recv_sems):
        x, y, c, _ = _position()
        me = 4 * x + 2 * y + c
        buf[me] = p_ref[...]

        def peer(k):
            px = 1 - x if k & 4 else x
            py = 1 - y if k & 2 else y
            pc = 1 - c if k & 1 else c
            return px, py, pc

        def copy(k, incoming):
            px, py, pc = peer(k)
            slot = (4 * px + 2 * py + pc) if incoming else me
            return pltpu.make_async_remote_copy(
                src_ref=p_ref, dst_ref=buf.at[slot], send_sem=send_sems.at[k], recv_sem=recv_sems.at[k],
                device_id=(px, py, pc), device_id_type=MESH_ID)

        for k in range(1, n_dev):
            copy(k, False).start()
        for k in range(1, n_dev):
            copy(k, True).wait_recv()
        for k in range(1, n_dev):
            copy(k, False).wait_send()
        acc = buf[0]
        for j in range(1, n_dev):
            acc = acc + buf[j]
        o_ref[...] = acc

    vmem = pl.BlockSpec(memory_space=pltpu.VMEM)
    return pl.pallas_call(
        body, name=name,
        in_specs=[vmem], out_specs=vmem, out_shape=jax.ShapeDtypeStruct((R, Cc), F32),
        scratch_shapes=[pltpu.VMEM((n_dev, R, Cc), F32), pltpu.SemaphoreType.DMA((n_dev,)),
                        pltpu.SemaphoreType.DMA((n_dev,))],
    )(pack)


def _local_forward_backward(x2, target2, S, fetch, layers, final_g, tm=512):
    T, D = x2.shape
    C = D // 2
    n_heads = C // GROUP
    n_layers = len(layers)
    weights = {}
    saved = []
    xc = x2
    for li, lw in enumerate(layers):
        if li == 0:
            weights.update(fetch(0, None))
        h1, qkv3, cv3 = _norm_proj(xc, lw["norm1"], weights[li, "w_in"], 0, ((3, C, F32), (3, C, BF16)), tm,
                                   min(C, 512), f"l{li}_norm_in_proj")
        o, lse, mix = _attn_fwd(qkv3, lw["attn_g"], 2, S, n_heads, f"l{li}_attn_fwd")
        if li == 0:
            weights.update(fetch(1, o))
        Fd = weights[li, "ffn_down"].shape[1]
        mix = _mix_conv_fwd(cv3, lw["taps"], lw["conv_g"], mix, S, f"l{li}_mix_conv_fwd")
        x_mid = _proj_residual(mix, weights[li, "w_out"], 0, xc, tm, f"l{li}_out_proj")
        h2, up3 = _norm_proj(x_mid, lw["norm2"], weights[li, "ffn_up"], 0, ((2, Fd, BF16),), tm // 2, 256,
                             f"l{li}_norm_ffn_up")
        act = _ffn_act_fwd(up3, lw["ffn_taps"], S, f"l{li}_ffn_act_fwd")
        x_out = _proj_residual(act.reshape(1, T, Fd), weights[li, "ffn_down"], 0, x_mid, tm, f"l{li}_ffn_down")
        if li + 1 < n_layers:
            weights.update(fetch(li + 2, x_out))
        saved.append(dict(x_in=xc, h1=h1, qkv3=qkv3, cv3=cv3, o=o, lse=lse, mix=mix, x_mid=x_mid, h2=h2, up3=up3,
                          act=act))
        xc = x_out

    dx, d_final_g, loss_part = _final_norm_loss(xc, final_g, target2, tm, "final_norm_loss")

    d_w_in = d_w_out = d_ffn_up = d_ffn_down = None
    small = [None] * n_layers
    for li in reversed(range(n_layers)):
        lw, sv = layers[li], saved[li]
        w_in, w_out, ffn_up, ffn_down = (weights[li, n] for n in ("w_in", "w_out", "ffn_up", "ffn_down"))
        dxb, dact3 = _grad_through_weight(dx, ffn_down, 0, 1, Fd, BF16, tm, 256, f"l{li}_d_act")
        d_ffn_down = _weight_grad(sv["act"].reshape(1, T, Fd), dxb.reshape(1, T, D), Fd // 2, D, 1024,
                                  li, n_layers, d_ffn_down, f"l{li}_d_ffn_down")
        dup3, d_taps_g, d_taps_v = _ffn_act_bwd(sv["up3"], dact3, lw["ffn_taps"], S, f"l{li}_ffn_act_bwd")
        d_ffn_up = _weight_grad(sv["h2"].reshape(1, T, D), dup3, D, Fd // 2, 1024, li, n_layers, d_ffn_up,
                                f"l{li}_d_ffn_up")
        dx_mid, d_norm2 = _grad_through_proj_norm(dup3, ffn_up, 0, sv["x_mid"], lw["norm2"], dx, tm // 2,
                                                  f"l{li}_d_norm2")
        dxmb, dmix3 = _grad_through_weight(dx_mid, w_out, 0, 2, C, F32, tm, min(C, 512), f"l{li}_d_mix")
        d_w_out = _weight_grad(sv["mix"], dxmb.reshape(1, T, D), C, D, 1024, li, n_layers, d_w_out,
                               f"l{li}_d_w_out")
        dproj, d_attn_g = _attn_bwd(sv["qkv3"], sv["o"], sv["lse"], dmix3, lw["attn_g"], 6, S, n_heads,
                                    f"l{li}_attn_bwd")
        dproj, d_taps, d_conv_g = _mix_conv_bwd(sv["cv3"], dmix3, lw["taps"], lw["conv_g"], dproj, S,
                                                f"l{li}_mix_conv_bwd")
        d_w_in = _weight_grad(sv["h1"].reshape(1, T, D), dproj, D, C, 1024, li, n_layers, d_w_in,
                              f"l{li}_d_w_in")
        dx, d_norm1 = _grad_through_proj_norm(dproj, w_in, 0, sv["x_in"], lw["norm1"], dx_mid, tm,
                                              f"l{li}_d_norm1")
        small[li] = dict(norm1=d_norm1, taps=d_taps, attn_g=d_attn_g, conv_g=d_conv_g, norm2=d_norm2,
                         ffn_taps=jnp.concatenate([d_taps_g, d_taps_v], axis=1))
    return loss_part, dx, (d_w_in, d_w_out, d_ffn_up, d_ffn_down), small, d_final_g


SMALL_ORDER = ("norm1", "attn_g", "conv_g", "norm2", "taps", "ffn_taps")


def _pack_small(small, d_final_g):
    parts = [small[li][k].reshape(-1) for li in range(len(small)) for k in SMALL_ORDER] + [d_final_g.reshape(-1)]
    return jnp.concatenate(parts).reshape(-1, LANES)


def _unpack_small(pack, small, d_final_g):
    flat = pack.reshape(-1)
    out, pos = [dict() for _ in small], 0
    for li in range(len(small)):
        for k in SMALL_ORDER:
            n = small[li][k].size
            out[li][k] = flat[pos:pos + n].reshape(small[li][k].shape)
            pos += n
    return out, flat[pos:pos + d_final_g.size]


def kernel(x, norm1_g, w_in, mix_conv_w, attn_out_g, conv_out_g, w_out, norm2_g, ffn_up, ffn_conv_w, ffn_down, final_norm_g, loss_target, m_norm1_g, m_w_in, m_mix_conv_w, m_attn_out_g, m_conv_out_g, m_w_out, m_norm2_g, m_ffn_up, m_ffn_conv_w, m_ffn_down, m_final_norm_g, v_norm1_g, v_w_in, v_mix_conv_w, v_attn_out_g, v_conv_out_g, v_w_out, v_norm2_g, v_ffn_up, v_ffn_conv_w, v_ffn_down, v_final_norm_g):
    Bl, S, D = x.shape
    L = w_in.shape[0]
    T = Bl * S
    shard = 2 * lax.axis_index("x") + lax.axis_index("y")
    where = jnp.stack([shard, lax.axis_index("c")]).astype(jnp.int32)
    big_names = ("w_in", "w_out", "ffn_up", "ffn_down")

    big_shards = dict(zip(big_names, (w_in, w_out, ffn_up, ffn_down)))
    col_of = dict(zip(big_names, COL_SHARDED))
    groups = [[(0, "w_in")], [(0, n) for n in big_names[1:]]] + [[(l, n) for n in big_names] for l in range(1, L)]
    keys = [k for g in groups for k in g]
    sems, in_flight = _gather_start(
        [_cast_into_full(big_shards[n], l, col_of[n], where, f"cast_{n}_{l}") for l, n in keys],
        [col_of[n] for _, n in keys], [len(g) for g in groups])
    in_flight = dict(zip(keys, in_flight))

    def fetch(g, after):
        done = _gather_wait([in_flight[k] for k in groups[g]], [col_of[n] for _, n in groups[g]], sems[g], after,
                            f"gather_wait_{g}")
        return dict(zip(groups[g], done))

    taps_w, ftaps_w = mix_conv_w.shape[2], ffn_conv_w.shape[2]
    taps_full = jnp.zeros((L, 3, 4 * taps_w), F32)
    taps_full = lax.dynamic_update_slice(taps_full, mix_conv_w, (0, 0, shard * taps_w))
    ftaps_full = jnp.zeros((L, 3, 4 * ftaps_w), F32)
    ftaps_full = lax.dynamic_update_slice(ftaps_full, ffn_conv_w, (0, 0, shard * ftaps_w))
    tap_pack = jnp.concatenate([taps_full.reshape(-1), ftaps_full.reshape(-1)]).reshape(-1, LANES)
    tap_pack = _all_reduce_small(tap_pack * 0.5, "all_gather_taps")
    n_taps = taps_full.size
    taps_full = tap_pack.reshape(-1)[:n_taps].reshape(taps_full.shape)
    ftaps_full = tap_pack.reshape(-1)[n_taps:].reshape(ftaps_full.shape)

    layers = [dict(norm1=norm1_g[l:l + 1], taps=taps_full[l], attn_g=attn_out_g[l:l + 1],
                   conv_g=conv_out_g[l:l + 1], norm2=norm2_g[l:l + 1], ffn_taps=ftaps_full[l]) for l in range(L)]

    loss_part, dx, big, small, d_final_g = _local_forward_backward(
        x.reshape(T, D), loss_target.reshape(T, D), S, fetch, layers, final_norm_g.reshape(1, D))
    loss = lax.psum(loss_part[0, 0], ("x", "y", "c"))

    others = _exchange_halves(big)
    chip_sums = [_chip_sum(g, o, cw, where, f"chip_sum_{n}")
                 for g, o, cw, n in zip(big, others, COL_SHARDED, big_names)]
    received = _scatter_to_owners(chip_sums)
    g_big = [_owner_sum(s, r, cw, where, f"owner_sum_{n}")
             for s, r, cw, n in zip(chip_sums, received, COL_SHARDED, big_names)]
    g_big = _share_with_sibling(g_big)

    pack = _all_reduce_small(_pack_small(small, d_final_g), "all_reduce_small_grads")
    g_small, g_final = _unpack_small(pack, small, d_final_g)

    def stacked(key):
        return jnp.stack([g_small[l][key].reshape(g_small[l][key].shape[-2:] if key.endswith("taps") else (-1,))
                          for l in range(L)])

    g_norm1, g_attn, g_conv, g_norm2 = stacked("norm1"), stacked("attn_g"), stacked("conv_g"), stacked("norm2")
    g_taps = lax.dynamic_slice(stacked("taps"), (0, 0, shard * taps_w), (L, 3, taps_w))
    g_ftaps = lax.dynamic_slice(stacked("ffn_taps"), (0, 0, shard * ftaps_w), (L, 3, ftaps_w))

    grads_out = dict(norm1_g=g_norm1, w_in=g_big[0], mix_conv_w=g_taps, attn_out_g=g_attn, conv_out_g=g_conv,
                     w_out=g_big[1], norm2_g=g_norm2, ffn_up=g_big[2], ffn_conv_w=g_ftaps, ffn_down=g_big[3],
                     final_norm_g=g_final)
    weights = dict(norm1_g=norm1_g, w_in=w_in, mix_conv_w=mix_conv_w, attn_out_g=attn_out_g, conv_out_g=conv_out_g,
                   w_out=w_out, norm2_g=norm2_g, ffn_up=ffn_up, ffn_conv_w=ffn_conv_w, ffn_down=ffn_down,
                   final_norm_g=final_norm_g)
    ms = dict(norm1_g=m_norm1_g, w_in=m_w_in, mix_conv_w=m_mix_conv_w, attn_out_g=m_attn_out_g,
              conv_out_g=m_conv_out_g, w_out=m_w_out, norm2_g=m_norm2_g, ffn_up=m_ffn_up, ffn_conv_w=m_ffn_conv_w,
              ffn_down=m_ffn_down, final_norm_g=m_final_norm_g)
    vs = dict(norm1_g=v_norm1_g, w_in=v_w_in, mix_conv_w=v_mix_conv_w, attn_out_g=v_attn_out_g,
              conv_out_g=v_conv_out_g, w_out=v_w_out, norm2_g=v_norm2_g, ffn_up=v_ffn_up, ffn_conv_w=v_ffn_conv_w,
              ffn_down=v_ffn_down, final_norm_g=v_final_norm_g)
    names = list(weights)
    small_names = [n for n in names if n not in big_names]
    delta, new_m, new_v = {}, {}, {}
    for n in big_names:
        shp = weights[n].shape
        two_d = (shp[0] * shp[1], shp[2])
        d_, m_, v_ = _adamw(weights[n].reshape(two_d), grads_out[n].reshape(two_d), ms[n].reshape(two_d),
                            vs[n].reshape(two_d), f"adamw_{n}")
        delta[n], new_m[n], new_v[n] = d_.reshape(shp), m_.reshape(shp), v_.reshape(shp)

    def packed(tree):
        return jnp.concatenate([tree[n].reshape(-1) for n in small_names]).reshape(-1, LANES)

    d_, m_, v_ = _adamw(packed(weights), packed(grads_out), packed(ms), packed(vs), "adamw_small")
    pos = 0
    for n in small_names:
        size, shp = weights[n].size, weights[n].shape
        delta[n] = d_.reshape(-1)[pos:pos + size].reshape(shp)
        new_m[n] = m_.reshape(-1)[pos:pos + size].reshape(shp)
        new_v[n] = v_.reshape(-1)[pos:pos + size].reshape(shp)
        pos += size

    return (loss, dx.reshape(Bl, S, D), *[grads_out[n] for n in names], *[delta[n] for n in names],
            *[new_m[n] for n in names], *[new_v[n] for n in names])
```

```python
import functools
import math

import jax
import jax.numpy as jnp
from jax import lax
from jax.experimental import pallas as pl
from jax.experimental.pallas import tpu as pltpu

F32 = jnp.float32
BF16 = jnp.bfloat16
EPS = 1e-6
GROUP = 64
LANES = 128
BAND = 128
DILATIONS = (1, 4, 16)
NEG = -1e30
MIB = 1024 * 1024
MESH_ID = pl.DeviceIdType.MESH

ADAM_LR = 0.001
ADAM_B1 = 0.9
ADAM_B2 = 0.999
ADAM_EPS = 1e-08
ADAM_WD = 0.01
ADAM_STEP = 10


ANY = pl.BlockSpec(memory_space=pl.ANY)


def _params(sem=None, vmem_mb=48):
    return pltpu.CompilerParams(dimension_semantics=sem, vmem_limit_bytes=vmem_mb * MIB)


def _nt(a, b):
    return lax.dot_general(a, b, (((1,), (1,)), ((), ())), preferred_element_type=F32)


def _tn(a, b):
    return lax.dot_general(a, b, (((0,), (0,)), ((), ())), preferred_element_type=F32)


def _seg_sum(x, is_a):
    s_a = jnp.sum(jnp.where(is_a, x, 0.0), axis=-1, keepdims=True)
    s_b = jnp.sum(jnp.where(is_a, 0.0, x), axis=-1, keepdims=True)
    return jnp.where(is_a, s_a, s_b)


def _lane_is_a():
    return lax.broadcasted_iota(jnp.int32, (1, LANES), 1) < GROUP


def _norm_proj(x, g, w3, layer, groups, tm, chunk, name):
    T, D = x.shape
    N = w3.shape[2]
    assert sum(p * c for p, c, _ in groups) == N and T % tm == 0

    def body(x_ref, g_ref, w_ref, h_ref, *out_refs):
        xv = x_ref[...]
        rstd = lax.rsqrt(jnp.mean(xv * xv, axis=-1, keepdims=True) + EPS)
        h = ((xv * rstd) * g_ref[...]).astype(BF16)
        h_ref[...] = h
        col = 0
        for (pieces, width, dtype), o_ref in zip(groups, out_refs):
            for p in range(pieces):
                for c0 in range(0, width, chunk):
                    acc = jnp.dot(h, w_ref[:, col + c0:col + c0 + chunk], preferred_element_type=F32)
                    o_ref[p, :, c0:c0 + chunk] = acc.astype(dtype)
                col += width

    out_shape = [jax.ShapeDtypeStruct((T, D), BF16)]
    out_specs = [pl.BlockSpec((tm, D), lambda i: (i, 0))]
    for pieces, width, dtype in groups:
        assert width % chunk == 0
        out_shape.append(jax.ShapeDtypeStruct((pieces, T, width), dtype))
        out_specs.append(pl.BlockSpec((pieces, tm, width), lambda i: (0, i, 0)))
    return pl.pallas_call(
        body, grid=(T // tm,), name=name,
        in_specs=[pl.BlockSpec((tm, D), lambda i: (i, 0)),
                  pl.BlockSpec((1, D), lambda i: (0, 0)),
                  pl.BlockSpec((None, D, N), lambda i: (layer, 0, 0))],
        out_specs=out_specs, out_shape=out_shape,
        compiler_params=_params(("parallel",), 56),
    )(x, g, w3)


def _proj_residual(pieces3, w3, layer, x, tm, name):
    P, T, C = pieces3.shape
    D = w3.shape[2]

    def body(a_ref, w_ref, x_ref, o_ref):
        acc = x_ref[...]
        for p in range(P):
            acc = acc + jnp.dot(a_ref[p], w_ref[p * C:(p + 1) * C, :], preferred_element_type=F32)
        o_ref[...] = acc

    return pl.pallas_call(
        body, grid=(T // tm,), name=name,
        in_specs=[pl.BlockSpec((P, tm, C), lambda i: (0, i, 0)),
                  pl.BlockSpec((None, P * C, D), lambda i: (layer, 0, 0)),
                  pl.BlockSpec((tm, D), lambda i: (i, 0))],
        out_specs=pl.BlockSpec((tm, D), lambda i: (i, 0)),
        out_shape=jax.ShapeDtypeStruct((T, D), F32),
        compiler_params=_params(("parallel",)),
    )(pieces3, w3, x)


def _grad_through_weight(dy, w3, layer, pieces, width, out_dtype, tm, chunk, name):
    T, D = dy.shape

    def body(dy_ref, w_ref, dyb_ref, o_ref):
        dyb = dy_ref[...].astype(BF16)
        dyb_ref[...] = dyb
        for p in range(pieces):
            for c0 in range(0, width, chunk):
                r0 = p * width + c0
                o_ref[p, :, c0:c0 + chunk] = _nt(dyb, w_ref[r0:r0 + chunk, :]).astype(out_dtype)

    return pl.pallas_call(
        body, grid=(T // tm,), name=name,
        in_specs=[pl.BlockSpec((tm, D), lambda i: (i, 0)),
                  pl.BlockSpec((None, pieces * width, D), lambda i: (layer, 0, 0))],
        out_specs=[pl.BlockSpec((tm, D), lambda i: (i, 0)),
                   pl.BlockSpec((pieces, tm, width), lambda i: (0, i, 0))],
        out_shape=[jax.ShapeDtypeStruct((T, D), BF16),
                   jax.ShapeDtypeStruct((pieces, T, width), out_dtype)],
        compiler_params=_params(("parallel",)),
    )(dy, w3)


def _grad_through_proj_norm(dp3, w3, layer, x, g, dx_in, tm, name):
    P, T, C = dp3.shape
    D = w3.shape[1]

    def body(dp_ref, w_ref, x_ref, g_ref, dxin_ref, dx_ref, dg_ref):
        dh = _nt(dp_ref[0], w_ref[:, 0:C])
        for p in range(1, P):
            dh = dh + _nt(dp_ref[p], w_ref[:, p * C:(p + 1) * C])
        xv = x_ref[...]
        rstd = lax.rsqrt(jnp.mean(xv * xv, axis=-1, keepdims=True) + EPS)
        xn = xv * rstd
        a = dh * g_ref[...]
        dx_ref[...] = dxin_ref[...] + rstd * (a - xn * jnp.mean(a * xn, axis=-1, keepdims=True))
        part = jnp.sum(dh * xn, axis=0, keepdims=True)

        @pl.when(pl.program_id(0) == 0)
        def _():
            dg_ref[...] = part

        @pl.when(pl.program_id(0) != 0)
        def _():
            dg_ref[...] += part

    return pl.pallas_call(
        body, grid=(T // tm,), name=name,
        in_specs=[pl.BlockSpec((P, tm, C), lambda i: (0, i, 0)),
                  pl.BlockSpec((None, D, P * C), lambda i: (layer, 0, 0)),
                  pl.BlockSpec((tm, D), lambda i: (i, 0)),
                  pl.BlockSpec((1, D), lambda i: (0, 0)),
                  pl.BlockSpec((tm, D), lambda i: (i, 0))],
        out_specs=[pl.BlockSpec((tm, D), lambda i: (i, 0)),
                   pl.BlockSpec((1, D), lambda i: (0, 0))],
        out_shape=[jax.ShapeDtypeStruct((T, D), F32), jax.ShapeDtypeStruct((1, D), F32)],
        compiler_params=_params(("arbitrary",), 56),
    )(dp3, w3, x, g, dx_in)


def _weight_grad(a3, g3, ta, tg, tt, layer, n_layers, prev, name):
    PA, T, CA = a3.shape
    PG, _, CG = g3.shape
    na, ng, nt = CA // ta, CG // tg, T // tt
    assert CA % ta == 0 and CG % tg == 0 and T % tt == 0

    def body(a_ref, g_ref, *rest):
        o_ref, acc_ref = rest[-2:]
        t = pl.program_id(2)
        part = _tn(a_ref[...], g_ref[...])

        @pl.when(t == 0)
        def _():
            acc_ref[...] = part

        @pl.when(t != 0)
        def _():
            acc_ref[...] += part

        @pl.when(t == nt - 1)
        def _():
            o_ref[...] = acc_ref[...].astype(o_ref.dtype)

    in_specs = [pl.BlockSpec((None, tt, ta), lambda i, j, t: (i // na, t, i % na)),
                pl.BlockSpec((None, tt, tg), lambda i, j, t: (j // ng, t, j % ng))]
    operands = [a3, g3]
    if prev is not None:
        in_specs.append(pl.BlockSpec(memory_space=pl.ANY))
        operands.append(prev)
    return pl.pallas_call(
        body, grid=(PA * na, PG * ng, nt), name=name,
        in_specs=in_specs,
        out_specs=pl.BlockSpec((None, ta, tg), lambda i, j, t: (layer, i, j)),
        out_shape=jax.ShapeDtypeStruct((n_layers, PA * CA, PG * CG), BF16),
        scratch_shapes=[pltpu.VMEM((ta, tg), F32)],
        input_output_aliases={} if prev is None else {2: 0},
        compiler_params=_params(("parallel", "parallel", "arbitrary"), 56),
    )(*operands)


def _bias_tables(bm_ref, pair, n_heads):
    ii = lax.broadcasted_iota(jnp.int32, (BAND, 2 * BAND), 0)
    jj = lax.broadcasted_iota(jnp.int32, (BAND, 2 * BAND), 1)
    dist = BAND + ii - jj
    valid = (dist >= 0) & (dist <= BAND)
    distf = dist.astype(F32)
    for hh in range(2):
        head = (2 * pair + hh + 1).astype(F32)
        slope = jnp.exp(jnp.full((1, 1), -8.0 / n_heads * math.log(2.0), F32) * head)
        for bi, d in enumerate(DILATIONS):
            bm_ref[hh * len(DILATIONS) + bi] = jnp.where(valid, -(slope * d) * distf, NEG)


def _gather_residues(dst_ref, src, d, S, convert):
    L = S // d
    for r in range(d):
        rows = pl.ds(r, L, stride=d) if d > 1 else slice(None)
        dst_ref[r * L:(r + 1) * L, :] = convert(src(rows))


def _block_rows(t, d, S):
    nb = S // (BAND * d)
    n = t % nb
    has_prev = jnp.minimum(n, 1)
    cur = pl.ds(pl.multiple_of(t * BAND, BAND), BAND)
    prev = pl.ds(pl.multiple_of((t - has_prev) * BAND, BAND), BAND)
    return cur, prev, has_prev


def _first_block_penalty(has_prev):
    jrow = lax.broadcasted_iota(jnp.int32, (1, 2 * BAND), 1)
    pen = jnp.where(has_prev == 0, NEG, 0.0).astype(F32)
    return jnp.where(jrow < BAND, pen, 0.0)


def _attn_fwd(qkv3, gain, mix_shape_pieces, S, n_heads, name):
    _, T, C = qkv3.shape
    B, P = T // S, C // LANES
    NBLK = S // BAND
    scale = GROUP ** -0.5
    nbr = len(DILATIONS)
    RC = 256

    def body(qkv_ref, g_ref, o_ref, lse_ref, an_ref, qs, ks, vs, op, mp, lp, ob, mb, lb, bm):
        pair = pl.program_id(1)
        is_a = _lane_is_a()
        _bias_tables(bm, pair, n_heads)

        for bi, d in enumerate(DILATIONS):
            nb = S // (BAND * d)
            _gather_residues(qs, lambda rows: qkv_ref.at[0][rows, :], d, S, lambda v: (v * scale).astype(BF16))
            _gather_residues(ks, lambda rows: qkv_ref.at[1][rows, :], d, S, lambda v: v.astype(BF16))
            _gather_residues(vs, lambda rows: qkv_ref.at[2][rows, :], d, S, lambda v: v.astype(BF16))
            o_dst, m_dst, l_dst = (ob.at[bi], mb.at[bi], lb.at[bi]) if d == 1 else (op, mp, lp)

            def block(t, carry, bi=bi, d=d, nb=nb, o_dst=o_dst, m_dst=m_dst, l_dst=l_dst):
                cur, prev, has_prev = _block_rows(t, d, S)
                q = qs[cur, :]
                kc = jnp.concatenate([ks[prev, :], ks[cur, :]], axis=0)
                vc = jnp.concatenate([vs[prev, :], vs[cur, :]], axis=0)
                pen = _first_block_penalty(has_prev)
                outs = []
                for hh in range(2):
                    qm = jnp.where(is_a == (hh == 0), q, jnp.zeros_like(q))
                    s = _nt(qm, kc) + bm[hh * nbr + bi] + pen
                    m = jnp.max(s, axis=-1, keepdims=True)
                    e = jnp.exp(s - m)
                    l = jnp.sum(e, axis=-1, keepdims=True)
                    pv = jnp.dot(e.astype(BF16), vc, preferred_element_type=F32)
                    outs.append((pv, m, l))
                o_dst[cur, :] = jnp.where(is_a, outs[0][0], outs[1][0])
                m_dst[cur, :] = jnp.where(is_a, outs[0][1], outs[1][1])
                l_dst[cur, :] = jnp.where(is_a, outs[0][2], outs[1][2])
                return carry

            lax.fori_loop(0, NBLK, block, 0, unroll=8)
            if d > 1:
                L = S // d
                for r in range(d):
                    rows = pl.ds(r, L, stride=d)
                    ob.at[bi][rows, :] = op[r * L:(r + 1) * L, :]
                    mb.at[bi][rows, :] = mp[r * L:(r + 1) * L, :]
                    lb.at[bi][rows, :] = lp[r * L:(r + 1) * L, :]

        def finish(ci, carry):
            rs = pl.ds(pl.multiple_of(ci * RC, RC), RC)
            ms = [mb[bi, rs, :] for bi in range(nbr)]
            mmax = functools.reduce(jnp.maximum, ms)
            ws = [jnp.exp(m - mmax) for m in ms]
            num = sum(ob[bi, rs, :] * ws[bi] for bi in range(nbr))
            den = sum(lb[bi, rs, :] * ws[bi] for bi in range(nbr))
            o = num / den
            o_ref[rs, :] = o
            lse_ref[rs, :] = mmax + jnp.log(den)
            rstd = lax.rsqrt(_seg_sum(o * o, is_a) * (1.0 / GROUP) + EPS)
            an_ref[rs, :] = ((o * rstd) * g_ref[...]).astype(BF16)
            return carry

        lax.fori_loop(0, S // RC, finish, 0)

    seq = pl.BlockSpec((S, LANES), lambda b, p: (b, p))
    return pl.pallas_call(
        body, grid=(B, P), name=name,
        in_specs=[pl.BlockSpec((3, S, LANES), lambda b, p: (0, b, p)),
                  pl.BlockSpec((1, LANES), lambda b, p: (0, p))],
        out_specs=[seq, seq, pl.BlockSpec((None, S, LANES), lambda b, p: (0, b, p))],
        out_shape=[jax.ShapeDtypeStruct((T, C), F32), jax.ShapeDtypeStruct((T, C), F32),
                   jax.ShapeDtypeStruct((mix_shape_pieces, T, C), BF16)],
        scratch_shapes=[pltpu.VMEM((S, LANES), BF16)] * 3 + [pltpu.VMEM((S, LANES), F32)] * 3
        + [pltpu.VMEM((nbr, S, LANES), F32)] * 3 + [pltpu.VMEM((2 * nbr, BAND, 2 * BAND), F32)],
        compiler_params=_params(("parallel", "parallel")),
    )(qkv3, gain)


def _attn_bwd(qkv3, o, lse, dmix3, gain, dproj_pieces, S, n_heads, name):
    _, T, C = qkv3.shape
    B, P = T // S, C // LANES
    NBLK = S // BAND
    scale = GROUP ** -0.5
    nbr = len(DILATIONS)
    RC = 256

    def body(qkv_ref, o_ref, lse_ref, dn_ref, g_ref, dqkv_ref, dg_ref,
             do_n, dd_n, qs, ks, vs, dos, lses, dds, dqp, dkp, dvp, dqn, dkn, dvn, bm):
        pair = pl.program_id(0)
        b = pl.program_id(1)
        is_a = _lane_is_a()
        _bias_tables(bm, pair, n_heads)

        def prologue(ci, dg_acc):
            rs = pl.ds(pl.multiple_of(ci * RC, RC), RC)
            ov = o_ref[rs, :]
            dn = dn_ref[rs, :]
            rstd = lax.rsqrt(_seg_sum(ov * ov, is_a) * (1.0 / GROUP) + EPS)
            on = ov * rstd
            a = dn * g_ref[...]
            do = rstd * (a - on * (_seg_sum(a * on, is_a) * (1.0 / GROUP)))
            do_n[rs, :] = do
            dd_n[rs, :] = _seg_sum(do * ov, is_a)
            zero = jnp.zeros((RC, LANES), F32)
            dqn[rs, :] = zero
            dkn[rs, :] = zero
            dvn[rs, :] = zero
            return dg_acc + jnp.sum(dn * on, axis=0, keepdims=True)

        dg_part = lax.fori_loop(0, S // RC, prologue, jnp.zeros((1, LANES), F32))

        @pl.when(b == 0)
        def _():
            dg_ref[...] = dg_part

        @pl.when(b != 0)
        def _():
            dg_ref[...] += dg_part

        for bi, d in enumerate(DILATIONS):
            nb = S // (BAND * d)
            L = S // d
            _gather_residues(qs, lambda rows: qkv_ref.at[0][rows, :], d, S, lambda v: (v * scale).astype(BF16))
            _gather_residues(ks, lambda rows: qkv_ref.at[1][rows, :], d, S, lambda v: v.astype(BF16))
            _gather_residues(vs, lambda rows: qkv_ref.at[2][rows, :], d, S, lambda v: v.astype(BF16))
            _gather_residues(dos, lambda rows: do_n[rows, :], d, S, lambda v: v.astype(BF16))
            _gather_residues(lses, lambda rows: lse_ref[rows, :], d, S, lambda v: v)
            _gather_residues(dds, lambda rows: dd_n[rows, :], d, S, lambda v: v)
            dkp[...] = jnp.zeros((S, LANES), F32)
            dvp[...] = jnp.zeros((S, LANES), F32)

            def block(t, carry, bi=bi, d=d, nb=nb):
                cur, prev, has_prev = _block_rows(t, d, S)
                q = qs[cur, :]
                do = dos[cur, :]
                lse_t = lses[cur, :]
                dd_t = dds[cur, :]
                if nb > 1:
                    kc = jnp.concatenate([ks[prev, :], ks[cur, :]], axis=0)
                    vc = jnp.concatenate([vs[prev, :], vs[cur, :]], axis=0)
                    pen = _first_block_penalty(has_prev)
                else:
                    kc, vc = ks[cur, :], vs[cur, :]
                dq = None
                dk = None
                dv = None
                for hh in range(2):
                    mine = is_a == (hh == 0)
                    qm = jnp.where(mine, q, jnp.zeros_like(q))
                    dom = jnp.where(mine, do, jnp.zeros_like(do))
                    c0 = hh * GROUP
                    s = _nt(qm, kc)
                    if nb > 1:
                        s = s + bm[hh * nbr + bi] + pen
                    else:
                        s = s + bm[hh * nbr + bi, :, BAND:2 * BAND]
                    p = jnp.exp(s - lse_t[:, c0:c0 + 1])
                    dp = _nt(dom, vc)
                    ds = (p * (dp - dd_t[:, c0:c0 + 1])).astype(BF16)
                    pb = p.astype(BF16)
                    dq_h = jnp.dot(ds, kc, preferred_element_type=F32)
                    dk_h = _tn(ds, qm)
                    dv_h = _tn(pb, dom)
                    dq = dq_h if dq is None else jnp.where(is_a, dq, dq_h)
                    dk = dk_h if dk is None else dk + dk_h
                    dv = dv_h if dv is None else dv + dv_h
                dqp[cur, :] = dq
                if nb > 1:
                    dkp[prev, :] += dk[0:BAND, :]
                    dvp[prev, :] += dv[0:BAND, :]
                    dkp[cur, :] += dk[BAND:2 * BAND, :]
                    dvp[cur, :] += dv[BAND:2 * BAND, :]
                else:
                    dkp[cur, :] += dk
                    dvp[cur, :] += dv
                return carry

            lax.fori_loop(0, NBLK, block, 0, unroll=4)
            for r in range(d):
                rows = pl.ds(r, L, stride=d) if d > 1 else slice(None)
                dqn[rows, :] += dqp[r * L:(r + 1) * L, :]
                dkn[rows, :] += dkp[r * L:(r + 1) * L, :]
                dvn[rows, :] += dvp[r * L:(r + 1) * L, :]

        dqkv_ref[0] = (dqn[...] * scale).astype(BF16)
        dqkv_ref[1] = dkn[...].astype(BF16)
        dqkv_ref[2] = dvn[...].astype(BF16)

    seq = pl.BlockSpec((S, LANES), lambda p, b: (b, p))
    f32_seq = pltpu.VMEM((S, LANES), F32)
    bf_seq = pltpu.VMEM((S, LANES), BF16)
    return pl.pallas_call(
        body, grid=(P, B), name=name,
        in_specs=[pl.BlockSpec((3, S, LANES), lambda p, b: (0, b, p)), seq, seq,
                  pl.BlockSpec((None, S, LANES), lambda p, b: (0, b, p)),
                  pl.BlockSpec((1, LANES), lambda p, b: (0, p))],
        out_specs=[pl.BlockSpec((3, S, LANES), lambda p, b: (0, b, p)),
                   pl.BlockSpec((1, LANES), lambda p, b: (0, p))],
        out_shape=[jax.ShapeDtypeStruct((dproj_pieces, T, C), BF16), jax.ShapeDtypeStruct((1, C), F32)],
        scratch_shapes=[f32_seq, f32_seq, bf_seq, bf_seq, bf_seq, bf_seq, f32_seq, f32_seq,
                        f32_seq, f32_seq, f32_seq, f32_seq, f32_seq, f32_seq,
                        pltpu.VMEM((2 * nbr, BAND, 2 * BAND), F32)],
        compiler_params=_params(("parallel", "arbitrary")),
    )(qkv3, o, lse, dmix3, gain)


def _delay(x, k, row):
    return jnp.where(row >= k, pltpu.roll(x, k, 0), 0.0)


def _advance(x, k, row, S):
    return jnp.where(row < S - k, pltpu.roll(x, S - k, 0), 0.0)


def _conv3(x, w, row):
    return (w[0:1, :] * _delay(x, 2, row) + w[1:2, :] * _delay(x, 1, row)) + w[2:3, :] * x


def _conv3_grads(dz, x, w, row, S):
    dx = (w[2:3, :] * dz + w[1:2, :] * _advance(dz, 1, row, S)) + w[0:1, :] * _advance(dz, 2, row, S)
    dw = jnp.concatenate([jnp.sum(dz * _delay(x, 2, row), axis=0, keepdims=True),
                          jnp.sum(dz * _delay(x, 1, row), axis=0, keepdims=True),
                          jnp.sum(dz * x, axis=0, keepdims=True)], axis=0)
    return dx, dw


def _mix_conv_fwd(cv3, taps, gain, mix, S, name):
    _, T, C = cv3.shape
    B, P = T // S, C // LANES

    def body(cv_ref, w_ref, g_ref, mix_hbm, y_ref):
        del mix_hbm
        row = lax.broadcasted_iota(jnp.int32, (S, 1), 0)
        is_a = _lane_is_a()
        gb = cv_ref[0].astype(F32)
        c = cv_ref[1].astype(F32) * cv_ref[2].astype(F32)
        y = gb * _conv3(c, w_ref[...], row)
        rstd = lax.rsqrt(_seg_sum(y * y, is_a) * (1.0 / GROUP) + EPS)
        y_ref[...] = ((y * rstd) * g_ref[...]).astype(BF16)

    return pl.pallas_call(
        body, grid=(B, P), name=name,
        in_specs=[pl.BlockSpec((3, S, LANES), lambda b, p: (0, b, p)),
                  pl.BlockSpec((3, LANES), lambda b, p: (0, p)),
                  pl.BlockSpec((1, LANES), lambda b, p: (0, p)),
                  pl.BlockSpec(memory_space=pl.ANY)],
        out_specs=pl.BlockSpec((None, S, LANES), lambda b, p: (1, b, p)),
        out_shape=jax.ShapeDtypeStruct(mix.shape, mix.dtype),
        input_output_aliases={3: 0},
        compiler_params=_params(("parallel", "parallel")),
    )(cv3, taps, gain, mix)


def _mix_conv_bwd(cv3, dmix3, taps, gain, dproj, S, name):
    _, T, C = cv3.shape
    B, P = T // S, C // LANES

    def body(cv_ref, dn_ref, w_ref, g_ref, dproj_hbm, dcv_ref, dw_ref, dg_ref):
        del dproj_hbm
        b = pl.program_id(1)
        row = lax.broadcasted_iota(jnp.int32, (S, 1), 0)
        is_a = _lane_is_a()
        w = w_ref[...]
        gb = cv_ref[0].astype(F32)
        gc = cv_ref[1].astype(F32)
        u = cv_ref[2].astype(F32)
        c = gc * u
        z = _conv3(c, w, row)
        y = gb * z
        rstd = lax.rsqrt(_seg_sum(y * y, is_a) * (1.0 / GROUP) + EPS)
        yn = y * rstd
        dn = dn_ref[...]
        a = dn * g_ref[...]
        dy = rstd * (a - yn * (_seg_sum(a * yn, is_a) * (1.0 / GROUP)))
        dg = jnp.sum(dn * yn, axis=0, keepdims=True)
        dc, dw = _conv3_grads(dy * gb, c, w, row, S)
        dcv_ref[0] = (dy * z).astype(BF16)
        dcv_ref[1] = (dc * u).astype(BF16)
        dcv_ref[2] = (dc * gc).astype(BF16)

        @pl.when(b == 0)
        def _():
            dw_ref[...] = dw
            dg_ref[...] = dg

        @pl.when(b != 0)
        def _():
            dw_ref[...] += dw
            dg_ref[...] += dg

    return pl.pallas_call(
        body, grid=(P, B), name=name,
        in_specs=[pl.BlockSpec((3, S, LANES), lambda p, b: (0, b, p)),
                  pl.BlockSpec((None, S, LANES), lambda p, b: (1, b, p)),
                  pl.BlockSpec((3, LANES), lambda p, b: (0, p)),
                  pl.BlockSpec((1, LANES), lambda p, b: (0, p)),
                  pl.BlockSpec(memory_space=pl.ANY)],
        out_specs=[pl.BlockSpec((3, S, LANES), lambda p, b: (1, b, p)),
                   pl.BlockSpec((3, LANES), lambda p, b: (0, p)),
                   pl.BlockSpec((1, LANES), lambda p, b: (0, p))],
        out_shape=[jax.ShapeDtypeStruct(dproj.shape, dproj.dtype),
                   jax.ShapeDtypeStruct((3, C), F32), jax.ShapeDtypeStruct((1, C), F32)],
        input_output_aliases={4: 0},
        compiler_params=_params(("parallel", "arbitrary")),
    )(cv3, dmix3, taps, gain, dproj)


def _sigmoid(x):
    return 1.0 / (1.0 + jnp.exp(-x))


def _ffn_act_fwd(up3, taps, S, name):
    _, T, Fd = up3.shape
    B, P = T // S, Fd // LANES

    def body(up_ref, wg_ref, wv_ref, act_ref):
        row = lax.broadcasted_iota(jnp.int32, (S, 1), 0)
        cg = _conv3(up_ref[0].astype(F32), wg_ref[...], row)
        cv = _conv3(up_ref[1].astype(F32), wv_ref[...], row)
        act_ref[...] = ((cg * _sigmoid(cg)) * cv).astype(BF16)

    return pl.pallas_call(
        body, grid=(B, P), name=name,
        in_specs=[pl.BlockSpec((2, S, LANES), lambda b, p: (0, b, p)),
                  pl.BlockSpec((3, LANES), lambda b, p: (0, p)),
                  pl.BlockSpec((3, LANES), lambda b, p: (0, P + p))],
        out_specs=pl.BlockSpec((S, LANES), lambda b, p: (b, p)),
        out_shape=jax.ShapeDtypeStruct((T, Fd), BF16),
        compiler_params=_params(("parallel", "parallel")),
    )(up3, taps, taps)


def _ffn_act_bwd(up3, dact3, taps, S, name):
    _, T, Fd = up3.shape
    B, P = T // S, Fd // LANES

    def body(up_ref, da_ref, wg_ref, wv_ref, dup_ref, dwg_ref, dwv_ref):
        b = pl.program_id(1)
        row = lax.broadcasted_iota(jnp.int32, (S, 1), 0)
        ug = up_ref[0].astype(F32)
        uv = up_ref[1].astype(F32)
        wg = wg_ref[...]
        wv = wv_ref[...]
        cg = _conv3(ug, wg, row)
        cv = _conv3(uv, wv, row)
        sg = _sigmoid(cg)
        da = da_ref[...].astype(F32)
        dcg = (da * cv) * (sg * (1.0 + cg * (1.0 - sg)))
        dcv = da * (cg * sg)
        dug, dwg = _conv3_grads(dcg, ug, wg, row, S)
        duv, dwv = _conv3_grads(dcv, uv, wv, row, S)
        dup_ref[0] = dug.astype(BF16)
        dup_ref[1] = duv.astype(BF16)

        @pl.when(b == 0)
        def _():
            dwg_ref[...] = dwg
            dwv_ref[...] = dwv

        @pl.when(b != 0)
        def _():
            dwg_ref[...] += dwg
            dwv_ref[...] += dwv

    tap_out = pl.BlockSpec((3, LANES), lambda p, b: (0, p))
    return pl.pallas_call(
        body, grid=(P, B), name=name,
        in_specs=[pl.BlockSpec((2, S, LANES), lambda p, b: (0, b, p)),
                  pl.BlockSpec((None, S, LANES), lambda p, b: (0, b, p)),
                  pl.BlockSpec((3, LANES), lambda p, b: (0, p)),
                  pl.BlockSpec((3, LANES), lambda p, b: (0, P + p))],
        out_specs=[pl.BlockSpec((2, S, LANES), lambda p, b: (0, b, p)), tap_out, tap_out],
        out_shape=[jax.ShapeDtypeStruct((2, T, Fd), BF16),
                   jax.ShapeDtypeStruct((3, Fd), F32), jax.ShapeDtypeStruct((3, Fd), F32)],
        compiler_params=_params(("parallel", "arbitrary")),
    )(up3, dact3, taps, taps)


def _final_norm_loss(x, g, target, tm, name):
    T, D = x.shape

    def body(x_ref, g_ref, t_ref, dx_ref, dg_ref, loss_ref):
        xv = x_ref[...]
        rstd = lax.rsqrt(jnp.mean(xv * xv, axis=-1, keepdims=True) + EPS)
        xn = xv * rstd
        err = xn * g_ref[...] - t_ref[...]
        part = 0.5 * jnp.sum(jnp.mean(err * err, axis=-1, keepdims=True), axis=0, keepdims=True)
        dy = err * (1.0 / D)
        a = dy * g_ref[...]
        dx_ref[...] = rstd * (a - xn * jnp.mean(a * xn, axis=-1, keepdims=True))
        dg = jnp.sum(dy * xn, axis=0, keepdims=True)
        lpart = jnp.broadcast_to(part, (1, LANES))

        @pl.when(pl.program_id(0) == 0)
        def _():
            dg_ref[...] = dg
            loss_ref[...] = lpart

        @pl.when(pl.program_id(0) != 0)
        def _():
            dg_ref[...] += dg
            loss_ref[...] += lpart

    row = pl.BlockSpec((tm, D), lambda i: (i, 0))
    return pl.pallas_call(
        body, grid=(T // tm,), name=name,
        in_specs=[row, pl.BlockSpec((1, D), lambda i: (0, 0)), row],
        out_specs=[row, pl.BlockSpec((1, D), lambda i: (0, 0)), pl.BlockSpec((1, LANES), lambda i: (0, 0))],
        out_shape=[jax.ShapeDtypeStruct((T, D), F32), jax.ShapeDtypeStruct((1, D), F32),
                   jax.ShapeDtypeStruct((1, LANES), F32)],
        compiler_params=_params(("arbitrary",)),
    )(x, g, target)


def _row_tile(rows, cols, budget_elems=512 * 1024):
    tr = rows
    while tr * cols > budget_elems and tr % 32 == 0:
        tr //= 2
    return tr


def _prefetch_call(body, grid, in_specs, out_specs, out_shape, name, sem, aliases=None):
    return pl.pallas_call(
        body, name=name, out_shape=out_shape,
        grid_spec=pltpu.PrefetchScalarGridSpec(num_scalar_prefetch=1, grid=grid, in_specs=in_specs,
                                               out_specs=out_specs),
        input_output_aliases=aliases or {},
        compiler_params=_params(sem))


def _cast_into_full(w, layer, colwise, where, name):
    _, K, N = w.shape
    tr = _row_tile(K, N)
    nrb = K // tr
    full_shape = (1, K, 4 * N) if colwise else (1, 4 * K, N)

    def body(where_ref, w_ref, o_ref):
        del where_ref
        o_ref[...] = w_ref[...].astype(BF16)

    if colwise:
        out_map = lambda i, wh: (0, i, wh[0])
    else:
        out_map = lambda i, wh: (0, wh[0] * nrb + i, 0)
    return _prefetch_call(
        body, (nrb,), [pl.BlockSpec((None, tr, N), lambda i, wh: (layer, i, 0))],
        pl.BlockSpec((None, tr, N), out_map), jax.ShapeDtypeStruct(full_shape, BF16), name,
        ("parallel",))(where, w)


def _chip_sum(g3, other, colwise, where, name):
    L, K, N = g3.shape
    hk, hn = (K // 2, N) if colwise else (K, N // 2)
    tr = _row_tile(hk, hn)
    nrb = hk // tr

    def body(where_ref, g_ref, o_ref, s_ref):
        del where_ref
        s_ref[...] = (g_ref[...].astype(F32) + o_ref[...].astype(F32)).astype(BF16)

    if colwise:
        g_map = lambda l, i, wh: (l, wh[1] * nrb + i, 0)
    else:
        g_map = lambda l, i, wh: (l, i, wh[1])
    blk = pl.BlockSpec((None, tr, hn), lambda l, i, wh: (l, i, 0))
    return _prefetch_call(
        body, (L, nrb), [pl.BlockSpec((None, tr, hn), g_map), blk], blk,
        jax.ShapeDtypeStruct((L, hk, hn), BF16), name, ("parallel", "parallel"))(where, g3, other)


def _owner_sum(chip_sum, received, colwise, where, layer, n_layers, prev, name):
    _, hk, hn = chip_sum.shape
    pk, pn = (hk, hn // 4) if colwise else (hk // 4, hn)
    tr = _row_tile(pk, pn)
    nrb = pk // tr
    shard_shape = (n_layers, 2 * pk, pn) if colwise else (n_layers, pk, 2 * pn)

    def body(where_ref, own_ref, rec_ref, *rest):
        del where_ref
        o_ref = rest[-1]
        acc = own_ref[...].astype(F32)
        for j in range(3):
            acc = acc + rec_ref[j].astype(F32)
        o_ref[...] = acc

    if colwise:
        own_map = lambda i, wh: (0, i, wh[0])
        out_map = lambda i, wh: (layer, wh[1] * nrb + i, 0)
    else:
        own_map = lambda i, wh: (0, wh[0] * nrb + i, 0)
        out_map = lambda i, wh: (layer, i, wh[1])
    in_specs = [pl.BlockSpec((None, tr, pn), own_map),
                pl.BlockSpec((3, None, tr, pn), lambda i, wh: (0, 0, i, 0))]
    operands = [where, chip_sum, received]
    if prev is not None:
        in_specs.append(ANY)
        operands.append(prev)
    return _prefetch_call(
        body, (nrb,), in_specs, pl.BlockSpec((None, tr, pn), out_map), jax.ShapeDtypeStruct(shard_shape, F32), name,
        ("parallel",), None if prev is None else {3: 0})(*operands)


def _adamw(w, g, m, v, name):
    R, Cc = w.shape
    tr = _row_tile(R, Cc, 256 * 1024)

    def body(w_ref, g_ref, m_ref, v_ref, d_ref, nm_ref, nv_ref):
        gv = g_ref[...]
        nm = ADAM_B1 * m_ref[...] + (1.0 - ADAM_B1) * gv
        nv = ADAM_B2 * v_ref[...] + (1.0 - ADAM_B2) * (gv * gv)
        m_hat = nm / (1.0 - ADAM_B1 ** ADAM_STEP)
        v_hat = nv / (1.0 - ADAM_B2 ** ADAM_STEP)
        d_ref[...] = -ADAM_LR * (m_hat / (jnp.sqrt(v_hat) + ADAM_EPS) + ADAM_WD * w_ref[...])
        nm_ref[...] = nm
        nv_ref[...] = nv

    blk = pl.BlockSpec((tr, Cc), lambda i: (i, 0))
    shp = jax.ShapeDtypeStruct((R, Cc), F32)
    return pl.pallas_call(
        body, grid=(R // tr,), name=name,
        in_specs=[blk] * 4, out_specs=[blk] * 3, out_shape=[shp] * 3,
        compiler_params=_params(("parallel",)),
    )(w, g, m, v)


COL_SHARDED = (True, False, True, False)


def _position():
    x, y, c = lax.axis_index("x"), lax.axis_index("y"), lax.axis_index("c")
    chips = [(1 - x, y), (x, 1 - y), (1 - x, 1 - y)]
    return x, y, c, chips


def _span(index, size, align):
    return pl.ds(pl.multiple_of(index * size, align), size)


def _window(ref, colwise, shard, half, shards=4):
    _, K, N = ref.shape
    rows = cols = slice(None)
    if colwise:
        if half is not None:
            rows = _span(half, K // 2, 16)
        if shard is not None:
            cols = _span(shard, N // shards, LANES)
    else:
        if shard is not None:
            rows = _span(shard, K // shards, 16)
        if half is not None:
            cols = _span(half, N // 2, LANES)
    return ref.at[:, rows, cols]


HBM = pl.BlockSpec(memory_space=pltpu.HBM)
SEMAPHORES = pl.BlockSpec(memory_space=pltpu.SEMAPHORE)


def _gather_start(fulls, colwise, group_sizes, after, name):
    n = len(fulls)
    n_groups = len(group_sizes)

    n_in = n if after is None else n + 1

    def body(*refs):
        ins = refs[:n]
        sems = refs[n_in:n_in + 2 * n_groups]
        x, y, c, chips = _position()
        me = 2 * x + y
        i = 0
        for g, size in enumerate(group_sizes):
            for a in range(size):
                win = _window(ins[i], colwise[i], me, None)
                for j, chip in enumerate(chips):
                    pltpu.make_async_remote_copy(
                        src_ref=win, dst_ref=win, send_sem=sems[2 * g].at[a * 3 + j],
                        recv_sem=sems[2 * g + 1].at[a * 3 + j],
                        device_id=(chip[0], chip[1], c), device_id_type=MESH_ID).start()
                i += 1

    sem_shapes = []
    for size in group_sizes:
        sem_shapes += [pltpu.SemaphoreType.DMA((3 * size,)), pltpu.SemaphoreType.DMA((3 * size,))]
    operands = [pltpu.with_memory_space_constraint(f, pltpu.HBM) for f in fulls]
    in_specs = [HBM] * n
    if after is not None:
        operands.append(after)
        in_specs.append(ANY)
    outs = pl.pallas_call(
        body, name=name,
        in_specs=in_specs, out_specs=[SEMAPHORES] * (2 * n_groups) + [HBM] * n,
        out_shape=sem_shapes + [pltpu.HBM(f.shape, f.dtype) for f in fulls],
        input_output_aliases={i: 2 * n_groups + i for i in range(n)},
        compiler_params=pltpu.CompilerParams(has_side_effects=pltpu.SideEffectType.DATAFLOW_SIDE_EFFECTING),
    )(*operands)
    sems = [(outs[2 * g], outs[2 * g + 1]) for g in range(n_groups)]
    return sems, list(outs[2 * n_groups:])


def _gather_wait(in_flight, colwise, sems, after, name):
    n = len(in_flight)

    def body(*refs):
        ins = refs[:n]
        send_sems, recv_sems = refs[n], refs[n + 1]
        x, y, c, chips = _position()
        me = 2 * x + y
        for a in range(n):
            for j, chip in enumerate(chips):
                cp = pltpu.make_async_remote_copy(
                    src_ref=_window(ins[a], colwise[a], me, None),
                    dst_ref=_window(ins[a], colwise[a], 2 * chip[0] + chip[1], None),
                    send_sem=send_sems.at[a * 3 + j], recv_sem=recv_sems.at[a * 3 + j],
                    device_id=(chip[0], chip[1], c), device_id_type=MESH_ID)
                cp.wait_send()
                cp.wait_recv()

    operands = list(in_flight) + list(sems)
    in_specs = [HBM] * n + [SEMAPHORES] * 2
    if after is not None:
        operands.append(after)
        in_specs.append(ANY)
    outs = pl.pallas_call(
        body, name=name,
        in_specs=in_specs, out_specs=[HBM] * n,
        out_shape=[pltpu.HBM(f.shape, f.dtype) for f in in_flight],
        input_output_aliases={i: i for i in range(n)},
        compiler_params=pltpu.CompilerParams(has_side_effects=pltpu.SideEffectType.DATAFLOW_SIDE_EFFECTING),
    )(*operands)
    return list(outs)


def _exchange_halves(grads, colwise, name):
    n = len(grads)
    out_shapes = []
    for g, cw in zip(grads, colwise):
        L, K, N = g.shape
        out_shapes.append(jax.ShapeDtypeStruct((L, K // 2, N) if cw else (L, K, N // 2), g.dtype))

    def body(*refs):
        g_refs, out = refs[:n], refs[n:2 * n]
        send_sems, recv_sems = refs[2 * n:]
        x, y, c, _ = _position()
        copies = [pltpu.make_async_remote_copy(
            src_ref=_window(g_refs[i], colwise[i], None, 1 - c), dst_ref=out[i],
            send_sem=send_sems.at[i], recv_sem=recv_sems.at[i],
            device_id=(x, y, 1 - c), device_id_type=MESH_ID) for i in range(n)]
        for cp in copies:
            cp.start()
        for cp in copies:
            cp.wait()

    return pl.pallas_call(
        body, name=name,
        in_specs=[ANY] * n, out_specs=[ANY] * n, out_shape=out_shapes,
        scratch_shapes=[pltpu.SemaphoreType.DMA((n,)), pltpu.SemaphoreType.DMA((n,))],
    )(*grads)


def _scatter_copy(src_ref, land_ref, colwise, j, chip, c, send_sem, recv_sem):
    return pltpu.make_async_remote_copy(
        src_ref=_window(src_ref, colwise, 2 * chip[0] + chip[1], None), dst_ref=land_ref.at[j],
        send_sem=send_sem, recv_sem=recv_sem, device_id=(chip[0], chip[1], c), device_id_type=MESH_ID)


def _scatter_start(chip_sums, colwise, name):
    n = len(chip_sums)
    lands = []
    for g, cw in zip(chip_sums, colwise):
        L, hk, hn = g.shape
        lands.append(lax.empty((3, L, hk, hn // 4) if cw else (3, L, hk // 4, hn), g.dtype))

    def body(*refs):
        src, land = refs[:n], refs[n:2 * n]
        send_sems, recv_sems = refs[2 * n], refs[2 * n + 1]
        x, y, c, chips = _position()
        for i in range(n):
            for j, chip in enumerate(chips):
                _scatter_copy(src[i], land[i], colwise[i], j, chip, c, send_sems.at[i * 3 + j],
                              recv_sems.at[i * 3 + j]).start()

    arrays = list(chip_sums) + lands
    outs = pl.pallas_call(
        body, name=name,
        in_specs=[HBM] * (2 * n), out_specs=[SEMAPHORES] * 2 + [HBM] * (2 * n),
        out_shape=[pltpu.SemaphoreType.DMA((3 * n,)), pltpu.SemaphoreType.DMA((3 * n,))]
        + [pltpu.HBM(a.shape, a.dtype) for a in arrays],
        input_output_aliases={i: 2 + i for i in range(2 * n)},
        compiler_params=pltpu.CompilerParams(has_side_effects=pltpu.SideEffectType.DATAFLOW_SIDE_EFFECTING),
    )(*[pltpu.with_memory_space_constraint(a, pltpu.HBM) for a in arrays])
    return (outs[0], outs[1]), list(outs[2:2 + n]), list(outs[2 + n:])


def _scatter_wait(sources, lands, colwise, sems, after, name):
    n = len(sources)

    def body(*refs):
        src, land = refs[:n], refs[n:2 * n]
        send_sems, recv_sems = refs[2 * n], refs[2 * n + 1]
        x, y, c, chips = _position()
        for i in range(n):
            for j, chip in enumerate(chips):
                cp = _scatter_copy(src[i], land[i], colwise[i], j, chip, c, send_sems.at[i * 3 + j],
                                   recv_sems.at[i * 3 + j])
                cp.wait_send()
                cp.wait_recv()

    arrays = list(sources) + list(lands)
    operands = arrays + list(sems)
    in_specs = [HBM] * (2 * n) + [SEMAPHORES] * 2
    if after is not None:
        operands.append(after)
        in_specs.append(ANY)
    outs = pl.pallas_call(
        body, name=name,
        in_specs=in_specs, out_specs=[HBM] * (2 * n),
        out_shape=[pltpu.HBM(a.shape, a.dtype) for a in arrays],
        input_output_aliases={i: i for i in range(2 * n)},
        compiler_params=pltpu.CompilerParams(has_side_effects=pltpu.SideEffectType.DATAFLOW_SIDE_EFFECTING),
    )(*operands)
    return list(outs[:n]), list(outs[n:])


def _share_with_sibling(shards):
    n = len(shards)

    def body(*refs):
        out = refs[n:2 * n]
        send_sems, recv_sems = refs[2 * n:]
        x, y, c, _ = _position()

        def copy(i, half):
            win = _window(out[i], COL_SHARDED[i], None, half)
            return pltpu.make_async_remote_copy(
                src_ref=win, dst_ref=win, send_sem=send_sems.at[i], recv_sem=recv_sems.at[i],
                device_id=(x, y, 1 - c), device_id_type=MESH_ID)

        for i in range(n):
            copy(i, c).start()
        for i in range(n):
            copy(i, 1 - c).wait_recv()
        for i in range(n):
            copy(i, c).wait_send()

    return pl.pallas_call(
        body, name="grad_share_with_sibling",
        in_specs=[ANY] * n, out_specs=[ANY] * n,
        out_shape=[jax.ShapeDtypeStruct(s.shape, s.dtype) for s in shards],
        input_output_aliases={i: i for i in range(n)},
        scratch_shapes=[pltpu.SemaphoreType.DMA((n,)), pltpu.SemaphoreType.DMA((n,))],
    )(*shards)


def _all_reduce_small(pack, name):
    R, Cc = pack.shape
    n_dev = 8

    def body(p_ref, o_ref, buf, send_sems, recv_sems):
        x, y, c, _ = _position()
        me = 4 * x + 2 * y + c
        buf[me] = p_ref[...]

        def peer(k):
            px = 1 - x if k & 4 else x
            py = 1 - y if k & 2 else y
            pc = 1 - c if k & 1 else c
            return px, py, pc

        def copy(k, incoming):
            px, py, pc = peer(k)
            slot = (4 * px + 2 * py + pc) if incoming else me
            return pltpu.make_async_remote_copy(
                src_ref=p_ref, dst_ref=buf.at[slot], send_sem=send_sems.at[k], recv_sem=recv_sems.at[k],
                device_id=(px, py, pc), device_id_type=MESH_ID)

        for k in range(1, n_dev):
            copy(k, False).start()
        for k in range(1, n_dev):
            copy(k, True).wait_recv()
        for k in range(1, n_dev):
            copy(k, False).wait_send()
        acc = buf[0]
        for j in range(1, n_dev):
            acc = acc + buf[j]
        o_ref[...] = acc

    vmem = pl.BlockSpec(memory_space=pltpu.VMEM)
    return pl.pallas_call(
        body, name=name,
        in_specs=[vmem], out_specs=vmem, out_shape=jax.ShapeDtypeStruct((R, Cc), F32),
        scratch_shapes=[pltpu.VMEM((n_dev, R, Cc), F32), pltpu.SemaphoreType.DMA((n_dev,)),
                        pltpu.SemaphoreType.DMA((n_dev,))],
    )(pack)


def _local_forward_backward(x2, target2, S, fetch, reduce, layers, final_g, tm=512):
    T, D = x2.shape
    C = D // 2
    n_heads = C // GROUP
    n_layers = len(layers)
    weights = {}
    saved = []
    xc = x2
    for li, lw in enumerate(layers):
        if li == 0:
            weights.update(fetch(0, None))
        h1, qkv3, cv3 = _norm_proj(xc, lw["norm1"], weights[li, "w_in"], 0, ((3, C, F32), (3, C, BF16)), tm,
                                   min(C, 512), f"l{li}_norm_in_proj")
        o, lse, mix = _attn_fwd(qkv3, lw["attn_g"], 2, S, n_heads, f"l{li}_attn_fwd")
        if li == 0:
            weights.update(fetch(1, o))
        mix = _mix_conv_fwd(cv3, lw["taps"], lw["conv_g"], mix, S, f"l{li}_mix_conv_fwd")
        x_mid = _proj_residual(mix, weights[li, "w_out"], 0, xc, tm, f"l{li}_out_proj")
        if li == 0:
            weights.update(fetch(2, x_mid))
        Fd = weights[li, "ffn_up"].shape[2] // 2
        h2, up3 = _norm_proj(x_mid, lw["norm2"], weights[li, "ffn_up"], 0, ((2, Fd, BF16),), tm // 2, 256,
                             f"l{li}_norm_ffn_up")
        if li == 0:
            weights.update(fetch(3, up3))
        act = _ffn_act_fwd(up3, lw["ffn_taps"], S, f"l{li}_ffn_act_fwd")
        x_out = _proj_residual(act.reshape(1, T, Fd), weights[li, "ffn_down"], 0, x_mid, tm, f"l{li}_ffn_down")
        if li + 1 < n_layers:
            weights.update(fetch(li + 4, x_out))
        saved.append(dict(x_in=xc, h1=h1, qkv3=qkv3, cv3=cv3, o=o, lse=lse, mix=mix, x_mid=x_mid, h2=h2, up3=up3,
                          act=act))
        xc = x_out

    dx, d_final_g, loss_part = _final_norm_loss(xc, final_g, target2, tm, "final_norm_loss")

    small = [None] * n_layers
    for li in reversed(range(n_layers)):
        lw, sv = layers[li], saved[li]
        w_in, w_out, ffn_up, ffn_down = (weights[li, n] for n in ("w_in", "w_out", "ffn_up", "ffn_down"))
        dxb, dact3 = _grad_through_weight(dx, ffn_down, 0, 1, Fd, BF16, tm, 256, f"l{li}_d_act")
        Fd = ffn_down.shape[1]
        d_ffn_down = _weight_grad(sv["act"].reshape(1, T, Fd), dxb.reshape(1, T, D), Fd // 2, D, 1024,
                                  0, 1, None, f"l{li}_d_ffn_down")
        dup3, d_taps_g, d_taps_v = _ffn_act_bwd(sv["up3"], dact3, lw["ffn_taps"], S, f"l{li}_ffn_act_bwd")
        d_ffn_up = _weight_grad(sv["h2"].reshape(1, T, D), dup3, D, Fd // 2, 1024, 0, 1, None,
                                f"l{li}_d_ffn_up")
        if li == 0:
            reduce({(li, "ffn_down"): d_ffn_down, (li, "ffn_up"): d_ffn_up})
        dx_mid, d_norm2 = _grad_through_proj_norm(dup3, ffn_up, 0, sv["x_mid"], lw["norm2"], dx, tm // 2,
                                                  f"l{li}_d_norm2")
        dxmb, dmix3 = _grad_through_weight(dx_mid, w_out, 0, 2, C, F32, tm, min(C, 512), f"l{li}_d_mix")
        d_w_out = _weight_grad(sv["mix"], dxmb.reshape(1, T, D), C, D, 1024, 0, 1, None, f"l{li}_d_w_out")
        dproj, d_attn_g = _attn_bwd(sv["qkv3"], sv["o"], sv["lse"], dmix3, lw["attn_g"], 6, S, n_heads,
                                    f"l{li}_attn_bwd")
        dproj, d_taps, d_conv_g = _mix_conv_bwd(sv["cv3"], dmix3, lw["taps"], lw["conv_g"], dproj, S,
                                                f"l{li}_mix_conv_bwd")
        d_w_in = _weight_grad(sv["h1"].reshape(1, T, D), dproj, D, C, 1024, 0, 1, None, f"l{li}_d_w_in")
        if li == 0:
            reduce({(li, "w_out"): d_w_out, (li, "w_in"): d_w_in})
        else:
            reduce({(li, "ffn_down"): d_ffn_down, (li, "ffn_up"): d_ffn_up, (li, "w_out"): d_w_out,
                    (li, "w_in"): d_w_in})
        dx, d_norm1 = _grad_through_proj_norm(dproj, w_in, 0, sv["x_in"], lw["norm1"], dx_mid, tm,
                                              f"l{li}_d_norm1")
        small[li] = dict(norm1=d_norm1, taps=d_taps, attn_g=d_attn_g, conv_g=d_conv_g, norm2=d_norm2,
                         ffn_taps=jnp.concatenate([d_taps_g, d_taps_v], axis=1))
    return loss_part, dx, small, d_final_g


SMALL_ORDER = ("norm1", "attn_g", "conv_g", "norm2", "taps", "ffn_taps")


def _pack_small(small, d_final_g):
    parts = [small[li][k].reshape(-1) for li in range(len(small)) for k in SMALL_ORDER] + [d_final_g.reshape(-1)]
    return jnp.concatenate(parts).reshape(-1, LANES)


def _unpack_small(pack, small, d_final_g):
    flat = pack.reshape(-1)
    out, pos = [dict() for _ in small], 0
    for li in range(len(small)):
        for k in SMALL_ORDER:
            n = small[li][k].size
            out[li][k] = flat[pos:pos + n].reshape(small[li][k].shape)
            pos += n
    return out, flat[pos:pos + d_final_g.size]


def kernel(x, norm1_g, w_in, mix_conv_w, attn_out_g, conv_out_g, w_out, norm2_g, ffn_up, ffn_conv_w, ffn_down, final_norm_g, loss_target, m_norm1_g, m_w_in, m_mix_conv_w, m_attn_out_g, m_conv_out_g, m_w_out, m_norm2_g, m_ffn_up, m_ffn_conv_w, m_ffn_down, m_final_norm_g, v_norm1_g, v_w_in, v_mix_conv_w, v_attn_out_g, v_conv_out_g, v_w_out, v_norm2_g, v_ffn_up, v_ffn_conv_w, v_ffn_down, v_final_norm_g):
    Bl, S, D = x.shape
    L = w_in.shape[0]
    T = Bl * S
    shard = 2 * lax.axis_index("x") + lax.axis_index("y")
    where = jnp.stack([shard, lax.axis_index("c")]).astype(jnp.int32)
    big_names = ("w_in", "w_out", "ffn_up", "ffn_down")

    taps_w, ftaps_w = mix_conv_w.shape[2], ffn_conv_w.shape[2]
    taps_full = jnp.zeros((L, 3, 4 * taps_w), F32)
    taps_full = lax.dynamic_update_slice(taps_full, mix_conv_w, (0, 0, shard * taps_w))
    ftaps_full = jnp.zeros((L, 3, 4 * ftaps_w), F32)
    ftaps_full = lax.dynamic_update_slice(ftaps_full, ffn_conv_w, (0, 0, shard * ftaps_w))
    tap_pack = jnp.concatenate([taps_full.reshape(-1), ftaps_full.reshape(-1)]).reshape(-1, LANES)
    tap_pack = _all_reduce_small(tap_pack * 0.5, "all_gather_taps")
    n_taps = taps_full.size
    taps_full = tap_pack.reshape(-1)[:n_taps].reshape(taps_full.shape)
    ftaps_full = tap_pack.reshape(-1)[n_taps:].reshape(ftaps_full.shape)

    big_shards = dict(zip(big_names, (w_in, w_out, ffn_up, ffn_down)))
    col_of = dict(zip(big_names, COL_SHARDED))
    groups = [[(0, n)] for n in big_names] + [[(l, n) for n in big_names] for l in range(1, L)]
    sems, in_flight = [], {}
    for first, last, after in ((0, 1, tap_pack), (1, len(groups), None)):
        keys = [k for g in groups[first:last] for k in g]
        new_sems, arrays = _gather_start(
            [_cast_into_full(big_shards[n], l, col_of[n], where, f"cast_{n}_{l}") for l, n in keys],
            [col_of[n] for _, n in keys], [len(g) for g in groups[first:last]], after, f"gather_start_{first}")
        sems += new_sems
        in_flight.update(zip(keys, arrays))

    def fetch(g, after):
        done = _gather_wait([in_flight[k] for k in groups[g]], [col_of[n] for _, n in groups[g]], sems[g], after,
                            f"gather_wait_{g}")
        return dict(zip(groups[g], done))

    pending = []

    def reduce(grads):
        g = len(pending)
        keys = list(grads)
        cols = [col_of[n] for _, n in keys]
        others = _exchange_halves([grads[k] for k in keys], cols, f"grad_exchange_halves_{g}")
        chip_sums = [_chip_sum(grads[k], o, cw, where, f"chip_sum_{k[1]}_{k[0]}")
                     for k, o, cw in zip(keys, others, cols)]
        pending.append((keys, cols) + _scatter_start(chip_sums, cols, f"scatter_start_{g}"))

    layers = [dict(norm1=norm1_g[l:l + 1], taps=taps_full[l], attn_g=attn_out_g[l:l + 1],
                   conv_g=conv_out_g[l:l + 1], norm2=norm2_g[l:l + 1], ffn_taps=ftaps_full[l]) for l in range(L)]

    loss_part, dx, small, d_final_g = _local_forward_backward(
        x.reshape(T, D), loss_target.reshape(T, D), S, fetch, reduce, layers, final_norm_g.reshape(1, D))
    loss = lax.psum(loss_part[0, 0], ("x", "y", "c"))

    reduced = dict.fromkeys(big_names)
    last_started = pending[-1][3][0]
    for g, (keys, cols, rs_sems, sources, lands) in enumerate(pending):
        after = last_started if g + 1 < len(pending) else None
        sources, lands = _scatter_wait(sources, lands, cols, rs_sems, after, f"scatter_wait_{g}")
        for (l, n), cw, src, land in zip(keys, cols, sources, lands):
            reduced[n] = _owner_sum(src, land, cw, where, l, L, reduced[n], f"owner_sum_{n}_{l}")
    g_big = _share_with_sibling([reduced[n] for n in big_names])

    pack = _all_reduce_small(_pack_small(small, d_final_g), "all_reduce_small_grads")
    g_small, g_final = _unpack_small(pack, small, d_final_g)

    def stacked(key):
        return jnp.stack([g_small[l][key].reshape(g_small[l][key].shape[-2:] if key.endswith("taps") else (-1,))
                          for l in range(L)])

    g_norm1, g_attn, g_conv, g_norm2 = stacked("norm1"), stacked("attn_g"), stacked("conv_g"), stacked("norm2")
    g_taps = lax.dynamic_slice(stacked("taps"), (0, 0, shard * taps_w), (L, 3, taps_w))
    g_ftaps = lax.dynamic_slice(stacked("ffn_taps"), (0, 0, shard * ftaps_w), (L, 3, ftaps_w))

    grads_out = dict(norm1_g=g_norm1, w_in=g_big[0], mix_conv_w=g_taps, attn_out_g=g_attn, conv_out_g=g_conv,
                     w_out=g_big[1], norm2_g=g_norm2, ffn_up=g_big[2], ffn_conv_w=g_ftaps, ffn_down=g_big[3],
                     final_norm_g=g_final)
    weights = dict(norm1_g=norm1_g, w_in=w_in, mix_conv_w=mix_conv_w, attn_out_g=attn_out_g, conv_out_g=conv_out_g,
                   w_out=w_out, norm2_g=norm2_g, ffn_up=ffn_up, ffn_conv_w=ffn_conv_w, ffn_down=ffn_down,
                   final_norm_g=final_norm_g)
    ms = dict(norm1_g=m_norm1_g, w_in=m_w_in, mix_conv_w=m_mix_conv_w, attn_out_g=m_attn_out_g,
              conv_out_g=m_conv_out_g, w_out=m_w_out, norm2_g=m_norm2_g, ffn_up=m_ffn_up, ffn_conv_w=m_ffn_conv_w,
              ffn_down=m_ffn_down, final_norm_g=m_final_norm_g)
    vs = dict(norm1_g=v_norm1_g, w_in=v_w_in, mix_conv_w=v_mix_conv_w, attn_out_g=v_attn_out_g,
              conv_out_g=v_conv_out_g, w_out=v_w_out, norm2_g=v_norm2_g, ffn_up=v_ffn_up, ffn_conv_w=v_ffn_conv_w,
              ffn_down=v_ffn_down, final_norm_g=v_final_norm_g)
    names = list(weights)
    small_names = [n for n in names if n not in big_names]
    delta, new_m, new_v = {}, {}, {}
    for n in big_names:
        shp = weights[n].shape
        two_d = (shp[0] * shp[1], shp[2])
        d_, m_, v_ = _adamw(weights[n].reshape(two_d), grads_out[n].reshape(two_d), ms[n].reshape(two_d),
                            vs[n].reshape(two_d), f"adamw_{n}")
        delta[n], new_m[n], new_v[n] = d_.reshape(shp), m_.reshape(shp), v_.reshape(shp)

    def packed(tree):
        return jnp.concatenate([tree[n].reshape(-1) for n in small_names]).reshape(-1, LANES)

    d_, m_, v_ = _adamw(packed(weights), packed(grads_out), packed(ms), packed(vs), "adamw_small")
    pos = 0
    for n in small_names:
        size, shp = weights[n].size, weights[n].shape
        delta[n] = d_.reshape(-1)[pos:pos + size].reshape(shp)
        new_m[n] = m_.reshape(-1)[pos:pos + size].reshape(shp)
        new_v[n] = v_.reshape(-1)[pos:pos + size].reshape(shp)
        pos += size

    return (loss, dx.reshape(Bl, S, D), *[grads_out[n] for n in names], *[delta[n] for n in names],
            *[new_m[n] for n in names], *[new_v[n] for n in names])
```

```python
import functools
import math

import jax
import jax.numpy as jnp
from jax import lax
from jax.experimental import pallas as pl
from jax.experimental.pallas import tpu as pltpu

F32 = jnp.float32
BF16 = jnp.bfloat16
EPS = 1e-6
GROUP = 64
LANES = 128
BAND = 128
DILATIONS = (1, 4, 16)
NEG = -1e30
MIB = 1024 * 1024
MESH_ID = pl.DeviceIdType.MESH

ADAM_LR = 0.001
ADAM_B1 = 0.9
ADAM_B2 = 0.999
ADAM_EPS = 1e-08
ADAM_WD = 0.01
ADAM_STEP = 10


ANY = pl.BlockSpec(memory_space=pl.ANY)


def _params(sem=None, vmem_mb=48):
    return pltpu.CompilerParams(dimension_semantics=sem, vmem_limit_bytes=vmem_mb * MIB)


def _nt(a, b):
    return lax.dot_general(a, b, (((1,), (1,)), ((), ())), preferred_element_type=F32)


def _tn(a, b):
    return lax.dot_general(a, b, (((0,), (0,)), ((), ())), preferred_element_type=F32)


def _seg_sum(x, is_a):
    s_a = jnp.sum(jnp.where(is_a, x, 0.0), axis=-1, keepdims=True)
    s_b = jnp.sum(jnp.where(is_a, 0.0, x), axis=-1, keepdims=True)
    return jnp.where(is_a, s_a, s_b)


def _lane_is_a():
    return lax.broadcasted_iota(jnp.int32, (1, LANES), 1) < GROUP


def _norm_proj(x, g, w3, layer, groups, tm, chunk, name):
    T, D = x.shape
    N = w3.shape[2]
    assert sum(p * c for p, c, _ in groups) == N and T % tm == 0

    def body(x_ref, g_ref, w_ref, h_ref, *out_refs):
        xv = x_ref[...]
        rstd = lax.rsqrt(jnp.mean(xv * xv, axis=-1, keepdims=True) + EPS)
        h = ((xv * rstd) * g_ref[...]).astype(BF16)
        h_ref[...] = h
        col = 0
        for (pieces, width, dtype), o_ref in zip(groups, out_refs):
            for p in range(pieces):
                for c0 in range(0, width, chunk):
                    acc = jnp.dot(h, w_ref[:, col + c0:col + c0 + chunk], preferred_element_type=F32)
                    o_ref[p, :, c0:c0 + chunk] = acc.astype(dtype)
                col += width

    out_shape = [jax.ShapeDtypeStruct((T, D), BF16)]
    out_specs = [pl.BlockSpec((tm, D), lambda i: (i, 0))]
    for pieces, width, dtype in groups:
        assert width % chunk == 0
        out_shape.append(jax.ShapeDtypeStruct((pieces, T, width), dtype))
        out_specs.append(pl.BlockSpec((pieces, tm, width), lambda i: (0, i, 0)))
    return pl.pallas_call(
        body, grid=(T // tm,), name=name,
        in_specs=[pl.BlockSpec((tm, D), lambda i: (i, 0)),
                  pl.BlockSpec((1, D), lambda i: (0, 0)),
                  pl.BlockSpec((None, D, N), lambda i: (layer, 0, 0))],
        out_specs=out_specs, out_shape=out_shape,
        compiler_params=_params(("parallel",), 56),
    )(x, g, w3)


def _proj_residual(pieces3, w3, layer, x, tm, name):
    P, T, C = pieces3.shape
    D = w3.shape[2]

    def body(a_ref, w_ref, x_ref, o_ref):
        acc = x_ref[...]
        for p in range(P):
            acc = acc + jnp.dot(a_ref[p], w_ref[p * C:(p + 1) * C, :], preferred_element_type=F32)
        o_ref[...] = acc

    return pl.pallas_call(
        body, grid=(T // tm,), name=name,
        in_specs=[pl.BlockSpec((P, tm, C), lambda i: (0, i, 0)),
                  pl.BlockSpec((None, P * C, D), lambda i: (layer, 0, 0)),
                  pl.BlockSpec((tm, D), lambda i: (i, 0))],
        out_specs=pl.BlockSpec((tm, D), lambda i: (i, 0)),
        out_shape=jax.ShapeDtypeStruct((T, D), F32),
        compiler_params=_params(("parallel",)),
    )(pieces3, w3, x)


def _grad_through_weight(dy, w3, layer, pieces, width, out_dtype, tm, chunk, name, after=None):
    T, D = dy.shape

    def body(dy_ref, w_ref, *rest):
        dyb_ref, o_ref = rest[-2:]
        dyb = dy_ref[...].astype(BF16)
        dyb_ref[...] = dyb
        for p in range(pieces):
            for c0 in range(0, width, chunk):
                r0 = p * width + c0
                o_ref[p, :, c0:c0 + chunk] = _nt(dyb, w_ref[r0:r0 + chunk, :]).astype(out_dtype)

    in_specs = [pl.BlockSpec((tm, D), lambda i: (i, 0)),
                pl.BlockSpec((None, pieces * width, D), lambda i: (layer, 0, 0))]
    operands = [dy, w3]
    if after is not None:
        in_specs.append(ANY)
        operands.append(after)
    return pl.pallas_call(
        body, grid=(T // tm,), name=name,
        in_specs=in_specs,
        out_specs=[pl.BlockSpec((tm, D), lambda i: (i, 0)),
                   pl.BlockSpec((pieces, tm, width), lambda i: (0, i, 0))],
        out_shape=[jax.ShapeDtypeStruct((T, D), BF16),
                   jax.ShapeDtypeStruct((pieces, T, width), out_dtype)],
        compiler_params=_params(("parallel",)),
    )(*operands)


def _grad_through_proj_norm(dp3, w3, layer, x, g, dx_in, tm, name):
    P, T, C = dp3.shape
    D = w3.shape[1]

    def body(dp_ref, w_ref, x_ref, g_ref, dxin_ref, dx_ref, dg_ref):
        dh = _nt(dp_ref[0], w_ref[:, 0:C])
        for p in range(1, P):
            dh = dh + _nt(dp_ref[p], w_ref[:, p * C:(p + 1) * C])
        xv = x_ref[...]
        rstd = lax.rsqrt(jnp.mean(xv * xv, axis=-1, keepdims=True) + EPS)
        xn = xv * rstd
        a = dh * g_ref[...]
        dx_ref[...] = dxin_ref[...] + rstd * (a - xn * jnp.mean(a * xn, axis=-1, keepdims=True))
        part = jnp.sum(dh * xn, axis=0, keepdims=True)

        @pl.when(pl.program_id(0) == 0)
        def _():
            dg_ref[...] = part

        @pl.when(pl.program_id(0) != 0)
        def _():
            dg_ref[...] += part

    return pl.pallas_call(
        body, grid=(T // tm,), name=name,
        in_specs=[pl.BlockSpec((P, tm, C), lambda i: (0, i, 0)),
                  pl.BlockSpec((None, D, P * C), lambda i: (layer, 0, 0)),
                  pl.BlockSpec((tm, D), lambda i: (i, 0)),
                  pl.BlockSpec((1, D), lambda i: (0, 0)),
                  pl.BlockSpec((tm, D), lambda i: (i, 0))],
        out_specs=[pl.BlockSpec((tm, D), lambda i: (i, 0)),
                   pl.BlockSpec((1, D), lambda i: (0, 0))],
        out_shape=[jax.ShapeDtypeStruct((T, D), F32), jax.ShapeDtypeStruct((1, D), F32)],
        compiler_params=_params(("arbitrary",), 56),
    )(dp3, w3, x, g, dx_in)


def _weight_grad(a3, g3, ta, tg, tt, layer, n_layers, prev, name):
    PA, T, CA = a3.shape
    PG, _, CG = g3.shape
    na, ng, nt = CA // ta, CG // tg, T // tt
    assert CA % ta == 0 and CG % tg == 0 and T % tt == 0

    def body(a_ref, g_ref, *rest):
        o_ref, acc_ref = rest[-2:]
        t = pl.program_id(2)
        part = _tn(a_ref[...], g_ref[...])

        @pl.when(t == 0)
        def _():
            acc_ref[...] = part

        @pl.when(t != 0)
        def _():
            acc_ref[...] += part

        @pl.when(t == nt - 1)
        def _():
            o_ref[...] = acc_ref[...].astype(o_ref.dtype)

    in_specs = [pl.BlockSpec((None, tt, ta), lambda i, j, t: (i // na, t, i % na)),
                pl.BlockSpec((None, tt, tg), lambda i, j, t: (j // ng, t, j % ng))]
    operands = [a3, g3]
    if prev is not None:
        in_specs.append(pl.BlockSpec(memory_space=pl.ANY))
        operands.append(prev)
    return pl.pallas_call(
        body, grid=(PA * na, PG * ng, nt), name=name,
        in_specs=in_specs,
        out_specs=pl.BlockSpec((None, ta, tg), lambda i, j, t: (layer, i, j)),
        out_shape=jax.ShapeDtypeStruct((n_layers, PA * CA, PG * CG), BF16),
        scratch_shapes=[pltpu.VMEM((ta, tg), F32)],
        input_output_aliases={} if prev is None else {2: 0},
        compiler_params=_params(("parallel", "parallel", "arbitrary"), 56),
    )(*operands)


def _bias_tables(bm_ref, pair, n_heads):
    ii = lax.broadcasted_iota(jnp.int32, (BAND, 2 * BAND), 0)
    jj = lax.broadcasted_iota(jnp.int32, (BAND, 2 * BAND), 1)
    dist = BAND + ii - jj
    valid = (dist >= 0) & (dist <= BAND)
    distf = dist.astype(F32)
    for hh in range(2):
        head = (2 * pair + hh + 1).astype(F32)
        slope = jnp.exp(jnp.full((1, 1), -8.0 / n_heads * math.log(2.0), F32) * head)
        for bi, d in enumerate(DILATIONS):
            bm_ref[hh * len(DILATIONS) + bi] = jnp.where(valid, -(slope * d) * distf, NEG)


def _gather_residues(dst_ref, src, d, S, convert):
    L = S // d
    for r in range(d):
        rows = pl.ds(r, L, stride=d) if d > 1 else slice(None)
        dst_ref[r * L:(r + 1) * L, :] = convert(src(rows))


def _block_rows(t, d, S):
    nb = S // (BAND * d)
    n = t % nb
    has_prev = jnp.minimum(n, 1)
    cur = pl.ds(pl.multiple_of(t * BAND, BAND), BAND)
    prev = pl.ds(pl.multiple_of((t - has_prev) * BAND, BAND), BAND)
    return cur, prev, has_prev


def _first_block_penalty(has_prev):
    jrow = lax.broadcasted_iota(jnp.int32, (1, 2 * BAND), 1)
    pen = jnp.where(has_prev == 0, NEG, 0.0).astype(F32)
    return jnp.where(jrow < BAND, pen, 0.0)


def _attn_fwd(qkv3, gain, mix_shape_pieces, S, n_heads, name):
    _, T, C = qkv3.shape
    B, P = T // S, C // LANES
    NBLK = S // BAND
    scale = GROUP ** -0.5
    nbr = len(DILATIONS)
    RC = 256

    def body(qkv_ref, g_ref, o_ref, lse_ref, an_ref, qs, ks, vs, op, mp, lp, ob, mb, lb, bm):
        pair = pl.program_id(1)
        is_a = _lane_is_a()
        _bias_tables(bm, pair, n_heads)

        for bi, d in enumerate(DILATIONS):
            nb = S // (BAND * d)
            _gather_residues(qs, lambda rows: qkv_ref.at[0][rows, :], d, S, lambda v: (v * scale).astype(BF16))
            _gather_residues(ks, lambda rows: qkv_ref.at[1][rows, :], d, S, lambda v: v.astype(BF16))
            _gather_residues(vs, lambda rows: qkv_ref.at[2][rows, :], d, S, lambda v: v.astype(BF16))
            o_dst, m_dst, l_dst = (ob.at[bi], mb.at[bi], lb.at[bi]) if d == 1 else (op, mp, lp)

            def block(t, carry, bi=bi, d=d, nb=nb, o_dst=o_dst, m_dst=m_dst, l_dst=l_dst):
                cur, prev, has_prev = _block_rows(t, d, S)
                q = qs[cur, :]
                kc = jnp.concatenate([ks[prev, :], ks[cur, :]], axis=0)
                vc = jnp.concatenate([vs[prev, :], vs[cur, :]], axis=0)
                pen = _first_block_penalty(has_prev)
                outs = []
                for hh in range(2):
                    qm = jnp.where(is_a == (hh == 0), q, jnp.zeros_like(q))
                    s = _nt(qm, kc) + bm[hh * nbr + bi] + pen
                    m = jnp.max(s, axis=-1, keepdims=True)
                    e = jnp.exp(s - m)
                    l = jnp.sum(e, axis=-1, keepdims=True)
                    pv = jnp.dot(e.astype(BF16), vc, preferred_element_type=F32)
                    outs.append((pv, m, l))
                o_dst[cur, :] = jnp.where(is_a, outs[0][0], outs[1][0])
                m_dst[cur, :] = jnp.where(is_a, outs[0][1], outs[1][1])
                l_dst[cur, :] = jnp.where(is_a, outs[0][2], outs[1][2])
                return carry

            lax.fori_loop(0, NBLK, block, 0, unroll=8)
            if d > 1:
                L = S // d
                for r in range(d):
                    rows = pl.ds(r, L, stride=d)
                    ob.at[bi][rows, :] = op[r * L:(r + 1) * L, :]
                    mb.at[bi][rows, :] = mp[r * L:(r + 1) * L, :]
                    lb.at[bi][rows, :] = lp[r * L:(r + 1) * L, :]

        def finish(ci, carry):
            rs = pl.ds(pl.multiple_of(ci * RC, RC), RC)
            ms = [mb[bi, rs, :] for bi in range(nbr)]
            mmax = functools.reduce(jnp.maximum, ms)
            ws = [jnp.exp(m - mmax) for m in ms]
            num = sum(ob[bi, rs, :] * ws[bi] for bi in range(nbr))
            den = sum(lb[bi, rs, :] * ws[bi] for bi in range(nbr))
            o = num / den
            o_ref[rs, :] = o
            lse_ref[rs, :] = mmax + jnp.log(den)
            rstd = lax.rsqrt(_seg_sum(o * o, is_a) * (1.0 / GROUP) + EPS)
            an_ref[rs, :] = ((o * rstd) * g_ref[...]).astype(BF16)
            return carry

        lax.fori_loop(0, S // RC, finish, 0)

    seq = pl.BlockSpec((S, LANES), lambda b, p: (b, p))
    return pl.pallas_call(
        body, grid=(B, P), name=name,
        in_specs=[pl.BlockSpec((3, S, LANES), lambda b, p: (0, b, p)),
                  pl.BlockSpec((1, LANES), lambda b, p: (0, p))],
        out_specs=[seq, seq, pl.BlockSpec((None, S, LANES), lambda b, p: (0, b, p))],
        out_shape=[jax.ShapeDtypeStruct((T, C), F32), jax.ShapeDtypeStruct((T, C), F32),
                   jax.ShapeDtypeStruct((mix_shape_pieces, T, C), BF16)],
        scratch_shapes=[pltpu.VMEM((S, LANES), BF16)] * 3 + [pltpu.VMEM((S, LANES), F32)] * 3
        + [pltpu.VMEM((nbr, S, LANES), F32)] * 3 + [pltpu.VMEM((2 * nbr, BAND, 2 * BAND), F32)],
        compiler_params=_params(("parallel", "parallel")),
    )(qkv3, gain)


def _attn_bwd(qkv3, o, lse, dmix3, gain, dproj_pieces, S, n_heads, name):
    _, T, C = qkv3.shape
    B, P = T // S, C // LANES
    NBLK = S // BAND
    scale = GROUP ** -0.5
    nbr = len(DILATIONS)
    RC = 256

    def body(qkv_ref, o_ref, lse_ref, dn_ref, g_ref, dqkv_ref, dg_ref,
             do_n, dd_n, qs, ks, vs, dos, lses, dds, dqp, dkp, dvp, dqn, dkn, dvn, bm):
        pair = pl.program_id(0)
        b = pl.program_id(1)
        is_a = _lane_is_a()
        _bias_tables(bm, pair, n_heads)

        def prologue(ci, dg_acc):
            rs = pl.ds(pl.multiple_of(ci * RC, RC), RC)
            ov = o_ref[rs, :]
            dn = dn_ref[rs, :]
            rstd = lax.rsqrt(_seg_sum(ov * ov, is_a) * (1.0 / GROUP) + EPS)
            on = ov * rstd
            a = dn * g_ref[...]
            do = rstd * (a - on * (_seg_sum(a * on, is_a) * (1.0 / GROUP)))
            do_n[rs, :] = do
            dd_n[rs, :] = _seg_sum(do * ov, is_a)
            zero = jnp.zeros((RC, LANES), F32)
            dqn[rs, :] = zero
            dkn[rs, :] = zero
            dvn[rs, :] = zero
            return dg_acc + jnp.sum(dn * on, axis=0, keepdims=True)

        dg_part = lax.fori_loop(0, S // RC, prologue, jnp.zeros((1, LANES), F32))

        @pl.when(b == 0)
        def _():
            dg_ref[...] = dg_part

        @pl.when(b != 0)
        def _():
            dg_ref[...] += dg_part

        for bi, d in enumerate(DILATIONS):
            nb = S // (BAND * d)
            L = S // d
            _gather_residues(qs, lambda rows: qkv_ref.at[0][rows, :], d, S, lambda v: (v * scale).astype(BF16))
            _gather_residues(ks, lambda rows: qkv_ref.at[1][rows, :], d, S, lambda v: v.astype(BF16))
            _gather_residues(vs, lambda rows: qkv_ref.at[2][rows, :], d, S, lambda v: v.astype(BF16))
            _gather_residues(dos, lambda rows: do_n[rows, :], d, S, lambda v: v.astype(BF16))
            _gather_residues(lses, lambda rows: lse_ref[rows, :], d, S, lambda v: v)
            _gather_residues(dds, lambda rows: dd_n[rows, :], d, S, lambda v: v)
            dkp[...] = jnp.zeros((S, LANES), F32)
            dvp[...] = jnp.zeros((S, LANES), F32)

            def block(t, carry, bi=bi, d=d, nb=nb):
                cur, prev, has_prev = _block_rows(t, d, S)
                q = qs[cur, :]
                do = dos[cur, :]
                lse_t = lses[cur, :]
                dd_t = dds[cur, :]
                if nb > 1:
                    kc = jnp.concatenate([ks[prev, :], ks[cur, :]], axis=0)
                    vc = jnp.concatenate([vs[prev, :], vs[cur, :]], axis=0)
                    pen = _first_block_penalty(has_prev)
                else:
                    kc, vc = ks[cur, :], vs[cur, :]
                dq = None
                dk = None
                dv = None
                for hh in range(2):
                    mine = is_a == (hh == 0)
                    qm = jnp.where(mine, q, jnp.zeros_like(q))
                    dom = jnp.where(mine, do, jnp.zeros_like(do))
                    c0 = hh * GROUP
                    s = _nt(qm, kc)
                    if nb > 1:
                        s = s + bm[hh * nbr + bi] + pen
                    else:
                        s = s + bm[hh * nbr + bi, :, BAND:2 * BAND]
                    p = jnp.exp(s - lse_t[:, c0:c0 + 1])
                    dp = _nt(dom, vc)
                    ds = (p * (dp - dd_t[:, c0:c0 + 1])).astype(BF16)
                    pb = p.astype(BF16)
                    dq_h = jnp.dot(ds, kc, preferred_element_type=F32)
                    dk_h = _tn(ds, qm)
                    dv_h = _tn(pb, dom)
                    dq = dq_h if dq is None else jnp.where(is_a, dq, dq_h)
                    dk = dk_h if dk is None else dk + dk_h
                    dv = dv_h if dv is None else dv + dv_h
                dqp[cur, :] = dq
                if nb > 1:
                    dkp[prev, :] += dk[0:BAND, :]
                    dvp[prev, :] += dv[0:BAND, :]
                    dkp[cur, :] += dk[BAND:2 * BAND, :]
                    dvp[cur, :] += dv[BAND:2 * BAND, :]
                else:
                    dkp[cur, :] += dk
                    dvp[cur, :] += dv
                return carry

            lax.fori_loop(0, NBLK, block, 0, unroll=4)
            for r in range(d):
                rows = pl.ds(r, L, stride=d) if d > 1 else slice(None)
                dqn[rows, :] += dqp[r * L:(r + 1) * L, :]
                dkn[rows, :] += dkp[r * L:(r + 1) * L, :]
                dvn[rows, :] += dvp[r * L:(r + 1) * L, :]

        dqkv_ref[0] = (dqn[...] * scale).astype(BF16)
        dqkv_ref[1] = dkn[...].astype(BF16)
        dqkv_ref[2] = dvn[...].astype(BF16)

    seq = pl.BlockSpec((S, LANES), lambda p, b: (b, p))
    f32_seq = pltpu.VMEM((S, LANES), F32)
    bf_seq = pltpu.VMEM((S, LANES), BF16)
    return pl.pallas_call(
        body, grid=(P, B), name=name,
        in_specs=[pl.BlockSpec((3, S, LANES), lambda p, b: (0, b, p)), seq, seq,
                  pl.BlockSpec((None, S, LANES), lambda p, b: (0, b, p)),
                  pl.BlockSpec((1, LANES), lambda p, b: (0, p))],
        out_specs=[pl.BlockSpec((3, S, LANES), lambda p, b: (0, b, p)),
                   pl.BlockSpec((1, LANES), lambda p, b: (0, p))],
        out_shape=[jax.ShapeDtypeStruct((dproj_pieces, T, C), BF16), jax.ShapeDtypeStruct((1, C), F32)],
        scratch_shapes=[f32_seq, f32_seq, bf_seq, bf_seq, bf_seq, bf_seq, f32_seq, f32_seq,
                        f32_seq, f32_seq, f32_seq, f32_seq, f32_seq, f32_seq,
                        pltpu.VMEM((2 * nbr, BAND, 2 * BAND), F32)],
        compiler_params=_params(("parallel", "arbitrary")),
    )(qkv3, o, lse, dmix3, gain)


def _delay(x, k, row):
    return jnp.where(row >= k, pltpu.roll(x, k, 0), 0.0)


def _advance(x, k, row, S):
    return jnp.where(row < S - k, pltpu.roll(x, S - k, 0), 0.0)


def _conv3(x, w, row):
    return (w[0:1, :] * _delay(x, 2, row) + w[1:2, :] * _delay(x, 1, row)) + w[2:3, :] * x


def _conv3_grads(dz, x, w, row, S):
    dx = (w[2:3, :] * dz + w[1:2, :] * _advance(dz, 1, row, S)) + w[0:1, :] * _advance(dz, 2, row, S)
    dw = jnp.concatenate([jnp.sum(dz * _delay(x, 2, row), axis=0, keepdims=True),
                          jnp.sum(dz * _delay(x, 1, row), axis=0, keepdims=True),
                          jnp.sum(dz * x, axis=0, keepdims=True)], axis=0)
    return dx, dw


def _mix_conv_fwd(cv3, taps, gain, mix, S, name):
    _, T, C = cv3.shape
    B, P = T // S, C // LANES

    def body(cv_ref, w_ref, g_ref, mix_hbm, y_ref):
        del mix_hbm
        row = lax.broadcasted_iota(jnp.int32, (S, 1), 0)
        is_a = _lane_is_a()
        gb = cv_ref[0].astype(F32)
        c = cv_ref[1].astype(F32) * cv_ref[2].astype(F32)
        y = gb * _conv3(c, w_ref[...], row)
        rstd = lax.rsqrt(_seg_sum(y * y, is_a) * (1.0 / GROUP) + EPS)
        y_ref[...] = ((y * rstd) * g_ref[...]).astype(BF16)

    return pl.pallas_call(
        body, grid=(B, P), name=name,
        in_specs=[pl.BlockSpec((3, S, LANES), lambda b, p: (0, b, p)),
                  pl.BlockSpec((3, LANES), lambda b, p: (0, p)),
                  pl.BlockSpec((1, LANES), lambda b, p: (0, p)),
                  pl.BlockSpec(memory_space=pl.ANY)],
        out_specs=pl.BlockSpec((None, S, LANES), lambda b, p: (1, b, p)),
        out_shape=jax.ShapeDtypeStruct(mix.shape, mix.dtype),
        input_output_aliases={3: 0},
        compiler_params=_params(("parallel", "parallel")),
    )(cv3, taps, gain, mix)


def _mix_conv_bwd(cv3, dmix3, taps, gain, dproj, S, name):
    _, T, C = cv3.shape
    B, P = T // S, C // LANES

    def body(cv_ref, dn_ref, w_ref, g_ref, dproj_hbm, dcv_ref, dw_ref, dg_ref):
        del dproj_hbm
        b = pl.program_id(1)
        row = lax.broadcasted_iota(jnp.int32, (S, 1), 0)
        is_a = _lane_is_a()
        w = w_ref[...]
        gb = cv_ref[0].astype(F32)
        gc = cv_ref[1].astype(F32)
        u = cv_ref[2].astype(F32)
        c = gc * u
        z = _conv3(c, w, row)
        y = gb * z
        rstd = lax.rsqrt(_seg_sum(y * y, is_a) * (1.0 / GROUP) + EPS)
        yn = y * rstd
        dn = dn_ref[...]
        a = dn * g_ref[...]
        dy = rstd * (a - yn * (_seg_sum(a * yn, is_a) * (1.0 / GROUP)))
        dg = jnp.sum(dn * yn, axis=0, keepdims=True)
        dc, dw = _conv3_grads(dy * gb, c, w, row, S)
        dcv_ref[0] = (dy * z).astype(BF16)
        dcv_ref[1] = (dc * u).astype(BF16)
        dcv_ref[2] = (dc * gc).astype(BF16)

        @pl.when(b == 0)
        def _():
            dw_ref[...] = dw
            dg_ref[...] = dg

        @pl.when(b != 0)
        def _():
            dw_ref[...] += dw
            dg_ref[...] += dg

    return pl.pallas_call(
        body, grid=(P, B), name=name,
        in_specs=[pl.BlockSpec((3, S, LANES), lambda p, b: (0, b, p)),
                  pl.BlockSpec((None, S, LANES), lambda p, b: (1, b, p)),
                  pl.BlockSpec((3, LANES), lambda p, b: (0, p)),
                  pl.BlockSpec((1, LANES), lambda p, b: (0, p)),
                  pl.BlockSpec(memory_space=pl.ANY)],
        out_specs=[pl.BlockSpec((3, S, LANES), lambda p, b: (1, b, p)),
                   pl.BlockSpec((3, LANES), lambda p, b: (0, p)),
                   pl.BlockSpec((1, LANES), lambda p, b: (0, p))],
        out_shape=[jax.ShapeDtypeStruct(dproj.shape, dproj.dtype),
                   jax.ShapeDtypeStruct((3, C), F32), jax.ShapeDtypeStruct((1, C), F32)],
        input_output_aliases={4: 0},
        compiler_params=_params(("parallel", "arbitrary")),
    )(cv3, dmix3, taps, gain, dproj)


def _sigmoid(x):
    return 1.0 / (1.0 + jnp.exp(-x))


def _ffn_act_fwd(up3, taps, S, name):
    _, T, Fd = up3.shape
    B, P = T // S, Fd // LANES

    def body(up_ref, wg_ref, wv_ref, act_ref):
        row = lax.broadcasted_iota(jnp.int32, (S, 1), 0)
        cg = _conv3(up_ref[0].astype(F32), wg_ref[...], row)
        cv = _conv3(up_ref[1].astype(F32), wv_ref[...], row)
        act_ref[...] = ((cg * _sigmoid(cg)) * cv).astype(BF16)

    return pl.pallas_call(
        body, grid=(B, P), name=name,
        in_specs=[pl.BlockSpec((2, S, LANES), lambda b, p: (0, b, p)),
                  pl.BlockSpec((3, LANES), lambda b, p: (0, p)),
                  pl.BlockSpec((3, LANES), lambda b, p: (0, P + p))],
        out_specs=pl.BlockSpec((S, LANES), lambda b, p: (b, p)),
        out_shape=jax.ShapeDtypeStruct((T, Fd), BF16),
        compiler_params=_params(("parallel", "parallel")),
    )(up3, taps, taps)


def _ffn_act_bwd(up3, dact3, taps, S, name):
    _, T, Fd = up3.shape
    B, P = T // S, Fd // LANES

    def body(up_ref, da_ref, wg_ref, wv_ref, dup_ref, dwg_ref, dwv_ref):
        b = pl.program_id(1)
        row = lax.broadcasted_iota(jnp.int32, (S, 1), 0)
        ug = up_ref[0].astype(F32)
        uv = up_ref[1].astype(F32)
        wg = wg_ref[...]
        wv = wv_ref[...]
        cg = _conv3(ug, wg, row)
        cv = _conv3(uv, wv, row)
        sg = _sigmoid(cg)
        da = da_ref[...].astype(F32)
        dcg = (da * cv) * (sg * (1.0 + cg * (1.0 - sg)))
        dcv = da * (cg * sg)
        dug, dwg = _conv3_grads(dcg, ug, wg, row, S)
        duv, dwv = _conv3_grads(dcv, uv, wv, row, S)
        dup_ref[0] = dug.astype(BF16)
        dup_ref[1] = duv.astype(BF16)

        @pl.when(b == 0)
        def _():
            dwg_ref[...] = dwg
            dwv_ref[...] = dwv

        @pl.when(b != 0)
        def _():
            dwg_ref[...] += dwg
            dwv_ref[...] += dwv

    tap_out = pl.BlockSpec((3, LANES), lambda p, b: (0, p))
    return pl.pallas_call(
        body, grid=(P, B), name=name,
        in_specs=[pl.BlockSpec((2, S, LANES), lambda p, b: (0, b, p)),
                  pl.BlockSpec((None, S, LANES), lambda p, b: (0, b, p)),
                  pl.BlockSpec((3, LANES), lambda p, b: (0, p)),
                  pl.BlockSpec((3, LANES), lambda p, b: (0, P + p))],
        out_specs=[pl.BlockSpec((2, S, LANES), lambda p, b: (0, b, p)), tap_out, tap_out],
        out_shape=[jax.ShapeDtypeStruct((2, T, Fd), BF16),
                   jax.ShapeDtypeStruct((3, Fd), F32), jax.ShapeDtypeStruct((3, Fd), F32)],
        compiler_params=_params(("parallel", "arbitrary")),
    )(up3, dact3, taps, taps)


def _final_norm_loss(x, g, target, tm, name):
    T, D = x.shape

    def body(x_ref, g_ref, t_ref, dx_ref, dg_ref, loss_ref):
        xv = x_ref[...]
        rstd = lax.rsqrt(jnp.mean(xv * xv, axis=-1, keepdims=True) + EPS)
        xn = xv * rstd
        err = xn * g_ref[...] - t_ref[...]
        part = 0.5 * jnp.sum(jnp.mean(err * err, axis=-1, keepdims=True), axis=0, keepdims=True)
        dy = err * (1.0 / D)
        a = dy * g_ref[...]
        dx_ref[...] = rstd * (a - xn * jnp.mean(a * xn, axis=-1, keepdims=True))
        dg = jnp.sum(dy * xn, axis=0, keepdims=True)
        lpart = jnp.broadcast_to(part, (1, LANES))

        @pl.when(pl.program_id(0) == 0)
        def _():
            dg_ref[...] = dg
            loss_ref[...] = lpart

        @pl.when(pl.program_id(0) != 0)
        def _():
            dg_ref[...] += dg
            loss_ref[...] += lpart

    row = pl.BlockSpec((tm, D), lambda i: (i, 0))
    return pl.pallas_call(
        body, grid=(T // tm,), name=name,
        in_specs=[row, pl.BlockSpec((1, D), lambda i: (0, 0)), row],
        out_specs=[row, pl.BlockSpec((1, D), lambda i: (0, 0)), pl.BlockSpec((1, LANES), lambda i: (0, 0))],
        out_shape=[jax.ShapeDtypeStruct((T, D), F32), jax.ShapeDtypeStruct((1, D), F32),
                   jax.ShapeDtypeStruct((1, LANES), F32)],
        compiler_params=_params(("arbitrary",)),
    )(x, g, target)


def _row_tile(rows, cols, budget_elems=512 * 1024):
    tr = rows
    while tr * cols > budget_elems and tr % 32 == 0:
        tr //= 2
    return tr


def _prefetch_call(body, grid, in_specs, out_specs, out_shape, name, sem, aliases=None):
    return pl.pallas_call(
        body, name=name, out_shape=out_shape,
        grid_spec=pltpu.PrefetchScalarGridSpec(num_scalar_prefetch=1, grid=grid, in_specs=in_specs,
                                               out_specs=out_specs),
        input_output_aliases=aliases or {},
        compiler_params=_params(sem))


def _cast_into_full(w, layer, colwise, where, name):
    _, K, N = w.shape
    tr = _row_tile(K, N)
    nrb = K // tr
    full_shape = (1, K, 4 * N) if colwise else (1, 4 * K, N)

    def body(where_ref, w_ref, o_ref):
        del where_ref
        o_ref[...] = w_ref[...].astype(BF16)

    if colwise:
        out_map = lambda i, wh: (0, i, wh[0])
    else:
        out_map = lambda i, wh: (0, wh[0] * nrb + i, 0)
    return _prefetch_call(
        body, (nrb,), [pl.BlockSpec((None, tr, N), lambda i, wh: (layer, i, 0))],
        pl.BlockSpec((None, tr, N), out_map), jax.ShapeDtypeStruct(full_shape, BF16), name,
        ("parallel",))(where, w)


def _chip_sum(g3, other, colwise, where, name):
    L, K, N = g3.shape
    hk, hn = (K // 2, N) if colwise else (K, N // 2)
    tr = _row_tile(hk, hn)
    nrb = hk // tr

    def body(where_ref, g_ref, o_ref, s_ref):
        del where_ref
        s_ref[...] = (g_ref[...].astype(F32) + o_ref[...].astype(F32)).astype(BF16)

    if colwise:
        g_map = lambda l, i, wh: (l, wh[1] * nrb + i, 0)
    else:
        g_map = lambda l, i, wh: (l, i, wh[1])
    blk = pl.BlockSpec((None, tr, hn), lambda l, i, wh: (l, i, 0))
    return _prefetch_call(
        body, (L, nrb), [pl.BlockSpec((None, tr, hn), g_map), blk], blk,
        jax.ShapeDtypeStruct((L, hk, hn), BF16), name, ("parallel", "parallel"))(where, g3, other)


def _owner_sum(chip_sum, received, colwise, where, layer, n_layers, prev, name):
    _, hk, hn = chip_sum.shape
    pk, pn = (hk, hn // 4) if colwise else (hk // 4, hn)
    tr = _row_tile(pk, pn)
    nrb = pk // tr
    shard_shape = (n_layers, 2 * pk, pn) if colwise else (n_layers, pk, 2 * pn)

    def body(where_ref, own_ref, rec_ref, *rest):
        del where_ref
        o_ref = rest[-1]
        acc = own_ref[...].astype(F32)
        for j in range(3):
            acc = acc + rec_ref[j].astype(F32)
        o_ref[...] = acc

    if colwise:
        own_map = lambda i, wh: (0, i, wh[0])
        out_map = lambda i, wh: (layer, wh[1] * nrb + i, 0)
    else:
        own_map = lambda i, wh: (0, wh[0] * nrb + i, 0)
        out_map = lambda i, wh: (layer, i, wh[1])
    in_specs = [pl.BlockSpec((None, tr, pn), own_map),
                pl.BlockSpec((3, None, tr, pn), lambda i, wh: (0, 0, i, 0))]
    operands = [where, chip_sum, received]
    if prev is not None:
        in_specs.append(ANY)
        operands.append(prev)
    return _prefetch_call(
        body, (nrb,), in_specs, pl.BlockSpec((None, tr, pn), out_map), jax.ShapeDtypeStruct(shard_shape, F32), name,
        ("parallel",), None if prev is None else {3: 0})(*operands)


def _adamw(w, g, m, v, name):
    R, Cc = w.shape
    tr = _row_tile(R, Cc, 256 * 1024)

    def body(w_ref, g_ref, m_ref, v_ref, d_ref, nm_ref, nv_ref):
        gv = g_ref[...]
        nm = ADAM_B1 * m_ref[...] + (1.0 - ADAM_B1) * gv
        nv = ADAM_B2 * v_ref[...] + (1.0 - ADAM_B2) * (gv * gv)
        m_hat = nm / (1.0 - ADAM_B1 ** ADAM_STEP)
        v_hat = nv / (1.0 - ADAM_B2 ** ADAM_STEP)
        d_ref[...] = -ADAM_LR * (m_hat / (jnp.sqrt(v_hat) + ADAM_EPS) + ADAM_WD * w_ref[...])
        nm_ref[...] = nm
        nv_ref[...] = nv

    blk = pl.BlockSpec((tr, Cc), lambda i: (i, 0))
    shp = jax.ShapeDtypeStruct((R, Cc), F32)
    return pl.pallas_call(
        body, grid=(R // tr,), name=name,
        in_specs=[blk] * 4, out_specs=[blk] * 3, out_shape=[shp] * 3,
        compiler_params=_params(("parallel",)),
    )(w, g, m, v)


COL_SHARDED = (True, False, True, False)


def _position():
    x, y, c = lax.axis_index("x"), lax.axis_index("y"), lax.axis_index("c")
    chips = [(1 - x, y), (x, 1 - y), (1 - x, 1 - y)]
    return x, y, c, chips


def _span(index, size, align):
    return pl.ds(pl.multiple_of(index * size, align), size)


def _window(ref, colwise, shard, half, shards=4):
    _, K, N = ref.shape
    rows = cols = slice(None)
    if colwise:
        if half is not None:
            rows = _span(half, K // 2, 16)
        if shard is not None:
            cols = _span(shard, N // shards, LANES)
    else:
        if shard is not None:
            rows = _span(shard, K // shards, 16)
        if half is not None:
            cols = _span(half, N // 2, LANES)
    return ref.at[:, rows, cols]


HBM = pl.BlockSpec(memory_space=pltpu.HBM)
SEMAPHORES = pl.BlockSpec(memory_space=pltpu.SEMAPHORE)


def _gather_start(fulls, colwise, group_sizes, after, name):
    n = len(fulls)
    n_groups = len(group_sizes)

    n_in = n if after is None else n + 1

    def body(*refs):
        ins = refs[:n]
        sems = refs[n_in:n_in + 2 * n_groups]
        x, y, c, chips = _position()
        me = 2 * x + y
        i = 0
        for g, size in enumerate(group_sizes):
            for a in range(size):
                win = _window(ins[i], colwise[i], me, c)
                for j, chip in enumerate(chips):
                    pltpu.make_async_remote_copy(
                        src_ref=win, dst_ref=win, send_sem=sems[2 * g].at[a * 3 + j],
                        recv_sem=sems[2 * g + 1].at[a * 3 + j],
                        device_id=(chip[0], chip[1], c), device_id_type=MESH_ID).start()
                i += 1

    sem_shapes = []
    for size in group_sizes:
        sem_shapes += [pltpu.SemaphoreType.DMA((3 * size,)), pltpu.SemaphoreType.DMA((3 * size,))]
    operands = [pltpu.with_memory_space_constraint(f, pltpu.HBM) for f in fulls]
    in_specs = [HBM] * n
    if after is not None:
        operands.append(after)
        in_specs.append(ANY)
    outs = pl.pallas_call(
        body, name=name,
        in_specs=in_specs, out_specs=[SEMAPHORES] * (2 * n_groups) + [HBM] * n,
        out_shape=sem_shapes + [pltpu.HBM(f.shape, f.dtype) for f in fulls],
        input_output_aliases={i: 2 * n_groups + i for i in range(n)},
        compiler_params=pltpu.CompilerParams(has_side_effects=pltpu.SideEffectType.DATAFLOW_SIDE_EFFECTING),
    )(*operands)
    sems = [(outs[2 * g], outs[2 * g + 1]) for g in range(n_groups)]
    return sems, list(outs[2 * n_groups:])


def _to_sibling(ref, colwise, chip, half, x, y, c, send_sem, recv_sem):
    win = _window(ref, colwise, 2 * chip[0] + chip[1], half)
    return pltpu.make_async_remote_copy(
        src_ref=win, dst_ref=win, send_sem=send_sem, recv_sem=recv_sem,
        device_id=(x, y, 1 - c), device_id_type=MESH_ID)


def _gather_pass(in_flight, colwise, sems, after, name):
    n = len(in_flight)

    def body(*refs):
        ins = refs[:n]
        send_sems, recv_sems = refs[n], refs[n + 1]
        pass_send, pass_recv = refs[-2 - n], refs[-1 - n]
        x, y, c, chips = _position()
        me = 2 * x + y
        for a in range(n):
            for j, chip in enumerate(chips):
                k = a * 3 + j
                pltpu.make_async_remote_copy(
                    src_ref=_window(ins[a], colwise[a], me, c),
                    dst_ref=_window(ins[a], colwise[a], 2 * chip[0] + chip[1], c),
                    send_sem=send_sems.at[k], recv_sem=recv_sems.at[k],
                    device_id=(chip[0], chip[1], c), device_id_type=MESH_ID).wait()
                _to_sibling(ins[a], colwise[a], chip, c, x, y, c, pass_send.at[k], pass_recv.at[k]).start()

    operands = list(in_flight) + list(sems)
    in_specs = [HBM] * n + [SEMAPHORES] * 2
    if after is not None:
        operands.append(after)
        in_specs.append(ANY)
    outs = pl.pallas_call(
        body, name=name,
        in_specs=in_specs, out_specs=[SEMAPHORES] * 2 + [HBM] * n,
        out_shape=[pltpu.SemaphoreType.DMA((3 * n,)), pltpu.SemaphoreType.DMA((3 * n,))]
        + [pltpu.HBM(f.shape, f.dtype) for f in in_flight],
        input_output_aliases={i: 2 + i for i in range(n)},
        compiler_params=pltpu.CompilerParams(has_side_effects=pltpu.SideEffectType.DATAFLOW_SIDE_EFFECTING),
    )(*operands)
    return (outs[0], outs[1]), list(outs[2:])


def _gather_wait(in_flight, colwise, sems, after, name):
    n = len(in_flight)

    def body(*refs):
        ins = refs[:n]
        send_sems, recv_sems = refs[n], refs[n + 1]
        x, y, c, chips = _position()
        for a in range(n):
            for j, chip in enumerate(chips):
                k = a * 3 + j
                _to_sibling(ins[a], colwise[a], chip, c, x, y, c, send_sems.at[k], recv_sems.at[k]).wait_send()
                _to_sibling(ins[a], colwise[a], chip, 1 - c, x, y, c, send_sems.at[k], recv_sems.at[k]).wait_recv()

    operands = list(in_flight) + list(sems)
    in_specs = [HBM] * n + [SEMAPHORES] * 2
    if after is not None:
        operands.append(after)
        in_specs.append(ANY)
    outs = pl.pallas_call(
        body, name=name,
        in_specs=in_specs, out_specs=[HBM] * n,
        out_shape=[pltpu.HBM(f.shape, f.dtype) for f in in_flight],
        input_output_aliases={i: i for i in range(n)},
        compiler_params=pltpu.CompilerParams(has_side_effects=pltpu.SideEffectType.DATAFLOW_SIDE_EFFECTING),
    )(*operands)
    return list(outs)


def _exchange_halves(grads, colwise, name):
    n = len(grads)
    out_shapes = []
    for g, cw in zip(grads, colwise):
        L, K, N = g.shape
        out_shapes.append(jax.ShapeDtypeStruct((L, K // 2, N) if cw else (L, K, N // 2), g.dtype))

    def body(*refs):
        g_refs, out = refs[:n], refs[n:2 * n]
        send_sems, recv_sems = refs[2 * n:]
        x, y, c, _ = _position()
        copies = [pltpu.make_async_remote_copy(
            src_ref=_window(g_refs[i], colwise[i], None, 1 - c), dst_ref=out[i],
            send_sem=send_sems.at[i], recv_sem=recv_sems.at[i],
            device_id=(x, y, 1 - c), device_id_type=MESH_ID) for i in range(n)]
        for cp in copies:
            cp.start()
        for cp in copies:
            cp.wait()

    return pl.pallas_call(
        body, name=name,
        in_specs=[ANY] * n, out_specs=[ANY] * n, out_shape=out_shapes,
        scratch_shapes=[pltpu.SemaphoreType.DMA((n,)), pltpu.SemaphoreType.DMA((n,))],
    )(*grads)


def _scatter_copy(src_ref, land_ref, colwise, j, chip, c, send_sem, recv_sem):
    return pltpu.make_async_remote_copy(
        src_ref=_window(src_ref, colwise, 2 * chip[0] + chip[1], None), dst_ref=land_ref.at[j],
        send_sem=send_sem, recv_sem=recv_sem, device_id=(chip[0], chip[1], c), device_id_type=MESH_ID)


def _scatter_start(chip_sums, colwise, name):
    n = len(chip_sums)
    lands = []
    for g, cw in zip(chip_sums, colwise):
        L, hk, hn = g.shape
        lands.append(lax.empty((3, L, hk, hn // 4) if cw else (3, L, hk // 4, hn), g.dtype))

    def body(*refs):
        src, land = refs[:n], refs[n:2 * n]
        send_sems, recv_sems = refs[2 * n], refs[2 * n + 1]
        x, y, c, chips = _position()
        for i in range(n):
            for j, chip in enumerate(chips):
                _scatter_copy(src[i], land[i], colwise[i], j, chip, c, send_sems.at[i * 3 + j],
                              recv_sems.at[i * 3 + j]).start()

    arrays = list(chip_sums) + lands
    outs = pl.pallas_call(
        body, name=name,
        in_specs=[HBM] * (2 * n), out_specs=[SEMAPHORES] * 2 + [HBM] * (2 * n),
        out_shape=[pltpu.SemaphoreType.DMA((3 * n,)), pltpu.SemaphoreType.DMA((3 * n,))]
        + [pltpu.HBM(a.shape, a.dtype) for a in arrays],
        input_output_aliases={i: 2 + i for i in range(2 * n)},
        compiler_params=pltpu.CompilerParams(has_side_effects=pltpu.SideEffectType.DATAFLOW_SIDE_EFFECTING),
    )(*[pltpu.with_memory_space_constraint(a, pltpu.HBM) for a in arrays])
    return (outs[0], outs[1]), list(outs[2:2 + n]), list(outs[2 + n:])


def _scatter_wait(sources, lands, colwise, sems, after, name):
    n = len(sources)

    def body(*refs):
        src, land = refs[:n], refs[n:2 * n]
        send_sems, recv_sems = refs[2 * n], refs[2 * n + 1]
        x, y, c, chips = _position()
        for i in range(n):
            for j, chip in enumerate(chips):
                cp = _scatter_copy(src[i], land[i], colwise[i], j, chip, c, send_sems.at[i * 3 + j],
                                   recv_sems.at[i * 3 + j])
                cp.wait_send()
                cp.wait_recv()

    arrays = list(sources) + list(lands)
    operands = arrays + list(sems)
    in_specs = [HBM] * (2 * n) + [SEMAPHORES] * 2
    if after is not None:
        operands.append(after)
        in_specs.append(ANY)
    outs = pl.pallas_call(
        body, name=name,
        in_specs=in_specs, out_specs=[HBM] * (2 * n),
        out_shape=[pltpu.HBM(a.shape, a.dtype) for a in arrays],
        input_output_aliases={i: i for i in range(2 * n)},
        compiler_params=pltpu.CompilerParams(has_side_effects=pltpu.SideEffectType.DATAFLOW_SIDE_EFFECTING),
    )(*operands)
    return list(outs[:n]), list(outs[n:])


def _share_with_sibling(shards):
    n = len(shards)

    def body(*refs):
        out = refs[n:2 * n]
        send_sems, recv_sems = refs[2 * n:]
        x, y, c, _ = _position()

        def copy(i, half):
            win = _window(out[i], COL_SHARDED[i], None, half)
            return pltpu.make_async_remote_copy(
                src_ref=win, dst_ref=win, send_sem=send_sems.at[i], recv_sem=recv_sems.at[i],
                device_id=(x, y, 1 - c), device_id_type=MESH_ID)

        for i in range(n):
            copy(i, c).start()
        for i in range(n):
            copy(i, 1 - c).wait_recv()
        for i in range(n):
            copy(i, c).wait_send()

    return pl.pallas_call(
        body, name="grad_share_with_sibling",
        in_specs=[ANY] * n, out_specs=[ANY] * n,
        out_shape=[jax.ShapeDtypeStruct(s.shape, s.dtype) for s in shards],
        input_output_aliases={i: i for i in range(n)},
        scratch_shapes=[pltpu.SemaphoreType.DMA((n,)), pltpu.SemaphoreType.DMA((n,))],
    )(*shards)


def _all_reduce_small(pack, name):
    R, Cc = pack.shape
    n_dev = 8

    def body(p_ref, o_ref, buf, send_sems, recv_sems):
        x, y, c, _ = _position()
        me = 4 * x + 2 * y + c
        buf[me] = p_ref[...]

        def peer(k):
            px = 1 - x if k & 4 else x
            py = 1 - y if k & 2 else y
            pc = 1 - c if k & 1 else c
            return px, py, pc

        def copy(k, incoming):
            px, py, pc = peer(k)
            slot = (4 * px + 2 * py + pc) if incoming else me
            return pltpu.make_async_remote_copy(
                src_ref=p_ref, dst_ref=buf.at[slot], send_sem=send_sems.at[k], recv_sem=recv_sems.at[k],
                device_id=(px, py, pc), device_id_type=MESH_ID)

        for k in range(1, n_dev):
            copy(k, False).start()
        for k in range(1, n_dev):
            copy(k, True).wait_recv()
        for k in range(1, n_dev):
            copy(k, False).wait_send()
        acc = buf[0]
        for j in range(1, n_dev):
            acc = acc + buf[j]
        o_ref[...] = acc

    vmem = pl.BlockSpec(memory_space=pltpu.VMEM)
    return pl.pallas_call(
        body, name=name,
        in_specs=[vmem], out_specs=vmem, out_shape=jax.ShapeDtypeStruct((R, Cc), F32),
        scratch_shapes=[pltpu.VMEM((n_dev, R, Cc), F32), pltpu.SemaphoreType.DMA((n_dev,)),
                        pltpu.SemaphoreType.DMA((n_dev,))],
    )(pack)


def _local_forward_backward(x2, target2, S, pass_on, fetch, reduce, layers, final_g, tm=512):
    T, D = x2.shape
    C = D // 2
    n_heads = C // GROUP
    n_layers = len(layers)
    weights = {}
    saved = []
    xc = x2
    for li, lw in enumerate(layers):
        if li == 0:
            pass_on(0, None)
            weights.update(fetch(0, None))
        h1, qkv3, cv3 = _norm_proj(xc, lw["norm1"], weights[li, "w_in"], 0, ((3, C, F32), (3, C, BF16)), tm,
                                   min(C, 512), f"l{li}_norm_in_proj")
        if li == 0:
            pass_on(1, h1)
        o, lse, mix = _attn_fwd(qkv3, lw["attn_g"], 2, S, n_heads, f"l{li}_attn_fwd")
        if li == 0:
            weights.update(fetch(1, o))
            pass_on(2, o)
            pass_on(3, o)
        mix = _mix_conv_fwd(cv3, lw["taps"], lw["conv_g"], mix, S, f"l{li}_mix_conv_fwd")
        x_mid = _proj_residual(mix, weights[li, "w_out"], 0, xc, tm, f"l{li}_out_proj")
        if li == 0:
            weights.update(fetch(2, x_mid))
        Fd = weights[li, "ffn_up"].shape[2] // 2
        h2, up3 = _norm_proj(x_mid, lw["norm2"], weights[li, "ffn_up"], 0, ((2, Fd, BF16),), tm // 2, 256,
                             f"l{li}_norm_ffn_up")
        if li == 0:
            weights.update(fetch(3, up3))
        act = _ffn_act_fwd(up3, lw["ffn_taps"], S, f"l{li}_ffn_act_fwd")
        if li + 1 < n_layers:
            pass_on(li + 4, act)
        x_out = _proj_residual(act.reshape(1, T, Fd), weights[li, "ffn_down"], 0, x_mid, tm, f"l{li}_ffn_down")
        if li + 1 < n_layers:
            weights.update(fetch(li + 4, x_out))
        saved.append(dict(x_in=xc, h1=h1, qkv3=qkv3, cv3=cv3, o=o, lse=lse, mix=mix, x_mid=x_mid, h2=h2, up3=up3,
                          act=act))
        xc = x_out

    dx, d_final_g, loss_part = _final_norm_loss(xc, final_g, target2, tm, "final_norm_loss")

    small = [None] * n_layers
    started = None
    for li in reversed(range(n_layers)):
        lw, sv = layers[li], saved[li]
        w_in, w_out, ffn_up, ffn_down = (weights[li, n] for n in ("w_in", "w_out", "ffn_up", "ffn_down"))
        dxb, dact3 = _grad_through_weight(dx, ffn_down, 0, 1, Fd, BF16, tm, 256, f"l{li}_d_act", started)
        Fd = ffn_down.shape[1]
        d_ffn_down = _weight_grad(sv["act"].reshape(1, T, Fd), dxb.reshape(1, T, D), Fd // 2, D, 1024,
                                  0, 1, None, f"l{li}_d_ffn_down")
        dup3, d_taps_g, d_taps_v = _ffn_act_bwd(sv["up3"], dact3, lw["ffn_taps"], S, f"l{li}_ffn_act_bwd")
        d_ffn_up = _weight_grad(sv["h2"].reshape(1, T, D), dup3, D, Fd // 2, 1024, 0, 1, None,
                                f"l{li}_d_ffn_up")
        started = None
        if li == 0:
            started = reduce({(li, "ffn_down"): d_ffn_down, (li, "ffn_up"): d_ffn_up})
        dx_mid, d_norm2 = _grad_through_proj_norm(dup3, ffn_up, 0, sv["x_mid"], lw["norm2"], dx, tm // 2,
                                                  f"l{li}_d_norm2")
        dxmb, dmix3 = _grad_through_weight(dx_mid, w_out, 0, 2, C, F32, tm, min(C, 512), f"l{li}_d_mix", started)
        d_w_out = _weight_grad(sv["mix"], dxmb.reshape(1, T, D), C, D, 1024, 0, 1, None, f"l{li}_d_w_out")
        dproj, d_attn_g = _attn_bwd(sv["qkv3"], sv["o"], sv["lse"], dmix3, lw["attn_g"], 6, S, n_heads,
                                    f"l{li}_attn_bwd")
        dproj, d_taps, d_conv_g = _mix_conv_bwd(sv["cv3"], dmix3, lw["taps"], lw["conv_g"], dproj, S,
                                                f"l{li}_mix_conv_bwd")
        d_w_in = _weight_grad(sv["h1"].reshape(1, T, D), dproj, D, C, 1024, 0, 1, None, f"l{li}_d_w_in")
        dx, d_norm1 = _grad_through_proj_norm(dproj, w_in, 0, sv["x_in"], lw["norm1"], dx_mid, tm,
                                              f"l{li}_d_norm1")
        if li == 0:
            reduce({(li, "w_out"): d_w_out, (li, "w_in"): d_w_in})
        else:
            started = reduce({(li, "ffn_down"): d_ffn_down, (li, "ffn_up"): d_ffn_up, (li, "w_out"): d_w_out,
                              (li, "w_in"): d_w_in})
        small[li] = dict(norm1=d_norm1, taps=d_taps, attn_g=d_attn_g, conv_g=d_conv_g, norm2=d_norm2,
                         ffn_taps=jnp.concatenate([d_taps_g, d_taps_v], axis=1))
    return loss_part, dx, small, d_final_g


SMALL_ORDER = ("norm1", "attn_g", "conv_g", "norm2", "taps", "ffn_taps")


def _pack_small(small, d_final_g):
    parts = [small[li][k].reshape(-1) for li in range(len(small)) for k in SMALL_ORDER] + [d_final_g.reshape(-1)]
    return jnp.concatenate(parts).reshape(-1, LANES)


def _unpack_small(pack, small, d_final_g):
    flat = pack.reshape(-1)
    out, pos = [dict() for _ in small], 0
    for li in range(len(small)):
        for k in SMALL_ORDER:
            n = small[li][k].size
            out[li][k] = flat[pos:pos + n].reshape(small[li][k].shape)
            pos += n
    return out, flat[pos:pos + d_final_g.size]


def kernel(x, norm1_g, w_in, mix_conv_w, attn_out_g, conv_out_g, w_out, norm2_g, ffn_up, ffn_conv_w, ffn_down, final_norm_g, loss_target, m_norm1_g, m_w_in, m_mix_conv_w, m_attn_out_g, m_conv_out_g, m_w_out, m_norm2_g, m_ffn_up, m_ffn_conv_w, m_ffn_down, m_final_norm_g, v_norm1_g, v_w_in, v_mix_conv_w, v_attn_out_g, v_conv_out_g, v_w_out, v_norm2_g, v_ffn_up, v_ffn_conv_w, v_ffn_down, v_final_norm_g):
    Bl, S, D = x.shape
    L = w_in.shape[0]
    T = Bl * S
    shard = 2 * lax.axis_index("x") + lax.axis_index("y")
    where = jnp.stack([shard, lax.axis_index("c")]).astype(jnp.int32)
    big_names = ("w_in", "w_out", "ffn_up", "ffn_down")

    taps_w, ftaps_w = mix_conv_w.shape[2], ffn_conv_w.shape[2]
    taps_full = jnp.zeros((L, 3, 4 * taps_w), F32)
    taps_full = lax.dynamic_update_slice(taps_full, mix_conv_w, (0, 0, shard * taps_w))
    ftaps_full = jnp.zeros((L, 3, 4 * ftaps_w), F32)
    ftaps_full = lax.dynamic_update_slice(ftaps_full, ffn_conv_w, (0, 0, shard * ftaps_w))
    tap_pack = jnp.concatenate([taps_full.reshape(-1), ftaps_full.reshape(-1)]).reshape(-1, LANES)
    tap_pack = _all_reduce_small(tap_pack * 0.5, "all_gather_taps")
    n_taps = taps_full.size
    taps_full = tap_pack.reshape(-1)[:n_taps].reshape(taps_full.shape)
    ftaps_full = tap_pack.reshape(-1)[n_taps:].reshape(ftaps_full.shape)

    big_shards = dict(zip(big_names, (w_in, w_out, ffn_up, ffn_down)))
    col_of = dict(zip(big_names, COL_SHARDED))
    groups = [[(0, n)] for n in big_names] + [[(l, n) for n in big_names] for l in range(1, L)]
    sems, in_flight = [], {}
    for first, last in ((0, 1), (1, len(groups))):
        keys = [k for g in groups[first:last] for k in g]
        new_sems, arrays = _gather_start(
            [_cast_into_full(big_shards[n], l, col_of[n], where, f"cast_{n}_{l}") for l, n in keys],
            [col_of[n] for _, n in keys], [len(g) for g in groups[first:last]], tap_pack, f"gather_start_{first}")
        sems += new_sems
        in_flight.update(zip(keys, arrays))
    all_started = in_flight[groups[-1][-1]]

    def pass_on(g, after):
        after = all_started if g == 0 else after
        sems[g], arrays = _gather_pass([in_flight[k] for k in groups[g]], [col_of[n] for _, n in groups[g]],
                                       sems[g], after, f"gather_pass_{g}")
        in_flight.update(zip(groups[g], arrays))

    def fetch(g, after):
        done = _gather_wait([in_flight[k] for k in groups[g]], [col_of[n] for _, n in groups[g]], sems[g], after,
                            f"gather_wait_{g}")
        return dict(zip(groups[g], done))

    pending = []

    def reduce(grads):
        g = len(pending)
        keys = list(grads)
        cols = [col_of[n] for _, n in keys]
        others = _exchange_halves([grads[k] for k in keys], cols, f"grad_exchange_halves_{g}")
        chip_sums = [_chip_sum(grads[k], o, cw, where, f"chip_sum_{k[1]}_{k[0]}")
                     for k, o, cw in zip(keys, others, cols)]
        pending.append((keys, cols) + _scatter_start(chip_sums, cols, f"scatter_start_{g}"))
        return pending[-1][3][0]

    layers = [dict(norm1=norm1_g[l:l + 1], taps=taps_full[l], attn_g=attn_out_g[l:l + 1],
                   conv_g=conv_out_g[l:l + 1], norm2=norm2_g[l:l + 1], ffn_taps=ftaps_full[l]) for l in range(L)]

    loss_part, dx, small, d_final_g = _local_forward_backward(
        x.reshape(T, D), loss_target.reshape(T, D), S, pass_on, fetch, reduce, layers, final_norm_g.reshape(1, D))
    loss = lax.psum(loss_part[0, 0], ("x", "y", "c"))

    reduced = dict.fromkeys(big_names)
    last_started = pending[-1][3][0]
    for g, (keys, cols, rs_sems, sources, lands) in enumerate(pending):
        after = last_started if g + 1 < len(pending) else None
        sources, lands = _scatter_wait(sources, lands, cols, rs_sems, after, f"scatter_wait_{g}")
        for (l, n), cw, src, land in zip(keys, cols, sources, lands):
            reduced[n] = _owner_sum(src, land, cw, where, l, L, reduced[n], f"owner_sum_{n}_{l}")
    g_big = _share_with_sibling([reduced[n] for n in big_names])

    pack = _all_reduce_small(_pack_small(small, d_final_g), "all_reduce_small_grads")
    g_small, g_final = _unpack_small(pack, small, d_final_g)

    def stacked(key):
        return jnp.stack([g_small[l][key].reshape(g_small[l][key].shape[-2:] if key.endswith("taps") else (-1,))
                          for l in range(L)])

    g_norm1, g_attn, g_conv, g_norm2 = stacked("norm1"), stacked("attn_g"), stacked("conv_g"), stacked("norm2")
    g_taps = lax.dynamic_slice(stacked("taps"), (0, 0, shard * taps_w), (L, 3, taps_w))
    g_ftaps = lax.dynamic_slice(stacked("ffn_taps"), (0, 0, shard * ftaps_w), (L, 3, ftaps_w))

    grads_out = dict(norm1_g=g_norm1, w_in=g_big[0], mix_conv_w=g_taps, attn_out_g=g_attn, conv_out_g=g_conv,
                     w_out=g_big[1], norm2_g=g_norm2, ffn_up=g_big[2], ffn_conv_w=g_ftaps, ffn_down=g_big[3],
                     final_norm_g=g_final)
    weights = dict(norm1_g=norm1_g, w_in=w_in, mix_conv_w=mix_conv_w, attn_out_g=attn_out_g, conv_out_g=conv_out_g,
                   w_out=w_out, norm2_g=norm2_g, ffn_up=ffn_up, ffn_conv_w=ffn_conv_w, ffn_down=ffn_down,
                   final_norm_g=final_norm_g)
    ms = dict(norm1_g=m_norm1_g, w_in=m_w_in, mix_conv_w=m_mix_conv_w, attn_out_g=m_attn_out_g,
              conv_out_g=m_conv_out_g, w_out=m_w_out, norm2_g=m_norm2_g, ffn_up=m_ffn_up, ffn_conv_w=m_ffn_conv_w,
              ffn_down=m_ffn_down, final_norm_g=m_final_norm_g)
    vs = dict(norm1_g=v_norm1_g, w_in=v_w_in, mix_conv_w=v_mix_conv_w, attn_out_g=v_attn_out_g,
              conv_out_g=v_conv_out_g, w_out=v_w_out, norm2_g=v_norm2_g, ffn_up=v_ffn_up, ffn_conv_w=v_ffn_conv_w,
              ffn_down=v_ffn_down, final_norm_g=v_final_norm_g)
    names = list(weights)
    small_names = [n for n in names if n not in big_names]
    delta, new_m, new_v = {}, {}, {}
    for n in big_names:
        shp = weights[n].shape
        two_d = (shp[0] * shp[1], shp[2])
        d_, m_, v_ = _adamw(weights[n].reshape(two_d), grads_out[n].reshape(two_d), ms[n].reshape(two_d),
                            vs[n].reshape(two_d), f"adamw_{n}")
        delta[n], new_m[n], new_v[n] = d_.reshape(shp), m_.reshape(shp), v_.reshape(shp)

    def packed(tree):
        return jnp.concatenate([tree[n].reshape(-1) for n in small_names]).reshape(-1, LANES)

    d_, m_, v_ = _adamw(packed(weights), packed(grads_out), packed(ms), packed(vs), "adamw_small")
    pos = 0
    for n in small_names:
        size, shp = weights[n].size, weights[n].shape
        delta[n] = d_.reshape(-1)[pos:pos + size].reshape(shp)
        new_m[n] = m_.reshape(-1)[pos:pos + size].reshape(shp)
        new_v[n] = v_.reshape(-1)[pos:pos + size].reshape(shp)
        pos += size

    return (loss, dx.reshape(Bl, S, D), *[grads_out[n] for n in names], *[delta[n] for n in names],
            *[new_m[n] for n in names], *[new_v[n] for n in names])
```

```python
import functools
import math

import jax
import jax.numpy as jnp
from jax import lax
from jax.experimental import pallas as pl
from jax.experimental.pallas import tpu as pltpu

F32 = jnp.float32
BF16 = jnp.bfloat16
EPS = 1e-6
GROUP = 64
LANES = 128
BAND = 128
DILATIONS = (1, 4, 16)
NEG = -1e30
MIB = 1024 * 1024
MESH_ID = pl.DeviceIdType.MESH

ADAM_LR = 0.001
ADAM_B1 = 0.9
ADAM_B2 = 0.999
ADAM_EPS = 1e-08
ADAM_WD = 0.01
ADAM_STEP = 10


ANY = pl.BlockSpec(memory_space=pl.ANY)


def _params(sem=None, vmem_mb=48):
    return pltpu.CompilerParams(dimension_semantics=sem, vmem_limit_bytes=vmem_mb * MIB)


def _nt(a, b):
    return lax.dot_general(a, b, (((1,), (1,)), ((), ())), preferred_element_type=F32)


def _tn(a, b):
    return lax.dot_general(a, b, (((0,), (0,)), ((), ())), preferred_element_type=F32)


def _seg_sum(x, is_a):
    s_a = jnp.sum(jnp.where(is_a, x, 0.0), axis=-1, keepdims=True)
    s_b = jnp.sum(jnp.where(is_a, 0.0, x), axis=-1, keepdims=True)
    return jnp.where(is_a, s_a, s_b)


def _lane_is_a():
    return lax.broadcasted_iota(jnp.int32, (1, LANES), 1) < GROUP


def _norm_proj(x, g, w3, layer, groups, tm, chunk, name):
    T, D = x.shape
    N = w3.shape[2]
    assert sum(p * c for p, c, _ in groups) == N and T % tm == 0

    def body(x_ref, g_ref, w_ref, h_ref, *out_refs):
        xv = x_ref[...]
        rstd = lax.rsqrt(jnp.mean(xv * xv, axis=-1, keepdims=True) + EPS)
        h = ((xv * rstd) * g_ref[...]).astype(BF16)
        h_ref[...] = h
        col = 0
        for (pieces, width, dtype), o_ref in zip(groups, out_refs):
            for p in range(pieces):
                for c0 in range(0, width, chunk):
                    acc = jnp.dot(h, w_ref[:, col + c0:col + c0 + chunk], preferred_element_type=F32)
                    o_ref[p, :, c0:c0 + chunk] = acc.astype(dtype)
                col += width

    out_shape = [jax.ShapeDtypeStruct((T, D), BF16)]
    out_specs = [pl.BlockSpec((tm, D), lambda i: (i, 0))]
    for pieces, width, dtype in groups:
        assert width % chunk == 0
        out_shape.append(jax.ShapeDtypeStruct((pieces, T, width), dtype))
        out_specs.append(pl.BlockSpec((pieces, tm, width), lambda i: (0, i, 0)))
    return pl.pallas_call(
        body, grid=(T // tm,), name=name,
        in_specs=[pl.BlockSpec((tm, D), lambda i: (i, 0)),
                  pl.BlockSpec((1, D), lambda i: (0, 0)),
                  pl.BlockSpec((None, D, N), lambda i: (layer, 0, 0))],
        out_specs=out_specs, out_shape=out_shape,
        compiler_params=_params(("parallel",), 56),
    )(x, g, w3)


def _proj_residual(pieces3, w3, layer, x, tm, name):
    P, T, C = pieces3.shape
    D = w3.shape[2]

    def body(a_ref, w_ref, x_ref, o_ref):
        acc = x_ref[...]
        for p in range(P):
            acc = acc + jnp.dot(a_ref[p], w_ref[p * C:(p + 1) * C, :], preferred_element_type=F32)
        o_ref[...] = acc

    return pl.pallas_call(
        body, grid=(T // tm,), name=name,
        in_specs=[pl.BlockSpec((P, tm, C), lambda i: (0, i, 0)),
                  pl.BlockSpec((None, P * C, D), lambda i: (layer, 0, 0)),
                  pl.BlockSpec((tm, D), lambda i: (i, 0))],
        out_specs=pl.BlockSpec((tm, D), lambda i: (i, 0)),
        out_shape=jax.ShapeDtypeStruct((T, D), F32),
        compiler_params=_params(("parallel",)),
    )(pieces3, w3, x)


def _grad_through_weight(dy, w3, layer, pieces, width, out_dtype, tm, chunk, name, after=None):
    T, D = dy.shape

    def body(dy_ref, w_ref, *rest):
        dyb_ref, o_ref = rest[-2:]
        dyb = dy_ref[...].astype(BF16)
        dyb_ref[...] = dyb
        for p in range(pieces):
            for c0 in range(0, width, chunk):
                r0 = p * width + c0
                o_ref[p, :, c0:c0 + chunk] = _nt(dyb, w_ref[r0:r0 + chunk, :]).astype(out_dtype)

    in_specs = [pl.BlockSpec((tm, D), lambda i: (i, 0)),
                pl.BlockSpec((None, pieces * width, D), lambda i: (layer, 0, 0))]
    operands = [dy, w3]
    if after is not None:
        in_specs.append(ANY)
        operands.append(after)
    return pl.pallas_call(
        body, grid=(T // tm,), name=name,
        in_specs=in_specs,
        out_specs=[pl.BlockSpec((tm, D), lambda i: (i, 0)),
                   pl.BlockSpec((pieces, tm, width), lambda i: (0, i, 0))],
        out_shape=[jax.ShapeDtypeStruct((T, D), BF16),
                   jax.ShapeDtypeStruct((pieces, T, width), out_dtype)],
        compiler_params=_params(("parallel",)),
    )(*operands)


def _grad_through_proj_norm(dp3, w3, layer, x, g, dx_in, tm, name):
    P, T, C = dp3.shape
    D = w3.shape[1]

    def body(dp_ref, w_ref, x_ref, g_ref, dxin_ref, dx_ref, dg_ref):
        dh = _nt(dp_ref[0], w_ref[:, 0:C])
        for p in range(1, P):
            dh = dh + _nt(dp_ref[p], w_ref[:, p * C:(p + 1) * C])
        xv = x_ref[...]
        rstd = lax.rsqrt(jnp.mean(xv * xv, axis=-1, keepdims=True) + EPS)
        xn = xv * rstd
        a = dh * g_ref[...]
        dx_ref[...] = dxin_ref[...] + rstd * (a - xn * jnp.mean(a * xn, axis=-1, keepdims=True))
        part = jnp.sum(dh * xn, axis=0, keepdims=True)

        @pl.when(pl.program_id(0) == 0)
        def _():
            dg_ref[...] = part

        @pl.when(pl.program_id(0) != 0)
        def _():
            dg_ref[...] += part

    return pl.pallas_call(
        body, grid=(T // tm,), name=name,
        in_specs=[pl.BlockSpec((P, tm, C), lambda i: (0, i, 0)),
                  pl.BlockSpec((None, D, P * C), lambda i: (layer, 0, 0)),
                  pl.BlockSpec((tm, D), lambda i: (i, 0)),
                  pl.BlockSpec((1, D), lambda i: (0, 0)),
                  pl.BlockSpec((tm, D), lambda i: (i, 0))],
        out_specs=[pl.BlockSpec((tm, D), lambda i: (i, 0)),
                   pl.BlockSpec((1, D), lambda i: (0, 0))],
        out_shape=[jax.ShapeDtypeStruct((T, D), F32), jax.ShapeDtypeStruct((1, D), F32)],
        compiler_params=_params(("arbitrary",), 56),
    )(dp3, w3, x, g, dx_in)


def _weight_grad(a3, g3, ta, tg, tt, layer, n_layers, prev, name):
    PA, T, CA = a3.shape
    PG, _, CG = g3.shape
    na, ng, nt = CA // ta, CG // tg, T // tt
    assert CA % ta == 0 and CG % tg == 0 and T % tt == 0

    def body(a_ref, g_ref, *rest):
        o_ref, acc_ref = rest[-2:]
        t = pl.program_id(2)
        part = _tn(a_ref[...], g_ref[...])

        @pl.when(t == 0)
        def _():
            acc_ref[...] = part

        @pl.when(t != 0)
        def _():
            acc_ref[...] += part

        @pl.when(t == nt - 1)
        def _():
            o_ref[...] = acc_ref[...].astype(o_ref.dtype)

    in_specs = [pl.BlockSpec((None, tt, ta), lambda i, j, t: (i // na, t, i % na)),
                pl.BlockSpec((None, tt, tg), lambda i, j, t: (j // ng, t, j % ng))]
    operands = [a3, g3]
    if prev is not None:
        in_specs.append(pl.BlockSpec(memory_space=pl.ANY))
        operands.append(prev)
    return pl.pallas_call(
        body, grid=(PA * na, PG * ng, nt), name=name,
        in_specs=in_specs,
        out_specs=pl.BlockSpec((None, ta, tg), lambda i, j, t: (layer, i, j)),
        out_shape=jax.ShapeDtypeStruct((n_layers, PA * CA, PG * CG), BF16),
        scratch_shapes=[pltpu.VMEM((ta, tg), F32)],
        input_output_aliases={} if prev is None else {2: 0},
        compiler_params=_params(("parallel", "parallel", "arbitrary"), 56),
    )(*operands)


def _bias_tables(bm_ref, pair, n_heads):
    ii = lax.broadcasted_iota(jnp.int32, (BAND, 2 * BAND), 0)
    jj = lax.broadcasted_iota(jnp.int32, (BAND, 2 * BAND), 1)
    dist = BAND + ii - jj
    valid = (dist >= 0) & (dist <= BAND)
    distf = dist.astype(F32)
    for hh in range(2):
        head = (2 * pair + hh + 1).astype(F32)
        slope = jnp.exp(jnp.full((1, 1), -8.0 / n_heads * math.log(2.0), F32) * head)
        for bi, d in enumerate(DILATIONS):
            bm_ref[hh * len(DILATIONS) + bi] = jnp.where(valid, -(slope * d) * distf, NEG)


def _gather_residues(dst_ref, src, d, S, convert):
    L = S // d
    for r in range(d):
        rows = pl.ds(r, L, stride=d) if d > 1 else slice(None)
        dst_ref[r * L:(r + 1) * L, :] = convert(src(rows))


def _block_rows(t, d, S):
    nb = S // (BAND * d)
    n = t % nb
    has_prev = jnp.minimum(n, 1)
    cur = pl.ds(pl.multiple_of(t * BAND, BAND), BAND)
    prev = pl.ds(pl.multiple_of((t - has_prev) * BAND, BAND), BAND)
    return cur, prev, has_prev


def _first_block_penalty(has_prev):
    jrow = lax.broadcasted_iota(jnp.int32, (1, 2 * BAND), 1)
    pen = jnp.where(has_prev == 0, NEG, 0.0).astype(F32)
    return jnp.where(jrow < BAND, pen, 0.0)


def _attn_fwd(qkv3, gain, mix_shape_pieces, S, n_heads, name):
    _, T, C = qkv3.shape
    B, P = T // S, C // LANES
    NBLK = S // BAND
    scale = GROUP ** -0.5
    nbr = len(DILATIONS)
    RC = 256

    def body(qkv_ref, g_ref, o_ref, lse_ref, an_ref, qs, ks, vs, op, mp, lp, ob, mb, lb, bm):
        pair = pl.program_id(1)
        is_a = _lane_is_a()
        _bias_tables(bm, pair, n_heads)

        for bi, d in enumerate(DILATIONS):
            nb = S // (BAND * d)
            _gather_residues(qs, lambda rows: qkv_ref.at[0][rows, :], d, S, lambda v: (v * scale).astype(BF16))
            _gather_residues(ks, lambda rows: qkv_ref.at[1][rows, :], d, S, lambda v: v.astype(BF16))
            _gather_residues(vs, lambda rows: qkv_ref.at[2][rows, :], d, S, lambda v: v.astype(BF16))
            o_dst, m_dst, l_dst = (ob.at[bi], mb.at[bi], lb.at[bi]) if d == 1 else (op, mp, lp)

            def block(t, carry, bi=bi, d=d, nb=nb, o_dst=o_dst, m_dst=m_dst, l_dst=l_dst):
                cur, prev, has_prev = _block_rows(t, d, S)
                q = qs[cur, :]
                kc = jnp.concatenate([ks[prev, :], ks[cur, :]], axis=0)
                vc = jnp.concatenate([vs[prev, :], vs[cur, :]], axis=0)
                pen = _first_block_penalty(has_prev)
                outs = []
                for hh in range(2):
                    qm = jnp.where(is_a == (hh == 0), q, jnp.zeros_like(q))
                    s = _nt(qm, kc) + bm[hh * nbr + bi] + pen
                    m = jnp.max(s, axis=-1, keepdims=True)
                    e = jnp.exp(s - m)
                    l = jnp.sum(e, axis=-1, keepdims=True)
                    pv = jnp.dot(e.astype(BF16), vc, preferred_element_type=F32)
                    outs.append((pv, m, l))
                o_dst[cur, :] = jnp.where(is_a, outs[0][0], outs[1][0])
                m_dst[cur, :] = jnp.where(is_a, outs[0][1], outs[1][1])
                l_dst[cur, :] = jnp.where(is_a, outs[0][2], outs[1][2])
                return carry

            lax.fori_loop(0, NBLK, block, 0, unroll=8)
            if d > 1:
                L = S // d
                for r in range(d):
                    rows = pl.ds(r, L, stride=d)
                    ob.at[bi][rows, :] = op[r * L:(r + 1) * L, :]
                    mb.at[bi][rows, :] = mp[r * L:(r + 1) * L, :]
                    lb.at[bi][rows, :] = lp[r * L:(r + 1) * L, :]

        def finish(ci, carry):
            rs = pl.ds(pl.multiple_of(ci * RC, RC), RC)
            ms = [mb[bi, rs, :] for bi in range(nbr)]
            mmax = functools.reduce(jnp.maximum, ms)
            ws = [jnp.exp(m - mmax) for m in ms]
            num = sum(ob[bi, rs, :] * ws[bi] for bi in range(nbr))
            den = sum(lb[bi, rs, :] * ws[bi] for bi in range(nbr))
            o = num / den
            o_ref[rs, :] = o
            lse_ref[rs, :] = mmax + jnp.log(den)
            rstd = lax.rsqrt(_seg_sum(o * o, is_a) * (1.0 / GROUP) + EPS)
            an_ref[rs, :] = ((o * rstd) * g_ref[...]).astype(BF16)
            return carry

        lax.fori_loop(0, S // RC, finish, 0)

    seq = pl.BlockSpec((S, LANES), lambda b, p: (b, p))
    return pl.pallas_call(
        body, grid=(B, P), name=name,
        in_specs=[pl.BlockSpec((3, S, LANES), lambda b, p: (0, b, p)),
                  pl.BlockSpec((1, LANES), lambda b, p: (0, p))],
        out_specs=[seq, seq, pl.BlockSpec((None, S, LANES), lambda b, p: (0, b, p))],
        out_shape=[jax.ShapeDtypeStruct((T, C), F32), jax.ShapeDtypeStruct((T, C), F32),
                   jax.ShapeDtypeStruct((mix_shape_pieces, T, C), BF16)],
        scratch_shapes=[pltpu.VMEM((S, LANES), BF16)] * 3 + [pltpu.VMEM((S, LANES), F32)] * 3
        + [pltpu.VMEM((nbr, S, LANES), F32)] * 3 + [pltpu.VMEM((2 * nbr, BAND, 2 * BAND), F32)],
        compiler_params=_params(("parallel", "parallel")),
    )(qkv3, gain)


def _attn_bwd(qkv3, o, lse, dmix3, gain, dproj_pieces, S, n_heads, name):
    _, T, C = qkv3.shape
    B, P = T // S, C // LANES
    NBLK = S // BAND
    scale = GROUP ** -0.5
    nbr = len(DILATIONS)
    RC = 256

    def body(qkv_ref, o_ref, lse_ref, dn_ref, g_ref, dqkv_ref, dg_ref,
             do_n, dd_n, qs, ks, vs, dos, lses, dds, dqp, dkp, dvp, dqn, dkn, dvn, bm):
        pair = pl.program_id(0)
        b = pl.program_id(1)
        is_a = _lane_is_a()
        _bias_tables(bm, pair, n_heads)

        def prologue(ci, dg_acc):
            rs = pl.ds(pl.multiple_of(ci * RC, RC), RC)
            ov = o_ref[rs, :]
            dn = dn_ref[rs, :]
            rstd = lax.rsqrt(_seg_sum(ov * ov, is_a) * (1.0 / GROUP) + EPS)
            on = ov * rstd
            a = dn * g_ref[...]
            do = rstd * (a - on * (_seg_sum(a * on, is_a) * (1.0 / GROUP)))
            do_n[rs, :] = do
            dd_n[rs, :] = _seg_sum(do * ov, is_a)
            zero = jnp.zeros((RC, LANES), F32)
            dqn[rs, :] = zero
            dkn[rs, :] = zero
            dvn[rs, :] = zero
            return dg_acc + jnp.sum(dn * on, axis=0, keepdims=True)

        dg_part = lax.fori_loop(0, S // RC, prologue, jnp.zeros((1, LANES), F32))

        @pl.when(b == 0)
        def _():
            dg_ref[...] = dg_part

        @pl.when(b != 0)
        def _():
            dg_ref[...] += dg_part

        for bi, d in enumerate(DILATIONS):
            nb = S // (BAND * d)
            L = S // d
            _gather_residues(qs, lambda rows: qkv_ref.at[0][rows, :], d, S, lambda v: (v * scale).astype(BF16))
            _gather_residues(ks, lambda rows: qkv_ref.at[1][rows, :], d, S, lambda v: v.astype(BF16))
            _gather_residues(vs, lambda rows: qkv_ref.at[2][rows, :], d, S, lambda v: v.astype(BF16))
            _gather_residues(dos, lambda rows: do_n[rows, :], d, S, lambda v: v.astype(BF16))
            _gather_residues(lses, lambda rows: lse_ref[rows, :], d, S, lambda v: v)
            _gather_residues(dds, lambda rows: dd_n[rows, :], d, S, lambda v: v)
            dkp[...] = jnp.zeros((S, LANES), F32)
            dvp[...] = jnp.zeros((S, LANES), F32)

            def block(t, carry, bi=bi, d=d, nb=nb):
                cur, prev, has_prev = _block_rows(t, d, S)
                q = qs[cur, :]
                do = dos[cur, :]
                lse_t = lses[cur, :]
                dd_t = dds[cur, :]
                if nb > 1:
                    kc = jnp.concatenate([ks[prev, :], ks[cur, :]], axis=0)
                    vc = jnp.concatenate([vs[prev, :], vs[cur, :]], axis=0)
                    pen = _first_block_penalty(has_prev)
                else:
                    kc, vc = ks[cur, :], vs[cur, :]
                dq = None
                dk = None
                dv = None
                for hh in range(2):
                    mine = is_a == (hh == 0)
                    qm = jnp.where(mine, q, jnp.zeros_like(q))
                    dom = jnp.where(mine, do, jnp.zeros_like(do))
                    c0 = hh * GROUP
                    s = _nt(qm, kc)
                    if nb > 1:
                        s = s + bm[hh * nbr + bi] + pen
                    else:
                        s = s + bm[hh * nbr + bi, :, BAND:2 * BAND]
                    p = jnp.exp(s - lse_t[:, c0:c0 + 1])
                    dp = _nt(dom, vc)
                    ds = (p * (dp - dd_t[:, c0:c0 + 1])).astype(BF16)
                    pb = p.astype(BF16)
                    dq_h = jnp.dot(ds, kc, preferred_element_type=F32)
                    dk_h = _tn(ds, qm)
                    dv_h = _tn(pb, dom)
                    dq = dq_h if dq is None else jnp.where(is_a, dq, dq_h)
                    dk = dk_h if dk is None else dk + dk_h
                    dv = dv_h if dv is None else dv + dv_h
                dqp[cur, :] = dq
                if nb > 1:
                    dkp[prev, :] += dk[0:BAND, :]
                    dvp[prev, :] += dv[0:BAND, :]
                    dkp[cur, :] += dk[BAND:2 * BAND, :]
                    dvp[cur, :] += dv[BAND:2 * BAND, :]
                else:
                    dkp[cur, :] += dk
                    dvp[cur, :] += dv
                return carry

            lax.fori_loop(0, NBLK, block, 0, unroll=4)
            for r in range(d):
                rows = pl.ds(r, L, stride=d) if d > 1 else slice(None)
                dqn[rows, :] += dqp[r * L:(r + 1) * L, :]
                dkn[rows, :] += dkp[r * L:(r + 1) * L, :]
                dvn[rows, :] += dvp[r * L:(r + 1) * L, :]

        dqkv_ref[0] = (dqn[...] * scale).astype(BF16)
        dqkv_ref[1] = dkn[...].astype(BF16)
        dqkv_ref[2] = dvn[...].astype(BF16)

    seq = pl.BlockSpec((S, LANES), lambda p, b: (b, p))
    f32_seq = pltpu.VMEM((S, LANES), F32)
    bf_seq = pltpu.VMEM((S, LANES), BF16)
    return pl.pallas_call(
        body, grid=(P, B), name=name,
        in_specs=[pl.BlockSpec((3, S, LANES), lambda p, b: (0, b, p)), seq, seq,
                  pl.BlockSpec((None, S, LANES), lambda p, b: (0, b, p)),
                  pl.BlockSpec((1, LANES), lambda p, b: (0, p))],
        out_specs=[pl.BlockSpec((3, S, LANES), lambda p, b: (0, b, p)),
                   pl.BlockSpec((1, LANES), lambda p, b: (0, p))],
        out_shape=[jax.ShapeDtypeStruct((dproj_pieces, T, C), BF16), jax.ShapeDtypeStruct((1, C), F32)],
        scratch_shapes=[f32_seq, f32_seq, bf_seq, bf_seq, bf_seq, bf_seq, f32_seq, f32_seq,
                        f32_seq, f32_seq, f32_seq, f32_seq, f32_seq, f32_seq,
                        pltpu.VMEM((2 * nbr, BAND, 2 * BAND), F32)],
        compiler_params=_params(("parallel", "arbitrary")),
    )(qkv3, o, lse, dmix3, gain)


def _delay(x, k, row):
    return jnp.where(row >= k, pltpu.roll(x, k, 0), 0.0)


def _advance(x, k, row, S):
    return jnp.where(row < S - k, pltpu.roll(x, S - k, 0), 0.0)


def _conv3(x, w, row):
    return (w[0:1, :] * _delay(x, 2, row) + w[1:2, :] * _delay(x, 1, row)) + w[2:3, :] * x


def _conv3_grads(dz, x, w, row, S):
    dz1 = _advance(dz, 1, row, S)
    dz2 = _advance(dz, 2, row, S)
    dx = (w[2:3, :] * dz + w[1:2, :] * dz1) + w[0:1, :] * dz2
    dw = jnp.concatenate([jnp.sum(dz2 * x, axis=0, keepdims=True),
                          jnp.sum(dz1 * x, axis=0, keepdims=True),
                          jnp.sum(dz * x, axis=0, keepdims=True)], axis=0)
    return dx, dw


def _mix_conv_fwd(cv3, taps, gain, mix, S, name):
    _, T, C = cv3.shape
    B, P = T // S, C // LANES

    def body(cv_ref, w_ref, g_ref, mix_hbm, y_ref):
        del mix_hbm
        row = lax.broadcasted_iota(jnp.int32, (S, 1), 0)
        is_a = _lane_is_a()
        gb = cv_ref[0].astype(F32)
        c = cv_ref[1].astype(F32) * cv_ref[2].astype(F32)
        y = gb * _conv3(c, w_ref[...], row)
        rstd = lax.rsqrt(_seg_sum(y * y, is_a) * (1.0 / GROUP) + EPS)
        y_ref[...] = ((y * rstd) * g_ref[...]).astype(BF16)

    return pl.pallas_call(
        body, grid=(B, P), name=name,
        in_specs=[pl.BlockSpec((3, S, LANES), lambda b, p: (0, b, p)),
                  pl.BlockSpec((3, LANES), lambda b, p: (0, p)),
                  pl.BlockSpec((1, LANES), lambda b, p: (0, p)),
                  pl.BlockSpec(memory_space=pl.ANY)],
        out_specs=pl.BlockSpec((None, S, LANES), lambda b, p: (1, b, p)),
        out_shape=jax.ShapeDtypeStruct(mix.shape, mix.dtype),
        input_output_aliases={3: 0},
        compiler_params=_params(("parallel", "parallel")),
    )(cv3, taps, gain, mix)


def _mix_conv_bwd(cv3, dmix3, taps, gain, dproj, S, name):
    _, T, C = cv3.shape
    B, P = T // S, C // LANES

    def body(cv_ref, dn_ref, w_ref, g_ref, dproj_hbm, dcv_ref, dw_ref, dg_ref):
        del dproj_hbm
        b = pl.program_id(1)
        row = lax.broadcasted_iota(jnp.int32, (S, 1), 0)
        is_a = _lane_is_a()
        w = w_ref[...]
        gb = cv_ref[0].astype(F32)
        gc = cv_ref[1].astype(F32)
        u = cv_ref[2].astype(F32)
        c = gc * u
        z = _conv3(c, w, row)
        y = gb * z
        rstd = lax.rsqrt(_seg_sum(y * y, is_a) * (1.0 / GROUP) + EPS)
        yn = y * rstd
        dn = dn_ref[...]
        a = dn * g_ref[...]
        dy = rstd * (a - yn * (_seg_sum(a * yn, is_a) * (1.0 / GROUP)))
        dg = jnp.sum(dn * yn, axis=0, keepdims=True)
        dc, dw = _conv3_grads(dy * gb, c, w, row, S)
        dcv_ref[0] = (dy * z).astype(BF16)
        dcv_ref[1] = (dc * u).astype(BF16)
        dcv_ref[2] = (dc * gc).astype(BF16)

        @pl.when(b == 0)
        def _():
            dw_ref[...] = dw
            dg_ref[...] = dg

        @pl.when(b != 0)
        def _():
            dw_ref[...] += dw
            dg_ref[...] += dg

    return pl.pallas_call(
        body, grid=(P, B), name=name,
        in_specs=[pl.BlockSpec((3, S, LANES), lambda p, b: (0, b, p)),
                  pl.BlockSpec((None, S, LANES), lambda p, b: (1, b, p)),
                  pl.BlockSpec((3, LANES), lambda p, b: (0, p)),
                  pl.BlockSpec((1, LANES), lambda p, b: (0, p)),
                  pl.BlockSpec(memory_space=pl.ANY)],
        out_specs=[pl.BlockSpec((3, S, LANES), lambda p, b: (1, b, p)),
                   pl.BlockSpec((3, LANES), lambda p, b: (0, p)),
                   pl.BlockSpec((1, LANES), lambda p, b: (0, p))],
        out_shape=[jax.ShapeDtypeStruct(dproj.shape, dproj.dtype),
                   jax.ShapeDtypeStruct((3, C), F32), jax.ShapeDtypeStruct((1, C), F32)],
        input_output_aliases={4: 0},
        compiler_params=_params(("parallel", "arbitrary")),
    )(cv3, dmix3, taps, gain, dproj)


def _sigmoid(x):
    return 0.5 * jnp.tanh(0.5 * x) + 0.5


def _ffn_act_fwd(up3, taps, S, name):
    _, T, Fd = up3.shape
    B, P = T // S, Fd // LANES

    def body(up_ref, wg_ref, wv_ref, act_ref):
        row = lax.broadcasted_iota(jnp.int32, (S, 1), 0)
        cg = _conv3(up_ref[0].astype(F32), wg_ref[...], row)
        cv = _conv3(up_ref[1].astype(F32), wv_ref[...], row)
        act_ref[...] = ((cg * _sigmoid(cg)) * cv).astype(BF16)

    return pl.pallas_call(
        body, grid=(B, P), name=name,
        in_specs=[pl.BlockSpec((2, S, LANES), lambda b, p: (0, b, p)),
                  pl.BlockSpec((3, LANES), lambda b, p: (0, p)),
                  pl.BlockSpec((3, LANES), lambda b, p: (0, P + p))],
        out_specs=pl.BlockSpec((S, LANES), lambda b, p: (b, p)),
        out_shape=jax.ShapeDtypeStruct((T, Fd), BF16),
        compiler_params=_params(("parallel", "parallel")),
    )(up3, taps, taps)


def _ffn_act_bwd(up3, dact3, taps, S, name):
    _, T, Fd = up3.shape
    B, P = T // S, Fd // LANES

    def body(up_ref, da_ref, wg_ref, wv_ref, dup_ref, dwg_ref, dwv_ref):
        b = pl.program_id(1)
        row = lax.broadcasted_iota(jnp.int32, (S, 1), 0)
        ug = up_ref[0].astype(F32)
        uv = up_ref[1].astype(F32)
        wg = wg_ref[...]
        wv = wv_ref[...]
        cg = _conv3(ug, wg, row)
        cv = _conv3(uv, wv, row)
        sg = _sigmoid(cg)
        da = da_ref[...].astype(F32)
        dcg = (da * cv) * (sg * (1.0 + cg * (1.0 - sg)))
        dcv = da * (cg * sg)
        dug, dwg = _conv3_grads(dcg, ug, wg, row, S)
        duv, dwv = _conv3_grads(dcv, uv, wv, row, S)
        dup_ref[0] = dug.astype(BF16)
        dup_ref[1] = duv.astype(BF16)

        @pl.when(b == 0)
        def _():
            dwg_ref[...] = dwg
            dwv_ref[...] = dwv

        @pl.when(b != 0)
        def _():
            dwg_ref[...] += dwg
            dwv_ref[...] += dwv

    tap_out = pl.BlockSpec((3, LANES), lambda p, b: (0, p))
    return pl.pallas_call(
        body, grid=(P, B), name=name,
        in_specs=[pl.BlockSpec((2, S, LANES), lambda p, b: (0, b, p)),
                  pl.BlockSpec((None, S, LANES), lambda p, b: (0, b, p)),
                  pl.BlockSpec((3, LANES), lambda p, b: (0, p)),
                  pl.BlockSpec((3, LANES), lambda p, b: (0, P + p))],
        out_specs=[pl.BlockSpec((2, S, LANES), lambda p, b: (0, b, p)), tap_out, tap_out],
        out_shape=[jax.ShapeDtypeStruct((2, T, Fd), BF16),
                   jax.ShapeDtypeStruct((3, Fd), F32), jax.ShapeDtypeStruct((3, Fd), F32)],
        compiler_params=_params(("parallel", "arbitrary")),
    )(up3, dact3, taps, taps)


def _final_norm_loss(x, g, target, tm, name):
    T, D = x.shape

    def body(x_ref, g_ref, t_ref, dx_ref, dg_ref, loss_ref):
        xv = x_ref[...]
        rstd = lax.rsqrt(jnp.mean(xv * xv, axis=-1, keepdims=True) + EPS)
        xn = xv * rstd
        err = xn * g_ref[...] - t_ref[...]
        part = 0.5 * jnp.sum(jnp.mean(err * err, axis=-1, keepdims=True), axis=0, keepdims=True)
        dy = err * (1.0 / D)
        a = dy * g_ref[...]
        dx_ref[...] = rstd * (a - xn * jnp.mean(a * xn, axis=-1, keepdims=True))
        dg = jnp.sum(dy * xn, axis=0, keepdims=True)
        lpart = jnp.broadcast_to(part, (1, LANES))

        @pl.when(pl.program_id(0) == 0)
        def _():
            dg_ref[...] = dg
            loss_ref[...] = lpart

        @pl.when(pl.program_id(0) != 0)
        def _():
            dg_ref[...] += dg
            loss_ref[...] += lpart

    row = pl.BlockSpec((tm, D), lambda i: (i, 0))
    return pl.pallas_call(
        body, grid=(T // tm,), name=name,
        in_specs=[row, pl.BlockSpec((1, D), lambda i: (0, 0)), row],
        out_specs=[row, pl.BlockSpec((1, D), lambda i: (0, 0)), pl.BlockSpec((1, LANES), lambda i: (0, 0))],
        out_shape=[jax.ShapeDtypeStruct((T, D), F32), jax.ShapeDtypeStruct((1, D), F32),
                   jax.ShapeDtypeStruct((1, LANES), F32)],
        compiler_params=_params(("arbitrary",)),
    )(x, g, target)


def _row_tile(rows, cols, budget_elems=512 * 1024):
    tr = rows
    while tr * cols > budget_elems and tr % 32 == 0:
        tr //= 2
    return tr


def _prefetch_call(body, grid, in_specs, out_specs, out_shape, name, sem, aliases=None):
    return pl.pallas_call(
        body, name=name, out_shape=out_shape,
        grid_spec=pltpu.PrefetchScalarGridSpec(num_scalar_prefetch=1, grid=grid, in_specs=in_specs,
                                               out_specs=out_specs),
        input_output_aliases=aliases or {},
        compiler_params=_params(sem))


def _cast_into_full(w, layer, colwise, where, name):
    _, K, N = w.shape
    tr = _row_tile(K, N)
    nrb = K // tr
    full_shape = (1, K, 4 * N) if colwise else (1, 4 * K, N)

    def body(where_ref, w_ref, o_ref):
        del where_ref
        o_ref[...] = w_ref[...].astype(BF16)

    if colwise:
        out_map = lambda i, wh: (0, i, wh[0])
    else:
        out_map = lambda i, wh: (0, wh[0] * nrb + i, 0)
    return _prefetch_call(
        body, (nrb,), [pl.BlockSpec((None, tr, N), lambda i, wh: (layer, i, 0))],
        pl.BlockSpec((None, tr, N), out_map), jax.ShapeDtypeStruct(full_shape, BF16), name,
        ("parallel",))(where, w)


def _chip_sum(g3, other, colwise, where, name):
    L, K, N = g3.shape
    hk, hn = (K // 2, N) if colwise else (K, N // 2)
    tr = _row_tile(hk, hn)
    nrb = hk // tr

    def body(where_ref, g_ref, o_ref, s_ref):
        del where_ref
        s_ref[...] = (g_ref[...].astype(F32) + o_ref[...].astype(F32)).astype(BF16)

    if colwise:
        g_map = lambda l, i, wh: (l, wh[1] * nrb + i, 0)
    else:
        g_map = lambda l, i, wh: (l, i, wh[1])
    blk = pl.BlockSpec((None, tr, hn), lambda l, i, wh: (l, i, 0))
    return _prefetch_call(
        body, (L, nrb), [pl.BlockSpec((None, tr, hn), g_map), blk], blk,
        jax.ShapeDtypeStruct((L, hk, hn), BF16), name, ("parallel", "parallel"))(where, g3, other)


def _owner_sum(chip_sum, received, colwise, where, layer, n_layers, prev, name):
    _, hk, hn = chip_sum.shape
    pk, pn = (hk, hn // 4) if colwise else (hk // 4, hn)
    tr = _row_tile(pk, pn)
    nrb = pk // tr
    shard_shape = (n_layers, 2 * pk, pn) if colwise else (n_layers, pk, 2 * pn)

    def body(where_ref, own_ref, rec_ref, *rest):
        del where_ref
        o_ref = rest[-1]
        acc = own_ref[...].astype(F32)
        for j in range(3):
            acc = acc + rec_ref[j].astype(F32)
        o_ref[...] = acc

    if colwise:
        own_map = lambda i, wh: (0, i, wh[0])
        out_map = lambda i, wh: (layer, wh[1] * nrb + i, 0)
    else:
        own_map = lambda i, wh: (0, wh[0] * nrb + i, 0)
        out_map = lambda i, wh: (layer, i, wh[1])
    in_specs = [pl.BlockSpec((None, tr, pn), own_map),
                pl.BlockSpec((3, None, tr, pn), lambda i, wh: (0, 0, i, 0))]
    operands = [where, chip_sum, received]
    if prev is not None:
        in_specs.append(ANY)
        operands.append(prev)
    return _prefetch_call(
        body, (nrb,), in_specs, pl.BlockSpec((None, tr, pn), out_map), jax.ShapeDtypeStruct(shard_shape, F32), name,
        ("parallel",), None if prev is None else {3: 0})(*operands)


def _adamw(w, g, m, v, name):
    R, Cc = w.shape
    tr = _row_tile(R, Cc, 256 * 1024)

    def body(w_ref, g_ref, m_ref, v_ref, d_ref, nm_ref, nv_ref):
        gv = g_ref[...]
        nm = ADAM_B1 * m_ref[...] + (1.0 - ADAM_B1) * gv
        nv = ADAM_B2 * v_ref[...] + (1.0 - ADAM_B2) * (gv * gv)
        m_hat = nm / (1.0 - ADAM_B1 ** ADAM_STEP)
        v_hat = nv / (1.0 - ADAM_B2 ** ADAM_STEP)
        d_ref[...] = -ADAM_LR * (m_hat / (jnp.sqrt(v_hat) + ADAM_EPS) + ADAM_WD * w_ref[...])
        nm_ref[...] = nm
        nv_ref[...] = nv

    blk = pl.BlockSpec((tr, Cc), lambda i: (i, 0))
    shp = jax.ShapeDtypeStruct((R, Cc), F32)
    return pl.pallas_call(
        body, grid=(R // tr,), name=name,
        in_specs=[blk] * 4, out_specs=[blk] * 3, out_shape=[shp] * 3,
        compiler_params=_params(("parallel",)),
    )(w, g, m, v)


COL_SHARDED = (True, False, True, False)


def _position():
    x, y, c = lax.axis_index("x"), lax.axis_index("y"), lax.axis_index("c")
    chips = [(1 - x, y), (x, 1 - y), (1 - x, 1 - y)]
    return x, y, c, chips


def _span(index, size, align):
    return pl.ds(pl.multiple_of(index * size, align), size)


def _window(ref, colwise, shard, half, shards=4):
    _, K, N = ref.shape
    rows = cols = slice(None)
    if colwise:
        if half is not None:
            rows = _span(half, K // 2, 16)
        if shard is not None:
            cols = _span(shard, N // shards, LANES)
    else:
        if shard is not None:
            rows = _span(shard, K // shards, 16)
        if half is not None:
            cols = _span(half, N // 2, LANES)
    return ref.at[:, rows, cols]


HBM = pl.BlockSpec(memory_space=pltpu.HBM)
SEMAPHORES = pl.BlockSpec(memory_space=pltpu.SEMAPHORE)


def _gather_start(fulls, colwise, group_sizes, after, name):
    n = len(fulls)
    n_groups = len(group_sizes)

    n_in = n if after is None else n + 1

    def body(*refs):
        ins = refs[:n]
        sems = refs[n_in:n_in + 2 * n_groups]
        x, y, c, chips = _position()
        me = 2 * x + y
        i = 0
        for g, size in enumerate(group_sizes):
            for a in range(size):
                win = _window(ins[i], colwise[i], me, c)
                for j, chip in enumerate(chips):
                    pltpu.make_async_remote_copy(
                        src_ref=win, dst_ref=win, send_sem=sems[2 * g].at[a * 3 + j],
                        recv_sem=sems[2 * g + 1].at[a * 3 + j],
                        device_id=(chip[0], chip[1], c), device_id_type=MESH_ID).start()
                i += 1

    sem_shapes = []
    for size in group_sizes:
        sem_shapes += [pltpu.SemaphoreType.DMA((3 * size,)), pltpu.SemaphoreType.DMA((3 * size,))]
    operands = [pltpu.with_memory_space_constraint(f, pltpu.HBM) for f in fulls]
    in_specs = [HBM] * n
    if after is not None:
        operands.append(after)
        in_specs.append(ANY)
    outs = pl.pallas_call(
        body, name=name,
        in_specs=in_specs, out_specs=[SEMAPHORES] * (2 * n_groups) + [HBM] * n,
        out_shape=sem_shapes + [pltpu.HBM(f.shape, f.dtype) for f in fulls],
        input_output_aliases={i: 2 * n_groups + i for i in range(n)},
        compiler_params=pltpu.CompilerParams(has_side_effects=pltpu.SideEffectType.DATAFLOW_SIDE_EFFECTING),
    )(*operands)
    sems = [(outs[2 * g], outs[2 * g + 1]) for g in range(n_groups)]
    return sems, list(outs[2 * n_groups:])


def _to_sibling(ref, colwise, chip, half, x, y, c, send_sem, recv_sem):
    win = _window(ref, colwise, 2 * chip[0] + chip[1], half)
    return pltpu.make_async_remote_copy(
        src_ref=win, dst_ref=win, send_sem=send_sem, recv_sem=recv_sem,
        device_id=(x, y, 1 - c), device_id_type=MESH_ID)


def _gather_pass(in_flight, colwise, sems, after, name):
    n = len(in_flight)

    def body(*refs):
        ins = refs[:n]
        send_sems, recv_sems = refs[n], refs[n + 1]
        pass_send, pass_recv = refs[-2 - n], refs[-1 - n]
        x, y, c, chips = _position()
        me = 2 * x + y
        for a in range(n):
            for j, chip in enumerate(chips):
                k = a * 3 + j
                pltpu.make_async_remote_copy(
                    src_ref=_window(ins[a], colwise[a], me, c),
                    dst_ref=_window(ins[a], colwise[a], 2 * chip[0] + chip[1], c),
                    send_sem=send_sems.at[k], recv_sem=recv_sems.at[k],
                    device_id=(chip[0], chip[1], c), device_id_type=MESH_ID).wait()
                _to_sibling(ins[a], colwise[a], chip, c, x, y, c, pass_send.at[k], pass_recv.at[k]).start()

    operands = list(in_flight) + list(sems)
    in_specs = [HBM] * n + [SEMAPHORES] * 2
    if after is not None:
        operands.append(after)
        in_specs.append(ANY)
    outs = pl.pallas_call(
        body, name=name,
        in_specs=in_specs, out_specs=[SEMAPHORES] * 2 + [HBM] * n,
        out_shape=[pltpu.SemaphoreType.DMA((3 * n,)), pltpu.SemaphoreType.DMA((3 * n,))]
        + [pltpu.HBM(f.shape, f.dtype) for f in in_flight],
        input_output_aliases={i: 2 + i for i in range(n)},
        compiler_params=pltpu.CompilerParams(has_side_effects=pltpu.SideEffectType.DATAFLOW_SIDE_EFFECTING),
    )(*operands)
    return (outs[0], outs[1]), list(outs[2:])


def _gather_wait(in_flight, colwise, sems, after, name):
    n = len(in_flight)

    def body(*refs):
        ins = refs[:n]
        send_sems, recv_sems = refs[n], refs[n + 1]
        x, y, c, chips = _position()
        for a in range(n):
            for j, chip in enumerate(chips):
                k = a * 3 + j
                _to_sibling(ins[a], colwise[a], chip, c, x, y, c, send_sems.at[k], recv_sems.at[k]).wait_send()
                _to_sibling(ins[a], colwise[a], chip, 1 - c, x, y, c, send_sems.at[k], recv_sems.at[k]).wait_recv()

    operands = list(in_flight) + list(sems)
    in_specs = [HBM] * n + [SEMAPHORES] * 2
    if after is not None:
        operands.append(after)
        in_specs.append(ANY)
    outs = pl.pallas_call(
        body, name=name,
        in_specs=in_specs, out_specs=[HBM] * n,
        out_shape=[pltpu.HBM(f.shape, f.dtype) for f in in_flight],
        input_output_aliases={i: i for i in range(n)},
        compiler_params=pltpu.CompilerParams(has_side_effects=pltpu.SideEffectType.DATAFLOW_SIDE_EFFECTING),
    )(*operands)
    return list(outs)


def _exchange_halves(grads, colwise, name):
    n = len(grads)
    out_shapes = []
    for g, cw in zip(grads, colwise):
        L, K, N = g.shape
        out_shapes.append(jax.ShapeDtypeStruct((L, K // 2, N) if cw else (L, K, N // 2), g.dtype))

    def body(*refs):
        g_refs, out = refs[:n], refs[n:2 * n]
        send_sems, recv_sems = refs[2 * n:]
        x, y, c, _ = _position()
        copies = [pltpu.make_async_remote_copy(
            src_ref=_window(g_refs[i], colwise[i], None, 1 - c), dst_ref=out[i],
            send_sem=send_sems.at[i], recv_sem=recv_sems.at[i],
            device_id=(x, y, 1 - c), device_id_type=MESH_ID) for i in range(n)]
        for cp in copies:
            cp.start()
        for cp in copies:
            cp.wait()

    return pl.pallas_call(
        body, name=name,
        in_specs=[ANY] * n, out_specs=[ANY] * n, out_shape=out_shapes,
        scratch_shapes=[pltpu.SemaphoreType.DMA((n,)), pltpu.SemaphoreType.DMA((n,))],
    )(*grads)


def _scatter_copy(src_ref, land_ref, colwise, j, chip, c, send_sem, recv_sem):
    return pltpu.make_async_remote_copy(
        src_ref=_window(src_ref, colwise, 2 * chip[0] + chip[1], None), dst_ref=land_ref.at[j],
        send_sem=send_sem, recv_sem=recv_sem, device_id=(chip[0], chip[1], c), device_id_type=MESH_ID)


def _scatter_start(chip_sums, colwise, name):
    n = len(chip_sums)
    lands = []
    for g, cw in zip(chip_sums, colwise):
        L, hk, hn = g.shape
        lands.append(lax.empty((3, L, hk, hn // 4) if cw else (3, L, hk // 4, hn), g.dtype))

    def body(*refs):
        src, land = refs[:n], refs[n:2 * n]
        send_sems, recv_sems = refs[2 * n], refs[2 * n + 1]
        x, y, c, chips = _position()
        for i in range(n):
            for j, chip in enumerate(chips):
                _scatter_copy(src[i], land[i], colwise[i], j, chip, c, send_sems.at[i * 3 + j],
                              recv_sems.at[i * 3 + j]).start()

    arrays = list(chip_sums) + lands
    outs = pl.pallas_call(
        body, name=name,
        in_specs=[HBM] * (2 * n), out_specs=[SEMAPHORES] * 2 + [HBM] * (2 * n),
        out_shape=[pltpu.SemaphoreType.DMA((3 * n,)), pltpu.SemaphoreType.DMA((3 * n,))]
        + [pltpu.HBM(a.shape, a.dtype) for a in arrays],
        input_output_aliases={i: 2 + i for i in range(2 * n)},
        compiler_params=pltpu.CompilerParams(has_side_effects=pltpu.SideEffectType.DATAFLOW_SIDE_EFFECTING),
    )(*[pltpu.with_memory_space_constraint(a, pltpu.HBM) for a in arrays])
    return (outs[0], outs[1]), list(outs[2:2 + n]), list(outs[2 + n:])


def _scatter_wait(sources, lands, colwise, sems, after, name):
    n = len(sources)

    def body(*refs):
        src, land = refs[:n], refs[n:2 * n]
        send_sems, recv_sems = refs[2 * n], refs[2 * n + 1]
        x, y, c, chips = _position()
        for i in range(n):
            for j, chip in enumerate(chips):
                cp = _scatter_copy(src[i], land[i], colwise[i], j, chip, c, send_sems.at[i * 3 + j],
                                   recv_sems.at[i * 3 + j])
                cp.wait_send()
                cp.wait_recv()

    arrays = list(sources) + list(lands)
    operands = arrays + list(sems)
    in_specs = [HBM] * (2 * n) + [SEMAPHORES] * 2
    if after is not None:
        operands.append(after)
        in_specs.append(ANY)
    outs = pl.pallas_call(
        body, name=name,
        in_specs=in_specs, out_specs=[HBM] * (2 * n),
        out_shape=[pltpu.HBM(a.shape, a.dtype) for a in arrays],
        input_output_aliases={i: i for i in range(2 * n)},
        compiler_params=pltpu.CompilerParams(has_side_effects=pltpu.SideEffectType.DATAFLOW_SIDE_EFFECTING),
    )(*operands)
    return list(outs[:n]), list(outs[n:])


def _share_with_sibling(shards):
    n = len(shards)

    def body(*refs):
        out = refs[n:2 * n]
        send_sems, recv_sems = refs[2 * n:]
        x, y, c, _ = _position()

        def copy(i, half):
            win = _window(out[i], COL_SHARDED[i], None, half)
            return pltpu.make_async_remote_copy(
                src_ref=win, dst_ref=win, send_sem=send_sems.at[i], recv_sem=recv_sems.at[i],
                device_id=(x, y, 1 - c), device_id_type=MESH_ID)

        for i in range(n):
            copy(i, c).start()
        for i in range(n):
            copy(i, 1 - c).wait_recv()
        for i in range(n):
            copy(i, c).wait_send()

    return pl.pallas_call(
        body, name="grad_share_with_sibling",
        in_specs=[ANY] * n, out_specs=[ANY] * n,
        out_shape=[jax.ShapeDtypeStruct(s.shape, s.dtype) for s in shards],
        input_output_aliases={i: i for i in range(n)},
        scratch_shapes=[pltpu.SemaphoreType.DMA((n,)), pltpu.SemaphoreType.DMA((n,))],
    )(*shards)


def _all_reduce_small(pack, name):
    R, Cc = pack.shape
    n_dev = 8

    def body(p_ref, o_ref, buf, send_sems, recv_sems):
        x, y, c, _ = _position()
        me = 4 * x + 2 * y + c
        buf[me] = p_ref[...]

        def peer(k):
            px = 1 - x if k & 4 else x
            py = 1 - y if k & 2 else y
            pc = 1 - c if k & 1 else c
            return px, py, pc

        def copy(k, incoming):
            px, py, pc = peer(k)
            slot = (4 * px + 2 * py + pc) if incoming else me
            return pltpu.make_async_remote_copy(
                src_ref=p_ref, dst_ref=buf.at[slot], send_sem=send_sems.at[k], recv_sem=recv_sems.at[k],
                device_id=(px, py, pc), device_id_type=MESH_ID)

        for k in range(1, n_dev):
            copy(k, False).start()
        for k in range(1, n_dev):
            copy(k, True).wait_recv()
        for k in range(1, n_dev):
            copy(k, False).wait_send()
        acc = buf[0]
        for j in range(1, n_dev):
            acc = acc + buf[j]
        o_ref[...] = acc

    vmem = pl.BlockSpec(memory_space=pltpu.VMEM)
    return pl.pallas_call(
        body, name=name,
        in_specs=[vmem], out_specs=vmem, out_shape=jax.ShapeDtypeStruct((R, Cc), F32),
        scratch_shapes=[pltpu.VMEM((n_dev, R, Cc), F32), pltpu.SemaphoreType.DMA((n_dev,)),
                        pltpu.SemaphoreType.DMA((n_dev,))],
    )(pack)


def _local_forward_backward(x2, target2, S, pass_on, fetch, reduce, layers, final_g, tm=512):
    T, D = x2.shape
    C = D // 2
    n_heads = C // GROUP
    n_layers = len(layers)
    weights = {}
    saved = []
    xc = x2
    for li, lw in enumerate(layers):
        if li == 0:
            pass_on(0, None)
            weights.update(fetch(0, None))
        h1, qkv3, cv3 = _norm_proj(xc, lw["norm1"], weights[li, "w_in"], 0, ((3, C, F32), (3, C, BF16)), tm,
                                   min(C, 512), f"l{li}_norm_in_proj")
        if li == 0:
            pass_on(1, h1)
        o, lse, mix = _attn_fwd(qkv3, lw["attn_g"], 2, S, n_heads, f"l{li}_attn_fwd")
        if li == 0:
            weights.update(fetch(1, o))
            pass_on(2, o)
            pass_on(3, o)
        mix = _mix_conv_fwd(cv3, lw["taps"], lw["conv_g"], mix, S, f"l{li}_mix_conv_fwd")
        x_mid = _proj_residual(mix, weights[li, "w_out"], 0, xc, tm, f"l{li}_out_proj")
        if li == 0:
            weights.update(fetch(2, x_mid))
        Fd = weights[li, "ffn_up"].shape[2] // 2
        h2, up3 = _norm_proj(x_mid, lw["norm2"], weights[li, "ffn_up"], 0, ((2, Fd, BF16),), tm // 2, 256,
                             f"l{li}_norm_ffn_up")
        if li == 0:
            weights.update(fetch(3, up3))
        act = _ffn_act_fwd(up3, lw["ffn_taps"], S, f"l{li}_ffn_act_fwd")
        if li + 1 < n_layers:
            pass_on(li + 4, act)
        x_out = _proj_residual(act.reshape(1, T, Fd), weights[li, "ffn_down"], 0, x_mid, tm, f"l{li}_ffn_down")
        if li + 1 < n_layers:
            weights.update(fetch(li + 4, x_out))
        saved.append(dict(x_in=xc, h1=h1, qkv3=qkv3, cv3=cv3, o=o, lse=lse, mix=mix, x_mid=x_mid, h2=h2, up3=up3,
                          act=act))
        xc = x_out

    dx, d_final_g, loss_part = _final_norm_loss(xc, final_g, target2, tm, "final_norm_loss")

    small = [None] * n_layers
    started = None
    for li in reversed(range(n_layers)):
        lw, sv = layers[li], saved[li]
        w_in, w_out, ffn_up, ffn_down = (weights[li, n] for n in ("w_in", "w_out", "ffn_up", "ffn_down"))
        dxb, dact3 = _grad_through_weight(dx, ffn_down, 0, 1, Fd, BF16, tm, 256, f"l{li}_d_act", started)
        Fd = ffn_down.shape[1]
        d_ffn_down = _weight_grad(sv["act"].reshape(1, T, Fd), dxb.reshape(1, T, D), Fd // 2, D, 1024,
                                  0, 1, None, f"l{li}_d_ffn_down")
        dup3, d_taps_g, d_taps_v = _ffn_act_bwd(sv["up3"], dact3, lw["ffn_taps"], S, f"l{li}_ffn_act_bwd")
        d_ffn_up = _weight_grad(sv["h2"].reshape(1, T, D), dup3, D, Fd // 2, 1024, 0, 1, None,
                                f"l{li}_d_ffn_up")
        started = None
        if li == 0:
            started = reduce({(li, "ffn_down"): d_ffn_down, (li, "ffn_up"): d_ffn_up})
        dx_mid, d_norm2 = _grad_through_proj_norm(dup3, ffn_up, 0, sv["x_mid"], lw["norm2"], dx, tm // 2,
                                                  f"l{li}_d_norm2")
        dxmb, dmix3 = _grad_through_weight(dx_mid, w_out, 0, 2, C, F32, tm, min(C, 512), f"l{li}_d_mix", started)
        d_w_out = _weight_grad(sv["mix"], dxmb.reshape(1, T, D), C, D, 1024, 0, 1, None, f"l{li}_d_w_out")
        dproj, d_attn_g = _attn_bwd(sv["qkv3"], sv["o"], sv["lse"], dmix3, lw["attn_g"], 6, S, n_heads,
                                    f"l{li}_attn_bwd")
        dproj, d_taps, d_conv_g = _mix_conv_bwd(sv["cv3"], dmix3, lw["taps"], lw["conv_g"], dproj, S,
                                                f"l{li}_mix_conv_bwd")
        d_w_in = _weight_grad(sv["h1"].reshape(1, T, D), dproj, D, C, 1024, 0, 1, None, f"l{li}_d_w_in")
        dx, d_norm1 = _grad_through_proj_norm(dproj, w_in, 0, sv["x_in"], lw["norm1"], dx_mid, tm,
                                              f"l{li}_d_norm1")
        if li == 0:
            reduce({(li, "w_out"): d_w_out, (li, "w_in"): d_w_in})
        else:
            started = reduce({(li, "ffn_down"): d_ffn_down, (li, "ffn_up"): d_ffn_up, (li, "w_out"): d_w_out,
                              (li, "w_in"): d_w_in})
        small[li] = dict(norm1=d_norm1, taps=d_taps, attn_g=d_attn_g, conv_g=d_conv_g, norm2=d_norm2,
                         ffn_taps=jnp.concatenate([d_taps_g, d_taps_v], axis=1))
    return loss_part, dx, small, d_final_g


SMALL_ORDER = ("norm1", "attn_g", "conv_g", "norm2", "taps", "ffn_taps")


def _pack_small(small, d_final_g):
    parts = [small[li][k].reshape(-1) for li in range(len(small)) for k in SMALL_ORDER] + [d_final_g.reshape(-1)]
    return jnp.concatenate(parts).reshape(-1, LANES)


def _unpack_small(pack, small, d_final_g):
    flat = pack.reshape(-1)
    out, pos = [dict() for _ in small], 0
    for li in range(len(small)):
        for k in SMALL_ORDER:
            n = small[li][k].size
            out[li][k] = flat[pos:pos + n].reshape(small[li][k].shape)
            pos += n
    return out, flat[pos:pos + d_final_g.size]


def kernel(x, norm1_g, w_in, mix_conv_w, attn_out_g, conv_out_g, w_out, norm2_g, ffn_up, ffn_conv_w, ffn_down, final_norm_g, loss_target, m_norm1_g, m_w_in, m_mix_conv_w, m_attn_out_g, m_conv_out_g, m_w_out, m_norm2_g, m_ffn_up, m_ffn_conv_w, m_ffn_down, m_final_norm_g, v_norm1_g, v_w_in, v_mix_conv_w, v_attn_out_g, v_conv_out_g, v_w_out, v_norm2_g, v_ffn_up, v_ffn_conv_w, v_ffn_down, v_final_norm_g):
    Bl, S, D = x.shape
    L = w_in.shape[0]
    T = Bl * S
    shard = 2 * lax.axis_index("x") + lax.axis_index("y")
    where = jnp.stack([shard, lax.axis_index("c")]).astype(jnp.int32)
    big_names = ("w_in", "w_out", "ffn_up", "ffn_down")

    taps_w, ftaps_w = mix_conv_w.shape[2], ffn_conv_w.shape[2]
    taps_full = jnp.zeros((L, 3, 4 * taps_w), F32)
    taps_full = lax.dynamic_update_slice(taps_full, mix_conv_w, (0, 0, shard * taps_w))
    ftaps_full = jnp.zeros((L, 3, 4 * ftaps_w), F32)
    ftaps_full = lax.dynamic_update_slice(ftaps_full, ffn_conv_w, (0, 0, shard * ftaps_w))
    tap_pack = jnp.concatenate([taps_full.reshape(-1), ftaps_full.reshape(-1)]).reshape(-1, LANES)
    tap_pack = _all_reduce_small(tap_pack * 0.5, "all_gather_taps")
    n_taps = taps_full.size
    taps_full = tap_pack.reshape(-1)[:n_taps].reshape(taps_full.shape)
    ftaps_full = tap_pack.reshape(-1)[n_taps:].reshape(ftaps_full.shape)

    big_shards = dict(zip(big_names, (w_in, w_out, ffn_up, ffn_down)))
    col_of = dict(zip(big_names, COL_SHARDED))
    groups = [[(0, n)] for n in big_names] + [[(l, n) for n in big_names] for l in range(1, L)]
    sems, in_flight = [], {}
    all_started = tap_pack
    for first, last in ((0, 1), (1, len(groups))):
        keys = [k for g in groups[first:last] for k in g]
        new_sems, arrays = _gather_start(
            [_cast_into_full(big_shards[n], l, col_of[n], where, f"cast_{n}_{l}") for l, n in keys],
            [col_of[n] for _, n in keys], [len(g) for g in groups[first:last]], all_started,
            f"gather_start_{first}")
        sems += new_sems
        in_flight.update(zip(keys, arrays))
        all_started = arrays[-1]

    def pass_on(g, after):
        after = all_started if g == 0 else after
        sems[g], arrays = _gather_pass([in_flight[k] for k in groups[g]], [col_of[n] for _, n in groups[g]],
                                       sems[g], after, f"gather_pass_{g}")
        in_flight.update(zip(groups[g], arrays))

    def fetch(g, after):
        done = _gather_wait([in_flight[k] for k in groups[g]], [col_of[n] for _, n in groups[g]], sems[g], after,
                            f"gather_wait_{g}")
        return dict(zip(groups[g], done))

    pending = []

    def reduce(grads):
        g = len(pending)
        keys = list(grads)
        cols = [col_of[n] for _, n in keys]
        others = _exchange_halves([grads[k] for k in keys], cols, f"grad_exchange_halves_{g}")
        chip_sums = [_chip_sum(grads[k], o, cw, where, f"chip_sum_{k[1]}_{k[0]}")
                     for k, o, cw in zip(keys, others, cols)]
        pending.append((keys, cols) + _scatter_start(chip_sums, cols, f"scatter_start_{g}"))
        return pending[-1][3][0]

    layers = [dict(norm1=norm1_g[l:l + 1], taps=taps_full[l], attn_g=attn_out_g[l:l + 1],
                   conv_g=conv_out_g[l:l + 1], norm2=norm2_g[l:l + 1], ffn_taps=ftaps_full[l]) for l in range(L)]

    loss_part, dx, small, d_final_g = _local_forward_backward(
        x.reshape(T, D), loss_target.reshape(T, D), S, pass_on, fetch, reduce, layers, final_norm_g.reshape(1, D))
    loss = lax.psum(loss_part[0, 0], ("x", "y", "c"))

    reduced = dict.fromkeys(big_names)
    last_started = pending[-1][3][0]
    for g, (keys, cols, rs_sems, sources, lands) in enumerate(pending):
        after = last_started if g + 1 < len(pending) else None
        sources, lands = _scatter_wait(sources, lands, cols, rs_sems, after, f"scatter_wait_{g}")
        for (l, n), cw, src, land in zip(keys, cols, sources, lands):
            reduced[n] = _owner_sum(src, land, cw, where, l, L, reduced[n], f"owner_sum_{n}_{l}")
    g_big = _share_with_sibling([reduced[n] for n in big_names])

    pack = _all_reduce_small(_pack_small(small, d_final_g), "all_reduce_small_grads")
    g_small, g_final = _unpack_small(pack, small, d_final_g)

    def stacked(key):
        return jnp.stack([g_small[l][key].reshape(g_small[l][key].shape[-2:] if key.endswith("taps") else (-1,))
                          for l in range(L)])

    g_norm1, g_attn, g_conv, g_norm2 = stacked("norm1"), stacked("attn_g"), stacked("conv_g"), stacked("norm2")
    g_taps = lax.dynamic_slice(stacked("taps"), (0, 0, shard * taps_w), (L, 3, taps_w))
    g_ftaps = lax.dynamic_slice(stacked("ffn_taps"), (0, 0, shard * ftaps_w), (L, 3, ftaps_w))

    grads_out = dict(norm1_g=g_norm1, w_in=g_big[0], mix_conv_w=g_taps, attn_out_g=g_attn, conv_out_g=g_conv,
                     w_out=g_big[1], norm2_g=g_norm2, ffn_up=g_big[2], ffn_conv_w=g_ftaps, ffn_down=g_big[3],
                     final_norm_g=g_final)
    weights = dict(norm1_g=norm1_g, w_in=w_in, mix_conv_w=mix_conv_w, attn_out_g=attn_out_g, conv_out_g=conv_out_g,
                   w_out=w_out, norm2_g=norm2_g, ffn_up=ffn_up, ffn_conv_w=ffn_conv_w, ffn_down=ffn_down,
                   final_norm_g=final_norm_g)
    ms = dict(norm1_g=m_norm1_g, w_in=m_w_in, mix_conv_w=m_mix_conv_w, attn_out_g=m_attn_out_g,
              conv_out_g=m_conv_out_g, w_out=m_w_out, norm2_g=m_norm2_g, ffn_up=m_ffn_up, ffn_conv_w=m_ffn_conv_w,
              ffn_down=m_ffn_down, final_norm_g=m_final_norm_g)
    vs = dict(norm1_g=v_norm1_g, w_in=v_w_in, mix_conv_w=v_mix_conv_w, attn_out_g=v_attn_out_g,
              conv_out_g=v_conv_out_g, w_out=v_w_out, norm2_g=v_norm2_g, ffn_up=v_ffn_up, ffn_conv_w=v_ffn_conv_w,
              ffn_down=v_ffn_down, final_norm_g=v_final_norm_g)
    names = list(weights)
    small_names = [n for n in names if n not in big_names]
    delta, new_m, new_v = {}, {}, {}
    for n in big_names:
        shp = weights[n].shape
        two_d = (shp[0] * shp[1], shp[2])
        d_, m_, v_ = _adamw(weights[n].reshape(two_d), grads_out[n].reshape(two_d), ms[n].reshape(two_d),
                            vs[n].reshape(two_d), f"adamw_{n}")
        delta[n], new_m[n], new_v[n] = d_.reshape(shp), m_.reshape(shp), v_.reshape(shp)

    def packed(tree):
        return jnp.concatenate([tree[n].reshape(-1) for n in small_names]).reshape(-1, LANES)

    d_, m_, v_ = _adamw(packed(weights), packed(grads_out), packed(ms), packed(vs), "adamw_small")
    pos = 0
    for n in small_names:
        size, shp = weights[n].size, weights[n].shape
        delta[n] = d_.reshape(-1)[pos:pos + size].reshape(shp)
        new_m[n] = m_.reshape(-1)[pos:pos + size].reshape(shp)
        new_v[n] = v_.reshape(-1)[pos:pos + size].reshape(shp)
        pos += size

    return (loss, dx.reshape(Bl, S, D), *[grads_out[n] for n in names], *[delta[n] for n in names],
            *[new_m[n] for n in names], *[new_v[n] for n in names])
```

```python
import functools
import math

import jax
import jax.numpy as jnp
from jax import lax
from jax.experimental import pallas as pl
from jax.experimental.pallas import tpu as pltpu

F32 = jnp.float32
BF16 = jnp.bfloat16
EPS = 1e-6
GROUP = 64
LANES = 128
BAND = 128
DILATIONS = (1, 4, 16)
NEG = -1e30
MIB = 1024 * 1024
MESH_ID = pl.DeviceIdType.MESH

ADAM_LR = 0.001
ADAM_B1 = 0.9
ADAM_B2 = 0.999
ADAM_EPS = 1e-08
ADAM_WD = 0.01
ADAM_STEP = 10


ANY = pl.BlockSpec(memory_space=pl.ANY)


def _params(sem=None, vmem_mb=48):
    return pltpu.CompilerParams(dimension_semantics=sem, vmem_limit_bytes=vmem_mb * MIB)


def _nt(a, b):
    return lax.dot_general(a, b, (((1,), (1,)), ((), ())), preferred_element_type=F32)


def _tn(a, b):
    return lax.dot_general(a, b, (((0,), (0,)), ((), ())), preferred_element_type=F32)


def _seg_sum(x, is_a):
    s_a = jnp.sum(jnp.where(is_a, x, 0.0), axis=-1, keepdims=True)
    s_b = jnp.sum(jnp.where(is_a, 0.0, x), axis=-1, keepdims=True)
    return jnp.where(is_a, s_a, s_b)


def _lane_is_a():
    return lax.broadcasted_iota(jnp.int32, (1, LANES), 1) < GROUP


def _norm_proj(x, g, w3, layer, groups, tm, chunk, name):
    T, D = x.shape
    N = w3.shape[2]
    assert sum(p * c for p, c, _ in groups) == N and T % tm == 0

    def body(x_ref, g_ref, w_ref, h_ref, *out_refs):
        xv = x_ref[...]
        rstd = lax.rsqrt(jnp.mean(xv * xv, axis=-1, keepdims=True) + EPS)
        h = ((xv * rstd) * g_ref[...]).astype(BF16)
        h_ref[...] = h
        col = 0
        for (pieces, width, dtype), o_ref in zip(groups, out_refs):
            for p in range(pieces):
                for c0 in range(0, width, chunk):
                    acc = jnp.dot(h, w_ref[:, col + c0:col + c0 + chunk], preferred_element_type=F32)
                    o_ref[p, :, c0:c0 + chunk] = acc.astype(dtype)
                col += width

    out_shape = [jax.ShapeDtypeStruct((T, D), BF16)]
    out_specs = [pl.BlockSpec((tm, D), lambda i: (i, 0))]
    for pieces, width, dtype in groups:
        assert width % chunk == 0
        out_shape.append(jax.ShapeDtypeStruct((pieces, T, width), dtype))
        out_specs.append(pl.BlockSpec((pieces, tm, width), lambda i: (0, i, 0)))
    return pl.pallas_call(
        body, grid=(T // tm,), name=name,
        in_specs=[pl.BlockSpec((tm, D), lambda i: (i, 0)),
                  pl.BlockSpec((1, D), lambda i: (0, 0)),
                  pl.BlockSpec((None, D, N), lambda i: (layer, 0, 0))],
        out_specs=out_specs, out_shape=out_shape,
        compiler_params=_params(("parallel",), 56),
    )(x, g, w3)


def _proj_residual(pieces3, w3, layer, x, tm, name):
    P, T, C = pieces3.shape
    D = w3.shape[2]

    def body(a_ref, w_ref, x_ref, o_ref):
        acc = x_ref[...]
        for p in range(P):
            acc = acc + jnp.dot(a_ref[p], w_ref[p * C:(p + 1) * C, :], preferred_element_type=F32)
        o_ref[...] = acc

    return pl.pallas_call(
        body, grid=(T // tm,), name=name,
        in_specs=[pl.BlockSpec((P, tm, C), lambda i: (0, i, 0)),
                  pl.BlockSpec((None, P * C, D), lambda i: (layer, 0, 0)),
                  pl.BlockSpec((tm, D), lambda i: (i, 0))],
        out_specs=pl.BlockSpec((tm, D), lambda i: (i, 0)),
        out_shape=jax.ShapeDtypeStruct((T, D), F32),
        compiler_params=_params(("parallel",)),
    )(pieces3, w3, x)


def _grad_through_weight(dy, w3, layer, pieces, width, out_dtype, tm, chunk, name, after=None):
    T, D = dy.shape

    def body(dy_ref, w_ref, *rest):
        dyb_ref, o_ref = rest[-2:]
        dyb = dy_ref[...].astype(BF16)
        dyb_ref[...] = dyb
        for p in range(pieces):
            for c0 in range(0, width, chunk):
                r0 = p * width + c0
                o_ref[p, :, c0:c0 + chunk] = _nt(dyb, w_ref[r0:r0 + chunk, :]).astype(out_dtype)

    in_specs = [pl.BlockSpec((tm, D), lambda i: (i, 0)),
                pl.BlockSpec((None, pieces * width, D), lambda i: (layer, 0, 0))]
    operands = [dy, w3]
    if after is not None:
        in_specs.append(ANY)
        operands.append(after)
    return pl.pallas_call(
        body, grid=(T // tm,), name=name,
        in_specs=in_specs,
        out_specs=[pl.BlockSpec((tm, D), lambda i: (i, 0)),
                   pl.BlockSpec((pieces, tm, width), lambda i: (0, i, 0))],
        out_shape=[jax.ShapeDtypeStruct((T, D), BF16),
                   jax.ShapeDtypeStruct((pieces, T, width), out_dtype)],
        compiler_params=_params(("parallel",)),
    )(*operands)


def _grad_through_proj_norm(dp3, w3, layer, x, g, dx_in, tm, name):
    P, T, C = dp3.shape
    D = w3.shape[1]

    def body(dp_ref, w_ref, x_ref, g_ref, dxin_ref, dx_ref, dg_ref):
        dh = _nt(dp_ref[0], w_ref[:, 0:C])
        for p in range(1, P):
            dh = dh + _nt(dp_ref[p], w_ref[:, p * C:(p + 1) * C])
        xv = x_ref[...]
        rstd = lax.rsqrt(jnp.mean(xv * xv, axis=-1, keepdims=True) + EPS)
        xn = xv * rstd
        a = dh * g_ref[...]
        dx_ref[...] = dxin_ref[...] + rstd * (a - xn * jnp.mean(a * xn, axis=-1, keepdims=True))
        part = jnp.sum(dh * xn, axis=0, keepdims=True)

        @pl.when(pl.program_id(0) == 0)
        def _():
            dg_ref[...] = part

        @pl.when(pl.program_id(0) != 0)
        def _():
            dg_ref[...] += part

    return pl.pallas_call(
        body, grid=(T // tm,), name=name,
        in_specs=[pl.BlockSpec((P, tm, C), lambda i: (0, i, 0)),
                  pl.BlockSpec((None, D, P * C), lambda i: (layer, 0, 0)),
                  pl.BlockSpec((tm, D), lambda i: (i, 0)),
                  pl.BlockSpec((1, D), lambda i: (0, 0)),
                  pl.BlockSpec((tm, D), lambda i: (i, 0))],
        out_specs=[pl.BlockSpec((tm, D), lambda i: (i, 0)),
                   pl.BlockSpec((1, D), lambda i: (0, 0))],
        out_shape=[jax.ShapeDtypeStruct((T, D), F32), jax.ShapeDtypeStruct((1, D), F32)],
        compiler_params=_params(("arbitrary",), 56),
    )(dp3, w3, x, g, dx_in)


def _weight_grad(a3, g3, ta, tg, tt, layer, n_layers, prev, name):
    PA, T, CA = a3.shape
    PG, _, CG = g3.shape
    na, ng, nt = CA // ta, CG // tg, T // tt
    assert CA % ta == 0 and CG % tg == 0 and T % tt == 0

    def body(a_ref, g_ref, *rest):
        o_ref, acc_ref = rest[-2:]
        t = pl.program_id(2)
        part = _tn(a_ref[...], g_ref[...])

        @pl.when(t == 0)
        def _():
            acc_ref[...] = part

        @pl.when(t != 0)
        def _():
            acc_ref[...] += part

        @pl.when(t == nt - 1)
        def _():
            o_ref[...] = acc_ref[...].astype(o_ref.dtype)

    in_specs = [pl.BlockSpec((None, tt, ta), lambda i, j, t: (i // na, t, i % na)),
                pl.BlockSpec((None, tt, tg), lambda i, j, t: (j // ng, t, j % ng))]
    operands = [a3, g3]
    if prev is not None:
        in_specs.append(pl.BlockSpec(memory_space=pl.ANY))
        operands.append(prev)
    return pl.pallas_call(
        body, grid=(PA * na, PG * ng, nt), name=name,
        in_specs=in_specs,
        out_specs=pl.BlockSpec((None, ta, tg), lambda i, j, t: (layer, i, j)),
        out_shape=jax.ShapeDtypeStruct((n_layers, PA * CA, PG * CG), BF16),
        scratch_shapes=[pltpu.VMEM((ta, tg), F32)],
        input_output_aliases={} if prev is None else {2: 0},
        compiler_params=_params(("parallel", "parallel", "arbitrary"), 56),
    )(*operands)


def _bias_tables(bm_ref, pair, n_heads):
    ii = lax.broadcasted_iota(jnp.int32, (BAND, 2 * BAND), 0)
    jj = lax.broadcasted_iota(jnp.int32, (BAND, 2 * BAND), 1)
    dist = BAND + ii - jj
    valid = (dist >= 0) & (dist <= BAND)
    distf = dist.astype(F32)
    for hh in range(2):
        head = (2 * pair + hh + 1).astype(F32)
        slope = jnp.exp(jnp.full((1, 1), -8.0 / n_heads * math.log(2.0), F32) * head)
        for bi, d in enumerate(DILATIONS):
            bm_ref[bi, hh * BAND:(hh + 1) * BAND, :] = jnp.where(valid, -(slope * d) * distf, NEG)


def _stack_heads(x, is_a):
    zero = jnp.zeros_like(x)
    return jnp.concatenate([jnp.where(is_a, x, zero), jnp.where(is_a, zero, x)], axis=0)


def _unstack_heads(x2, is_a):
    return jnp.where(is_a, x2[0:BAND], x2[BAND:2 * BAND])


def _gather_residues(dst_ref, src, d, S, convert):
    L = S // d
    for r in range(d):
        rows = pl.ds(r, L, stride=d) if d > 1 else slice(None)
        dst_ref[r * L:(r + 1) * L, :] = convert(src(rows))


def _block_rows(t, d, S):
    nb = S // (BAND * d)
    n = t % nb
    has_prev = jnp.minimum(n, 1)
    cur = pl.ds(pl.multiple_of(t * BAND, BAND), BAND)
    prev = pl.ds(pl.multiple_of((t - has_prev) * BAND, BAND), BAND)
    return cur, prev, has_prev


def _first_block_penalty(has_prev):
    jrow = lax.broadcasted_iota(jnp.int32, (1, 2 * BAND), 1)
    pen = jnp.where(has_prev == 0, NEG, 0.0).astype(F32)
    return jnp.where(jrow < BAND, pen, 0.0)


def _attn_fwd(qkv3, gain, mix_shape_pieces, S, n_heads, name):
    _, T, C = qkv3.shape
    B, P = T // S, C // LANES
    NBLK = S // BAND
    scale = GROUP ** -0.5
    nbr = len(DILATIONS)
    RC = 256

    def body(qkv_ref, g_ref, o_ref, lse_ref, an_ref, qs, ks, vs, op, mp, lp, ob, mb, lb, bm):
        pair = pl.program_id(1)
        is_a = _lane_is_a()
        _bias_tables(bm, pair, n_heads)

        for bi, d in enumerate(DILATIONS):
            nb = S // (BAND * d)
            _gather_residues(qs, lambda rows: qkv_ref.at[0][rows, :], d, S, lambda v: (v * scale).astype(BF16))
            _gather_residues(ks, lambda rows: qkv_ref.at[1][rows, :], d, S, lambda v: v.astype(BF16))
            _gather_residues(vs, lambda rows: qkv_ref.at[2][rows, :], d, S, lambda v: v.astype(BF16))
            o_dst, m_dst, l_dst = (ob.at[bi], mb.at[bi], lb.at[bi]) if d == 1 else (op, mp, lp)

            def block(t, carry, bi=bi, d=d, nb=nb, o_dst=o_dst, m_dst=m_dst, l_dst=l_dst):
                cur, prev, has_prev = _block_rows(t, d, S)
                q2 = _stack_heads(qs[cur, :], is_a)
                kc = jnp.concatenate([ks[prev, :], ks[cur, :]], axis=0)
                vc = jnp.concatenate([vs[prev, :], vs[cur, :]], axis=0)
                s = _nt(q2, kc) + bm[bi] + _first_block_penalty(has_prev)
                m = jnp.max(s, axis=-1, keepdims=True)
                e = jnp.exp(s - m)
                l = jnp.sum(e, axis=-1, keepdims=True)
                pv = jnp.dot(e.astype(BF16), vc, preferred_element_type=F32)
                o_dst[cur, :] = _unstack_heads(pv, is_a)
                m_dst[cur, :] = _unstack_heads(m, is_a)
                l_dst[cur, :] = _unstack_heads(l, is_a)
                return carry

            lax.fori_loop(0, NBLK, block, 0, unroll=8)
            if d > 1:
                L = S // d
                for r in range(d):
                    rows = pl.ds(r, L, stride=d)
                    ob.at[bi][rows, :] = op[r * L:(r + 1) * L, :]
                    mb.at[bi][rows, :] = mp[r * L:(r + 1) * L, :]
                    lb.at[bi][rows, :] = lp[r * L:(r + 1) * L, :]

        def finish(ci, carry):
            rs = pl.ds(pl.multiple_of(ci * RC, RC), RC)
            ms = [mb[bi, rs, :] for bi in range(nbr)]
            mmax = functools.reduce(jnp.maximum, ms)
            ws = [jnp.exp(m - mmax) for m in ms]
            num = sum(ob[bi, rs, :] * ws[bi] for bi in range(nbr))
            den = sum(lb[bi, rs, :] * ws[bi] for bi in range(nbr))
            o = num / den
            o_ref[rs, :] = o
            lse_ref[rs, :] = mmax + jnp.log(den)
            rstd = lax.rsqrt(_seg_sum(o * o, is_a) * (1.0 / GROUP) + EPS)
            an_ref[rs, :] = ((o * rstd) * g_ref[...]).astype(BF16)
            return carry

        lax.fori_loop(0, S // RC, finish, 0)

    seq = pl.BlockSpec((S, LANES), lambda b, p: (b, p))
    return pl.pallas_call(
        body, grid=(B, P), name=name,
        in_specs=[pl.BlockSpec((3, S, LANES), lambda b, p: (0, b, p)),
                  pl.BlockSpec((1, LANES), lambda b, p: (0, p))],
        out_specs=[seq, seq, pl.BlockSpec((None, S, LANES), lambda b, p: (0, b, p))],
        out_shape=[jax.ShapeDtypeStruct((T, C), F32), jax.ShapeDtypeStruct((T, C), F32),
                   jax.ShapeDtypeStruct((mix_shape_pieces, T, C), BF16)],
        scratch_shapes=[pltpu.VMEM((S, LANES), BF16)] * 3 + [pltpu.VMEM((S, LANES), F32)] * 3
        + [pltpu.VMEM((nbr, S, LANES), F32)] * 3 + [pltpu.VMEM((nbr, 2 * BAND, 2 * BAND), F32)],
        compiler_params=_params(("parallel", "parallel")),
    )(qkv3, gain)


def _attn_bwd(qkv3, o, lse, dmix3, gain, dproj_pieces, S, n_heads, name):
    _, T, C = qkv3.shape
    B, P = T // S, C // LANES
    NBLK = S // BAND
    scale = GROUP ** -0.5
    nbr = len(DILATIONS)
    RC = 256

    def body(qkv_ref, o_ref, lse_ref, dn_ref, g_ref, dqkv_ref, dg_ref,
             do_n, dd_n, qs, ks, vs, dos, lses, dds, dqp, dkp, dvp, dqn, dkn, dvn, bm):
        pair = pl.program_id(0)
        b = pl.program_id(1)
        is_a = _lane_is_a()
        _bias_tables(bm, pair, n_heads)

        def prologue(ci, dg_acc):
            rs = pl.ds(pl.multiple_of(ci * RC, RC), RC)
            ov = o_ref[rs, :]
            dn = dn_ref[rs, :]
            rstd = lax.rsqrt(_seg_sum(ov * ov, is_a) * (1.0 / GROUP) + EPS)
            on = ov * rstd
            a = dn * g_ref[...]
            do = rstd * (a - on * (_seg_sum(a * on, is_a) * (1.0 / GROUP)))
            do_n[rs, :] = do
            dd_n[rs, :] = _seg_sum(do * ov, is_a)
            zero = jnp.zeros((RC, LANES), F32)
            dqn[rs, :] = zero
            dkn[rs, :] = zero
            dvn[rs, :] = zero
            return dg_acc + jnp.sum(dn * on, axis=0, keepdims=True)

        dg_part = lax.fori_loop(0, S // RC, prologue, jnp.zeros((1, LANES), F32))

        @pl.when(b == 0)
        def _():
            dg_ref[...] = dg_part

        @pl.when(b != 0)
        def _():
            dg_ref[...] += dg_part

        for bi, d in enumerate(DILATIONS):
            nb = S // (BAND * d)
            L = S // d
            _gather_residues(qs, lambda rows: qkv_ref.at[0][rows, :], d, S, lambda v: (v * scale).astype(BF16))
            _gather_residues(ks, lambda rows: qkv_ref.at[1][rows, :], d, S, lambda v: v.astype(BF16))
            _gather_residues(vs, lambda rows: qkv_ref.at[2][rows, :], d, S, lambda v: v.astype(BF16))
            _gather_residues(dos, lambda rows: do_n[rows, :], d, S, lambda v: v.astype(BF16))
            if d == 1:
                lse_src, dd_src, dq_dst, dk_dst, dv_dst = lse_ref, dd_n, dqn, dkn, dvn
            else:
                _gather_residues(lses, lambda rows: lse_ref[rows, :], d, S, lambda v: v)
                _gather_residues(dds, lambda rows: dd_n[rows, :], d, S, lambda v: v)
                dkp[...] = jnp.zeros((S, LANES), F32)
                dvp[...] = jnp.zeros((S, LANES), F32)
                lse_src, dd_src, dq_dst, dk_dst, dv_dst = lses, dds, dqp, dkp, dvp

            def block(t, carry, bi=bi, d=d, lse_src=lse_src, dd_src=dd_src, dq_dst=dq_dst, dk_dst=dk_dst,
                      dv_dst=dv_dst):
                cur, prev, has_prev = _block_rows(t, d, S)
                q2 = _stack_heads(qs[cur, :], is_a)
                do2 = _stack_heads(dos[cur, :], is_a)
                lse_t = lse_src[cur, :]
                dd_t = dd_src[cur, :]
                lse2 = jnp.concatenate([lse_t[:, 0:1], lse_t[:, GROUP:GROUP + 1]], axis=0)
                dd2 = jnp.concatenate([dd_t[:, 0:1], dd_t[:, GROUP:GROUP + 1]], axis=0)
                kc = jnp.concatenate([ks[prev, :], ks[cur, :]], axis=0)
                vc = jnp.concatenate([vs[prev, :], vs[cur, :]], axis=0)
                s = _nt(q2, kc) + bm[bi] + _first_block_penalty(has_prev)
                p = jnp.exp(s - lse2)
                ds = (p * (_nt(do2, vc) - dd2)).astype(BF16)
                dq = _unstack_heads(jnp.dot(ds, kc, preferred_element_type=F32), is_a)
                dk = _tn(ds, q2)
                dv = _tn(p.astype(BF16), do2)
                dq_dst[cur, :] = dq
                dk_dst[prev, :] += dk[0:BAND, :]
                dv_dst[prev, :] += dv[0:BAND, :]
                dk_dst[cur, :] += dk[BAND:2 * BAND, :]
                dv_dst[cur, :] += dv[BAND:2 * BAND, :]
                return carry

            lax.fori_loop(0, NBLK, block, 0, unroll=8)
            if d > 1:
                for r in range(d):
                    rows = pl.ds(r, L, stride=d)
                    dqn[rows, :] += dqp[r * L:(r + 1) * L, :]
                    dkn[rows, :] += dkp[r * L:(r + 1) * L, :]
                    dvn[rows, :] += dvp[r * L:(r + 1) * L, :]

        dqkv_ref[0] = (dqn[...] * scale).astype(BF16)
        dqkv_ref[1] = dkn[...].astype(BF16)
        dqkv_ref[2] = dvn[...].astype(BF16)

    seq = pl.BlockSpec((S, LANES), lambda p, b: (b, p))
    f32_seq = pltpu.VMEM((S, LANES), F32)
    bf_seq = pltpu.VMEM((S, LANES), BF16)
    return pl.pallas_call(
        body, grid=(P, B), name=name,
        in_specs=[pl.BlockSpec((3, S, LANES), lambda p, b: (0, b, p)), seq, seq,
                  pl.BlockSpec((None, S, LANES), lambda p, b: (0, b, p)),
                  pl.BlockSpec((1, LANES), lambda p, b: (0, p))],
        out_specs=[pl.BlockSpec((3, S, LANES), lambda p, b: (0, b, p)),
                   pl.BlockSpec((1, LANES), lambda p, b: (0, p))],
        out_shape=[jax.ShapeDtypeStruct((dproj_pieces, T, C), BF16), jax.ShapeDtypeStruct((1, C), F32)],
        scratch_shapes=[f32_seq, f32_seq, bf_seq, bf_seq, bf_seq, bf_seq, f32_seq, f32_seq,
                        f32_seq, f32_seq, f32_seq, f32_seq, f32_seq, f32_seq,
                        pltpu.VMEM((nbr, 2 * BAND, 2 * BAND), F32)],
        compiler_params=_params(("parallel", "arbitrary")),
    )(qkv3, o, lse, dmix3, gain)


def _delay(x, k, row):
    return jnp.where(row >= k, pltpu.roll(x, k, 0), 0.0)


def _advance(x, k, row, S):
    return jnp.where(row < S - k, pltpu.roll(x, S - k, 0), 0.0)


def _conv3(x, w, row):
    return (w[0:1, :] * _delay(x, 2, row) + w[1:2, :] * _delay(x, 1, row)) + w[2:3, :] * x


def _conv3_grads(dz, x, w, row, S):
    dz1 = _advance(dz, 1, row, S)
    dz2 = _advance(dz, 2, row, S)
    dx = (w[2:3, :] * dz + w[1:2, :] * dz1) + w[0:1, :] * dz2
    dw = jnp.concatenate([jnp.sum(dz2 * x, axis=0, keepdims=True),
                          jnp.sum(dz1 * x, axis=0, keepdims=True),
                          jnp.sum(dz * x, axis=0, keepdims=True)], axis=0)
    return dx, dw


def _mix_conv_fwd(cv3, taps, gain, mix, S, name):
    _, T, C = cv3.shape
    B, P = T // S, C // LANES

    def body(cv_ref, w_ref, g_ref, mix_hbm, y_ref):
        del mix_hbm
        row = lax.broadcasted_iota(jnp.int32, (S, 1), 0)
        is_a = _lane_is_a()
        gb = cv_ref[0].astype(F32)
        c = cv_ref[1].astype(F32) * cv_ref[2].astype(F32)
        y = gb * _conv3(c, w_ref[...], row)
        rstd = lax.rsqrt(_seg_sum(y * y, is_a) * (1.0 / GROUP) + EPS)
        y_ref[...] = ((y * rstd) * g_ref[...]).astype(BF16)

    return pl.pallas_call(
        body, grid=(B, P), name=name,
        in_specs=[pl.BlockSpec((3, S, LANES), lambda b, p: (0, b, p)),
                  pl.BlockSpec((3, LANES), lambda b, p: (0, p)),
                  pl.BlockSpec((1, LANES), lambda b, p: (0, p)),
                  pl.BlockSpec(memory_space=pl.ANY)],
        out_specs=pl.BlockSpec((None, S, LANES), lambda b, p: (1, b, p)),
        out_shape=jax.ShapeDtypeStruct(mix.shape, mix.dtype),
        input_output_aliases={3: 0},
        compiler_params=_params(("parallel", "parallel")),
    )(cv3, taps, gain, mix)


def _mix_conv_bwd(cv3, dmix3, taps, gain, dproj, S, name):
    _, T, C = cv3.shape
    B, P = T // S, C // LANES

    def body(cv_ref, dn_ref, w_ref, g_ref, dproj_hbm, dcv_ref, dw_ref, dg_ref):
        del dproj_hbm
        b = pl.program_id(1)
        row = lax.broadcasted_iota(jnp.int32, (S, 1), 0)
        is_a = _lane_is_a()
        w = w_ref[...]
        gb = cv_ref[0].astype(F32)
        gc = cv_ref[1].astype(F32)
        u = cv_ref[2].astype(F32)
        c = gc * u
        z = _conv3(c, w, row)
        y = gb * z
        rstd = lax.rsqrt(_seg_sum(y * y, is_a) * (1.0 / GROUP) + EPS)
        yn = y * rstd
        dn = dn_ref[...]
        a = dn * g_ref[...]
        dy = rstd * (a - yn * (_seg_sum(a * yn, is_a) * (1.0 / GROUP)))
        dg = jnp.sum(dn * yn, axis=0, keepdims=True)
        dc, dw = _conv3_grads(dy * gb, c, w, row, S)
        dcv_ref[0] = (dy * z).astype(BF16)
        dcv_ref[1] = (dc * u).astype(BF16)
        dcv_ref[2] = (dc * gc).astype(BF16)

        @pl.when(b == 0)
        def _():
            dw_ref[...] = dw
            dg_ref[...] = dg

        @pl.when(b != 0)
        def _():
            dw_ref[...] += dw
            dg_ref[...] += dg

    return pl.pallas_call(
        body, grid=(P, B), name=name,
        in_specs=[pl.BlockSpec((3, S, LANES), lambda p, b: (0, b, p)),
                  pl.BlockSpec((None, S, LANES), lambda p, b: (1, b, p)),
                  pl.BlockSpec((3, LANES), lambda p, b: (0, p)),
                  pl.BlockSpec((1, LANES), lambda p, b: (0, p)),
                  pl.BlockSpec(memory_space=pl.ANY)],
        out_specs=[pl.BlockSpec((3, S, LANES), lambda p, b: (1, b, p)),
                   pl.BlockSpec((3, LANES), lambda p, b: (0, p)),
                   pl.BlockSpec((1, LANES), lambda p, b: (0, p))],
        out_shape=[jax.ShapeDtypeStruct(dproj.shape, dproj.dtype),
                   jax.ShapeDtypeStruct((3, C), F32), jax.ShapeDtypeStruct((1, C), F32)],
        input_output_aliases={4: 0},
        compiler_params=_params(("parallel", "arbitrary")),
    )(cv3, dmix3, taps, gain, dproj)


def _sigmoid(x):
    return 0.5 * jnp.tanh(0.5 * x) + 0.5


def _ffn_act_fwd(up3, taps, S, name):
    _, T, Fd = up3.shape
    B, P = T // S, Fd // LANES

    def body(up_ref, wg_ref, wv_ref, act_ref):
        row = lax.broadcasted_iota(jnp.int32, (S, 1), 0)
        cg = _conv3(up_ref[0].astype(F32), wg_ref[...], row)
        cv = _conv3(up_ref[1].astype(F32), wv_ref[...], row)
        act_ref[...] = ((cg * _sigmoid(cg)) * cv).astype(BF16)

    return pl.pallas_call(
        body, grid=(B, P), name=name,
        in_specs=[pl.BlockSpec((2, S, LANES), lambda b, p: (0, b, p)),
                  pl.BlockSpec((3, LANES), lambda b, p: (0, p)),
                  pl.BlockSpec((3, LANES), lambda b, p: (0, P + p))],
        out_specs=pl.BlockSpec((S, LANES), lambda b, p: (b, p)),
        out_shape=jax.ShapeDtypeStruct((T, Fd), BF16),
        compiler_params=_params(("parallel", "parallel")),
    )(up3, taps, taps)


def _ffn_act_bwd(up3, dact3, taps, S, name):
    _, T, Fd = up3.shape
    B, P = T // S, Fd // LANES

    def body(up_ref, da_ref, wg_ref, wv_ref, dup_ref, dwg_ref, dwv_ref):
        b = pl.program_id(1)
        row = lax.broadcasted_iota(jnp.int32, (S, 1), 0)
        ug = up_ref[0].astype(F32)
        uv = up_ref[1].astype(F32)
        wg = wg_ref[...]
        wv = wv_ref[...]
        cg = _conv3(ug, wg, row)
        cv = _conv3(uv, wv, row)
        sg = _sigmoid(cg)
        da = da_ref[...].astype(F32)
        dcg = (da * cv) * (sg * (1.0 + cg * (1.0 - sg)))
        dcv = da * (cg * sg)
        dug, dwg = _conv3_grads(dcg, ug, wg, row, S)
        duv, dwv = _conv3_grads(dcv, uv, wv, row, S)
        dup_ref[0] = dug.astype(BF16)
        dup_ref[1] = duv.astype(BF16)

        @pl.when(b == 0)
        def _():
            dwg_ref[...] = dwg
            dwv_ref[...] = dwv

        @pl.when(b != 0)
        def _():
            dwg_ref[...] += dwg
            dwv_ref[...] += dwv

    tap_out = pl.BlockSpec((3, LANES), lambda p, b: (0, p))
    return pl.pallas_call(
        body, grid=(P, B), name=name,
        in_specs=[pl.BlockSpec((2, S, LANES), lambda p, b: (0, b, p)),
                  pl.BlockSpec((None, S, LANES), lambda p, b: (0, b, p)),
                  pl.BlockSpec((3, LANES), lambda p, b: (0, p)),
                  pl.BlockSpec((3, LANES), lambda p, b: (0, P + p))],
        out_specs=[pl.BlockSpec((2, S, LANES), lambda p, b: (0, b, p)), tap_out, tap_out],
        out_shape=[jax.ShapeDtypeStruct((2, T, Fd), BF16),
                   jax.ShapeDtypeStruct((3, Fd), F32), jax.ShapeDtypeStruct((3, Fd), F32)],
        compiler_params=_params(("parallel", "arbitrary")),
    )(up3, dact3, taps, taps)


def _final_norm_loss(x, g, target, tm, name):
    T, D = x.shape

    def body(x_ref, g_ref, t_ref, dx_ref, dg_ref, loss_ref):
        xv = x_ref[...]
        rstd = lax.rsqrt(jnp.mean(xv * xv, axis=-1, keepdims=True) + EPS)
        xn = xv * rstd
        err = xn * g_ref[...] - t_ref[...]
        part = 0.5 * jnp.sum(jnp.mean(err * err, axis=-1, keepdims=True), axis=0, keepdims=True)
        dy = err * (1.0 / D)
        a = dy * g_ref[...]
        dx_ref[...] = rstd * (a - xn * jnp.mean(a * xn, axis=-1, keepdims=True))
        dg = jnp.sum(dy * xn, axis=0, keepdims=True)
        lpart = jnp.broadcast_to(part, (1, LANES))

        @pl.when(pl.program_id(0) == 0)
        def _():
            dg_ref[...] = dg
            loss_ref[...] = lpart

        @pl.when(pl.program_id(0) != 0)
        def _():
            dg_ref[...] += dg
            loss_ref[...] += lpart

    row = pl.BlockSpec((tm, D), lambda i: (i, 0))
    return pl.pallas_call(
        body, grid=(T // tm,), name=name,
        in_specs=[row, pl.BlockSpec((1, D), lambda i: (0, 0)), row],
        out_specs=[row, pl.BlockSpec((1, D), lambda i: (0, 0)), pl.BlockSpec((1, LANES), lambda i: (0, 0))],
        out_shape=[jax.ShapeDtypeStruct((T, D), F32), jax.ShapeDtypeStruct((1, D), F32),
                   jax.ShapeDtypeStruct((1, LANES), F32)],
        compiler_params=_params(("arbitrary",)),
    )(x, g, target)


def _row_tile(rows, cols, budget_elems=512 * 1024):
    tr = rows
    while tr * cols > budget_elems and tr % 32 == 0:
        tr //= 2
    return tr


def _prefetch_call(body, grid, in_specs, out_specs, out_shape, name, sem, aliases=None):
    return pl.pallas_call(
        body, name=name, out_shape=out_shape,
        grid_spec=pltpu.PrefetchScalarGridSpec(num_scalar_prefetch=1, grid=grid, in_specs=in_specs,
                                               out_specs=out_specs),
        input_output_aliases=aliases or {},
        compiler_params=_params(sem))


def _cast_into_full(w, layer, colwise, where, name):
    _, K, N = w.shape
    tr = _row_tile(K, N)
    nrb = K // tr
    full_shape = (1, K, 4 * N) if colwise else (1, 4 * K, N)

    def body(where_ref, w_ref, o_ref):
        del where_ref
        o_ref[...] = w_ref[...].astype(BF16)

    if colwise:
        out_map = lambda i, wh: (0, i, wh[0])
    else:
        out_map = lambda i, wh: (0, wh[0] * nrb + i, 0)
    return _prefetch_call(
        body, (nrb,), [pl.BlockSpec((None, tr, N), lambda i, wh: (layer, i, 0))],
        pl.BlockSpec((None, tr, N), out_map), jax.ShapeDtypeStruct(full_shape, BF16), name,
        ("parallel",))(where, w)


def _chip_sum(g3, other, colwise, where, name):
    L, K, N = g3.shape
    hk, hn = (K // 2, N) if colwise else (K, N // 2)
    tr = _row_tile(hk, hn)
    nrb = hk // tr

    def body(where_ref, g_ref, o_ref, s_ref):
        del where_ref
        s_ref[...] = (g_ref[...].astype(F32) + o_ref[...].astype(F32)).astype(BF16)

    if colwise:
        g_map = lambda l, i, wh: (l, wh[1] * nrb + i, 0)
    else:
        g_map = lambda l, i, wh: (l, i, wh[1])
    blk = pl.BlockSpec((None, tr, hn), lambda l, i, wh: (l, i, 0))
    return _prefetch_call(
        body, (L, nrb), [pl.BlockSpec((None, tr, hn), g_map), blk], blk,
        jax.ShapeDtypeStruct((L, hk, hn), BF16), name, ("parallel", "parallel"))(where, g3, other)


def _owner_sum(chip_sum, received, colwise, where, layer, n_layers, prev, name):
    _, hk, hn = chip_sum.shape
    pk, pn = (hk, hn // 4) if colwise else (hk // 4, hn)
    tr = _row_tile(pk, pn)
    nrb = pk // tr
    shard_shape = (n_layers, 2 * pk, pn) if colwise else (n_layers, pk, 2 * pn)

    def body(where_ref, own_ref, rec_ref, *rest):
        del where_ref
        o_ref = rest[-1]
        acc = own_ref[...].astype(F32)
        for j in range(3):
            acc = acc + rec_ref[j].astype(F32)
        o_ref[...] = acc

    if colwise:
        own_map = lambda i, wh: (0, i, wh[0])
        out_map = lambda i, wh: (layer, wh[1] * nrb + i, 0)
    else:
        own_map = lambda i, wh: (0, wh[0] * nrb + i, 0)
        out_map = lambda i, wh: (layer, i, wh[1])
    in_specs = [pl.BlockSpec((None, tr, pn), own_map),
                pl.BlockSpec((3, None, tr, pn), lambda i, wh: (0, 0, i, 0))]
    operands = [where, chip_sum, received]
    if prev is not None:
        in_specs.append(ANY)
        operands.append(prev)
    return _prefetch_call(
        body, (nrb,), in_specs, pl.BlockSpec((None, tr, pn), out_map), jax.ShapeDtypeStruct(shard_shape, F32), name,
        ("parallel",), None if prev is None else {3: 0})(*operands)


def _adamw(w, g, m, v, name):
    R, Cc = w.shape
    tr = _row_tile(R, Cc, 256 * 1024)

    def body(w_ref, g_ref, m_ref, v_ref, d_ref, nm_ref, nv_ref):
        gv = g_ref[...]
        nm = ADAM_B1 * m_ref[...] + (1.0 - ADAM_B1) * gv
        nv = ADAM_B2 * v_ref[...] + (1.0 - ADAM_B2) * (gv * gv)
        m_hat = nm / (1.0 - ADAM_B1 ** ADAM_STEP)
        v_hat = nv / (1.0 - ADAM_B2 ** ADAM_STEP)
        d_ref[...] = -ADAM_LR * (m_hat / (jnp.sqrt(v_hat) + ADAM_EPS) + ADAM_WD * w_ref[...])
        nm_ref[...] = nm
        nv_ref[...] = nv

    blk = pl.BlockSpec((tr, Cc), lambda i: (i, 0))
    shp = jax.ShapeDtypeStruct((R, Cc), F32)
    return pl.pallas_call(
        body, grid=(R // tr,), name=name,
        in_specs=[blk] * 4, out_specs=[blk] * 3, out_shape=[shp] * 3,
        compiler_params=_params(("parallel",)),
    )(w, g, m, v)


COL_SHARDED = (True, False, True, False)


def _position():
    x, y, c = lax.axis_index("x"), lax.axis_index("y"), lax.axis_index("c")
    chips = [(1 - x, y), (x, 1 - y), (1 - x, 1 - y)]
    return x, y, c, chips


def _span(index, size, align):
    return pl.ds(pl.multiple_of(index * size, align), size)


def _window(ref, colwise, shard, half, shards=4):
    _, K, N = ref.shape
    rows = cols = slice(None)
    if colwise:
        if half is not None:
            rows = _span(half, K // 2, 16)
        if shard is not None:
            cols = _span(shard, N // shards, LANES)
    else:
        if shard is not None:
            rows = _span(shard, K // shards, 16)
        if half is not None:
            cols = _span(half, N // 2, LANES)
    return ref.at[:, rows, cols]


HBM = pl.BlockSpec(memory_space=pltpu.HBM)
SEMAPHORES = pl.BlockSpec(memory_space=pltpu.SEMAPHORE)


def _gather_start(fulls, colwise, group_sizes, after, name):
    n = len(fulls)
    n_groups = len(group_sizes)

    n_in = n if after is None else n + 1

    def body(*refs):
        ins = refs[:n]
        sems = refs[n_in:n_in + 2 * n_groups]
        x, y, c, chips = _position()
        me = 2 * x + y
        i = 0
        for g, size in enumerate(group_sizes):
            for a in range(size):
                win = _window(ins[i], colwise[i], me, c)
                for j, chip in enumerate(chips):
                    pltpu.make_async_remote_copy(
                        src_ref=win, dst_ref=win, send_sem=sems[2 * g].at[a * 3 + j],
                        recv_sem=sems[2 * g + 1].at[a * 3 + j],
                        device_id=(chip[0], chip[1], c), device_id_type=MESH_ID).start()
                i += 1

    sem_shapes = []
    for size in group_sizes:
        sem_shapes += [pltpu.SemaphoreType.DMA((3 * size,)), pltpu.SemaphoreType.DMA((3 * size,))]
    operands = [pltpu.with_memory_space_constraint(f, pltpu.HBM) for f in fulls]
    in_specs = [HBM] * n
    if after is not None:
        operands.append(after)
        in_specs.append(ANY)
    outs = pl.pallas_call(
        body, name=name,
        in_specs=in_specs, out_specs=[SEMAPHORES] * (2 * n_groups) + [HBM] * n,
        out_shape=sem_shapes + [pltpu.HBM(f.shape, f.dtype) for f in fulls],
        input_output_aliases={i: 2 * n_groups + i for i in range(n)},
        compiler_params=pltpu.CompilerParams(has_side_effects=pltpu.SideEffectType.DATAFLOW_SIDE_EFFECTING),
    )(*operands)
    sems = [(outs[2 * g], outs[2 * g + 1]) for g in range(n_groups)]
    return sems, list(outs[2 * n_groups:])


def _to_sibling(ref, colwise, chip, half, x, y, c, send_sem, recv_sem):
    win = _window(ref, colwise, 2 * chip[0] + chip[1], half)
    return pltpu.make_async_remote_copy(
        src_ref=win, dst_ref=win, send_sem=send_sem, recv_sem=recv_sem,
        device_id=(x, y, 1 - c), device_id_type=MESH_ID)


def _gather_pass(in_flight, colwise, sems, after, name):
    n = len(in_flight)

    def body(*refs):
        ins = refs[:n]
        send_sems, recv_sems = refs[n], refs[n + 1]
        pass_send, pass_recv = refs[-2 - n], refs[-1 - n]
        x, y, c, chips = _position()
        me = 2 * x + y
        for a in range(n):
            for j, chip in enumerate(chips):
                k = a * 3 + j
                pltpu.make_async_remote_copy(
                    src_ref=_window(ins[a], colwise[a], me, c),
                    dst_ref=_window(ins[a], colwise[a], 2 * chip[0] + chip[1], c),
                    send_sem=send_sems.at[k], recv_sem=recv_sems.at[k],
                    device_id=(chip[0], chip[1], c), device_id_type=MESH_ID).wait()
                _to_sibling(ins[a], colwise[a], chip, c, x, y, c, pass_send.at[k], pass_recv.at[k]).start()

    operands = list(in_flight) + list(sems)
    in_specs = [HBM] * n + [SEMAPHORES] * 2
    if after is not None:
        operands.append(after)
        in_specs.append(ANY)
    outs = pl.pallas_call(
        body, name=name,
        in_specs=in_specs, out_specs=[SEMAPHORES] * 2 + [HBM] * n,
        out_shape=[pltpu.SemaphoreType.DMA((3 * n,)), pltpu.SemaphoreType.DMA((3 * n,))]
        + [pltpu.HBM(f.shape, f.dtype) for f in in_flight],
        input_output_aliases={i: 2 + i for i in range(n)},
        compiler_params=pltpu.CompilerParams(has_side_effects=pltpu.SideEffectType.DATAFLOW_SIDE_EFFECTING),
    )(*operands)
    return (outs[0], outs[1]), list(outs[2:])


def _gather_wait(in_flight, colwise, sems, after, name):
    n = len(in_flight)

    def body(*refs):
        ins = refs[:n]
        send_sems, recv_sems = refs[n], refs[n + 1]
        x, y, c, chips = _position()
        for a in range(n):
            for j, chip in enumerate(chips):
                k = a * 3 + j
                _to_sibling(ins[a], colwise[a], chip, c, x, y, c, send_sems.at[k], recv_sems.at[k]).wait_send()
                _to_sibling(ins[a], colwise[a], chip, 1 - c, x, y, c, send_sems.at[k], recv_sems.at[k]).wait_recv()

    operands = list(in_flight) + list(sems)
    in_specs = [HBM] * n + [SEMAPHORES] * 2
    if after is not None:
        operands.append(after)
        in_specs.append(ANY)
    outs = pl.pallas_call(
        body, name=name,
        in_specs=in_specs, out_specs=[HBM] * n,
        out_shape=[pltpu.HBM(f.shape, f.dtype) for f in in_flight],
        input_output_aliases={i: i for i in range(n)},
        compiler_params=pltpu.CompilerParams(has_side_effects=pltpu.SideEffectType.DATAFLOW_SIDE_EFFECTING),
    )(*operands)
    return list(outs)


def _exchange_halves(grads, colwise, name):
    n = len(grads)
    out_shapes = []
    for g, cw in zip(grads, colwise):
        L, K, N = g.shape
        out_shapes.append(jax.ShapeDtypeStruct((L, K // 2, N) if cw else (L, K, N // 2), g.dtype))

    def body(*refs):
        g_refs, out = refs[:n], refs[n:2 * n]
        send_sems, recv_sems = refs[2 * n:]
        x, y, c, _ = _position()
        copies = [pltpu.make_async_remote_copy(
            src_ref=_window(g_refs[i], colwise[i], None, 1 - c), dst_ref=out[i],
            send_sem=send_sems.at[i], recv_sem=recv_sems.at[i],
            device_id=(x, y, 1 - c), device_id_type=MESH_ID) for i in range(n)]
        for cp in copies:
            cp.start()
        for cp in copies:
            cp.wait()

    return pl.pallas_call(
        body, name=name,
        in_specs=[ANY] * n, out_specs=[ANY] * n, out_shape=out_shapes,
        scratch_shapes=[pltpu.SemaphoreType.DMA((n,)), pltpu.SemaphoreType.DMA((n,))],
    )(*grads)


def _scatter_copy(src_ref, land_ref, colwise, j, chip, c, send_sem, recv_sem):
    return pltpu.make_async_remote_copy(
        src_ref=_window(src_ref, colwise, 2 * chip[0] + chip[1], None), dst_ref=land_ref.at[j],
        send_sem=send_sem, recv_sem=recv_sem, device_id=(chip[0], chip[1], c), device_id_type=MESH_ID)


def _scatter_start(chip_sums, colwise, name):
    n = len(chip_sums)
    lands = []
    for g, cw in zip(chip_sums, colwise):
        L, hk, hn = g.shape
        lands.append(lax.empty((3, L, hk, hn // 4) if cw else (3, L, hk // 4, hn), g.dtype))

    def body(*refs):
        src, land = refs[:n], refs[n:2 * n]
        send_sems, recv_sems = refs[2 * n], refs[2 * n + 1]
        x, y, c, chips = _position()
        for i in range(n):
            for j, chip in enumerate(chips):
                _scatter_copy(src[i], land[i], colwise[i], j, chip, c, send_sems.at[i * 3 + j],
                              recv_sems.at[i * 3 + j]).start()

    arrays = list(chip_sums) + lands
    outs = pl.pallas_call(
        body, name=name,
        in_specs=[HBM] * (2 * n), out_specs=[SEMAPHORES] * 2 + [HBM] * (2 * n),
        out_shape=[pltpu.SemaphoreType.DMA((3 * n,)), pltpu.SemaphoreType.DMA((3 * n,))]
        + [pltpu.HBM(a.shape, a.dtype) for a in arrays],
        input_output_aliases={i: 2 + i for i in range(2 * n)},
        compiler_params=pltpu.CompilerParams(has_side_effects=pltpu.SideEffectType.DATAFLOW_SIDE_EFFECTING),
    )(*[pltpu.with_memory_space_constraint(a, pltpu.HBM) for a in arrays])
    return (outs[0], outs[1]), list(outs[2:2 + n]), list(outs[2 + n:])


def _scatter_wait(sources, lands, colwise, sems, after, name):
    n = len(sources)

    def body(*refs):
        src, land = refs[:n], refs[n:2 * n]
        send_sems, recv_sems = refs[2 * n], refs[2 * n + 1]
        x, y, c, chips = _position()
        for i in range(n):
            for j, chip in enumerate(chips):
                cp = _scatter_copy(src[i], land[i], colwise[i], j, chip, c, send_sems.at[i * 3 + j],
                                   recv_sems.at[i * 3 + j])
                cp.wait_send()
                cp.wait_recv()

    arrays = list(sources) + list(lands)
    operands = arrays + list(sems)
    in_specs = [HBM] * (2 * n) + [SEMAPHORES] * 2
    if after is not None:
        operands.append(after)
        in_specs.append(ANY)
    outs = pl.pallas_call(
        body, name=name,
        in_specs=in_specs, out_specs=[HBM] * (2 * n),
        out_shape=[pltpu.HBM(a.shape, a.dtype) for a in arrays],
        input_output_aliases={i: i for i in range(2 * n)},
        compiler_params=pltpu.CompilerParams(has_side_effects=pltpu.SideEffectType.DATAFLOW_SIDE_EFFECTING),
    )(*operands)
    return list(outs[:n]), list(outs[n:])


def _share_with_sibling(shards):
    n = len(shards)

    def body(*refs):
        out = refs[n:2 * n]
        send_sems, recv_sems = refs[2 * n:]
        x, y, c, _ = _position()

        def copy(i, half):
            win = _window(out[i], COL_SHARDED[i], None, half)
            return pltpu.make_async_remote_copy(
                src_ref=win, dst_ref=win, send_sem=send_sems.at[i], recv_sem=recv_sems.at[i],
                device_id=(x, y, 1 - c), device_id_type=MESH_ID)

        for i in range(n):
            copy(i, c).start()
        for i in range(n):
            copy(i, 1 - c).wait_recv()
        for i in range(n):
            copy(i, c).wait_send()

    return pl.pallas_call(
        body, name="grad_share_with_sibling",
        in_specs=[ANY] * n, out_specs=[ANY] * n,
        out_shape=[jax.ShapeDtypeStruct(s.shape, s.dtype) for s in shards],
        input_output_aliases={i: i for i in range(n)},
        scratch_shapes=[pltpu.SemaphoreType.DMA((n,)), pltpu.SemaphoreType.DMA((n,))],
    )(*shards)


def _all_reduce_small(pack, name):
    R, Cc = pack.shape
    n_dev = 8

    def body(p_ref, o_ref, buf, send_sems, recv_sems):
        x, y, c, _ = _position()
        me = 4 * x + 2 * y + c
        buf[me] = p_ref[...]

        def peer(k):
            px = 1 - x if k & 4 else x
            py = 1 - y if k & 2 else y
            pc = 1 - c if k & 1 else c
            return px, py, pc

        def copy(k, incoming):
            px, py, pc = peer(k)
            slot = (4 * px + 2 * py + pc) if incoming else me
            return pltpu.make_async_remote_copy(
                src_ref=p_ref, dst_ref=buf.at[slot], send_sem=send_sems.at[k], recv_sem=recv_sems.at[k],
                device_id=(px, py, pc), device_id_type=MESH_ID)

        for k in range(1, n_dev):
            copy(k, False).start()
        for k in range(1, n_dev):
            copy(k, True).wait_recv()
        for k in range(1, n_dev):
            copy(k, False).wait_send()
        acc = buf[0]
        for j in range(1, n_dev):
            acc = acc + buf[j]
        o_ref[...] = acc

    vmem = pl.BlockSpec(memory_space=pltpu.VMEM)
    return pl.pallas_call(
        body, name=name,
        in_specs=[vmem], out_specs=vmem, out_shape=jax.ShapeDtypeStruct((R, Cc), F32),
        scratch_shapes=[pltpu.VMEM((n_dev, R, Cc), F32), pltpu.SemaphoreType.DMA((n_dev,)),
                        pltpu.SemaphoreType.DMA((n_dev,))],
    )(pack)


def _local_forward_backward(x2, target2, S, pass_on, fetch, reduce, layers, final_g, tm=512):
    T, D = x2.shape
    C = D // 2
    n_heads = C // GROUP
    n_layers = len(layers)
    weights = {}
    saved = []
    xc = x2
    for li, lw in enumerate(layers):
        if li == 0:
            pass_on(0, None)
            weights.update(fetch(0, None))
        h1, qkv3, cv3 = _norm_proj(xc, lw["norm1"], weights[li, "w_in"], 0, ((3, C, F32), (3, C, BF16)), tm,
                                   min(C, 512), f"l{li}_norm_in_proj")
        if li == 0:
            pass_on(1, h1)
        o, lse, mix = _attn_fwd(qkv3, lw["attn_g"], 2, S, n_heads, f"l{li}_attn_fwd")
        if li == 0:
            weights.update(fetch(1, o))
            pass_on(2, o)
            pass_on(3, o)
        mix = _mix_conv_fwd(cv3, lw["taps"], lw["conv_g"], mix, S, f"l{li}_mix_conv_fwd")
        x_mid = _proj_residual(mix, weights[li, "w_out"], 0, xc, tm, f"l{li}_out_proj")
        if li == 0:
            weights.update(fetch(2, x_mid))
        Fd = weights[li, "ffn_up"].shape[2] // 2
        h2, up3 = _norm_proj(x_mid, lw["norm2"], weights[li, "ffn_up"], 0, ((2, Fd, BF16),), tm // 2, 256,
                             f"l{li}_norm_ffn_up")
        if li == 0:
            weights.update(fetch(3, up3))
        act = _ffn_act_fwd(up3, lw["ffn_taps"], S, f"l{li}_ffn_act_fwd")
        if li + 1 < n_layers:
            pass_on(li + 4, act)
        x_out = _proj_residual(act.reshape(1, T, Fd), weights[li, "ffn_down"], 0, x_mid, tm, f"l{li}_ffn_down")
        if li + 1 < n_layers:
            weights.update(fetch(li + 4, x_out))
        saved.append(dict(x_in=xc, h1=h1, qkv3=qkv3, cv3=cv3, o=o, lse=lse, mix=mix, x_mid=x_mid, h2=h2, up3=up3,
                          act=act))
        xc = x_out

    dx, d_final_g, loss_part = _final_norm_loss(xc, final_g, target2, tm, "final_norm_loss")

    small = [None] * n_layers
    started = None
    for li in reversed(range(n_layers)):
        lw, sv = layers[li], saved[li]
        w_in, w_out, ffn_up, ffn_down = (weights[li, n] for n in ("w_in", "w_out", "ffn_up", "ffn_down"))
        dxb, dact3 = _grad_through_weight(dx, ffn_down, 0, 1, Fd, BF16, tm, 256, f"l{li}_d_act", started)
        Fd = ffn_down.shape[1]
        d_ffn_down = _weight_grad(sv["act"].reshape(1, T, Fd), dxb.reshape(1, T, D), Fd // 2, D, 1024,
                                  0, 1, None, f"l{li}_d_ffn_down")
        dup3, d_taps_g, d_taps_v = _ffn_act_bwd(sv["up3"], dact3, lw["ffn_taps"], S, f"l{li}_ffn_act_bwd")
        d_ffn_up = _weight_grad(sv["h2"].reshape(1, T, D), dup3, D, Fd, 1024, 0, 1, None,
                                f"l{li}_d_ffn_up")
        started = None
        if li == 0:
            started = reduce({(li, "ffn_down"): d_ffn_down, (li, "ffn_up"): d_ffn_up})
        dx_mid, d_norm2 = _grad_through_proj_norm(dup3, ffn_up, 0, sv["x_mid"], lw["norm2"], dx, tm // 2,
                                                  f"l{li}_d_norm2")
        dxmb, dmix3 = _grad_through_weight(dx_mid, w_out, 0, 2, C, F32, tm, min(C, 512), f"l{li}_d_mix", started)
        d_w_out = _weight_grad(sv["mix"], dxmb.reshape(1, T, D), C, D, 1024, 0, 1, None, f"l{li}_d_w_out")
        dproj, d_attn_g = _attn_bwd(sv["qkv3"], sv["o"], sv["lse"], dmix3, lw["attn_g"], 6, S, n_heads,
                                    f"l{li}_attn_bwd")
        dproj, d_taps, d_conv_g = _mix_conv_bwd(sv["cv3"], dmix3, lw["taps"], lw["conv_g"], dproj, S,
                                                f"l{li}_mix_conv_bwd")
        d_w_in = _weight_grad(sv["h1"].reshape(1, T, D), dproj, D, C, 1024, 0, 1, None, f"l{li}_d_w_in")
        dx, d_norm1 = _grad_through_proj_norm(dproj, w_in, 0, sv["x_in"], lw["norm1"], dx_mid, tm,
                                              f"l{li}_d_norm1")
        if li == 0:
            reduce({(li, "w_out"): d_w_out, (li, "w_in"): d_w_in})
        else:
            started = reduce({(li, "ffn_down"): d_ffn_down, (li, "ffn_up"): d_ffn_up, (li, "w_out"): d_w_out,
                              (li, "w_in"): d_w_in})
        small[li] = dict(norm1=d_norm1, taps=d_taps, attn_g=d_attn_g, conv_g=d_conv_g, norm2=d_norm2,
                         ffn_taps=jnp.concatenate([d_taps_g, d_taps_v], axis=1))
    return loss_part, dx, small, d_final_g


SMALL_ORDER = ("norm1", "attn_g", "conv_g", "norm2", "taps", "ffn_taps")


def _pack_small(small, d_final_g):
    parts = [small[li][k].reshape(-1) for li in range(len(small)) for k in SMALL_ORDER] + [d_final_g.reshape(-1)]
    return jnp.concatenate(parts).reshape(-1, LANES)


def _unpack_small(pack, small, d_final_g):
    flat = pack.reshape(-1)
    out, pos = [dict() for _ in small], 0
    for li in range(len(small)):
        for k in SMALL_ORDER:
            n = small[li][k].size
            out[li][k] = flat[pos:pos + n].reshape(small[li][k].shape)
            pos += n
    return out, flat[pos:pos + d_final_g.size]


def kernel(x, norm1_g, w_in, mix_conv_w, attn_out_g, conv_out_g, w_out, norm2_g, ffn_up, ffn_conv_w, ffn_down, final_norm_g, loss_target, m_norm1_g, m_w_in, m_mix_conv_w, m_attn_out_g, m_conv_out_g, m_w_out, m_norm2_g, m_ffn_up, m_ffn_conv_w, m_ffn_down, m_final_norm_g, v_norm1_g, v_w_in, v_mix_conv_w, v_attn_out_g, v_conv_out_g, v_w_out, v_norm2_g, v_ffn_up, v_ffn_conv_w, v_ffn_down, v_final_norm_g):
    Bl, S, D = x.shape
    L = w_in.shape[0]
    T = Bl * S
    shard = 2 * lax.axis_index("x") + lax.axis_index("y")
    where = jnp.stack([shard, lax.axis_index("c")]).astype(jnp.int32)
    big_names = ("w_in", "w_out", "ffn_up", "ffn_down")

    taps_w, ftaps_w = mix_conv_w.shape[2], ffn_conv_w.shape[2]
    taps_full = jnp.zeros((L, 3, 4 * taps_w), F32)
    taps_full = lax.dynamic_update_slice(taps_full, mix_conv_w, (0, 0, shard * taps_w))
    ftaps_full = jnp.zeros((L, 3, 4 * ftaps_w), F32)
    ftaps_full = lax.dynamic_update_slice(ftaps_full, ffn_conv_w, (0, 0, shard * ftaps_w))
    tap_pack = jnp.concatenate([taps_full.reshape(-1), ftaps_full.reshape(-1)]).reshape(-1, LANES)
    tap_pack = _all_reduce_small(tap_pack * 0.5, "all_gather_taps")
    n_taps = taps_full.size
    taps_full = tap_pack.reshape(-1)[:n_taps].reshape(taps_full.shape)
    ftaps_full = tap_pack.reshape(-1)[n_taps:].reshape(ftaps_full.shape)

    big_shards = dict(zip(big_names, (w_in, w_out, ffn_up, ffn_down)))
    col_of = dict(zip(big_names, COL_SHARDED))
    groups = [[(0, n)] for n in big_names] + [[(l, n) for n in big_names] for l in range(1, L)]
    sems, in_flight = [], {}
    all_started = tap_pack
    for first, last in ((0, 1), (1, len(groups))):
        keys = [k for g in groups[first:last] for k in g]
        new_sems, arrays = _gather_start(
            [_cast_into_full(big_shards[n], l, col_of[n], where, f"cast_{n}_{l}") for l, n in keys],
            [col_of[n] for _, n in keys], [len(g) for g in groups[first:last]], all_started,
            f"gather_start_{first}")
        sems += new_sems
        in_flight.update(zip(keys, arrays))
        all_started = arrays[-1]

    def pass_on(g, after):
        after = all_started if g == 0 else after
        sems[g], arrays = _gather_pass([in_flight[k] for k in groups[g]], [col_of[n] for _, n in groups[g]],
                                       sems[g], after, f"gather_pass_{g}")
        in_flight.update(zip(groups[g], arrays))

    def fetch(g, after):
        done = _gather_wait([in_flight[k] for k in groups[g]], [col_of[n] for _, n in groups[g]], sems[g], after,
                            f"gather_wait_{g}")
        return dict(zip(groups[g], done))

    pending = []

    def reduce(grads):
        g = len(pending)
        keys = list(grads)
        cols = [col_of[n] for _, n in keys]
        others = _exchange_halves([grads[k] for k in keys], cols, f"grad_exchange_halves_{g}")
        chip_sums = [_chip_sum(grads[k], o, cw, where, f"chip_sum_{k[1]}_{k[0]}")
                     for k, o, cw in zip(keys, others, cols)]
        pending.append((keys, cols) + _scatter_start(chip_sums, cols, f"scatter_start_{g}"))
        return pending[-1][3][0]

    layers = [dict(norm1=norm1_g[l:l + 1], taps=taps_full[l], attn_g=attn_out_g[l:l + 1],
                   conv_g=conv_out_g[l:l + 1], norm2=norm2_g[l:l + 1], ffn_taps=ftaps_full[l]) for l in range(L)]

    loss_part, dx, small, d_final_g = _local_forward_backward(
        x.reshape(T, D), loss_target.reshape(T, D), S, pass_on, fetch, reduce, layers, final_norm_g.reshape(1, D))
    loss = lax.psum(loss_part[0, 0], ("x", "y", "c"))

    reduced = dict.fromkeys(big_names)
    last_started = pending[-1][3][0]
    for g, (keys, cols, rs_sems, sources, lands) in enumerate(pending):
        after = last_started if g + 1 < len(pending) else None
        sources, lands = _scatter_wait(sources, lands, cols, rs_sems, after, f"scatter_wait_{g}")
        for (l, n), cw, src, land in zip(keys, cols, sources, lands):
            reduced[n] = _owner_sum(src, land, cw, where, l, L, reduced[n], f"owner_sum_{n}_{l}")
    g_big = _share_with_sibling([reduced[n] for n in big_names])

    pack = _all_reduce_small(_pack_small(small, d_final_g), "all_reduce_small_grads")
    g_small, g_final = _unpack_small(pack, small, d_final_g)

    def stacked(key):
        return jnp.stack([g_small[l][key].reshape(g_small[l][key].shape[-2:] if key.endswith("taps") else (-1,))
                          for l in range(L)])

    g_norm1, g_attn, g_conv, g_norm2 = stacked("norm1"), stacked("attn_g"), stacked("conv_g"), stacked("norm2")
    g_taps = lax.dynamic_slice(stacked("taps"), (0, 0, shard * taps_w), (L, 3, taps_w))
    g_ftaps = lax.dynamic_slice(stacked("ffn_taps"), (0, 0, shard * ftaps_w), (L, 3, ftaps_w))

    grads_out = dict(norm1_g=g_norm1, w_in=g_big[0], mix_conv_w=g_taps, attn_out_g=g_attn, conv_out_g=g_conv,
                     w_out=g_big[1], norm2_g=g_norm2, ffn_up=g_big[2], ffn_conv_w=g_ftaps, ffn_down=g_big[3],
                     final_norm_g=g_final)
    weights = dict(norm1_g=norm1_g, w_in=w_in, mix_conv_w=mix_conv_w, attn_out_g=attn_out_g, conv_out_g=conv_out_g,
                   w_out=w_out, norm2_g=norm2_g, ffn_up=ffn_up, ffn_conv_w=ffn_conv_w, ffn_down=ffn_down,
                   final_norm_g=final_norm_g)
    ms = dict(norm1_g=m_norm1_g, w_in=m_w_in, mix_conv_w=m_mix_conv_w, attn_out_g=m_attn_out_g,
              conv_out_g=m_conv_out_g, w_out=m_w_out, norm2_g=m_norm2_g, ffn_up=m_ffn_up, ffn_conv_w=m_ffn_conv_w,
              ffn_down=m_ffn_down, final_norm_g=m_final_norm_g)
    vs = dict(norm1_g=v_norm1_g, w_in=v_w_in, mix_conv_w=v_mix_conv_w, attn_out_g=v_attn_out_g,
              conv_out_g=v_conv_out_g, w_out=v_w_out, norm2_g=v_norm2_g, ffn_up=v_ffn_up, ffn_conv_w=v_ffn_conv_w,
              ffn_down=v_ffn_down, final_norm_g=v_final_norm_g)
    names = list(weights)
    small_names = [n for n in names if n not in big_names]
    delta, new_m, new_v = {}, {}, {}
    for n in big_names:
        shp = weights[n].shape
        two_d = (shp[0] * shp[1], shp[2])
        d_, m_, v_ = _adamw(weights[n].reshape(two_d), grads_out[n].reshape(two_d), ms[n].reshape(two_d),
                            vs[n].reshape(two_d), f"adamw_{n}")
        delta[n], new_m[n], new_v[n] = d_.reshape(shp), m_.reshape(shp), v_.reshape(shp)

    def packed(tree):
        return jnp.concatenate([tree[n].reshape(-1) for n in small_names]).reshape(-1, LANES)

    d_, m_, v_ = _adamw(packed(weights), packed(grads_out), packed(ms), packed(vs), "adamw_small")
    pos = 0
    for n in small_names:
        size, shp = weights[n].size, weights[n].shape
        delta[n] = d_.reshape(-1)[pos:pos + size].reshape(shp)
        new_m[n] = m_.reshape(-1)[pos:pos + size].reshape(shp)
        new_v[n] = v_.reshape(-1)[pos:pos + size].reshape(shp)
        pos += size

    return (loss, dx.reshape(Bl, S, D), *[grads_out[n] for n in names], *[delta[n] for n in names],
            *[new_m[n] for n in names], *[new_v[n] for n in names])
```

```python
import functools
import math

import jax
import jax.numpy as jnp
from jax import lax
from jax.experimental import pallas as pl
from jax.experimental.pallas import tpu as pltpu

F32 = jnp.float32
BF16 = jnp.bfloat16
EPS = 1e-6
GROUP = 64
LANES = 128
BAND = 128
DILATIONS = (1, 4, 16)
NEG = -1e30
MIB = 1024 * 1024
MESH_ID = pl.DeviceIdType.MESH

ADAM_LR = 0.001
ADAM_B1 = 0.9
ADAM_B2 = 0.999
ADAM_EPS = 1e-08
ADAM_WD = 0.01
ADAM_STEP = 10


ANY = pl.BlockSpec(memory_space=pl.ANY)


def _params(sem=None, vmem_mb=48):
    return pltpu.CompilerParams(dimension_semantics=sem, vmem_limit_bytes=vmem_mb * MIB)


def _nt(a, b):
    return lax.dot_general(a, b, (((1,), (1,)), ((), ())), preferred_element_type=F32)


def _tn(a, b):
    return lax.dot_general(a, b, (((0,), (0,)), ((), ())), preferred_element_type=F32)


def _seg_sum(x, is_a):
    s_a = jnp.sum(jnp.where(is_a, x, 0.0), axis=-1, keepdims=True)
    s_b = jnp.sum(jnp.where(is_a, 0.0, x), axis=-1, keepdims=True)
    return jnp.where(is_a, s_a, s_b)


def _lane_is_a():
    return lax.broadcasted_iota(jnp.int32, (1, LANES), 1) < GROUP


def _norm_proj(x, g, w3, layer, groups, tm, chunk, name):
    T, D = x.shape
    N = w3.shape[2]
    assert sum(p * c for p, c, _ in groups) == N and T % tm == 0

    def body(x_ref, g_ref, w_ref, h_ref, *out_refs):
        xv = x_ref[...]
        rstd = lax.rsqrt(jnp.mean(xv * xv, axis=-1, keepdims=True) + EPS)
        h = ((xv * rstd) * g_ref[...]).astype(BF16)
        h_ref[...] = h
        col = 0
        for (pieces, width, dtype), o_ref in zip(groups, out_refs):
            for p in range(pieces):
                for c0 in range(0, width, chunk):
                    acc = jnp.dot(h, w_ref[:, col + c0:col + c0 + chunk], preferred_element_type=F32)
                    o_ref[p, :, c0:c0 + chunk] = acc.astype(dtype)
                col += width

    out_shape = [jax.ShapeDtypeStruct((T, D), BF16)]
    out_specs = [pl.BlockSpec((tm, D), lambda i: (i, 0))]
    for pieces, width, dtype in groups:
        assert width % chunk == 0
        out_shape.append(jax.ShapeDtypeStruct((pieces, T, width), dtype))
        out_specs.append(pl.BlockSpec((pieces, tm, width), lambda i: (0, i, 0)))
    return pl.pallas_call(
        body, grid=(T // tm,), name=name,
        in_specs=[pl.BlockSpec((tm, D), lambda i: (i, 0)),
                  pl.BlockSpec((1, D), lambda i: (0, 0)),
                  pl.BlockSpec((None, D, N), lambda i: (layer, 0, 0))],
        out_specs=out_specs, out_shape=out_shape,
        compiler_params=_params(("parallel",), 56),
    )(x, g, w3)


def _proj_residual(pieces3, w3, layer, x, tm, name):
    P, T, C = pieces3.shape
    D = w3.shape[2]

    def body(a_ref, w_ref, x_ref, o_ref):
        acc = x_ref[...]
        for p in range(P):
            acc = acc + jnp.dot(a_ref[p], w_ref[p * C:(p + 1) * C, :], preferred_element_type=F32)
        o_ref[...] = acc

    return pl.pallas_call(
        body, grid=(T // tm,), name=name,
        in_specs=[pl.BlockSpec((P, tm, C), lambda i: (0, i, 0)),
                  pl.BlockSpec((None, P * C, D), lambda i: (layer, 0, 0)),
                  pl.BlockSpec((tm, D), lambda i: (i, 0))],
        out_specs=pl.BlockSpec((tm, D), lambda i: (i, 0)),
        out_shape=jax.ShapeDtypeStruct((T, D), F32),
        compiler_params=_params(("parallel",)),
    )(pieces3, w3, x)


def _grad_through_weight(dy, w3, layer, pieces, width, out_dtype, tm, chunk, name, after=None):
    T, D = dy.shape

    def body(dy_ref, w_ref, *rest):
        dyb_ref, o_ref = rest[-2:]
        dyb = dy_ref[...].astype(BF16)
        dyb_ref[...] = dyb
        for p in range(pieces):
            for c0 in range(0, width, chunk):
                r0 = p * width + c0
                o_ref[p, :, c0:c0 + chunk] = _nt(dyb, w_ref[r0:r0 + chunk, :]).astype(out_dtype)

    in_specs = [pl.BlockSpec((tm, D), lambda i: (i, 0)),
                pl.BlockSpec((None, pieces * width, D), lambda i: (layer, 0, 0))]
    operands = [dy, w3]
    if after is not None:
        in_specs.append(ANY)
        operands.append(after)
    return pl.pallas_call(
        body, grid=(T // tm,), name=name,
        in_specs=in_specs,
        out_specs=[pl.BlockSpec((tm, D), lambda i: (i, 0)),
                   pl.BlockSpec((pieces, tm, width), lambda i: (0, i, 0))],
        out_shape=[jax.ShapeDtypeStruct((T, D), BF16),
                   jax.ShapeDtypeStruct((pieces, T, width), out_dtype)],
        compiler_params=_params(("parallel",)),
    )(*operands)


def _grad_through_proj_norm(dp3, w3, layer, x, g, dx_in, tm, name):
    P, T, C = dp3.shape
    D = w3.shape[1]

    def body(dp_ref, w_ref, x_ref, g_ref, dxin_ref, dx_ref, dg_ref):
        dh = _nt(dp_ref[0], w_ref[:, 0:C])
        for p in range(1, P):
            dh = dh + _nt(dp_ref[p], w_ref[:, p * C:(p + 1) * C])
        xv = x_ref[...]
        rstd = lax.rsqrt(jnp.mean(xv * xv, axis=-1, keepdims=True) + EPS)
        xn = xv * rstd
        a = dh * g_ref[...]
        dx_ref[...] = dxin_ref[...] + rstd * (a - xn * jnp.mean(a * xn, axis=-1, keepdims=True))
        part = jnp.sum(dh * xn, axis=0, keepdims=True)

        @pl.when(pl.program_id(0) == 0)
        def _():
            dg_ref[...] = part

        @pl.when(pl.program_id(0) != 0)
        def _():
            dg_ref[...] += part

    return pl.pallas_call(
        body, grid=(T // tm,), name=name,
        in_specs=[pl.BlockSpec((P, tm, C), lambda i: (0, i, 0)),
                  pl.BlockSpec((None, D, P * C), lambda i: (layer, 0, 0)),
                  pl.BlockSpec((tm, D), lambda i: (i, 0)),
                  pl.BlockSpec((1, D), lambda i: (0, 0)),
                  pl.BlockSpec((tm, D), lambda i: (i, 0))],
        out_specs=[pl.BlockSpec((tm, D), lambda i: (i, 0)),
                   pl.BlockSpec((1, D), lambda i: (0, 0))],
        out_shape=[jax.ShapeDtypeStruct((T, D), F32), jax.ShapeDtypeStruct((1, D), F32)],
        compiler_params=_params(("arbitrary",), 56),
    )(dp3, w3, x, g, dx_in)


def _weight_grad(a3, g3, ta, tg, tt, layer, n_layers, prev, name):
    PA, T, CA = a3.shape
    PG, _, CG = g3.shape
    na, ng, nt = CA // ta, CG // tg, T // tt
    assert CA % ta == 0 and CG % tg == 0 and T % tt == 0

    def body(a_ref, g_ref, *rest):
        o_ref, acc_ref = rest[-2:]
        t = pl.program_id(2)
        part = _tn(a_ref[...], g_ref[...])

        @pl.when(t == 0)
        def _():
            acc_ref[...] = part

        @pl.when(t != 0)
        def _():
            acc_ref[...] += part

        @pl.when(t == nt - 1)
        def _():
            o_ref[...] = acc_ref[...].astype(o_ref.dtype)

    in_specs = [pl.BlockSpec((None, tt, ta), lambda i, j, t: (i // na, t, i % na)),
                pl.BlockSpec((None, tt, tg), lambda i, j, t: (j // ng, t, j % ng))]
    operands = [a3, g3]
    if prev is not None:
        in_specs.append(pl.BlockSpec(memory_space=pl.ANY))
        operands.append(prev)
    return pl.pallas_call(
        body, grid=(PA * na, PG * ng, nt), name=name,
        in_specs=in_specs,
        out_specs=pl.BlockSpec((None, ta, tg), lambda i, j, t: (layer, i, j)),
        out_shape=jax.ShapeDtypeStruct((n_layers, PA * CA, PG * CG), BF16),
        scratch_shapes=[pltpu.VMEM((ta, tg), F32)],
        input_output_aliases={} if prev is None else {2: 0},
        compiler_params=_params(("parallel", "parallel", "arbitrary"), 56),
    )(*operands)


def _bias_tables(bm_ref, pair, n_heads):
    ii = lax.broadcasted_iota(jnp.int32, (BAND, 2 * BAND), 0)
    jj = lax.broadcasted_iota(jnp.int32, (BAND, 2 * BAND), 1)
    dist = BAND + ii - jj
    valid = (dist >= 0) & (dist <= BAND)
    distf = dist.astype(F32)
    for hh in range(2):
        head = (2 * pair + hh + 1).astype(F32)
        slope = jnp.exp(jnp.full((1, 1), -8.0 / n_heads * math.log(2.0), F32) * head)
        for bi, d in enumerate(DILATIONS):
            bm_ref[bi, hh * BAND:(hh + 1) * BAND, :] = jnp.where(valid, -(slope * d) * distf, NEG)


def _stack_heads(x, is_a):
    zero = jnp.zeros_like(x)
    return jnp.concatenate([jnp.where(is_a, x, zero), jnp.where(is_a, zero, x)], axis=0)


def _unstack_heads(x2, is_a):
    return jnp.where(is_a, x2[0:BAND], x2[BAND:2 * BAND])


def _gather_residues(dst_ref, src, d, S, convert):
    L = S // d
    for r in range(d):
        rows = pl.ds(r, L, stride=d) if d > 1 else slice(None)
        dst_ref[r * L:(r + 1) * L, :] = convert(src(rows))


def _block_rows(t, d, S):
    nb = S // (BAND * d)
    n = t % nb
    has_prev = jnp.minimum(n, 1)
    cur = pl.ds(pl.multiple_of(t * BAND, BAND), BAND)
    prev = pl.ds(pl.multiple_of((t - has_prev) * BAND, BAND), BAND)
    return cur, prev, has_prev


def _first_block_penalty(has_prev):
    jrow = lax.broadcasted_iota(jnp.int32, (1, 2 * BAND), 1)
    pen = jnp.where(has_prev == 0, NEG, 0.0).astype(F32)
    return jnp.where(jrow < BAND, pen, 0.0)


def _attn_fwd(qkv3, gain, mix_shape_pieces, S, n_heads, name):
    _, T, C = qkv3.shape
    B, P = T // S, C // LANES
    NBLK = S // BAND
    scale = GROUP ** -0.5
    nbr = len(DILATIONS)
    RC = 256

    def body(qkv_ref, g_ref, o_ref, lse_ref, an_ref, qs, ks, vs, op, mp, lp, ob, mb, lb, bm):
        pair = pl.program_id(1)
        is_a = _lane_is_a()
        _bias_tables(bm, pair, n_heads)

        for bi, d in enumerate(DILATIONS):
            nb = S // (BAND * d)
            _gather_residues(qs, lambda rows: qkv_ref.at[0][rows, :], d, S, lambda v: (v * scale).astype(BF16))
            _gather_residues(ks, lambda rows: qkv_ref.at[1][rows, :], d, S, lambda v: v.astype(BF16))
            _gather_residues(vs, lambda rows: qkv_ref.at[2][rows, :], d, S, lambda v: v.astype(BF16))
            o_dst, m_dst, l_dst = (ob.at[bi], mb.at[bi], lb.at[bi]) if d == 1 else (op, mp, lp)

            def block(t, carry, bi=bi, d=d, nb=nb, o_dst=o_dst, m_dst=m_dst, l_dst=l_dst):
                cur, prev, has_prev = _block_rows(t, d, S)
                q2 = _stack_heads(qs[cur, :], is_a)
                kc = jnp.concatenate([ks[prev, :], ks[cur, :]], axis=0)
                vc = jnp.concatenate([vs[prev, :], vs[cur, :]], axis=0)
                s = _nt(q2, kc) + bm[bi] + _first_block_penalty(has_prev)
                m = jnp.max(s, axis=-1, keepdims=True)
                e = jnp.exp(s - m)
                l = jnp.sum(e, axis=-1, keepdims=True)
                pv = jnp.dot(e.astype(BF16), vc, preferred_element_type=F32)
                o_dst[cur, :] = _unstack_heads(pv, is_a)
                m_dst[cur, :] = _unstack_heads(m, is_a)
                l_dst[cur, :] = _unstack_heads(l, is_a)
                return carry

            lax.fori_loop(0, NBLK, block, 0, unroll=8)
            if d > 1:
                L = S // d
                for r in range(d):
                    rows = pl.ds(r, L, stride=d)
                    ob.at[bi][rows, :] = op[r * L:(r + 1) * L, :]
                    mb.at[bi][rows, :] = mp[r * L:(r + 1) * L, :]
                    lb.at[bi][rows, :] = lp[r * L:(r + 1) * L, :]

        def finish(ci, carry):
            rs = pl.ds(pl.multiple_of(ci * RC, RC), RC)
            ms = [mb[bi, rs, :] for bi in range(nbr)]
            mmax = functools.reduce(jnp.maximum, ms)
            ws = [jnp.exp(m - mmax) for m in ms]
            num = sum(ob[bi, rs, :] * ws[bi] for bi in range(nbr))
            den = sum(lb[bi, rs, :] * ws[bi] for bi in range(nbr))
            o = num / den
            o_ref[rs, :] = o
            lse_ref[rs, :] = mmax + jnp.log(den)
            rstd = lax.rsqrt(_seg_sum(o * o, is_a) * (1.0 / GROUP) + EPS)
            an_ref[rs, :] = ((o * rstd) * g_ref[...]).astype(BF16)
            return carry

        lax.fori_loop(0, S // RC, finish, 0)

    seq = pl.BlockSpec((S, LANES), lambda b, p: (b, p))
    return pl.pallas_call(
        body, grid=(B, P), name=name,
        in_specs=[pl.BlockSpec((3, S, LANES), lambda b, p: (0, b, p)),
                  pl.BlockSpec((1, LANES), lambda b, p: (0, p))],
        out_specs=[seq, seq, pl.BlockSpec((None, S, LANES), lambda b, p: (0, b, p))],
        out_shape=[jax.ShapeDtypeStruct((T, C), F32), jax.ShapeDtypeStruct((T, C), F32),
                   jax.ShapeDtypeStruct((mix_shape_pieces, T, C), BF16)],
        scratch_shapes=[pltpu.VMEM((S, LANES), BF16)] * 3 + [pltpu.VMEM((S, LANES), F32)] * 3
        + [pltpu.VMEM((nbr, S, LANES), F32)] * 3 + [pltpu.VMEM((nbr, 2 * BAND, 2 * BAND), F32)],
        compiler_params=_params(("parallel", "parallel")),
    )(qkv3, gain)


def _attn_bwd(qkv3, o, lse, dmix3, gain, dproj_pieces, S, n_heads, name):
    _, T, C = qkv3.shape
    B, P = T // S, C // LANES
    NBLK = S // BAND
    scale = GROUP ** -0.5
    nbr = len(DILATIONS)
    RC = 256

    def body(qkv_ref, o_ref, lse_ref, dn_ref, g_ref, dqkv_ref, dg_ref,
             do_n, dd_n, qs, ks, vs, dos, lses, dds, dqp, dkp, dvp, dqn, dkn, dvn, bm):
        pair = pl.program_id(0)
        b = pl.program_id(1)
        is_a = _lane_is_a()
        _bias_tables(bm, pair, n_heads)

        def prologue(ci, dg_acc):
            rs = pl.ds(pl.multiple_of(ci * RC, RC), RC)
            ov = o_ref[rs, :]
            dn = dn_ref[rs, :]
            rstd = lax.rsqrt(_seg_sum(ov * ov, is_a) * (1.0 / GROUP) + EPS)
            on = ov * rstd
            a = dn * g_ref[...]
            do = rstd * (a - on * (_seg_sum(a * on, is_a) * (1.0 / GROUP)))
            do_n[rs, :] = do
            dd_n[rs, :] = _seg_sum(do * ov, is_a)
            zero = jnp.zeros((RC, LANES), F32)
            dqn[rs, :] = zero
            dkn[rs, :] = zero
            dvn[rs, :] = zero
            return dg_acc + jnp.sum(dn * on, axis=0, keepdims=True)

        dg_part = lax.fori_loop(0, S // RC, prologue, jnp.zeros((1, LANES), F32))

        @pl.when(b == 0)
        def _():
            dg_ref[...] = dg_part

        @pl.when(b != 0)
        def _():
            dg_ref[...] += dg_part

        for bi, d in enumerate(DILATIONS):
            nb = S // (BAND * d)
            L = S // d
            _gather_residues(qs, lambda rows: qkv_ref.at[0][rows, :], d, S, lambda v: (v * scale).astype(BF16))
            _gather_residues(ks, lambda rows: qkv_ref.at[1][rows, :], d, S, lambda v: v.astype(BF16))
            _gather_residues(vs, lambda rows: qkv_ref.at[2][rows, :], d, S, lambda v: v.astype(BF16))
            _gather_residues(dos, lambda rows: do_n[rows, :], d, S, lambda v: v.astype(BF16))
            if d == 1:
                lse_src, dd_src, dq_dst, dk_dst, dv_dst = lse_ref, dd_n, dqn, dkn, dvn
            else:
                _gather_residues(lses, lambda rows: lse_ref[rows, :], d, S, lambda v: v)
                _gather_residues(dds, lambda rows: dd_n[rows, :], d, S, lambda v: v)
                dkp[...] = jnp.zeros((S, LANES), F32)
                dvp[...] = jnp.zeros((S, LANES), F32)
                lse_src, dd_src, dq_dst, dk_dst, dv_dst = lses, dds, dqp, dkp, dvp

            def block(t, carry, bi=bi, d=d, lse_src=lse_src, dd_src=dd_src, dq_dst=dq_dst, dk_dst=dk_dst,
                      dv_dst=dv_dst):
                cur, prev, has_prev = _block_rows(t, d, S)
                q2 = _stack_heads(qs[cur, :], is_a)
                do2 = _stack_heads(dos[cur, :], is_a)
                lse_t = lse_src[cur, :]
                dd_t = dd_src[cur, :]
                lse2 = jnp.concatenate([lse_t[:, 0:1], lse_t[:, GROUP:GROUP + 1]], axis=0)
                dd2 = jnp.concatenate([dd_t[:, 0:1], dd_t[:, GROUP:GROUP + 1]], axis=0)
                kc = jnp.concatenate([ks[prev, :], ks[cur, :]], axis=0)
                vc = jnp.concatenate([vs[prev, :], vs[cur, :]], axis=0)
                s = _nt(q2, kc) + bm[bi] + _first_block_penalty(has_prev)
                p = jnp.exp(s - lse2)
                ds = (p * (_nt(do2, vc) - dd2)).astype(BF16)
                dq = _unstack_heads(jnp.dot(ds, kc, preferred_element_type=F32), is_a)
                dk = _tn(ds, q2)
                dv = _tn(p.astype(BF16), do2)
                dq_dst[cur, :] = dq
                dk_dst[prev, :] += dk[0:BAND, :]
                dv_dst[prev, :] += dv[0:BAND, :]
                dk_dst[cur, :] += dk[BAND:2 * BAND, :]
                dv_dst[cur, :] += dv[BAND:2 * BAND, :]
                return carry

            lax.fori_loop(0, NBLK, block, 0, unroll=8)
            if d > 1:
                for r in range(d):
                    rows = pl.ds(r, L, stride=d)
                    dqn[rows, :] += dqp[r * L:(r + 1) * L, :]
                    dkn[rows, :] += dkp[r * L:(r + 1) * L, :]
                    dvn[rows, :] += dvp[r * L:(r + 1) * L, :]

        dqkv_ref[0] = (dqn[...] * scale).astype(BF16)
        dqkv_ref[1] = dkn[...].astype(BF16)
        dqkv_ref[2] = dvn[...].astype(BF16)

    seq = pl.BlockSpec((S, LANES), lambda p, b: (b, p))
    f32_seq = pltpu.VMEM((S, LANES), F32)
    bf_seq = pltpu.VMEM((S, LANES), BF16)
    return pl.pallas_call(
        body, grid=(P, B), name=name,
        in_specs=[pl.BlockSpec((3, S, LANES), lambda p, b: (0, b, p)), seq, seq,
                  pl.BlockSpec((None, S, LANES), lambda p, b: (0, b, p)),
                  pl.BlockSpec((1, LANES), lambda p, b: (0, p))],
        out_specs=[pl.BlockSpec((3, S, LANES), lambda p, b: (0, b, p)),
                   pl.BlockSpec((1, LANES), lambda p, b: (0, p))],
        out_shape=[jax.ShapeDtypeStruct((dproj_pieces, T, C), BF16), jax.ShapeDtypeStruct((1, C), F32)],
        scratch_shapes=[f32_seq, f32_seq, bf_seq, bf_seq, bf_seq, bf_seq, f32_seq, f32_seq,
                        f32_seq, f32_seq, f32_seq, f32_seq, f32_seq, f32_seq,
                        pltpu.VMEM((nbr, 2 * BAND, 2 * BAND), F32)],
        compiler_params=_params(("parallel", "arbitrary")),
    )(qkv3, o, lse, dmix3, gain)


def _delay(x, k, row):
    return jnp.where(row >= k, pltpu.roll(x, k, 0), 0.0)


def _advance(x, k, row, S):
    return jnp.where(row < S - k, pltpu.roll(x, S - k, 0), 0.0)


def _conv3(x, w, row):
    return (w[0:1, :] * _delay(x, 2, row) + w[1:2, :] * _delay(x, 1, row)) + w[2:3, :] * x


HALO = 8


CONV_ROWS = 128


def _zero_halo(pad_ref, S):
    zeros = jnp.zeros((HALO, pad_ref.shape[1]), pad_ref.dtype)
    pad_ref[0:HALO, :] = zeros
    pad_ref[HALO + S:2 * HALO + S, :] = zeros


def _window_at(pad_ref, r0, shift):
    return pad_ref[HALO + r0 + shift:HALO + r0 + shift + CONV_ROWS, :]


def _conv3_at(pad_ref, w, r0):
    return ((w[0:1, :] * _window_at(pad_ref, r0, -2) + w[1:2, :] * _window_at(pad_ref, r0, -1))
            + w[2:3, :] * _window_at(pad_ref, r0, 0))


def _conv3_grads_at(dz_ref, x_ref, w, r0):
    dz, dz1, dz2 = (_window_at(dz_ref, r0, k) for k in range(3))
    x = _window_at(x_ref, r0, 0)
    dx = (w[2:3, :] * dz + w[1:2, :] * dz1) + w[0:1, :] * dz2
    parts = [jnp.sum((d * x).reshape(CONV_ROWS // 8, 8, x.shape[1]), axis=0) for d in (dz2, dz1, dz)]
    return dx, parts


def _conv3_grads(dz, x, w, row, S):
    dz1 = _advance(dz, 1, row, S)
    dz2 = _advance(dz, 2, row, S)
    dx = (w[2:3, :] * dz + w[1:2, :] * dz1) + w[0:1, :] * dz2
    dw = jnp.concatenate([jnp.sum(dz2 * x, axis=0, keepdims=True),
                          jnp.sum(dz1 * x, axis=0, keepdims=True),
                          jnp.sum(dz * x, axis=0, keepdims=True)], axis=0)
    return dx, dw


def _mix_conv_fwd(cv3, taps, gain, mix, S, name):
    _, T, C = cv3.shape
    B, P = T // S, C // LANES

    def body(cv_ref, w_ref, g_ref, mix_hbm, y_ref):
        del mix_hbm
        row = lax.broadcasted_iota(jnp.int32, (S, 1), 0)
        is_a = _lane_is_a()
        gb = cv_ref[0].astype(F32)
        c = cv_ref[1].astype(F32) * cv_ref[2].astype(F32)
        y = gb * _conv3(c, w_ref[...], row)
        rstd = lax.rsqrt(_seg_sum(y * y, is_a) * (1.0 / GROUP) + EPS)
        y_ref[...] = ((y * rstd) * g_ref[...]).astype(BF16)

    return pl.pallas_call(
        body, grid=(B, P), name=name,
        in_specs=[pl.BlockSpec((3, S, LANES), lambda b, p: (0, b, p)),
                  pl.BlockSpec((3, LANES), lambda b, p: (0, p)),
                  pl.BlockSpec((1, LANES), lambda b, p: (0, p)),
                  pl.BlockSpec(memory_space=pl.ANY)],
        out_specs=pl.BlockSpec((None, S, LANES), lambda b, p: (1, b, p)),
        out_shape=jax.ShapeDtypeStruct(mix.shape, mix.dtype),
        input_output_aliases={3: 0},
        compiler_params=_params(("parallel", "parallel")),
    )(cv3, taps, gain, mix)


def _mix_conv_bwd(cv3, dmix3, taps, gain, dproj, S, name):
    _, T, C = cv3.shape
    B, P = T // S, C // LANES

    def body(cv_ref, dn_ref, w_ref, g_ref, dproj_hbm, dcv_ref, dw_ref, dg_ref):
        del dproj_hbm
        b = pl.program_id(1)
        row = lax.broadcasted_iota(jnp.int32, (S, 1), 0)
        is_a = _lane_is_a()
        w = w_ref[...]
        gb = cv_ref[0].astype(F32)
        gc = cv_ref[1].astype(F32)
        u = cv_ref[2].astype(F32)
        c = gc * u
        z = _conv3(c, w, row)
        y = gb * z
        rstd = lax.rsqrt(_seg_sum(y * y, is_a) * (1.0 / GROUP) + EPS)
        yn = y * rstd
        dn = dn_ref[...]
        a = dn * g_ref[...]
        dy = rstd * (a - yn * (_seg_sum(a * yn, is_a) * (1.0 / GROUP)))
        dg = jnp.sum(dn * yn, axis=0, keepdims=True)
        dc, dw = _conv3_grads(dy * gb, c, w, row, S)
        dcv_ref[0] = (dy * z).astype(BF16)
        dcv_ref[1] = (dc * u).astype(BF16)
        dcv_ref[2] = (dc * gc).astype(BF16)

        @pl.when(b == 0)
        def _():
            dw_ref[...] = dw
            dg_ref[...] = dg

        @pl.when(b != 0)
        def _():
            dw_ref[...] += dw
            dg_ref[...] += dg

    return pl.pallas_call(
        body, grid=(P, B), name=name,
        in_specs=[pl.BlockSpec((3, S, LANES), lambda p, b: (0, b, p)),
                  pl.BlockSpec((None, S, LANES), lambda p, b: (1, b, p)),
                  pl.BlockSpec((3, LANES), lambda p, b: (0, p)),
                  pl.BlockSpec((1, LANES), lambda p, b: (0, p)),
                  pl.BlockSpec(memory_space=pl.ANY)],
        out_specs=[pl.BlockSpec((3, S, LANES), lambda p, b: (1, b, p)),
                   pl.BlockSpec((3, LANES), lambda p, b: (0, p)),
                   pl.BlockSpec((1, LANES), lambda p, b: (0, p))],
        out_shape=[jax.ShapeDtypeStruct(dproj.shape, dproj.dtype),
                   jax.ShapeDtypeStruct((3, C), F32), jax.ShapeDtypeStruct((1, C), F32)],
        input_output_aliases={4: 0},
        compiler_params=_params(("parallel", "arbitrary")),
    )(cv3, dmix3, taps, gain, dproj)


def _sigmoid(x):
    return 0.5 * jnp.tanh(0.5 * x) + 0.5


def _ffn_act_fwd(up3, taps, S, name):
    _, T, Fd = up3.shape
    B, P = T // S, Fd // LANES

    def body(up_ref, wg_ref, wv_ref, act_ref, pad_g, pad_v):
        _zero_halo(pad_g, S)
        _zero_halo(pad_v, S)
        pad_g[HALO:HALO + S, :] = up_ref[0].astype(F32)
        pad_v[HALO:HALO + S, :] = up_ref[1].astype(F32)
        wg = wg_ref[...]
        wv = wv_ref[...]
        for r0 in range(0, S, CONV_ROWS):
            cg = _conv3_at(pad_g, wg, r0)
            cv = _conv3_at(pad_v, wv, r0)
            act_ref[r0:r0 + CONV_ROWS, :] = ((cg * _sigmoid(cg)) * cv).astype(BF16)

    return pl.pallas_call(
        body, grid=(B, P), name=name,
        in_specs=[pl.BlockSpec((2, S, LANES), lambda b, p: (0, b, p)),
                  pl.BlockSpec((3, LANES), lambda b, p: (0, p)),
                  pl.BlockSpec((3, LANES), lambda b, p: (0, P + p))],
        out_specs=pl.BlockSpec((S, LANES), lambda b, p: (b, p)),
        out_shape=jax.ShapeDtypeStruct((T, Fd), BF16),
        scratch_shapes=[pltpu.VMEM((S + 2 * HALO, LANES), F32)] * 2,
        compiler_params=_params(("parallel", "parallel")),
    )(up3, taps, taps)


def _ffn_act_bwd(up3, dact3, taps, S, name):
    _, T, Fd = up3.shape
    B, P = T // S, Fd // LANES

    def body(up_ref, da_ref, wg_ref, wv_ref, dup_ref, dwg_ref, dwv_ref, pad_ug, pad_uv, pad_dg, pad_dv):
        b = pl.program_id(1)
        for pad in (pad_ug, pad_uv, pad_dg, pad_dv):
            _zero_halo(pad, S)
        pad_ug[HALO:HALO + S, :] = up_ref[0].astype(F32)
        pad_uv[HALO:HALO + S, :] = up_ref[1].astype(F32)
        wg = wg_ref[...]
        wv = wv_ref[...]
        for r0 in range(0, S, CONV_ROWS):
            cg = _conv3_at(pad_ug, wg, r0)
            cv = _conv3_at(pad_uv, wv, r0)
            sg = _sigmoid(cg)
            da = da_ref[r0:r0 + CONV_ROWS, :].astype(F32)
            pad_dg[HALO + r0:HALO + r0 + CONV_ROWS, :] = (da * cv) * (sg * (1.0 + cg * (1.0 - sg)))
            pad_dv[HALO + r0:HALO + r0 + CONV_ROWS, :] = da * (cg * sg)
        sums_g = [jnp.zeros((8, LANES), F32)] * 3
        sums_v = [jnp.zeros((8, LANES), F32)] * 3
        for r0 in range(0, S, CONV_ROWS):
            dug, parts_g = _conv3_grads_at(pad_dg, pad_ug, wg, r0)
            duv, parts_v = _conv3_grads_at(pad_dv, pad_uv, wv, r0)
            dup_ref[0, r0:r0 + CONV_ROWS, :] = dug.astype(BF16)
            dup_ref[1, r0:r0 + CONV_ROWS, :] = duv.astype(BF16)
            sums_g = [a + p for a, p in zip(sums_g, parts_g)]
            sums_v = [a + p for a, p in zip(sums_v, parts_v)]
        dwg = jnp.concatenate([jnp.sum(a, axis=0, keepdims=True) for a in sums_g], axis=0)
        dwv = jnp.concatenate([jnp.sum(a, axis=0, keepdims=True) for a in sums_v], axis=0)

        @pl.when(b == 0)
        def _():
            dwg_ref[...] = dwg
            dwv_ref[...] = dwv

        @pl.when(b != 0)
        def _():
            dwg_ref[...] += dwg
            dwv_ref[...] += dwv

    tap_out = pl.BlockSpec((3, LANES), lambda p, b: (0, p))
    return pl.pallas_call(
        body, grid=(P, B), name=name,
        in_specs=[pl.BlockSpec((2, S, LANES), lambda p, b: (0, b, p)),
                  pl.BlockSpec((None, S, LANES), lambda p, b: (0, b, p)),
                  pl.BlockSpec((3, LANES), lambda p, b: (0, p)),
                  pl.BlockSpec((3, LANES), lambda p, b: (0, P + p))],
        out_specs=[pl.BlockSpec((2, S, LANES), lambda p, b: (0, b, p)), tap_out, tap_out],
        out_shape=[jax.ShapeDtypeStruct((2, T, Fd), BF16),
                   jax.ShapeDtypeStruct((3, Fd), F32), jax.ShapeDtypeStruct((3, Fd), F32)],
        scratch_shapes=[pltpu.VMEM((S + 2 * HALO, LANES), F32)] * 4,
        compiler_params=_params(("parallel", "arbitrary")),
    )(up3, dact3, taps, taps)


def _final_norm_loss(x, g, target, tm, name):
    T, D = x.shape

    def body(x_ref, g_ref, t_ref, dx_ref, dg_ref, loss_ref):
        xv = x_ref[...]
        rstd = lax.rsqrt(jnp.mean(xv * xv, axis=-1, keepdims=True) + EPS)
        xn = xv * rstd
        err = xn * g_ref[...] - t_ref[...]
        part = 0.5 * jnp.sum(jnp.mean(err * err, axis=-1, keepdims=True), axis=0, keepdims=True)
        dy = err * (1.0 / D)
        a = dy * g_ref[...]
        dx_ref[...] = rstd * (a - xn * jnp.mean(a * xn, axis=-1, keepdims=True))
        dg = jnp.sum(dy * xn, axis=0, keepdims=True)
        lpart = jnp.broadcast_to(part, (1, LANES))

        @pl.when(pl.program_id(0) == 0)
        def _():
            dg_ref[...] = dg
            loss_ref[...] = lpart

        @pl.when(pl.program_id(0) != 0)
        def _():
            dg_ref[...] += dg
            loss_ref[...] += lpart

    row = pl.BlockSpec((tm, D), lambda i: (i, 0))
    return pl.pallas_call(
        body, grid=(T // tm,), name=name,
        in_specs=[row, pl.BlockSpec((1, D), lambda i: (0, 0)), row],
        out_specs=[row, pl.BlockSpec((1, D), lambda i: (0, 0)), pl.BlockSpec((1, LANES), lambda i: (0, 0))],
        out_shape=[jax.ShapeDtypeStruct((T, D), F32), jax.ShapeDtypeStruct((1, D), F32),
                   jax.ShapeDtypeStruct((1, LANES), F32)],
        compiler_params=_params(("arbitrary",)),
    )(x, g, target)


def _row_tile(rows, cols, budget_elems=512 * 1024):
    tr = rows
    while tr * cols > budget_elems and tr % 32 == 0:
        tr //= 2
    return tr


def _prefetch_call(body, grid, in_specs, out_specs, out_shape, name, sem, aliases=None):
    return pl.pallas_call(
        body, name=name, out_shape=out_shape,
        grid_spec=pltpu.PrefetchScalarGridSpec(num_scalar_prefetch=1, grid=grid, in_specs=in_specs,
                                               out_specs=out_specs),
        input_output_aliases=aliases or {},
        compiler_params=_params(sem))


def _cast_into_full(w, layer, colwise, where, name):
    _, K, N = w.shape
    tr = _row_tile(K, N)
    nrb = K // tr
    full_shape = (1, K, 4 * N) if colwise else (1, 4 * K, N)

    def body(where_ref, w_ref, o_ref):
        del where_ref
        o_ref[...] = w_ref[...].astype(BF16)

    if colwise:
        out_map = lambda i, wh: (0, i, wh[0])
    else:
        out_map = lambda i, wh: (0, wh[0] * nrb + i, 0)
    return _prefetch_call(
        body, (nrb,), [pl.BlockSpec((None, tr, N), lambda i, wh: (layer, i, 0))],
        pl.BlockSpec((None, tr, N), out_map), jax.ShapeDtypeStruct(full_shape, BF16), name,
        ("parallel",))(where, w)


def _chip_sum(g3, other, colwise, where, name):
    L, K, N = g3.shape
    hk, hn = (K // 2, N) if colwise else (K, N // 2)
    tr = _row_tile(hk, hn)
    nrb = hk // tr

    def body(where_ref, g_ref, o_ref, s_ref):
        del where_ref
        s_ref[...] = (g_ref[...].astype(F32) + o_ref[...].astype(F32)).astype(BF16)

    if colwise:
        g_map = lambda l, i, wh: (l, wh[1] * nrb + i, 0)
    else:
        g_map = lambda l, i, wh: (l, i, wh[1])
    blk = pl.BlockSpec((None, tr, hn), lambda l, i, wh: (l, i, 0))
    return _prefetch_call(
        body, (L, nrb), [pl.BlockSpec((None, tr, hn), g_map), blk], blk,
        jax.ShapeDtypeStruct((L, hk, hn), BF16), name, ("parallel", "parallel"))(where, g3, other)


def _owner_sum(chip_sum, received, colwise, where, layer, n_layers, prev, name):
    _, hk, hn = chip_sum.shape
    pk, pn = (hk, hn // 4) if colwise else (hk // 4, hn)
    tr = _row_tile(pk, pn)
    nrb = pk // tr
    shard_shape = (n_layers, 2 * pk, pn) if colwise else (n_layers, pk, 2 * pn)

    def body(where_ref, own_ref, rec_ref, *rest):
        del where_ref
        o_ref = rest[-1]
        acc = own_ref[...].astype(F32)
        for j in range(3):
            acc = acc + rec_ref[j].astype(F32)
        o_ref[...] = acc

    if colwise:
        own_map = lambda i, wh: (0, i, wh[0])
        out_map = lambda i, wh: (layer, wh[1] * nrb + i, 0)
    else:
        own_map = lambda i, wh: (0, wh[0] * nrb + i, 0)
        out_map = lambda i, wh: (layer, i, wh[1])
    in_specs = [pl.BlockSpec((None, tr, pn), own_map),
                pl.BlockSpec((3, None, tr, pn), lambda i, wh: (0, 0, i, 0))]
    operands = [where, chip_sum, received]
    if prev is not None:
        in_specs.append(ANY)
        operands.append(prev)
    return _prefetch_call(
        body, (nrb,), in_specs, pl.BlockSpec((None, tr, pn), out_map), jax.ShapeDtypeStruct(shard_shape, F32), name,
        ("parallel",), None if prev is None else {3: 0})(*operands)


def _adamw(w, g, m, v, name):
    R, Cc = w.shape
    tr = _row_tile(R, Cc, 256 * 1024)

    def body(w_ref, g_ref, m_ref, v_ref, d_ref, nm_ref, nv_ref):
        gv = g_ref[...]
        nm = ADAM_B1 * m_ref[...] + (1.0 - ADAM_B1) * gv
        nv = ADAM_B2 * v_ref[...] + (1.0 - ADAM_B2) * (gv * gv)
        m_hat = nm / (1.0 - ADAM_B1 ** ADAM_STEP)
        v_hat = nv / (1.0 - ADAM_B2 ** ADAM_STEP)
        d_ref[...] = -ADAM_LR * (m_hat / (jnp.sqrt(v_hat) + ADAM_EPS) + ADAM_WD * w_ref[...])
        nm_ref[...] = nm
        nv_ref[...] = nv

    blk = pl.BlockSpec((tr, Cc), lambda i: (i, 0))
    shp = jax.ShapeDtypeStruct((R, Cc), F32)
    return pl.pallas_call(
        body, grid=(R // tr,), name=name,
        in_specs=[blk] * 4, out_specs=[blk] * 3, out_shape=[shp] * 3,
        compiler_params=_params(("parallel",)),
    )(w, g, m, v)


COL_SHARDED = (True, False, True, False)


def _position():
    x, y, c = lax.axis_index("x"), lax.axis_index("y"), lax.axis_index("c")
    chips = [(1 - x, y), (x, 1 - y), (1 - x, 1 - y)]
    return x, y, c, chips


def _span(index, size, align):
    return pl.ds(pl.multiple_of(index * size, align), size)


def _window(ref, colwise, shard, half, shards=4):
    _, K, N = ref.shape
    rows = cols = slice(None)
    if colwise:
        if half is not None:
            rows = _span(half, K // 2, 16)
        if shard is not None:
            cols = _span(shard, N // shards, LANES)
    else:
        if shard is not None:
            rows = _span(shard, K // shards, 16)
        if half is not None:
            cols = _span(half, N // 2, LANES)
    return ref.at[:, rows, cols]


HBM = pl.BlockSpec(memory_space=pltpu.HBM)
SEMAPHORES = pl.BlockSpec(memory_space=pltpu.SEMAPHORE)


def _gather_start(fulls, colwise, group_sizes, after, name):
    n = len(fulls)
    n_groups = len(group_sizes)

    n_in = n if after is None else n + 1

    def body(*refs):
        ins = refs[:n]
        sems = refs[n_in:n_in + 2 * n_groups]
        x, y, c, chips = _position()
        me = 2 * x + y
        i = 0
        for g, size in enumerate(group_sizes):
            for a in range(size):
                win = _window(ins[i], colwise[i], me, c)
                for j, chip in enumerate(chips):
                    pltpu.make_async_remote_copy(
                        src_ref=win, dst_ref=win, send_sem=sems[2 * g].at[a * 3 + j],
                        recv_sem=sems[2 * g + 1].at[a * 3 + j],
                        device_id=(chip[0], chip[1], c), device_id_type=MESH_ID).start()
                i += 1

    sem_shapes = []
    for size in group_sizes:
        sem_shapes += [pltpu.SemaphoreType.DMA((3 * size,)), pltpu.SemaphoreType.DMA((3 * size,))]
    operands = [pltpu.with_memory_space_constraint(f, pltpu.HBM) for f in fulls]
    in_specs = [HBM] * n
    if after is not None:
        operands.append(after)
        in_specs.append(ANY)
    outs = pl.pallas_call(
        body, name=name,
        in_specs=in_specs, out_specs=[SEMAPHORES] * (2 * n_groups) + [HBM] * n,
        out_shape=sem_shapes + [pltpu.HBM(f.shape, f.dtype) for f in fulls],
        input_output_aliases={i: 2 * n_groups + i for i in range(n)},
        compiler_params=pltpu.CompilerParams(has_side_effects=pltpu.SideEffectType.DATAFLOW_SIDE_EFFECTING),
    )(*operands)
    sems = [(outs[2 * g], outs[2 * g + 1]) for g in range(n_groups)]
    return sems, list(outs[2 * n_groups:])


def _to_sibling(ref, colwise, chip, half, x, y, c, send_sem, recv_sem):
    win = _window(ref, colwise, 2 * chip[0] + chip[1], half)
    return pltpu.make_async_remote_copy(
        src_ref=win, dst_ref=win, send_sem=send_sem, recv_sem=recv_sem,
        device_id=(x, y, 1 - c), device_id_type=MESH_ID)


def _gather_pass(in_flight, colwise, sems, after, name):
    n = len(in_flight)

    def body(*refs):
        ins = refs[:n]
        send_sems, recv_sems = refs[n], refs[n + 1]
        pass_send, pass_recv = refs[-2 - n], refs[-1 - n]
        x, y, c, chips = _position()
        me = 2 * x + y
        for a in range(n):
            for j, chip in enumerate(chips):
                k = a * 3 + j
                pltpu.make_async_remote_copy(
                    src_ref=_window(ins[a], colwise[a], me, c),
                    dst_ref=_window(ins[a], colwise[a], 2 * chip[0] + chip[1], c),
                    send_sem=send_sems.at[k], recv_sem=recv_sems.at[k],
                    device_id=(chip[0], chip[1], c), device_id_type=MESH_ID).wait()
                _to_sibling(ins[a], colwise[a], chip, c, x, y, c, pass_send.at[k], pass_recv.at[k]).start()

    operands = list(in_flight) + list(sems)
    in_specs = [HBM] * n + [SEMAPHORES] * 2
    if after is not None:
        operands.append(after)
        in_specs.append(ANY)
    outs = pl.pallas_call(
        body, name=name,
        in_specs=in_specs, out_specs=[SEMAPHORES] * 2 + [HBM] * n,
        out_shape=[pltpu.SemaphoreType.DMA((3 * n,)), pltpu.SemaphoreType.DMA((3 * n,))]
        + [pltpu.HBM(f.shape, f.dtype) for f in in_flight],
        input_output_aliases={i: 2 + i for i in range(n)},
        compiler_params=pltpu.CompilerParams(has_side_effects=pltpu.SideEffectType.DATAFLOW_SIDE_EFFECTING),
    )(*operands)
    return (outs[0], outs[1]), list(outs[2:])


def _gather_wait(in_flight, colwise, sems, after, name):
    n = len(in_flight)

    def body(*refs):
        ins = refs[:n]
        send_sems, recv_sems = refs[n], refs[n + 1]
        x, y, c, chips = _position()
        for a in range(n):
            for j, chip in enumerate(chips):
                k = a * 3 + j
                _to_sibling(ins[a], colwise[a], chip, c, x, y, c, send_sems.at[k], recv_sems.at[k]).wait_send()
                _to_sibling(ins[a], colwise[a], chip, 1 - c, x, y, c, send_sems.at[k], recv_sems.at[k]).wait_recv()

    operands = list(in_flight) + list(sems)
    in_specs = [HBM] * n + [SEMAPHORES] * 2
    if after is not None:
        operands.append(after)
        in_specs.append(ANY)
    outs = pl.pallas_call(
        body, name=name,
        in_specs=in_specs, out_specs=[HBM] * n,
        out_shape=[pltpu.HBM(f.shape, f.dtype) for f in in_flight],
        input_output_aliases={i: i for i in range(n)},
        compiler_params=pltpu.CompilerParams(has_side_effects=pltpu.SideEffectType.DATAFLOW_SIDE_EFFECTING),
    )(*operands)
    return list(outs)


def _exchange_halves(grads, colwise, name):
    n = len(grads)
    out_shapes = []
    for g, cw in zip(grads, colwise):
        L, K, N = g.shape
        out_shapes.append(jax.ShapeDtypeStruct((L, K // 2, N) if cw else (L, K, N // 2), g.dtype))

    def body(*refs):
        g_refs, out = refs[:n], refs[n:2 * n]
        send_sems, recv_sems = refs[2 * n:]
        x, y, c, _ = _position()
        copies = [pltpu.make_async_remote_copy(
            src_ref=_window(g_refs[i], colwise[i], None, 1 - c), dst_ref=out[i],
            send_sem=send_sems.at[i], recv_sem=recv_sems.at[i],
            device_id=(x, y, 1 - c), device_id_type=MESH_ID) for i in range(n)]
        for cp in copies:
            cp.start()
        for cp in copies:
            cp.wait()

    return pl.pallas_call(
        body, name=name,
        in_specs=[ANY] * n, out_specs=[ANY] * n, out_shape=out_shapes,
        scratch_shapes=[pltpu.SemaphoreType.DMA((n,)), pltpu.SemaphoreType.DMA((n,))],
    )(*grads)


def _scatter_copy(src_ref, land_ref, colwise, j, chip, c, send_sem, recv_sem):
    return pltpu.make_async_remote_copy(
        src_ref=_window(src_ref, colwise, 2 * chip[0] + chip[1], None), dst_ref=land_ref.at[j],
        send_sem=send_sem, recv_sem=recv_sem, device_id=(chip[0], chip[1], c), device_id_type=MESH_ID)


def _scatter_start(chip_sums, colwise, name):
    n = len(chip_sums)
    lands = []
    for g, cw in zip(chip_sums, colwise):
        L, hk, hn = g.shape
        lands.append(lax.empty((3, L, hk, hn // 4) if cw else (3, L, hk // 4, hn), g.dtype))

    def body(*refs):
        src, land = refs[:n], refs[n:2 * n]
        send_sems, recv_sems = refs[2 * n], refs[2 * n + 1]
        x, y, c, chips = _position()
        for i in range(n):
            for j, chip in enumerate(chips):
                _scatter_copy(src[i], land[i], colwise[i], j, chip, c, send_sems.at[i * 3 + j],
                              recv_sems.at[i * 3 + j]).start()

    arrays = list(chip_sums) + lands
    outs = pl.pallas_call(
        body, name=name,
        in_specs=[HBM] * (2 * n), out_specs=[SEMAPHORES] * 2 + [HBM] * (2 * n),
        out_shape=[pltpu.SemaphoreType.DMA((3 * n,)), pltpu.SemaphoreType.DMA((3 * n,))]
        + [pltpu.HBM(a.shape, a.dtype) for a in arrays],
        input_output_aliases={i: 2 + i for i in range(2 * n)},
        compiler_params=pltpu.CompilerParams(has_side_effects=pltpu.SideEffectType.DATAFLOW_SIDE_EFFECTING),
    )(*[pltpu.with_memory_space_constraint(a, pltpu.HBM) for a in arrays])
    return (outs[0], outs[1]), list(outs[2:2 + n]), list(outs[2 + n:])


def _scatter_wait(sources, lands, colwise, sems, after, name):
    n = len(sources)

    def body(*refs):
        src, land = refs[:n], refs[n:2 * n]
        send_sems, recv_sems = refs[2 * n], refs[2 * n + 1]
        x, y, c, chips = _position()
        for i in range(n):
            for j, chip in enumerate(chips):
                cp = _scatter_copy(src[i], land[i], colwise[i], j, chip, c, send_sems.at[i * 3 + j],
                                   recv_sems.at[i * 3 + j])
                cp.wait_send()
                cp.wait_recv()

    arrays = list(sources) + list(lands)
    operands = arrays + list(sems)
    in_specs = [HBM] * (2 * n) + [SEMAPHORES] * 2
    if after is not None:
        operands.append(after)
        in_specs.append(ANY)
    outs = pl.pallas_call(
        body, name=name,
        in_specs=in_specs, out_specs=[HBM] * (2 * n),
        out_shape=[pltpu.HBM(a.shape, a.dtype) for a in arrays],
        input_output_aliases={i: i for i in range(2 * n)},
        compiler_params=pltpu.CompilerParams(has_side_effects=pltpu.SideEffectType.DATAFLOW_SIDE_EFFECTING),
    )(*operands)
    return list(outs[:n]), list(outs[n:])


def _share_with_sibling(shards):
    n = len(shards)

    def body(*refs):
        out = refs[n:2 * n]
        send_sems, recv_sems = refs[2 * n:]
        x, y, c, _ = _position()

        def copy(i, half):
            win = _window(out[i], COL_SHARDED[i], None, half)
            return pltpu.make_async_remote_copy(
                src_ref=win, dst_ref=win, send_sem=send_sems.at[i], recv_sem=recv_sems.at[i],
                device_id=(x, y, 1 - c), device_id_type=MESH_ID)

        for i in range(n):
            copy(i, c).start()
        for i in range(n):
            copy(i, 1 - c).wait_recv()
        for i in range(n):
            copy(i, c).wait_send()

    return pl.pallas_call(
        body, name="grad_share_with_sibling",
        in_specs=[ANY] * n, out_specs=[ANY] * n,
        out_shape=[jax.ShapeDtypeStruct(s.shape, s.dtype) for s in shards],
        input_output_aliases={i: i for i in range(n)},
        scratch_shapes=[pltpu.SemaphoreType.DMA((n,)), pltpu.SemaphoreType.DMA((n,))],
    )(*shards)


def _all_reduce_small(pack, name):
    R, Cc = pack.shape
    n_dev = 8

    def body(p_ref, o_ref, buf, send_sems, recv_sems):
        x, y, c, _ = _position()
        me = 4 * x + 2 * y + c
        buf[me] = p_ref[...]

        def peer(k):
            px = 1 - x if k & 4 else x
            py = 1 - y if k & 2 else y
            pc = 1 - c if k & 1 else c
            return px, py, pc

        def copy(k, incoming):
            px, py, pc = peer(k)
            slot = (4 * px + 2 * py + pc) if incoming else me
            return pltpu.make_async_remote_copy(
                src_ref=p_ref, dst_ref=buf.at[slot], send_sem=send_sems.at[k], recv_sem=recv_sems.at[k],
                device_id=(px, py, pc), device_id_type=MESH_ID)

        for k in range(1, n_dev):
            copy(k, False).start()
        for k in range(1, n_dev):
            copy(k, True).wait_recv()
        for k in range(1, n_dev):
            copy(k, False).wait_send()
        acc = buf[0]
        for j in range(1, n_dev):
            acc = acc + buf[j]
        o_ref[...] = acc

    vmem = pl.BlockSpec(memory_space=pltpu.VMEM)
    return pl.pallas_call(
        body, name=name,
        in_specs=[vmem], out_specs=vmem, out_shape=jax.ShapeDtypeStruct((R, Cc), F32),
        scratch_shapes=[pltpu.VMEM((n_dev, R, Cc), F32), pltpu.SemaphoreType.DMA((n_dev,)),
                        pltpu.SemaphoreType.DMA((n_dev,))],
    )(pack)


def _local_forward_backward(x2, target2, S, pass_on, fetch, reduce, layers, final_g, tm=512):
    T, D = x2.shape
    C = D // 2
    n_heads = C // GROUP
    n_layers = len(layers)
    weights = {}
    saved = []
    xc = x2
    for li, lw in enumerate(layers):
        if li == 0:
            pass_on(0, None)
            weights.update(fetch(0, None))
        h1, qkv3, cv3 = _norm_proj(xc, lw["norm1"], weights[li, "w_in"], 0, ((3, C, F32), (3, C, BF16)), tm,
                                   min(C, 512), f"l{li}_norm_in_proj")
        if li == 0:
            pass_on(1, h1)
        o, lse, mix = _attn_fwd(qkv3, lw["attn_g"], 2, S, n_heads, f"l{li}_attn_fwd")
        if li == 0:
            weights.update(fetch(1, o))
            pass_on(2, o)
            pass_on(3, o)
        mix = _mix_conv_fwd(cv3, lw["taps"], lw["conv_g"], mix, S, f"l{li}_mix_conv_fwd")
        x_mid = _proj_residual(mix, weights[li, "w_out"], 0, xc, tm, f"l{li}_out_proj")
        if li == 0:
            weights.update(fetch(2, x_mid))
        Fd = weights[li, "ffn_up"].shape[2] // 2
        h2, up3 = _norm_proj(x_mid, lw["norm2"], weights[li, "ffn_up"], 0, ((2, Fd, BF16),), tm // 2, 256,
                             f"l{li}_norm_ffn_up")
        if li == 0:
            weights.update(fetch(3, up3))
        act = _ffn_act_fwd(up3, lw["ffn_taps"], S, f"l{li}_ffn_act_fwd")
        if li + 1 < n_layers:
            pass_on(li + 4, act)
        x_out = _proj_residual(act.reshape(1, T, Fd), weights[li, "ffn_down"], 0, x_mid, tm, f"l{li}_ffn_down")
        if li + 1 < n_layers:
            weights.update(fetch(li + 4, x_out))
        saved.append(dict(x_in=xc, h1=h1, qkv3=qkv3, cv3=cv3, o=o, lse=lse, mix=mix, x_mid=x_mid, h2=h2, up3=up3,
                          act=act))
        xc = x_out

    dx, d_final_g, loss_part = _final_norm_loss(xc, final_g, target2, tm, "final_norm_loss")

    small = [None] * n_layers
    started = None
    for li in reversed(range(n_layers)):
        lw, sv = layers[li], saved[li]
        w_in, w_out, ffn_up, ffn_down = (weights[li, n] for n in ("w_in", "w_out", "ffn_up", "ffn_down"))
        dxb, dact3 = _grad_through_weight(dx, ffn_down, 0, 1, Fd, BF16, tm, 256, f"l{li}_d_act", started)
        Fd = ffn_down.shape[1]
        d_ffn_down = _weight_grad(sv["act"].reshape(1, T, Fd), dxb.reshape(1, T, D), Fd // 2, D, 1024,
                                  0, 1, None, f"l{li}_d_ffn_down")
        dup3, d_taps_g, d_taps_v = _ffn_act_bwd(sv["up3"], dact3, lw["ffn_taps"], S, f"l{li}_ffn_act_bwd")
        d_ffn_up = _weight_grad(sv["h2"].reshape(1, T, D), dup3, D, Fd, 1024, 0, 1, None,
                                f"l{li}_d_ffn_up")
        started = None
        if li == 0:
            started = reduce({(li, "ffn_down"): d_ffn_down, (li, "ffn_up"): d_ffn_up})
        dx_mid, d_norm2 = _grad_through_proj_norm(dup3, ffn_up, 0, sv["x_mid"], lw["norm2"], dx, tm // 2,
                                                  f"l{li}_d_norm2")
        dxmb, dmix3 = _grad_through_weight(dx_mid, w_out, 0, 2, C, F32, tm, min(C, 512), f"l{li}_d_mix", started)
        d_w_out = _weight_grad(sv["mix"], dxmb.reshape(1, T, D), C, D, 1024, 0, 1, None, f"l{li}_d_w_out")
        dproj, d_attn_g = _attn_bwd(sv["qkv3"], sv["o"], sv["lse"], dmix3, lw["attn_g"], 6, S, n_heads,
                                    f"l{li}_attn_bwd")
        dproj, d_taps, d_conv_g = _mix_conv_bwd(sv["cv3"], dmix3, lw["taps"], lw["conv_g"], dproj, S,
                                                f"l{li}_mix_conv_bwd")
        d_w_in = _weight_grad(sv["h1"].reshape(1, T, D), dproj, D, C, 1024, 0, 1, None, f"l{li}_d_w_in")
        dx, d_norm1 = _grad_through_proj_norm(dproj, w_in, 0, sv["x_in"], lw["norm1"], dx_mid, tm,
                                              f"l{li}_d_norm1")
        if li == 0:
            reduce({(li, "w_out"): d_w_out, (li, "w_in"): d_w_in})
        else:
            started = reduce({(li, "ffn_down"): d_ffn_down, (li, "ffn_up"): d_ffn_up, (li, "w_out"): d_w_out,
                              (li, "w_in"): d_w_in})
        small[li] = dict(norm1=d_norm1, taps=d_taps, attn_g=d_attn_g, conv_g=d_conv_g, norm2=d_norm2,
                         ffn_taps=jnp.concatenate([d_taps_g, d_taps_v], axis=1))
    return loss_part, dx, small, d_final_g


SMALL_ORDER = ("norm1", "attn_g", "conv_g", "norm2", "taps", "ffn_taps")


def _pack_small(small, d_final_g):
    parts = [small[li][k].reshape(-1) for li in range(len(small)) for k in SMALL_ORDER] + [d_final_g.reshape(-1)]
    return jnp.concatenate(parts).reshape(-1, LANES)


def _unpack_small(pack, small, d_final_g):
    flat = pack.reshape(-1)
    out, pos = [dict() for _ in small], 0
    for li in range(len(small)):
        for k in SMALL_ORDER:
            n = small[li][k].size
            out[li][k] = flat[pos:pos + n].reshape(small[li][k].shape)
            pos += n
    return out, flat[pos:pos + d_final_g.size]


def kernel(x, norm1_g, w_in, mix_conv_w, attn_out_g, conv_out_g, w_out, norm2_g, ffn_up, ffn_conv_w, ffn_down, final_norm_g, loss_target, m_norm1_g, m_w_in, m_mix_conv_w, m_attn_out_g, m_conv_out_g, m_w_out, m_norm2_g, m_ffn_up, m_ffn_conv_w, m_ffn_down, m_final_norm_g, v_norm1_g, v_w_in, v_mix_conv_w, v_attn_out_g, v_conv_out_g, v_w_out, v_norm2_g, v_ffn_up, v_ffn_conv_w, v_ffn_down, v_final_norm_g):
    Bl, S, D = x.shape
    L = w_in.shape[0]
    T = Bl * S
    shard = 2 * lax.axis_index("x") + lax.axis_index("y")
    where = jnp.stack([shard, lax.axis_index("c")]).astype(jnp.int32)
    big_names = ("w_in", "w_out", "ffn_up", "ffn_down")

    taps_w, ftaps_w = mix_conv_w.shape[2], ffn_conv_w.shape[2]
    taps_full = jnp.zeros((L, 3, 4 * taps_w), F32)
    taps_full = lax.dynamic_update_slice(taps_full, mix_conv_w, (0, 0, shard * taps_w))
    ftaps_full = jnp.zeros((L, 3, 4 * ftaps_w), F32)
    ftaps_full = lax.dynamic_update_slice(ftaps_full, ffn_conv_w, (0, 0, shard * ftaps_w))
    tap_pack = jnp.concatenate([taps_full.reshape(-1), ftaps_full.reshape(-1)]).reshape(-1, LANES)
    tap_pack = _all_reduce_small(tap_pack * 0.5, "all_gather_taps")
    n_taps = taps_full.size
    taps_full = tap_pack.reshape(-1)[:n_taps].reshape(taps_full.shape)
    ftaps_full = tap_pack.reshape(-1)[n_taps:].reshape(ftaps_full.shape)

    big_shards = dict(zip(big_names, (w_in, w_out, ffn_up, ffn_down)))
    col_of = dict(zip(big_names, COL_SHARDED))
    groups = [[(0, n)] for n in big_names] + [[(l, n) for n in big_names] for l in range(1, L)]
    sems, in_flight = [], {}
    all_started = tap_pack
    for first, last in ((0, 1), (1, len(groups))):
        keys = [k for g in groups[first:last] for k in g]
        new_sems, arrays = _gather_start(
            [_cast_into_full(big_shards[n], l, col_of[n], where, f"cast_{n}_{l}") for l, n in keys],
            [col_of[n] for _, n in keys], [len(g) for g in groups[first:last]], all_started,
            f"gather_start_{first}")
        sems += new_sems
        in_flight.update(zip(keys, arrays))
        all_started = arrays[-1]

    def pass_on(g, after):
        after = all_started if g == 0 else after
        sems[g], arrays = _gather_pass([in_flight[k] for k in groups[g]], [col_of[n] for _, n in groups[g]],
                                       sems[g], after, f"gather_pass_{g}")
        in_flight.update(zip(groups[g], arrays))

    def fetch(g, after):
        done = _gather_wait([in_flight[k] for k in groups[g]], [col_of[n] for _, n in groups[g]], sems[g], after,
                            f"gather_wait_{g}")
        return dict(zip(groups[g], done))

    pending = []

    def reduce(grads):
        g = len(pending)
        keys = list(grads)
        cols = [col_of[n] for _, n in keys]
        others = _exchange_halves([grads[k] for k in keys], cols, f"grad_exchange_halves_{g}")
        chip_sums = [_chip_sum(grads[k], o, cw, where, f"chip_sum_{k[1]}_{k[0]}")
                     for k, o, cw in zip(keys, others, cols)]
        pending.append((keys, cols) + _scatter_start(chip_sums, cols, f"scatter_start_{g}"))
        return pending[-1][3][0]

    layers = [dict(norm1=norm1_g[l:l + 1], taps=taps_full[l], attn_g=attn_out_g[l:l + 1],
                   conv_g=conv_out_g[l:l + 1], norm2=norm2_g[l:l + 1], ffn_taps=ftaps_full[l]) for l in range(L)]

    loss_part, dx, small, d_final_g = _local_forward_backward(
        x.reshape(T, D), loss_target.reshape(T, D), S, pass_on, fetch, reduce, layers, final_norm_g.reshape(1, D))
    loss = lax.psum(loss_part[0, 0], ("x", "y", "c"))

    reduced = dict.fromkeys(big_names)
    last_started = pending[-1][3][0]
    for g, (keys, cols, rs_sems, sources, lands) in enumerate(pending):
        after = last_started if g + 1 < len(pending) else None
        sources, lands = _scatter_wait(sources, lands, cols, rs_sems, after, f"scatter_wait_{g}")
        for (l, n), cw, src, land in zip(keys, cols, sources, lands):
            reduced[n] = _owner_sum(src, land, cw, where, l, L, reduced[n], f"owner_sum_{n}_{l}")
    g_big = _share_with_sibling([reduced[n] for n in big_names])

    pack = _all_reduce_small(_pack_small(small, d_final_g), "all_reduce_small_grads")
    g_small, g_final = _unpack_small(pack, small, d_final_g)

    def stacked(key):
        return jnp.stack([g_small[l][key].reshape(g_small[l][key].shape[-2:] if key.endswith("taps") else (-1,))
                          for l in range(L)])

    g_norm1, g_attn, g_conv, g_norm2 = stacked("norm1"), stacked("attn_g"), stacked("conv_g"), stacked("norm2")
    g_taps = lax.dynamic_slice(stacked("taps"), (0, 0, shard * taps_w), (L, 3, taps_w))
    g_ftaps = lax.dynamic_slice(stacked("ffn_taps"), (0, 0, shard * ftaps_w), (L, 3, ftaps_w))

    grads_out = dict(norm1_g=g_norm1, w_in=g_big[0], mix_conv_w=g_taps, attn_out_g=g_attn, conv_out_g=g_conv,
                     w_out=g_big[1], norm2_g=g_norm2, ffn_up=g_big[2], ffn_conv_w=g_ftaps, ffn_down=g_big[3],
                     final_norm_g=g_final)
    weights = dict(norm1_g=norm1_g, w_in=w_in, mix_conv_w=mix_conv_w, attn_out_g=attn_out_g, conv_out_g=conv_out_g,
                   w_out=w_out, norm2_g=norm2_g, ffn_up=ffn_up, ffn_conv_w=ffn_conv_w, ffn_down=ffn_down,
                   final_norm_g=final_norm_g)
    ms = dict(norm1_g=m_norm1_g, w_in=m_w_in, mix_conv_w=m_mix_conv_w, attn_out_g=m_attn_out_g,
              conv_out_g=m_conv_out_g, w_out=m_w_out, norm2_g=m_norm2_g, ffn_up=m_ffn_up, ffn_conv_w=m_ffn_conv_w,
              ffn_down=m_ffn_down, final_norm_g=m_final_norm_g)
    vs = dict(norm1_g=v_norm1_g, w_in=v_w_in, mix_conv_w=v_mix_conv_w, attn_out_g=v_attn_out_g,
              conv_out_g=v_conv_out_g, w_out=v_w_out, norm2_g=v_norm2_g, ffn_up=v_ffn_up, ffn_conv_w=v_ffn_conv_w,
              ffn_down=v_ffn_down, final_norm_g=v_final_norm_g)
    names = list(weights)
    small_names = [n for n in names if n not in big_names]
    delta, new_m, new_v = {}, {}, {}
    for n in big_names:
        shp = weights[n].shape
        two_d = (shp[0] * shp[1], shp[2])
        d_, m_, v_ = _adamw(weights[n].reshape(two_d), grads_out[n].reshape(two_d), ms[n].reshape(two_d),
                            vs[n].reshape(two_d), f"adamw_{n}")
        delta[n], new_m[n], new_v[n] = d_.reshape(shp), m_.reshape(shp), v_.reshape(shp)

    def packed(tree):
        return jnp.concatenate([tree[n].reshape(-1) for n in small_names]).reshape(-1, LANES)

    d_, m_, v_ = _adamw(packed(weights), packed(grads_out), packed(ms), packed(vs), "adamw_small")
    pos = 0
    for n in small_names:
        size, shp = weights[n].size, weights[n].shape
        delta[n] = d_.reshape(-1)[pos:pos + size].reshape(shp)
        new_m[n] = m_.reshape(-1)[pos:pos + size].reshape(shp)
        new_v[n] = v_.reshape(-1)[pos:pos + size].reshape(shp)
        pos += size

    return (loss, dx.reshape(Bl, S, D), *[grads_out[n] for n in names], *[delta[n] for n in names],
            *[new_m[n] for n in names], *[new_v[n] for n in names])
```

```python
import functools
import math

import jax
import jax.numpy as jnp
from jax import lax
from jax.experimental import pallas as pl
from jax.experimental.pallas import tpu as pltpu

F32 = jnp.float32
BF16 = jnp.bfloat16
EPS = 1e-6
GROUP = 64
LANES = 128
BAND = 128
DILATIONS = (1, 4, 16)
NEG = -1e30
MIB = 1024 * 1024
MESH_ID = pl.DeviceIdType.MESH

ADAM_LR = 0.001
ADAM_B1 = 0.9
ADAM_B2 = 0.999
ADAM_EPS = 1e-08
ADAM_WD = 0.01
ADAM_STEP = 10


ANY = pl.BlockSpec(memory_space=pl.ANY)


def _params(sem=None, vmem_mb=48):
    return pltpu.CompilerParams(dimension_semantics=sem, vmem_limit_bytes=vmem_mb * MIB)


def _nt(a, b):
    return lax.dot_general(a, b, (((1,), (1,)), ((), ())), preferred_element_type=F32)


def _tn(a, b):
    return lax.dot_general(a, b, (((0,), (0,)), ((), ())), preferred_element_type=F32)


def _seg_sum(x, is_a):
    s_a = jnp.sum(jnp.where(is_a, x, 0.0), axis=-1, keepdims=True)
    s_b = jnp.sum(jnp.where(is_a, 0.0, x), axis=-1, keepdims=True)
    return jnp.where(is_a, s_a, s_b)


def _lane_is_a():
    return lax.broadcasted_iota(jnp.int32, (1, LANES), 1) < GROUP


def _norm_proj(x, g, w3, layer, groups, tm, chunk, name):
    T, D = x.shape
    N = w3.shape[2]
    assert sum(p * c for p, c, _ in groups) == N and T % tm == 0

    def body(x_ref, g_ref, w_ref, h_ref, *out_refs):
        xv = x_ref[...]
        rstd = lax.rsqrt(jnp.mean(xv * xv, axis=-1, keepdims=True) + EPS)
        h = ((xv * rstd) * g_ref[...]).astype(BF16)
        h_ref[...] = h
        col = 0
        for (pieces, width, dtype), o_ref in zip(groups, out_refs):
            for p in range(pieces):
                for c0 in range(0, width, chunk):
                    acc = jnp.dot(h, w_ref[:, col + c0:col + c0 + chunk], preferred_element_type=F32)
                    o_ref[p, :, c0:c0 + chunk] = acc.astype(dtype)
                col += width

    out_shape = [jax.ShapeDtypeStruct((T, D), BF16)]
    out_specs = [pl.BlockSpec((tm, D), lambda i: (i, 0))]
    for pieces, width, dtype in groups:
        assert width % chunk == 0
        out_shape.append(jax.ShapeDtypeStruct((pieces, T, width), dtype))
        out_specs.append(pl.BlockSpec((pieces, tm, width), lambda i: (0, i, 0)))
    return pl.pallas_call(
        body, grid=(T // tm,), name=name,
        in_specs=[pl.BlockSpec((tm, D), lambda i: (i, 0)),
                  pl.BlockSpec((1, D), lambda i: (0, 0)),
                  pl.BlockSpec((None, D, N), lambda i: (layer, 0, 0))],
        out_specs=out_specs, out_shape=out_shape,
        compiler_params=_params(("parallel",), 56),
    )(x, g, w3)


def _proj_residual(pieces3, w3, layer, x, tm, name):
    P, T, C = pieces3.shape
    D = w3.shape[2]

    def body(a_ref, w_ref, x_ref, o_ref):
        acc = x_ref[...]
        for p in range(P):
            acc = acc + jnp.dot(a_ref[p], w_ref[p * C:(p + 1) * C, :], preferred_element_type=F32)
        o_ref[...] = acc

    return pl.pallas_call(
        body, grid=(T // tm,), name=name,
        in_specs=[pl.BlockSpec((P, tm, C), lambda i: (0, i, 0)),
                  pl.BlockSpec((None, P * C, D), lambda i: (layer, 0, 0)),
                  pl.BlockSpec((tm, D), lambda i: (i, 0))],
        out_specs=pl.BlockSpec((tm, D), lambda i: (i, 0)),
        out_shape=jax.ShapeDtypeStruct((T, D), F32),
        compiler_params=_params(("parallel",)),
    )(pieces3, w3, x)


def _grad_through_weight(dy, w3, layer, pieces, width, out_dtype, tm, chunk, name, after=None):
    T, D = dy.shape

    def body(dy_ref, w_ref, *rest):
        dyb_ref, o_ref = rest[-2:]
        dyb = dy_ref[...].astype(BF16)
        dyb_ref[...] = dyb
        for p in range(pieces):
            for c0 in range(0, width, chunk):
                r0 = p * width + c0
                o_ref[p, :, c0:c0 + chunk] = _nt(dyb, w_ref[r0:r0 + chunk, :]).astype(out_dtype)

    in_specs = [pl.BlockSpec((tm, D), lambda i: (i, 0)),
                pl.BlockSpec((None, pieces * width, D), lambda i: (layer, 0, 0))]
    operands = [dy, w3]
    if after is not None:
        in_specs.append(ANY)
        operands.append(after)
    return pl.pallas_call(
        body, grid=(T // tm,), name=name,
        in_specs=in_specs,
        out_specs=[pl.BlockSpec((tm, D), lambda i: (i, 0)),
                   pl.BlockSpec((pieces, tm, width), lambda i: (0, i, 0))],
        out_shape=[jax.ShapeDtypeStruct((T, D), BF16),
                   jax.ShapeDtypeStruct((pieces, T, width), out_dtype)],
        compiler_params=_params(("parallel",)),
    )(*operands)


def _grad_through_proj_norm(dp3, w3, layer, x, g, dx_in, tm, name):
    P, T, C = dp3.shape
    D = w3.shape[1]

    def body(dp_ref, w_ref, x_ref, g_ref, dxin_ref, dx_ref, dg_ref):
        dh = _nt(dp_ref[0], w_ref[:, 0:C])
        for p in range(1, P):
            dh = dh + _nt(dp_ref[p], w_ref[:, p * C:(p + 1) * C])
        xv = x_ref[...]
        rstd = lax.rsqrt(jnp.mean(xv * xv, axis=-1, keepdims=True) + EPS)
        xn = xv * rstd
        a = dh * g_ref[...]
        dx_ref[...] = dxin_ref[...] + rstd * (a - xn * jnp.mean(a * xn, axis=-1, keepdims=True))
        part = jnp.sum(dh * xn, axis=0, keepdims=True)

        @pl.when(pl.program_id(0) == 0)
        def _():
            dg_ref[...] = part

        @pl.when(pl.program_id(0) != 0)
        def _():
            dg_ref[...] += part

    return pl.pallas_call(
        body, grid=(T // tm,), name=name,
        in_specs=[pl.BlockSpec((P, tm, C), lambda i: (0, i, 0)),
                  pl.BlockSpec((None, D, P * C), lambda i: (layer, 0, 0)),
                  pl.BlockSpec((tm, D), lambda i: (i, 0)),
                  pl.BlockSpec((1, D), lambda i: (0, 0)),
                  pl.BlockSpec((tm, D), lambda i: (i, 0))],
        out_specs=[pl.BlockSpec((tm, D), lambda i: (i, 0)),
                   pl.BlockSpec((1, D), lambda i: (0, 0))],
        out_shape=[jax.ShapeDtypeStruct((T, D), F32), jax.ShapeDtypeStruct((1, D), F32)],
        compiler_params=_params(("arbitrary",), 56),
    )(dp3, w3, x, g, dx_in)


def _weight_grad(a3, g3, ta, tg, tt, layer, n_layers, prev, name):
    PA, T, CA = a3.shape
    PG, _, CG = g3.shape
    na, ng, nt = CA // ta, CG // tg, T // tt
    assert CA % ta == 0 and CG % tg == 0 and T % tt == 0

    def body(a_ref, g_ref, *rest):
        o_ref, acc_ref = rest[-2:]
        t = pl.program_id(2)
        part = _tn(a_ref[...], g_ref[...])

        @pl.when(t == 0)
        def _():
            acc_ref[...] = part

        @pl.when(t != 0)
        def _():
            acc_ref[...] += part

        @pl.when(t == nt - 1)
        def _():
            o_ref[...] = acc_ref[...].astype(o_ref.dtype)

    in_specs = [pl.BlockSpec((None, tt, ta), lambda i, j, t: (i // na, t, i % na)),
                pl.BlockSpec((None, tt, tg), lambda i, j, t: (j // ng, t, j % ng))]
    operands = [a3, g3]
    if prev is not None:
        in_specs.append(pl.BlockSpec(memory_space=pl.ANY))
        operands.append(prev)
    return pl.pallas_call(
        body, grid=(PA * na, PG * ng, nt), name=name,
        in_specs=in_specs,
        out_specs=pl.BlockSpec((None, ta, tg), lambda i, j, t: (layer, i, j)),
        out_shape=jax.ShapeDtypeStruct((n_layers, PA * CA, PG * CG), BF16),
        scratch_shapes=[pltpu.VMEM((ta, tg), F32)],
        input_output_aliases={} if prev is None else {2: 0},
        compiler_params=_params(("parallel", "parallel", "arbitrary"), 56),
    )(*operands)


def _bias_tables(bm_ref, pair, n_heads):
    ii = lax.broadcasted_iota(jnp.int32, (BAND, 2 * BAND), 0)
    jj = lax.broadcasted_iota(jnp.int32, (BAND, 2 * BAND), 1)
    dist = BAND + ii - jj
    valid = (dist >= 0) & (dist <= BAND)
    distf = dist.astype(F32)
    for hh in range(2):
        head = (2 * pair + hh + 1).astype(F32)
        slope = jnp.exp(jnp.full((1, 1), -8.0 / n_heads * math.log(2.0), F32) * head)
        for bi, d in enumerate(DILATIONS):
            bm_ref[bi, hh * BAND:(hh + 1) * BAND, :] = jnp.where(valid, -(slope * d) * distf, NEG)


def _stack_heads(x, is_a):
    zero = jnp.zeros_like(x)
    return jnp.concatenate([jnp.where(is_a, x, zero), jnp.where(is_a, zero, x)], axis=0)


def _unstack_heads(x2, is_a):
    return jnp.where(is_a, x2[0:BAND], x2[BAND:2 * BAND])


def _gather_residues(dst_ref, src, d, S, convert):
    L = S // d
    for r in range(d):
        rows = pl.ds(r, L, stride=d) if d > 1 else slice(None)
        dst_ref[r * L:(r + 1) * L, :] = convert(src(rows))


def _block_rows(t, d, S):
    nb = S // (BAND * d)
    n = t % nb
    has_prev = jnp.minimum(n, 1)
    cur = pl.ds(pl.multiple_of(t * BAND, BAND), BAND)
    prev = pl.ds(pl.multiple_of((t - has_prev) * BAND, BAND), BAND)
    return cur, prev, has_prev


def _first_block_penalty(has_prev):
    jrow = lax.broadcasted_iota(jnp.int32, (1, 2 * BAND), 1)
    pen = jnp.where(has_prev == 0, NEG, 0.0).astype(F32)
    return jnp.where(jrow < BAND, pen, 0.0)


def _attn_fwd(qkv3, gain, mix_shape_pieces, S, n_heads, name):
    _, T, C = qkv3.shape
    B, P = T // S, C // LANES
    NBLK = S // BAND
    scale = GROUP ** -0.5
    nbr = len(DILATIONS)
    RC = 256

    def body(qkv_ref, g_ref, o_ref, lse_ref, an_ref, qs, ks, vs, op, mp, lp, ob, mb, lb, bm):
        pair = pl.program_id(1)
        is_a = _lane_is_a()
        _bias_tables(bm, pair, n_heads)

        for bi, d in enumerate(DILATIONS):
            nb = S // (BAND * d)
            _gather_residues(qs, lambda rows: qkv_ref.at[0][rows, :], d, S, lambda v: (v * scale).astype(BF16))
            _gather_residues(ks, lambda rows: qkv_ref.at[1][rows, :], d, S, lambda v: v.astype(BF16))
            _gather_residues(vs, lambda rows: qkv_ref.at[2][rows, :], d, S, lambda v: v.astype(BF16))
            o_dst, m_dst, l_dst = (ob.at[bi], mb.at[bi], lb.at[bi]) if d == 1 else (op, mp, lp)

            def block(t, carry, bi=bi, d=d, nb=nb, o_dst=o_dst, m_dst=m_dst, l_dst=l_dst):
                cur, prev, has_prev = _block_rows(t, d, S)
                q2 = _stack_heads(qs[cur, :], is_a)
                kc = jnp.concatenate([ks[prev, :], ks[cur, :]], axis=0)
                vc = jnp.concatenate([vs[prev, :], vs[cur, :]], axis=0)
                s = _nt(q2, kc) + bm[bi] + _first_block_penalty(has_prev)
                m = jnp.max(s, axis=-1, keepdims=True)
                e = jnp.exp(s - m)
                l = jnp.sum(e, axis=-1, keepdims=True)
                pv = jnp.dot(e.astype(BF16), vc, preferred_element_type=F32)
                o_dst[cur, :] = _unstack_heads(pv, is_a)
                m_dst[cur, :] = _unstack_heads(m, is_a)
                l_dst[cur, :] = _unstack_heads(l, is_a)
                return carry

            lax.fori_loop(0, NBLK, block, 0, unroll=8)
            if d > 1:
                L = S // d
                for r in range(d):
                    rows = pl.ds(r, L, stride=d)
                    ob.at[bi][rows, :] = op[r * L:(r + 1) * L, :]
                    mb.at[bi][rows, :] = mp[r * L:(r + 1) * L, :]
                    lb.at[bi][rows, :] = lp[r * L:(r + 1) * L, :]

        def finish(ci, carry):
            rs = pl.ds(pl.multiple_of(ci * RC, RC), RC)
            ms = [mb[bi, rs, :] for bi in range(nbr)]
            mmax = functools.reduce(jnp.maximum, ms)
            ws = [jnp.exp(m - mmax) for m in ms]
            num = sum(ob[bi, rs, :] * ws[bi] for bi in range(nbr))
            den = sum(lb[bi, rs, :] * ws[bi] for bi in range(nbr))
            o = num / den
            o_ref[rs, :] = o
            lse_ref[rs, :] = mmax + jnp.log(den)
            rstd = lax.rsqrt(_seg_sum(o * o, is_a) * (1.0 / GROUP) + EPS)
            an_ref[rs, :] = ((o * rstd) * g_ref[...]).astype(BF16)
            return carry

        lax.fori_loop(0, S // RC, finish, 0)

    seq = pl.BlockSpec((S, LANES), lambda b, p: (b, p))
    return pl.pallas_call(
        body, grid=(B, P), name=name,
        in_specs=[pl.BlockSpec((3, S, LANES), lambda b, p: (0, b, p)),
                  pl.BlockSpec((1, LANES), lambda b, p: (0, p))],
        out_specs=[seq, seq, pl.BlockSpec((None, S, LANES), lambda b, p: (0, b, p))],
        out_shape=[jax.ShapeDtypeStruct((T, C), F32), jax.ShapeDtypeStruct((T, C), F32),
                   jax.ShapeDtypeStruct((mix_shape_pieces, T, C), BF16)],
        scratch_shapes=[pltpu.VMEM((S, LANES), BF16)] * 3 + [pltpu.VMEM((S, LANES), F32)] * 3
        + [pltpu.VMEM((nbr, S, LANES), F32)] * 3 + [pltpu.VMEM((nbr, 2 * BAND, 2 * BAND), F32)],
        compiler_params=_params(("parallel", "parallel")),
    )(qkv3, gain)


def _attn_bwd(qkv3, o, lse, dmix3, gain, dproj_pieces, S, n_heads, name):
    _, T, C = qkv3.shape
    B, P = T // S, C // LANES
    NBLK = S // BAND
    scale = GROUP ** -0.5
    nbr = len(DILATIONS)
    RC = 256

    def body(qkv_ref, o_ref, lse_ref, dn_ref, g_ref, dqkv_ref, dg_ref,
             do_n, dd_n, qs, ks, vs, dos, lses, dds, dqp, dkp, dvp, dqn, dkn, dvn, bm):
        pair = pl.program_id(0)
        b = pl.program_id(1)
        is_a = _lane_is_a()
        _bias_tables(bm, pair, n_heads)

        def prologue(ci, dg_acc):
            rs = pl.ds(pl.multiple_of(ci * RC, RC), RC)
            ov = o_ref[rs, :]
            dn = dn_ref[rs, :]
            rstd = lax.rsqrt(_seg_sum(ov * ov, is_a) * (1.0 / GROUP) + EPS)
            on = ov * rstd
            a = dn * g_ref[...]
            do = rstd * (a - on * (_seg_sum(a * on, is_a) * (1.0 / GROUP)))
            do_n[rs, :] = do
            dd_n[rs, :] = _seg_sum(do * ov, is_a)
            zero = jnp.zeros((RC, LANES), F32)
            dqn[rs, :] = zero
            dkn[rs, :] = zero
            dvn[rs, :] = zero
            return dg_acc + jnp.sum(dn * on, axis=0, keepdims=True)

        dg_part = lax.fori_loop(0, S // RC, prologue, jnp.zeros((1, LANES), F32))

        @pl.when(b == 0)
        def _():
            dg_ref[...] = dg_part

        @pl.when(b != 0)
        def _():
            dg_ref[...] += dg_part

        for bi, d in enumerate(DILATIONS):
            nb = S // (BAND * d)
            L = S // d
            _gather_residues(qs, lambda rows: qkv_ref.at[0][rows, :], d, S, lambda v: (v * scale).astype(BF16))
            _gather_residues(ks, lambda rows: qkv_ref.at[1][rows, :], d, S, lambda v: v.astype(BF16))
            _gather_residues(vs, lambda rows: qkv_ref.at[2][rows, :], d, S, lambda v: v.astype(BF16))
            _gather_residues(dos, lambda rows: do_n[rows, :], d, S, lambda v: v.astype(BF16))
            if d == 1:
                lse_src, dd_src, dq_dst, dk_dst, dv_dst = lse_ref, dd_n, dqn, dkn, dvn
            else:
                _gather_residues(lses, lambda rows: lse_ref[rows, :], d, S, lambda v: v)
                _gather_residues(dds, lambda rows: dd_n[rows, :], d, S, lambda v: v)
                dkp[...] = jnp.zeros((S, LANES), F32)
                dvp[...] = jnp.zeros((S, LANES), F32)
                lse_src, dd_src, dq_dst, dk_dst, dv_dst = lses, dds, dqp, dkp, dvp

            def block(t, carry, bi=bi, d=d, lse_src=lse_src, dd_src=dd_src, dq_dst=dq_dst, dk_dst=dk_dst,
                      dv_dst=dv_dst):
                cur, prev, has_prev = _block_rows(t, d, S)
                q2 = _stack_heads(qs[cur, :], is_a)
                do2 = _stack_heads(dos[cur, :], is_a)
                lse_t = lse_src[cur, :]
                dd_t = dd_src[cur, :]
                lse2 = jnp.concatenate([lse_t[:, 0:1], lse_t[:, GROUP:GROUP + 1]], axis=0)
                dd2 = jnp.concatenate([dd_t[:, 0:1], dd_t[:, GROUP:GROUP + 1]], axis=0)
                kc = jnp.concatenate([ks[prev, :], ks[cur, :]], axis=0)
                vc = jnp.concatenate([vs[prev, :], vs[cur, :]], axis=0)
                s = _nt(q2, kc) + bm[bi] + _first_block_penalty(has_prev)
                p = jnp.exp(s - lse2)
                ds = (p * (_nt(do2, vc) - dd2)).astype(BF16)
                dq = _unstack_heads(jnp.dot(ds, kc, preferred_element_type=F32), is_a)
                dk = _tn(ds, q2)
                dv = _tn(p.astype(BF16), do2)
                dq_dst[cur, :] = dq
                dk_dst[prev, :] += dk[0:BAND, :]
                dv_dst[prev, :] += dv[0:BAND, :]
                dk_dst[cur, :] += dk[BAND:2 * BAND, :]
                dv_dst[cur, :] += dv[BAND:2 * BAND, :]
                return carry

            lax.fori_loop(0, NBLK, block, 0, unroll=8)
            if d > 1:
                for r in range(d):
                    rows = pl.ds(r, L, stride=d)
                    dqn[rows, :] += dqp[r * L:(r + 1) * L, :]
                    dkn[rows, :] += dkp[r * L:(r + 1) * L, :]
                    dvn[rows, :] += dvp[r * L:(r + 1) * L, :]

        dqkv_ref[0] = (dqn[...] * scale).astype(BF16)
        dqkv_ref[1] = dkn[...].astype(BF16)
        dqkv_ref[2] = dvn[...].astype(BF16)

    seq = pl.BlockSpec((S, LANES), lambda p, b: (b, p))
    f32_seq = pltpu.VMEM((S, LANES), F32)
    bf_seq = pltpu.VMEM((S, LANES), BF16)
    return pl.pallas_call(
        body, grid=(P, B), name=name,
        in_specs=[pl.BlockSpec((3, S, LANES), lambda p, b: (0, b, p)), seq, seq,
                  pl.BlockSpec((None, S, LANES), lambda p, b: (0, b, p)),
                  pl.BlockSpec((1, LANES), lambda p, b: (0, p))],
        out_specs=[pl.BlockSpec((3, S, LANES), lambda p, b: (0, b, p)),
                   pl.BlockSpec((1, LANES), lambda p, b: (0, p))],
        out_shape=[jax.ShapeDtypeStruct((dproj_pieces, T, C), BF16), jax.ShapeDtypeStruct((1, C), F32)],
        scratch_shapes=[f32_seq, f32_seq, bf_seq, bf_seq, bf_seq, bf_seq, f32_seq, f32_seq,
                        f32_seq, f32_seq, f32_seq, f32_seq, f32_seq, f32_seq,
                        pltpu.VMEM((nbr, 2 * BAND, 2 * BAND), F32)],
        compiler_params=_params(("parallel", "arbitrary")),
    )(qkv3, o, lse, dmix3, gain)


def _delay(x, k, row):
    return jnp.where(row >= k, pltpu.roll(x, k, 0), 0.0)


def _advance(x, k, row, S):
    return jnp.where(row < S - k, pltpu.roll(x, S - k, 0), 0.0)


def _conv3(x, w, row):
    return (w[0:1, :] * _delay(x, 2, row) + w[1:2, :] * _delay(x, 1, row)) + w[2:3, :] * x


HALO = 8


CONV_ROWS = 128
FFN_LANES = 256


def _zero_halo(pad_ref, S):
    zeros = jnp.zeros((HALO, pad_ref.shape[1]), pad_ref.dtype)
    pad_ref[0:HALO, :] = zeros
    pad_ref[HALO + S:2 * HALO + S, :] = zeros


def _window_at(pad_ref, r0, shift):
    return pad_ref[HALO + r0 + shift:HALO + r0 + shift + CONV_ROWS, :]


def _conv3_at(pad_ref, w, r0):
    return ((w[0:1, :] * _window_at(pad_ref, r0, -2) + w[1:2, :] * _window_at(pad_ref, r0, -1))
            + w[2:3, :] * _window_at(pad_ref, r0, 0))


def _conv3_grads_at(dz_ref, x_ref, w, r0):
    dz, dz1, dz2 = (_window_at(dz_ref, r0, k) for k in range(3))
    x = _window_at(x_ref, r0, 0)
    dx = (w[2:3, :] * dz + w[1:2, :] * dz1) + w[0:1, :] * dz2
    parts = [jnp.sum((d * x).reshape(CONV_ROWS // 8, 8, x.shape[1]), axis=0) for d in (dz2, dz1, dz)]
    return dx, parts


def _conv3_grads(dz, x, w, row, S):
    dz1 = _advance(dz, 1, row, S)
    dz2 = _advance(dz, 2, row, S)
    dx = (w[2:3, :] * dz + w[1:2, :] * dz1) + w[0:1, :] * dz2
    dw = jnp.concatenate([jnp.sum(dz2 * x, axis=0, keepdims=True),
                          jnp.sum(dz1 * x, axis=0, keepdims=True),
                          jnp.sum(dz * x, axis=0, keepdims=True)], axis=0)
    return dx, dw


def _mix_conv_fwd(cv3, taps, gain, mix, S, name):
    _, T, C = cv3.shape
    B, P = T // S, C // LANES

    def body(cv_ref, w_ref, g_ref, mix_hbm, y_ref):
        del mix_hbm
        row = lax.broadcasted_iota(jnp.int32, (S, 1), 0)
        is_a = _lane_is_a()
        gb = cv_ref[0].astype(F32)
        c = cv_ref[1].astype(F32) * cv_ref[2].astype(F32)
        y = gb * _conv3(c, w_ref[...], row)
        rstd = lax.rsqrt(_seg_sum(y * y, is_a) * (1.0 / GROUP) + EPS)
        y_ref[...] = ((y * rstd) * g_ref[...]).astype(BF16)

    return pl.pallas_call(
        body, grid=(B, P), name=name,
        in_specs=[pl.BlockSpec((3, S, LANES), lambda b, p: (0, b, p)),
                  pl.BlockSpec((3, LANES), lambda b, p: (0, p)),
                  pl.BlockSpec((1, LANES), lambda b, p: (0, p)),
                  pl.BlockSpec(memory_space=pl.ANY)],
        out_specs=pl.BlockSpec((None, S, LANES), lambda b, p: (1, b, p)),
        out_shape=jax.ShapeDtypeStruct(mix.shape, mix.dtype),
        input_output_aliases={3: 0},
        compiler_params=_params(("parallel", "parallel")),
    )(cv3, taps, gain, mix)


def _mix_conv_bwd(cv3, dmix3, taps, gain, dproj, S, name):
    _, T, C = cv3.shape
    B, P = T // S, C // LANES

    def body(cv_ref, dn_ref, w_ref, g_ref, dproj_hbm, dcv_ref, dw_ref, dg_ref):
        del dproj_hbm
        b = pl.program_id(1)
        row = lax.broadcasted_iota(jnp.int32, (S, 1), 0)
        is_a = _lane_is_a()
        w = w_ref[...]
        gb = cv_ref[0].astype(F32)
        gc = cv_ref[1].astype(F32)
        u = cv_ref[2].astype(F32)
        c = gc * u
        z = _conv3(c, w, row)
        y = gb * z
        rstd = lax.rsqrt(_seg_sum(y * y, is_a) * (1.0 / GROUP) + EPS)
        yn = y * rstd
        dn = dn_ref[...]
        a = dn * g_ref[...]
        dy = rstd * (a - yn * (_seg_sum(a * yn, is_a) * (1.0 / GROUP)))
        dg = jnp.sum(dn * yn, axis=0, keepdims=True)
        dc, dw = _conv3_grads(dy * gb, c, w, row, S)
        dcv_ref[0] = (dy * z).astype(BF16)
        dcv_ref[1] = (dc * u).astype(BF16)
        dcv_ref[2] = (dc * gc).astype(BF16)

        @pl.when(b == 0)
        def _():
            dw_ref[...] = dw
            dg_ref[...] = dg

        @pl.when(b != 0)
        def _():
            dw_ref[...] += dw
            dg_ref[...] += dg

    return pl.pallas_call(
        body, grid=(P, B), name=name,
        in_specs=[pl.BlockSpec((3, S, LANES), lambda p, b: (0, b, p)),
                  pl.BlockSpec((None, S, LANES), lambda p, b: (1, b, p)),
                  pl.BlockSpec((3, LANES), lambda p, b: (0, p)),
                  pl.BlockSpec((1, LANES), lambda p, b: (0, p)),
                  pl.BlockSpec(memory_space=pl.ANY)],
        out_specs=[pl.BlockSpec((3, S, LANES), lambda p, b: (1, b, p)),
                   pl.BlockSpec((3, LANES), lambda p, b: (0, p)),
                   pl.BlockSpec((1, LANES), lambda p, b: (0, p))],
        out_shape=[jax.ShapeDtypeStruct(dproj.shape, dproj.dtype),
                   jax.ShapeDtypeStruct((3, C), F32), jax.ShapeDtypeStruct((1, C), F32)],
        input_output_aliases={4: 0},
        compiler_params=_params(("parallel", "arbitrary")),
    )(cv3, dmix3, taps, gain, dproj)


def _sigmoid(x):
    return 0.5 * jnp.tanh(0.5 * x) + 0.5


def _ffn_act_fwd(up3, taps, S, name):
    _, T, Fd = up3.shape
    W = FFN_LANES
    B, P = T // S, Fd // W

    def body(up_ref, wg_ref, wv_ref, act_ref, pad_g, pad_v):
        _zero_halo(pad_g, S)
        _zero_halo(pad_v, S)
        pad_g[HALO:HALO + S, :] = up_ref[0].astype(F32)
        pad_v[HALO:HALO + S, :] = up_ref[1].astype(F32)
        wg = wg_ref[...]
        wv = wv_ref[...]
        for r0 in range(0, S, CONV_ROWS):
            cg = _conv3_at(pad_g, wg, r0)
            cv = _conv3_at(pad_v, wv, r0)
            act_ref[r0:r0 + CONV_ROWS, :] = ((cg * _sigmoid(cg)) * cv).astype(BF16)

    return pl.pallas_call(
        body, grid=(B, P), name=name,
        in_specs=[pl.BlockSpec((2, S, W), lambda b, p: (0, b, p)),
                  pl.BlockSpec((3, W), lambda b, p: (0, p)),
                  pl.BlockSpec((3, W), lambda b, p: (0, P + p))],
        out_specs=pl.BlockSpec((S, W), lambda b, p: (b, p)),
        out_shape=jax.ShapeDtypeStruct((T, Fd), BF16),
        scratch_shapes=[pltpu.VMEM((S + 2 * HALO, W), F32)] * 2,
        compiler_params=_params(("parallel", "parallel")),
    )(up3, taps, taps)


def _ffn_act_bwd(up3, dact3, taps, S, name):
    _, T, Fd = up3.shape
    W = FFN_LANES
    B, P = T // S, Fd // W

    def body(up_ref, da_ref, wg_ref, wv_ref, dup_ref, dwg_ref, dwv_ref, pad_ug, pad_uv, pad_dg, pad_dv):
        b = pl.program_id(1)
        for pad in (pad_ug, pad_uv, pad_dg, pad_dv):
            _zero_halo(pad, S)
        pad_ug[HALO:HALO + S, :] = up_ref[0].astype(F32)
        pad_uv[HALO:HALO + S, :] = up_ref[1].astype(F32)
        wg = wg_ref[...]
        wv = wv_ref[...]
        for r0 in range(0, S, CONV_ROWS):
            cg = _conv3_at(pad_ug, wg, r0)
            cv = _conv3_at(pad_uv, wv, r0)
            sg = _sigmoid(cg)
            da = da_ref[r0:r0 + CONV_ROWS, :].astype(F32)
            pad_dg[HALO + r0:HALO + r0 + CONV_ROWS, :] = (da * cv) * (sg * (1.0 + cg * (1.0 - sg)))
            pad_dv[HALO + r0:HALO + r0 + CONV_ROWS, :] = da * (cg * sg)
        sums_g = [jnp.zeros((8, W), F32)] * 3
        sums_v = [jnp.zeros((8, W), F32)] * 3
        for r0 in range(0, S, CONV_ROWS):
            dug, parts_g = _conv3_grads_at(pad_dg, pad_ug, wg, r0)
            duv, parts_v = _conv3_grads_at(pad_dv, pad_uv, wv, r0)
            dup_ref[0, r0:r0 + CONV_ROWS, :] = dug.astype(BF16)
            dup_ref[1, r0:r0 + CONV_ROWS, :] = duv.astype(BF16)
            sums_g = [a + p for a, p in zip(sums_g, parts_g)]
            sums_v = [a + p for a, p in zip(sums_v, parts_v)]
        dwg = jnp.concatenate([jnp.sum(a, axis=0, keepdims=True) for a in sums_g], axis=0)
        dwv = jnp.concatenate([jnp.sum(a, axis=0, keepdims=True) for a in sums_v], axis=0)

        @pl.when(b == 0)
        def _():
            dwg_ref[...] = dwg
            dwv_ref[...] = dwv

        @pl.when(b != 0)
        def _():
            dwg_ref[...] += dwg
            dwv_ref[...] += dwv

    tap_out = pl.BlockSpec((3, W), lambda p, b: (0, p))
    return pl.pallas_call(
        body, grid=(P, B), name=name,
        in_specs=[pl.BlockSpec((2, S, W), lambda p, b: (0, b, p)),
                  pl.BlockSpec((None, S, W), lambda p, b: (0, b, p)),
                  pl.BlockSpec((3, W), lambda p, b: (0, p)),
                  pl.BlockSpec((3, W), lambda p, b: (0, P + p))],
        out_specs=[pl.BlockSpec((2, S, W), lambda p, b: (0, b, p)), tap_out, tap_out],
        out_shape=[jax.ShapeDtypeStruct((2, T, Fd), BF16),
                   jax.ShapeDtypeStruct((3, Fd), F32), jax.ShapeDtypeStruct((3, Fd), F32)],
        scratch_shapes=[pltpu.VMEM((S + 2 * HALO, W), F32)] * 4,
        compiler_params=_params(("parallel", "arbitrary")),
    )(up3, dact3, taps, taps)


def _final_norm_loss(x, g, target, tm, name):
    T, D = x.shape

    def body(x_ref, g_ref, t_ref, dx_ref, dg_ref, loss_ref):
        xv = x_ref[...]
        rstd = lax.rsqrt(jnp.mean(xv * xv, axis=-1, keepdims=True) + EPS)
        xn = xv * rstd
        err = xn * g_ref[...] - t_ref[...]
        part = 0.5 * jnp.sum(jnp.mean(err * err, axis=-1, keepdims=True), axis=0, keepdims=True)
        dy = err * (1.0 / D)
        a = dy * g_ref[...]
        dx_ref[...] = rstd * (a - xn * jnp.mean(a * xn, axis=-1, keepdims=True))
        dg = jnp.sum(dy * xn, axis=0, keepdims=True)
        lpart = jnp.broadcast_to(part, (1, LANES))

        @pl.when(pl.program_id(0) == 0)
        def _():
            dg_ref[...] = dg
            loss_ref[...] = lpart

        @pl.when(pl.program_id(0) != 0)
        def _():
            dg_ref[...] += dg
            loss_ref[...] += lpart

    row = pl.BlockSpec((tm, D), lambda i: (i, 0))
    return pl.pallas_call(
        body, grid=(T // tm,), name=name,
        in_specs=[row, pl.BlockSpec((1, D), lambda i: (0, 0)), row],
        out_specs=[row, pl.BlockSpec((1, D), lambda i: (0, 0)), pl.BlockSpec((1, LANES), lambda i: (0, 0))],
        out_shape=[jax.ShapeDtypeStruct((T, D), F32), jax.ShapeDtypeStruct((1, D), F32),
                   jax.ShapeDtypeStruct((1, LANES), F32)],
        compiler_params=_params(("arbitrary",)),
    )(x, g, target)


def _row_tile(rows, cols, budget_elems=512 * 1024):
    tr = rows
    while tr * cols > budget_elems and tr % 32 == 0:
        tr //= 2
    return tr


def _prefetch_call(body, grid, in_specs, out_specs, out_shape, name, sem, aliases=None):
    return pl.pallas_call(
        body, name=name, out_shape=out_shape,
        grid_spec=pltpu.PrefetchScalarGridSpec(num_scalar_prefetch=1, grid=grid, in_specs=in_specs,
                                               out_specs=out_specs),
        input_output_aliases=aliases or {},
        compiler_params=_params(sem))


def _cast_into_full(w, layer, colwise, where, name):
    _, K, N = w.shape
    tr = _row_tile(K, N)
    nrb = K // tr
    full_shape = (1, K, 4 * N) if colwise else (1, 4 * K, N)

    def body(where_ref, w_ref, o_ref):
        del where_ref
        o_ref[...] = w_ref[...].astype(BF16)

    if colwise:
        out_map = lambda i, wh: (0, i, wh[0])
    else:
        out_map = lambda i, wh: (0, wh[0] * nrb + i, 0)
    return _prefetch_call(
        body, (nrb,), [pl.BlockSpec((None, tr, N), lambda i, wh: (layer, i, 0))],
        pl.BlockSpec((None, tr, N), out_map), jax.ShapeDtypeStruct(full_shape, BF16), name,
        ("parallel",))(where, w)


def _chip_sum(g3, other, colwise, where, name):
    L, K, N = g3.shape
    hk, hn = (K // 2, N) if colwise else (K, N // 2)
    tr = _row_tile(hk, hn)
    nrb = hk // tr

    def body(where_ref, g_ref, o_ref, s_ref):
        del where_ref
        s_ref[...] = (g_ref[...].astype(F32) + o_ref[...].astype(F32)).astype(BF16)

    if colwise:
        g_map = lambda l, i, wh: (l, wh[1] * nrb + i, 0)
    else:
        g_map = lambda l, i, wh: (l, i, wh[1])
    blk = pl.BlockSpec((None, tr, hn), lambda l, i, wh: (l, i, 0))
    return _prefetch_call(
        body, (L, nrb), [pl.BlockSpec((None, tr, hn), g_map), blk], blk,
        jax.ShapeDtypeStruct((L, hk, hn), BF16), name, ("parallel", "parallel"))(where, g3, other)


def _owner_sum(chip_sum, received, colwise, where, layer, n_layers, prev, name):
    _, hk, hn = chip_sum.shape
    pk, pn = (hk, hn // 4) if colwise else (hk // 4, hn)
    tr = _row_tile(pk, pn)
    nrb = pk // tr
    shard_shape = (n_layers, 2 * pk, pn) if colwise else (n_layers, pk, 2 * pn)

    def body(where_ref, own_ref, rec_ref, *rest):
        del where_ref
        o_ref = rest[-1]
        acc = own_ref[...].astype(F32)
        for j in range(3):
            acc = acc + rec_ref[j].astype(F32)
        o_ref[...] = acc

    if colwise:
        own_map = lambda i, wh: (0, i, wh[0])
        out_map = lambda i, wh: (layer, wh[1] * nrb + i, 0)
    else:
        own_map = lambda i, wh: (0, wh[0] * nrb + i, 0)
        out_map = lambda i, wh: (layer, i, wh[1])
    in_specs = [pl.BlockSpec((None, tr, pn), own_map),
                pl.BlockSpec((3, None, tr, pn), lambda i, wh: (0, 0, i, 0))]
    operands = [where, chip_sum, received]
    if prev is not None:
        in_specs.append(ANY)
        operands.append(prev)
    return _prefetch_call(
        body, (nrb,), in_specs, pl.BlockSpec((None, tr, pn), out_map), jax.ShapeDtypeStruct(shard_shape, F32), name,
        ("parallel",), None if prev is None else {3: 0})(*operands)


def _adamw(w, g, m, v, name):
    R, Cc = w.shape
    tr = _row_tile(R, Cc, 256 * 1024)

    def body(w_ref, g_ref, m_ref, v_ref, d_ref, nm_ref, nv_ref, go_ref):
        gv = g_ref[...]
        go_ref[...] = gv
        nm = ADAM_B1 * m_ref[...] + (1.0 - ADAM_B1) * gv
        nv = ADAM_B2 * v_ref[...] + (1.0 - ADAM_B2) * (gv * gv)
        m_hat = nm / (1.0 - ADAM_B1 ** ADAM_STEP)
        v_hat = nv / (1.0 - ADAM_B2 ** ADAM_STEP)
        d_ref[...] = -ADAM_LR * (m_hat / (jnp.sqrt(v_hat) + ADAM_EPS) + ADAM_WD * w_ref[...])
        nm_ref[...] = nm
        nv_ref[...] = nv

    blk = pl.BlockSpec((tr, Cc), lambda i: (i, 0))
    shp = jax.ShapeDtypeStruct((R, Cc), F32)
    return pl.pallas_call(
        body, grid=(R // tr,), name=name,
        in_specs=[blk] * 4, out_specs=[blk] * 4, out_shape=[shp] * 4,
        compiler_params=_params(("parallel",)),
    )(w, g, m, v)


COL_SHARDED = (True, False, True, False)


def _position():
    x, y, c = lax.axis_index("x"), lax.axis_index("y"), lax.axis_index("c")
    chips = [(1 - x, y), (x, 1 - y), (1 - x, 1 - y)]
    return x, y, c, chips


def _span(index, size, align):
    return pl.ds(pl.multiple_of(index * size, align), size)


def _window(ref, colwise, shard, half, shards=4):
    _, K, N = ref.shape
    rows = cols = slice(None)
    if colwise:
        if half is not None:
            rows = _span(half, K // 2, 16)
        if shard is not None:
            cols = _span(shard, N // shards, LANES)
    else:
        if shard is not None:
            rows = _span(shard, K // shards, 16)
        if half is not None:
            cols = _span(half, N // 2, LANES)
    return ref.at[:, rows, cols]


HBM = pl.BlockSpec(memory_space=pltpu.HBM)
SEMAPHORES = pl.BlockSpec(memory_space=pltpu.SEMAPHORE)


def _gather_start(fulls, colwise, group_sizes, after, name):
    n = len(fulls)
    n_groups = len(group_sizes)

    n_in = n if after is None else n + 1

    def body(*refs):
        ins = refs[:n]
        sems = refs[n_in:n_in + 2 * n_groups]
        x, y, c, chips = _position()
        me = 2 * x + y
        i = 0
        for g, size in enumerate(group_sizes):
            for a in range(size):
                win = _window(ins[i], colwise[i], me, c)
                for j, chip in enumerate(chips):
                    pltpu.make_async_remote_copy(
                        src_ref=win, dst_ref=win, send_sem=sems[2 * g].at[a * 3 + j],
                        recv_sem=sems[2 * g + 1].at[a * 3 + j],
                        device_id=(chip[0], chip[1], c), device_id_type=MESH_ID).start()
                i += 1

    sem_shapes = []
    for size in group_sizes:
        sem_shapes += [pltpu.SemaphoreType.DMA((3 * size,)), pltpu.SemaphoreType.DMA((3 * size,))]
    operands = [pltpu.with_memory_space_constraint(f, pltpu.HBM) for f in fulls]
    in_specs = [HBM] * n
    if after is not None:
        operands.append(after)
        in_specs.append(ANY)
    outs = pl.pallas_call(
        body, name=name,
        in_specs=in_specs, out_specs=[SEMAPHORES] * (2 * n_groups) + [HBM] * n,
        out_shape=sem_shapes + [pltpu.HBM(f.shape, f.dtype) for f in fulls],
        input_output_aliases={i: 2 * n_groups + i for i in range(n)},
        compiler_params=pltpu.CompilerParams(has_side_effects=pltpu.SideEffectType.DATAFLOW_SIDE_EFFECTING),
    )(*operands)
    sems = [(outs[2 * g], outs[2 * g + 1]) for g in range(n_groups)]
    return sems, list(outs[2 * n_groups:])


def _to_sibling(ref, colwise, chip, half, x, y, c, send_sem, recv_sem):
    win = _window(ref, colwise, 2 * chip[0] + chip[1], half)
    return pltpu.make_async_remote_copy(
        src_ref=win, dst_ref=win, send_sem=send_sem, recv_sem=recv_sem,
        device_id=(x, y, 1 - c), device_id_type=MESH_ID)


def _gather_pass(in_flight, colwise, sems, after, name):
    n = len(in_flight)

    def body(*refs):
        ins = refs[:n]
        send_sems, recv_sems = refs[n], refs[n + 1]
        pass_send, pass_recv = refs[-2 - n], refs[-1 - n]
        x, y, c, chips = _position()
        me = 2 * x + y
        for a in range(n):
            for j, chip in enumerate(chips):
                k = a * 3 + j
                pltpu.make_async_remote_copy(
                    src_ref=_window(ins[a], colwise[a], me, c),
                    dst_ref=_window(ins[a], colwise[a], 2 * chip[0] + chip[1], c),
                    send_sem=send_sems.at[k], recv_sem=recv_sems.at[k],
                    device_id=(chip[0], chip[1], c), device_id_type=MESH_ID).wait()
                _to_sibling(ins[a], colwise[a], chip, c, x, y, c, pass_send.at[k], pass_recv.at[k]).start()

    operands = list(in_flight) + list(sems)
    in_specs = [HBM] * n + [SEMAPHORES] * 2
    if after is not None:
        operands.append(after)
        in_specs.append(ANY)
    outs = pl.pallas_call(
        body, name=name,
        in_specs=in_specs, out_specs=[SEMAPHORES] * 2 + [HBM] * n,
        out_shape=[pltpu.SemaphoreType.DMA((3 * n,)), pltpu.SemaphoreType.DMA((3 * n,))]
        + [pltpu.HBM(f.shape, f.dtype) for f in in_flight],
        input_output_aliases={i: 2 + i for i in range(n)},
        compiler_params=pltpu.CompilerParams(has_side_effects=pltpu.SideEffectType.DATAFLOW_SIDE_EFFECTING),
    )(*operands)
    return (outs[0], outs[1]), list(outs[2:])


def _gather_wait(in_flight, colwise, sems, after, name):
    n = len(in_flight)

    def body(*refs):
        ins = refs[:n]
        send_sems, recv_sems = refs[n], refs[n + 1]
        x, y, c, chips = _position()
        for a in range(n):
            for j, chip in enumerate(chips):
                k = a * 3 + j
                _to_sibling(ins[a], colwise[a], chip, c, x, y, c, send_sems.at[k], recv_sems.at[k]).wait_send()
                _to_sibling(ins[a], colwise[a], chip, 1 - c, x, y, c, send_sems.at[k], recv_sems.at[k]).wait_recv()

    operands = list(in_flight) + list(sems)
    in_specs = [HBM] * n + [SEMAPHORES] * 2
    if after is not None:
        operands.append(after)
        in_specs.append(ANY)
    outs = pl.pallas_call(
        body, name=name,
        in_specs=in_specs, out_specs=[HBM] * n,
        out_shape=[pltpu.HBM(f.shape, f.dtype) for f in in_flight],
        input_output_aliases={i: i for i in range(n)},
        compiler_params=pltpu.CompilerParams(has_side_effects=pltpu.SideEffectType.DATAFLOW_SIDE_EFFECTING),
    )(*operands)
    return list(outs)


def _exchange_copy(g_ref, land_ref, colwise, x, y, c, send_sem, recv_sem):
    return pltpu.make_async_remote_copy(
        src_ref=_window(g_ref, colwise, None, 1 - c), dst_ref=land_ref, send_sem=send_sem, recv_sem=recv_sem,
        device_id=(x, y, 1 - c), device_id_type=MESH_ID)


def _exchange_start(grads, colwise, name):
    n = len(grads)
    lands = []
    for g, cw in zip(grads, colwise):
        L, K, N = g.shape
        lands.append(lax.empty((L, K // 2, N) if cw else (L, K, N // 2), g.dtype))

    def body(*refs):
        src, land = refs[:n], refs[n:2 * n]
        send_sems, recv_sems = refs[2 * n], refs[2 * n + 1]
        x, y, c, _ = _position()
        for i in range(n):
            _exchange_copy(src[i], land[i], colwise[i], x, y, c, send_sems.at[i], recv_sems.at[i]).start()

    arrays = list(grads) + lands
    outs = pl.pallas_call(
        body, name=name,
        in_specs=[HBM] * (2 * n), out_specs=[SEMAPHORES] * 2 + [HBM] * (2 * n),
        out_shape=[pltpu.SemaphoreType.DMA((n,)), pltpu.SemaphoreType.DMA((n,))]
        + [pltpu.HBM(a.shape, a.dtype) for a in arrays],
        input_output_aliases={i: 2 + i for i in range(2 * n)},
        compiler_params=pltpu.CompilerParams(has_side_effects=pltpu.SideEffectType.DATAFLOW_SIDE_EFFECTING),
    )(*[pltpu.with_memory_space_constraint(a, pltpu.HBM) for a in arrays])
    return (outs[0], outs[1]), list(outs[2:2 + n]), list(outs[2 + n:])


def _exchange_wait(grads, lands, colwise, sems, after, name):
    n = len(grads)

    def body(*refs):
        src, land = refs[:n], refs[n:2 * n]
        send_sems, recv_sems = refs[2 * n], refs[2 * n + 1]
        x, y, c, _ = _position()
        for i in range(n):
            _exchange_copy(src[i], land[i], colwise[i], x, y, c, send_sems.at[i], recv_sems.at[i]).wait()

    arrays = list(grads) + list(lands)
    operands = arrays + list(sems)
    in_specs = [HBM] * (2 * n) + [SEMAPHORES] * 2
    if after is not None:
        operands.append(after)
        in_specs.append(ANY)
    outs = pl.pallas_call(
        body, name=name,
        in_specs=in_specs, out_specs=[HBM] * (2 * n),
        out_shape=[pltpu.HBM(a.shape, a.dtype) for a in arrays],
        input_output_aliases={i: i for i in range(2 * n)},
        compiler_params=pltpu.CompilerParams(has_side_effects=pltpu.SideEffectType.DATAFLOW_SIDE_EFFECTING),
    )(*operands)
    return list(outs[:n]), list(outs[n:])


def _scatter_copy(src_ref, land_ref, colwise, j, chip, c, send_sem, recv_sem):
    return pltpu.make_async_remote_copy(
        src_ref=_window(src_ref, colwise, 2 * chip[0] + chip[1], None), dst_ref=land_ref.at[j],
        send_sem=send_sem, recv_sem=recv_sem, device_id=(chip[0], chip[1], c), device_id_type=MESH_ID)


def _scatter_start(chip_sums, colwise, name):
    n = len(chip_sums)
    lands = []
    for g, cw in zip(chip_sums, colwise):
        L, hk, hn = g.shape
        lands.append(lax.empty((3, L, hk, hn // 4) if cw else (3, L, hk // 4, hn), g.dtype))

    def body(*refs):
        src, land = refs[:n], refs[n:2 * n]
        send_sems, recv_sems = refs[2 * n], refs[2 * n + 1]
        x, y, c, chips = _position()
        for i in range(n):
            for j, chip in enumerate(chips):
                _scatter_copy(src[i], land[i], colwise[i], j, chip, c, send_sems.at[i * 3 + j],
                              recv_sems.at[i * 3 + j]).start()

    arrays = list(chip_sums) + lands
    outs = pl.pallas_call(
        body, name=name,
        in_specs=[HBM] * (2 * n), out_specs=[SEMAPHORES] * 2 + [HBM] * (2 * n),
        out_shape=[pltpu.SemaphoreType.DMA((3 * n,)), pltpu.SemaphoreType.DMA((3 * n,))]
        + [pltpu.HBM(a.shape, a.dtype) for a in arrays],
        input_output_aliases={i: 2 + i for i in range(2 * n)},
        compiler_params=pltpu.CompilerParams(has_side_effects=pltpu.SideEffectType.DATAFLOW_SIDE_EFFECTING),
    )(*[pltpu.with_memory_space_constraint(a, pltpu.HBM) for a in arrays])
    return (outs[0], outs[1]), list(outs[2:2 + n]), list(outs[2 + n:])


def _scatter_wait(sources, lands, colwise, sems, after, name):
    n = len(sources)

    def body(*refs):
        src, land = refs[:n], refs[n:2 * n]
        send_sems, recv_sems = refs[2 * n], refs[2 * n + 1]
        x, y, c, chips = _position()
        for i in range(n):
            for j, chip in enumerate(chips):
                cp = _scatter_copy(src[i], land[i], colwise[i], j, chip, c, send_sems.at[i * 3 + j],
                                   recv_sems.at[i * 3 + j])
                cp.wait_send()
                cp.wait_recv()

    arrays = list(sources) + list(lands)
    operands = arrays + list(sems)
    in_specs = [HBM] * (2 * n) + [SEMAPHORES] * 2
    if after is not None:
        operands.append(after)
        in_specs.append(ANY)
    outs = pl.pallas_call(
        body, name=name,
        in_specs=in_specs, out_specs=[HBM] * (2 * n),
        out_shape=[pltpu.HBM(a.shape, a.dtype) for a in arrays],
        input_output_aliases={i: i for i in range(2 * n)},
        compiler_params=pltpu.CompilerParams(has_side_effects=pltpu.SideEffectType.DATAFLOW_SIDE_EFFECTING),
    )(*operands)
    return list(outs[:n]), list(outs[n:])


def _share_with_sibling(shards):
    n = len(shards)

    def body(*refs):
        out = refs[n:2 * n]
        send_sems, recv_sems = refs[2 * n:]
        x, y, c, _ = _position()

        def copy(i, half):
            win = _window(out[i], COL_SHARDED[i], None, half)
            return pltpu.make_async_remote_copy(
                src_ref=win, dst_ref=win, send_sem=send_sems.at[i], recv_sem=recv_sems.at[i],
                device_id=(x, y, 1 - c), device_id_type=MESH_ID)

        for i in range(n):
            copy(i, c).start()
        for i in range(n):
            copy(i, 1 - c).wait_recv()
        for i in range(n):
            copy(i, c).wait_send()

    return pl.pallas_call(
        body, name="grad_share_with_sibling",
        in_specs=[ANY] * n, out_specs=[ANY] * n,
        out_shape=[jax.ShapeDtypeStruct(s.shape, s.dtype) for s in shards],
        input_output_aliases={i: i for i in range(n)},
        scratch_shapes=[pltpu.SemaphoreType.DMA((n,)), pltpu.SemaphoreType.DMA((n,))],
    )(*shards)


def _all_reduce_small(pack, name):
    R, Cc = pack.shape
    n_dev = 8

    def body(p_ref, o_ref, buf, send_sems, recv_sems):
        x, y, c, _ = _position()
        me = 4 * x + 2 * y + c
        buf[me] = p_ref[...]

        def peer(k):
            px = 1 - x if k & 4 else x
            py = 1 - y if k & 2 else y
            pc = 1 - c if k & 1 else c
            return px, py, pc

        def copy(k, incoming):
            px, py, pc = peer(k)
            slot = (4 * px + 2 * py + pc) if incoming else me
            return pltpu.make_async_remote_copy(
                src_ref=p_ref, dst_ref=buf.at[slot], send_sem=send_sems.at[k], recv_sem=recv_sems.at[k],
                device_id=(px, py, pc), device_id_type=MESH_ID)

        for k in range(1, n_dev):
            copy(k, False).start()
        for k in range(1, n_dev):
            copy(k, True).wait_recv()
        for k in range(1, n_dev):
            copy(k, False).wait_send()
        acc = buf[0]
        for j in range(1, n_dev):
            acc = acc + buf[j]
        o_ref[...] = acc

    vmem = pl.BlockSpec(memory_space=pltpu.VMEM)
    return pl.pallas_call(
        body, name=name,
        in_specs=[vmem], out_specs=vmem, out_shape=jax.ShapeDtypeStruct((R, Cc), F32),
        scratch_shapes=[pltpu.VMEM((n_dev, R, Cc), F32), pltpu.SemaphoreType.DMA((n_dev,)),
                        pltpu.SemaphoreType.DMA((n_dev,))],
    )(pack)


def _local_forward_backward(x2, target2, S, pass_on, fetch, reduce_begin, reduce_commit, layers, final_g, tm=512):
    T, D = x2.shape
    C = D // 2
    n_heads = C // GROUP
    n_layers = len(layers)
    weights = {}
    saved = []
    xc = x2
    for li, lw in enumerate(layers):
        if li == 0:
            pass_on(0, None)
            weights.update(fetch(0, None))
        h1, qkv3, cv3 = _norm_proj(xc, lw["norm1"], weights[li, "w_in"], 0, ((3, C, F32), (3, C, BF16)), tm,
                                   min(C, 512), f"l{li}_norm_in_proj")
        if li == 0:
            pass_on(1, h1)
        o, lse, mix = _attn_fwd(qkv3, lw["attn_g"], 2, S, n_heads, f"l{li}_attn_fwd")
        if li == 0:
            weights.update(fetch(1, o))
            pass_on(2, o)
            pass_on(3, o)
        mix = _mix_conv_fwd(cv3, lw["taps"], lw["conv_g"], mix, S, f"l{li}_mix_conv_fwd")
        x_mid = _proj_residual(mix, weights[li, "w_out"], 0, xc, tm, f"l{li}_out_proj")
        if li == 0:
            weights.update(fetch(2, x_mid))
        Fd = weights[li, "ffn_up"].shape[2] // 2
        h2, up3 = _norm_proj(x_mid, lw["norm2"], weights[li, "ffn_up"], 0, ((2, Fd, BF16),), tm // 2, 256,
                             f"l{li}_norm_ffn_up")
        if li == 0:
            weights.update(fetch(3, up3))
        act = _ffn_act_fwd(up3, lw["ffn_taps"], S, f"l{li}_ffn_act_fwd")
        if li + 1 < n_layers:
            pass_on(li + 4, act)
        x_out = _proj_residual(act.reshape(1, T, Fd), weights[li, "ffn_down"], 0, x_mid, tm, f"l{li}_ffn_down")
        if li + 1 < n_layers:
            weights.update(fetch(li + 4, x_out))
        saved.append(dict(x_in=xc, h1=h1, qkv3=qkv3, cv3=cv3, o=o, lse=lse, mix=mix, x_mid=x_mid, h2=h2, up3=up3,
                          act=act))
        xc = x_out

    dx, d_final_g, loss_part = _final_norm_loss(xc, final_g, target2, tm, "final_norm_loss")

    small = [None] * n_layers
    started = None
    for li in reversed(range(n_layers)):
        lw, sv = layers[li], saved[li]
        w_in, w_out, ffn_up, ffn_down = (weights[li, n] for n in ("w_in", "w_out", "ffn_up", "ffn_down"))
        dxb, dact3 = _grad_through_weight(dx, ffn_down, 0, 1, Fd, BF16, tm, 256, f"l{li}_d_act", started)
        Fd = ffn_down.shape[1]
        d_ffn_down = _weight_grad(sv["act"].reshape(1, T, Fd), dxb.reshape(1, T, D), Fd // 2, D, 1024,
                                  0, 1, None, f"l{li}_d_ffn_down")
        dup3, d_taps_g, d_taps_v = _ffn_act_bwd(sv["up3"], dact3, lw["ffn_taps"], S, f"l{li}_ffn_act_bwd")
        d_ffn_up = _weight_grad(sv["h2"].reshape(1, T, D), dup3, D, Fd, 1024, 0, 1, None,
                                f"l{li}_d_ffn_up")
        if li == 0:
            early = reduce_begin({(li, "ffn_down"): d_ffn_down, (li, "ffn_up"): d_ffn_up})
        dx_mid, d_norm2 = _grad_through_proj_norm(dup3, ffn_up, 0, sv["x_mid"], lw["norm2"], dx, tm // 2,
                                                  f"l{li}_d_norm2")
        started = reduce_commit(early, dx_mid) if li == 0 else None
        dxmb, dmix3 = _grad_through_weight(dx_mid, w_out, 0, 2, C, F32, tm, min(C, 512), f"l{li}_d_mix", started)
        d_w_out = _weight_grad(sv["mix"], dxmb.reshape(1, T, D), C, D, 1024, 0, 1, None, f"l{li}_d_w_out")
        dproj, d_attn_g = _attn_bwd(sv["qkv3"], sv["o"], sv["lse"], dmix3, lw["attn_g"], 6, S, n_heads,
                                    f"l{li}_attn_bwd")
        dproj, d_taps, d_conv_g = _mix_conv_bwd(sv["cv3"], dmix3, lw["taps"], lw["conv_g"], dproj, S,
                                                f"l{li}_mix_conv_bwd")
        d_w_in = _weight_grad(sv["h1"].reshape(1, T, D), dproj, D, C, 1024, 0, 1, None, f"l{li}_d_w_in")
        late = {(li, "w_out"): d_w_out, (li, "w_in"): d_w_in}
        if li > 0:
            late.update({(li, "ffn_down"): d_ffn_down, (li, "ffn_up"): d_ffn_up})
        late = reduce_begin(late)
        dx, d_norm1 = _grad_through_proj_norm(dproj, w_in, 0, sv["x_in"], lw["norm1"], dx_mid, tm,
                                              f"l{li}_d_norm1")
        started = reduce_commit(late, dx)
        small[li] = dict(norm1=d_norm1, taps=d_taps, attn_g=d_attn_g, conv_g=d_conv_g, norm2=d_norm2,
                         ffn_taps=jnp.concatenate([d_taps_g, d_taps_v], axis=1))
    return loss_part, dx, small, d_final_g


SMALL_ORDER = ("norm1", "attn_g", "conv_g", "norm2", "taps", "ffn_taps")


def _pack_small(small, d_final_g):
    parts = [small[li][k].reshape(-1) for li in range(len(small)) for k in SMALL_ORDER] + [d_final_g.reshape(-1)]
    return jnp.concatenate(parts).reshape(-1, LANES)


def _unpack_small(pack, small, d_final_g):
    flat = pack.reshape(-1)
    out, pos = [dict() for _ in small], 0
    for li in range(len(small)):
        for k in SMALL_ORDER:
            n = small[li][k].size
            out[li][k] = flat[pos:pos + n].reshape(small[li][k].shape)
            pos += n
    return out, flat[pos:pos + d_final_g.size]


def kernel(x, norm1_g, w_in, mix_conv_w, attn_out_g, conv_out_g, w_out, norm2_g, ffn_up, ffn_conv_w, ffn_down, final_norm_g, loss_target, m_norm1_g, m_w_in, m_mix_conv_w, m_attn_out_g, m_conv_out_g, m_w_out, m_norm2_g, m_ffn_up, m_ffn_conv_w, m_ffn_down, m_final_norm_g, v_norm1_g, v_w_in, v_mix_conv_w, v_attn_out_g, v_conv_out_g, v_w_out, v_norm2_g, v_ffn_up, v_ffn_conv_w, v_ffn_down, v_final_norm_g):
    Bl, S, D = x.shape
    L = w_in.shape[0]
    T = Bl * S
    shard = 2 * lax.axis_index("x") + lax.axis_index("y")
    where = jnp.stack([shard, lax.axis_index("c")]).astype(jnp.int32)
    big_names = ("w_in", "w_out", "ffn_up", "ffn_down")

    taps_w, ftaps_w = mix_conv_w.shape[2], ffn_conv_w.shape[2]
    taps_full = jnp.zeros((L, 3, 4 * taps_w), F32)
    taps_full = lax.dynamic_update_slice(taps_full, mix_conv_w, (0, 0, shard * taps_w))
    ftaps_full = jnp.zeros((L, 3, 4 * ftaps_w), F32)
    ftaps_full = lax.dynamic_update_slice(ftaps_full, ffn_conv_w, (0, 0, shard * ftaps_w))
    tap_pack = jnp.concatenate([taps_full.reshape(-1), ftaps_full.reshape(-1)]).reshape(-1, LANES)
    tap_pack = _all_reduce_small(tap_pack * 0.5, "all_gather_taps")
    n_taps = taps_full.size
    taps_full = tap_pack.reshape(-1)[:n_taps].reshape(taps_full.shape)
    ftaps_full = tap_pack.reshape(-1)[n_taps:].reshape(ftaps_full.shape)

    big_shards = dict(zip(big_names, (w_in, w_out, ffn_up, ffn_down)))
    col_of = dict(zip(big_names, COL_SHARDED))
    groups = [[(0, n)] for n in big_names] + [[(l, n) for n in big_names] for l in range(1, L)]
    sems, in_flight = [], {}
    all_started = tap_pack
    for first, last in ((0, 1), (1, len(groups))):
        keys = [k for g in groups[first:last] for k in g]
        new_sems, arrays = _gather_start(
            [_cast_into_full(big_shards[n], l, col_of[n], where, f"cast_{n}_{l}") for l, n in keys],
            [col_of[n] for _, n in keys], [len(g) for g in groups[first:last]], all_started,
            f"gather_start_{first}")
        sems += new_sems
        in_flight.update(zip(keys, arrays))
        all_started = arrays[-1]

    def pass_on(g, after):
        after = all_started if g == 0 else after
        sems[g], arrays = _gather_pass([in_flight[k] for k in groups[g]], [col_of[n] for _, n in groups[g]],
                                       sems[g], after, f"gather_pass_{g}")
        in_flight.update(zip(groups[g], arrays))

    def fetch(g, after):
        done = _gather_wait([in_flight[k] for k in groups[g]], [col_of[n] for _, n in groups[g]], sems[g], after,
                            f"gather_wait_{g}")
        return dict(zip(groups[g], done))

    pending = []

    begun = []

    def reduce_begin(grads):
        g = len(begun)
        keys = list(grads)
        cols = [col_of[n] for _, n in keys]
        begun.append((g, keys, cols) + _exchange_start([grads[k] for k in keys], cols, f"exchange_start_{g}"))
        return begun[-1]

    def reduce_commit(handle, after):
        g, keys, cols, ex_sems, mine, lands = handle
        mine, others = _exchange_wait(mine, lands, cols, ex_sems, after, f"exchange_wait_{g}")
        chip_sums = [_chip_sum(m, o, cw, where, f"chip_sum_{k[1]}_{k[0]}")
                     for k, m, o, cw in zip(keys, mine, others, cols)]
        pending.append((keys, cols) + _scatter_start(chip_sums, cols, f"scatter_start_{g}"))
        return pending[-1][3][0]

    layers = [dict(norm1=norm1_g[l:l + 1], taps=taps_full[l], attn_g=attn_out_g[l:l + 1],
                   conv_g=conv_out_g[l:l + 1], norm2=norm2_g[l:l + 1], ffn_taps=ftaps_full[l]) for l in range(L)]

    loss_part, dx, small, d_final_g = _local_forward_backward(
        x.reshape(T, D), loss_target.reshape(T, D), S, pass_on, fetch, reduce_begin, reduce_commit, layers,
        final_norm_g.reshape(1, D))
    loss = lax.psum(loss_part[0, 0], ("x", "y", "c"))

    reduced = dict.fromkeys(big_names)
    last_started = pending[-1][3][0]
    for g, (keys, cols, rs_sems, sources, lands) in enumerate(pending):
        after = last_started if g + 1 < len(pending) else None
        sources, lands = _scatter_wait(sources, lands, cols, rs_sems, after, f"scatter_wait_{g}")
        for (l, n), cw, src, land in zip(keys, cols, sources, lands):
            reduced[n] = _owner_sum(src, land, cw, where, l, L, reduced[n], f"owner_sum_{n}_{l}")
    g_big = _share_with_sibling([reduced[n] for n in big_names])

    pack = _all_reduce_small(_pack_small(small, d_final_g), "all_reduce_small_grads")
    g_small, g_final = _unpack_small(pack, small, d_final_g)

    def stacked(key):
        return jnp.stack([g_small[l][key].reshape(g_small[l][key].shape[-2:] if key.endswith("taps") else (-1,))
                          for l in range(L)])

    g_norm1, g_attn, g_conv, g_norm2 = stacked("norm1"), stacked("attn_g"), stacked("conv_g"), stacked("norm2")
    g_taps = lax.dynamic_slice(stacked("taps"), (0, 0, shard * taps_w), (L, 3, taps_w))
    g_ftaps = lax.dynamic_slice(stacked("ffn_taps"), (0, 0, shard * ftaps_w), (L, 3, ftaps_w))

    grads_out = dict(norm1_g=g_norm1, w_in=g_big[0], mix_conv_w=g_taps, attn_out_g=g_attn, conv_out_g=g_conv,
                     w_out=g_big[1], norm2_g=g_norm2, ffn_up=g_big[2], ffn_conv_w=g_ftaps, ffn_down=g_big[3],
                     final_norm_g=g_final)
    weights = dict(norm1_g=norm1_g, w_in=w_in, mix_conv_w=mix_conv_w, attn_out_g=attn_out_g, conv_out_g=conv_out_g,
                   w_out=w_out, norm2_g=norm2_g, ffn_up=ffn_up, ffn_conv_w=ffn_conv_w, ffn_down=ffn_down,
                   final_norm_g=final_norm_g)
    ms = dict(norm1_g=m_norm1_g, w_in=m_w_in, mix_conv_w=m_mix_conv_w, attn_out_g=m_attn_out_g,
              conv_out_g=m_conv_out_g, w_out=m_w_out, norm2_g=m_norm2_g, ffn_up=m_ffn_up, ffn_conv_w=m_ffn_conv_w,
              ffn_down=m_ffn_down, final_norm_g=m_final_norm_g)
    vs = dict(norm1_g=v_norm1_g, w_in=v_w_in, mix_conv_w=v_mix_conv_w, attn_out_g=v_attn_out_g,
              conv_out_g=v_conv_out_g, w_out=v_w_out, norm2_g=v_norm2_g, ffn_up=v_ffn_up, ffn_conv_w=v_ffn_conv_w,
              ffn_down=v_ffn_down, final_norm_g=v_final_norm_g)
    names = list(weights)
    small_names = [n for n in names if n not in big_names]
    delta, new_m, new_v = {}, {}, {}
    for n in big_names:
        shp = weights[n].shape
        two_d = (shp[0] * shp[1], shp[2])
        d_, m_, v_, g_ = _adamw(weights[n].reshape(two_d), grads_out[n].reshape(two_d), ms[n].reshape(two_d),
                                vs[n].reshape(two_d), f"adamw_{n}")
        delta[n], new_m[n], new_v[n], grads_out[n] = (a.reshape(shp) for a in (d_, m_, v_, g_))

    def packed(tree):
        return jnp.concatenate([tree[n].reshape(-1) for n in small_names]).reshape(-1, LANES)

    d_, m_, v_, _ = _adamw(packed(weights), packed(grads_out), packed(ms), packed(vs), "adamw_small")
    pos = 0
    for n in small_names:
        size, shp = weights[n].size, weights[n].shape
        delta[n] = d_.reshape(-1)[pos:pos + size].reshape(shp)
        new_m[n] = m_.reshape(-1)[pos:pos + size].reshape(shp)
        new_v[n] = v_.reshape(-1)[pos:pos + size].reshape(shp)
        pos += size

    return (loss, dx.reshape(Bl, S, D), *[grads_out[n] for n in names], *[delta[n] for n in names],
            *[new_m[n] for n in names], *[new_v[n] for n in names])
```

```python
import functools
import math

import jax
import jax.numpy as jnp
from jax import lax
from jax.experimental import pallas as pl
from jax.experimental.pallas import tpu as pltpu

F32 = jnp.float32
BF16 = jnp.bfloat16
EPS = 1e-6
GROUP = 64
LANES = 128
BAND = 128
DILATIONS = (1, 4, 16)
NEG = -1e30
MIB = 1024 * 1024
MESH_ID = pl.DeviceIdType.MESH

ADAM_LR = 0.001
ADAM_B1 = 0.9
ADAM_B2 = 0.999
ADAM_EPS = 1e-08
ADAM_WD = 0.01
ADAM_STEP = 10


ANY = pl.BlockSpec(memory_space=pl.ANY)


def _params(sem=None, vmem_mb=48):
    return pltpu.CompilerParams(dimension_semantics=sem, vmem_limit_bytes=vmem_mb * MIB)


def _nt(a, b):
    return lax.dot_general(a, b, (((1,), (1,)), ((), ())), preferred_element_type=F32)


def _tn(a, b):
    return lax.dot_general(a, b, (((0,), (0,)), ((), ())), preferred_element_type=F32)


def _seg_sum(x, is_a):
    s_a = jnp.sum(jnp.where(is_a, x, 0.0), axis=-1, keepdims=True)
    s_b = jnp.sum(jnp.where(is_a, 0.0, x), axis=-1, keepdims=True)
    return jnp.where(is_a, s_a, s_b)


def _lane_is_a():
    return lax.broadcasted_iota(jnp.int32, (1, LANES), 1) < GROUP


def _norm_proj(x, g, w3, layer, groups, tm, chunk, name):
    T, D = x.shape
    N = w3.shape[2]
    assert sum(p * c for p, c, _ in groups) == N and T % tm == 0

    def body(x_ref, g_ref, w_ref, h_ref, *out_refs):
        xv = x_ref[...]
        rstd = lax.rsqrt(jnp.mean(xv * xv, axis=-1, keepdims=True) + EPS)
        h = ((xv * rstd) * g_ref[...]).astype(BF16)
        h_ref[...] = h
        col = 0
        for (pieces, width, dtype), o_ref in zip(groups, out_refs):
            for p in range(pieces):
                for c0 in range(0, width, chunk):
                    acc = jnp.dot(h, w_ref[:, col + c0:col + c0 + chunk], preferred_element_type=F32)
                    o_ref[p, :, c0:c0 + chunk] = acc.astype(dtype)
                col += width

    out_shape = [jax.ShapeDtypeStruct((T, D), BF16)]
    out_specs = [pl.BlockSpec((tm, D), lambda i: (i, 0))]
    for pieces, width, dtype in groups:
        assert width % chunk == 0
        out_shape.append(jax.ShapeDtypeStruct((pieces, T, width), dtype))
        out_specs.append(pl.BlockSpec((pieces, tm, width), lambda i: (0, i, 0)))
    return pl.pallas_call(
        body, grid=(T // tm,), name=name,
        in_specs=[pl.BlockSpec((tm, D), lambda i: (i, 0)),
                  pl.BlockSpec((1, D), lambda i: (0, 0)),
                  pl.BlockSpec((None, D, N), lambda i: (layer, 0, 0))],
        out_specs=out_specs, out_shape=out_shape,
        compiler_params=_params(("parallel",), 56),
    )(x, g, w3)


def _proj_residual(pieces3, w3, layer, x, tm, name):
    P, T, C = pieces3.shape
    D = w3.shape[2]

    def body(a_ref, w_ref, x_ref, o_ref):
        acc = x_ref[...]
        for p in range(P):
            acc = acc + jnp.dot(a_ref[p], w_ref[p * C:(p + 1) * C, :], preferred_element_type=F32)
        o_ref[...] = acc

    return pl.pallas_call(
        body, grid=(T // tm,), name=name,
        in_specs=[pl.BlockSpec((P, tm, C), lambda i: (0, i, 0)),
                  pl.BlockSpec((None, P * C, D), lambda i: (layer, 0, 0)),
                  pl.BlockSpec((tm, D), lambda i: (i, 0))],
        out_specs=pl.BlockSpec((tm, D), lambda i: (i, 0)),
        out_shape=jax.ShapeDtypeStruct((T, D), F32),
        compiler_params=_params(("parallel",)),
    )(pieces3, w3, x)


def _grad_through_weight(dy, w3, layer, pieces, width, out_dtype, tm, chunk, name, after=None):
    T, D = dy.shape

    def body(dy_ref, w_ref, *rest):
        dyb_ref, o_ref = rest[-2:]
        dyb = dy_ref[...].astype(BF16)
        dyb_ref[...] = dyb
        for p in range(pieces):
            for c0 in range(0, width, chunk):
                r0 = p * width + c0
                o_ref[p, :, c0:c0 + chunk] = _nt(dyb, w_ref[r0:r0 + chunk, :]).astype(out_dtype)

    in_specs = [pl.BlockSpec((tm, D), lambda i: (i, 0)),
                pl.BlockSpec((None, pieces * width, D), lambda i: (layer, 0, 0))]
    operands = [dy, w3]
    if after is not None:
        in_specs.append(ANY)
        operands.append(after)
    return pl.pallas_call(
        body, grid=(T // tm,), name=name,
        in_specs=in_specs,
        out_specs=[pl.BlockSpec((tm, D), lambda i: (i, 0)),
                   pl.BlockSpec((pieces, tm, width), lambda i: (0, i, 0))],
        out_shape=[jax.ShapeDtypeStruct((T, D), BF16),
                   jax.ShapeDtypeStruct((pieces, T, width), out_dtype)],
        compiler_params=_params(("parallel",)),
    )(*operands)


def _grad_through_proj_norm(dp3, w3, layer, x, g, dx_in, tm, name):
    P, T, C = dp3.shape
    D = w3.shape[1]

    def body(dp_ref, w_ref, x_ref, g_ref, dxin_ref, dx_ref, dg_ref):
        dh = _nt(dp_ref[0], w_ref[:, 0:C])
        for p in range(1, P):
            dh = dh + _nt(dp_ref[p], w_ref[:, p * C:(p + 1) * C])
        xv = x_ref[...]
        rstd = lax.rsqrt(jnp.mean(xv * xv, axis=-1, keepdims=True) + EPS)
        xn = xv * rstd
        a = dh * g_ref[...]
        dx_ref[...] = dxin_ref[...] + rstd * (a - xn * jnp.mean(a * xn, axis=-1, keepdims=True))
        part = jnp.sum(dh * xn, axis=0, keepdims=True)

        @pl.when(pl.program_id(0) == 0)
        def _():
            dg_ref[...] = part

        @pl.when(pl.program_id(0) != 0)
        def _():
            dg_ref[...] += part

    return pl.pallas_call(
        body, grid=(T // tm,), name=name,
        in_specs=[pl.BlockSpec((P, tm, C), lambda i: (0, i, 0)),
                  pl.BlockSpec((None, D, P * C), lambda i: (layer, 0, 0)),
                  pl.BlockSpec((tm, D), lambda i: (i, 0)),
                  pl.BlockSpec((1, D), lambda i: (0, 0)),
                  pl.BlockSpec((tm, D), lambda i: (i, 0))],
        out_specs=[pl.BlockSpec((tm, D), lambda i: (i, 0)),
                   pl.BlockSpec((1, D), lambda i: (0, 0))],
        out_shape=[jax.ShapeDtypeStruct((T, D), F32), jax.ShapeDtypeStruct((1, D), F32)],
        compiler_params=_params(("arbitrary",), 56),
    )(dp3, w3, x, g, dx_in)


def _weight_grad(a3, g3, ta, tg, tt, layer, n_layers, prev, name):
    PA, T, CA = a3.shape
    PG, _, CG = g3.shape
    na, ng, nt = CA // ta, CG // tg, T // tt
    assert CA % ta == 0 and CG % tg == 0 and T % tt == 0

    def body(a_ref, g_ref, *rest):
        o_ref, acc_ref = rest[-2:]
        t = pl.program_id(2)
        part = _tn(a_ref[...], g_ref[...])

        @pl.when(t == 0)
        def _():
            acc_ref[...] = part

        @pl.when(t != 0)
        def _():
            acc_ref[...] += part

        @pl.when(t == nt - 1)
        def _():
            o_ref[...] = acc_ref[...].astype(o_ref.dtype)

    in_specs = [pl.BlockSpec((None, tt, ta), lambda i, j, t: (i // na, t, i % na)),
                pl.BlockSpec((None, tt, tg), lambda i, j, t: (j // ng, t, j % ng))]
    operands = [a3, g3]
    if prev is not None:
        in_specs.append(pl.BlockSpec(memory_space=pl.ANY))
        operands.append(prev)
    return pl.pallas_call(
        body, grid=(PA * na, PG * ng, nt), name=name,
        in_specs=in_specs,
        out_specs=pl.BlockSpec((None, ta, tg), lambda i, j, t: (layer, i, j)),
        out_shape=jax.ShapeDtypeStruct((n_layers, PA * CA, PG * CG), BF16),
        scratch_shapes=[pltpu.VMEM((ta, tg), F32)],
        input_output_aliases={} if prev is None else {2: 0},
        compiler_params=_params(("parallel", "parallel", "arbitrary"), 56),
    )(*operands)


def _bias_tables(bm_ref, pair, n_heads):
    ii = lax.broadcasted_iota(jnp.int32, (BAND, 2 * BAND), 0)
    jj = lax.broadcasted_iota(jnp.int32, (BAND, 2 * BAND), 1)
    dist = BAND + ii - jj
    valid = (dist >= 0) & (dist <= BAND)
    distf = dist.astype(F32)
    for hh in range(2):
        head = (2 * pair + hh + 1).astype(F32)
        slope = jnp.exp(jnp.full((1, 1), -8.0 / n_heads * math.log(2.0), F32) * head)
        for bi, d in enumerate(DILATIONS):
            bm_ref[bi, hh * BAND:(hh + 1) * BAND, :] = jnp.where(valid, -(slope * d) * distf, NEG)


def _stack_heads(x, is_a):
    zero = jnp.zeros_like(x)
    return jnp.concatenate([jnp.where(is_a, x, zero), jnp.where(is_a, zero, x)], axis=0)


def _unstack_heads(x2, is_a):
    return jnp.where(is_a, x2[0:BAND], x2[BAND:2 * BAND])


def _gather_residues(dst_ref, src, d, S, convert):
    L = S // d
    for r in range(d):
        rows = pl.ds(r, L, stride=d) if d > 1 else slice(None)
        dst_ref[r * L:(r + 1) * L, :] = convert(src(rows))


def _block_rows(t, d, S):
    nb = S // (BAND * d)
    n = t % nb
    has_prev = jnp.minimum(n, 1)
    cur = pl.ds(pl.multiple_of(t * BAND, BAND), BAND)
    prev = pl.ds(pl.multiple_of((t - has_prev) * BAND, BAND), BAND)
    return cur, prev, has_prev


def _first_block_penalty(has_prev):
    jrow = lax.broadcasted_iota(jnp.int32, (1, 2 * BAND), 1)
    pen = jnp.where(has_prev == 0, NEG, 0.0).astype(F32)
    return jnp.where(jrow < BAND, pen, 0.0)


def _attn_fwd(qkv3, gain, mix_shape_pieces, S, n_heads, name):
    _, T, C = qkv3.shape
    B, P = T // S, C // LANES
    NBLK = S // BAND
    scale = GROUP ** -0.5
    nbr = len(DILATIONS)
    RC = 256

    def body(qkv_ref, g_ref, o_ref, lse_ref, an_ref, qs, ks, vs, op, mp, lp, ob, mb, lb, bm):
        pair = pl.program_id(1)
        is_a = _lane_is_a()
        _bias_tables(bm, pair, n_heads)

        for bi, d in enumerate(DILATIONS):
            nb = S // (BAND * d)
            _gather_residues(qs, lambda rows: qkv_ref.at[0][rows, :], d, S, lambda v: (v * scale).astype(BF16))
            _gather_residues(ks, lambda rows: qkv_ref.at[1][rows, :], d, S, lambda v: v.astype(BF16))
            _gather_residues(vs, lambda rows: qkv_ref.at[2][rows, :], d, S, lambda v: v.astype(BF16))
            o_dst, m_dst, l_dst = (ob.at[bi], mb.at[bi], lb.at[bi]) if d == 1 else (op, mp, lp)

            def block(t, carry, bi=bi, d=d, nb=nb, o_dst=o_dst, m_dst=m_dst, l_dst=l_dst):
                cur, prev, has_prev = _block_rows(t, d, S)
                q2 = _stack_heads(qs[cur, :], is_a)
                kc = jnp.concatenate([ks[prev, :], ks[cur, :]], axis=0)
                vc = jnp.concatenate([vs[prev, :], vs[cur, :]], axis=0)
                s = _nt(q2, kc) + bm[bi] + _first_block_penalty(has_prev)
                m = jnp.max(s, axis=-1, keepdims=True)
                e = jnp.exp(s - m)
                l = jnp.sum(e, axis=-1, keepdims=True)
                pv = jnp.dot(e.astype(BF16), vc, preferred_element_type=F32)
                o_dst[cur, :] = _unstack_heads(pv, is_a)
                m_dst[cur, :] = _unstack_heads(m, is_a)
                l_dst[cur, :] = _unstack_heads(l, is_a)
                return carry

            lax.fori_loop(0, NBLK, block, 0, unroll=8)
            if d > 1:
                L = S // d
                for r in range(d):
                    rows = pl.ds(r, L, stride=d)
                    ob.at[bi][rows, :] = op[r * L:(r + 1) * L, :]
                    mb.at[bi][rows, :] = mp[r * L:(r + 1) * L, :]
                    lb.at[bi][rows, :] = lp[r * L:(r + 1) * L, :]

        def finish(ci, carry):
            rs = pl.ds(pl.multiple_of(ci * RC, RC), RC)
            ms = [mb[bi, rs, :] for bi in range(nbr)]
            mmax = functools.reduce(jnp.maximum, ms)
            ws = [jnp.exp(m - mmax) for m in ms]
            num = sum(ob[bi, rs, :] * ws[bi] for bi in range(nbr))
            den = sum(lb[bi, rs, :] * ws[bi] for bi in range(nbr))
            o = num / den
            o_ref[rs, :] = o
            lse_ref[rs, :] = mmax + jnp.log(den)
            rstd = lax.rsqrt(_seg_sum(o * o, is_a) * (1.0 / GROUP) + EPS)
            an_ref[rs, :] = ((o * rstd) * g_ref[...]).astype(BF16)
            return carry

        lax.fori_loop(0, S // RC, finish, 0)

    seq = pl.BlockSpec((S, LANES), lambda b, p: (b, p))
    return pl.pallas_call(
        body, grid=(B, P), name=name,
        in_specs=[pl.BlockSpec((3, S, LANES), lambda b, p: (0, b, p)),
                  pl.BlockSpec((1, LANES), lambda b, p: (0, p))],
        out_specs=[seq, seq, pl.BlockSpec((None, S, LANES), lambda b, p: (0, b, p))],
        out_shape=[jax.ShapeDtypeStruct((T, C), F32), jax.ShapeDtypeStruct((T, C), F32),
                   jax.ShapeDtypeStruct((mix_shape_pieces, T, C), BF16)],
        scratch_shapes=[pltpu.VMEM((S, LANES), BF16)] * 3 + [pltpu.VMEM((S, LANES), F32)] * 3
        + [pltpu.VMEM((nbr, S, LANES), F32)] * 3 + [pltpu.VMEM((nbr, 2 * BAND, 2 * BAND), F32)],
        compiler_params=_params(("parallel", "parallel")),
    )(qkv3, gain)


def _attn_bwd(qkv3, o, lse, dmix3, gain, dproj_pieces, S, n_heads, name):
    _, T, C = qkv3.shape
    B, P = T // S, C // LANES
    NBLK = S // BAND
    scale = GROUP ** -0.5
    nbr = len(DILATIONS)
    RC = 256

    def body(qkv_ref, o_ref, lse_ref, dn_ref, g_ref, dqkv_ref, dg_ref,
             do_n, dd_n, qs, ks, vs, dos, lses, dds, dqp, dkp, dvp, dqn, dkn, dvn, bm):
        pair = pl.program_id(0)
        b = pl.program_id(1)
        is_a = _lane_is_a()
        _bias_tables(bm, pair, n_heads)

        def prologue(ci, dg_acc):
            rs = pl.ds(pl.multiple_of(ci * RC, RC), RC)
            ov = o_ref[rs, :]
            dn = dn_ref[rs, :]
            rstd = lax.rsqrt(_seg_sum(ov * ov, is_a) * (1.0 / GROUP) + EPS)
            on = ov * rstd
            a = dn * g_ref[...]
            do = rstd * (a - on * (_seg_sum(a * on, is_a) * (1.0 / GROUP)))
            do_n[rs, :] = do
            dd_n[rs, :] = _seg_sum(do * ov, is_a)
            zero = jnp.zeros((RC, LANES), F32)
            dqn[rs, :] = zero
            dkn[rs, :] = zero
            dvn[rs, :] = zero
            return dg_acc + jnp.sum(dn * on, axis=0, keepdims=True)

        dg_part = lax.fori_loop(0, S // RC, prologue, jnp.zeros((1, LANES), F32))

        @pl.when(b == 0)
        def _():
            dg_ref[...] = dg_part

        @pl.when(b != 0)
        def _():
            dg_ref[...] += dg_part

        for bi, d in enumerate(DILATIONS):
            nb = S // (BAND * d)
            L = S // d
            _gather_residues(qs, lambda rows: qkv_ref.at[0][rows, :], d, S, lambda v: (v * scale).astype(BF16))
            _gather_residues(ks, lambda rows: qkv_ref.at[1][rows, :], d, S, lambda v: v.astype(BF16))
            _gather_residues(vs, lambda rows: qkv_ref.at[2][rows, :], d, S, lambda v: v.astype(BF16))
            _gather_residues(dos, lambda rows: do_n[rows, :], d, S, lambda v: v.astype(BF16))
            if d == 1:
                lse_src, dd_src, dq_dst, dk_dst, dv_dst = lse_ref, dd_n, dqn, dkn, dvn
            else:
                _gather_residues(lses, lambda rows: lse_ref[rows, :], d, S, lambda v: v)
                _gather_residues(dds, lambda rows: dd_n[rows, :], d, S, lambda v: v)
                dkp[...] = jnp.zeros((S, LANES), F32)
                dvp[...] = jnp.zeros((S, LANES), F32)
                lse_src, dd_src, dq_dst, dk_dst, dv_dst = lses, dds, dqp, dkp, dvp

            def block(t, carry, bi=bi, d=d, lse_src=lse_src, dd_src=dd_src, dq_dst=dq_dst, dk_dst=dk_dst,
                      dv_dst=dv_dst):
                cur, prev, has_prev = _block_rows(t, d, S)
                q2 = _stack_heads(qs[cur, :], is_a)
                do2 = _stack_heads(dos[cur, :], is_a)
                lse_t = lse_src[cur, :]
                dd_t = dd_src[cur, :]
                lse2 = jnp.concatenate([lse_t[:, 0:1], lse_t[:, GROUP:GROUP + 1]], axis=0)
                dd2 = jnp.concatenate([dd_t[:, 0:1], dd_t[:, GROUP:GROUP + 1]], axis=0)
                kc = jnp.concatenate([ks[prev, :], ks[cur, :]], axis=0)
                vc = jnp.concatenate([vs[prev, :], vs[cur, :]], axis=0)
                s = _nt(q2, kc) + bm[bi] + _first_block_penalty(has_prev)
                p = jnp.exp(s - lse2)
                ds = (p * (_nt(do2, vc) - dd2)).astype(BF16)
                dq = _unstack_heads(jnp.dot(ds, kc, preferred_element_type=F32), is_a)
                dk = _tn(ds, q2)
                dv = _tn(p.astype(BF16), do2)
                dq_dst[cur, :] = dq
                dk_dst[prev, :] += dk[0:BAND, :]
                dv_dst[prev, :] += dv[0:BAND, :]
                dk_dst[cur, :] += dk[BAND:2 * BAND, :]
                dv_dst[cur, :] += dv[BAND:2 * BAND, :]
                return carry

            lax.fori_loop(0, NBLK, block, 0, unroll=8)
            if d > 1:
                for r in range(d):
                    rows = pl.ds(r, L, stride=d)
                    dqn[rows, :] += dqp[r * L:(r + 1) * L, :]
                    dkn[rows, :] += dkp[r * L:(r + 1) * L, :]
                    dvn[rows, :] += dvp[r * L:(r + 1) * L, :]

        dqkv_ref[0] = (dqn[...] * scale).astype(BF16)
        dqkv_ref[1] = dkn[...].astype(BF16)
        dqkv_ref[2] = dvn[...].astype(BF16)

    seq = pl.BlockSpec((S, LANES), lambda p, b: (b, p))
    f32_seq = pltpu.VMEM((S, LANES), F32)
    bf_seq = pltpu.VMEM((S, LANES), BF16)
    return pl.pallas_call(
        body, grid=(P, B), name=name,
        in_specs=[pl.BlockSpec((3, S, LANES), lambda p, b: (0, b, p)), seq, seq,
                  pl.BlockSpec((None, S, LANES), lambda p, b: (0, b, p)),
                  pl.BlockSpec((1, LANES), lambda p, b: (0, p))],
        out_specs=[pl.BlockSpec((3, S, LANES), lambda p, b: (0, b, p)),
                   pl.BlockSpec((1, LANES), lambda p, b: (0, p))],
        out_shape=[jax.ShapeDtypeStruct((dproj_pieces, T, C), BF16), jax.ShapeDtypeStruct((1, C), F32)],
        scratch_shapes=[f32_seq, f32_seq, bf_seq, bf_seq, bf_seq, bf_seq, f32_seq, f32_seq,
                        f32_seq, f32_seq, f32_seq, f32_seq, f32_seq, f32_seq,
                        pltpu.VMEM((nbr, 2 * BAND, 2 * BAND), F32)],
        compiler_params=_params(("parallel", "arbitrary")),
    )(qkv3, o, lse, dmix3, gain)


def _delay(x, k, row):
    return jnp.where(row >= k, pltpu.roll(x, k, 0), 0.0)


def _advance(x, k, row, S):
    return jnp.where(row < S - k, pltpu.roll(x, S - k, 0), 0.0)


def _conv3(x, w, row):
    return (w[0:1, :] * _delay(x, 2, row) + w[1:2, :] * _delay(x, 1, row)) + w[2:3, :] * x


HALO = 8


CONV_ROWS = 128
FFN_LANES = 128


def _zero_halo(pad_ref, S):
    zeros = jnp.zeros((HALO, pad_ref.shape[1]), pad_ref.dtype)
    pad_ref[0:HALO, :] = zeros
    pad_ref[HALO + S:2 * HALO + S, :] = zeros


def _window_at(pad_ref, r0, shift):
    return pad_ref[HALO + r0 + shift:HALO + r0 + shift + CONV_ROWS, :]


def _conv3_at(pad_ref, w, r0):
    return ((w[0:1, :] * _window_at(pad_ref, r0, -2) + w[1:2, :] * _window_at(pad_ref, r0, -1))
            + w[2:3, :] * _window_at(pad_ref, r0, 0))


def _conv3_grads_at(dz_ref, x_ref, w, r0):
    dz, dz1, dz2 = (_window_at(dz_ref, r0, k) for k in range(3))
    x = _window_at(x_ref, r0, 0)
    dx = (w[2:3, :] * dz + w[1:2, :] * dz1) + w[0:1, :] * dz2
    parts = [jnp.sum((d * x).reshape(CONV_ROWS // 8, 8, x.shape[1]), axis=0) for d in (dz2, dz1, dz)]
    return dx, parts


def _conv3_grads(dz, x, w, row, S):
    dz1 = _advance(dz, 1, row, S)
    dz2 = _advance(dz, 2, row, S)
    dx = (w[2:3, :] * dz + w[1:2, :] * dz1) + w[0:1, :] * dz2
    dw = jnp.concatenate([jnp.sum(dz2 * x, axis=0, keepdims=True),
                          jnp.sum(dz1 * x, axis=0, keepdims=True),
                          jnp.sum(dz * x, axis=0, keepdims=True)], axis=0)
    return dx, dw


def _mix_conv_fwd(cv3, taps, gain, mix, S, name):
    _, T, C = cv3.shape
    B, P = T // S, C // LANES

    def body(cv_ref, w_ref, g_ref, mix_hbm, y_ref, pad_c):
        del mix_hbm
        is_a = _lane_is_a()
        _zero_halo(pad_c, S)
        pad_c[HALO:HALO + S, :] = cv_ref[1].astype(F32) * cv_ref[2].astype(F32)
        w = w_ref[...]
        for r0 in range(0, S, CONV_ROWS):
            y = cv_ref[0, r0:r0 + CONV_ROWS, :].astype(F32) * _conv3_at(pad_c, w, r0)
            rstd = lax.rsqrt(_seg_sum(y * y, is_a) * (1.0 / GROUP) + EPS)
            y_ref[r0:r0 + CONV_ROWS, :] = ((y * rstd) * g_ref[...]).astype(BF16)

    return pl.pallas_call(
        body, grid=(B, P), name=name,
        in_specs=[pl.BlockSpec((3, S, LANES), lambda b, p: (0, b, p)),
                  pl.BlockSpec((3, LANES), lambda b, p: (0, p)),
                  pl.BlockSpec((1, LANES), lambda b, p: (0, p)),
                  pl.BlockSpec(memory_space=pl.ANY)],
        out_specs=pl.BlockSpec((None, S, LANES), lambda b, p: (1, b, p)),
        out_shape=jax.ShapeDtypeStruct(mix.shape, mix.dtype),
        scratch_shapes=[pltpu.VMEM((S + 2 * HALO, LANES), F32)],
        input_output_aliases={3: 0},
        compiler_params=_params(("parallel", "parallel")),
    )(cv3, taps, gain, mix)


def _mix_conv_bwd(cv3, dmix3, taps, gain, dproj, S, name):
    _, T, C = cv3.shape
    B, P = T // S, C // LANES

    def body(cv_ref, dn_ref, w_ref, g_ref, dproj_hbm, dcv_ref, dw_ref, dg_ref):
        del dproj_hbm
        b = pl.program_id(1)
        row = lax.broadcasted_iota(jnp.int32, (S, 1), 0)
        is_a = _lane_is_a()
        w = w_ref[...]
        gb = cv_ref[0].astype(F32)
        gc = cv_ref[1].astype(F32)
        u = cv_ref[2].astype(F32)
        c = gc * u
        z = _conv3(c, w, row)
        y = gb * z
        rstd = lax.rsqrt(_seg_sum(y * y, is_a) * (1.0 / GROUP) + EPS)
        yn = y * rstd
        dn = dn_ref[...]
        a = dn * g_ref[...]
        dy = rstd * (a - yn * (_seg_sum(a * yn, is_a) * (1.0 / GROUP)))
        dg = jnp.sum(dn * yn, axis=0, keepdims=True)
        dc, dw = _conv3_grads(dy * gb, c, w, row, S)
        dcv_ref[0] = (dy * z).astype(BF16)
        dcv_ref[1] = (dc * u).astype(BF16)
        dcv_ref[2] = (dc * gc).astype(BF16)

        @pl.when(b == 0)
        def _():
            dw_ref[...] = dw
            dg_ref[...] = dg

        @pl.when(b != 0)
        def _():
            dw_ref[...] += dw
            dg_ref[...] += dg

    return pl.pallas_call(
        body, grid=(P, B), name=name,
        in_specs=[pl.BlockSpec((3, S, LANES), lambda p, b: (0, b, p)),
                  pl.BlockSpec((None, S, LANES), lambda p, b: (1, b, p)),
                  pl.BlockSpec((3, LANES), lambda p, b: (0, p)),
                  pl.BlockSpec((1, LANES), lambda p, b: (0, p)),
                  pl.BlockSpec(memory_space=pl.ANY)],
        out_specs=[pl.BlockSpec((3, S, LANES), lambda p, b: (1, b, p)),
                   pl.BlockSpec((3, LANES), lambda p, b: (0, p)),
                   pl.BlockSpec((1, LANES), lambda p, b: (0, p))],
        out_shape=[jax.ShapeDtypeStruct(dproj.shape, dproj.dtype),
                   jax.ShapeDtypeStruct((3, C), F32), jax.ShapeDtypeStruct((1, C), F32)],
        input_output_aliases={4: 0},
        compiler_params=_params(("parallel", "arbitrary")),
    )(cv3, dmix3, taps, gain, dproj)


def _sigmoid(x):
    return 0.5 * jnp.tanh(0.5 * x) + 0.5


def _ffn_act_fwd(up3, taps, S, name):
    _, T, Fd = up3.shape
    W = FFN_LANES
    B, P = T // S, Fd // W

    def body(up_ref, wg_ref, wv_ref, act_ref, pad_g, pad_v):
        _zero_halo(pad_g, S)
        _zero_halo(pad_v, S)
        pad_g[HALO:HALO + S, :] = up_ref[0].astype(F32)
        pad_v[HALO:HALO + S, :] = up_ref[1].astype(F32)
        wg = wg_ref[...]
        wv = wv_ref[...]
        for r0 in range(0, S, CONV_ROWS):
            cg = _conv3_at(pad_g, wg, r0)
            cv = _conv3_at(pad_v, wv, r0)
            act_ref[r0:r0 + CONV_ROWS, :] = ((cg * _sigmoid(cg)) * cv).astype(BF16)

    return pl.pallas_call(
        body, grid=(B, P), name=name,
        in_specs=[pl.BlockSpec((2, S, W), lambda b, p: (0, b, p)),
                  pl.BlockSpec((3, W), lambda b, p: (0, p)),
                  pl.BlockSpec((3, W), lambda b, p: (0, P + p))],
        out_specs=pl.BlockSpec((S, W), lambda b, p: (b, p)),
        out_shape=jax.ShapeDtypeStruct((T, Fd), BF16),
        scratch_shapes=[pltpu.VMEM((S + 2 * HALO, W), F32)] * 2,
        compiler_params=_params(("parallel", "parallel")),
    )(up3, taps, taps)


def _ffn_act_bwd(up3, dact3, taps, S, name):
    _, T, Fd = up3.shape
    W = FFN_LANES
    B, P = T // S, Fd // W

    def body(up_ref, da_ref, wg_ref, wv_ref, dup_ref, dwg_ref, dwv_ref, pad_ug, pad_uv, pad_dg, pad_dv):
        b = pl.program_id(1)
        for pad in (pad_ug, pad_uv, pad_dg, pad_dv):
            _zero_halo(pad, S)
        pad_ug[HALO:HALO + S, :] = up_ref[0].astype(F32)
        pad_uv[HALO:HALO + S, :] = up_ref[1].astype(F32)
        wg = wg_ref[...]
        wv = wv_ref[...]
        for r0 in range(0, S, CONV_ROWS):
            cg = _conv3_at(pad_ug, wg, r0)
            cv = _conv3_at(pad_uv, wv, r0)
            sg = _sigmoid(cg)
            da = da_ref[r0:r0 + CONV_ROWS, :].astype(F32)
            pad_dg[HALO + r0:HALO + r0 + CONV_ROWS, :] = (da * cv) * (sg * (1.0 + cg * (1.0 - sg)))
            pad_dv[HALO + r0:HALO + r0 + CONV_ROWS, :] = da * (cg * sg)
        sums_g = [jnp.zeros((8, W), F32)] * 3
        sums_v = [jnp.zeros((8, W), F32)] * 3
        for r0 in range(0, S, CONV_ROWS):
            dug, parts_g = _conv3_grads_at(pad_dg, pad_ug, wg, r0)
            duv, parts_v = _conv3_grads_at(pad_dv, pad_uv, wv, r0)
            dup_ref[0, r0:r0 + CONV_ROWS, :] = dug.astype(BF16)
            dup_ref[1, r0:r0 + CONV_ROWS, :] = duv.astype(BF16)
            sums_g = [a + p for a, p in zip(sums_g, parts_g)]
            sums_v = [a + p for a, p in zip(sums_v, parts_v)]
        dwg = jnp.concatenate([jnp.sum(a, axis=0, keepdims=True) for a in sums_g], axis=0)
        dwv = jnp.concatenate([jnp.sum(a, axis=0, keepdims=True) for a in sums_v], axis=0)

        @pl.when(b == 0)
        def _():
            dwg_ref[...] = dwg
            dwv_ref[...] = dwv

        @pl.when(b != 0)
        def _():
            dwg_ref[...] += dwg
            dwv_ref[...] += dwv

    tap_out = pl.BlockSpec((3, W), lambda p, b: (0, p))
    return pl.pallas_call(
        body, grid=(P, B), name=name,
        in_specs=[pl.BlockSpec((2, S, W), lambda p, b: (0, b, p)),
                  pl.BlockSpec((None, S, W), lambda p, b: (0, b, p)),
                  pl.BlockSpec((3, W), lambda p, b: (0, p)),
                  pl.BlockSpec((3, W), lambda p, b: (0, P + p))],
        out_specs=[pl.BlockSpec((2, S, W), lambda p, b: (0, b, p)), tap_out, tap_out],
        out_shape=[jax.ShapeDtypeStruct((2, T, Fd), BF16),
                   jax.ShapeDtypeStruct((3, Fd), F32), jax.ShapeDtypeStruct((3, Fd), F32)],
        scratch_shapes=[pltpu.VMEM((S + 2 * HALO, W), F32)] * 4,
        compiler_params=_params(("parallel", "arbitrary")),
    )(up3, dact3, taps, taps)


def _final_norm_loss(x, g, target, tm, name):
    T, D = x.shape

    def body(x_ref, g_ref, t_ref, dx_ref, dg_ref, loss_ref):
        xv = x_ref[...]
        rstd = lax.rsqrt(jnp.mean(xv * xv, axis=-1, keepdims=True) + EPS)
        xn = xv * rstd
        err = xn * g_ref[...] - t_ref[...]
        part = 0.5 * jnp.sum(jnp.mean(err * err, axis=-1, keepdims=True), axis=0, keepdims=True)
        dy = err * (1.0 / D)
        a = dy * g_ref[...]
        dx_ref[...] = rstd * (a - xn * jnp.mean(a * xn, axis=-1, keepdims=True))
        dg = jnp.sum(dy * xn, axis=0, keepdims=True)
        lpart = jnp.broadcast_to(part, (1, LANES))

        @pl.when(pl.program_id(0) == 0)
        def _():
            dg_ref[...] = dg
            loss_ref[...] = lpart

        @pl.when(pl.program_id(0) != 0)
        def _():
            dg_ref[...] += dg
            loss_ref[...] += lpart

    row = pl.BlockSpec((tm, D), lambda i: (i, 0))
    return pl.pallas_call(
        body, grid=(T // tm,), name=name,
        in_specs=[row, pl.BlockSpec((1, D), lambda i: (0, 0)), row],
        out_specs=[row, pl.BlockSpec((1, D), lambda i: (0, 0)), pl.BlockSpec((1, LANES), lambda i: (0, 0))],
        out_shape=[jax.ShapeDtypeStruct((T, D), F32), jax.ShapeDtypeStruct((1, D), F32),
                   jax.ShapeDtypeStruct((1, LANES), F32)],
        compiler_params=_params(("arbitrary",)),
    )(x, g, target)


def _row_tile(rows, cols, budget_elems=512 * 1024):
    tr = rows
    while tr * cols > budget_elems and tr % 32 == 0:
        tr //= 2
    return tr


def _prefetch_call(body, grid, in_specs, out_specs, out_shape, name, sem, aliases=None):
    return pl.pallas_call(
        body, name=name, out_shape=out_shape,
        grid_spec=pltpu.PrefetchScalarGridSpec(num_scalar_prefetch=1, grid=grid, in_specs=in_specs,
                                               out_specs=out_specs),
        input_output_aliases=aliases or {},
        compiler_params=_params(sem))


def _cast_into_full(w, layer, colwise, where, name):
    _, K, N = w.shape
    tr = _row_tile(K, N)
    nrb = K // tr
    full_shape = (1, K, 4 * N) if colwise else (1, 4 * K, N)

    def body(where_ref, w_ref, o_ref):
        del where_ref
        o_ref[...] = w_ref[...].astype(BF16)

    if colwise:
        out_map = lambda i, wh: (0, i, wh[0])
    else:
        out_map = lambda i, wh: (0, wh[0] * nrb + i, 0)
    return _prefetch_call(
        body, (nrb,), [pl.BlockSpec((None, tr, N), lambda i, wh: (layer, i, 0))],
        pl.BlockSpec((None, tr, N), out_map), jax.ShapeDtypeStruct(full_shape, BF16), name,
        ("parallel",))(where, w)


def _chip_sum(g3, other, colwise, where, name):
    L, K, N = g3.shape
    hk, hn = (K // 2, N) if colwise else (K, N // 2)
    tr = _row_tile(hk, hn)
    nrb = hk // tr

    def body(where_ref, g_ref, o_ref, s_ref):
        del where_ref
        s_ref[...] = (g_ref[...].astype(F32) + o_ref[...].astype(F32)).astype(BF16)

    if colwise:
        g_map = lambda l, i, wh: (l, wh[1] * nrb + i, 0)
    else:
        g_map = lambda l, i, wh: (l, i, wh[1])
    blk = pl.BlockSpec((None, tr, hn), lambda l, i, wh: (l, i, 0))
    return _prefetch_call(
        body, (L, nrb), [pl.BlockSpec((None, tr, hn), g_map), blk], blk,
        jax.ShapeDtypeStruct((L, hk, hn), BF16), name, ("parallel", "parallel"))(where, g3, other)


def _owner_sum(chip_sum, received, colwise, where, layer, n_layers, prev, name):
    _, hk, hn = chip_sum.shape
    pk, pn = (hk, hn // 4) if colwise else (hk // 4, hn)
    tr = _row_tile(pk, pn)
    nrb = pk // tr
    shard_shape = (n_layers, 2 * pk, pn) if colwise else (n_layers, pk, 2 * pn)

    def body(where_ref, own_ref, rec_ref, *rest):
        del where_ref
        o_ref = rest[-1]
        acc = own_ref[...].astype(F32)
        for j in range(3):
            acc = acc + rec_ref[j].astype(F32)
        o_ref[...] = acc

    if colwise:
        own_map = lambda i, wh: (0, i, wh[0])
        out_map = lambda i, wh: (layer, wh[1] * nrb + i, 0)
    else:
        own_map = lambda i, wh: (0, wh[0] * nrb + i, 0)
        out_map = lambda i, wh: (layer, i, wh[1])
    in_specs = [pl.BlockSpec((None, tr, pn), own_map),
                pl.BlockSpec((3, None, tr, pn), lambda i, wh: (0, 0, i, 0))]
    operands = [where, chip_sum, received]
    if prev is not None:
        in_specs.append(ANY)
        operands.append(prev)
    return _prefetch_call(
        body, (nrb,), in_specs, pl.BlockSpec((None, tr, pn), out_map), jax.ShapeDtypeStruct(shard_shape, F32), name,
        ("parallel",), None if prev is None else {3: 0})(*operands)


def _adamw(w, g, m, v, name):
    R, Cc = w.shape
    tr = _row_tile(R, Cc, 256 * 1024)

    def body(w_ref, g_ref, m_ref, v_ref, d_ref, nm_ref, nv_ref, go_ref):
        gv = g_ref[...]
        go_ref[...] = gv
        nm = ADAM_B1 * m_ref[...] + (1.0 - ADAM_B1) * gv
        nv = ADAM_B2 * v_ref[...] + (1.0 - ADAM_B2) * (gv * gv)
        m_hat = nm / (1.0 - ADAM_B1 ** ADAM_STEP)
        v_hat = nv / (1.0 - ADAM_B2 ** ADAM_STEP)
        d_ref[...] = -ADAM_LR * (m_hat / (jnp.sqrt(v_hat) + ADAM_EPS) + ADAM_WD * w_ref[...])
        nm_ref[...] = nm
        nv_ref[...] = nv

    blk = pl.BlockSpec((tr, Cc), lambda i: (i, 0))
    shp = jax.ShapeDtypeStruct((R, Cc), F32)
    return pl.pallas_call(
        body, grid=(R // tr,), name=name,
        in_specs=[blk] * 4, out_specs=[blk] * 4, out_shape=[shp] * 4,
        compiler_params=_params(("parallel",)),
    )(w, g, m, v)


COL_SHARDED = (True, False, True, False)


def _position():
    x, y, c = lax.axis_index("x"), lax.axis_index("y"), lax.axis_index("c")
    chips = [(1 - x, y), (x, 1 - y), (1 - x, 1 - y)]
    return x, y, c, chips


def _span(index, size, align):
    return pl.ds(pl.multiple_of(index * size, align), size)


def _window(ref, colwise, shard, half, shards=4):
    _, K, N = ref.shape
    rows = cols = slice(None)
    if colwise:
        if half is not None:
            rows = _span(half, K // 2, 16)
        if shard is not None:
            cols = _span(shard, N // shards, LANES)
    else:
        if shard is not None:
            rows = _span(shard, K // shards, 16)
        if half is not None:
            cols = _span(half, N // 2, LANES)
    return ref.at[:, rows, cols]


HBM = pl.BlockSpec(memory_space=pltpu.HBM)
SEMAPHORES = pl.BlockSpec(memory_space=pltpu.SEMAPHORE)


def _gather_start(fulls, colwise, group_sizes, after, name):
    n = len(fulls)
    n_groups = len(group_sizes)

    n_in = n if after is None else n + 1

    def body(*refs):
        ins = refs[:n]
        sems = refs[n_in:n_in + 2 * n_groups]
        x, y, c, chips = _position()
        me = 2 * x + y
        i = 0
        for g, size in enumerate(group_sizes):
            for a in range(size):
                win = _window(ins[i], colwise[i], me, c)
                for j, chip in enumerate(chips):
                    pltpu.make_async_remote_copy(
                        src_ref=win, dst_ref=win, send_sem=sems[2 * g].at[a * 3 + j],
                        recv_sem=sems[2 * g + 1].at[a * 3 + j],
                        device_id=(chip[0], chip[1], c), device_id_type=MESH_ID).start()
                i += 1

    sem_shapes = []
    for size in group_sizes:
        sem_shapes += [pltpu.SemaphoreType.DMA((3 * size,)), pltpu.SemaphoreType.DMA((3 * size,))]
    operands = [pltpu.with_memory_space_constraint(f, pltpu.HBM) for f in fulls]
    in_specs = [HBM] * n
    if after is not None:
        operands.append(after)
        in_specs.append(ANY)
    outs = pl.pallas_call(
        body, name=name,
        in_specs=in_specs, out_specs=[SEMAPHORES] * (2 * n_groups) + [HBM] * n,
        out_shape=sem_shapes + [pltpu.HBM(f.shape, f.dtype) for f in fulls],
        input_output_aliases={i: 2 * n_groups + i for i in range(n)},
        compiler_params=pltpu.CompilerParams(has_side_effects=pltpu.SideEffectType.DATAFLOW_SIDE_EFFECTING),
    )(*operands)
    sems = [(outs[2 * g], outs[2 * g + 1]) for g in range(n_groups)]
    return sems, list(outs[2 * n_groups:])


def _to_sibling(ref, colwise, chip, half, x, y, c, send_sem, recv_sem):
    win = _window(ref, colwise, 2 * chip[0] + chip[1], half)
    return pltpu.make_async_remote_copy(
        src_ref=win, dst_ref=win, send_sem=send_sem, recv_sem=recv_sem,
        device_id=(x, y, 1 - c), device_id_type=MESH_ID)


def _gather_pass(in_flight, colwise, sems, after, name):
    n = len(in_flight)

    def body(*refs):
        ins = refs[:n]
        send_sems, recv_sems = refs[n], refs[n + 1]
        pass_send, pass_recv = refs[-2 - n], refs[-1 - n]
        x, y, c, chips = _position()
        me = 2 * x + y
        for a in range(n):
            for j, chip in enumerate(chips):
                k = a * 3 + j
                pltpu.make_async_remote_copy(
                    src_ref=_window(ins[a], colwise[a], me, c),
                    dst_ref=_window(ins[a], colwise[a], 2 * chip[0] + chip[1], c),
                    send_sem=send_sems.at[k], recv_sem=recv_sems.at[k],
                    device_id=(chip[0], chip[1], c), device_id_type=MESH_ID).wait()
                _to_sibling(ins[a], colwise[a], chip, c, x, y, c, pass_send.at[k], pass_recv.at[k]).start()

    operands = list(in_flight) + list(sems)
    in_specs = [HBM] * n + [SEMAPHORES] * 2
    if after is not None:
        operands.append(after)
        in_specs.append(ANY)
    outs = pl.pallas_call(
        body, name=name,
        in_specs=in_specs, out_specs=[SEMAPHORES] * 2 + [HBM] * n,
        out_shape=[pltpu.SemaphoreType.DMA((3 * n,)), pltpu.SemaphoreType.DMA((3 * n,))]
        + [pltpu.HBM(f.shape, f.dtype) for f in in_flight],
        input_output_aliases={i: 2 + i for i in range(n)},
        compiler_params=pltpu.CompilerParams(has_side_effects=pltpu.SideEffectType.DATAFLOW_SIDE_EFFECTING),
    )(*operands)
    return (outs[0], outs[1]), list(outs[2:])


def _gather_wait(in_flight, colwise, sems, after, name):
    n = len(in_flight)

    def body(*refs):
        ins = refs[:n]
        send_sems, recv_sems = refs[n], refs[n + 1]
        x, y, c, chips = _position()
        for a in range(n):
            for j, chip in enumerate(chips):
                k = a * 3 + j
                _to_sibling(ins[a], colwise[a], chip, c, x, y, c, send_sems.at[k], recv_sems.at[k]).wait_send()
                _to_sibling(ins[a], colwise[a], chip, 1 - c, x, y, c, send_sems.at[k], recv_sems.at[k]).wait_recv()

    operands = list(in_flight) + list(sems)
    in_specs = [HBM] * n + [SEMAPHORES] * 2
    if after is not None:
        operands.append(after)
        in_specs.append(ANY)
    outs = pl.pallas_call(
        body, name=name,
        in_specs=in_specs, out_specs=[HBM] * n,
        out_shape=[pltpu.HBM(f.shape, f.dtype) for f in in_flight],
        input_output_aliases={i: i for i in range(n)},
        compiler_params=pltpu.CompilerParams(has_side_effects=pltpu.SideEffectType.DATAFLOW_SIDE_EFFECTING),
    )(*operands)
    return list(outs)


def _exchange_copy(g_ref, land_ref, colwise, x, y, c, send_sem, recv_sem):
    return pltpu.make_async_remote_copy(
        src_ref=_window(g_ref, colwise, None, 1 - c), dst_ref=land_ref, send_sem=send_sem, recv_sem=recv_sem,
        device_id=(x, y, 1 - c), device_id_type=MESH_ID)


def _exchange_start(grads, colwise, name):
    n = len(grads)
    lands = []
    for g, cw in zip(grads, colwise):
        L, K, N = g.shape
        lands.append(lax.empty((L, K // 2, N) if cw else (L, K, N // 2), g.dtype))

    def body(*refs):
        src, land = refs[:n], refs[n:2 * n]
        send_sems, recv_sems = refs[2 * n], refs[2 * n + 1]
        x, y, c, _ = _position()
        for i in range(n):
            _exchange_copy(src[i], land[i], colwise[i], x, y, c, send_sems.at[i], recv_sems.at[i]).start()

    arrays = list(grads) + lands
    outs = pl.pallas_call(
        body, name=name,
        in_specs=[HBM] * (2 * n), out_specs=[SEMAPHORES] * 2 + [HBM] * (2 * n),
        out_shape=[pltpu.SemaphoreType.DMA((n,)), pltpu.SemaphoreType.DMA((n,))]
        + [pltpu.HBM(a.shape, a.dtype) for a in arrays],
        input_output_aliases={i: 2 + i for i in range(2 * n)},
        compiler_params=pltpu.CompilerParams(has_side_effects=pltpu.SideEffectType.DATAFLOW_SIDE_EFFECTING),
    )(*[pltpu.with_memory_space_constraint(a, pltpu.HBM) for a in arrays])
    return (outs[0], outs[1]), list(outs[2:2 + n]), list(outs[2 + n:])


def _exchange_wait(grads, lands, colwise, sems, after, name):
    n = len(grads)

    def body(*refs):
        src, land = refs[:n], refs[n:2 * n]
        send_sems, recv_sems = refs[2 * n], refs[2 * n + 1]
        x, y, c, _ = _position()
        for i in range(n):
            _exchange_copy(src[i], land[i], colwise[i], x, y, c, send_sems.at[i], recv_sems.at[i]).wait()

    arrays = list(grads) + list(lands)
    operands = arrays + list(sems)
    in_specs = [HBM] * (2 * n) + [SEMAPHORES] * 2
    if after is not None:
        operands.append(after)
        in_specs.append(ANY)
    outs = pl.pallas_call(
        body, name=name,
        in_specs=in_specs, out_specs=[HBM] * (2 * n),
        out_shape=[pltpu.HBM(a.shape, a.dtype) for a in arrays],
        input_output_aliases={i: i for i in range(2 * n)},
        compiler_params=pltpu.CompilerParams(has_side_effects=pltpu.SideEffectType.DATAFLOW_SIDE_EFFECTING),
    )(*operands)
    return list(outs[:n]), list(outs[n:])


def _scatter_copy(src_ref, land_ref, colwise, j, chip, c, send_sem, recv_sem):
    return pltpu.make_async_remote_copy(
        src_ref=_window(src_ref, colwise, 2 * chip[0] + chip[1], None), dst_ref=land_ref.at[j],
        send_sem=send_sem, recv_sem=recv_sem, device_id=(chip[0], chip[1], c), device_id_type=MESH_ID)


def _scatter_start(chip_sums, colwise, name):
    n = len(chip_sums)
    lands = []
    for g, cw in zip(chip_sums, colwise):
        L, hk, hn = g.shape
        lands.append(lax.empty((3, L, hk, hn // 4) if cw else (3, L, hk // 4, hn), g.dtype))

    def body(*refs):
        src, land = refs[:n], refs[n:2 * n]
        send_sems, recv_sems = refs[2 * n], refs[2 * n + 1]
        x, y, c, chips = _position()
        for i in range(n):
            for j, chip in enumerate(chips):
                _scatter_copy(src[i], land[i], colwise[i], j, chip, c, send_sems.at[i * 3 + j],
                              recv_sems.at[i * 3 + j]).start()

    arrays = list(chip_sums) + lands
    outs = pl.pallas_call(
        body, name=name,
        in_specs=[HBM] * (2 * n), out_specs=[SEMAPHORES] * 2 + [HBM] * (2 * n),
        out_shape=[pltpu.SemaphoreType.DMA((3 * n,)), pltpu.SemaphoreType.DMA((3 * n,))]
        + [pltpu.HBM(a.shape, a.dtype) for a in arrays],
        input_output_aliases={i: 2 + i for i in range(2 * n)},
        compiler_params=pltpu.CompilerParams(has_side_effects=pltpu.SideEffectType.DATAFLOW_SIDE_EFFECTING),
    )(*[pltpu.with_memory_space_constraint(a, pltpu.HBM) for a in arrays])
    return (outs[0], outs[1]), list(outs[2:2 + n]), list(outs[2 + n:])


def _scatter_wait(sources, lands, colwise, sems, after, name):
    n = len(sources)

    def body(*refs):
        src, land = refs[:n], refs[n:2 * n]
        send_sems, recv_sems = refs[2 * n], refs[2 * n + 1]
        x, y, c, chips = _position()
        for i in range(n):
            for j, chip in enumerate(chips):
                cp = _scatter_copy(src[i], land[i], colwise[i], j, chip, c, send_sems.at[i * 3 + j],
                                   recv_sems.at[i * 3 + j])
                cp.wait_send()
                cp.wait_recv()

    arrays = list(sources) + list(lands)
    operands = arrays + list(sems)
    in_specs = [HBM] * (2 * n) + [SEMAPHORES] * 2
    if after is not None:
        operands.append(after)
        in_specs.append(ANY)
    outs = pl.pallas_call(
        body, name=name,
        in_specs=in_specs, out_specs=[HBM] * (2 * n),
        out_shape=[pltpu.HBM(a.shape, a.dtype) for a in arrays],
        input_output_aliases={i: i for i in range(2 * n)},
        compiler_params=pltpu.CompilerParams(has_side_effects=pltpu.SideEffectType.DATAFLOW_SIDE_EFFECTING),
    )(*operands)
    return list(outs[:n]), list(outs[n:])


def _share_with_sibling(shards):
    n = len(shards)

    def body(*refs):
        out = refs[n:2 * n]
        send_sems, recv_sems = refs[2 * n:]
        x, y, c, _ = _position()

        def copy(i, half):
            win = _window(out[i], COL_SHARDED[i], None, half)
            return pltpu.make_async_remote_copy(
                src_ref=win, dst_ref=win, send_sem=send_sems.at[i], recv_sem=recv_sems.at[i],
                device_id=(x, y, 1 - c), device_id_type=MESH_ID)

        for i in range(n):
            copy(i, c).start()
        for i in range(n):
            copy(i, 1 - c).wait_recv()
        for i in range(n):
            copy(i, c).wait_send()

    return pl.pallas_call(
        body, name="grad_share_with_sibling",
        in_specs=[ANY] * n, out_specs=[ANY] * n,
        out_shape=[jax.ShapeDtypeStruct(s.shape, s.dtype) for s in shards],
        input_output_aliases={i: i for i in range(n)},
        scratch_shapes=[pltpu.SemaphoreType.DMA((n,)), pltpu.SemaphoreType.DMA((n,))],
    )(*shards)


def _all_reduce_small(pack, name):
    R, Cc = pack.shape
    n_dev = 8

    def body(p_ref, o_ref, buf, send_sems, recv_sems):
        x, y, c, _ = _position()
        me = 4 * x + 2 * y + c
        buf[me] = p_ref[...]

        def peer(k):
            px = 1 - x if k & 4 else x
            py = 1 - y if k & 2 else y
            pc = 1 - c if k & 1 else c
            return px, py, pc

        def copy(k, incoming):
            px, py, pc = peer(k)
            slot = (4 * px + 2 * py + pc) if incoming else me
            return pltpu.make_async_remote_copy(
                src_ref=p_ref, dst_ref=buf.at[slot], send_sem=send_sems.at[k], recv_sem=recv_sems.at[k],
                device_id=(px, py, pc), device_id_type=MESH_ID)

        for k in range(1, n_dev):
            copy(k, False).start()
        for k in range(1, n_dev):
            copy(k, True).wait_recv()
        for k in range(1, n_dev):
            copy(k, False).wait_send()
        acc = buf[0]
        for j in range(1, n_dev):
            acc = acc + buf[j]
        o_ref[...] = acc

    vmem = pl.BlockSpec(memory_space=pltpu.VMEM)
    return pl.pallas_call(
        body, name=name,
        in_specs=[vmem], out_specs=vmem, out_shape=jax.ShapeDtypeStruct((R, Cc), F32),
        scratch_shapes=[pltpu.VMEM((n_dev, R, Cc), F32), pltpu.SemaphoreType.DMA((n_dev,)),
                        pltpu.SemaphoreType.DMA((n_dev,))],
    )(pack)


def _local_forward_backward(x2, target2, S, pass_on, fetch, reduce_begin, reduce_commit, layers, final_g, tm=512):
    T, D = x2.shape
    C = D // 2
    n_heads = C // GROUP
    n_layers = len(layers)
    weights = {}
    saved = []
    xc = x2
    for li, lw in enumerate(layers):
        if li == 0:
            pass_on(0, None)
            weights.update(fetch(0, None))
        h1, qkv3, cv3 = _norm_proj(xc, lw["norm1"], weights[li, "w_in"], 0, ((3, C, F32), (3, C, BF16)), tm,
                                   min(C, 512), f"l{li}_norm_in_proj")
        if li == 0:
            pass_on(1, h1)
        o, lse, mix = _attn_fwd(qkv3, lw["attn_g"], 2, S, n_heads, f"l{li}_attn_fwd")
        if li == 0:
            weights.update(fetch(1, o))
            pass_on(2, o)
            pass_on(3, o)
        mix = _mix_conv_fwd(cv3, lw["taps"], lw["conv_g"], mix, S, f"l{li}_mix_conv_fwd")
        x_mid = _proj_residual(mix, weights[li, "w_out"], 0, xc, tm, f"l{li}_out_proj")
        if li == 0:
            weights.update(fetch(2, x_mid))
        Fd = weights[li, "ffn_up"].shape[2] // 2
        h2, up3 = _norm_proj(x_mid, lw["norm2"], weights[li, "ffn_up"], 0, ((2, Fd, BF16),), tm // 2, 256,
                             f"l{li}_norm_ffn_up")
        if li == 0:
            weights.update(fetch(3, up3))
        act = _ffn_act_fwd(up3, lw["ffn_taps"], S, f"l{li}_ffn_act_fwd")
        if li + 1 < n_layers:
            pass_on(li + 4, act)
        x_out = _proj_residual(act.reshape(1, T, Fd), weights[li, "ffn_down"], 0, x_mid, tm, f"l{li}_ffn_down")
        if li + 1 < n_layers:
            weights.update(fetch(li + 4, x_out))
        saved.append(dict(x_in=xc, h1=h1, qkv3=qkv3, cv3=cv3, o=o, lse=lse, mix=mix, x_mid=x_mid, h2=h2, up3=up3,
                          act=act))
        xc = x_out

    dx, d_final_g, loss_part = _final_norm_loss(xc, final_g, target2, tm, "final_norm_loss")

    small = [None] * n_layers
    started = None
    for li in reversed(range(n_layers)):
        lw, sv = layers[li], saved[li]
        w_in, w_out, ffn_up, ffn_down = (weights[li, n] for n in ("w_in", "w_out", "ffn_up", "ffn_down"))
        dxb, dact3 = _grad_through_weight(dx, ffn_down, 0, 1, Fd, BF16, tm, 256, f"l{li}_d_act", started)
        Fd = ffn_down.shape[1]
        d_ffn_down = _weight_grad(sv["act"].reshape(1, T, Fd), dxb.reshape(1, T, D), Fd // 2, D, 1024,
                                  0, 1, None, f"l{li}_d_ffn_down")
        dup3, d_taps_g, d_taps_v = _ffn_act_bwd(sv["up3"], dact3, lw["ffn_taps"], S, f"l{li}_ffn_act_bwd")
        d_ffn_up = _weight_grad(sv["h2"].reshape(1, T, D), dup3, D, Fd, 1024, 0, 1, None,
                                f"l{li}_d_ffn_up")
        if li == 0:
            early = reduce_begin({(li, "ffn_down"): d_ffn_down, (li, "ffn_up"): d_ffn_up})
        dx_mid, d_norm2 = _grad_through_proj_norm(dup3, ffn_up, 0, sv["x_mid"], lw["norm2"], dx, tm // 2,
                                                  f"l{li}_d_norm2")
        started = reduce_commit(early, dx_mid) if li == 0 else None
        dxmb, dmix3 = _grad_through_weight(dx_mid, w_out, 0, 2, C, F32, tm, min(C, 512), f"l{li}_d_mix", started)
        d_w_out = _weight_grad(sv["mix"], dxmb.reshape(1, T, D), C, D, 1024, 0, 1, None, f"l{li}_d_w_out")
        dproj, d_attn_g = _attn_bwd(sv["qkv3"], sv["o"], sv["lse"], dmix3, lw["attn_g"], 6, S, n_heads,
                                    f"l{li}_attn_bwd")
        dproj, d_taps, d_conv_g = _mix_conv_bwd(sv["cv3"], dmix3, lw["taps"], lw["conv_g"], dproj, S,
                                                f"l{li}_mix_conv_bwd")
        d_w_in = _weight_grad(sv["h1"].reshape(1, T, D), dproj, D, C, 1024, 0, 1, None, f"l{li}_d_w_in")
        late = {(li, "w_out"): d_w_out, (li, "w_in"): d_w_in}
        if li > 0:
            late.update({(li, "ffn_down"): d_ffn_down, (li, "ffn_up"): d_ffn_up})
        late = reduce_begin(late)
        dx, d_norm1 = _grad_through_proj_norm(dproj, w_in, 0, sv["x_in"], lw["norm1"], dx_mid, tm,
                                              f"l{li}_d_norm1")
        started = reduce_commit(late, dx)
        small[li] = dict(norm1=d_norm1, taps=d_taps, attn_g=d_attn_g, conv_g=d_conv_g, norm2=d_norm2,
                         ffn_taps=jnp.concatenate([d_taps_g, d_taps_v], axis=1))
    return loss_part, dx, small, d_final_g


SMALL_ORDER = ("norm1", "attn_g", "conv_g", "norm2", "taps", "ffn_taps")


def _pack_small(small, d_final_g):
    parts = [small[li][k].reshape(-1) for li in range(len(small)) for k in SMALL_ORDER] + [d_final_g.reshape(-1)]
    return jnp.concatenate(parts).reshape(-1, LANES)


def _unpack_small(pack, small, d_final_g):
    flat = pack.reshape(-1)
    out, pos = [dict() for _ in small], 0
    for li in range(len(small)):
        for k in SMALL_ORDER:
            n = small[li][k].size
            out[li][k] = flat[pos:pos + n].reshape(small[li][k].shape)
            pos += n
    return out, flat[pos:pos + d_final_g.size]


def kernel(x, norm1_g, w_in, mix_conv_w, attn_out_g, conv_out_g, w_out, norm2_g, ffn_up, ffn_conv_w, ffn_down, final_norm_g, loss_target, m_norm1_g, m_w_in, m_mix_conv_w, m_attn_out_g, m_conv_out_g, m_w_out, m_norm2_g, m_ffn_up, m_ffn_conv_w, m_ffn_down, m_final_norm_g, v_norm1_g, v_w_in, v_mix_conv_w, v_attn_out_g, v_conv_out_g, v_w_out, v_norm2_g, v_ffn_up, v_ffn_conv_w, v_ffn_down, v_final_norm_g):
    Bl, S, D = x.shape
    L = w_in.shape[0]
    T = Bl * S
    shard = 2 * lax.axis_index("x") + lax.axis_index("y")
    where = jnp.stack([shard, lax.axis_index("c")]).astype(jnp.int32)
    big_names = ("w_in", "w_out", "ffn_up", "ffn_down")

    taps_w, ftaps_w = mix_conv_w.shape[2], ffn_conv_w.shape[2]
    taps_full = jnp.zeros((L, 3, 4 * taps_w), F32)
    taps_full = lax.dynamic_update_slice(taps_full, mix_conv_w, (0, 0, shard * taps_w))
    ftaps_full = jnp.zeros((L, 3, 4 * ftaps_w), F32)
    ftaps_full = lax.dynamic_update_slice(ftaps_full, ffn_conv_w, (0, 0, shard * ftaps_w))
    tap_pack = jnp.concatenate([taps_full.reshape(-1), ftaps_full.reshape(-1)]).reshape(-1, LANES)
    tap_pack = _all_reduce_small(tap_pack * 0.5, "all_gather_taps")
    n_taps = taps_full.size
    taps_full = tap_pack.reshape(-1)[:n_taps].reshape(taps_full.shape)
    ftaps_full = tap_pack.reshape(-1)[n_taps:].reshape(ftaps_full.shape)

    big_shards = dict(zip(big_names, (w_in, w_out, ffn_up, ffn_down)))
    col_of = dict(zip(big_names, COL_SHARDED))
    groups = [[(0, n)] for n in big_names] + [[(l, n) for n in big_names] for l in range(1, L)]
    sems, in_flight = [], {}
    all_started = tap_pack
    for first, last in ((0, 1), (1, len(groups))):
        keys = [k for g in groups[first:last] for k in g]
        new_sems, arrays = _gather_start(
            [_cast_into_full(big_shards[n], l, col_of[n], where, f"cast_{n}_{l}") for l, n in keys],
            [col_of[n] for _, n in keys], [len(g) for g in groups[first:last]], all_started,
            f"gather_start_{first}")
        sems += new_sems
        in_flight.update(zip(keys, arrays))
        all_started = arrays[-1]

    def pass_on(g, after):
        after = all_started if g == 0 else after
        sems[g], arrays = _gather_pass([in_flight[k] for k in groups[g]], [col_of[n] for _, n in groups[g]],
                                       sems[g], after, f"gather_pass_{g}")
        in_flight.update(zip(groups[g], arrays))

    def fetch(g, after):
        done = _gather_wait([in_flight[k] for k in groups[g]], [col_of[n] for _, n in groups[g]], sems[g], after,
                            f"gather_wait_{g}")
        return dict(zip(groups[g], done))

    pending = []

    begun = []

    def reduce_begin(grads):
        g = len(begun)
        keys = list(grads)
        cols = [col_of[n] for _, n in keys]
        begun.append((g, keys, cols) + _exchange_start([grads[k] for k in keys], cols, f"exchange_start_{g}"))
        return begun[-1]

    def reduce_commit(handle, after):
        g, keys, cols, ex_sems, mine, lands = handle
        mine, others = _exchange_wait(mine, lands, cols, ex_sems, after, f"exchange_wait_{g}")
        chip_sums = [_chip_sum(m, o, cw, where, f"chip_sum_{k[1]}_{k[0]}")
                     for k, m, o, cw in zip(keys, mine, others, cols)]
        pending.append((keys, cols) + _scatter_start(chip_sums, cols, f"scatter_start_{g}"))
        return pending[-1][3][0]

    layers = [dict(norm1=norm1_g[l:l + 1], taps=taps_full[l], attn_g=attn_out_g[l:l + 1],
                   conv_g=conv_out_g[l:l + 1], norm2=norm2_g[l:l + 1], ffn_taps=ftaps_full[l]) for l in range(L)]

    loss_part, dx, small, d_final_g = _local_forward_backward(
        x.reshape(T, D), loss_target.reshape(T, D), S, pass_on, fetch, reduce_begin, reduce_commit, layers,
        final_norm_g.reshape(1, D))
    loss = lax.psum(loss_part[0, 0], ("x", "y", "c"))

    reduced = dict.fromkeys(big_names)
    last_started = pending[-1][3][0]
    for g, (keys, cols, rs_sems, sources, lands) in enumerate(pending):
        after = last_started if g + 1 < len(pending) else None
        sources, lands = _scatter_wait(sources, lands, cols, rs_sems, after, f"scatter_wait_{g}")
        for (l, n), cw, src, land in zip(keys, cols, sources, lands):
            reduced[n] = _owner_sum(src, land, cw, where, l, L, reduced[n], f"owner_sum_{n}_{l}")
    g_big = _share_with_sibling([reduced[n] for n in big_names])

    pack = _all_reduce_small(_pack_small(small, d_final_g), "all_reduce_small_grads")
    g_small, g_final = _unpack_small(pack, small, d_final_g)

    def stacked(key):
        return jnp.stack([g_small[l][key].reshape(g_small[l][key].shape[-2:] if key.endswith("taps") else (-1,))
                          for l in range(L)])

    g_norm1, g_attn, g_conv, g_norm2 = stacked("norm1"), stacked("attn_g"), stacked("conv_g"), stacked("norm2")
    g_taps = lax.dynamic_slice(stacked("taps"), (0, 0, shard * taps_w), (L, 3, taps_w))
    g_ftaps = lax.dynamic_slice(stacked("ffn_taps"), (0, 0, shard * ftaps_w), (L, 3, ftaps_w))

    grads_out = dict(norm1_g=g_norm1, w_in=g_big[0], mix_conv_w=g_taps, attn_out_g=g_attn, conv_out_g=g_conv,
                     w_out=g_big[1], norm2_g=g_norm2, ffn_up=g_big[2], ffn_conv_w=g_ftaps, ffn_down=g_big[3],
                     final_norm_g=g_final)
    weights = dict(norm1_g=norm1_g, w_in=w_in, mix_conv_w=mix_conv_w, attn_out_g=attn_out_g, conv_out_g=conv_out_g,
                   w_out=w_out, norm2_g=norm2_g, ffn_up=ffn_up, ffn_conv_w=ffn_conv_w, ffn_down=ffn_down,
                   final_norm_g=final_norm_g)
    ms = dict(norm1_g=m_norm1_g, w_in=m_w_in, mix_conv_w=m_mix_conv_w, attn_out_g=m_attn_out_g,
              conv_out_g=m_conv_out_g, w_out=m_w_out, norm2_g=m_norm2_g, ffn_up=m_ffn_up, ffn_conv_w=m_ffn_conv_w,
              ffn_down=m_ffn_down, final_norm_g=m_final_norm_g)
    vs = dict(norm1_g=v_norm1_g, w_in=v_w_in, mix_conv_w=v_mix_conv_w, attn_out_g=v_attn_out_g,
              conv_out_g=v_conv_out_g, w_out=v_w_out, norm2_g=v_norm2_g, ffn_up=v_ffn_up, ffn_conv_w=v_ffn_conv_w,
              ffn_down=v_ffn_down, final_norm_g=v_final_norm_g)
    names = list(weights)
    small_names = [n for n in names if n not in big_names]
    delta, new_m, new_v = {}, {}, {}
    for n in big_names:
        shp = weights[n].shape
        two_d = (shp[0] * shp[1], shp[2])
        d_, m_, v_, g_ = _adamw(weights[n].reshape(two_d), grads_out[n].reshape(two_d), ms[n].reshape(two_d),
                                vs[n].reshape(two_d), f"adamw_{n}")
        delta[n], new_m[n], new_v[n], grads_out[n] = (a.reshape(shp) for a in (d_, m_, v_, g_))

    def packed(tree):
        return jnp.concatenate([tree[n].reshape(-1) for n in small_names]).reshape(-1, LANES)

    d_, m_, v_, _ = _adamw(packed(weights), packed(grads_out), packed(ms), packed(vs), "adamw_small")
    pos = 0
    for n in small_names:
        size, shp = weights[n].size, weights[n].shape
        delta[n] = d_.reshape(-1)[pos:pos + size].reshape(shp)
        new_m[n] = m_.reshape(-1)[pos:pos + size].reshape(shp)
        new_v[n] = v_.reshape(-1)[pos:pos + size].reshape(shp)
        pos += size

    return (loss, dx.reshape(Bl, S, D), *[grads_out[n] for n in names], *[delta[n] for n in names],
            *[new_m[n] for n in names], *[new_v[n] for n in names])
```

```python
import functools
import math

import jax
import jax.numpy as jnp
from jax import lax
from jax.experimental import pallas as pl
from jax.experimental.pallas import tpu as pltpu

F32 = jnp.float32
BF16 = jnp.bfloat16
EPS = 1e-6
GROUP = 64
LANES = 128
BAND = 128
DILATIONS = (1, 4, 16)
NEG = -1e30
MIB = 1024 * 1024
MESH_ID = pl.DeviceIdType.MESH

ADAM_LR = 0.001
ADAM_B1 = 0.9
ADAM_B2 = 0.999
ADAM_EPS = 1e-08
ADAM_WD = 0.01
ADAM_STEP = 10


ANY = pl.BlockSpec(memory_space=pl.ANY)


def _params(sem=None, vmem_mb=48):
    return pltpu.CompilerParams(dimension_semantics=sem, vmem_limit_bytes=vmem_mb * MIB)


def _nt(a, b):
    return lax.dot_general(a, b, (((1,), (1,)), ((), ())), preferred_element_type=F32)


def _tn(a, b):
    return lax.dot_general(a, b, (((0,), (0,)), ((), ())), preferred_element_type=F32)


def _seg_sum(x, is_a):
    s_a = jnp.sum(jnp.where(is_a, x, 0.0), axis=-1, keepdims=True)
    s_b = jnp.sum(jnp.where(is_a, 0.0, x), axis=-1, keepdims=True)
    return jnp.where(is_a, s_a, s_b)


def _lane_is_a():
    return lax.broadcasted_iota(jnp.int32, (1, LANES), 1) < GROUP


def _norm_proj(x, g, w3, layer, groups, tm, chunk, name):
    T, D = x.shape
    N = w3.shape[2]
    assert sum(p * c for p, c, _ in groups) == N and T % tm == 0

    def body(x_ref, g_ref, w_ref, h_ref, *out_refs):
        xv = x_ref[...]
        rstd = lax.rsqrt(jnp.mean(xv * xv, axis=-1, keepdims=True) + EPS)
        h = ((xv * rstd) * g_ref[...]).astype(BF16)
        h_ref[...] = h
        col = 0
        for (pieces, width, dtype), o_ref in zip(groups, out_refs):
            for p in range(pieces):
                for c0 in range(0, width, chunk):
                    acc = jnp.dot(h, w_ref[:, col + c0:col + c0 + chunk], preferred_element_type=F32)
                    o_ref[p, :, c0:c0 + chunk] = acc.astype(dtype)
                col += width

    out_shape = [jax.ShapeDtypeStruct((T, D), BF16)]
    out_specs = [pl.BlockSpec((tm, D), lambda i: (i, 0))]
    for pieces, width, dtype in groups:
        assert width % chunk == 0
        out_shape.append(jax.ShapeDtypeStruct((pieces, T, width), dtype))
        out_specs.append(pl.BlockSpec((pieces, tm, width), lambda i: (0, i, 0)))
    return pl.pallas_call(
        body, grid=(T // tm,), name=name,
        in_specs=[pl.BlockSpec((tm, D), lambda i: (i, 0)),
                  pl.BlockSpec((1, D), lambda i: (0, 0)),
                  pl.BlockSpec((None, D, N), lambda i: (layer, 0, 0))],
        out_specs=out_specs, out_shape=out_shape,
        compiler_params=_params(("parallel",), 56),
    )(x, g, w3)


def _proj_residual(pieces3, w3, layer, x, tm, name, after=None):
    P, T, C = pieces3.shape
    D = w3.shape[2]

    def body(a_ref, w_ref, x_ref, *rest):
        o_ref = rest[-1]
        acc = x_ref[...]
        for p in range(P):
            acc = acc + jnp.dot(a_ref[p], w_ref[p * C:(p + 1) * C, :], preferred_element_type=F32)
        o_ref[...] = acc

    in_specs = [pl.BlockSpec((P, tm, C), lambda i: (0, i, 0)),
                pl.BlockSpec((None, P * C, D), lambda i: (layer, 0, 0)),
                pl.BlockSpec((tm, D), lambda i: (i, 0))]
    operands = [pieces3, w3, x]
    if after is not None:
        in_specs.append(ANY)
        operands.append(after)
    return pl.pallas_call(
        body, grid=(T // tm,), name=name,
        in_specs=in_specs,
        out_specs=pl.BlockSpec((tm, D), lambda i: (i, 0)),
        out_shape=jax.ShapeDtypeStruct((T, D), F32),
        compiler_params=_params(("parallel",)),
    )(*operands)


def _grad_through_weight(dy, w3, layer, pieces, width, out_dtype, tm, chunk, name, after=None):
    T, D = dy.shape

    def body(dy_ref, w_ref, *rest):
        dyb_ref, o_ref = rest[-2:]
        dyb = dy_ref[...].astype(BF16)
        dyb_ref[...] = dyb
        for p in range(pieces):
            for c0 in range(0, width, chunk):
                r0 = p * width + c0
                o_ref[p, :, c0:c0 + chunk] = _nt(dyb, w_ref[r0:r0 + chunk, :]).astype(out_dtype)

    in_specs = [pl.BlockSpec((tm, D), lambda i: (i, 0)),
                pl.BlockSpec((None, pieces * width, D), lambda i: (layer, 0, 0))]
    operands = [dy, w3]
    if after is not None:
        in_specs.append(ANY)
        operands.append(after)
    return pl.pallas_call(
        body, grid=(T // tm,), name=name,
        in_specs=in_specs,
        out_specs=[pl.BlockSpec((tm, D), lambda i: (i, 0)),
                   pl.BlockSpec((pieces, tm, width), lambda i: (0, i, 0))],
        out_shape=[jax.ShapeDtypeStruct((T, D), BF16),
                   jax.ShapeDtypeStruct((pieces, T, width), out_dtype)],
        compiler_params=_params(("parallel",)),
    )(*operands)


def _grad_through_proj_norm(dp3, w3, layer, x, g, dx_in, tm, name):
    P, T, C = dp3.shape
    D = w3.shape[1]

    def body(dp_ref, w_ref, x_ref, g_ref, dxin_ref, dx_ref, dg_ref):
        dh = _nt(dp_ref[0], w_ref[:, 0:C])
        for p in range(1, P):
            dh = dh + _nt(dp_ref[p], w_ref[:, p * C:(p + 1) * C])
        xv = x_ref[...]
        rstd = lax.rsqrt(jnp.mean(xv * xv, axis=-1, keepdims=True) + EPS)
        xn = xv * rstd
        a = dh * g_ref[...]
        dx_ref[...] = dxin_ref[...] + rstd * (a - xn * jnp.mean(a * xn, axis=-1, keepdims=True))
        part = jnp.sum(dh * xn, axis=0, keepdims=True)

        @pl.when(pl.program_id(0) == 0)
        def _():
            dg_ref[...] = part

        @pl.when(pl.program_id(0) != 0)
        def _():
            dg_ref[...] += part

    return pl.pallas_call(
        body, grid=(T // tm,), name=name,
        in_specs=[pl.BlockSpec((P, tm, C), lambda i: (0, i, 0)),
                  pl.BlockSpec((None, D, P * C), lambda i: (layer, 0, 0)),
                  pl.BlockSpec((tm, D), lambda i: (i, 0)),
                  pl.BlockSpec((1, D), lambda i: (0, 0)),
                  pl.BlockSpec((tm, D), lambda i: (i, 0))],
        out_specs=[pl.BlockSpec((tm, D), lambda i: (i, 0)),
                   pl.BlockSpec((1, D), lambda i: (0, 0))],
        out_shape=[jax.ShapeDtypeStruct((T, D), F32), jax.ShapeDtypeStruct((1, D), F32)],
        compiler_params=_params(("arbitrary",), 56),
    )(dp3, w3, x, g, dx_in)


def _weight_grad(a3, g3, ta, tg, tt, layer, n_layers, prev, name):
    PA, T, CA = a3.shape
    PG, _, CG = g3.shape
    na, ng, nt = CA // ta, CG // tg, T // tt
    assert CA % ta == 0 and CG % tg == 0 and T % tt == 0

    def body(a_ref, g_ref, *rest):
        o_ref, acc_ref = rest[-2:]
        t = pl.program_id(2)
        part = _tn(a_ref[...], g_ref[...])

        @pl.when(t == 0)
        def _():
            acc_ref[...] = part

        @pl.when(t != 0)
        def _():
            acc_ref[...] += part

        @pl.when(t == nt - 1)
        def _():
            o_ref[...] = acc_ref[...].astype(o_ref.dtype)

    in_specs = [pl.BlockSpec((None, tt, ta), lambda i, j, t: (i // na, t, i % na)),
                pl.BlockSpec((None, tt, tg), lambda i, j, t: (j // ng, t, j % ng))]
    operands = [a3, g3]
    if prev is not None:
        in_specs.append(pl.BlockSpec(memory_space=pl.ANY))
        operands.append(prev)
    return pl.pallas_call(
        body, grid=(PA * na, PG * ng, nt), name=name,
        in_specs=in_specs,
        out_specs=pl.BlockSpec((None, ta, tg), lambda i, j, t: (layer, i, j)),
        out_shape=jax.ShapeDtypeStruct((n_layers, PA * CA, PG * CG), BF16),
        scratch_shapes=[pltpu.VMEM((ta, tg), F32)],
        input_output_aliases={} if prev is None else {2: 0},
        compiler_params=_params(("parallel", "parallel", "arbitrary"), 56),
    )(*operands)


def _bias_tables(bm_ref, lone_ref, pair, n_heads, S):
    ii = lax.broadcasted_iota(jnp.int32, (BAND, 2 * BAND), 0)
    jj = lax.broadcasted_iota(jnp.int32, (BAND, 2 * BAND), 1)
    dist = BAND + ii - jj
    valid = (dist >= 0) & (dist <= BAND)
    distf = dist.astype(F32)
    for hh in range(2):
        head = (2 * pair + hh + 1).astype(F32)
        slope = jnp.exp(jnp.full((1, 1), -8.0 / n_heads * math.log(2.0), F32) * head)
        for bi, d in enumerate(DILATIONS):
            table = jnp.where(valid, -(slope * d) * distf, NEG)
            bm_ref[bi, hh * BAND:(hh + 1) * BAND, :] = table
            if S // (BAND * d) == 1:
                lone_ref[bi, hh * BAND:(hh + 1) * BAND, :] = table[:, BAND:2 * BAND]


def _stack_heads(x, is_a):
    zero = jnp.zeros_like(x)
    return jnp.concatenate([jnp.where(is_a, x, zero), jnp.where(is_a, zero, x)], axis=0)


def _unstack_heads(x2, is_a):
    return jnp.where(is_a, x2[0:BAND], x2[BAND:2 * BAND])


def _gather_residues(dst_ref, src, d, S, convert):
    L = S // d
    for r in range(d):
        rows = pl.ds(r, L, stride=d) if d > 1 else slice(None)
        dst_ref[r * L:(r + 1) * L, :] = convert(src(rows))


def _block_rows(t, d, S):
    nb = S // (BAND * d)
    n = t % nb
    has_prev = jnp.minimum(n, 1)
    cur = pl.ds(pl.multiple_of(t * BAND, BAND), BAND)
    prev = pl.ds(pl.multiple_of((t - has_prev) * BAND, BAND), BAND)
    return cur, prev, has_prev


def _first_block_penalty(has_prev):
    jrow = lax.broadcasted_iota(jnp.int32, (1, 2 * BAND), 1)
    pen = jnp.where(has_prev == 0, NEG, 0.0).astype(F32)
    return jnp.where(jrow < BAND, pen, 0.0)


def _attn_fwd(qkv3, gain, mix_shape_pieces, S, n_heads, name):
    _, T, C = qkv3.shape
    B, P = T // S, C // LANES
    NBLK = S // BAND
    scale = GROUP ** -0.5
    nbr = len(DILATIONS)
    RC = 256

    def body(qkv_ref, g_ref, o_ref, lse_ref, an_ref, qs, ks, vs, op, mp, lp, ob, mb, lb, bm, bml):
        pair = pl.program_id(1)
        is_a = _lane_is_a()
        _bias_tables(bm, bml, pair, n_heads, S)

        for bi, d in enumerate(DILATIONS):
            nb = S // (BAND * d)
            _gather_residues(qs, lambda rows: qkv_ref.at[0][rows, :], d, S, lambda v: (v * scale).astype(BF16))
            _gather_residues(ks, lambda rows: qkv_ref.at[1][rows, :], d, S, lambda v: v.astype(BF16))
            _gather_residues(vs, lambda rows: qkv_ref.at[2][rows, :], d, S, lambda v: v.astype(BF16))
            o_dst, m_dst, l_dst = (ob.at[bi], mb.at[bi], lb.at[bi]) if d == 1 else (op, mp, lp)

            def block(t, carry, bi=bi, d=d, nb=nb, o_dst=o_dst, m_dst=m_dst, l_dst=l_dst):
                cur, prev, has_prev = _block_rows(t, d, S)
                q2 = _stack_heads(qs[cur, :], is_a)
                if nb == 1:
                    kc, vc = ks[cur, :], vs[cur, :]
                    s = _nt(q2, kc) + bml[bi]
                else:
                    kc = jnp.concatenate([ks[prev, :], ks[cur, :]], axis=0)
                    vc = jnp.concatenate([vs[prev, :], vs[cur, :]], axis=0)
                    s = _nt(q2, kc) + bm[bi] + _first_block_penalty(has_prev)
                m = jnp.max(s, axis=-1, keepdims=True)
                e = jnp.exp(s - m)
                l = jnp.sum(e, axis=-1, keepdims=True)
                pv = jnp.dot(e.astype(BF16), vc, preferred_element_type=F32)
                o_dst[cur, :] = _unstack_heads(pv, is_a)
                m_dst[cur, :] = _unstack_heads(m, is_a)
                l_dst[cur, :] = _unstack_heads(l, is_a)
                return carry

            lax.fori_loop(0, NBLK, block, 0, unroll=8)
            if d > 1:
                L = S // d
                for r in range(d):
                    rows = pl.ds(r, L, stride=d)
                    ob.at[bi][rows, :] = op[r * L:(r + 1) * L, :]
                    mb.at[bi][rows, :] = mp[r * L:(r + 1) * L, :]
                    lb.at[bi][rows, :] = lp[r * L:(r + 1) * L, :]

        def finish(ci, carry):
            rs = pl.ds(pl.multiple_of(ci * RC, RC), RC)
            ms = [mb[bi, rs, :] for bi in range(nbr)]
            mmax = functools.reduce(jnp.maximum, ms)
            ws = [jnp.exp(m - mmax) for m in ms]
            num = sum(ob[bi, rs, :] * ws[bi] for bi in range(nbr))
            den = sum(lb[bi, rs, :] * ws[bi] for bi in range(nbr))
            o = num / den
            o_ref[rs, :] = o
            lse_ref[rs, :] = mmax + jnp.log(den)
            rstd = lax.rsqrt(_seg_sum(o * o, is_a) * (1.0 / GROUP) + EPS)
            an_ref[rs, :] = ((o * rstd) * g_ref[...]).astype(BF16)
            return carry

        lax.fori_loop(0, S // RC, finish, 0)

    seq = pl.BlockSpec((S, LANES), lambda b, p: (b, p))
    return pl.pallas_call(
        body, grid=(B, P), name=name,
        in_specs=[pl.BlockSpec((3, S, LANES), lambda b, p: (0, b, p)),
                  pl.BlockSpec((1, LANES), lambda b, p: (0, p))],
        out_specs=[seq, seq, pl.BlockSpec((None, S, LANES), lambda b, p: (0, b, p))],
        out_shape=[jax.ShapeDtypeStruct((T, C), F32), jax.ShapeDtypeStruct((T, C), F32),
                   jax.ShapeDtypeStruct((mix_shape_pieces, T, C), BF16)],
        scratch_shapes=[pltpu.VMEM((S, LANES), BF16)] * 3 + [pltpu.VMEM((S, LANES), F32)] * 3
        + [pltpu.VMEM((nbr, S, LANES), F32)] * 3
        + [pltpu.VMEM((nbr, 2 * BAND, 2 * BAND), F32), pltpu.VMEM((nbr, 2 * BAND, BAND), F32)],
        compiler_params=_params(("parallel", "parallel")),
    )(qkv3, gain)


def _attn_bwd(qkv3, o, lse, dmix3, gain, dproj_pieces, S, n_heads, name):
    _, T, C = qkv3.shape
    B, P = T // S, C // LANES
    NBLK = S // BAND
    scale = GROUP ** -0.5
    nbr = len(DILATIONS)
    RC = 256

    def body(qkv_ref, o_ref, lse_ref, dn_ref, g_ref, dqkv_ref, dg_ref,
             do_n, dd_n, qs, ks, vs, dos, lses, dds, dqp, dkp, dvp, dqn, dkn, dvn, bm, bml):
        pair = pl.program_id(0)
        b = pl.program_id(1)
        is_a = _lane_is_a()
        _bias_tables(bm, bml, pair, n_heads, S)

        def prologue(ci, dg_acc):
            rs = pl.ds(pl.multiple_of(ci * RC, RC), RC)
            ov = o_ref[rs, :]
            dn = dn_ref[rs, :]
            rstd = lax.rsqrt(_seg_sum(ov * ov, is_a) * (1.0 / GROUP) + EPS)
            on = ov * rstd
            a = dn * g_ref[...]
            do = rstd * (a - on * (_seg_sum(a * on, is_a) * (1.0 / GROUP)))
            do_n[rs, :] = do
            dd_n[rs, :] = _seg_sum(do * ov, is_a)
            zero = jnp.zeros((RC, LANES), F32)
            dqn[rs, :] = zero
            dkn[rs, :] = zero
            dvn[rs, :] = zero
            return dg_acc + jnp.sum(dn * on, axis=0, keepdims=True)

        dg_part = lax.fori_loop(0, S // RC, prologue, jnp.zeros((1, LANES), F32))

        @pl.when(b == 0)
        def _():
            dg_ref[...] = dg_part

        @pl.when(b != 0)
        def _():
            dg_ref[...] += dg_part

        for bi, d in enumerate(DILATIONS):
            nb = S // (BAND * d)
            L = S // d
            _gather_residues(qs, lambda rows: qkv_ref.at[0][rows, :], d, S, lambda v: (v * scale).astype(BF16))
            _gather_residues(ks, lambda rows: qkv_ref.at[1][rows, :], d, S, lambda v: v.astype(BF16))
            _gather_residues(vs, lambda rows: qkv_ref.at[2][rows, :], d, S, lambda v: v.astype(BF16))
            _gather_residues(dos, lambda rows: do_n[rows, :], d, S, lambda v: v.astype(BF16))
            if d == 1:
                lse_src, dd_src, dq_dst, dk_dst, dv_dst = lse_ref, dd_n, dqn, dkn, dvn
            else:
                _gather_residues(lses, lambda rows: lse_ref[rows, :], d, S, lambda v: v)
                _gather_residues(dds, lambda rows: dd_n[rows, :], d, S, lambda v: v)
                dkp[...] = jnp.zeros((S, LANES), F32)
                dvp[...] = jnp.zeros((S, LANES), F32)
                lse_src, dd_src, dq_dst, dk_dst, dv_dst = lses, dds, dqp, dkp, dvp

            def block(t, carry, bi=bi, d=d, nb=nb, lse_src=lse_src, dd_src=dd_src, dq_dst=dq_dst, dk_dst=dk_dst,
                      dv_dst=dv_dst):
                cur, prev, has_prev = _block_rows(t, d, S)
                q2 = _stack_heads(qs[cur, :], is_a)
                do2 = _stack_heads(dos[cur, :], is_a)
                lse_t = lse_src[cur, :]
                dd_t = dd_src[cur, :]
                lse2 = jnp.concatenate([lse_t[:, 0:1], lse_t[:, GROUP:GROUP + 1]], axis=0)
                dd2 = jnp.concatenate([dd_t[:, 0:1], dd_t[:, GROUP:GROUP + 1]], axis=0)
                if nb == 1:
                    kc, vc = ks[cur, :], vs[cur, :]
                    s = _nt(q2, kc) + bml[bi]
                else:
                    kc = jnp.concatenate([ks[prev, :], ks[cur, :]], axis=0)
                    vc = jnp.concatenate([vs[prev, :], vs[cur, :]], axis=0)
                    s = _nt(q2, kc) + bm[bi] + _first_block_penalty(has_prev)
                p = jnp.exp(s - lse2)
                ds = (p * (_nt(do2, vc) - dd2)).astype(BF16)
                dq = _unstack_heads(jnp.dot(ds, kc, preferred_element_type=F32), is_a)
                dk = _tn(ds, q2)
                dv = _tn(p.astype(BF16), do2)
                dq_dst[cur, :] = dq
                if nb == 1:
                    dk_dst[cur, :] += dk
                    dv_dst[cur, :] += dv
                else:
                    dk_dst[prev, :] += dk[0:BAND, :]
                    dv_dst[prev, :] += dv[0:BAND, :]
                    dk_dst[cur, :] += dk[BAND:2 * BAND, :]
                    dv_dst[cur, :] += dv[BAND:2 * BAND, :]
                return carry

            lax.fori_loop(0, NBLK, block, 0, unroll=8)
            if d > 1:
                for r in range(d):
                    rows = pl.ds(r, L, stride=d)
                    dqn[rows, :] += dqp[r * L:(r + 1) * L, :]
                    dkn[rows, :] += dkp[r * L:(r + 1) * L, :]
                    dvn[rows, :] += dvp[r * L:(r + 1) * L, :]

        dqkv_ref[0] = (dqn[...] * scale).astype(BF16)
        dqkv_ref[1] = dkn[...].astype(BF16)
        dqkv_ref[2] = dvn[...].astype(BF16)

    seq = pl.BlockSpec((S, LANES), lambda p, b: (b, p))
    f32_seq = pltpu.VMEM((S, LANES), F32)
    bf_seq = pltpu.VMEM((S, LANES), BF16)
    return pl.pallas_call(
        body, grid=(P, B), name=name,
        in_specs=[pl.BlockSpec((3, S, LANES), lambda p, b: (0, b, p)), seq, seq,
                  pl.BlockSpec((None, S, LANES), lambda p, b: (0, b, p)),
                  pl.BlockSpec((1, LANES), lambda p, b: (0, p))],
        out_specs=[pl.BlockSpec((3, S, LANES), lambda p, b: (0, b, p)),
                   pl.BlockSpec((1, LANES), lambda p, b: (0, p))],
        out_shape=[jax.ShapeDtypeStruct((dproj_pieces, T, C), BF16), jax.ShapeDtypeStruct((1, C), F32)],
        scratch_shapes=[f32_seq, f32_seq, bf_seq, bf_seq, bf_seq, bf_seq, f32_seq, f32_seq,
                        f32_seq, f32_seq, f32_seq, f32_seq, f32_seq, f32_seq,
                        pltpu.VMEM((nbr, 2 * BAND, 2 * BAND), F32), pltpu.VMEM((nbr, 2 * BAND, BAND), F32)],
        compiler_params=_params(("parallel", "arbitrary")),
    )(qkv3, o, lse, dmix3, gain)


def _delay(x, k, row):
    return jnp.where(row >= k, pltpu.roll(x, k, 0), 0.0)


def _advance(x, k, row, S):
    return jnp.where(row < S - k, pltpu.roll(x, S - k, 0), 0.0)


def _conv3(x, w, row):
    return (w[0:1, :] * _delay(x, 2, row) + w[1:2, :] * _delay(x, 1, row)) + w[2:3, :] * x


HALO = 8


CONV_ROWS = 128
FFN_LANES = 128


def _zero_halo(pad_ref, S):
    zeros = jnp.zeros((HALO, pad_ref.shape[1]), pad_ref.dtype)
    pad_ref[0:HALO, :] = zeros
    pad_ref[HALO + S:2 * HALO + S, :] = zeros


def _window_at(pad_ref, r0, shift):
    return pad_ref[HALO + r0 + shift:HALO + r0 + shift + CONV_ROWS, :]


def _conv3_at(pad_ref, w, r0):
    return ((w[0:1, :] * _window_at(pad_ref, r0, -2) + w[1:2, :] * _window_at(pad_ref, r0, -1))
            + w[2:3, :] * _window_at(pad_ref, r0, 0))


def _conv3_grads_at(dz_ref, x_ref, w, r0):
    dz, dz1, dz2 = (_window_at(dz_ref, r0, k) for k in range(3))
    x = _window_at(x_ref, r0, 0)
    dx = (w[2:3, :] * dz + w[1:2, :] * dz1) + w[0:1, :] * dz2
    parts = [jnp.sum((d * x).reshape(CONV_ROWS // 8, 8, x.shape[1]), axis=0) for d in (dz2, dz1, dz)]
    return dx, parts


def _conv3_grads(dz, x, w, row, S):
    dz1 = _advance(dz, 1, row, S)
    dz2 = _advance(dz, 2, row, S)
    dx = (w[2:3, :] * dz + w[1:2, :] * dz1) + w[0:1, :] * dz2
    dw = jnp.concatenate([jnp.sum(dz2 * x, axis=0, keepdims=True),
                          jnp.sum(dz1 * x, axis=0, keepdims=True),
                          jnp.sum(dz * x, axis=0, keepdims=True)], axis=0)
    return dx, dw


def _mix_conv_fwd(cv3, taps, gain, mix, S, name, after=None):
    _, T, C = cv3.shape
    B, P = T // S, C // LANES

    def body(cv_ref, w_ref, g_ref, mix_hbm, *rest):
        y_ref, pad_c = rest[-2:]
        del mix_hbm
        is_a = _lane_is_a()
        _zero_halo(pad_c, S)
        pad_c[HALO:HALO + S, :] = cv_ref[1].astype(F32) * cv_ref[2].astype(F32)
        w = w_ref[...]
        for r0 in range(0, S, CONV_ROWS):
            y = cv_ref[0, r0:r0 + CONV_ROWS, :].astype(F32) * _conv3_at(pad_c, w, r0)
            rstd = lax.rsqrt(_seg_sum(y * y, is_a) * (1.0 / GROUP) + EPS)
            y_ref[r0:r0 + CONV_ROWS, :] = ((y * rstd) * g_ref[...]).astype(BF16)

    in_specs = [pl.BlockSpec((3, S, LANES), lambda b, p: (0, b, p)),
                pl.BlockSpec((3, LANES), lambda b, p: (0, p)),
                pl.BlockSpec((1, LANES), lambda b, p: (0, p)),
                ANY]
    operands = [cv3, taps, gain, mix]
    if after is not None:
        in_specs.append(ANY)
        operands.append(after)
    return pl.pallas_call(
        body, grid=(B, P), name=name,
        in_specs=in_specs,
        out_specs=pl.BlockSpec((None, S, LANES), lambda b, p: (1, b, p)),
        out_shape=jax.ShapeDtypeStruct(mix.shape, mix.dtype),
        scratch_shapes=[pltpu.VMEM((S + 2 * HALO, LANES), F32)],
        input_output_aliases={3: 0},
        compiler_params=_params(("parallel", "parallel")),
    )(*operands)


def _mix_conv_bwd(cv3, dmix3, taps, gain, dproj, S, name):
    _, T, C = cv3.shape
    B, P = T // S, C // LANES

    def body(cv_ref, dn_ref, w_ref, g_ref, dproj_hbm, dcv_ref, dw_ref, dg_ref):
        del dproj_hbm
        b = pl.program_id(1)
        row = lax.broadcasted_iota(jnp.int32, (S, 1), 0)
        is_a = _lane_is_a()
        w = w_ref[...]
        gb = cv_ref[0].astype(F32)
        gc = cv_ref[1].astype(F32)
        u = cv_ref[2].astype(F32)
        c = gc * u
        z = _conv3(c, w, row)
        y = gb * z
        rstd = lax.rsqrt(_seg_sum(y * y, is_a) * (1.0 / GROUP) + EPS)
        yn = y * rstd
        dn = dn_ref[...]
        a = dn * g_ref[...]
        dy = rstd * (a - yn * (_seg_sum(a * yn, is_a) * (1.0 / GROUP)))
        dg = jnp.sum(dn * yn, axis=0, keepdims=True)
        dc, dw = _conv3_grads(dy * gb, c, w, row, S)
        dcv_ref[0] = (dy * z).astype(BF16)
        dcv_ref[1] = (dc * u).astype(BF16)
        dcv_ref[2] = (dc * gc).astype(BF16)

        @pl.when(b == 0)
        def _():
            dw_ref[...] = dw
            dg_ref[...] = dg

        @pl.when(b != 0)
        def _():
            dw_ref[...] += dw
            dg_ref[...] += dg

    return pl.pallas_call(
        body, grid=(P, B), name=name,
        in_specs=[pl.BlockSpec((3, S, LANES), lambda p, b: (0, b, p)),
                  pl.BlockSpec((None, S, LANES), lambda p, b: (1, b, p)),
                  pl.BlockSpec((3, LANES), lambda p, b: (0, p)),
                  pl.BlockSpec((1, LANES), lambda p, b: (0, p)),
                  pl.BlockSpec(memory_space=pl.ANY)],
        out_specs=[pl.BlockSpec((3, S, LANES), lambda p, b: (1, b, p)),
                   pl.BlockSpec((3, LANES), lambda p, b: (0, p)),
                   pl.BlockSpec((1, LANES), lambda p, b: (0, p))],
        out_shape=[jax.ShapeDtypeStruct(dproj.shape, dproj.dtype),
                   jax.ShapeDtypeStruct((3, C), F32), jax.ShapeDtypeStruct((1, C), F32)],
        input_output_aliases={4: 0},
        compiler_params=_params(("parallel", "arbitrary")),
    )(cv3, dmix3, taps, gain, dproj)


def _sigmoid(x):
    return 0.5 * jnp.tanh(0.5 * x) + 0.5


def _ffn_act_fwd(up3, taps, S, name):
    _, T, Fd = up3.shape
    W = FFN_LANES
    B, P = T // S, Fd // W

    def body(up_ref, wg_ref, wv_ref, act_ref, pad_g, pad_v):
        _zero_halo(pad_g, S)
        _zero_halo(pad_v, S)
        pad_g[HALO:HALO + S, :] = up_ref[0].astype(F32)
        pad_v[HALO:HALO + S, :] = up_ref[1].astype(F32)
        wg = wg_ref[...]
        wv = wv_ref[...]
        for r0 in range(0, S, CONV_ROWS):
            cg = _conv3_at(pad_g, wg, r0)
            cv = _conv3_at(pad_v, wv, r0)
            act_ref[r0:r0 + CONV_ROWS, :] = ((cg * _sigmoid(cg)) * cv).astype(BF16)

    return pl.pallas_call(
        body, grid=(B, P), name=name,
        in_specs=[pl.BlockSpec((2, S, W), lambda b, p: (0, b, p)),
                  pl.BlockSpec((3, W), lambda b, p: (0, p)),
                  pl.BlockSpec((3, W), lambda b, p: (0, P + p))],
        out_specs=pl.BlockSpec((S, W), lambda b, p: (b, p)),
        out_shape=jax.ShapeDtypeStruct((T, Fd), BF16),
        scratch_shapes=[pltpu.VMEM((S + 2 * HALO, W), F32)] * 2,
        compiler_params=_params(("parallel", "parallel")),
    )(up3, taps, taps)


def _ffn_act_bwd(up3, dact3, taps, S, name):
    _, T, Fd = up3.shape
    W = FFN_LANES
    B, P = T // S, Fd // W

    def body(up_ref, da_ref, wg_ref, wv_ref, dup_ref, dwg_ref, dwv_ref, pad_ug, pad_uv, pad_dg, pad_dv):
        b = pl.program_id(1)
        for pad in (pad_ug, pad_uv, pad_dg, pad_dv):
            _zero_halo(pad, S)
        pad_ug[HALO:HALO + S, :] = up_ref[0].astype(F32)
        pad_uv[HALO:HALO + S, :] = up_ref[1].astype(F32)
        wg = wg_ref[...]
        wv = wv_ref[...]
        for r0 in range(0, S, CONV_ROWS):
            cg = _conv3_at(pad_ug, wg, r0)
            cv = _conv3_at(pad_uv, wv, r0)
            sg = _sigmoid(cg)
            da = da_ref[r0:r0 + CONV_ROWS, :].astype(F32)
            pad_dg[HALO + r0:HALO + r0 + CONV_ROWS, :] = (da * cv) * (sg * (1.0 + cg * (1.0 - sg)))
            pad_dv[HALO + r0:HALO + r0 + CONV_ROWS, :] = da * (cg * sg)
        sums_g = [jnp.zeros((8, W), F32)] * 3
        sums_v = [jnp.zeros((8, W), F32)] * 3
        for r0 in range(0, S, CONV_ROWS):
            dug, parts_g = _conv3_grads_at(pad_dg, pad_ug, wg, r0)
            duv, parts_v = _conv3_grads_at(pad_dv, pad_uv, wv, r0)
            dup_ref[0, r0:r0 + CONV_ROWS, :] = dug.astype(BF16)
            dup_ref[1, r0:r0 + CONV_ROWS, :] = duv.astype(BF16)
            sums_g = [a + p for a, p in zip(sums_g, parts_g)]
            sums_v = [a + p for a, p in zip(sums_v, parts_v)]
        dwg = jnp.concatenate([jnp.sum(a, axis=0, keepdims=True) for a in sums_g], axis=0)
        dwv = jnp.concatenate([jnp.sum(a, axis=0, keepdims=True) for a in sums_v], axis=0)

        @pl.when(b == 0)
        def _():
            dwg_ref[...] = dwg
            dwv_ref[...] = dwv

        @pl.when(b != 0)
        def _():
            dwg_ref[...] += dwg
            dwv_ref[...] += dwv

    tap_out = pl.BlockSpec((3, W), lambda p, b: (0, p))
    return pl.pallas_call(
        body, grid=(P, B), name=name,
        in_specs=[pl.BlockSpec((2, S, W), lambda p, b: (0, b, p)),
                  pl.BlockSpec((None, S, W), lambda p, b: (0, b, p)),
                  pl.BlockSpec((3, W), lambda p, b: (0, p)),
                  pl.BlockSpec((3, W), lambda p, b: (0, P + p))],
        out_specs=[pl.BlockSpec((2, S, W), lambda p, b: (0, b, p)), tap_out, tap_out],
        out_shape=[jax.ShapeDtypeStruct((2, T, Fd), BF16),
                   jax.ShapeDtypeStruct((3, Fd), F32), jax.ShapeDtypeStruct((3, Fd), F32)],
        scratch_shapes=[pltpu.VMEM((S + 2 * HALO, W), F32)] * 4,
        compiler_params=_params(("parallel", "arbitrary")),
    )(up3, dact3, taps, taps)


def _final_norm_loss(x, g, target, tm, name):
    T, D = x.shape

    def body(x_ref, g_ref, t_ref, dx_ref, dg_ref, loss_ref):
        xv = x_ref[...]
        rstd = lax.rsqrt(jnp.mean(xv * xv, axis=-1, keepdims=True) + EPS)
        xn = xv * rstd
        err = xn * g_ref[...] - t_ref[...]
        part = 0.5 * jnp.sum(jnp.mean(err * err, axis=-1, keepdims=True), axis=0, keepdims=True)
        dy = err * (1.0 / D)
        a = dy * g_ref[...]
        dx_ref[...] = rstd * (a - xn * jnp.mean(a * xn, axis=-1, keepdims=True))
        dg = jnp.sum(dy * xn, axis=0, keepdims=True)
        lpart = jnp.broadcast_to(part, (1, LANES))

        @pl.when(pl.program_id(0) == 0)
        def _():
            dg_ref[...] = dg
            loss_ref[...] = lpart

        @pl.when(pl.program_id(0) != 0)
        def _():
            dg_ref[...] += dg
            loss_ref[...] += lpart

    row = pl.BlockSpec((tm, D), lambda i: (i, 0))
    return pl.pallas_call(
        body, grid=(T // tm,), name=name,
        in_specs=[row, pl.BlockSpec((1, D), lambda i: (0, 0)), row],
        out_specs=[row, pl.BlockSpec((1, D), lambda i: (0, 0)), pl.BlockSpec((1, LANES), lambda i: (0, 0))],
        out_shape=[jax.ShapeDtypeStruct((T, D), F32), jax.ShapeDtypeStruct((1, D), F32),
                   jax.ShapeDtypeStruct((1, LANES), F32)],
        compiler_params=_params(("arbitrary",)),
    )(x, g, target)


def _row_tile(rows, cols, budget_elems=512 * 1024):
    tr = rows
    while tr * cols > budget_elems and tr % 32 == 0:
        tr //= 2
    return tr


def _prefetch_call(body, grid, in_specs, out_specs, out_shape, name, sem, aliases=None):
    return pl.pallas_call(
        body, name=name, out_shape=out_shape,
        grid_spec=pltpu.PrefetchScalarGridSpec(num_scalar_prefetch=1, grid=grid, in_specs=in_specs,
                                               out_specs=out_specs),
        input_output_aliases=aliases or {},
        compiler_params=_params(sem))


def _cast_into_full(w, layer, colwise, where, name):
    _, K, N = w.shape
    tr = _row_tile(K, N)
    nrb = K // tr
    full_shape = (1, K, 4 * N) if colwise else (1, 4 * K, N)

    def body(where_ref, w_ref, o_ref):
        del where_ref
        o_ref[...] = w_ref[...].astype(BF16)

    if colwise:
        out_map = lambda i, wh: (0, i, wh[0])
    else:
        out_map = lambda i, wh: (0, wh[0] * nrb + i, 0)
    return _prefetch_call(
        body, (nrb,), [pl.BlockSpec((None, tr, N), lambda i, wh: (layer, i, 0))],
        pl.BlockSpec((None, tr, N), out_map), jax.ShapeDtypeStruct(full_shape, BF16), name,
        ("parallel",))(where, w)


def _chip_sum(g3, other, colwise, where, name):
    L, K, N = g3.shape
    hk, hn = (K // 2, N) if colwise else (K, N // 2)
    tr = _row_tile(hk, hn)
    nrb = hk // tr

    def body(where_ref, g_ref, o_ref, s_ref):
        del where_ref
        s_ref[...] = (g_ref[...].astype(F32) + o_ref[...].astype(F32)).astype(BF16)

    if colwise:
        g_map = lambda l, i, wh: (l, wh[1] * nrb + i, 0)
    else:
        g_map = lambda l, i, wh: (l, i, wh[1])
    blk = pl.BlockSpec((None, tr, hn), lambda l, i, wh: (l, i, 0))
    return _prefetch_call(
        body, (L, nrb), [pl.BlockSpec((None, tr, hn), g_map), blk], blk,
        jax.ShapeDtypeStruct((L, hk, hn), BF16), name, ("parallel", "parallel"))(where, g3, other)


def _owner_sum(chip_sum, received, colwise, where, layer, n_layers, prev, name):
    _, hk, hn = chip_sum.shape
    pk, pn = (hk, hn // 4) if colwise else (hk // 4, hn)
    tr = _row_tile(pk, pn)
    nrb = pk // tr
    shard_shape = (n_layers, 2 * pk, pn) if colwise else (n_layers, pk, 2 * pn)

    def body(where_ref, own_ref, rec_ref, *rest):
        del where_ref
        o_ref = rest[-1]
        acc = own_ref[...].astype(F32)
        for j in range(3):
            acc = acc + rec_ref[j].astype(F32)
        o_ref[...] = acc

    if colwise:
        own_map = lambda i, wh: (0, i, wh[0])
        out_map = lambda i, wh: (layer, wh[1] * nrb + i, 0)
    else:
        own_map = lambda i, wh: (0, wh[0] * nrb + i, 0)
        out_map = lambda i, wh: (layer, i, wh[1])
    in_specs = [pl.BlockSpec((None, tr, pn), own_map),
                pl.BlockSpec((3, None, tr, pn), lambda i, wh: (0, 0, i, 0))]
    operands = [where, chip_sum, received]
    if prev is not None:
        in_specs.append(ANY)
        operands.append(prev)
    return _prefetch_call(
        body, (nrb,), in_specs, pl.BlockSpec((None, tr, pn), out_map), jax.ShapeDtypeStruct(shard_shape, F32), name,
        ("parallel",), None if prev is None else {3: 0})(*operands)


def _adamw(w, g, m, v, name):
    R, Cc = w.shape
    tr = _row_tile(R, Cc, 256 * 1024)

    def body(w_ref, g_ref, m_ref, v_ref, d_ref, nm_ref, nv_ref, go_ref):
        gv = g_ref[...]
        go_ref[...] = gv
        nm = ADAM_B1 * m_ref[...] + (1.0 - ADAM_B1) * gv
        nv = ADAM_B2 * v_ref[...] + (1.0 - ADAM_B2) * (gv * gv)
        m_hat = nm / (1.0 - ADAM_B1 ** ADAM_STEP)
        v_hat = nv / (1.0 - ADAM_B2 ** ADAM_STEP)
        d_ref[...] = -ADAM_LR * (m_hat / (jnp.sqrt(v_hat) + ADAM_EPS) + ADAM_WD * w_ref[...])
        nm_ref[...] = nm
        nv_ref[...] = nv

    blk = pl.BlockSpec((tr, Cc), lambda i: (i, 0))
    shp = jax.ShapeDtypeStruct((R, Cc), F32)
    return pl.pallas_call(
        body, grid=(R // tr,), name=name,
        in_specs=[blk] * 4, out_specs=[blk] * 4, out_shape=[shp] * 4,
        compiler_params=_params(("parallel",)),
    )(w, g, m, v)


COL_SHARDED = (True, False, True, False)


def _position():
    x, y, c = lax.axis_index("x"), lax.axis_index("y"), lax.axis_index("c")
    chips = [(1 - x, y), (x, 1 - y), (1 - x, 1 - y)]
    return x, y, c, chips


def _span(index, size, align):
    return pl.ds(pl.multiple_of(index * size, align), size)


def _window(ref, colwise, shard, half, shards=4):
    _, K, N = ref.shape
    rows = cols = slice(None)
    if colwise:
        if half is not None:
            rows = _span(half, K // 2, 16)
        if shard is not None:
            cols = _span(shard, N // shards, LANES)
    else:
        if shard is not None:
            rows = _span(shard, K // shards, 16)
        if half is not None:
            cols = _span(half, N // 2, LANES)
    return ref.at[:, rows, cols]


HBM = pl.BlockSpec(memory_space=pltpu.HBM)
SEMAPHORES = pl.BlockSpec(memory_space=pltpu.SEMAPHORE)


def _gather_start(fulls, colwise, group_sizes, after, name):
    n = len(fulls)
    n_groups = len(group_sizes)

    n_in = n if after is None else n + 1

    def body(*refs):
        ins = refs[:n]
        sems = refs[n_in:n_in + 2 * n_groups]
        x, y, c, chips = _position()
        me = 2 * x + y
        i = 0
        for g, size in enumerate(group_sizes):
            for a in range(size):
                win = _window(ins[i], colwise[i], me, c)
                for j, chip in enumerate(chips):
                    pltpu.make_async_remote_copy(
                        src_ref=win, dst_ref=win, send_sem=sems[2 * g].at[a * 3 + j],
                        recv_sem=sems[2 * g + 1].at[a * 3 + j],
                        device_id=(chip[0], chip[1], c), device_id_type=MESH_ID).start()
                i += 1

    sem_shapes = []
    for size in group_sizes:
        sem_shapes += [pltpu.SemaphoreType.DMA((3 * size,)), pltpu.SemaphoreType.DMA((3 * size,))]
    operands = [pltpu.with_memory_space_constraint(f, pltpu.HBM) for f in fulls]
    in_specs = [HBM] * n
    if after is not None:
        operands.append(after)
        in_specs.append(ANY)
    outs = pl.pallas_call(
        body, name=name,
        in_specs=in_specs, out_specs=[SEMAPHORES] * (2 * n_groups) + [HBM] * n,
        out_shape=sem_shapes + [pltpu.HBM(f.shape, f.dtype) for f in fulls],
        input_output_aliases={i: 2 * n_groups + i for i in range(n)},
        compiler_params=pltpu.CompilerParams(has_side_effects=pltpu.SideEffectType.DATAFLOW_SIDE_EFFECTING),
    )(*operands)
    sems = [(outs[2 * g], outs[2 * g + 1]) for g in range(n_groups)]
    return sems, list(outs[2 * n_groups:])


def _to_sibling(ref, colwise, chip, half, x, y, c, send_sem, recv_sem):
    win = _window(ref, colwise, 2 * chip[0] + chip[1], half)
    return pltpu.make_async_remote_copy(
        src_ref=win, dst_ref=win, send_sem=send_sem, recv_sem=recv_sem,
        device_id=(x, y, 1 - c), device_id_type=MESH_ID)


def _gather_pass(in_flight, colwise, sems, after, name):
    n = len(in_flight)

    def body(*refs):
        ins = refs[:n]
        send_sems, recv_sems = refs[n], refs[n + 1]
        pass_send, pass_recv = refs[-2 - n], refs[-1 - n]
        x, y, c, chips = _position()
        me = 2 * x + y
        for a in range(n):
            for j, chip in enumerate(chips):
                k = a * 3 + j
                pltpu.make_async_remote_copy(
                    src_ref=_window(ins[a], colwise[a], me, c),
                    dst_ref=_window(ins[a], colwise[a], 2 * chip[0] + chip[1], c),
                    send_sem=send_sems.at[k], recv_sem=recv_sems.at[k],
                    device_id=(chip[0], chip[1], c), device_id_type=MESH_ID).wait()
                _to_sibling(ins[a], colwise[a], chip, c, x, y, c, pass_send.at[k], pass_recv.at[k]).start()

    operands = list(in_flight) + list(sems)
    in_specs = [HBM] * n + [SEMAPHORES] * 2
    if after is not None:
        operands.append(after)
        in_specs.append(ANY)
    outs = pl.pallas_call(
        body, name=name,
        in_specs=in_specs, out_specs=[SEMAPHORES] * 2 + [HBM] * n,
        out_shape=[pltpu.SemaphoreType.DMA((3 * n,)), pltpu.SemaphoreType.DMA((3 * n,))]
        + [pltpu.HBM(f.shape, f.dtype) for f in in_flight],
        input_output_aliases={i: 2 + i for i in range(n)},
        compiler_params=pltpu.CompilerParams(has_side_effects=pltpu.SideEffectType.DATAFLOW_SIDE_EFFECTING),
    )(*operands)
    return (outs[0], outs[1]), list(outs[2:])


def _gather_wait(in_flight, colwise, sems, after, name):
    n = len(in_flight)

    def body(*refs):
        ins = refs[:n]
        send_sems, recv_sems = refs[n], refs[n + 1]
        x, y, c, chips = _position()
        for a in range(n):
            for j, chip in enumerate(chips):
                k = a * 3 + j
                _to_sibling(ins[a], colwise[a], chip, c, x, y, c, send_sems.at[k], recv_sems.at[k]).wait_send()
                _to_sibling(ins[a], colwise[a], chip, 1 - c, x, y, c, send_sems.at[k], recv_sems.at[k]).wait_recv()

    operands = list(in_flight) + list(sems)
    in_specs = [HBM] * n + [SEMAPHORES] * 2
    if after is not None:
        operands.append(after)
        in_specs.append(ANY)
    outs = pl.pallas_call(
        body, name=name,
        in_specs=in_specs, out_specs=[HBM] * n,
        out_shape=[pltpu.HBM(f.shape, f.dtype) for f in in_flight],
        input_output_aliases={i: i for i in range(n)},
        compiler_params=pltpu.CompilerParams(has_side_effects=pltpu.SideEffectType.DATAFLOW_SIDE_EFFECTING),
    )(*operands)
    return list(outs)


def _exchange_copy(g_ref, land_ref, colwise, x, y, c, send_sem, recv_sem):
    return pltpu.make_async_remote_copy(
        src_ref=_window(g_ref, colwise, None, 1 - c), dst_ref=land_ref, send_sem=send_sem, recv_sem=recv_sem,
        device_id=(x, y, 1 - c), device_id_type=MESH_ID)


def _exchange_start(grads, colwise, name):
    n = len(grads)
    lands = []
    for g, cw in zip(grads, colwise):
        L, K, N = g.shape
        lands.append(lax.empty((L, K // 2, N) if cw else (L, K, N // 2), g.dtype))

    def body(*refs):
        src, land = refs[:n], refs[n:2 * n]
        send_sems, recv_sems = refs[2 * n], refs[2 * n + 1]
        x, y, c, _ = _position()
        for i in range(n):
            _exchange_copy(src[i], land[i], colwise[i], x, y, c, send_sems.at[i], recv_sems.at[i]).start()

    arrays = list(grads) + lands
    outs = pl.pallas_call(
        body, name=name,
        in_specs=[HBM] * (2 * n), out_specs=[SEMAPHORES] * 2 + [HBM] * (2 * n),
        out_shape=[pltpu.SemaphoreType.DMA((n,)), pltpu.SemaphoreType.DMA((n,))]
        + [pltpu.HBM(a.shape, a.dtype) for a in arrays],
        input_output_aliases={i: 2 + i for i in range(2 * n)},
        compiler_params=pltpu.CompilerParams(has_side_effects=pltpu.SideEffectType.DATAFLOW_SIDE_EFFECTING),
    )(*[pltpu.with_memory_space_constraint(a, pltpu.HBM) for a in arrays])
    return (outs[0], outs[1]), list(outs[2:2 + n]), list(outs[2 + n:])


def _exchange_wait(grads, lands, colwise, sems, after, name):
    n = len(grads)

    def body(*refs):
        src, land = refs[:n], refs[n:2 * n]
        send_sems, recv_sems = refs[2 * n], refs[2 * n + 1]
        x, y, c, _ = _position()
        for i in range(n):
            _exchange_copy(src[i], land[i], colwise[i], x, y, c, send_sems.at[i], recv_sems.at[i]).wait()

    arrays = list(grads) + list(lands)
    operands = arrays + list(sems)
    in_specs = [HBM] * (2 * n) + [SEMAPHORES] * 2
    if after is not None:
        operands.append(after)
        in_specs.append(ANY)
    outs = pl.pallas_call(
        body, name=name,
        in_specs=in_specs, out_specs=[HBM] * (2 * n),
        out_shape=[pltpu.HBM(a.shape, a.dtype) for a in arrays],
        input_output_aliases={i: i for i in range(2 * n)},
        compiler_params=pltpu.CompilerParams(has_side_effects=pltpu.SideEffectType.DATAFLOW_SIDE_EFFECTING),
    )(*operands)
    return list(outs[:n]), list(outs[n:])


def _scatter_copy(src_ref, land_ref, colwise, j, chip, c, send_sem, recv_sem):
    return pltpu.make_async_remote_copy(
        src_ref=_window(src_ref, colwise, 2 * chip[0] + chip[1], None), dst_ref=land_ref.at[j],
        send_sem=send_sem, recv_sem=recv_sem, device_id=(chip[0], chip[1], c), device_id_type=MESH_ID)


def _scatter_start(chip_sums, colwise, name):
    n = len(chip_sums)
    lands = []
    for g, cw in zip(chip_sums, colwise):
        L, hk, hn = g.shape
        lands.append(lax.empty((3, L, hk, hn // 4) if cw else (3, L, hk // 4, hn), g.dtype))

    def body(*refs):
        src, land = refs[:n], refs[n:2 * n]
        send_sems, recv_sems = refs[2 * n], refs[2 * n + 1]
        x, y, c, chips = _position()
        for i in range(n):
            for j, chip in enumerate(chips):
                _scatter_copy(src[i], land[i], colwise[i], j, chip, c, send_sems.at[i * 3 + j],
                              recv_sems.at[i * 3 + j]).start()

    arrays = list(chip_sums) + lands
    outs = pl.pallas_call(
        body, name=name,
        in_specs=[HBM] * (2 * n), out_specs=[SEMAPHORES] * 2 + [HBM] * (2 * n),
        out_shape=[pltpu.SemaphoreType.DMA((3 * n,)), pltpu.SemaphoreType.DMA((3 * n,))]
        + [pltpu.HBM(a.shape, a.dtype) for a in arrays],
        input_output_aliases={i: 2 + i for i in range(2 * n)},
        compiler_params=pltpu.CompilerParams(has_side_effects=pltpu.SideEffectType.DATAFLOW_SIDE_EFFECTING),
    )(*[pltpu.with_memory_space_constraint(a, pltpu.HBM) for a in arrays])
    return (outs[0], outs[1]), list(outs[2:2 + n]), list(outs[2 + n:])


def _scatter_wait(sources, lands, colwise, sems, after, name):
    n = len(sources)

    def body(*refs):
        src, land = refs[:n], refs[n:2 * n]
        send_sems, recv_sems = refs[2 * n], refs[2 * n + 1]
        x, y, c, chips = _position()
        for i in range(n):
            for j, chip in enumerate(chips):
                cp = _scatter_copy(src[i], land[i], colwise[i], j, chip, c, send_sems.at[i * 3 + j],
                                   recv_sems.at[i * 3 + j])
                cp.wait_send()
                cp.wait_recv()

    arrays = list(sources) + list(lands)
    operands = arrays + list(sems)
    in_specs = [HBM] * (2 * n) + [SEMAPHORES] * 2
    if after is not None:
        operands.append(after)
        in_specs.append(ANY)
    outs = pl.pallas_call(
        body, name=name,
        in_specs=in_specs, out_specs=[HBM] * (2 * n),
        out_shape=[pltpu.HBM(a.shape, a.dtype) for a in arrays],
        input_output_aliases={i: i for i in range(2 * n)},
        compiler_params=pltpu.CompilerParams(has_side_effects=pltpu.SideEffectType.DATAFLOW_SIDE_EFFECTING),
    )(*operands)
    return list(outs[:n]), list(outs[n:])


def _share_with_sibling(shards, colwise, name):
    n = len(shards)

    def body(*refs):
        out = refs[n:2 * n]
        send_sems, recv_sems = refs[2 * n:]
        x, y, c, _ = _position()

        def copy(i, half):
            win = _window(out[i], colwise[i], None, half)
            return pltpu.make_async_remote_copy(
                src_ref=win, dst_ref=win, send_sem=send_sems.at[i], recv_sem=recv_sems.at[i],
                device_id=(x, y, 1 - c), device_id_type=MESH_ID)

        for i in range(n):
            copy(i, c).start()
        for i in range(n):
            copy(i, 1 - c).wait_recv()
        for i in range(n):
            copy(i, c).wait_send()

    return pl.pallas_call(
        body, name=name,
        in_specs=[ANY] * n, out_specs=[ANY] * n,
        out_shape=[jax.ShapeDtypeStruct(s.shape, s.dtype) for s in shards],
        input_output_aliases={i: i for i in range(n)},
        scratch_shapes=[pltpu.SemaphoreType.DMA((n,)), pltpu.SemaphoreType.DMA((n,))],
    )(*shards)


def _all_reduce_small(pack, name):
    R, Cc = pack.shape
    n_dev = 8

    def body(p_ref, o_ref, buf, send_sems, recv_sems):
        x, y, c, _ = _position()
        me = 4 * x + 2 * y + c
        buf[me] = p_ref[...]

        def peer(k):
            px = 1 - x if k & 4 else x
            py = 1 - y if k & 2 else y
            pc = 1 - c if k & 1 else c
            return px, py, pc

        def copy(k, incoming):
            px, py, pc = peer(k)
            slot = (4 * px + 2 * py + pc) if incoming else me
            return pltpu.make_async_remote_copy(
                src_ref=p_ref, dst_ref=buf.at[slot], send_sem=send_sems.at[k], recv_sem=recv_sems.at[k],
                device_id=(px, py, pc), device_id_type=MESH_ID)

        for k in range(1, n_dev):
            copy(k, False).start()
        for k in range(1, n_dev):
            copy(k, True).wait_recv()
        for k in range(1, n_dev):
            copy(k, False).wait_send()
        acc = buf[0]
        for j in range(1, n_dev):
            acc = acc + buf[j]
        o_ref[...] = acc

    vmem = pl.BlockSpec(memory_space=pltpu.VMEM)
    return pl.pallas_call(
        body, name=name,
        in_specs=[vmem], out_specs=vmem, out_shape=jax.ShapeDtypeStruct((R, Cc), F32),
        scratch_shapes=[pltpu.VMEM((n_dev, R, Cc), F32), pltpu.SemaphoreType.DMA((n_dev,)),
                        pltpu.SemaphoreType.DMA((n_dev,))],
    )(pack)


def _local_forward_backward(x2, target2, S, pass_on, fetch, reduce_begin, reduce_commit, layers, final_g, tm=512):
    T, D = x2.shape
    C = D // 2
    n_heads = C // GROUP
    n_layers = len(layers)
    weights = {}
    saved = []
    xc = x2
    for li, lw in enumerate(layers):
        if li == 0:
            pass_on(0, None)
            weights.update(fetch(0, None))
        h1, qkv3, cv3 = _norm_proj(xc, lw["norm1"], weights[li, "w_in"], 0, ((3, C, F32), (3, C, BF16)), tm,
                                   min(C, 512), f"l{li}_norm_in_proj")
        if li == 0:
            pass_on(1, h1)
        o, lse, mix = _attn_fwd(qkv3, lw["attn_g"], 2, S, n_heads, f"l{li}_attn_fwd")
        pin = None
        if li == 0:
            weights.update(fetch(1, o))
            pin = pass_on(3, pass_on(2, o))
        mix = _mix_conv_fwd(cv3, lw["taps"], lw["conv_g"], mix, S, f"l{li}_mix_conv_fwd", pin)
        x_mid = _proj_residual(mix, weights[li, "w_out"], 0, xc, tm, f"l{li}_out_proj")
        if li == 0:
            weights.update(fetch(2, x_mid))
        Fd = weights[li, "ffn_up"].shape[2] // 2
        h2, up3 = _norm_proj(x_mid, lw["norm2"], weights[li, "ffn_up"], 0, ((2, Fd, BF16),), tm // 2, 256,
                             f"l{li}_norm_ffn_up")
        if li == 0:
            weights.update(fetch(3, up3))
        act = _ffn_act_fwd(up3, lw["ffn_taps"], S, f"l{li}_ffn_act_fwd")
        pin = pass_on(li + 4, act) if li + 1 < n_layers else None
        x_out = _proj_residual(act.reshape(1, T, Fd), weights[li, "ffn_down"], 0, x_mid, tm, f"l{li}_ffn_down", pin)
        if li + 1 < n_layers:
            weights.update(fetch(li + 4, x_out))
        saved.append(dict(x_in=xc, h1=h1, qkv3=qkv3, cv3=cv3, o=o, lse=lse, mix=mix, x_mid=x_mid, h2=h2, up3=up3,
                          act=act))
        xc = x_out

    dx, d_final_g, loss_part = _final_norm_loss(xc, final_g, target2, tm, "final_norm_loss")

    small = [None] * n_layers
    started = None
    for li in reversed(range(n_layers)):
        lw, sv = layers[li], saved[li]
        w_in, w_out, ffn_up, ffn_down = (weights[li, n] for n in ("w_in", "w_out", "ffn_up", "ffn_down"))
        dxb, dact3 = _grad_through_weight(dx, ffn_down, 0, 1, Fd, BF16, tm, 256, f"l{li}_d_act", started)
        Fd = ffn_down.shape[1]
        d_ffn_down = _weight_grad(sv["act"].reshape(1, T, Fd), dxb.reshape(1, T, D), Fd // 2, D, 1024,
                                  0, 1, None, f"l{li}_d_ffn_down")
        dup3, d_taps_g, d_taps_v = _ffn_act_bwd(sv["up3"], dact3, lw["ffn_taps"], S, f"l{li}_ffn_act_bwd")
        d_ffn_up = _weight_grad(sv["h2"].reshape(1, T, D), dup3, D, Fd, 1024, 0, 1, None,
                                f"l{li}_d_ffn_up")
        if li == 0:
            early = reduce_begin({(li, "ffn_down"): d_ffn_down, (li, "ffn_up"): d_ffn_up})
        dx_mid, d_norm2 = _grad_through_proj_norm(dup3, ffn_up, 0, sv["x_mid"], lw["norm2"], dx, tm // 2,
                                                  f"l{li}_d_norm2")
        started = reduce_commit(early, dx_mid) if li == 0 else None
        dxmb, dmix3 = _grad_through_weight(dx_mid, w_out, 0, 2, C, F32, tm, min(C, 512), f"l{li}_d_mix", started)
        d_w_out = _weight_grad(sv["mix"], dxmb.reshape(1, T, D), C, D, 1024, 0, 1, None, f"l{li}_d_w_out")
        dproj, d_attn_g = _attn_bwd(sv["qkv3"], sv["o"], sv["lse"], dmix3, lw["attn_g"], 6, S, n_heads,
                                    f"l{li}_attn_bwd")
        dproj, d_taps, d_conv_g = _mix_conv_bwd(sv["cv3"], dmix3, lw["taps"], lw["conv_g"], dproj, S,
                                                f"l{li}_mix_conv_bwd")
        d_w_in = _weight_grad(sv["h1"].reshape(1, T, D), dproj, D, C, 1024, 0, 1, None, f"l{li}_d_w_in")
        late = {(li, "w_out"): d_w_out, (li, "w_in"): d_w_in}
        if li > 0:
            late.update({(li, "ffn_down"): d_ffn_down, (li, "ffn_up"): d_ffn_up})
        late = reduce_begin(late)
        dx, d_norm1 = _grad_through_proj_norm(dproj, w_in, 0, sv["x_in"], lw["norm1"], dx_mid, tm,
                                              f"l{li}_d_norm1")
        started = reduce_commit(late, dx)
        small[li] = dict(norm1=d_norm1, taps=d_taps, attn_g=d_attn_g, conv_g=d_conv_g, norm2=d_norm2,
                         ffn_taps=jnp.concatenate([d_taps_g, d_taps_v], axis=1))
    return loss_part, dx, small, d_final_g


SMALL_ORDER = ("norm1", "attn_g", "conv_g", "norm2", "taps", "ffn_taps")


def _pack_small(small, d_final_g, loss_row):
    parts = [small[li][k].reshape(-1) for li in range(len(small)) for k in SMALL_ORDER]
    loss_rows = jnp.tile(loss_row.reshape(1, LANES), (8, 1))
    return jnp.concatenate(parts + [d_final_g.reshape(-1), loss_rows.reshape(-1)]).reshape(-1, LANES)


def _unpack_small(pack, small, d_final_g):
    flat = pack.reshape(-1)
    out, pos = [dict() for _ in small], 0
    for li in range(len(small)):
        for k in SMALL_ORDER:
            n = small[li][k].size
            out[li][k] = flat[pos:pos + n].reshape(small[li][k].shape)
            pos += n
    return out, flat[pos:pos + d_final_g.size], flat[pos + d_final_g.size]


def kernel(x, norm1_g, w_in, mix_conv_w, attn_out_g, conv_out_g, w_out, norm2_g, ffn_up, ffn_conv_w, ffn_down, final_norm_g, loss_target, m_norm1_g, m_w_in, m_mix_conv_w, m_attn_out_g, m_conv_out_g, m_w_out, m_norm2_g, m_ffn_up, m_ffn_conv_w, m_ffn_down, m_final_norm_g, v_norm1_g, v_w_in, v_mix_conv_w, v_attn_out_g, v_conv_out_g, v_w_out, v_norm2_g, v_ffn_up, v_ffn_conv_w, v_ffn_down, v_final_norm_g):
    Bl, S, D = x.shape
    L = w_in.shape[0]
    T = Bl * S
    shard = 2 * lax.axis_index("x") + lax.axis_index("y")
    where = jnp.stack([shard, lax.axis_index("c")]).astype(jnp.int32)
    big_names = ("w_in", "w_out", "ffn_up", "ffn_down")

    taps_w, ftaps_w = mix_conv_w.shape[2], ffn_conv_w.shape[2]
    taps_full = jnp.zeros((L, 3, 4 * taps_w), F32)
    taps_full = lax.dynamic_update_slice(taps_full, mix_conv_w, (0, 0, shard * taps_w))
    ftaps_full = jnp.zeros((L, 3, 4 * ftaps_w), F32)
    ftaps_full = lax.dynamic_update_slice(ftaps_full, ffn_conv_w, (0, 0, shard * ftaps_w))
    tap_pack = jnp.concatenate([taps_full.reshape(-1), ftaps_full.reshape(-1)]).reshape(-1, LANES)
    tap_pack = _all_reduce_small(tap_pack * 0.5, "all_gather_taps")
    n_taps = taps_full.size
    taps_full = tap_pack.reshape(-1)[:n_taps].reshape(taps_full.shape)
    ftaps_full = tap_pack.reshape(-1)[n_taps:].reshape(ftaps_full.shape)

    big_shards = dict(zip(big_names, (w_in, w_out, ffn_up, ffn_down)))
    col_of = dict(zip(big_names, COL_SHARDED))
    groups = [[(0, n)] for n in big_names] + [[(l, n) for n in big_names] for l in range(1, L)]
    sems, in_flight = [], {}
    all_started = tap_pack
    for first, last in ((0, 1), (1, len(groups))):
        keys = [k for g in groups[first:last] for k in g]
        new_sems, arrays = _gather_start(
            [_cast_into_full(big_shards[n], l, col_of[n], where, f"cast_{n}_{l}") for l, n in keys],
            [col_of[n] for _, n in keys], [len(g) for g in groups[first:last]], all_started,
            f"gather_start_{first}")
        sems += new_sems
        in_flight.update(zip(keys, arrays))
        all_started = arrays[-1]

    def pass_on(g, after):
        after = all_started if g == 0 else after
        sems[g], arrays = _gather_pass([in_flight[k] for k in groups[g]], [col_of[n] for _, n in groups[g]],
                                       sems[g], after, f"gather_pass_{g}")
        in_flight.update(zip(groups[g], arrays))
        return arrays[0]

    def fetch(g, after):
        done = _gather_wait([in_flight[k] for k in groups[g]], [col_of[n] for _, n in groups[g]], sems[g], after,
                            f"gather_wait_{g}")
        return dict(zip(groups[g], done))

    pending = []

    begun = []

    def reduce_begin(grads):
        g = len(begun)
        keys = list(grads)
        cols = [col_of[n] for _, n in keys]
        begun.append((g, keys, cols) + _exchange_start([grads[k] for k in keys], cols, f"exchange_start_{g}"))
        return begun[-1]

    def reduce_commit(handle, after):
        g, keys, cols, ex_sems, mine, lands = handle
        mine, others = _exchange_wait(mine, lands, cols, ex_sems, after, f"exchange_wait_{g}")
        chip_sums = [_chip_sum(m, o, cw, where, f"chip_sum_{k[1]}_{k[0]}")
                     for k, m, o, cw in zip(keys, mine, others, cols)]
        pending.append((keys, cols) + _scatter_start(chip_sums, cols, f"scatter_start_{g}"))
        return pending[-1][3][0]

    layers = [dict(norm1=norm1_g[l:l + 1], taps=taps_full[l], attn_g=attn_out_g[l:l + 1],
                   conv_g=conv_out_g[l:l + 1], norm2=norm2_g[l:l + 1], ffn_taps=ftaps_full[l]) for l in range(L)]

    loss_part, dx, small, d_final_g = _local_forward_backward(
        x.reshape(T, D), loss_target.reshape(T, D), S, pass_on, fetch, reduce_begin, reduce_commit, layers,
        final_norm_g.reshape(1, D))

    def finish_group(g, after):
        keys, cols, rs_sems, sources, lands = pending[g]
        sources, lands = _scatter_wait(sources, lands, cols, rs_sems, after, f"scatter_wait_{g}")
        for (l, n), cw, src, land in zip(keys, cols, sources, lands):
            reduced[n] = _owner_sum(src, land, cw, where, l, L, reduced[n], f"owner_sum_{n}_{l}")

    reduced = dict.fromkeys(big_names)
    last_started = pending[-1][3][0]
    for g in range(len(pending) - 1):
        finish_group(g, last_started)
    late_names = [n for n in big_names if any(n == name for _, name in pending[-1][0])]
    early_names = [n for n in big_names if n not in late_names]
    g_big = dict(zip(early_names, _share_with_sibling([reduced[n] for n in early_names],
                                                      [col_of[n] for n in early_names], "grad_share_early")))

    pack = _all_reduce_small(_pack_small(small, d_final_g, loss_part), "all_reduce_small_grads")
    g_small, g_final, loss = _unpack_small(pack, small, d_final_g)

    def stacked(key):
        return jnp.stack([g_small[l][key].reshape(g_small[l][key].shape[-2:] if key.endswith("taps") else (-1,))
                          for l in range(L)])

    g_norm1, g_attn, g_conv, g_norm2 = stacked("norm1"), stacked("attn_g"), stacked("conv_g"), stacked("norm2")
    g_taps = lax.dynamic_slice(stacked("taps"), (0, 0, shard * taps_w), (L, 3, taps_w))
    g_ftaps = lax.dynamic_slice(stacked("ffn_taps"), (0, 0, shard * ftaps_w), (L, 3, ftaps_w))

    grads_out = dict(norm1_g=g_norm1, w_in=None, mix_conv_w=g_taps, attn_out_g=g_attn, conv_out_g=g_conv,
                     w_out=None, norm2_g=g_norm2, ffn_up=None, ffn_conv_w=g_ftaps, ffn_down=None,
                     final_norm_g=g_final)
    weights = dict(norm1_g=norm1_g, w_in=w_in, mix_conv_w=mix_conv_w, attn_out_g=attn_out_g, conv_out_g=conv_out_g,
                   w_out=w_out, norm2_g=norm2_g, ffn_up=ffn_up, ffn_conv_w=ffn_conv_w, ffn_down=ffn_down,
                   final_norm_g=final_norm_g)
    ms = dict(norm1_g=m_norm1_g, w_in=m_w_in, mix_conv_w=m_mix_conv_w, attn_out_g=m_attn_out_g,
              conv_out_g=m_conv_out_g, w_out=m_w_out, norm2_g=m_norm2_g, ffn_up=m_ffn_up, ffn_conv_w=m_ffn_conv_w,
              ffn_down=m_ffn_down, final_norm_g=m_final_norm_g)
    vs = dict(norm1_g=v_norm1_g, w_in=v_w_in, mix_conv_w=v_mix_conv_w, attn_out_g=v_attn_out_g,
              conv_out_g=v_conv_out_g, w_out=v_w_out, norm2_g=v_norm2_g, ffn_up=v_ffn_up, ffn_conv_w=v_ffn_conv_w,
              ffn_down=v_ffn_down, final_norm_g=v_final_norm_g)
    names = list(weights)
    small_names = [n for n in names if n not in big_names]
    delta, new_m, new_v = {}, {}, {}

    def update_big(n):
        shp = weights[n].shape
        two_d = (shp[0] * shp[1], shp[2])
        d_, m_, v_, g_ = _adamw(weights[n].reshape(two_d), g_big[n].reshape(two_d), ms[n].reshape(two_d),
                                vs[n].reshape(two_d), f"adamw_{n}")
        delta[n], new_m[n], new_v[n], grads_out[n] = (a.reshape(shp) for a in (d_, m_, v_, g_))

    for n in early_names:
        update_big(n)
    finish_group(len(pending) - 1, delta[early_names[-1]] if early_names else None)
    g_big.update(zip(late_names, _share_with_sibling([reduced[n] for n in late_names],
                                                     [col_of[n] for n in late_names], "grad_share_late")))
    for n in late_names:
        update_big(n)

    def packed(tree):
        return jnp.concatenate([tree[n].reshape(-1) for n in small_names]).reshape(-1, LANES)

    d_, m_, v_, _ = _adamw(packed(weights), packed(grads_out), packed(ms), packed(vs), "adamw_small")
    pos = 0
    for n in small_names:
        size, shp = weights[n].size, weights[n].shape
        delta[n] = d_.reshape(-1)[pos:pos + size].reshape(shp)
        new_m[n] = m_.reshape(-1)[pos:pos + size].reshape(shp)
        new_v[n] = v_.reshape(-1)[pos:pos + size].reshape(shp)
        pos += size

    return (loss, dx.reshape(Bl, S, D), *[grads_out[n] for n in names], *[delta[n] for n in names],
            *[new_m[n] for n in names], *[new_v[n] for n in names])
```

```python
import functools
import math

import jax
import jax.numpy as jnp
from jax import lax
from jax.experimental import pallas as pl
from jax.experimental.pallas import tpu as pltpu

F32 = jnp.float32
BF16 = jnp.bfloat16
EPS = 1e-6
GROUP = 64
LANES = 128
BAND = 128
DILATIONS = (1, 4, 16)
NEG = -1e30
MIB = 1024 * 1024
MESH_ID = pl.DeviceIdType.MESH

ADAM_LR = 0.001
ADAM_B1 = 0.9
ADAM_B2 = 0.999
ADAM_EPS = 1e-08
ADAM_WD = 0.01
ADAM_STEP = 10


ANY = pl.BlockSpec(memory_space=pl.ANY)


def _params(sem=None, vmem_mb=48):
    return pltpu.CompilerParams(dimension_semantics=sem, vmem_limit_bytes=vmem_mb * MIB)


def _nt(a, b):
    return lax.dot_general(a, b, (((1,), (1,)), ((), ())), preferred_element_type=F32)


def _tn(a, b):
    return lax.dot_general(a, b, (((0,), (0,)), ((), ())), preferred_element_type=F32)


def _seg_sum(x, is_a):
    s_a = jnp.sum(jnp.where(is_a, x, 0.0), axis=-1, keepdims=True)
    s_b = jnp.sum(jnp.where(is_a, 0.0, x), axis=-1, keepdims=True)
    return jnp.where(is_a, s_a, s_b)


def _lane_is_a():
    return lax.broadcasted_iota(jnp.int32, (1, LANES), 1) < GROUP


def _norm_proj(x, g, w3, layer, groups, tm, chunk, name):
    T, D = x.shape
    N = w3.shape[2]
    assert sum(p * c for p, c, _ in groups) == N and T % tm == 0

    def body(x_ref, g_ref, w_ref, h_ref, *out_refs):
        xv = x_ref[...]
        rstd = lax.rsqrt(jnp.mean(xv * xv, axis=-1, keepdims=True) + EPS)
        h = ((xv * rstd) * g_ref[...]).astype(BF16)
        h_ref[...] = h.T
        col = 0
        for (pieces, width, dtype), o_ref in zip(groups, out_refs):
            for p in range(pieces):
                for c0 in range(0, width, chunk):
                    acc = jnp.dot(h, w_ref[:, col + c0:col + c0 + chunk], preferred_element_type=F32)
                    o_ref[p, :, c0:c0 + chunk] = acc.astype(dtype)
                col += width

    out_shape = [jax.ShapeDtypeStruct((D, T), BF16)]
    out_specs = [pl.BlockSpec((D, tm), lambda i: (0, i))]
    for pieces, width, dtype in groups:
        assert width % chunk == 0
        out_shape.append(jax.ShapeDtypeStruct((pieces, T, width), dtype))
        out_specs.append(pl.BlockSpec((pieces, tm, width), lambda i: (0, i, 0)))
    return pl.pallas_call(
        body, grid=(T // tm,), name=name,
        in_specs=[pl.BlockSpec((tm, D), lambda i: (i, 0)),
                  pl.BlockSpec((1, D), lambda i: (0, 0)),
                  pl.BlockSpec((None, D, N), lambda i: (layer, 0, 0))],
        out_specs=out_specs, out_shape=out_shape,
        compiler_params=_params(("parallel",), 56),
    )(x, g, w3)


def _proj_residual(pieces3, w3, layer, x, tm, name, after=None):
    P, T, C = pieces3.shape
    D = w3.shape[2]

    def body(a_ref, w_ref, x_ref, *rest):
        o_ref = rest[-1]
        acc = x_ref[...]
        for p in range(P):
            acc = acc + jnp.dot(a_ref[p], w_ref[p * C:(p + 1) * C, :], preferred_element_type=F32)
        o_ref[...] = acc

    in_specs = [pl.BlockSpec((P, tm, C), lambda i: (0, i, 0)),
                pl.BlockSpec((None, P * C, D), lambda i: (layer, 0, 0)),
                pl.BlockSpec((tm, D), lambda i: (i, 0))]
    operands = [pieces3, w3, x]
    if after is not None:
        in_specs.append(ANY)
        operands.append(after)
    return pl.pallas_call(
        body, grid=(T // tm,), name=name,
        in_specs=in_specs,
        out_specs=pl.BlockSpec((tm, D), lambda i: (i, 0)),
        out_shape=jax.ShapeDtypeStruct((T, D), F32),
        compiler_params=_params(("parallel",)),
    )(*operands)


def _grad_through_weight(dy, w3, layer, pieces, width, out_dtype, tm, chunk, name, after=None):
    T, D = dy.shape

    def body(dy_ref, w_ref, *rest):
        dyb_ref, o_ref = rest[-2:]
        dyb = dy_ref[...].astype(BF16)
        dyb_ref[...] = dyb
        for p in range(pieces):
            for c0 in range(0, width, chunk):
                r0 = p * width + c0
                o_ref[p, :, c0:c0 + chunk] = _nt(dyb, w_ref[r0:r0 + chunk, :]).astype(out_dtype)

    in_specs = [pl.BlockSpec((tm, D), lambda i: (i, 0)),
                pl.BlockSpec((None, pieces * width, D), lambda i: (layer, 0, 0))]
    operands = [dy, w3]
    if after is not None:
        in_specs.append(ANY)
        operands.append(after)
    return pl.pallas_call(
        body, grid=(T // tm,), name=name,
        in_specs=in_specs,
        out_specs=[pl.BlockSpec((tm, D), lambda i: (i, 0)),
                   pl.BlockSpec((pieces, tm, width), lambda i: (0, i, 0))],
        out_shape=[jax.ShapeDtypeStruct((T, D), BF16),
                   jax.ShapeDtypeStruct((pieces, T, width), out_dtype)],
        compiler_params=_params(("parallel",)),
    )(*operands)


def _grad_through_proj_norm(dp3, w3, layer, x, g, dx_in, tm, name):
    P, T, C = dp3.shape
    D = w3.shape[1]

    def body(dp_ref, w_ref, x_ref, g_ref, dxin_ref, dx_ref, dg_ref):
        dh = _nt(dp_ref[0], w_ref[:, 0:C])
        for p in range(1, P):
            dh = dh + _nt(dp_ref[p], w_ref[:, p * C:(p + 1) * C])
        xv = x_ref[...]
        rstd = lax.rsqrt(jnp.mean(xv * xv, axis=-1, keepdims=True) + EPS)
        xn = xv * rstd
        a = dh * g_ref[...]
        dx_ref[...] = dxin_ref[...] + rstd * (a - xn * jnp.mean(a * xn, axis=-1, keepdims=True))
        part = jnp.sum(dh * xn, axis=0, keepdims=True)

        @pl.when(pl.program_id(0) == 0)
        def _():
            dg_ref[...] = part

        @pl.when(pl.program_id(0) != 0)
        def _():
            dg_ref[...] += part

    return pl.pallas_call(
        body, grid=(T // tm,), name=name,
        in_specs=[pl.BlockSpec((P, tm, C), lambda i: (0, i, 0)),
                  pl.BlockSpec((None, D, P * C), lambda i: (layer, 0, 0)),
                  pl.BlockSpec((tm, D), lambda i: (i, 0)),
                  pl.BlockSpec((1, D), lambda i: (0, 0)),
                  pl.BlockSpec((tm, D), lambda i: (i, 0))],
        out_specs=[pl.BlockSpec((tm, D), lambda i: (i, 0)),
                   pl.BlockSpec((1, D), lambda i: (0, 0))],
        out_shape=[jax.ShapeDtypeStruct((T, D), F32), jax.ShapeDtypeStruct((1, D), F32)],
        compiler_params=_params(("arbitrary",), 56),
    )(dp3, w3, x, g, dx_in)


def _weight_grad(a3, g3, ta, tg, name, a_transposed=False):
    PG, T, CG = g3.shape
    PA, CA = (a3.shape[0], a3.shape[1]) if a_transposed else (a3.shape[0], a3.shape[2])
    na, ng = CA // ta, CG // tg
    assert CA % ta == 0 and CG % tg == 0

    def body(a_ref, g_ref, o_ref):
        if a_transposed:
            part = jnp.dot(a_ref[...], g_ref[...], preferred_element_type=F32)
        else:
            part = _tn(a_ref[...], g_ref[...])
        o_ref[...] = part.astype(o_ref.dtype)

    a_spec = (pl.BlockSpec((None, ta, T), lambda i, j: (i // na, i % na, 0)) if a_transposed
              else pl.BlockSpec((None, T, ta), lambda i, j: (i // na, 0, i % na)))
    return pl.pallas_call(
        body, grid=(PA * na, PG * ng), name=name,
        in_specs=[a_spec, pl.BlockSpec((None, T, tg), lambda i, j: (j // ng, 0, j % ng))],
        out_specs=pl.BlockSpec((None, ta, tg), lambda i, j: (0, i, j)),
        out_shape=jax.ShapeDtypeStruct((1, PA * CA, PG * CG), BF16),
        compiler_params=_params(("parallel", "parallel"), 56),
    )(a3, g3)


def _bias_tables(bm_ref, lone_ref, pair, n_heads, S):
    ii = lax.broadcasted_iota(jnp.int32, (BAND, 2 * BAND), 0)
    jj = lax.broadcasted_iota(jnp.int32, (BAND, 2 * BAND), 1)
    dist = BAND + ii - jj
    valid = (dist >= 0) & (dist <= BAND)
    distf = dist.astype(F32)
    for hh in range(2):
        head = (2 * pair + hh + 1).astype(F32)
        slope = jnp.exp(jnp.full((1, 1), -8.0 / n_heads * math.log(2.0), F32) * head)
        for bi, d in enumerate(DILATIONS):
            table = jnp.where(valid, -(slope * d) * distf, NEG)
            bm_ref[bi, hh * BAND:(hh + 1) * BAND, :] = table
            if S // (BAND * d) == 1:
                lone_ref[bi, hh * BAND:(hh + 1) * BAND, :] = table[:, BAND:2 * BAND]


def _stack_heads(x, is_a):
    zero = jnp.zeros_like(x)
    return jnp.concatenate([jnp.where(is_a, x, zero), jnp.where(is_a, zero, x)], axis=0)


def _unstack_heads(x2, is_a):
    return jnp.where(is_a, x2[0:BAND], x2[BAND:2 * BAND])


def _gather_residues(dst_ref, src, d, S, convert):
    L = S // d
    for r in range(d):
        rows = pl.ds(r, L, stride=d) if d > 1 else slice(None)
        dst_ref[r * L:(r + 1) * L, :] = convert(src(rows))


def _block_rows(t, d, S):
    nb = S // (BAND * d)
    n = t % nb
    has_prev = jnp.minimum(n, 1)
    cur = pl.ds(pl.multiple_of(t * BAND, BAND), BAND)
    prev = pl.ds(pl.multiple_of((t - has_prev) * BAND, BAND), BAND)
    return cur, prev, has_prev


def _first_block_penalty(has_prev):
    jrow = lax.broadcasted_iota(jnp.int32, (1, 2 * BAND), 1)
    pen = jnp.where(has_prev == 0, NEG, 0.0).astype(F32)
    return jnp.where(jrow < BAND, pen, 0.0)


def _attn_fwd(qkv3, gain, mix_shape_pieces, S, n_heads, name):
    _, T, C = qkv3.shape
    B, P = T // S, C // LANES
    NBLK = S // BAND
    scale = GROUP ** -0.5
    nbr = len(DILATIONS)
    RC = 256

    def body(qkv_ref, g_ref, o_ref, lse_ref, an_ref, qs, ks, vs, op, mp, lp, ob, mb, lb, bm, bml):
        pair = pl.program_id(1)
        is_a = _lane_is_a()
        _bias_tables(bm, bml, pair, n_heads, S)

        for bi, d in enumerate(DILATIONS):
            nb = S // (BAND * d)
            _gather_residues(qs, lambda rows: qkv_ref.at[0][rows, :], d, S, lambda v: (v * scale).astype(BF16))
            _gather_residues(ks, lambda rows: qkv_ref.at[1][rows, :], d, S, lambda v: v.astype(BF16))
            _gather_residues(vs, lambda rows: qkv_ref.at[2][rows, :], d, S, lambda v: v.astype(BF16))
            o_dst, m_dst, l_dst = (ob.at[bi], mb.at[bi], lb.at[bi]) if d == 1 else (op, mp, lp)

            def block(t, carry, bi=bi, d=d, nb=nb, o_dst=o_dst, m_dst=m_dst, l_dst=l_dst):
                cur, prev, has_prev = _block_rows(t, d, S)
                q2 = _stack_heads(qs[cur, :], is_a)
                if nb == 1:
                    kc, vc = ks[cur, :], vs[cur, :]
                    s = _nt(q2, kc) + bml[bi]
                else:
                    kc = jnp.concatenate([ks[prev, :], ks[cur, :]], axis=0)
                    vc = jnp.concatenate([vs[prev, :], vs[cur, :]], axis=0)
                    s = _nt(q2, kc) + bm[bi] + _first_block_penalty(has_prev)
                m = jnp.max(s, axis=-1, keepdims=True)
                e = jnp.exp(s - m)
                l = jnp.sum(e, axis=-1, keepdims=True)
                pv = jnp.dot(e.astype(BF16), vc, preferred_element_type=F32)
                o_dst[cur, :] = _unstack_heads(pv, is_a)
                m_dst[cur, :] = _unstack_heads(m, is_a)
                l_dst[cur, :] = _unstack_heads(l, is_a)
                return carry

            lax.fori_loop(0, NBLK, block, 0, unroll=8)
            if d > 1:
                L = S // d
                for r in range(d):
                    rows = pl.ds(r, L, stride=d)
                    ob.at[bi][rows, :] = op[r * L:(r + 1) * L, :]
                    mb.at[bi][rows, :] = mp[r * L:(r + 1) * L, :]
                    lb.at[bi][rows, :] = lp[r * L:(r + 1) * L, :]

        def finish(ci, carry):
            rs = pl.ds(pl.multiple_of(ci * RC, RC), RC)
            ms = [mb[bi, rs, :] for bi in range(nbr)]
            mmax = functools.reduce(jnp.maximum, ms)
            ws = [jnp.exp(m - mmax) for m in ms]
            num = sum(ob[bi, rs, :] * ws[bi] for bi in range(nbr))
            den = sum(lb[bi, rs, :] * ws[bi] for bi in range(nbr))
            o = num / den
            o_ref[rs, :] = o
            lse_ref[rs, :] = mmax + jnp.log(den)
            rstd = lax.rsqrt(_seg_sum(o * o, is_a) * (1.0 / GROUP) + EPS)
            an_ref[rs, :] = ((o * rstd) * g_ref[...]).astype(BF16)
            return carry

        lax.fori_loop(0, S // RC, finish, 0)

    seq = pl.BlockSpec((S, LANES), lambda b, p: (b, p))
    return pl.pallas_call(
        body, grid=(B, P), name=name,
        in_specs=[pl.BlockSpec((3, S, LANES), lambda b, p: (0, b, p)),
                  pl.BlockSpec((1, LANES), lambda b, p: (0, p))],
        out_specs=[seq, seq, pl.BlockSpec((None, S, LANES), lambda b, p: (0, b, p))],
        out_shape=[jax.ShapeDtypeStruct((T, C), F32), jax.ShapeDtypeStruct((T, C), F32),
                   jax.ShapeDtypeStruct((mix_shape_pieces, T, C), BF16)],
        scratch_shapes=[pltpu.VMEM((S, LANES), BF16)] * 3 + [pltpu.VMEM((S, LANES), F32)] * 3
        + [pltpu.VMEM((nbr, S, LANES), F32)] * 3
        + [pltpu.VMEM((nbr, 2 * BAND, 2 * BAND), F32), pltpu.VMEM((nbr, 2 * BAND, BAND), F32)],
        compiler_params=_params(("parallel", "parallel")),
    )(qkv3, gain)


def _attn_bwd(qkv3, o, lse, dmix3, gain, dproj_pieces, S, n_heads, name):
    _, T, C = qkv3.shape
    B, P = T // S, C // LANES
    NBLK = S // BAND
    scale = GROUP ** -0.5
    nbr = len(DILATIONS)
    RC = 256

    def body(qkv_ref, o_ref, lse_ref, dn_ref, g_ref, dqkv_ref, dg_ref,
             do_n, dd_n, qs, ks, vs, dos, lses, dds, dqp, dkp, dvp, dqn, dkn, dvn, bm, bml):
        pair = pl.program_id(0)
        b = pl.program_id(1)
        is_a = _lane_is_a()
        _bias_tables(bm, bml, pair, n_heads, S)

        def prologue(ci, dg_acc):
            rs = pl.ds(pl.multiple_of(ci * RC, RC), RC)
            ov = o_ref[rs, :]
            dn = dn_ref[rs, :]
            rstd = lax.rsqrt(_seg_sum(ov * ov, is_a) * (1.0 / GROUP) + EPS)
            on = ov * rstd
            a = dn * g_ref[...]
            do = rstd * (a - on * (_seg_sum(a * on, is_a) * (1.0 / GROUP)))
            do_n[rs, :] = do
            dd_n[rs, :] = _seg_sum(do * ov, is_a)
            zero = jnp.zeros((RC, LANES), F32)
            dqn[rs, :] = zero
            dkn[rs, :] = zero
            dvn[rs, :] = zero
            return dg_acc + jnp.sum(dn * on, axis=0, keepdims=True)

        dg_part = lax.fori_loop(0, S // RC, prologue, jnp.zeros((1, LANES), F32))

        @pl.when(b == 0)
        def _():
            dg_ref[...] = dg_part

        @pl.when(b != 0)
        def _():
            dg_ref[...] += dg_part

        for bi, d in enumerate(DILATIONS):
            nb = S // (BAND * d)
            L = S // d
            _gather_residues(qs, lambda rows: qkv_ref.at[0][rows, :], d, S, lambda v: (v * scale).astype(BF16))
            _gather_residues(ks, lambda rows: qkv_ref.at[1][rows, :], d, S, lambda v: v.astype(BF16))
            _gather_residues(vs, lambda rows: qkv_ref.at[2][rows, :], d, S, lambda v: v.astype(BF16))
            _gather_residues(dos, lambda rows: do_n[rows, :], d, S, lambda v: v.astype(BF16))
            if d == 1:
                lse_src, dd_src, dq_dst, dk_dst, dv_dst = lse_ref, dd_n, dqn, dkn, dvn
            else:
                _gather_residues(lses, lambda rows: lse_ref[rows, :], d, S, lambda v: v)
                _gather_residues(dds, lambda rows: dd_n[rows, :], d, S, lambda v: v)
                dkp[...] = jnp.zeros((S, LANES), F32)
                dvp[...] = jnp.zeros((S, LANES), F32)
                lse_src, dd_src, dq_dst, dk_dst, dv_dst = lses, dds, dqp, dkp, dvp

            def block(t, carry, bi=bi, d=d, nb=nb, lse_src=lse_src, dd_src=dd_src, dq_dst=dq_dst, dk_dst=dk_dst,
                      dv_dst=dv_dst):
                cur, prev, has_prev = _block_rows(t, d, S)
                q2 = _stack_heads(qs[cur, :], is_a)
                do2 = _stack_heads(dos[cur, :], is_a)
                lse_t = lse_src[cur, :]
                dd_t = dd_src[cur, :]
                lse2 = jnp.concatenate([lse_t[:, 0:1], lse_t[:, GROUP:GROUP + 1]], axis=0)
                dd2 = jnp.concatenate([dd_t[:, 0:1], dd_t[:, GROUP:GROUP + 1]], axis=0)
                if nb == 1:
                    kc, vc = ks[cur, :], vs[cur, :]
                    s = _nt(q2, kc) + bml[bi]
                else:
                    kc = jnp.concatenate([ks[prev, :], ks[cur, :]], axis=0)
                    vc = jnp.concatenate([vs[prev, :], vs[cur, :]], axis=0)
                    s = _nt(q2, kc) + bm[bi] + _first_block_penalty(has_prev)
                p = jnp.exp(s - lse2)
                ds = (p * (_nt(do2, vc) - dd2)).astype(BF16)
                dq = _unstack_heads(jnp.dot(ds, kc, preferred_element_type=F32), is_a)
                dk = _tn(ds, q2)
                dv = _tn(p.astype(BF16), do2)
                dq_dst[cur, :] = dq
                if nb == 1:
                    dk_dst[cur, :] += dk
                    dv_dst[cur, :] += dv
                else:
                    dk_dst[prev, :] += dk[0:BAND, :]
                    dv_dst[prev, :] += dv[0:BAND, :]
                    dk_dst[cur, :] += dk[BAND:2 * BAND, :]
                    dv_dst[cur, :] += dv[BAND:2 * BAND, :]
                return carry

            lax.fori_loop(0, NBLK, block, 0, unroll=8)
            if d > 1:
                for r in range(d):
                    rows = pl.ds(r, L, stride=d)
                    dqn[rows, :] += dqp[r * L:(r + 1) * L, :]
                    dkn[rows, :] += dkp[r * L:(r + 1) * L, :]
                    dvn[rows, :] += dvp[r * L:(r + 1) * L, :]

        dqkv_ref[0] = (dqn[...] * scale).astype(BF16)
        dqkv_ref[1] = dkn[...].astype(BF16)
        dqkv_ref[2] = dvn[...].astype(BF16)

    seq = pl.BlockSpec((S, LANES), lambda p, b: (b, p))
    f32_seq = pltpu.VMEM((S, LANES), F32)
    bf_seq = pltpu.VMEM((S, LANES), BF16)
    return pl.pallas_call(
        body, grid=(P, B), name=name,
        in_specs=[pl.BlockSpec((3, S, LANES), lambda p, b: (0, b, p)), seq, seq,
                  pl.BlockSpec((None, S, LANES), lambda p, b: (0, b, p)),
                  pl.BlockSpec((1, LANES), lambda p, b: (0, p))],
        out_specs=[pl.BlockSpec((3, S, LANES), lambda p, b: (0, b, p)),
                   pl.BlockSpec((1, LANES), lambda p, b: (0, p))],
        out_shape=[jax.ShapeDtypeStruct((dproj_pieces, T, C), BF16), jax.ShapeDtypeStruct((1, C), F32)],
        scratch_shapes=[f32_seq, f32_seq, bf_seq, bf_seq, bf_seq, bf_seq, f32_seq, f32_seq,
                        f32_seq, f32_seq, f32_seq, f32_seq, f32_seq, f32_seq,
                        pltpu.VMEM((nbr, 2 * BAND, 2 * BAND), F32), pltpu.VMEM((nbr, 2 * BAND, BAND), F32)],
        compiler_params=_params(("parallel", "arbitrary")),
    )(qkv3, o, lse, dmix3, gain)


def _delay(x, k, row):
    return jnp.where(row >= k, pltpu.roll(x, k, 0), 0.0)


def _advance(x, k, row, S):
    return jnp.where(row < S - k, pltpu.roll(x, S - k, 0), 0.0)


def _conv3(x, w, row):
    return (w[0:1, :] * _delay(x, 2, row) + w[1:2, :] * _delay(x, 1, row)) + w[2:3, :] * x


HALO = 8


CONV_ROWS = 128
FFN_LANES = 128


def _zero_halo(pad_ref, S):
    zeros = jnp.zeros((HALO, pad_ref.shape[1]), pad_ref.dtype)
    pad_ref[0:HALO, :] = zeros
    pad_ref[HALO + S:2 * HALO + S, :] = zeros


def _window_at(pad_ref, r0, shift):
    return pad_ref[HALO + r0 + shift:HALO + r0 + shift + CONV_ROWS, :]


def _conv3_at(pad_ref, w, r0):
    return ((w[0:1, :] * _window_at(pad_ref, r0, -2) + w[1:2, :] * _window_at(pad_ref, r0, -1))
            + w[2:3, :] * _window_at(pad_ref, r0, 0))


def _conv3_grads_at(dz_ref, x_ref, w, r0):
    dz, dz1, dz2 = (_window_at(dz_ref, r0, k) for k in range(3))
    x = _window_at(x_ref, r0, 0)
    dx = (w[2:3, :] * dz + w[1:2, :] * dz1) + w[0:1, :] * dz2
    parts = [jnp.sum((d * x).reshape(CONV_ROWS // 8, 8, x.shape[1]), axis=0) for d in (dz2, dz1, dz)]
    return dx, parts


def _conv3_grads(dz, x, w, row, S):
    dz1 = _advance(dz, 1, row, S)
    dz2 = _advance(dz, 2, row, S)
    dx = (w[2:3, :] * dz + w[1:2, :] * dz1) + w[0:1, :] * dz2
    dw = jnp.concatenate([jnp.sum(dz2 * x, axis=0, keepdims=True),
                          jnp.sum(dz1 * x, axis=0, keepdims=True),
                          jnp.sum(dz * x, axis=0, keepdims=True)], axis=0)
    return dx, dw


def _mix_conv_fwd(cv3, taps, gain, mix, S, name, after=None):
    _, T, C = cv3.shape
    B, P = T // S, C // LANES

    def body(cv_ref, w_ref, g_ref, mix_hbm, *rest):
        y_ref, pad_c = rest[-2:]
        del mix_hbm
        is_a = _lane_is_a()
        _zero_halo(pad_c, S)
        pad_c[HALO:HALO + S, :] = cv_ref[1].astype(F32) * cv_ref[2].astype(F32)
        w = w_ref[...]
        for r0 in range(0, S, CONV_ROWS):
            y = cv_ref[0, r0:r0 + CONV_ROWS, :].astype(F32) * _conv3_at(pad_c, w, r0)
            rstd = lax.rsqrt(_seg_sum(y * y, is_a) * (1.0 / GROUP) + EPS)
            y_ref[r0:r0 + CONV_ROWS, :] = ((y * rstd) * g_ref[...]).astype(BF16)

    in_specs = [pl.BlockSpec((3, S, LANES), lambda b, p: (0, b, p)),
                pl.BlockSpec((3, LANES), lambda b, p: (0, p)),
                pl.BlockSpec((1, LANES), lambda b, p: (0, p)),
                ANY]
    operands = [cv3, taps, gain, mix]
    if after is not None:
        in_specs.append(ANY)
        operands.append(after)
    return pl.pallas_call(
        body, grid=(B, P), name=name,
        in_specs=in_specs,
        out_specs=pl.BlockSpec((None, S, LANES), lambda b, p: (1, b, p)),
        out_shape=jax.ShapeDtypeStruct(mix.shape, mix.dtype),
        scratch_shapes=[pltpu.VMEM((S + 2 * HALO, LANES), F32)],
        input_output_aliases={3: 0},
        compiler_params=_params(("parallel", "parallel")),
    )(*operands)


def _mix_conv_bwd(cv3, dmix3, taps, gain, dproj, S, name):
    _, T, C = cv3.shape
    B, P = T // S, C // LANES

    def body(cv_ref, dn_ref, w_ref, g_ref, dproj_hbm, dcv_ref, dw_ref, dg_ref):
        del dproj_hbm
        b = pl.program_id(1)
        row = lax.broadcasted_iota(jnp.int32, (S, 1), 0)
        is_a = _lane_is_a()
        w = w_ref[...]
        gb = cv_ref[0].astype(F32)
        gc = cv_ref[1].astype(F32)
        u = cv_ref[2].astype(F32)
        c = gc * u
        z = _conv3(c, w, row)
        y = gb * z
        rstd = lax.rsqrt(_seg_sum(y * y, is_a) * (1.0 / GROUP) + EPS)
        yn = y * rstd
        dn = dn_ref[...]
        a = dn * g_ref[...]
        dy = rstd * (a - yn * (_seg_sum(a * yn, is_a) * (1.0 / GROUP)))
        dg = jnp.sum(dn * yn, axis=0, keepdims=True)
        dc, dw = _conv3_grads(dy * gb, c, w, row, S)
        dcv_ref[0] = (dy * z).astype(BF16)
        dcv_ref[1] = (dc * u).astype(BF16)
        dcv_ref[2] = (dc * gc).astype(BF16)

        @pl.when(b == 0)
        def _():
            dw_ref[...] = dw
            dg_ref[...] = dg

        @pl.when(b != 0)
        def _():
            dw_ref[...] += dw
            dg_ref[...] += dg

    return pl.pallas_call(
        body, grid=(P, B), name=name,
        in_specs=[pl.BlockSpec((3, S, LANES), lambda p, b: (0, b, p)),
                  pl.BlockSpec((None, S, LANES), lambda p, b: (1, b, p)),
                  pl.BlockSpec((3, LANES), lambda p, b: (0, p)),
                  pl.BlockSpec((1, LANES), lambda p, b: (0, p)),
                  pl.BlockSpec(memory_space=pl.ANY)],
        out_specs=[pl.BlockSpec((3, S, LANES), lambda p, b: (1, b, p)),
                   pl.BlockSpec((3, LANES), lambda p, b: (0, p)),
                   pl.BlockSpec((1, LANES), lambda p, b: (0, p))],
        out_shape=[jax.ShapeDtypeStruct(dproj.shape, dproj.dtype),
                   jax.ShapeDtypeStruct((3, C), F32), jax.ShapeDtypeStruct((1, C), F32)],
        input_output_aliases={4: 0},
        compiler_params=_params(("parallel", "arbitrary")),
    )(cv3, dmix3, taps, gain, dproj)


def _sigmoid(x):
    return 0.5 * jnp.tanh(0.5 * x) + 0.5


def _ffn_act_fwd(up3, taps, S, name):
    _, T, Fd = up3.shape
    W = FFN_LANES
    B, P = T // S, Fd // W

    def body(up_ref, wg_ref, wv_ref, act_ref, pad_g, pad_v):
        _zero_halo(pad_g, S)
        _zero_halo(pad_v, S)
        pad_g[HALO:HALO + S, :] = up_ref[0].astype(F32)
        pad_v[HALO:HALO + S, :] = up_ref[1].astype(F32)
        wg = wg_ref[...]
        wv = wv_ref[...]
        for r0 in range(0, S, CONV_ROWS):
            cg = _conv3_at(pad_g, wg, r0)
            cv = _conv3_at(pad_v, wv, r0)
            act_ref[r0:r0 + CONV_ROWS, :] = ((cg * _sigmoid(cg)) * cv).astype(BF16)

    return pl.pallas_call(
        body, grid=(B, P), name=name,
        in_specs=[pl.BlockSpec((2, S, W), lambda b, p: (0, b, p)),
                  pl.BlockSpec((3, W), lambda b, p: (0, p)),
                  pl.BlockSpec((3, W), lambda b, p: (0, P + p))],
        out_specs=pl.BlockSpec((S, W), lambda b, p: (b, p)),
        out_shape=jax.ShapeDtypeStruct((T, Fd), BF16),
        scratch_shapes=[pltpu.VMEM((S + 2 * HALO, W), F32)] * 2,
        compiler_params=_params(("parallel", "parallel")),
    )(up3, taps, taps)


def _ffn_act_bwd(up3, dact3, taps, S, name):
    _, T, Fd = up3.shape
    W = FFN_LANES
    B, P = T // S, Fd // W

    def body(up_ref, da_ref, wg_ref, wv_ref, dup_ref, dwg_ref, dwv_ref, pad_ug, pad_uv, pad_dg, pad_dv):
        b = pl.program_id(1)
        for pad in (pad_ug, pad_uv, pad_dg, pad_dv):
            _zero_halo(pad, S)
        pad_ug[HALO:HALO + S, :] = up_ref[0].astype(F32)
        pad_uv[HALO:HALO + S, :] = up_ref[1].astype(F32)
        wg = wg_ref[...]
        wv = wv_ref[...]
        for r0 in range(0, S, CONV_ROWS):
            cg = _conv3_at(pad_ug, wg, r0)
            cv = _conv3_at(pad_uv, wv, r0)
            sg = _sigmoid(cg)
            da = da_ref[r0:r0 + CONV_ROWS, :].astype(F32)
            pad_dg[HALO + r0:HALO + r0 + CONV_ROWS, :] = (da * cv) * (sg * (1.0 + cg * (1.0 - sg)))
            pad_dv[HALO + r0:HALO + r0 + CONV_ROWS, :] = da * (cg * sg)
        sums_g = [jnp.zeros((8, W), F32)] * 3
        sums_v = [jnp.zeros((8, W), F32)] * 3
        for r0 in range(0, S, CONV_ROWS):
            dug, parts_g = _conv3_grads_at(pad_dg, pad_ug, wg, r0)
            duv, parts_v = _conv3_grads_at(pad_dv, pad_uv, wv, r0)
            dup_ref[0, r0:r0 + CONV_ROWS, :] = dug.astype(BF16)
            dup_ref[1, r0:r0 + CONV_ROWS, :] = duv.astype(BF16)
            sums_g = [a + p for a, p in zip(sums_g, parts_g)]
            sums_v = [a + p for a, p in zip(sums_v, parts_v)]
        dwg = jnp.concatenate([jnp.sum(a, axis=0, keepdims=True) for a in sums_g], axis=0)
        dwv = jnp.concatenate([jnp.sum(a, axis=0, keepdims=True) for a in sums_v], axis=0)

        @pl.when(b == 0)
        def _():
            dwg_ref[...] = dwg
            dwv_ref[...] = dwv

        @pl.when(b != 0)
        def _():
            dwg_ref[...] += dwg
            dwv_ref[...] += dwv

    tap_out = pl.BlockSpec((3, W), lambda p, b: (0, p))
    return pl.pallas_call(
        body, grid=(P, B), name=name,
        in_specs=[pl.BlockSpec((2, S, W), lambda p, b: (0, b, p)),
                  pl.BlockSpec((None, S, W), lambda p, b: (0, b, p)),
                  pl.BlockSpec((3, W), lambda p, b: (0, p)),
                  pl.BlockSpec((3, W), lambda p, b: (0, P + p))],
        out_specs=[pl.BlockSpec((2, S, W), lambda p, b: (0, b, p)), tap_out, tap_out],
        out_shape=[jax.ShapeDtypeStruct((2, T, Fd), BF16),
                   jax.ShapeDtypeStruct((3, Fd), F32), jax.ShapeDtypeStruct((3, Fd), F32)],
        scratch_shapes=[pltpu.VMEM((S + 2 * HALO, W), F32)] * 4,
        compiler_params=_params(("parallel", "arbitrary")),
    )(up3, dact3, taps, taps)


def _final_norm_loss(x, g, target, tm, name):
    T, D = x.shape

    def body(x_ref, g_ref, t_ref, dx_ref, dg_ref, loss_ref):
        xv = x_ref[...]
        rstd = lax.rsqrt(jnp.mean(xv * xv, axis=-1, keepdims=True) + EPS)
        xn = xv * rstd
        err = xn * g_ref[...] - t_ref[...]
        part = 0.5 * jnp.sum(jnp.mean(err * err, axis=-1, keepdims=True), axis=0, keepdims=True)
        dy = err * (1.0 / D)
        a = dy * g_ref[...]
        dx_ref[...] = rstd * (a - xn * jnp.mean(a * xn, axis=-1, keepdims=True))
        dg = jnp.sum(dy * xn, axis=0, keepdims=True)
        lpart = jnp.broadcast_to(part, (1, LANES))

        @pl.when(pl.program_id(0) == 0)
        def _():
            dg_ref[...] = dg
            loss_ref[...] = lpart

        @pl.when(pl.program_id(0) != 0)
        def _():
            dg_ref[...] += dg
            loss_ref[...] += lpart

    row = pl.BlockSpec((tm, D), lambda i: (i, 0))
    return pl.pallas_call(
        body, grid=(T // tm,), name=name,
        in_specs=[row, pl.BlockSpec((1, D), lambda i: (0, 0)), row],
        out_specs=[row, pl.BlockSpec((1, D), lambda i: (0, 0)), pl.BlockSpec((1, LANES), lambda i: (0, 0))],
        out_shape=[jax.ShapeDtypeStruct((T, D), F32), jax.ShapeDtypeStruct((1, D), F32),
                   jax.ShapeDtypeStruct((1, LANES), F32)],
        compiler_params=_params(("arbitrary",)),
    )(x, g, target)


def _row_tile(rows, cols, budget_elems=512 * 1024):
    tr = rows
    while tr * cols > budget_elems and tr % 32 == 0:
        tr //= 2
    return tr


def _prefetch_call(body, grid, in_specs, out_specs, out_shape, name, sem, aliases=None):
    return pl.pallas_call(
        body, name=name, out_shape=out_shape,
        grid_spec=pltpu.PrefetchScalarGridSpec(num_scalar_prefetch=1, grid=grid, in_specs=in_specs,
                                               out_specs=out_specs),
        input_output_aliases=aliases or {},
        compiler_params=_params(sem))


def _cast_into_full(w, layer, colwise, where, name):
    _, K, N = w.shape
    tr = _row_tile(K, N)
    nrb = K // tr
    full_shape = (1, K, 4 * N) if colwise else (1, 4 * K, N)

    def body(where_ref, w_ref, o_ref):
        del where_ref
        o_ref[...] = w_ref[...].astype(BF16)

    if colwise:
        out_map = lambda i, wh: (0, i, wh[0])
    else:
        out_map = lambda i, wh: (0, wh[0] * nrb + i, 0)
    return _prefetch_call(
        body, (nrb,), [pl.BlockSpec((None, tr, N), lambda i, wh: (layer, i, 0))],
        pl.BlockSpec((None, tr, N), out_map), jax.ShapeDtypeStruct(full_shape, BF16), name,
        ("parallel",))(where, w)


def _chip_sum(g3, other, colwise, where, name):
    L, K, N = g3.shape
    hk, hn = (K // 2, N) if colwise else (K, N // 2)
    tr = _row_tile(hk, hn)
    nrb = hk // tr

    def body(where_ref, g_ref, o_ref, s_ref):
        del where_ref
        s_ref[...] = (g_ref[...].astype(F32) + o_ref[...].astype(F32)).astype(BF16)

    if colwise:
        g_map = lambda l, i, wh: (l, wh[1] * nrb + i, 0)
    else:
        g_map = lambda l, i, wh: (l, i, wh[1])
    blk = pl.BlockSpec((None, tr, hn), lambda l, i, wh: (l, i, 0))
    return _prefetch_call(
        body, (L, nrb), [pl.BlockSpec((None, tr, hn), g_map), blk], blk,
        jax.ShapeDtypeStruct((L, hk, hn), BF16), name, ("parallel", "parallel"))(where, g3, other)


def _owner_sum(chip_sum, received, colwise, where, layer, n_layers, prev, name):
    _, hk, hn = chip_sum.shape
    pk, pn = (hk, hn // 4) if colwise else (hk // 4, hn)
    tr = _row_tile(pk, pn)
    nrb = pk // tr
    shard_shape = (n_layers, 2 * pk, pn) if colwise else (n_layers, pk, 2 * pn)

    def body(where_ref, own_ref, rec_ref, *rest):
        del where_ref
        o_ref = rest[-1]
        acc = own_ref[...].astype(F32)
        for j in range(3):
            acc = acc + rec_ref[j].astype(F32)
        o_ref[...] = acc

    if colwise:
        own_map = lambda i, wh: (0, i, wh[0])
        out_map = lambda i, wh: (layer, wh[1] * nrb + i, 0)
    else:
        own_map = lambda i, wh: (0, wh[0] * nrb + i, 0)
        out_map = lambda i, wh: (layer, i, wh[1])
    in_specs = [pl.BlockSpec((None, tr, pn), own_map),
                pl.BlockSpec((3, None, tr, pn), lambda i, wh: (0, 0, i, 0))]
    operands = [where, chip_sum, received]
    if prev is not None:
        in_specs.append(ANY)
        operands.append(prev)
    return _prefetch_call(
        body, (nrb,), in_specs, pl.BlockSpec((None, tr, pn), out_map), jax.ShapeDtypeStruct(shard_shape, F32), name,
        ("parallel",), None if prev is None else {3: 0})(*operands)


def _adamw(w, g, m, v, name):
    R, Cc = w.shape
    tr = _row_tile(R, Cc, 256 * 1024)

    def body(w_ref, g_ref, m_ref, v_ref, d_ref, nm_ref, nv_ref, go_ref):
        gv = g_ref[...]
        go_ref[...] = gv
        nm = ADAM_B1 * m_ref[...] + (1.0 - ADAM_B1) * gv
        nv = ADAM_B2 * v_ref[...] + (1.0 - ADAM_B2) * (gv * gv)
        m_hat = nm / (1.0 - ADAM_B1 ** ADAM_STEP)
        v_hat = nv / (1.0 - ADAM_B2 ** ADAM_STEP)
        d_ref[...] = -ADAM_LR * (m_hat / (jnp.sqrt(v_hat) + ADAM_EPS) + ADAM_WD * w_ref[...])
        nm_ref[...] = nm
        nv_ref[...] = nv

    blk = pl.BlockSpec((tr, Cc), lambda i: (i, 0))
    shp = jax.ShapeDtypeStruct((R, Cc), F32)
    return pl.pallas_call(
        body, grid=(R // tr,), name=name,
        in_specs=[blk] * 4, out_specs=[blk] * 4, out_shape=[shp] * 4,
        compiler_params=_params(("parallel",)),
    )(w, g, m, v)


COL_SHARDED = (True, False, True, False)


def _position():
    x, y, c = lax.axis_index("x"), lax.axis_index("y"), lax.axis_index("c")
    chips = [(1 - x, y), (x, 1 - y), (1 - x, 1 - y)]
    return x, y, c, chips


def _span(index, size, align):
    return pl.ds(pl.multiple_of(index * size, align), size)


def _window(ref, colwise, shard, half, shards=4):
    _, K, N = ref.shape
    rows = cols = slice(None)
    if colwise:
        if half is not None:
            rows = _span(half, K // 2, 16)
        if shard is not None:
            cols = _span(shard, N // shards, LANES)
    else:
        if shard is not None:
            rows = _span(shard, K // shards, 16)
        if half is not None:
            cols = _span(half, N // 2, LANES)
    return ref.at[:, rows, cols]


HBM = pl.BlockSpec(memory_space=pltpu.HBM)
SEMAPHORES = pl.BlockSpec(memory_space=pltpu.SEMAPHORE)


def _gather_start(fulls, colwise, group_sizes, after, name):
    n = len(fulls)
    n_groups = len(group_sizes)

    n_in = n if after is None else n + 1

    def body(*refs):
        ins = refs[:n]
        sems = refs[n_in:n_in + 2 * n_groups]
        x, y, c, chips = _position()
        me = 2 * x + y
        i = 0
        for g, size in enumerate(group_sizes):
            for a in range(size):
                win = _window(ins[i], colwise[i], me, c)
                for j, chip in enumerate(chips):
                    pltpu.make_async_remote_copy(
                        src_ref=win, dst_ref=win, send_sem=sems[2 * g].at[a * 3 + j],
                        recv_sem=sems[2 * g + 1].at[a * 3 + j],
                        device_id=(chip[0], chip[1], c), device_id_type=MESH_ID).start()
                i += 1

    sem_shapes = []
    for size in group_sizes:
        sem_shapes += [pltpu.SemaphoreType.DMA((3 * size,)), pltpu.SemaphoreType.DMA((3 * size,))]
    operands = [pltpu.with_memory_space_constraint(f, pltpu.HBM) for f in fulls]
    in_specs = [HBM] * n
    if after is not None:
        operands.append(after)
        in_specs.append(ANY)
    outs = pl.pallas_call(
        body, name=name,
        in_specs=in_specs, out_specs=[SEMAPHORES] * (2 * n_groups) + [HBM] * n,
        out_shape=sem_shapes + [pltpu.HBM(f.shape, f.dtype) for f in fulls],
        input_output_aliases={i: 2 * n_groups + i for i in range(n)},
        compiler_params=pltpu.CompilerParams(has_side_effects=pltpu.SideEffectType.DATAFLOW_SIDE_EFFECTING),
    )(*operands)
    sems = [(outs[2 * g], outs[2 * g + 1]) for g in range(n_groups)]
    return sems, list(outs[2 * n_groups:])


def _to_sibling(ref, colwise, chip, half, x, y, c, send_sem, recv_sem):
    win = _window(ref, colwise, 2 * chip[0] + chip[1], half)
    return pltpu.make_async_remote_copy(
        src_ref=win, dst_ref=win, send_sem=send_sem, recv_sem=recv_sem,
        device_id=(x, y, 1 - c), device_id_type=MESH_ID)


def _gather_pass(in_flight, colwise, sems, after, name):
    n = len(in_flight)

    def body(*refs):
        ins = refs[:n]
        send_sems, recv_sems = refs[n], refs[n + 1]
        pass_send, pass_recv = refs[-2 - n], refs[-1 - n]
        x, y, c, chips = _position()
        me = 2 * x + y
        for a in range(n):
            for j, chip in enumerate(chips):
                k = a * 3 + j
                pltpu.make_async_remote_copy(
                    src_ref=_window(ins[a], colwise[a], me, c),
                    dst_ref=_window(ins[a], colwise[a], 2 * chip[0] + chip[1], c),
                    send_sem=send_sems.at[k], recv_sem=recv_sems.at[k],
                    device_id=(chip[0], chip[1], c), device_id_type=MESH_ID).wait()
                _to_sibling(ins[a], colwise[a], chip, c, x, y, c, pass_send.at[k], pass_recv.at[k]).start()

    operands = list(in_flight) + list(sems)
    in_specs = [HBM] * n + [SEMAPHORES] * 2
    if after is not None:
        operands.append(after)
        in_specs.append(ANY)
    outs = pl.pallas_call(
        body, name=name,
        in_specs=in_specs, out_specs=[SEMAPHORES] * 2 + [HBM] * n,
        out_shape=[pltpu.SemaphoreType.DMA((3 * n,)), pltpu.SemaphoreType.DMA((3 * n,))]
        + [pltpu.HBM(f.shape, f.dtype) for f in in_flight],
        input_output_aliases={i: 2 + i for i in range(n)},
        compiler_params=pltpu.CompilerParams(has_side_effects=pltpu.SideEffectType.DATAFLOW_SIDE_EFFECTING),
    )(*operands)
    return (outs[0], outs[1]), list(outs[2:])


def _gather_wait(in_flight, colwise, sems, after, name):
    n = len(in_flight)

    def body(*refs):
        ins = refs[:n]
        send_sems, recv_sems = refs[n], refs[n + 1]
        x, y, c, chips = _position()
        for a in range(n):
            for j, chip in enumerate(chips):
                k = a * 3 + j
                _to_sibling(ins[a], colwise[a], chip, c, x, y, c, send_sems.at[k], recv_sems.at[k]).wait_send()
                _to_sibling(ins[a], colwise[a], chip, 1 - c, x, y, c, send_sems.at[k], recv_sems.at[k]).wait_recv()

    operands = list(in_flight) + list(sems)
    in_specs = [HBM] * n + [SEMAPHORES] * 2
    if after is not None:
        operands.append(after)
        in_specs.append(ANY)
    outs = pl.pallas_call(
        body, name=name,
        in_specs=in_specs, out_specs=[HBM] * n,
        out_shape=[pltpu.HBM(f.shape, f.dtype) for f in in_flight],
        input_output_aliases={i: i for i in range(n)},
        compiler_params=pltpu.CompilerParams(has_side_effects=pltpu.SideEffectType.DATAFLOW_SIDE_EFFECTING),
    )(*operands)
    return list(outs)


def _exchange_copy(g_ref, land_ref, colwise, x, y, c, send_sem, recv_sem):
    return pltpu.make_async_remote_copy(
        src_ref=_window(g_ref, colwise, None, 1 - c), dst_ref=land_ref, send_sem=send_sem, recv_sem=recv_sem,
        device_id=(x, y, 1 - c), device_id_type=MESH_ID)


def _exchange_start(grads, colwise, name):
    n = len(grads)
    lands = []
    for g, cw in zip(grads, colwise):
        L, K, N = g.shape
        lands.append(lax.empty((L, K // 2, N) if cw else (L, K, N // 2), g.dtype))

    def body(*refs):
        src, land = refs[:n], refs[n:2 * n]
        send_sems, recv_sems = refs[2 * n], refs[2 * n + 1]
        x, y, c, _ = _position()
        for i in range(n):
            _exchange_copy(src[i], land[i], colwise[i], x, y, c, send_sems.at[i], recv_sems.at[i]).start()

    arrays = list(grads) + lands
    outs = pl.pallas_call(
        body, name=name,
        in_specs=[HBM] * (2 * n), out_specs=[SEMAPHORES] * 2 + [HBM] * (2 * n),
        out_shape=[pltpu.SemaphoreType.DMA((n,)), pltpu.SemaphoreType.DMA((n,))]
        + [pltpu.HBM(a.shape, a.dtype) for a in arrays],
        input_output_aliases={i: 2 + i for i in range(2 * n)},
        compiler_params=pltpu.CompilerParams(has_side_effects=pltpu.SideEffectType.DATAFLOW_SIDE_EFFECTING),
    )(*[pltpu.with_memory_space_constraint(a, pltpu.HBM) for a in arrays])
    return (outs[0], outs[1]), list(outs[2:2 + n]), list(outs[2 + n:])


def _exchange_wait(grads, lands, colwise, sems, after, name):
    n = len(grads)

    def body(*refs):
        src, land = refs[:n], refs[n:2 * n]
        send_sems, recv_sems = refs[2 * n], refs[2 * n + 1]
        x, y, c, _ = _position()
        for i in range(n):
            _exchange_copy(src[i], land[i], colwise[i], x, y, c, send_sems.at[i], recv_sems.at[i]).wait()

    arrays = list(grads) + list(lands)
    operands = arrays + list(sems)
    in_specs = [HBM] * (2 * n) + [SEMAPHORES] * 2
    if after is not None:
        operands.append(after)
        in_specs.append(ANY)
    outs = pl.pallas_call(
        body, name=name,
        in_specs=in_specs, out_specs=[HBM] * (2 * n),
        out_shape=[pltpu.HBM(a.shape, a.dtype) for a in arrays],
        input_output_aliases={i: i for i in range(2 * n)},
        compiler_params=pltpu.CompilerParams(has_side_effects=pltpu.SideEffectType.DATAFLOW_SIDE_EFFECTING),
    )(*operands)
    return list(outs[:n]), list(outs[n:])


def _scatter_copy(src_ref, land_ref, colwise, j, chip, c, send_sem, recv_sem):
    return pltpu.make_async_remote_copy(
        src_ref=_window(src_ref, colwise, 2 * chip[0] + chip[1], None), dst_ref=land_ref.at[j],
        send_sem=send_sem, recv_sem=recv_sem, device_id=(chip[0], chip[1], c), device_id_type=MESH_ID)


def _scatter_start(chip_sums, colwise, name):
    n = len(chip_sums)
    lands = []
    for g, cw in zip(chip_sums, colwise):
        L, hk, hn = g.shape
        lands.append(lax.empty((3, L, hk, hn // 4) if cw else (3, L, hk // 4, hn), g.dtype))

    def body(*refs):
        src, land = refs[:n], refs[n:2 * n]
        send_sems, recv_sems = refs[2 * n], refs[2 * n + 1]
        x, y, c, chips = _position()
        for i in range(n):
            for j, chip in enumerate(chips):
                _scatter_copy(src[i], land[i], colwise[i], j, chip, c, send_sems.at[i * 3 + j],
                              recv_sems.at[i * 3 + j]).start()

    arrays = list(chip_sums) + lands
    outs = pl.pallas_call(
        body, name=name,
        in_specs=[HBM] * (2 * n), out_specs=[SEMAPHORES] * 2 + [HBM] * (2 * n),
        out_shape=[pltpu.SemaphoreType.DMA((3 * n,)), pltpu.SemaphoreType.DMA((3 * n,))]
        + [pltpu.HBM(a.shape, a.dtype) for a in arrays],
        input_output_aliases={i: 2 + i for i in range(2 * n)},
        compiler_params=pltpu.CompilerParams(has_side_effects=pltpu.SideEffectType.DATAFLOW_SIDE_EFFECTING),
    )(*[pltpu.with_memory_space_constraint(a, pltpu.HBM) for a in arrays])
    return (outs[0], outs[1]), list(outs[2:2 + n]), list(outs[2 + n:])


def _scatter_wait(sources, lands, colwise, sems, after, name):
    n = len(sources)

    def body(*refs):
        src, land = refs[:n], refs[n:2 * n]
        send_sems, recv_sems = refs[2 * n], refs[2 * n + 1]
        x, y, c, chips = _position()
        for i in range(n):
            for j, chip in enumerate(chips):
                cp = _scatter_copy(src[i], land[i], colwise[i], j, chip, c, send_sems.at[i * 3 + j],
                                   recv_sems.at[i * 3 + j])
                cp.wait_send()
                cp.wait_recv()

    arrays = list(sources) + list(lands)
    operands = arrays + list(sems)
    in_specs = [HBM] * (2 * n) + [SEMAPHORES] * 2
    if after is not None:
        operands.append(after)
        in_specs.append(ANY)
    outs = pl.pallas_call(
        body, name=name,
        in_specs=in_specs, out_specs=[HBM] * (2 * n),
        out_shape=[pltpu.HBM(a.shape, a.dtype) for a in arrays],
        input_output_aliases={i: i for i in range(2 * n)},
        compiler_params=pltpu.CompilerParams(has_side_effects=pltpu.SideEffectType.DATAFLOW_SIDE_EFFECTING),
    )(*operands)
    return list(outs[:n]), list(outs[n:])


def _share_with_sibling(shards, colwise, name):
    n = len(shards)

    def body(*refs):
        out = refs[n:2 * n]
        send_sems, recv_sems = refs[2 * n:]
        x, y, c, _ = _position()

        def copy(i, half):
            win = _window(out[i], colwise[i], None, half)
            return pltpu.make_async_remote_copy(
                src_ref=win, dst_ref=win, send_sem=send_sems.at[i], recv_sem=recv_sems.at[i],
                device_id=(x, y, 1 - c), device_id_type=MESH_ID)

        for i in range(n):
            copy(i, c).start()
        for i in range(n):
            copy(i, 1 - c).wait_recv()
        for i in range(n):
            copy(i, c).wait_send()

    return pl.pallas_call(
        body, name=name,
        in_specs=[ANY] * n, out_specs=[ANY] * n,
        out_shape=[jax.ShapeDtypeStruct(s.shape, s.dtype) for s in shards],
        input_output_aliases={i: i for i in range(n)},
        scratch_shapes=[pltpu.SemaphoreType.DMA((n,)), pltpu.SemaphoreType.DMA((n,))],
    )(*shards)


def _all_reduce_small(pack, name):
    R, Cc = pack.shape
    n_dev = 8

    def body(p_ref, o_ref, buf, send_sems, recv_sems):
        x, y, c, _ = _position()
        me = 4 * x + 2 * y + c
        buf[me] = p_ref[...]

        def peer(k):
            px = 1 - x if k & 4 else x
            py = 1 - y if k & 2 else y
            pc = 1 - c if k & 1 else c
            return px, py, pc

        def copy(k, incoming):
            px, py, pc = peer(k)
            slot = (4 * px + 2 * py + pc) if incoming else me
            return pltpu.make_async_remote_copy(
                src_ref=p_ref, dst_ref=buf.at[slot], send_sem=send_sems.at[k], recv_sem=recv_sems.at[k],
                device_id=(px, py, pc), device_id_type=MESH_ID)

        for k in range(1, n_dev):
            copy(k, False).start()
        for k in range(1, n_dev):
            copy(k, True).wait_recv()
        for k in range(1, n_dev):
            copy(k, False).wait_send()
        acc = buf[0]
        for j in range(1, n_dev):
            acc = acc + buf[j]
        o_ref[...] = acc

    vmem = pl.BlockSpec(memory_space=pltpu.VMEM)
    return pl.pallas_call(
        body, name=name,
        in_specs=[vmem], out_specs=vmem, out_shape=jax.ShapeDtypeStruct((R, Cc), F32),
        scratch_shapes=[pltpu.VMEM((n_dev, R, Cc), F32), pltpu.SemaphoreType.DMA((n_dev,)),
                        pltpu.SemaphoreType.DMA((n_dev,))],
    )(pack)


def _local_forward_backward(x2, target2, S, pass_on, fetch, reduce_begin, reduce_commit, layers, final_g, tm=512):
    T, D = x2.shape
    C = D // 2
    n_heads = C // GROUP
    n_layers = len(layers)
    weights = {}
    saved = []
    xc = x2
    for li, lw in enumerate(layers):
        if li == 0:
            pass_on(0, None)
            weights.update(fetch(0, None))
        h1, qkv3, cv3 = _norm_proj(xc, lw["norm1"], weights[li, "w_in"], 0, ((3, C, F32), (3, C, BF16)), tm,
                                   min(C, 512), f"l{li}_norm_in_proj")
        if li == 0:
            pass_on(1, h1)
        o, lse, mix = _attn_fwd(qkv3, lw["attn_g"], 2, S, n_heads, f"l{li}_attn_fwd")
        pin = None
        if li == 0:
            weights.update(fetch(1, o))
            pin = pass_on(3, pass_on(2, o))
        mix = _mix_conv_fwd(cv3, lw["taps"], lw["conv_g"], mix, S, f"l{li}_mix_conv_fwd", pin)
        x_mid = _proj_residual(mix, weights[li, "w_out"], 0, xc, tm, f"l{li}_out_proj")
        if li == 0:
            weights.update(fetch(2, x_mid))
        Fd = weights[li, "ffn_up"].shape[2] // 2
        h2, up3 = _norm_proj(x_mid, lw["norm2"], weights[li, "ffn_up"], 0, ((2, Fd, BF16),), tm // 2, 256,
                             f"l{li}_norm_ffn_up")
        if li == 0:
            weights.update(fetch(3, up3))
        act = _ffn_act_fwd(up3, lw["ffn_taps"], S, f"l{li}_ffn_act_fwd")
        pin = pass_on(li + 4, act) if li + 1 < n_layers else None
        x_out = _proj_residual(act.reshape(1, T, Fd), weights[li, "ffn_down"], 0, x_mid, tm, f"l{li}_ffn_down", pin)
        if li + 1 < n_layers:
            weights.update(fetch(li + 4, x_out))
        saved.append(dict(x_in=xc, h1=h1, qkv3=qkv3, cv3=cv3, o=o, lse=lse, mix=mix, x_mid=x_mid, h2=h2, up3=up3,
                          act=act))
        xc = x_out

    dx, d_final_g, loss_part = _final_norm_loss(xc, final_g, target2, tm, "final_norm_loss")

    small = [None] * n_layers
    started = None
    for li in reversed(range(n_layers)):
        lw, sv = layers[li], saved[li]
        w_in, w_out, ffn_up, ffn_down = (weights[li, n] for n in ("w_in", "w_out", "ffn_up", "ffn_down"))
        dxb, dact3 = _grad_through_weight(dx, ffn_down, 0, 1, Fd, BF16, tm, 256, f"l{li}_d_act", started)
        Fd = ffn_down.shape[1]
        d_ffn_down = _weight_grad(sv["act"].reshape(1, T, Fd), dxb.reshape(1, T, D), 256, D, f"l{li}_d_ffn_down")
        dup3, d_taps_g, d_taps_v = _ffn_act_bwd(sv["up3"], dact3, lw["ffn_taps"], S, f"l{li}_ffn_act_bwd")
        d_ffn_up = _weight_grad(sv["h2"].reshape(1, D, T), dup3, D, 256, f"l{li}_d_ffn_up", a_transposed=True)
        if li == 0:
            early = reduce_begin({(li, "ffn_down"): d_ffn_down, (li, "ffn_up"): d_ffn_up})
        dx_mid, d_norm2 = _grad_through_proj_norm(dup3, ffn_up, 0, sv["x_mid"], lw["norm2"], dx, tm // 2,
                                                  f"l{li}_d_norm2")
        started = reduce_commit(early, dx_mid) if li == 0 else None
        dxmb, dmix3 = _grad_through_weight(dx_mid, w_out, 0, 2, C, F32, tm, min(C, 512), f"l{li}_d_mix", started)
        d_w_out = _weight_grad(sv["mix"], dxmb.reshape(1, T, D), min(C, 256), D, f"l{li}_d_w_out")
        dproj, d_attn_g = _attn_bwd(sv["qkv3"], sv["o"], sv["lse"], dmix3, lw["attn_g"], 6, S, n_heads,
                                    f"l{li}_attn_bwd")
        dproj, d_taps, d_conv_g = _mix_conv_bwd(sv["cv3"], dmix3, lw["taps"], lw["conv_g"], dproj, S,
                                                f"l{li}_mix_conv_bwd")
        d_w_in = _weight_grad(sv["h1"].reshape(1, D, T), dproj, D, C, f"l{li}_d_w_in", a_transposed=True)
        late = {(li, "w_out"): d_w_out, (li, "w_in"): d_w_in}
        if li > 0:
            late.update({(li, "ffn_down"): d_ffn_down, (li, "ffn_up"): d_ffn_up})
        late = reduce_begin(late)
        dx, d_norm1 = _grad_through_proj_norm(dproj, w_in, 0, sv["x_in"], lw["norm1"], dx_mid, tm,
                                              f"l{li}_d_norm1")
        started = reduce_commit(late, dx)
        small[li] = dict(norm1=d_norm1, taps=d_taps, attn_g=d_attn_g, conv_g=d_conv_g, norm2=d_norm2,
                         ffn_taps=jnp.concatenate([d_taps_g, d_taps_v], axis=1))
    return loss_part, dx, small, d_final_g


SMALL_ORDER = ("norm1", "attn_g", "conv_g", "norm2", "taps", "ffn_taps")


def _pack_small(small, d_final_g, loss_row):
    parts = [small[li][k].reshape(-1) for li in range(len(small)) for k in SMALL_ORDER]
    loss_rows = jnp.tile(loss_row.reshape(1, LANES), (8, 1))
    return jnp.concatenate(parts + [d_final_g.reshape(-1), loss_rows.reshape(-1)]).reshape(-1, LANES)


def _unpack_small(pack, small, d_final_g):
    flat = pack.reshape(-1)
    out, pos = [dict() for _ in small], 0
    for li in range(len(small)):
        for k in SMALL_ORDER:
            n = small[li][k].size
            out[li][k] = flat[pos:pos + n].reshape(small[li][k].shape)
            pos += n
    return out, flat[pos:pos + d_final_g.size], flat[pos + d_final_g.size]


def kernel(x, norm1_g, w_in, mix_conv_w, attn_out_g, conv_out_g, w_out, norm2_g, ffn_up, ffn_conv_w, ffn_down, final_norm_g, loss_target, m_norm1_g, m_w_in, m_mix_conv_w, m_attn_out_g, m_conv_out_g, m_w_out, m_norm2_g, m_ffn_up, m_ffn_conv_w, m_ffn_down, m_final_norm_g, v_norm1_g, v_w_in, v_mix_conv_w, v_attn_out_g, v_conv_out_g, v_w_out, v_norm2_g, v_ffn_up, v_ffn_conv_w, v_ffn_down, v_final_norm_g):
    Bl, S, D = x.shape
    L = w_in.shape[0]
    T = Bl * S
    shard = 2 * lax.axis_index("x") + lax.axis_index("y")
    where = jnp.stack([shard, lax.axis_index("c")]).astype(jnp.int32)
    big_names = ("w_in", "w_out", "ffn_up", "ffn_down")

    taps_w, ftaps_w = mix_conv_w.shape[2], ffn_conv_w.shape[2]
    taps_full = jnp.zeros((L, 3, 4 * taps_w), F32)
    taps_full = lax.dynamic_update_slice(taps_full, mix_conv_w, (0, 0, shard * taps_w))
    ftaps_full = jnp.zeros((L, 3, 4 * ftaps_w), F32)
    ftaps_full = lax.dynamic_update_slice(ftaps_full, ffn_conv_w, (0, 0, shard * ftaps_w))
    tap_pack = jnp.concatenate([taps_full.reshape(-1), ftaps_full.reshape(-1)]).reshape(-1, LANES)
    tap_pack = _all_reduce_small(tap_pack * 0.5, "all_gather_taps")
    n_taps = taps_full.size
    taps_full = tap_pack.reshape(-1)[:n_taps].reshape(taps_full.shape)
    ftaps_full = tap_pack.reshape(-1)[n_taps:].reshape(ftaps_full.shape)

    big_shards = dict(zip(big_names, (w_in, w_out, ffn_up, ffn_down)))
    col_of = dict(zip(big_names, COL_SHARDED))
    groups = [[(0, n)] for n in big_names] + [[(l, n) for n in big_names] for l in range(1, L)]
    sems, in_flight = [], {}
    all_started = tap_pack
    for first, last in ((0, 1), (1, len(groups))):
        keys = [k for g in groups[first:last] for k in g]
        new_sems, arrays = _gather_start(
            [_cast_into_full(big_shards[n], l, col_of[n], where, f"cast_{n}_{l}") for l, n in keys],
            [col_of[n] for _, n in keys], [len(g) for g in groups[first:last]], all_started,
            f"gather_start_{first}")
        sems += new_sems
        in_flight.update(zip(keys, arrays))
        all_started = arrays[-1]

    def pass_on(g, after):
        after = all_started if g == 0 else after
        sems[g], arrays = _gather_pass([in_flight[k] for k in groups[g]], [col_of[n] for _, n in groups[g]],
                                       sems[g], after, f"gather_pass_{g}")
        in_flight.update(zip(groups[g], arrays))
        return arrays[0]

    def fetch(g, after):
        done = _gather_wait([in_flight[k] for k in groups[g]], [col_of[n] for _, n in groups[g]], sems[g], after,
                            f"gather_wait_{g}")
        return dict(zip(groups[g], done))

    pending = []

    begun = []

    def reduce_begin(grads):
        g = len(begun)
        keys = list(grads)
        cols = [col_of[n] for _, n in keys]
        begun.append((g, keys, cols) + _exchange_start([grads[k] for k in keys], cols, f"exchange_start_{g}"))
        return begun[-1]

    def reduce_commit(handle, after):
        g, keys, cols, ex_sems, mine, lands = handle
        mine, others = _exchange_wait(mine, lands, cols, ex_sems, after, f"exchange_wait_{g}")
        chip_sums = [_chip_sum(m, o, cw, where, f"chip_sum_{k[1]}_{k[0]}")
                     for k, m, o, cw in zip(keys, mine, others, cols)]
        pending.append((keys, cols) + _scatter_start(chip_sums, cols, f"scatter_start_{g}"))
        return pending[-1][3][0]

    layers = [dict(norm1=norm1_g[l:l + 1], taps=taps_full[l], attn_g=attn_out_g[l:l + 1],
                   conv_g=conv_out_g[l:l + 1], norm2=norm2_g[l:l + 1], ffn_taps=ftaps_full[l]) for l in range(L)]

    loss_part, dx, small, d_final_g = _local_forward_backward(
        x.reshape(T, D), loss_target.reshape(T, D), S, pass_on, fetch, reduce_begin, reduce_commit, layers,
        final_norm_g.reshape(1, D))

    def finish_group(g, after):
        keys, cols, rs_sems, sources, lands = pending[g]
        sources, lands = _scatter_wait(sources, lands, cols, rs_sems, after, f"scatter_wait_{g}")
        for (l, n), cw, src, land in zip(keys, cols, sources, lands):
            reduced[n] = _owner_sum(src, land, cw, where, l, L, reduced[n], f"owner_sum_{n}_{l}")

    reduced = dict.fromkeys(big_names)
    last_started = pending[-1][3][0]
    for g in range(len(pending) - 1):
        finish_group(g, last_started)
    late_names = [n for n in big_names if any(n == name for _, name in pending[-1][0])]
    early_names = [n for n in big_names if n not in late_names]
    g_big = dict(zip(early_names, _share_with_sibling([reduced[n] for n in early_names],
                                                      [col_of[n] for n in early_names], "grad_share_early")))

    pack = _all_reduce_small(_pack_small(small, d_final_g, loss_part), "all_reduce_small_grads")
    g_small, g_final, loss = _unpack_small(pack, small, d_final_g)

    def stacked(key):
        return jnp.stack([g_small[l][key].reshape(g_small[l][key].shape[-2:] if key.endswith("taps") else (-1,))
                          for l in range(L)])

    g_norm1, g_attn, g_conv, g_norm2 = stacked("norm1"), stacked("attn_g"), stacked("conv_g"), stacked("norm2")
    g_taps = lax.dynamic_slice(stacked("taps"), (0, 0, shard * taps_w), (L, 3, taps_w))
    g_ftaps = lax.dynamic_slice(stacked("ffn_taps"), (0, 0, shard * ftaps_w), (L, 3, ftaps_w))

    grads_out = dict(norm1_g=g_norm1, w_in=None, mix_conv_w=g_taps, attn_out_g=g_attn, conv_out_g=g_conv,
                     w_out=None, norm2_g=g_norm2, ffn_up=None, ffn_conv_w=g_ftaps, ffn_down=None,
                     final_norm_g=g_final)
    weights = dict(norm1_g=norm1_g, w_in=w_in, mix_conv_w=mix_conv_w, attn_out_g=attn_out_g, conv_out_g=conv_out_g,
                   w_out=w_out, norm2_g=norm2_g, ffn_up=ffn_up, ffn_conv_w=ffn_conv_w, ffn_down=ffn_down,
                   final_norm_g=final_norm_g)
    ms = dict(norm1_g=m_norm1_g, w_in=m_w_in, mix_conv_w=m_mix_conv_w, attn_out_g=m_attn_out_g,
              conv_out_g=m_conv_out_g, w_out=m_w_out, norm2_g=m_norm2_g, ffn_up=m_ffn_up, ffn_conv_w=m_ffn_conv_w,
              ffn_down=m_ffn_down, final_norm_g=m_final_norm_g)
    vs = dict(norm1_g=v_norm1_g, w_in=v_w_in, mix_conv_w=v_mix_conv_w, attn_out_g=v_attn_out_g,
              conv_out_g=v_conv_out_g, w_out=v_w_out, norm2_g=v_norm2_g, ffn_up=v_ffn_up, ffn_conv_w=v_ffn_conv_w,
              ffn_down=v_ffn_down, final_norm_g=v_final_norm_g)
    names = list(weights)
    small_names = [n for n in names if n not in big_names]
    delta, new_m, new_v = {}, {}, {}

    def update_big(n):
        shp = weights[n].shape
        two_d = (shp[0] * shp[1], shp[2])
        d_, m_, v_, g_ = _adamw(weights[n].reshape(two_d), g_big[n].reshape(two_d), ms[n].reshape(two_d),
                                vs[n].reshape(two_d), f"adamw_{n}")
        delta[n], new_m[n], new_v[n], grads_out[n] = (a.reshape(shp) for a in (d_, m_, v_, g_))

    for n in early_names:
        update_big(n)
    finish_group(len(pending) - 1, delta[early_names[-1]] if early_names else None)
    g_big.update(zip(late_names, _share_with_sibling([reduced[n] for n in late_names],
                                                     [col_of[n] for n in late_names], "grad_share_late")))
    for n in late_names:
        update_big(n)

    def packed(tree):
        return jnp.concatenate([tree[n].reshape(-1) for n in small_names]).reshape(-1, LANES)

    d_, m_, v_, _ = _adamw(packed(weights), packed(grads_out), packed(ms), packed(vs), "adamw_small")
    pos = 0
    for n in small_names:
        size, shp = weights[n].size, weights[n].shape
        delta[n] = d_.reshape(-1)[pos:pos + size].reshape(shp)
        new_m[n] = m_.reshape(-1)[pos:pos + size].reshape(shp)
        new_v[n] = v_.reshape(-1)[pos:pos + size].reshape(shp)
        pos += size

    return (loss, dx.reshape(Bl, S, D), *[grads_out[n] for n in names], *[delta[n] for n in names],
            *[new_m[n] for n in names], *[new_v[n] for n in names])
```

```python
import functools
import math

import jax
import jax.numpy as jnp
from jax import lax
from jax.experimental import pallas as pl
from jax.experimental.pallas import tpu as pltpu

F32 = jnp.float32
BF16 = jnp.bfloat16
EPS = 1e-6
GROUP = 64
LANES = 128
BAND = 128
DILATIONS = (1, 4, 16)
NEG = -1e30
MIB = 1024 * 1024
MESH_ID = pl.DeviceIdType.MESH

ADAM_LR = 0.001
ADAM_B1 = 0.9
ADAM_B2 = 0.999
ADAM_EPS = 1e-08
ADAM_WD = 0.01
ADAM_STEP = 10


ANY = pl.BlockSpec(memory_space=pl.ANY)


def _params(sem=None, vmem_mb=48):
    return pltpu.CompilerParams(dimension_semantics=sem, vmem_limit_bytes=vmem_mb * MIB)


def _nt(a, b):
    return lax.dot_general(a, b, (((1,), (1,)), ((), ())), preferred_element_type=F32)


def _tn(a, b):
    return lax.dot_general(a, b, (((0,), (0,)), ((), ())), preferred_element_type=F32)


def _seg_sum(x, is_a):
    s_a = jnp.sum(jnp.where(is_a, x, 0.0), axis=-1, keepdims=True)
    s_b = jnp.sum(jnp.where(is_a, 0.0, x), axis=-1, keepdims=True)
    return jnp.where(is_a, s_a, s_b)


def _lane_is_a():
    return lax.broadcasted_iota(jnp.int32, (1, LANES), 1) < GROUP


def _norm_proj(x, g, w3, layer, groups, tm, chunk, name):
    T, D = x.shape
    N = w3.shape[2]
    assert sum(p * c for p, c, _ in groups) == N and T % tm == 0

    def body(x_ref, g_ref, w_ref, h_ref, *out_refs):
        xv = x_ref[...]
        rstd = lax.rsqrt(jnp.mean(xv * xv, axis=-1, keepdims=True) + EPS)
        h = ((xv * rstd) * g_ref[...]).astype(BF16)
        h_ref[...] = h.T
        col = 0
        for (pieces, width, dtype), o_ref in zip(groups, out_refs):
            for p in range(pieces):
                for c0 in range(0, width, chunk):
                    acc = jnp.dot(h, w_ref[:, col + c0:col + c0 + chunk], preferred_element_type=F32)
                    o_ref[p, :, c0:c0 + chunk] = acc.astype(dtype)
                col += width

    out_shape = [jax.ShapeDtypeStruct((D, T), BF16)]
    out_specs = [pl.BlockSpec((D, tm), lambda i: (0, i))]
    for pieces, width, dtype in groups:
        assert width % chunk == 0
        out_shape.append(jax.ShapeDtypeStruct((pieces, T, width), dtype))
        out_specs.append(pl.BlockSpec((pieces, tm, width), lambda i: (0, i, 0)))
    return pl.pallas_call(
        body, grid=(T // tm,), name=name,
        in_specs=[pl.BlockSpec((tm, D), lambda i: (i, 0)),
                  pl.BlockSpec((1, D), lambda i: (0, 0)),
                  pl.BlockSpec((None, D, N), lambda i: (layer, 0, 0))],
        out_specs=out_specs, out_shape=out_shape,
        compiler_params=_params(("parallel",), 56),
    )(x, g, w3)


def _proj_residual(pieces3, w3, layer, x, tm, name, after=None):
    P, T, C = pieces3.shape
    D = w3.shape[2]

    def body(a_ref, w_ref, x_ref, *rest):
        o_ref = rest[-1]
        acc = x_ref[...]
        for p in range(P):
            acc = acc + jnp.dot(a_ref[p], w_ref[p * C:(p + 1) * C, :], preferred_element_type=F32)
        o_ref[...] = acc

    in_specs = [pl.BlockSpec((P, tm, C), lambda i: (0, i, 0)),
                pl.BlockSpec((None, P * C, D), lambda i: (layer, 0, 0)),
                pl.BlockSpec((tm, D), lambda i: (i, 0))]
    operands = [pieces3, w3, x]
    if after is not None:
        in_specs.append(ANY)
        operands.append(after)
    return pl.pallas_call(
        body, grid=(T // tm,), name=name,
        in_specs=in_specs,
        out_specs=pl.BlockSpec((tm, D), lambda i: (i, 0)),
        out_shape=jax.ShapeDtypeStruct((T, D), F32),
        compiler_params=_params(("parallel",)),
    )(*operands)


def _grad_through_weight(dy, w3, layer, pieces, width, out_dtype, tm, chunk, name, after=None):
    T, D = dy.shape

    def body(dy_ref, w_ref, *rest):
        dyb_ref, o_ref = rest[-2:]
        dyb = dy_ref[...].astype(BF16)
        dyb_ref[...] = dyb
        for p in range(pieces):
            for c0 in range(0, width, chunk):
                r0 = p * width + c0
                o_ref[p, :, c0:c0 + chunk] = _nt(dyb, w_ref[r0:r0 + chunk, :]).astype(out_dtype)

    in_specs = [pl.BlockSpec((tm, D), lambda i: (i, 0)),
                pl.BlockSpec((None, pieces * width, D), lambda i: (layer, 0, 0))]
    operands = [dy, w3]
    if after is not None:
        in_specs.append(ANY)
        operands.append(after)
    return pl.pallas_call(
        body, grid=(T // tm,), name=name,
        in_specs=in_specs,
        out_specs=[pl.BlockSpec((tm, D), lambda i: (i, 0)),
                   pl.BlockSpec((pieces, tm, width), lambda i: (0, i, 0))],
        out_shape=[jax.ShapeDtypeStruct((T, D), BF16),
                   jax.ShapeDtypeStruct((pieces, T, width), out_dtype)],
        compiler_params=_params(("parallel",)),
    )(*operands)


def _grad_through_proj_norm(dp3, w3, layer, x, g, dx_in, tm, name):
    P, T, C = dp3.shape
    D = w3.shape[1]

    def body(dp_ref, w_ref, x_ref, g_ref, dxin_ref, dx_ref, dg_ref):
        dh = _nt(dp_ref[0], w_ref[:, 0:C])
        for p in range(1, P):
            dh = dh + _nt(dp_ref[p], w_ref[:, p * C:(p + 1) * C])
        xv = x_ref[...]
        rstd = lax.rsqrt(jnp.mean(xv * xv, axis=-1, keepdims=True) + EPS)
        xn = xv * rstd
        a = dh * g_ref[...]
        dx_ref[...] = dxin_ref[...] + rstd * (a - xn * jnp.mean(a * xn, axis=-1, keepdims=True))
        part = jnp.sum(dh * xn, axis=0, keepdims=True)

        @pl.when(pl.program_id(0) == 0)
        def _():
            dg_ref[...] = part

        @pl.when(pl.program_id(0) != 0)
        def _():
            dg_ref[...] += part

    return pl.pallas_call(
        body, grid=(T // tm,), name=name,
        in_specs=[pl.BlockSpec((P, tm, C), lambda i: (0, i, 0)),
                  pl.BlockSpec((None, D, P * C), lambda i: (layer, 0, 0)),
                  pl.BlockSpec((tm, D), lambda i: (i, 0)),
                  pl.BlockSpec((1, D), lambda i: (0, 0)),
                  pl.BlockSpec((tm, D), lambda i: (i, 0))],
        out_specs=[pl.BlockSpec((tm, D), lambda i: (i, 0)),
                   pl.BlockSpec((1, D), lambda i: (0, 0))],
        out_shape=[jax.ShapeDtypeStruct((T, D), F32), jax.ShapeDtypeStruct((1, D), F32)],
        compiler_params=_params(("arbitrary",), 56),
    )(dp3, w3, x, g, dx_in)


def _weight_grad(a3, g3, ta, tg, name, a_transposed=False):
    PG, T, CG = g3.shape
    PA, CA = (a3.shape[0], a3.shape[1]) if a_transposed else (a3.shape[0], a3.shape[2])
    na, ng = CA // ta, CG // tg
    assert CA % ta == 0 and CG % tg == 0

    def body(a_ref, g_ref, o_ref):
        if a_transposed:
            part = jnp.dot(a_ref[...], g_ref[...], preferred_element_type=F32)
        else:
            part = _tn(a_ref[...], g_ref[...])
        o_ref[...] = part.astype(o_ref.dtype)

    a_spec = (pl.BlockSpec((None, ta, T), lambda i, j: (i // na, i % na, 0)) if a_transposed
              else pl.BlockSpec((None, T, ta), lambda i, j: (i // na, 0, i % na)))
    return pl.pallas_call(
        body, grid=(PA * na, PG * ng), name=name,
        in_specs=[a_spec, pl.BlockSpec((None, T, tg), lambda i, j: (j // ng, 0, j % ng))],
        out_specs=pl.BlockSpec((None, ta, tg), lambda i, j: (0, i, j)),
        out_shape=jax.ShapeDtypeStruct((1, PA * CA, PG * CG), BF16),
        compiler_params=_params(("parallel", "parallel"), 56),
    )(a3, g3)


def _bias_tables(bm_ref, lone_ref, pair, n_heads, S):
    ii = lax.broadcasted_iota(jnp.int32, (BAND, 2 * BAND), 0)
    jj = lax.broadcasted_iota(jnp.int32, (BAND, 2 * BAND), 1)
    dist = BAND + ii - jj
    valid = (dist >= 0) & (dist <= BAND)
    distf = dist.astype(F32)
    for hh in range(2):
        head = (2 * pair + hh + 1).astype(F32)
        slope = jnp.exp(jnp.full((1, 1), -8.0 / n_heads * math.log(2.0), F32) * head)
        for bi, d in enumerate(DILATIONS):
            table = jnp.where(valid, -(slope * d) * distf, NEG)
            bm_ref[bi, hh * BAND:(hh + 1) * BAND, :] = table
            if S // (BAND * d) == 1:
                lone_ref[bi, hh * BAND:(hh + 1) * BAND, :] = table[:, BAND:2 * BAND]


def _stack_heads(x, is_a):
    zero = jnp.zeros_like(x)
    return jnp.concatenate([jnp.where(is_a, x, zero), jnp.where(is_a, zero, x)], axis=0)


def _unstack_heads(x2, is_a):
    return jnp.where(is_a, x2[0:BAND], x2[BAND:2 * BAND])


def _gather_residues(dst_ref, src, d, S, convert):
    L = S // d
    for r in range(d):
        rows = pl.ds(r, L, stride=d) if d > 1 else slice(None)
        dst_ref[r * L:(r + 1) * L, :] = convert(src(rows))


def _block_rows(t, d, S):
    nb = S // (BAND * d)
    n = t % nb
    has_prev = jnp.minimum(n, 1)
    cur = pl.ds(pl.multiple_of(t * BAND, BAND), BAND)
    prev = pl.ds(pl.multiple_of((t - has_prev) * BAND, BAND), BAND)
    return cur, prev, has_prev


def _first_block_penalty(has_prev):
    jrow = lax.broadcasted_iota(jnp.int32, (1, 2 * BAND), 1)
    pen = jnp.where(has_prev == 0, NEG, 0.0).astype(F32)
    return jnp.where(jrow < BAND, pen, 0.0)


def _attn_fwd(qkv3, gain, mix_shape_pieces, S, n_heads, name):
    _, T, C = qkv3.shape
    B, P = T // S, C // LANES
    NBLK = S // BAND
    scale = GROUP ** -0.5
    nbr = len(DILATIONS)
    RC = 256

    def body(qkv_ref, g_ref, o_ref, lse_ref, an_ref, qs, ks, vs, op, mp, lp, ob, mb, lb, bm, bml):
        pair = pl.program_id(1)
        is_a = _lane_is_a()
        _bias_tables(bm, bml, pair, n_heads, S)

        for bi, d in enumerate(DILATIONS):
            nb = S // (BAND * d)
            _gather_residues(qs, lambda rows: qkv_ref.at[0][rows, :], d, S, lambda v: (v * scale).astype(BF16))
            _gather_residues(ks, lambda rows: qkv_ref.at[1][rows, :], d, S, lambda v: v.astype(BF16))
            _gather_residues(vs, lambda rows: qkv_ref.at[2][rows, :], d, S, lambda v: v.astype(BF16))
            o_dst, m_dst, l_dst = (ob.at[bi], mb.at[bi], lb.at[bi]) if d == 1 else (op, mp, lp)

            def block(t, carry, bi=bi, d=d, nb=nb, o_dst=o_dst, m_dst=m_dst, l_dst=l_dst):
                cur, prev, has_prev = _block_rows(t, d, S)
                q2 = _stack_heads(qs[cur, :], is_a)
                if nb == 1:
                    kc, vc = ks[cur, :], vs[cur, :]
                    s = _nt(q2, kc) + bml[bi]
                else:
                    kc = jnp.concatenate([ks[prev, :], ks[cur, :]], axis=0)
                    vc = jnp.concatenate([vs[prev, :], vs[cur, :]], axis=0)
                    s = _nt(q2, kc) + bm[bi] + _first_block_penalty(has_prev)
                m = jnp.max(s, axis=-1, keepdims=True)
                e = jnp.exp(s - m)
                l = jnp.sum(e, axis=-1, keepdims=True)
                pv = jnp.dot(e.astype(BF16), vc, preferred_element_type=F32)
                o_dst[cur, :] = _unstack_heads(pv, is_a)
                m_dst[cur, :] = _unstack_heads(m, is_a)
                l_dst[cur, :] = _unstack_heads(l, is_a)
                return carry

            lax.fori_loop(0, NBLK, block, 0, unroll=8)
            if d > 1:
                L = S // d
                for r in range(d):
                    rows = pl.ds(r, L, stride=d)
                    ob.at[bi][rows, :] = op[r * L:(r + 1) * L, :]
                    mb.at[bi][rows, :] = mp[r * L:(r + 1) * L, :]
                    lb.at[bi][rows, :] = lp[r * L:(r + 1) * L, :]

        def finish(ci, carry):
            rs = pl.ds(pl.multiple_of(ci * RC, RC), RC)
            ms = [mb[bi, rs, :] for bi in range(nbr)]
            mmax = functools.reduce(jnp.maximum, ms)
            ws = [jnp.exp(m - mmax) for m in ms]
            num = sum(ob[bi, rs, :] * ws[bi] for bi in range(nbr))
            den = sum(lb[bi, rs, :] * ws[bi] for bi in range(nbr))
            o = num / den
            o_ref[rs, :] = o
            lse_ref[rs, :] = mmax + jnp.log(den)
            rstd = lax.rsqrt(_seg_sum(o * o, is_a) * (1.0 / GROUP) + EPS)
            an_ref[rs, :] = ((o * rstd) * g_ref[...]).astype(BF16)
            return carry

        lax.fori_loop(0, S // RC, finish, 0)

    seq = pl.BlockSpec((S, LANES), lambda b, p: (b, p))
    return pl.pallas_call(
        body, grid=(B, P), name=name,
        in_specs=[pl.BlockSpec((3, S, LANES), lambda b, p: (0, b, p)),
                  pl.BlockSpec((1, LANES), lambda b, p: (0, p))],
        out_specs=[seq, seq, pl.BlockSpec((None, S, LANES), lambda b, p: (0, b, p))],
        out_shape=[jax.ShapeDtypeStruct((T, C), F32), jax.ShapeDtypeStruct((T, C), F32),
                   jax.ShapeDtypeStruct((mix_shape_pieces, T, C), BF16)],
        scratch_shapes=[pltpu.VMEM((S, LANES), BF16)] * 3 + [pltpu.VMEM((S, LANES), F32)] * 3
        + [pltpu.VMEM((nbr, S, LANES), F32)] * 3
        + [pltpu.VMEM((nbr, 2 * BAND, 2 * BAND), F32), pltpu.VMEM((nbr, 2 * BAND, BAND), F32)],
        compiler_params=_params(("parallel", "parallel")),
    )(qkv3, gain)


def _attn_bwd(qkv3, o, lse, dmix3, gain, dproj_pieces, S, n_heads, name):
    _, T, C = qkv3.shape
    B, P = T // S, C // LANES
    NBLK = S // BAND
    scale = GROUP ** -0.5
    nbr = len(DILATIONS)
    RC = 256

    def body(qkv_ref, o_ref, lse_ref, dn_ref, g_ref, dqkv_ref, dg_ref,
             do_n, dd_n, qs, ks, vs, dos, lses, dds, dqp, dkp, dvp, dqn, dkn, dvn, bm, bml):
        pair = pl.program_id(0)
        b = pl.program_id(1)
        is_a = _lane_is_a()
        _bias_tables(bm, bml, pair, n_heads, S)

        def prologue(ci, dg_acc):
            rs = pl.ds(pl.multiple_of(ci * RC, RC), RC)
            ov = o_ref[rs, :]
            dn = dn_ref[rs, :]
            rstd = lax.rsqrt(_seg_sum(ov * ov, is_a) * (1.0 / GROUP) + EPS)
            on = ov * rstd
            a = dn * g_ref[...]
            do = rstd * (a - on * (_seg_sum(a * on, is_a) * (1.0 / GROUP)))
            do_n[rs, :] = do
            dd_n[rs, :] = _seg_sum(do * ov, is_a)
            zero = jnp.zeros((RC, LANES), F32)
            dqn[rs, :] = zero
            dkn[rs, :] = zero
            dvn[rs, :] = zero
            return dg_acc + jnp.sum(dn * on, axis=0, keepdims=True)

        dg_part = lax.fori_loop(0, S // RC, prologue, jnp.zeros((1, LANES), F32))

        @pl.when(b == 0)
        def _():
            dg_ref[...] = dg_part

        @pl.when(b != 0)
        def _():
            dg_ref[...] += dg_part

        for bi, d in enumerate(DILATIONS):
            nb = S // (BAND * d)
            L = S // d
            _gather_residues(qs, lambda rows: qkv_ref.at[0][rows, :], d, S, lambda v: (v * scale).astype(BF16))
            _gather_residues(ks, lambda rows: qkv_ref.at[1][rows, :], d, S, lambda v: v.astype(BF16))
            _gather_residues(vs, lambda rows: qkv_ref.at[2][rows, :], d, S, lambda v: v.astype(BF16))
            _gather_residues(dos, lambda rows: do_n[rows, :], d, S, lambda v: v.astype(BF16))
            if d == 1:
                lse_src, dd_src, dq_dst, dk_dst, dv_dst = lse_ref, dd_n, dqn, dkn, dvn
            else:
                _gather_residues(lses, lambda rows: lse_ref[rows, :], d, S, lambda v: v)
                _gather_residues(dds, lambda rows: dd_n[rows, :], d, S, lambda v: v)
                dkp[...] = jnp.zeros((S, LANES), F32)
                dvp[...] = jnp.zeros((S, LANES), F32)
                lse_src, dd_src, dq_dst, dk_dst, dv_dst = lses, dds, dqp, dkp, dvp

            def block(t, carry, bi=bi, d=d, nb=nb, lse_src=lse_src, dd_src=dd_src, dq_dst=dq_dst, dk_dst=dk_dst,
                      dv_dst=dv_dst):
                cur, prev, has_prev = _block_rows(t, d, S)
                q2 = _stack_heads(qs[cur, :], is_a)
                do2 = _stack_heads(dos[cur, :], is_a)
                lse_t = lse_src[cur, :]
                dd_t = dd_src[cur, :]
                lse2 = jnp.concatenate([lse_t[:, 0:1], lse_t[:, GROUP:GROUP + 1]], axis=0)
                dd2 = jnp.concatenate([dd_t[:, 0:1], dd_t[:, GROUP:GROUP + 1]], axis=0)
                if nb == 1:
                    kc, vc = ks[cur, :], vs[cur, :]
                    s = _nt(q2, kc) + bml[bi]
                else:
                    kc = jnp.concatenate([ks[prev, :], ks[cur, :]], axis=0)
                    vc = jnp.concatenate([vs[prev, :], vs[cur, :]], axis=0)
                    s = _nt(q2, kc) + bm[bi] + _first_block_penalty(has_prev)
                p = jnp.exp(s - lse2)
                ds = (p * (_nt(do2, vc) - dd2)).astype(BF16)
                dq = _unstack_heads(jnp.dot(ds, kc, preferred_element_type=F32), is_a)
                dk = _tn(ds, q2)
                dv = _tn(p.astype(BF16), do2)
                dq_dst[cur, :] = dq
                if nb == 1:
                    dk_dst[cur, :] += dk
                    dv_dst[cur, :] += dv
                else:
                    dk_dst[prev, :] += dk[0:BAND, :]
                    dv_dst[prev, :] += dv[0:BAND, :]
                    dk_dst[cur, :] += dk[BAND:2 * BAND, :]
                    dv_dst[cur, :] += dv[BAND:2 * BAND, :]
                return carry

            lax.fori_loop(0, NBLK, block, 0, unroll=8)
            if d > 1:
                for r in range(d):
                    rows = pl.ds(r, L, stride=d)
                    dqn[rows, :] += dqp[r * L:(r + 1) * L, :]
                    dkn[rows, :] += dkp[r * L:(r + 1) * L, :]
                    dvn[rows, :] += dvp[r * L:(r + 1) * L, :]

        dqkv_ref[0] = (dqn[...] * scale).astype(BF16)
        dqkv_ref[1] = dkn[...].astype(BF16)
        dqkv_ref[2] = dvn[...].astype(BF16)

    seq = pl.BlockSpec((S, LANES), lambda p, b: (b, p))
    f32_seq = pltpu.VMEM((S, LANES), F32)
    bf_seq = pltpu.VMEM((S, LANES), BF16)
    return pl.pallas_call(
        body, grid=(P, B), name=name,
        in_specs=[pl.BlockSpec((3, S, LANES), lambda p, b: (0, b, p)), seq, seq,
                  pl.BlockSpec((None, S, LANES), lambda p, b: (0, b, p)),
                  pl.BlockSpec((1, LANES), lambda p, b: (0, p))],
        out_specs=[pl.BlockSpec((3, S, LANES), lambda p, b: (0, b, p)),
                   pl.BlockSpec((1, LANES), lambda p, b: (0, p))],
        out_shape=[jax.ShapeDtypeStruct((dproj_pieces, T, C), BF16), jax.ShapeDtypeStruct((1, C), F32)],
        scratch_shapes=[f32_seq, f32_seq, bf_seq, bf_seq, bf_seq, bf_seq, f32_seq, f32_seq,
                        f32_seq, f32_seq, f32_seq, f32_seq, f32_seq, f32_seq,
                        pltpu.VMEM((nbr, 2 * BAND, 2 * BAND), F32), pltpu.VMEM((nbr, 2 * BAND, BAND), F32)],
        compiler_params=_params(("parallel", "arbitrary")),
    )(qkv3, o, lse, dmix3, gain)


def _delay(x, k, row):
    return jnp.where(row >= k, pltpu.roll(x, k, 0), 0.0)


def _advance(x, k, row, S):
    return jnp.where(row < S - k, pltpu.roll(x, S - k, 0), 0.0)


def _conv3(x, w, row):
    return (w[0:1, :] * _delay(x, 2, row) + w[1:2, :] * _delay(x, 1, row)) + w[2:3, :] * x


HALO = 8


CONV_ROWS = 128
FFN_LANES = 128


def _zero_halo(pad_ref, S):
    zeros = jnp.zeros((HALO, pad_ref.shape[1]), pad_ref.dtype)
    pad_ref[0:HALO, :] = zeros
    pad_ref[HALO + S:2 * HALO + S, :] = zeros


def _window_at(pad_ref, r0, shift):
    return pad_ref[HALO + r0 + shift:HALO + r0 + shift + CONV_ROWS, :]


def _conv3_at(pad_ref, w, r0):
    return ((w[0:1, :] * _window_at(pad_ref, r0, -2) + w[1:2, :] * _window_at(pad_ref, r0, -1))
            + w[2:3, :] * _window_at(pad_ref, r0, 0))


def _conv3_grads_at(dz_ref, x_ref, w, r0):
    dz, dz1, dz2 = (_window_at(dz_ref, r0, k) for k in range(3))
    x = _window_at(x_ref, r0, 0)
    dx = (w[2:3, :] * dz + w[1:2, :] * dz1) + w[0:1, :] * dz2
    parts = [jnp.sum((d * x).reshape(CONV_ROWS // 8, 8, x.shape[1]), axis=0) for d in (dz2, dz1, dz)]
    return dx, parts


def _conv3_grads(dz, x, w, row, S):
    dz1 = _advance(dz, 1, row, S)
    dz2 = _advance(dz, 2, row, S)
    dx = (w[2:3, :] * dz + w[1:2, :] * dz1) + w[0:1, :] * dz2
    dw = jnp.concatenate([jnp.sum(dz2 * x, axis=0, keepdims=True),
                          jnp.sum(dz1 * x, axis=0, keepdims=True),
                          jnp.sum(dz * x, axis=0, keepdims=True)], axis=0)
    return dx, dw


def _mix_conv_fwd(cv3, taps, gain, mix, S, name, after=None):
    _, T, C = cv3.shape
    B, P = T // S, C // LANES

    def body(cv_ref, w_ref, g_ref, mix_hbm, *rest):
        y_ref, pad_c = rest[-2:]
        del mix_hbm
        is_a = _lane_is_a()
        _zero_halo(pad_c, S)
        pad_c[HALO:HALO + S, :] = cv_ref[1].astype(F32) * cv_ref[2].astype(F32)
        w = w_ref[...]
        for r0 in range(0, S, CONV_ROWS):
            y = cv_ref[0, r0:r0 + CONV_ROWS, :].astype(F32) * _conv3_at(pad_c, w, r0)
            rstd = lax.rsqrt(_seg_sum(y * y, is_a) * (1.0 / GROUP) + EPS)
            y_ref[r0:r0 + CONV_ROWS, :] = ((y * rstd) * g_ref[...]).astype(BF16)

    in_specs = [pl.BlockSpec((3, S, LANES), lambda b, p: (0, b, p)),
                pl.BlockSpec((3, LANES), lambda b, p: (0, p)),
                pl.BlockSpec((1, LANES), lambda b, p: (0, p)),
                ANY]
    operands = [cv3, taps, gain, mix]
    if after is not None:
        in_specs.append(ANY)
        operands.append(after)
    return pl.pallas_call(
        body, grid=(B, P), name=name,
        in_specs=in_specs,
        out_specs=pl.BlockSpec((None, S, LANES), lambda b, p: (1, b, p)),
        out_shape=jax.ShapeDtypeStruct(mix.shape, mix.dtype),
        scratch_shapes=[pltpu.VMEM((S + 2 * HALO, LANES), F32)],
        input_output_aliases={3: 0},
        compiler_params=_params(("parallel", "parallel")),
    )(*operands)


def _mix_conv_bwd(cv3, dmix3, taps, gain, dproj, S, name):
    _, T, C = cv3.shape
    B, P = T // S, C // LANES

    def body(cv_ref, dn_ref, w_ref, g_ref, dproj_hbm, dcv_ref, dw_ref, dg_ref):
        del dproj_hbm
        b = pl.program_id(1)
        row = lax.broadcasted_iota(jnp.int32, (S, 1), 0)
        is_a = _lane_is_a()
        w = w_ref[...]
        gb = cv_ref[0].astype(F32)
        gc = cv_ref[1].astype(F32)
        u = cv_ref[2].astype(F32)
        c = gc * u
        z = _conv3(c, w, row)
        y = gb * z
        rstd = lax.rsqrt(_seg_sum(y * y, is_a) * (1.0 / GROUP) + EPS)
        yn = y * rstd
        dn = dn_ref[...]
        a = dn * g_ref[...]
        dy = rstd * (a - yn * (_seg_sum(a * yn, is_a) * (1.0 / GROUP)))
        dg = jnp.sum(dn * yn, axis=0, keepdims=True)
        dc, dw = _conv3_grads(dy * gb, c, w, row, S)
        dcv_ref[0] = (dy * z).astype(BF16)
        dcv_ref[1] = (dc * u).astype(BF16)
        dcv_ref[2] = (dc * gc).astype(BF16)

        @pl.when(b == 0)
        def _():
            dw_ref[...] = dw
            dg_ref[...] = dg

        @pl.when(b != 0)
        def _():
            dw_ref[...] += dw
            dg_ref[...] += dg

    return pl.pallas_call(
        body, grid=(P, B), name=name,
        in_specs=[pl.BlockSpec((3, S, LANES), lambda p, b: (0, b, p)),
                  pl.BlockSpec((None, S, LANES), lambda p, b: (1, b, p)),
                  pl.BlockSpec((3, LANES), lambda p, b: (0, p)),
                  pl.BlockSpec((1, LANES), lambda p, b: (0, p)),
                  pl.BlockSpec(memory_space=pl.ANY)],
        out_specs=[pl.BlockSpec((3, S, LANES), lambda p, b: (1, b, p)),
                   pl.BlockSpec((3, LANES), lambda p, b: (0, p)),
                   pl.BlockSpec((1, LANES), lambda p, b: (0, p))],
        out_shape=[jax.ShapeDtypeStruct(dproj.shape, dproj.dtype),
                   jax.ShapeDtypeStruct((3, C), F32), jax.ShapeDtypeStruct((1, C), F32)],
        input_output_aliases={4: 0},
        compiler_params=_params(("parallel", "arbitrary")),
    )(cv3, dmix3, taps, gain, dproj)


def _sigmoid(x):
    return 0.5 * jnp.tanh(0.5 * x) + 0.5


def _ffn_act_fwd(up3, taps, S, name):
    _, T, Fd = up3.shape
    W = FFN_LANES
    B, P = T // S, Fd // W

    def body(up_ref, wg_ref, wv_ref, act_ref, pad_g, pad_v):
        _zero_halo(pad_g, S)
        _zero_halo(pad_v, S)
        pad_g[HALO:HALO + S, :] = up_ref[0].astype(F32)
        pad_v[HALO:HALO + S, :] = up_ref[1].astype(F32)
        wg = wg_ref[...]
        wv = wv_ref[...]
        for r0 in range(0, S, CONV_ROWS):
            cg = _conv3_at(pad_g, wg, r0)
            cv = _conv3_at(pad_v, wv, r0)
            act_ref[r0:r0 + CONV_ROWS, :] = ((cg * _sigmoid(cg)) * cv).astype(BF16)

    return pl.pallas_call(
        body, grid=(B, P), name=name,
        in_specs=[pl.BlockSpec((2, S, W), lambda b, p: (0, b, p)),
                  pl.BlockSpec((3, W), lambda b, p: (0, p)),
                  pl.BlockSpec((3, W), lambda b, p: (0, P + p))],
        out_specs=pl.BlockSpec((S, W), lambda b, p: (b, p)),
        out_shape=jax.ShapeDtypeStruct((T, Fd), BF16),
        scratch_shapes=[pltpu.VMEM((S + 2 * HALO, W), F32)] * 2,
        compiler_params=_params(("parallel", "parallel")),
    )(up3, taps, taps)


def _ffn_act_bwd(up3, dact3, taps, S, name):
    _, T, Fd = up3.shape
    W = FFN_LANES
    B, P = T // S, Fd // W

    def body(up_ref, da_ref, wg_ref, wv_ref, dup_ref, dwg_ref, dwv_ref, pad_ug, pad_uv, pad_dg, pad_dv):
        b = pl.program_id(1)
        for pad in (pad_ug, pad_uv, pad_dg, pad_dv):
            _zero_halo(pad, S)
        pad_ug[HALO:HALO + S, :] = up_ref[0].astype(F32)
        pad_uv[HALO:HALO + S, :] = up_ref[1].astype(F32)
        wg = wg_ref[...]
        wv = wv_ref[...]
        for r0 in range(0, S, CONV_ROWS):
            cg = _conv3_at(pad_ug, wg, r0)
            cv = _conv3_at(pad_uv, wv, r0)
            sg = _sigmoid(cg)
            da = da_ref[r0:r0 + CONV_ROWS, :].astype(F32)
            pad_dg[HALO + r0:HALO + r0 + CONV_ROWS, :] = (da * cv) * (sg * (1.0 + cg * (1.0 - sg)))
            pad_dv[HALO + r0:HALO + r0 + CONV_ROWS, :] = da * (cg * sg)
        sums_g = [jnp.zeros((8, W), F32)] * 3
        sums_v = [jnp.zeros((8, W), F32)] * 3
        for r0 in range(0, S, CONV_ROWS):
            dug, parts_g = _conv3_grads_at(pad_dg, pad_ug, wg, r0)
            duv, parts_v = _conv3_grads_at(pad_dv, pad_uv, wv, r0)
            dup_ref[0, r0:r0 + CONV_ROWS, :] = dug.astype(BF16)
            dup_ref[1, r0:r0 + CONV_ROWS, :] = duv.astype(BF16)
            sums_g = [a + p for a, p in zip(sums_g, parts_g)]
            sums_v = [a + p for a, p in zip(sums_v, parts_v)]
        dwg = jnp.concatenate([jnp.sum(a, axis=0, keepdims=True) for a in sums_g], axis=0)
        dwv = jnp.concatenate([jnp.sum(a, axis=0, keepdims=True) for a in sums_v], axis=0)

        @pl.when(b == 0)
        def _():
            dwg_ref[...] = dwg
            dwv_ref[...] = dwv

        @pl.when(b != 0)
        def _():
            dwg_ref[...] += dwg
            dwv_ref[...] += dwv

    tap_out = pl.BlockSpec((3, W), lambda p, b: (0, p))
    return pl.pallas_call(
        body, grid=(P, B), name=name,
        in_specs=[pl.BlockSpec((2, S, W), lambda p, b: (0, b, p)),
                  pl.BlockSpec((None, S, W), lambda p, b: (0, b, p)),
                  pl.BlockSpec((3, W), lambda p, b: (0, p)),
                  pl.BlockSpec((3, W), lambda p, b: (0, P + p))],
        out_specs=[pl.BlockSpec((2, S, W), lambda p, b: (0, b, p)), tap_out, tap_out],
        out_shape=[jax.ShapeDtypeStruct((2, T, Fd), BF16),
                   jax.ShapeDtypeStruct((3, Fd), F32), jax.ShapeDtypeStruct((3, Fd), F32)],
        scratch_shapes=[pltpu.VMEM((S + 2 * HALO, W), F32)] * 4,
        compiler_params=_params(("parallel", "arbitrary")),
    )(up3, dact3, taps, taps)


def _final_norm_loss(x, g, target, tm, name):
    T, D = x.shape

    def body(x_ref, g_ref, t_ref, dx_ref, dg_ref, loss_ref):
        xv = x_ref[...]
        rstd = lax.rsqrt(jnp.mean(xv * xv, axis=-1, keepdims=True) + EPS)
        xn = xv * rstd
        err = xn * g_ref[...] - t_ref[...]
        part = 0.5 * jnp.sum(jnp.mean(err * err, axis=-1, keepdims=True), axis=0, keepdims=True)
        dy = err * (1.0 / D)
        a = dy * g_ref[...]
        dx_ref[...] = rstd * (a - xn * jnp.mean(a * xn, axis=-1, keepdims=True))
        dg = jnp.sum(dy * xn, axis=0, keepdims=True)
        lpart = jnp.broadcast_to(part, (1, LANES))

        @pl.when(pl.program_id(0) == 0)
        def _():
            dg_ref[...] = dg
            loss_ref[...] = lpart

        @pl.when(pl.program_id(0) != 0)
        def _():
            dg_ref[...] += dg
            loss_ref[...] += lpart

    row = pl.BlockSpec((tm, D), lambda i: (i, 0))
    return pl.pallas_call(
        body, grid=(T // tm,), name=name,
        in_specs=[row, pl.BlockSpec((1, D), lambda i: (0, 0)), row],
        out_specs=[row, pl.BlockSpec((1, D), lambda i: (0, 0)), pl.BlockSpec((1, LANES), lambda i: (0, 0))],
        out_shape=[jax.ShapeDtypeStruct((T, D), F32), jax.ShapeDtypeStruct((1, D), F32),
                   jax.ShapeDtypeStruct((1, LANES), F32)],
        compiler_params=_params(("arbitrary",)),
    )(x, g, target)


def _row_tile(rows, cols, budget_elems=512 * 1024):
    tr = rows
    while tr * cols > budget_elems and tr % 32 == 0:
        tr //= 2
    return tr


def _prefetch_call(body, grid, in_specs, out_specs, out_shape, name, sem, aliases=None):
    return pl.pallas_call(
        body, name=name, out_shape=out_shape,
        grid_spec=pltpu.PrefetchScalarGridSpec(num_scalar_prefetch=1, grid=grid, in_specs=in_specs,
                                               out_specs=out_specs),
        input_output_aliases=aliases or {},
        compiler_params=_params(sem))


def _cast_into_full(w, layer, colwise, where, name):
    _, K, N = w.shape
    tr = _row_tile(K, N)
    nrb = K // tr
    full_shape = (1, K, 4 * N) if colwise else (1, 4 * K, N)

    def body(where_ref, w_ref, o_ref):
        del where_ref
        o_ref[...] = w_ref[...].astype(BF16)

    if colwise:
        out_map = lambda i, wh: (0, i, wh[0])
    else:
        out_map = lambda i, wh: (0, wh[0] * nrb + i, 0)
    return _prefetch_call(
        body, (nrb,), [pl.BlockSpec((None, tr, N), lambda i, wh: (layer, i, 0))],
        pl.BlockSpec((None, tr, N), out_map), jax.ShapeDtypeStruct(full_shape, BF16), name,
        ("parallel",))(where, w)


def _chip_sum(g3, other, colwise, where, name):
    L, K, N = g3.shape
    hk, hn = (K // 2, N) if colwise else (K, N // 2)
    tr = _row_tile(hk, hn)
    nrb = hk // tr

    def body(where_ref, g_ref, o_ref, s_ref):
        del where_ref
        s_ref[...] = (g_ref[...].astype(F32) + o_ref[...].astype(F32)).astype(BF16)

    if colwise:
        g_map = lambda l, i, wh: (l, wh[1] * nrb + i, 0)
    else:
        g_map = lambda l, i, wh: (l, i, wh[1])
    blk = pl.BlockSpec((None, tr, hn), lambda l, i, wh: (l, i, 0))
    return _prefetch_call(
        body, (L, nrb), [pl.BlockSpec((None, tr, hn), g_map), blk], blk,
        jax.ShapeDtypeStruct((L, hk, hn), BF16), name, ("parallel", "parallel"))(where, g3, other)


def _owner_sum(chip_sum, received, colwise, where, layer, n_layers, prev, name):
    _, hk, hn = chip_sum.shape
    pk, pn = (hk, hn // 4) if colwise else (hk // 4, hn)
    tr = _row_tile(pk, pn)
    nrb = pk // tr
    shard_shape = (n_layers, 2 * pk, pn) if colwise else (n_layers, pk, 2 * pn)

    def body(where_ref, own_ref, rec_ref, *rest):
        del where_ref
        o_ref = rest[-1]
        acc = own_ref[...].astype(F32)
        for j in range(3):
            acc = acc + rec_ref[j].astype(F32)
        o_ref[...] = acc

    if colwise:
        own_map = lambda i, wh: (0, i, wh[0])
        out_map = lambda i, wh: (layer, wh[1] * nrb + i, 0)
    else:
        own_map = lambda i, wh: (0, wh[0] * nrb + i, 0)
        out_map = lambda i, wh: (layer, i, wh[1])
    in_specs = [pl.BlockSpec((None, tr, pn), own_map),
                pl.BlockSpec((3, None, tr, pn), lambda i, wh: (0, 0, i, 0))]
    operands = [where, chip_sum, received]
    if prev is not None:
        in_specs.append(ANY)
        operands.append(prev)
    return _prefetch_call(
        body, (nrb,), in_specs, pl.BlockSpec((None, tr, pn), out_map), jax.ShapeDtypeStruct(shard_shape, F32), name,
        ("parallel",), None if prev is None else {3: 0})(*operands)


def _adamw(w, g, m, v, name):
    R, Cc = w.shape
    tr = _row_tile(R, Cc, 256 * 1024)

    def body(w_ref, g_ref, m_ref, v_ref, d_ref, nm_ref, nv_ref, go_ref):
        gv = g_ref[...]
        go_ref[...] = gv
        nm = ADAM_B1 * m_ref[...] + (1.0 - ADAM_B1) * gv
        nv = ADAM_B2 * v_ref[...] + (1.0 - ADAM_B2) * (gv * gv)
        m_hat = nm / (1.0 - ADAM_B1 ** ADAM_STEP)
        v_hat = nv / (1.0 - ADAM_B2 ** ADAM_STEP)
        d_ref[...] = -ADAM_LR * (m_hat / (jnp.sqrt(v_hat) + ADAM_EPS) + ADAM_WD * w_ref[...])
        nm_ref[...] = nm
        nv_ref[...] = nv

    blk = pl.BlockSpec((tr, Cc), lambda i: (i, 0))
    shp = jax.ShapeDtypeStruct((R, Cc), F32)
    return pl.pallas_call(
        body, grid=(R // tr,), name=name,
        in_specs=[blk] * 4, out_specs=[blk] * 4, out_shape=[shp] * 4,
        compiler_params=_params(("parallel",)),
    )(w, g, m, v)


COL_SHARDED = (True, False, True, False)


def _position():
    x, y, c = lax.axis_index("x"), lax.axis_index("y"), lax.axis_index("c")
    chips = [(1 - x, y), (x, 1 - y), (1 - x, 1 - y)]
    return x, y, c, chips


def _span(index, size, align):
    return pl.ds(pl.multiple_of(index * size, align), size)


def _window(ref, colwise, shard, half, shards=4):
    _, K, N = ref.shape
    rows = cols = slice(None)
    if colwise:
        if half is not None:
            rows = _span(half, K // 2, 16)
        if shard is not None:
            cols = _span(shard, N // shards, LANES)
    else:
        if shard is not None:
            rows = _span(shard, K // shards, 16)
        if half is not None:
            cols = _span(half, N // 2, LANES)
    return ref.at[:, rows, cols]


HBM = pl.BlockSpec(memory_space=pltpu.HBM)
SEMAPHORES = pl.BlockSpec(memory_space=pltpu.SEMAPHORE)


def _gather_start(fulls, colwise, group_sizes, after, name):
    n = len(fulls)
    n_groups = len(group_sizes)

    n_in = n if after is None else n + 1

    def body(*refs):
        ins = refs[:n]
        sems = refs[n_in:n_in + 2 * n_groups]
        x, y, c, chips = _position()
        me = 2 * x + y
        i = 0
        for g, size in enumerate(group_sizes):
            for a in range(size):
                win = _window(ins[i], colwise[i], me, c)
                for j, chip in enumerate(chips):
                    pltpu.make_async_remote_copy(
                        src_ref=win, dst_ref=win, send_sem=sems[2 * g].at[a * 3 + j],
                        recv_sem=sems[2 * g + 1].at[a * 3 + j],
                        device_id=(chip[0], chip[1], c), device_id_type=MESH_ID).start()
                i += 1

    sem_shapes = []
    for size in group_sizes:
        sem_shapes += [pltpu.SemaphoreType.DMA((3 * size,)), pltpu.SemaphoreType.DMA((3 * size,))]
    operands = [pltpu.with_memory_space_constraint(f, pltpu.HBM) for f in fulls]
    in_specs = [HBM] * n
    if after is not None:
        operands.append(after)
        in_specs.append(ANY)
    outs = pl.pallas_call(
        body, name=name,
        in_specs=in_specs, out_specs=[SEMAPHORES] * (2 * n_groups) + [HBM] * n,
        out_shape=sem_shapes + [pltpu.HBM(f.shape, f.dtype) for f in fulls],
        input_output_aliases={i: 2 * n_groups + i for i in range(n)},
        compiler_params=pltpu.CompilerParams(has_side_effects=pltpu.SideEffectType.DATAFLOW_SIDE_EFFECTING),
    )(*operands)
    sems = [(outs[2 * g], outs[2 * g + 1]) for g in range(n_groups)]
    return sems, list(outs[2 * n_groups:])


def _to_sibling(ref, colwise, chip, half, x, y, c, send_sem, recv_sem):
    win = _window(ref, colwise, 2 * chip[0] + chip[1], half)
    return pltpu.make_async_remote_copy(
        src_ref=win, dst_ref=win, send_sem=send_sem, recv_sem=recv_sem,
        device_id=(x, y, 1 - c), device_id_type=MESH_ID)


def _gather_pass(in_flight, colwise, sems, after, name):
    n = len(in_flight)

    def body(*refs):
        ins = refs[:n]
        send_sems, recv_sems = refs[n], refs[n + 1]
        pass_send, pass_recv = refs[-2 - n], refs[-1 - n]
        x, y, c, chips = _position()
        me = 2 * x + y
        for a in range(n):
            for j, chip in enumerate(chips):
                k = a * 3 + j
                pltpu.make_async_remote_copy(
                    src_ref=_window(ins[a], colwise[a], me, c),
                    dst_ref=_window(ins[a], colwise[a], 2 * chip[0] + chip[1], c),
                    send_sem=send_sems.at[k], recv_sem=recv_sems.at[k],
                    device_id=(chip[0], chip[1], c), device_id_type=MESH_ID).wait()
                _to_sibling(ins[a], colwise[a], chip, c, x, y, c, pass_send.at[k], pass_recv.at[k]).start()

    operands = list(in_flight) + list(sems)
    in_specs = [HBM] * n + [SEMAPHORES] * 2
    if after is not None:
        operands.append(after)
        in_specs.append(ANY)
    outs = pl.pallas_call(
        body, name=name,
        in_specs=in_specs, out_specs=[SEMAPHORES] * 2 + [HBM] * n,
        out_shape=[pltpu.SemaphoreType.DMA((3 * n,)), pltpu.SemaphoreType.DMA((3 * n,))]
        + [pltpu.HBM(f.shape, f.dtype) for f in in_flight],
        input_output_aliases={i: 2 + i for i in range(n)},
        compiler_params=pltpu.CompilerParams(has_side_effects=pltpu.SideEffectType.DATAFLOW_SIDE_EFFECTING),
    )(*operands)
    return (outs[0], outs[1]), list(outs[2:])


def _gather_wait(in_flight, colwise, sems, after, name):
    n = len(in_flight)

    def body(*refs):
        ins = refs[:n]
        send_sems, recv_sems = refs[n], refs[n + 1]
        x, y, c, chips = _position()
        for a in range(n):
            for j, chip in enumerate(chips):
                k = a * 3 + j
                _to_sibling(ins[a], colwise[a], chip, c, x, y, c, send_sems.at[k], recv_sems.at[k]).wait_send()
                _to_sibling(ins[a], colwise[a], chip, 1 - c, x, y, c, send_sems.at[k], recv_sems.at[k]).wait_recv()

    operands = list(in_flight) + list(sems)
    in_specs = [HBM] * n + [SEMAPHORES] * 2
    if after is not None:
        operands.append(after)
        in_specs.append(ANY)
    outs = pl.pallas_call(
        body, name=name,
        in_specs=in_specs, out_specs=[HBM] * n,
        out_shape=[pltpu.HBM(f.shape, f.dtype) for f in in_flight],
        input_output_aliases={i: i for i in range(n)},
        compiler_params=pltpu.CompilerParams(has_side_effects=pltpu.SideEffectType.DATAFLOW_SIDE_EFFECTING),
    )(*operands)
    return list(outs)


def _exchange_copy(g_ref, land_ref, colwise, x, y, c, send_sem, recv_sem):
    return pltpu.make_async_remote_copy(
        src_ref=_window(g_ref, colwise, None, 1 - c), dst_ref=land_ref, send_sem=send_sem, recv_sem=recv_sem,
        device_id=(x, y, 1 - c), device_id_type=MESH_ID)


def _exchange_start(grads, colwise, name):
    n = len(grads)
    lands = []
    for g, cw in zip(grads, colwise):
        L, K, N = g.shape
        lands.append(lax.empty((L, K // 2, N) if cw else (L, K, N // 2), g.dtype))

    def body(*refs):
        src, land = refs[:n], refs[n:2 * n]
        send_sems, recv_sems = refs[2 * n], refs[2 * n + 1]
        x, y, c, _ = _position()
        for i in range(n):
            _exchange_copy(src[i], land[i], colwise[i], x, y, c, send_sems.at[i], recv_sems.at[i]).start()

    arrays = list(grads) + lands
    outs = pl.pallas_call(
        body, name=name,
        in_specs=[HBM] * (2 * n), out_specs=[SEMAPHORES] * 2 + [HBM] * (2 * n),
        out_shape=[pltpu.SemaphoreType.DMA((n,)), pltpu.SemaphoreType.DMA((n,))]
        + [pltpu.HBM(a.shape, a.dtype) for a in arrays],
        input_output_aliases={i: 2 + i for i in range(2 * n)},
        compiler_params=pltpu.CompilerParams(has_side_effects=pltpu.SideEffectType.DATAFLOW_SIDE_EFFECTING),
    )(*[pltpu.with_memory_space_constraint(a, pltpu.HBM) for a in arrays])
    return (outs[0], outs[1]), list(outs[2:2 + n]), list(outs[2 + n:])


def _exchange_wait(grads, lands, colwise, sems, after, name):
    n = len(grads)

    def body(*refs):
        src, land = refs[:n], refs[n:2 * n]
        send_sems, recv_sems = refs[2 * n], refs[2 * n + 1]
        x, y, c, _ = _position()
        for i in range(n):
            _exchange_copy(src[i], land[i], colwise[i], x, y, c, send_sems.at[i], recv_sems.at[i]).wait()

    arrays = list(grads) + list(lands)
    operands = arrays + list(sems)
    in_specs = [HBM] * (2 * n) + [SEMAPHORES] * 2
    if after is not None:
        operands.append(after)
        in_specs.append(ANY)
    outs = pl.pallas_call(
        body, name=name,
        in_specs=in_specs, out_specs=[HBM] * (2 * n),
        out_shape=[pltpu.HBM(a.shape, a.dtype) for a in arrays],
        input_output_aliases={i: i for i in range(2 * n)},
        compiler_params=pltpu.CompilerParams(has_side_effects=pltpu.SideEffectType.DATAFLOW_SIDE_EFFECTING),
    )(*operands)
    return list(outs[:n]), list(outs[n:])


def _scatter_copy(src_ref, land_ref, colwise, j, chip, c, send_sem, recv_sem):
    return pltpu.make_async_remote_copy(
        src_ref=_window(src_ref, colwise, 2 * chip[0] + chip[1], None), dst_ref=land_ref.at[j],
        send_sem=send_sem, recv_sem=recv_sem, device_id=(chip[0], chip[1], c), device_id_type=MESH_ID)


def _scatter_start(chip_sums, colwise, name):
    n = len(chip_sums)
    lands = []
    for g, cw in zip(chip_sums, colwise):
        L, hk, hn = g.shape
        lands.append(lax.empty((3, L, hk, hn // 4) if cw else (3, L, hk // 4, hn), g.dtype))

    def body(*refs):
        src, land = refs[:n], refs[n:2 * n]
        send_sems, recv_sems = refs[2 * n], refs[2 * n + 1]
        x, y, c, chips = _position()
        for i in range(n):
            for j, chip in enumerate(chips):
                _scatter_copy(src[i], land[i], colwise[i], j, chip, c, send_sems.at[i * 3 + j],
                              recv_sems.at[i * 3 + j]).start()

    arrays = list(chip_sums) + lands
    outs = pl.pallas_call(
        body, name=name,
        in_specs=[HBM] * (2 * n), out_specs=[SEMAPHORES] * 2 + [HBM] * (2 * n),
        out_shape=[pltpu.SemaphoreType.DMA((3 * n,)), pltpu.SemaphoreType.DMA((3 * n,))]
        + [pltpu.HBM(a.shape, a.dtype) for a in arrays],
        input_output_aliases={i: 2 + i for i in range(2 * n)},
        compiler_params=pltpu.CompilerParams(has_side_effects=pltpu.SideEffectType.DATAFLOW_SIDE_EFFECTING),
    )(*[pltpu.with_memory_space_constraint(a, pltpu.HBM) for a in arrays])
    return (outs[0], outs[1]), list(outs[2:2 + n]), list(outs[2 + n:])


def _scatter_wait(sources, lands, colwise, sems, after, name):
    n = len(sources)

    def body(*refs):
        src, land = refs[:n], refs[n:2 * n]
        send_sems, recv_sems = refs[2 * n], refs[2 * n + 1]
        x, y, c, chips = _position()
        for i in range(n):
            for j, chip in enumerate(chips):
                cp = _scatter_copy(src[i], land[i], colwise[i], j, chip, c, send_sems.at[i * 3 + j],
                                   recv_sems.at[i * 3 + j])
                cp.wait_send()
                cp.wait_recv()

    arrays = list(sources) + list(lands)
    operands = arrays + list(sems)
    in_specs = [HBM] * (2 * n) + [SEMAPHORES] * 2
    if after is not None:
        operands.append(after)
        in_specs.append(ANY)
    outs = pl.pallas_call(
        body, name=name,
        in_specs=in_specs, out_specs=[HBM] * (2 * n),
        out_shape=[pltpu.HBM(a.shape, a.dtype) for a in arrays],
        input_output_aliases={i: i for i in range(2 * n)},
        compiler_params=pltpu.CompilerParams(has_side_effects=pltpu.SideEffectType.DATAFLOW_SIDE_EFFECTING),
    )(*operands)
    return list(outs[:n]), list(outs[n:])


def _share_with_sibling(shards, colwise, name):
    n = len(shards)

    def body(*refs):
        out = refs[n:2 * n]
        send_sems, recv_sems = refs[2 * n:]
        x, y, c, _ = _position()

        def copy(i, half):
            win = _window(out[i], colwise[i], None, half)
            return pltpu.make_async_remote_copy(
                src_ref=win, dst_ref=win, send_sem=send_sems.at[i], recv_sem=recv_sems.at[i],
                device_id=(x, y, 1 - c), device_id_type=MESH_ID)

        for i in range(n):
            copy(i, c).start()
        for i in range(n):
            copy(i, 1 - c).wait_recv()
        for i in range(n):
            copy(i, c).wait_send()

    return pl.pallas_call(
        body, name=name,
        in_specs=[ANY] * n, out_specs=[ANY] * n,
        out_shape=[jax.ShapeDtypeStruct(s.shape, s.dtype) for s in shards],
        input_output_aliases={i: i for i in range(n)},
        scratch_shapes=[pltpu.SemaphoreType.DMA((n,)), pltpu.SemaphoreType.DMA((n,))],
    )(*shards)


def _chip_exchange(buf, me, x, y, c, chips, send_sems, recv_sems):
    def copy(j, chip, slot):
        return pltpu.make_async_remote_copy(
            src_ref=buf.at[me], dst_ref=buf.at[slot], send_sem=send_sems.at[j], recv_sem=recv_sems.at[j],
            device_id=(chip[0], chip[1], c), device_id_type=MESH_ID)

    for j, chip in enumerate(chips):
        copy(j, chip, me).start()
    for j, chip in enumerate(chips):
        copy(j, chip, 2 * chip[0] + chip[1]).wait_recv()
    for j, chip in enumerate(chips):
        copy(j, chip, me).wait_send()


def _gather_over_chips(pack, name):
    R, Cc = pack.shape

    def body(p_ref, o_ref, send_sems, recv_sems):
        x, y, c, chips = _position()
        me = 2 * x + y
        o_ref[me] = p_ref[...]
        _chip_exchange(o_ref, me, x, y, c, chips, send_sems, recv_sems)

    vmem = pl.BlockSpec(memory_space=pltpu.VMEM)
    return pl.pallas_call(
        body, name=name,
        in_specs=[vmem], out_specs=vmem, out_shape=jax.ShapeDtypeStruct((4, R, Cc), F32),
        scratch_shapes=[pltpu.SemaphoreType.DMA((3,)), pltpu.SemaphoreType.DMA((3,))],
    )(pack)


def _all_reduce_small(pack, name):
    R, Cc = pack.shape

    def body(p_ref, o_ref, sibling, buf, send_sems, recv_sems):
        x, y, c, chips = _position()
        me = 2 * x + y
        swap = pltpu.make_async_remote_copy(
            src_ref=p_ref, dst_ref=sibling, send_sem=send_sems.at[3], recv_sem=recv_sems.at[3],
            device_id=(x, y, 1 - c), device_id_type=MESH_ID)
        swap.start()
        swap.wait()
        buf[me] = p_ref[...] + sibling[...]
        _chip_exchange(buf, me, x, y, c, chips, send_sems, recv_sems)
        o_ref[...] = (buf[0] + buf[1]) + (buf[2] + buf[3])

    vmem = pl.BlockSpec(memory_space=pltpu.VMEM)
    return pl.pallas_call(
        body, name=name,
        in_specs=[vmem], out_specs=vmem, out_shape=jax.ShapeDtypeStruct((R, Cc), F32),
        scratch_shapes=[pltpu.VMEM((R, Cc), F32), pltpu.VMEM((4, R, Cc), F32), pltpu.SemaphoreType.DMA((4,)),
                        pltpu.SemaphoreType.DMA((4,))],
    )(pack)


def _local_forward_backward(x2, target2, S, pass_on, fetch, reduce_begin, reduce_commit, layers, final_g, tm=512):
    T, D = x2.shape
    C = D // 2
    n_heads = C // GROUP
    n_layers = len(layers)
    weights = {}
    saved = []
    xc = x2
    for li, lw in enumerate(layers):
        if li == 0:
            pass_on(0, None)
            weights.update(fetch(0, None))
        h1, qkv3, cv3 = _norm_proj(xc, lw["norm1"], weights[li, "w_in"], 0, ((3, C, F32), (3, C, BF16)), tm,
                                   min(C, 512), f"l{li}_norm_in_proj")
        if li == 0:
            pass_on(1, h1)
        o, lse, mix = _attn_fwd(qkv3, lw["attn_g"], 2, S, n_heads, f"l{li}_attn_fwd")
        pin = None
        if li == 0:
            weights.update(fetch(1, o))
            pin = pass_on(3, pass_on(2, o))
        mix = _mix_conv_fwd(cv3, lw["taps"], lw["conv_g"], mix, S, f"l{li}_mix_conv_fwd", pin)
        x_mid = _proj_residual(mix, weights[li, "w_out"], 0, xc, tm, f"l{li}_out_proj")
        if li == 0:
            weights.update(fetch(2, x_mid))
        Fd = weights[li, "ffn_up"].shape[2] // 2
        h2, up3 = _norm_proj(x_mid, lw["norm2"], weights[li, "ffn_up"], 0, ((2, Fd, BF16),), tm // 2, 256,
                             f"l{li}_norm_ffn_up")
        if li == 0:
            weights.update(fetch(3, up3))
        act = _ffn_act_fwd(up3, lw["ffn_taps"], S, f"l{li}_ffn_act_fwd")
        pin = pass_on(li + 4, act) if li + 1 < n_layers else None
        x_out = _proj_residual(act.reshape(1, T, Fd), weights[li, "ffn_down"], 0, x_mid, tm, f"l{li}_ffn_down", pin)
        if li + 1 < n_layers:
            weights.update(fetch(li + 4, x_out))
        saved.append(dict(x_in=xc, h1=h1, qkv3=qkv3, cv3=cv3, o=o, lse=lse, mix=mix, x_mid=x_mid, h2=h2, up3=up3,
                          act=act))
        xc = x_out

    dx, d_final_g, loss_part = _final_norm_loss(xc, final_g, target2, tm, "final_norm_loss")

    small = [None] * n_layers
    started = None
    for li in reversed(range(n_layers)):
        lw, sv = layers[li], saved[li]
        w_in, w_out, ffn_up, ffn_down = (weights[li, n] for n in ("w_in", "w_out", "ffn_up", "ffn_down"))
        dxb, dact3 = _grad_through_weight(dx, ffn_down, 0, 1, Fd, BF16, tm, 256, f"l{li}_d_act", started)
        Fd = ffn_down.shape[1]
        d_ffn_down = _weight_grad(sv["act"].reshape(1, T, Fd), dxb.reshape(1, T, D), Fd // 2, D,
                                  f"l{li}_d_ffn_down")
        dup3, d_taps_g, d_taps_v = _ffn_act_bwd(sv["up3"], dact3, lw["ffn_taps"], S, f"l{li}_ffn_act_bwd")
        d_ffn_up = _weight_grad(sv["h2"].reshape(1, D, T), dup3, D, Fd // 2, f"l{li}_d_ffn_up", a_transposed=True)
        if li == 0:
            early = reduce_begin({(li, "ffn_down"): d_ffn_down, (li, "ffn_up"): d_ffn_up})
        dx_mid, d_norm2 = _grad_through_proj_norm(dup3, ffn_up, 0, sv["x_mid"], lw["norm2"], dx, tm // 2,
                                                  f"l{li}_d_norm2")
        started = reduce_commit(early, dx_mid) if li == 0 else None
        dxmb, dmix3 = _grad_through_weight(dx_mid, w_out, 0, 2, C, F32, tm, min(C, 512), f"l{li}_d_mix", started)
        d_w_out = _weight_grad(sv["mix"], dxmb.reshape(1, T, D), min(C, 256), D, f"l{li}_d_w_out")
        dproj, d_attn_g = _attn_bwd(sv["qkv3"], sv["o"], sv["lse"], dmix3, lw["attn_g"], 6, S, n_heads,
                                    f"l{li}_attn_bwd")
        dproj, d_taps, d_conv_g = _mix_conv_bwd(sv["cv3"], dmix3, lw["taps"], lw["conv_g"], dproj, S,
                                                f"l{li}_mix_conv_bwd")
        d_w_in = _weight_grad(sv["h1"].reshape(1, D, T), dproj, D, C, f"l{li}_d_w_in", a_transposed=True)
        late = {(li, "w_out"): d_w_out, (li, "w_in"): d_w_in}
        if li > 0:
            late.update({(li, "ffn_down"): d_ffn_down, (li, "ffn_up"): d_ffn_up})
        late = reduce_begin(late)
        dx, d_norm1 = _grad_through_proj_norm(dproj, w_in, 0, sv["x_in"], lw["norm1"], dx_mid, tm,
                                              f"l{li}_d_norm1")
        started = reduce_commit(late, dx)
        small[li] = dict(norm1=d_norm1, taps=d_taps, attn_g=d_attn_g, conv_g=d_conv_g, norm2=d_norm2,
                         ffn_taps=jnp.concatenate([d_taps_g, d_taps_v], axis=1))
    return loss_part, dx, small, d_final_g


SMALL_ORDER = ("norm1", "attn_g", "conv_g", "norm2", "taps", "ffn_taps")


def _pack_small(small, d_final_g, loss_row):
    parts = [small[li][k].reshape(-1) for li in range(len(small)) for k in SMALL_ORDER]
    loss_rows = jnp.tile(loss_row.reshape(1, LANES), (8, 1))
    return jnp.concatenate(parts + [d_final_g.reshape(-1), loss_rows.reshape(-1)]).reshape(-1, LANES)


def _unpack_small(pack, small, d_final_g):
    flat = pack.reshape(-1)
    out, pos = [dict() for _ in small], 0
    for li in range(len(small)):
        for k in SMALL_ORDER:
            n = small[li][k].size
            out[li][k] = flat[pos:pos + n].reshape(small[li][k].shape)
            pos += n
    return out, flat[pos:pos + d_final_g.size], flat[pos + d_final_g.size]


def kernel(x, norm1_g, w_in, mix_conv_w, attn_out_g, conv_out_g, w_out, norm2_g, ffn_up, ffn_conv_w, ffn_down, final_norm_g, loss_target, m_norm1_g, m_w_in, m_mix_conv_w, m_attn_out_g, m_conv_out_g, m_w_out, m_norm2_g, m_ffn_up, m_ffn_conv_w, m_ffn_down, m_final_norm_g, v_norm1_g, v_w_in, v_mix_conv_w, v_attn_out_g, v_conv_out_g, v_w_out, v_norm2_g, v_ffn_up, v_ffn_conv_w, v_ffn_down, v_final_norm_g):
    Bl, S, D = x.shape
    L = w_in.shape[0]
    T = Bl * S
    shard = 2 * lax.axis_index("x") + lax.axis_index("y")
    where = jnp.stack([shard, lax.axis_index("c")]).astype(jnp.int32)
    big_names = ("w_in", "w_out", "ffn_up", "ffn_down")

    taps_w, ftaps_w = mix_conv_w.shape[2], ffn_conv_w.shape[2]
    tap_pack = _gather_over_chips(
        jnp.concatenate([mix_conv_w.reshape(-1), ffn_conv_w.reshape(-1)]).reshape(-1, LANES), "all_gather_taps")
    by_chip = tap_pack.reshape(4, -1)
    n_taps = mix_conv_w.size
    taps_full = by_chip[:, :n_taps].reshape(4, L, 3, taps_w).transpose(1, 2, 0, 3).reshape(L, 3, 4 * taps_w)
    ftaps_full = by_chip[:, n_taps:].reshape(4, L, 3, ftaps_w).transpose(1, 2, 0, 3).reshape(L, 3, 4 * ftaps_w)

    big_shards = dict(zip(big_names, (w_in, w_out, ffn_up, ffn_down)))
    col_of = dict(zip(big_names, COL_SHARDED))
    groups = [[(0, n)] for n in big_names] + [[(l, n) for n in big_names] for l in range(1, L)]
    sems, in_flight = [], {}
    all_started = tap_pack
    for first, last in ((0, 1), (1, len(groups))):
        keys = [k for g in groups[first:last] for k in g]
        new_sems, arrays = _gather_start(
            [_cast_into_full(big_shards[n], l, col_of[n], where, f"cast_{n}_{l}") for l, n in keys],
            [col_of[n] for _, n in keys], [len(g) for g in groups[first:last]], all_started,
            f"gather_start_{first}")
        sems += new_sems
        in_flight.update(zip(keys, arrays))
        all_started = arrays[-1]

    def pass_on(g, after):
        after = all_started if g == 0 else after
        sems[g], arrays = _gather_pass([in_flight[k] for k in groups[g]], [col_of[n] for _, n in groups[g]],
                                       sems[g], after, f"gather_pass_{g}")
        in_flight.update(zip(groups[g], arrays))
        return arrays[0]

    def fetch(g, after):
        done = _gather_wait([in_flight[k] for k in groups[g]], [col_of[n] for _, n in groups[g]], sems[g], after,
                            f"gather_wait_{g}")
        return dict(zip(groups[g], done))

    pending = []

    begun = []

    def reduce_begin(grads):
        g = len(begun)
        keys = list(grads)
        cols = [col_of[n] for _, n in keys]
        begun.append((g, keys, cols) + _exchange_start([grads[k] for k in keys], cols, f"exchange_start_{g}"))
        return begun[-1]

    def reduce_commit(handle, after):
        g, keys, cols, ex_sems, mine, lands = handle
        mine, others = _exchange_wait(mine, lands, cols, ex_sems, after, f"exchange_wait_{g}")
        chip_sums = [_chip_sum(m, o, cw, where, f"chip_sum_{k[1]}_{k[0]}")
                     for k, m, o, cw in zip(keys, mine, others, cols)]
        pending.append((keys, cols) + _scatter_start(chip_sums, cols, f"scatter_start_{g}"))
        return pending[-1][3][0]

    layers = [dict(norm1=norm1_g[l:l + 1], taps=taps_full[l], attn_g=attn_out_g[l:l + 1],
                   conv_g=conv_out_g[l:l + 1], norm2=norm2_g[l:l + 1], ffn_taps=ftaps_full[l]) for l in range(L)]

    loss_part, dx, small, d_final_g = _local_forward_backward(
        x.reshape(T, D), loss_target.reshape(T, D), S, pass_on, fetch, reduce_begin, reduce_commit, layers,
        final_norm_g.reshape(1, D))

    def finish_group(g, after):
        keys, cols, rs_sems, sources, lands = pending[g]
        sources, lands = _scatter_wait(sources, lands, cols, rs_sems, after, f"scatter_wait_{g}")
        for (l, n), cw, src, land in zip(keys, cols, sources, lands):
            reduced[n] = _owner_sum(src, land, cw, where, l, L, reduced[n], f"owner_sum_{n}_{l}")

    reduced = dict.fromkeys(big_names)
    last_started = pending[-1][3][0]
    for g in range(len(pending) - 1):
        finish_group(g, last_started)
    late_names = [n for n in big_names if any(n == name for _, name in pending[-1][0])]
    early_names = [n for n in big_names if n not in late_names]
    g_big = dict(zip(early_names, _share_with_sibling([reduced[n] for n in early_names],
                                                      [col_of[n] for n in early_names], "grad_share_early")))

    pack = _all_reduce_small(_pack_small(small, d_final_g, loss_part), "all_reduce_small_grads")
    g_small, g_final, loss = _unpack_small(pack, small, d_final_g)

    def stacked(key):
        return jnp.stack([g_small[l][key].reshape(g_small[l][key].shape[-2:] if key.endswith("taps") else (-1,))
                          for l in range(L)])

    g_norm1, g_attn, g_conv, g_norm2 = stacked("norm1"), stacked("attn_g"), stacked("conv_g"), stacked("norm2")
    g_taps = lax.dynamic_slice(stacked("taps"), (0, 0, shard * taps_w), (L, 3, taps_w))
    g_ftaps = lax.dynamic_slice(stacked("ffn_taps"), (0, 0, shard * ftaps_w), (L, 3, ftaps_w))

    grads_out = dict(norm1_g=g_norm1, w_in=None, mix_conv_w=g_taps, attn_out_g=g_attn, conv_out_g=g_conv,
                     w_out=None, norm2_g=g_norm2, ffn_up=None, ffn_conv_w=g_ftaps, ffn_down=None,
                     final_norm_g=g_final)
    weights = dict(norm1_g=norm1_g, w_in=w_in, mix_conv_w=mix_conv_w, attn_out_g=attn_out_g, conv_out_g=conv_out_g,
                   w_out=w_out, norm2_g=norm2_g, ffn_up=ffn_up, ffn_conv_w=ffn_conv_w, ffn_down=ffn_down,
                   final_norm_g=final_norm_g)
    ms = dict(norm1_g=m_norm1_g, w_in=m_w_in, mix_conv_w=m_mix_conv_w, attn_out_g=m_attn_out_g,
              conv_out_g=m_conv_out_g, w_out=m_w_out, norm2_g=m_norm2_g, ffn_up=m_ffn_up, ffn_conv_w=m_ffn_conv_w,
              ffn_down=m_ffn_down, final_norm_g=m_final_norm_g)
    vs = dict(norm1_g=v_norm1_g, w_in=v_w_in, mix_conv_w=v_mix_conv_w, attn_out_g=v_attn_out_g,
              conv_out_g=v_conv_out_g, w_out=v_w_out, norm2_g=v_norm2_g, ffn_up=v_ffn_up, ffn_conv_w=v_ffn_conv_w,
              ffn_down=v_ffn_down, final_norm_g=v_final_norm_g)
    names = list(weights)
    small_names = [n for n in names if n not in big_names]
    delta, new_m, new_v = {}, {}, {}

    def update_big(n):
        shp = weights[n].shape
        two_d = (shp[0] * shp[1], shp[2])
        d_, m_, v_, g_ = _adamw(weights[n].reshape(two_d), g_big[n].reshape(two_d), ms[n].reshape(two_d),
                                vs[n].reshape(two_d), f"adamw_{n}")
        delta[n], new_m[n], new_v[n], grads_out[n] = (a.reshape(shp) for a in (d_, m_, v_, g_))

    for n in early_names:
        update_big(n)
    finish_group(len(pending) - 1, delta[early_names[-1]] if early_names else None)
    g_big.update(zip(late_names, _share_with_sibling([reduced[n] for n in late_names],
                                                     [col_of[n] for n in late_names], "grad_share_late")))
    for n in late_names:
        update_big(n)

    def packed(tree):
        return jnp.concatenate([tree[n].reshape(-1) for n in small_names]).reshape(-1, LANES)

    d_, m_, v_, _ = _adamw(packed(weights), packed(grads_out), packed(ms), packed(vs), "adamw_small")
    pos = 0
    for n in small_names:
        size, shp = weights[n].size, weights[n].shape
        delta[n] = d_.reshape(-1)[pos:pos + size].reshape(shp)
        new_m[n] = m_.reshape(-1)[pos:pos + size].reshape(shp)
        new_v[n] = v_.reshape(-1)[pos:pos + size].reshape(shp)
        pos += size

    return (loss, dx.reshape(Bl, S, D), *[grads_out[n] for n in names], *[delta[n] for n in names],
            *[new_m[n] for n in names], *[new_v[n] for n in names])
```

```python
import functools
import math

import jax
import jax.numpy as jnp
from jax import lax
from jax.experimental import pallas as pl
from jax.experimental.pallas import tpu as pltpu

F32 = jnp.float32
BF16 = jnp.bfloat16
EPS = 1e-6
GROUP = 64
LANES = 128
BAND = 128
DILATIONS = (1, 4, 16)
NEG = -1e30
MIB = 1024 * 1024
MESH_ID = pl.DeviceIdType.MESH

ADAM_LR = 0.001
ADAM_B1 = 0.9
ADAM_B2 = 0.999
ADAM_EPS = 1e-08
ADAM_WD = 0.01
ADAM_STEP = 10


ANY = pl.BlockSpec(memory_space=pl.ANY)


def _params(sem=None, vmem_mb=48):
    return pltpu.CompilerParams(dimension_semantics=sem, vmem_limit_bytes=vmem_mb * MIB)


def _nt(a, b):
    return lax.dot_general(a, b, (((1,), (1,)), ((), ())), preferred_element_type=F32)


def _tn(a, b):
    return lax.dot_general(a, b, (((0,), (0,)), ((), ())), preferred_element_type=F32)


def _seg_sum(x, is_a):
    s_a = jnp.sum(jnp.where(is_a, x, 0.0), axis=-1, keepdims=True)
    s_b = jnp.sum(jnp.where(is_a, 0.0, x), axis=-1, keepdims=True)
    return jnp.where(is_a, s_a, s_b)


def _lane_is_a():
    return lax.broadcasted_iota(jnp.int32, (1, LANES), 1) < GROUP


def _norm_proj(x, g, w3, layer, groups, tm, chunk, name):
    T, D = x.shape
    N = w3.shape[2]
    assert sum(p * c for p, c, _ in groups) == N and T % tm == 0

    def body(x_ref, g_ref, w_ref, h_ref, *out_refs):
        xv = x_ref[...]
        rstd = lax.rsqrt(jnp.mean(xv * xv, axis=-1, keepdims=True) + EPS)
        h = ((xv * rstd) * g_ref[...]).astype(BF16)
        h_ref[...] = h.T
        col = 0
        for (pieces, width, dtype), o_ref in zip(groups, out_refs):
            for p in range(pieces):
                for c0 in range(0, width, chunk):
                    acc = jnp.dot(h, w_ref[:, col + c0:col + c0 + chunk], preferred_element_type=F32)
                    o_ref[p, :, c0:c0 + chunk] = acc.astype(dtype)
                col += width

    out_shape = [jax.ShapeDtypeStruct((D, T), BF16)]
    out_specs = [pl.BlockSpec((D, tm), lambda i: (0, i))]
    for pieces, width, dtype in groups:
        assert width % chunk == 0
        out_shape.append(jax.ShapeDtypeStruct((pieces, T, width), dtype))
        out_specs.append(pl.BlockSpec((pieces, tm, width), lambda i: (0, i, 0)))
    return pl.pallas_call(
        body, grid=(T // tm,), name=name,
        in_specs=[pl.BlockSpec((tm, D), lambda i: (i, 0)),
                  pl.BlockSpec((1, D), lambda i: (0, 0)),
                  pl.BlockSpec((None, D, N), lambda i: (layer, 0, 0))],
        out_specs=out_specs, out_shape=out_shape,
        compiler_params=_params(("parallel",), 56),
    )(x, g, w3)


def _proj_residual(pieces3, w3, layer, x, tm, name, after=None):
    P, T, C = pieces3.shape
    D = w3.shape[2]

    def body(a_ref, w_ref, x_ref, *rest):
        o_ref = rest[-1]
        acc = x_ref[...]
        for p in range(P):
            acc = acc + jnp.dot(a_ref[p], w_ref[p * C:(p + 1) * C, :], preferred_element_type=F32)
        o_ref[...] = acc

    in_specs = [pl.BlockSpec((P, tm, C), lambda i: (0, i, 0)),
                pl.BlockSpec((None, P * C, D), lambda i: (layer, 0, 0)),
                pl.BlockSpec((tm, D), lambda i: (i, 0))]
    operands = [pieces3, w3, x]
    if after is not None:
        in_specs.append(ANY)
        operands.append(after)
    return pl.pallas_call(
        body, grid=(T // tm,), name=name,
        in_specs=in_specs,
        out_specs=pl.BlockSpec((tm, D), lambda i: (i, 0)),
        out_shape=jax.ShapeDtypeStruct((T, D), F32),
        compiler_params=_params(("parallel",)),
    )(*operands)


def _grad_through_weight(dy, w3, layer, pieces, width, out_dtype, tm, chunk, name, after=None):
    T, D = dy.shape

    def body(dy_ref, w_ref, *rest):
        dyb_ref, o_ref = rest[-2:]
        dyb = dy_ref[...].astype(BF16)
        dyb_ref[...] = dyb
        for p in range(pieces):
            for c0 in range(0, width, chunk):
                r0 = p * width + c0
                o_ref[p, :, c0:c0 + chunk] = _nt(dyb, w_ref[r0:r0 + chunk, :]).astype(out_dtype)

    in_specs = [pl.BlockSpec((tm, D), lambda i: (i, 0)),
                pl.BlockSpec((None, pieces * width, D), lambda i: (layer, 0, 0))]
    operands = [dy, w3]
    if after is not None:
        in_specs.append(ANY)
        operands.append(after)
    return pl.pallas_call(
        body, grid=(T // tm,), name=name,
        in_specs=in_specs,
        out_specs=[pl.BlockSpec((tm, D), lambda i: (i, 0)),
                   pl.BlockSpec((pieces, tm, width), lambda i: (0, i, 0))],
        out_shape=[jax.ShapeDtypeStruct((T, D), BF16),
                   jax.ShapeDtypeStruct((pieces, T, width), out_dtype)],
        compiler_params=_params(("parallel",)),
    )(*operands)


def _grad_through_proj_norm(dp3, w3, layer, x, g, dx_in, tm, name):
    P, T, C = dp3.shape
    D = w3.shape[1]

    def body(dp_ref, w_ref, x_ref, g_ref, dxin_ref, dx_ref, dg_ref):
        dh = _nt(dp_ref[0], w_ref[:, 0:C])
        for p in range(1, P):
            dh = dh + _nt(dp_ref[p], w_ref[:, p * C:(p + 1) * C])
        xv = x_ref[...]
        rstd = lax.rsqrt(jnp.mean(xv * xv, axis=-1, keepdims=True) + EPS)
        xn = xv * rstd
        a = dh * g_ref[...]
        dx_ref[...] = dxin_ref[...] + rstd * (a - xn * jnp.mean(a * xn, axis=-1, keepdims=True))
        part = jnp.sum(dh * xn, axis=0, keepdims=True)

        @pl.when(pl.program_id(0) == 0)
        def _():
            dg_ref[...] = part

        @pl.when(pl.program_id(0) != 0)
        def _():
            dg_ref[...] += part

    return pl.pallas_call(
        body, grid=(T // tm,), name=name,
        in_specs=[pl.BlockSpec((P, tm, C), lambda i: (0, i, 0)),
                  pl.BlockSpec((None, D, P * C), lambda i: (layer, 0, 0)),
                  pl.BlockSpec((tm, D), lambda i: (i, 0)),
                  pl.BlockSpec((1, D), lambda i: (0, 0)),
                  pl.BlockSpec((tm, D), lambda i: (i, 0))],
        out_specs=[pl.BlockSpec((tm, D), lambda i: (i, 0)),
                   pl.BlockSpec((1, D), lambda i: (0, 0))],
        out_shape=[jax.ShapeDtypeStruct((T, D), F32), jax.ShapeDtypeStruct((1, D), F32)],
        compiler_params=_params(("arbitrary",), 56),
    )(dp3, w3, x, g, dx_in)


def _weight_grad(a3, g3, ta, tg, name, a_transposed=False):
    PG, T, CG = g3.shape
    PA, CA = (a3.shape[0], a3.shape[1]) if a_transposed else (a3.shape[0], a3.shape[2])
    na, ng = CA // ta, CG // tg
    assert CA % ta == 0 and CG % tg == 0

    def body(a_ref, g_ref, o_ref):
        if a_transposed:
            part = jnp.dot(a_ref[...], g_ref[...], preferred_element_type=F32)
        else:
            part = _tn(a_ref[...], g_ref[...])
        o_ref[...] = part.astype(o_ref.dtype)

    a_spec = (pl.BlockSpec((None, ta, T), lambda i, j: (i // na, i % na, 0)) if a_transposed
              else pl.BlockSpec((None, T, ta), lambda i, j: (i // na, 0, i % na)))
    return pl.pallas_call(
        body, grid=(PA * na, PG * ng), name=name,
        in_specs=[a_spec, pl.BlockSpec((None, T, tg), lambda i, j: (j // ng, 0, j % ng))],
        out_specs=pl.BlockSpec((None, ta, tg), lambda i, j: (0, i, j)),
        out_shape=jax.ShapeDtypeStruct((1, PA * CA, PG * CG), BF16),
        compiler_params=_params(("parallel", "parallel"), 56),
    )(a3, g3)


def _bias_tables(bm_ref, lone_ref, pair, n_heads, S):
    ii = lax.broadcasted_iota(jnp.int32, (BAND, 2 * BAND), 0)
    jj = lax.broadcasted_iota(jnp.int32, (BAND, 2 * BAND), 1)
    dist = BAND + ii - jj
    valid = (dist >= 0) & (dist <= BAND)
    distf = dist.astype(F32)
    for hh in range(2):
        head = (2 * pair + hh + 1).astype(F32)
        slope = jnp.exp(jnp.full((1, 1), -8.0 / n_heads * math.log(2.0), F32) * head)
        for bi, d in enumerate(DILATIONS):
            table = jnp.where(valid, -(slope * d) * distf, NEG)
            bm_ref[bi, hh * BAND:(hh + 1) * BAND, :] = table
            if S // (BAND * d) == 1:
                lone_ref[bi, hh * BAND:(hh + 1) * BAND, :] = table[:, BAND:2 * BAND]


def _stack_heads(x, is_a):
    zero = jnp.zeros_like(x)
    return jnp.concatenate([jnp.where(is_a, x, zero), jnp.where(is_a, zero, x)], axis=0)


def _unstack_heads(x2, is_a):
    return jnp.where(is_a, x2[0:BAND], x2[BAND:2 * BAND])


def _gather_residues(dst_ref, src, d, S, convert):
    L = S // d
    for r in range(d):
        rows = pl.ds(r, L, stride=d) if d > 1 else slice(None)
        dst_ref[r * L:(r + 1) * L, :] = convert(src(rows))


def _block_rows(t, d, S):
    nb = S // (BAND * d)
    n = t % nb
    has_prev = jnp.minimum(n, 1)
    cur = pl.ds(pl.multiple_of(t * BAND, BAND), BAND)
    prev = pl.ds(pl.multiple_of((t - has_prev) * BAND, BAND), BAND)
    return cur, prev, has_prev


def _first_block_penalty(has_prev):
    jrow = lax.broadcasted_iota(jnp.int32, (1, 2 * BAND), 1)
    pen = jnp.where(has_prev == 0, NEG, 0.0).astype(F32)
    return jnp.where(jrow < BAND, pen, 0.0)


def _attn_fwd(qkv3, gain, mix_shape_pieces, S, n_heads, name):
    _, T, C = qkv3.shape
    B, P = T // S, C // LANES
    NBLK = S // BAND
    scale = GROUP ** -0.5
    nbr = len(DILATIONS)
    RC = 256

    def body(qkv_ref, g_ref, o_ref, lse_ref, an_ref, qs, ks, vs, op, mp, lp, ob, mb, lb, bm, bml):
        pair = pl.program_id(1)
        is_a = _lane_is_a()
        _bias_tables(bm, bml, pair, n_heads, S)

        for bi, d in enumerate(DILATIONS):
            nb = S // (BAND * d)
            _gather_residues(qs, lambda rows: qkv_ref.at[0][rows, :], d, S, lambda v: (v * scale).astype(BF16))
            _gather_residues(ks, lambda rows: qkv_ref.at[1][rows, :], d, S, lambda v: v.astype(BF16))
            _gather_residues(vs, lambda rows: qkv_ref.at[2][rows, :], d, S, lambda v: v.astype(BF16))
            o_dst, m_dst, l_dst = (ob.at[bi], mb.at[bi], lb.at[bi]) if d == 1 else (op, mp, lp)

            def block(t, carry, bi=bi, d=d, nb=nb, o_dst=o_dst, m_dst=m_dst, l_dst=l_dst):
                cur, prev, has_prev = _block_rows(t, d, S)
                q2 = _stack_heads(qs[cur, :], is_a)
                if nb == 1:
                    kc, vc = ks[cur, :], vs[cur, :]
                    s = _nt(q2, kc) + bml[bi]
                else:
                    kc = jnp.concatenate([ks[prev, :], ks[cur, :]], axis=0)
                    vc = jnp.concatenate([vs[prev, :], vs[cur, :]], axis=0)
                    s = _nt(q2, kc) + bm[bi] + _first_block_penalty(has_prev)
                m = jnp.max(s, axis=-1, keepdims=True)
                e = jnp.exp(s - m)
                l = jnp.sum(e, axis=-1, keepdims=True)
                pv = jnp.dot(e.astype(BF16), vc, preferred_element_type=F32)
                o_dst[cur, :] = _unstack_heads(pv, is_a)
                m_dst[cur, :] = _unstack_heads(m, is_a)
                l_dst[cur, :] = _unstack_heads(l, is_a)
                return carry

            lax.fori_loop(0, NBLK, block, 0, unroll=8)
            if d > 1:
                L = S // d
                for r in range(d):
                    rows = pl.ds(r, L, stride=d)
                    ob.at[bi][rows, :] = op[r * L:(r + 1) * L, :]
                    mb.at[bi][rows, :] = mp[r * L:(r + 1) * L, :]
                    lb.at[bi][rows, :] = lp[r * L:(r + 1) * L, :]

        def finish(ci, carry):
            rs = pl.ds(pl.multiple_of(ci * RC, RC), RC)
            ms = [mb[bi, rs, :] for bi in range(nbr)]
            mmax = functools.reduce(jnp.maximum, ms)
            ws = [jnp.exp(m - mmax) for m in ms]
            num = sum(ob[bi, rs, :] * ws[bi] for bi in range(nbr))
            den = sum(lb[bi, rs, :] * ws[bi] for bi in range(nbr))
            o = num / den
            o_ref[rs, :] = o
            lse_ref[rs, :] = mmax + jnp.log(den)
            rstd = lax.rsqrt(_seg_sum(o * o, is_a) * (1.0 / GROUP) + EPS)
            an_ref[rs, :] = ((o * rstd) * g_ref[...]).astype(BF16)
            return carry

        lax.fori_loop(0, S // RC, finish, 0)

    seq = pl.BlockSpec((S, LANES), lambda b, p: (b, p))
    return pl.pallas_call(
        body, grid=(B, P), name=name,
        in_specs=[pl.BlockSpec((3, S, LANES), lambda b, p: (0, b, p)),
                  pl.BlockSpec((1, LANES), lambda b, p: (0, p))],
        out_specs=[seq, seq, pl.BlockSpec((None, S, LANES), lambda b, p: (0, b, p))],
        out_shape=[jax.ShapeDtypeStruct((T, C), F32), jax.ShapeDtypeStruct((T, C), F32),
                   jax.ShapeDtypeStruct((mix_shape_pieces, T, C), BF16)],
        scratch_shapes=[pltpu.VMEM((S, LANES), BF16)] * 3 + [pltpu.VMEM((S, LANES), F32)] * 3
        + [pltpu.VMEM((nbr, S, LANES), F32)] * 3
        + [pltpu.VMEM((nbr, 2 * BAND, 2 * BAND), F32), pltpu.VMEM((nbr, 2 * BAND, BAND), F32)],
        compiler_params=_params(("parallel", "parallel")),
    )(qkv3, gain)


def _attn_bwd(qkv3, o, lse, dmix3, gain, dproj_pieces, S, n_heads, name):
    _, T, C = qkv3.shape
    B, P = T // S, C // LANES
    NBLK = S // BAND
    scale = GROUP ** -0.5
    nbr = len(DILATIONS)
    RC = 256

    def body(qkv_ref, o_ref, lse_ref, dn_ref, g_ref, dqkv_ref, dg_ref,
             do_n, dd_n, qs, ks, vs, dos, lses, dds, dqp, dkp, dvp, dqn, dkn, dvn, bm, bml):
        pair = pl.program_id(0)
        b = pl.program_id(1)
        is_a = _lane_is_a()
        _bias_tables(bm, bml, pair, n_heads, S)

        def prologue(ci, dg_acc):
            rs = pl.ds(pl.multiple_of(ci * RC, RC), RC)
            ov = o_ref[rs, :]
            dn = dn_ref[rs, :]
            rstd = lax.rsqrt(_seg_sum(ov * ov, is_a) * (1.0 / GROUP) + EPS)
            on = ov * rstd
            a = dn * g_ref[...]
            s_a = _seg_sum(a * on, is_a)
            do_n[rs, :] = rstd * (a - on * (s_a * (1.0 / GROUP)))
            dd_n[rs, :] = (EPS * s_a) * (rstd * rstd)
            zero = jnp.zeros((RC, LANES), F32)
            dqn[rs, :] = zero
            dkn[rs, :] = zero
            dvn[rs, :] = zero
            return dg_acc + jnp.sum(dn * on, axis=0, keepdims=True)

        dg_part = lax.fori_loop(0, S // RC, prologue, jnp.zeros((1, LANES), F32))

        @pl.when(b == 0)
        def _():
            dg_ref[...] = dg_part

        @pl.when(b != 0)
        def _():
            dg_ref[...] += dg_part

        for bi, d in enumerate(DILATIONS):
            nb = S // (BAND * d)
            L = S // d
            _gather_residues(qs, lambda rows: qkv_ref.at[0][rows, :], d, S, lambda v: (v * scale).astype(BF16))
            _gather_residues(ks, lambda rows: qkv_ref.at[1][rows, :], d, S, lambda v: v.astype(BF16))
            _gather_residues(vs, lambda rows: qkv_ref.at[2][rows, :], d, S, lambda v: v.astype(BF16))
            _gather_residues(dos, lambda rows: do_n[rows, :], d, S, lambda v: v.astype(BF16))
            if d == 1:
                lse_src, dd_src, dq_dst, dk_dst, dv_dst = lse_ref, dd_n, dqn, dkn, dvn
            else:
                _gather_residues(lses, lambda rows: lse_ref[rows, :], d, S, lambda v: v)
                _gather_residues(dds, lambda rows: dd_n[rows, :], d, S, lambda v: v)
                dkp[...] = jnp.zeros((S, LANES), F32)
                dvp[...] = jnp.zeros((S, LANES), F32)
                lse_src, dd_src, dq_dst, dk_dst, dv_dst = lses, dds, dqp, dkp, dvp

            def block(t, carry, bi=bi, d=d, nb=nb, lse_src=lse_src, dd_src=dd_src, dq_dst=dq_dst, dk_dst=dk_dst,
                      dv_dst=dv_dst):
                cur, prev, has_prev = _block_rows(t, d, S)
                q2 = _stack_heads(qs[cur, :], is_a)
                do2 = _stack_heads(dos[cur, :], is_a)
                lse_t = lse_src[cur, :]
                dd_t = dd_src[cur, :]
                lse2 = jnp.concatenate([lse_t[:, 0:1], lse_t[:, GROUP:GROUP + 1]], axis=0)
                dd2 = jnp.concatenate([dd_t[:, 0:1], dd_t[:, GROUP:GROUP + 1]], axis=0)
                if nb == 1:
                    kc, vc = ks[cur, :], vs[cur, :]
                    s = _nt(q2, kc) + bml[bi]
                else:
                    kc = jnp.concatenate([ks[prev, :], ks[cur, :]], axis=0)
                    vc = jnp.concatenate([vs[prev, :], vs[cur, :]], axis=0)
                    s = _nt(q2, kc) + bm[bi] + _first_block_penalty(has_prev)
                p = jnp.exp(s - lse2)
                ds = (p * (_nt(do2, vc) - dd2)).astype(BF16)
                dq = _unstack_heads(jnp.dot(ds, kc, preferred_element_type=F32), is_a)
                dk = _tn(ds, q2)
                dv = _tn(p.astype(BF16), do2)
                dq_dst[cur, :] = dq
                if nb == 1:
                    dk_dst[cur, :] += dk
                    dv_dst[cur, :] += dv
                else:
                    dk_dst[prev, :] += dk[0:BAND, :]
                    dv_dst[prev, :] += dv[0:BAND, :]
                    dk_dst[cur, :] += dk[BAND:2 * BAND, :]
                    dv_dst[cur, :] += dv[BAND:2 * BAND, :]
                return carry

            lax.fori_loop(0, NBLK, block, 0, unroll=8)
            if d > 1:
                for r in range(d):
                    rows = pl.ds(r, L, stride=d)
                    dqn[rows, :] += dqp[r * L:(r + 1) * L, :]
                    dkn[rows, :] += dkp[r * L:(r + 1) * L, :]
                    dvn[rows, :] += dvp[r * L:(r + 1) * L, :]

        dqkv_ref[0] = (dqn[...] * scale).astype(BF16)
        dqkv_ref[1] = dkn[...].astype(BF16)
        dqkv_ref[2] = dvn[...].astype(BF16)

    seq = pl.BlockSpec((S, LANES), lambda p, b: (b, p))
    f32_seq = pltpu.VMEM((S, LANES), F32)
    bf_seq = pltpu.VMEM((S, LANES), BF16)
    return pl.pallas_call(
        body, grid=(P, B), name=name,
        in_specs=[pl.BlockSpec((3, S, LANES), lambda p, b: (0, b, p)), seq, seq,
                  pl.BlockSpec((None, S, LANES), lambda p, b: (0, b, p)),
                  pl.BlockSpec((1, LANES), lambda p, b: (0, p))],
        out_specs=[pl.BlockSpec((3, S, LANES), lambda p, b: (0, b, p)),
                   pl.BlockSpec((1, LANES), lambda p, b: (0, p))],
        out_shape=[jax.ShapeDtypeStruct((dproj_pieces, T, C), BF16), jax.ShapeDtypeStruct((1, C), F32)],
        scratch_shapes=[f32_seq, f32_seq, bf_seq, bf_seq, bf_seq, bf_seq, f32_seq, f32_seq,
                        f32_seq, f32_seq, f32_seq, f32_seq, f32_seq, f32_seq,
                        pltpu.VMEM((nbr, 2 * BAND, 2 * BAND), F32), pltpu.VMEM((nbr, 2 * BAND, BAND), F32)],
        compiler_params=_params(("parallel", "arbitrary")),
    )(qkv3, o, lse, dmix3, gain)


def _delay(x, k, row):
    return jnp.where(row >= k, pltpu.roll(x, k, 0), 0.0)


def _advance(x, k, row, S):
    return jnp.where(row < S - k, pltpu.roll(x, S - k, 0), 0.0)


def _conv3(x, w, row):
    return (w[0:1, :] * _delay(x, 2, row) + w[1:2, :] * _delay(x, 1, row)) + w[2:3, :] * x


HALO = 8


CONV_ROWS = 128
FFN_LANES = 128


def _zero_halo(pad_ref, S):
    zeros = jnp.zeros((HALO, pad_ref.shape[1]), pad_ref.dtype)
    pad_ref[0:HALO, :] = zeros
    pad_ref[HALO + S:2 * HALO + S, :] = zeros


def _window_at(pad_ref, r0, shift):
    return pad_ref[HALO + r0 + shift:HALO + r0 + shift + CONV_ROWS, :]


def _conv3_at(pad_ref, w, r0):
    return ((w[0:1, :] * _window_at(pad_ref, r0, -2) + w[1:2, :] * _window_at(pad_ref, r0, -1))
            + w[2:3, :] * _window_at(pad_ref, r0, 0))


def _conv3_grads_at(dz_ref, x_ref, w, r0):
    dz, dz1, dz2 = (_window_at(dz_ref, r0, k) for k in range(3))
    x = _window_at(x_ref, r0, 0)
    dx = (w[2:3, :] * dz + w[1:2, :] * dz1) + w[0:1, :] * dz2
    parts = [jnp.sum((d * x).reshape(CONV_ROWS // 8, 8, x.shape[1]), axis=0) for d in (dz2, dz1, dz)]
    return dx, parts


def _conv3_grads(dz, x, w, row, S):
    dz1 = _advance(dz, 1, row, S)
    dz2 = _advance(dz, 2, row, S)
    dx = (w[2:3, :] * dz + w[1:2, :] * dz1) + w[0:1, :] * dz2
    dw = jnp.concatenate([jnp.sum(dz2 * x, axis=0, keepdims=True),
                          jnp.sum(dz1 * x, axis=0, keepdims=True),
                          jnp.sum(dz * x, axis=0, keepdims=True)], axis=0)
    return dx, dw


def _mix_conv_fwd(cv3, taps, gain, mix, S, name, after=None):
    _, T, C = cv3.shape
    B, P = T // S, C // LANES

    def body(cv_ref, w_ref, g_ref, mix_hbm, *rest):
        y_ref, pad_c = rest[-2:]
        del mix_hbm
        is_a = _lane_is_a()
        _zero_halo(pad_c, S)
        pad_c[HALO:HALO + S, :] = cv_ref[1].astype(F32) * cv_ref[2].astype(F32)
        w = w_ref[...]
        for r0 in range(0, S, CONV_ROWS):
            y = cv_ref[0, r0:r0 + CONV_ROWS, :].astype(F32) * _conv3_at(pad_c, w, r0)
            rstd = lax.rsqrt(_seg_sum(y * y, is_a) * (1.0 / GROUP) + EPS)
            y_ref[r0:r0 + CONV_ROWS, :] = ((y * rstd) * g_ref[...]).astype(BF16)

    in_specs = [pl.BlockSpec((3, S, LANES), lambda b, p: (0, b, p)),
                pl.BlockSpec((3, LANES), lambda b, p: (0, p)),
                pl.BlockSpec((1, LANES), lambda b, p: (0, p)),
                ANY]
    operands = [cv3, taps, gain, mix]
    if after is not None:
        in_specs.append(ANY)
        operands.append(after)
    return pl.pallas_call(
        body, grid=(B, P), name=name,
        in_specs=in_specs,
        out_specs=pl.BlockSpec((None, S, LANES), lambda b, p: (1, b, p)),
        out_shape=jax.ShapeDtypeStruct(mix.shape, mix.dtype),
        scratch_shapes=[pltpu.VMEM((S + 2 * HALO, LANES), F32)],
        input_output_aliases={3: 0},
        compiler_params=_params(("parallel", "parallel")),
    )(*operands)


def _mix_conv_bwd(cv3, dmix3, taps, gain, dproj, S, name):
    _, T, C = cv3.shape
    B, P = T // S, C // LANES

    def body(cv_ref, dn_ref, w_ref, g_ref, dproj_hbm, dcv_ref, dw_ref, dg_ref):
        del dproj_hbm
        b = pl.program_id(1)
        row = lax.broadcasted_iota(jnp.int32, (S, 1), 0)
        is_a = _lane_is_a()
        w = w_ref[...]
        gb = cv_ref[0].astype(F32)
        gc = cv_ref[1].astype(F32)
        u = cv_ref[2].astype(F32)
        c = gc * u
        z = _conv3(c, w, row)
        y = gb * z
        rstd = lax.rsqrt(_seg_sum(y * y, is_a) * (1.0 / GROUP) + EPS)
        yn = y * rstd
        dn = dn_ref[...]
        a = dn * g_ref[...]
        dy = rstd * (a - yn * (_seg_sum(a * yn, is_a) * (1.0 / GROUP)))
        dg = jnp.sum(dn * yn, axis=0, keepdims=True)
        dc, dw = _conv3_grads(dy * gb, c, w, row, S)
        dcv_ref[0] = (dy * z).astype(BF16)
        dcv_ref[1] = (dc * u).astype(BF16)
        dcv_ref[2] = (dc * gc).astype(BF16)

        @pl.when(b == 0)
        def _():
            dw_ref[...] = dw
            dg_ref[...] = dg

        @pl.when(b != 0)
        def _():
            dw_ref[...] += dw
            dg_ref[...] += dg

    return pl.pallas_call(
        body, grid=(P, B), name=name,
        in_specs=[pl.BlockSpec((3, S, LANES), lambda p, b: (0, b, p)),
                  pl.BlockSpec((None, S, LANES), lambda p, b: (1, b, p)),
                  pl.BlockSpec((3, LANES), lambda p, b: (0, p)),
                  pl.BlockSpec((1, LANES), lambda p, b: (0, p)),
                  pl.BlockSpec(memory_space=pl.ANY)],
        out_specs=[pl.BlockSpec((3, S, LANES), lambda p, b: (1, b, p)),
                   pl.BlockSpec((3, LANES), lambda p, b: (0, p)),
                   pl.BlockSpec((1, LANES), lambda p, b: (0, p))],
        out_shape=[jax.ShapeDtypeStruct(dproj.shape, dproj.dtype),
                   jax.ShapeDtypeStruct((3, C), F32), jax.ShapeDtypeStruct((1, C), F32)],
        input_output_aliases={4: 0},
        compiler_params=_params(("parallel", "arbitrary")),
    )(cv3, dmix3, taps, gain, dproj)


def _sigmoid(x):
    return 0.5 * jnp.tanh(0.5 * x) + 0.5


def _ffn_act_fwd(up3, taps, S, name):
    _, T, Fd = up3.shape
    W = FFN_LANES
    B, P = T // S, Fd // W

    def body(up_ref, wg_ref, wv_ref, act_ref, pad_g, pad_v):
        _zero_halo(pad_g, S)
        _zero_halo(pad_v, S)
        pad_g[HALO:HALO + S, :] = up_ref[0].astype(F32)
        pad_v[HALO:HALO + S, :] = up_ref[1].astype(F32)
        wg = wg_ref[...]
        wv = wv_ref[...]
        for r0 in range(0, S, CONV_ROWS):
            cg = _conv3_at(pad_g, wg, r0)
            cv = _conv3_at(pad_v, wv, r0)
            act_ref[r0:r0 + CONV_ROWS, :] = ((cg * _sigmoid(cg)) * cv).astype(BF16)

    return pl.pallas_call(
        body, grid=(B, P), name=name,
        in_specs=[pl.BlockSpec((2, S, W), lambda b, p: (0, b, p)),
                  pl.BlockSpec((3, W), lambda b, p: (0, p)),
                  pl.BlockSpec((3, W), lambda b, p: (0, P + p))],
        out_specs=pl.BlockSpec((S, W), lambda b, p: (b, p)),
        out_shape=jax.ShapeDtypeStruct((T, Fd), BF16),
        scratch_shapes=[pltpu.VMEM((S + 2 * HALO, W), F32)] * 2,
        compiler_params=_params(("parallel", "parallel")),
    )(up3, taps, taps)


def _ffn_act_bwd(up3, dact3, taps, S, name):
    _, T, Fd = up3.shape
    W = FFN_LANES
    B, P = T // S, Fd // W

    def body(up_ref, da_ref, wg_ref, wv_ref, dup_ref, dwg_ref, dwv_ref, pad_ug, pad_uv, pad_dg, pad_dv):
        b = pl.program_id(1)
        for pad in (pad_ug, pad_uv, pad_dg, pad_dv):
            _zero_halo(pad, S)
        pad_ug[HALO:HALO + S, :] = up_ref[0].astype(F32)
        pad_uv[HALO:HALO + S, :] = up_ref[1].astype(F32)
        wg = wg_ref[...]
        wv = wv_ref[...]
        for r0 in range(0, S, CONV_ROWS):
            cg = _conv3_at(pad_ug, wg, r0)
            cv = _conv3_at(pad_uv, wv, r0)
            sg = _sigmoid(cg)
            da = da_ref[r0:r0 + CONV_ROWS, :].astype(F32)
            t = da * sg
            dcv = cg * t
            pad_dg[HALO + r0:HALO + r0 + CONV_ROWS, :] = cv * ((t + dcv) - dcv * sg)
            pad_dv[HALO + r0:HALO + r0 + CONV_ROWS, :] = dcv
        sums_g = [jnp.zeros((8, W), F32)] * 3
        sums_v = [jnp.zeros((8, W), F32)] * 3
        for r0 in range(0, S, CONV_ROWS):
            dug, parts_g = _conv3_grads_at(pad_dg, pad_ug, wg, r0)
            duv, parts_v = _conv3_grads_at(pad_dv, pad_uv, wv, r0)
            dup_ref[0, r0:r0 + CONV_ROWS, :] = dug.astype(BF16)
            dup_ref[1, r0:r0 + CONV_ROWS, :] = duv.astype(BF16)
            sums_g = [a + p for a, p in zip(sums_g, parts_g)]
            sums_v = [a + p for a, p in zip(sums_v, parts_v)]
        dwg = jnp.concatenate([jnp.sum(a, axis=0, keepdims=True) for a in sums_g], axis=0)
        dwv = jnp.concatenate([jnp.sum(a, axis=0, keepdims=True) for a in sums_v], axis=0)

        @pl.when(b == 0)
        def _():
            dwg_ref[...] = dwg
            dwv_ref[...] = dwv

        @pl.when(b != 0)
        def _():
            dwg_ref[...] += dwg
            dwv_ref[...] += dwv

    tap_out = pl.BlockSpec((3, W), lambda p, b: (0, p))
    return pl.pallas_call(
        body, grid=(P, B), name=name,
        in_specs=[pl.BlockSpec((2, S, W), lambda p, b: (0, b, p)),
                  pl.BlockSpec((None, S, W), lambda p, b: (0, b, p)),
                  pl.BlockSpec((3, W), lambda p, b: (0, p)),
                  pl.BlockSpec((3, W), lambda p, b: (0, P + p))],
        out_specs=[pl.BlockSpec((2, S, W), lambda p, b: (0, b, p)), tap_out, tap_out],
        out_shape=[jax.ShapeDtypeStruct((2, T, Fd), BF16),
                   jax.ShapeDtypeStruct((3, Fd), F32), jax.ShapeDtypeStruct((3, Fd), F32)],
        scratch_shapes=[pltpu.VMEM((S + 2 * HALO, W), F32)] * 4,
        compiler_params=_params(("parallel", "arbitrary")),
    )(up3, dact3, taps, taps)


def _final_norm_loss(x, g, target, tm, name):
    T, D = x.shape

    def body(x_ref, g_ref, t_ref, dx_ref, dg_ref, loss_ref):
        xv = x_ref[...]
        rstd = lax.rsqrt(jnp.mean(xv * xv, axis=-1, keepdims=True) + EPS)
        xn = xv * rstd
        err = xn * g_ref[...] - t_ref[...]
        part = 0.5 * jnp.sum(jnp.mean(err * err, axis=-1, keepdims=True), axis=0, keepdims=True)
        dy = err * (1.0 / D)
        a = dy * g_ref[...]
        dx_ref[...] = rstd * (a - xn * jnp.mean(a * xn, axis=-1, keepdims=True))
        dg = jnp.sum(dy * xn, axis=0, keepdims=True)
        lpart = jnp.broadcast_to(part, (1, LANES))

        @pl.when(pl.program_id(0) == 0)
        def _():
            dg_ref[...] = dg
            loss_ref[...] = lpart

        @pl.when(pl.program_id(0) != 0)
        def _():
            dg_ref[...] += dg
            loss_ref[...] += lpart

    row = pl.BlockSpec((tm, D), lambda i: (i, 0))
    return pl.pallas_call(
        body, grid=(T // tm,), name=name,
        in_specs=[row, pl.BlockSpec((1, D), lambda i: (0, 0)), row],
        out_specs=[row, pl.BlockSpec((1, D), lambda i: (0, 0)), pl.BlockSpec((1, LANES), lambda i: (0, 0))],
        out_shape=[jax.ShapeDtypeStruct((T, D), F32), jax.ShapeDtypeStruct((1, D), F32),
                   jax.ShapeDtypeStruct((1, LANES), F32)],
        compiler_params=_params(("arbitrary",)),
    )(x, g, target)


def _row_tile(rows, cols, budget_elems=512 * 1024):
    tr = rows
    while tr * cols > budget_elems and tr % 32 == 0:
        tr //= 2
    return tr


def _prefetch_call(body, grid, in_specs, out_specs, out_shape, name, sem, aliases=None):
    return pl.pallas_call(
        body, name=name, out_shape=out_shape,
        grid_spec=pltpu.PrefetchScalarGridSpec(num_scalar_prefetch=1, grid=grid, in_specs=in_specs,
                                               out_specs=out_specs),
        input_output_aliases=aliases or {},
        compiler_params=_params(sem))


def _cast_into_full(w, layer, colwise, where, name):
    _, K, N = w.shape
    tr = _row_tile(K, N)
    nrb = K // tr
    full_shape = (1, K, 4 * N) if colwise else (1, 4 * K, N)

    def body(where_ref, w_ref, o_ref):
        del where_ref
        o_ref[...] = w_ref[...].astype(BF16)

    if colwise:
        out_map = lambda i, wh: (0, i, wh[0])
    else:
        out_map = lambda i, wh: (0, wh[0] * nrb + i, 0)
    return _prefetch_call(
        body, (nrb,), [pl.BlockSpec((None, tr, N), lambda i, wh: (layer, i, 0))],
        pl.BlockSpec((None, tr, N), out_map), jax.ShapeDtypeStruct(full_shape, BF16), name,
        ("parallel",))(where, w)


def _chip_sum(g3, other, colwise, where, name):
    L, K, N = g3.shape
    hk, hn = (K // 2, N) if colwise else (K, N // 2)
    tr = _row_tile(hk, hn)
    nrb = hk // tr

    def body(where_ref, g_ref, o_ref, s_ref):
        del where_ref
        s_ref[...] = (g_ref[...].astype(F32) + o_ref[...].astype(F32)).astype(BF16)

    if colwise:
        g_map = lambda l, i, wh: (l, wh[1] * nrb + i, 0)
    else:
        g_map = lambda l, i, wh: (l, i, wh[1])
    blk = pl.BlockSpec((None, tr, hn), lambda l, i, wh: (l, i, 0))
    return _prefetch_call(
        body, (L, nrb), [pl.BlockSpec((None, tr, hn), g_map), blk], blk,
        jax.ShapeDtypeStruct((L, hk, hn), BF16), name, ("parallel", "parallel"))(where, g3, other)


def _owner_sum(chip_sum, received, colwise, where, layer, n_layers, prev, name):
    _, hk, hn = chip_sum.shape
    pk, pn = (hk, hn // 4) if colwise else (hk // 4, hn)
    tr = _row_tile(pk, pn)
    nrb = pk // tr
    shard_shape = (n_layers, 2 * pk, pn) if colwise else (n_layers, pk, 2 * pn)

    def body(where_ref, own_ref, rec_ref, *rest):
        del where_ref
        o_ref = rest[-1]
        acc = own_ref[...].astype(F32)
        for j in range(3):
            acc = acc + rec_ref[j].astype(F32)
        o_ref[...] = acc

    if colwise:
        own_map = lambda i, wh: (0, i, wh[0])
        out_map = lambda i, wh: (layer, wh[1] * nrb + i, 0)
    else:
        own_map = lambda i, wh: (0, wh[0] * nrb + i, 0)
        out_map = lambda i, wh: (layer, i, wh[1])
    in_specs = [pl.BlockSpec((None, tr, pn), own_map),
                pl.BlockSpec((3, None, tr, pn), lambda i, wh: (0, 0, i, 0))]
    operands = [where, chip_sum, received]
    if prev is not None:
        in_specs.append(ANY)
        operands.append(prev)
    return _prefetch_call(
        body, (nrb,), in_specs, pl.BlockSpec((None, tr, pn), out_map), jax.ShapeDtypeStruct(shard_shape, F32), name,
        ("parallel",), None if prev is None else {3: 0})(*operands)


def _adamw(w, g, m, v, name):
    R, Cc = w.shape
    tr = _row_tile(R, Cc, 256 * 1024)

    def body(w_ref, g_ref, m_ref, v_ref, d_ref, nm_ref, nv_ref, go_ref):
        gv = g_ref[...]
        go_ref[...] = gv
        nm = ADAM_B1 * m_ref[...] + (1.0 - ADAM_B1) * gv
        nv = ADAM_B2 * v_ref[...] + (1.0 - ADAM_B2) * (gv * gv)
        m_hat = nm / (1.0 - ADAM_B1 ** ADAM_STEP)
        v_hat = nv / (1.0 - ADAM_B2 ** ADAM_STEP)
        d_ref[...] = -ADAM_LR * (m_hat / (jnp.sqrt(v_hat) + ADAM_EPS) + ADAM_WD * w_ref[...])
        nm_ref[...] = nm
        nv_ref[...] = nv

    blk = pl.BlockSpec((tr, Cc), lambda i: (i, 0))
    shp = jax.ShapeDtypeStruct((R, Cc), F32)
    return pl.pallas_call(
        body, grid=(R // tr,), name=name,
        in_specs=[blk] * 4, out_specs=[blk] * 4, out_shape=[shp] * 4,
        compiler_params=_params(("parallel",)),
    )(w, g, m, v)


COL_SHARDED = (True, False, True, False)


def _position():
    x, y, c = lax.axis_index("x"), lax.axis_index("y"), lax.axis_index("c")
    chips = [(1 - x, y), (x, 1 - y), (1 - x, 1 - y)]
    return x, y, c, chips


def _span(index, size, align):
    return pl.ds(pl.multiple_of(index * size, align), size)


def _window(ref, colwise, shard, half, shards=4):
    _, K, N = ref.shape
    rows = cols = slice(None)
    if colwise:
        if half is not None:
            rows = _span(half, K // 2, 16)
        if shard is not None:
            cols = _span(shard, N // shards, LANES)
    else:
        if shard is not None:
            rows = _span(shard, K // shards, 16)
        if half is not None:
            cols = _span(half, N // 2, LANES)
    return ref.at[:, rows, cols]


HBM = pl.BlockSpec(memory_space=pltpu.HBM)
SEMAPHORES = pl.BlockSpec(memory_space=pltpu.SEMAPHORE)


def _gather_start(fulls, colwise, group_sizes, after, name):
    n = len(fulls)
    n_groups = len(group_sizes)

    n_in = n if after is None else n + 1

    def body(*refs):
        ins = refs[:n]
        sems = refs[n_in:n_in + 2 * n_groups]
        x, y, c, chips = _position()
        me = 2 * x + y
        i = 0
        for g, size in enumerate(group_sizes):
            for a in range(size):
                win = _window(ins[i], colwise[i], me, c)
                for j, chip in enumerate(chips):
                    pltpu.make_async_remote_copy(
                        src_ref=win, dst_ref=win, send_sem=sems[2 * g].at[a * 3 + j],
                        recv_sem=sems[2 * g + 1].at[a * 3 + j],
                        device_id=(chip[0], chip[1], c), device_id_type=MESH_ID).start()
                i += 1

    sem_shapes = []
    for size in group_sizes:
        sem_shapes += [pltpu.SemaphoreType.DMA((3 * size,)), pltpu.SemaphoreType.DMA((3 * size,))]
    operands = [pltpu.with_memory_space_constraint(f, pltpu.HBM) for f in fulls]
    in_specs = [HBM] * n
    if after is not None:
        operands.append(after)
        in_specs.append(ANY)
    outs = pl.pallas_call(
        body, name=name,
        in_specs=in_specs, out_specs=[SEMAPHORES] * (2 * n_groups) + [HBM] * n,
        out_shape=sem_shapes + [pltpu.HBM(f.shape, f.dtype) for f in fulls],
        input_output_aliases={i: 2 * n_groups + i for i in range(n)},
        compiler_params=pltpu.CompilerParams(has_side_effects=pltpu.SideEffectType.DATAFLOW_SIDE_EFFECTING),
    )(*operands)
    sems = [(outs[2 * g], outs[2 * g + 1]) for g in range(n_groups)]
    return sems, list(outs[2 * n_groups:])


def _to_sibling(ref, colwise, chip, half, x, y, c, send_sem, recv_sem):
    win = _window(ref, colwise, 2 * chip[0] + chip[1], half)
    return pltpu.make_async_remote_copy(
        src_ref=win, dst_ref=win, send_sem=send_sem, recv_sem=recv_sem,
        device_id=(x, y, 1 - c), device_id_type=MESH_ID)


def _gather_pass(in_flight, colwise, sems, after, name):
    n = len(in_flight)

    def body(*refs):
        ins = refs[:n]
        send_sems, recv_sems = refs[n], refs[n + 1]
        pass_send, pass_recv = refs[-2 - n], refs[-1 - n]
        x, y, c, chips = _position()
        me = 2 * x + y
        for a in range(n):
            for j, chip in enumerate(chips):
                k = a * 3 + j
                pltpu.make_async_remote_copy(
                    src_ref=_window(ins[a], colwise[a], me, c),
                    dst_ref=_window(ins[a], colwise[a], 2 * chip[0] + chip[1], c),
                    send_sem=send_sems.at[k], recv_sem=recv_sems.at[k],
                    device_id=(chip[0], chip[1], c), device_id_type=MESH_ID).wait()
                _to_sibling(ins[a], colwise[a], chip, c, x, y, c, pass_send.at[k], pass_recv.at[k]).start()

    operands = list(in_flight) + list(sems)
    in_specs = [HBM] * n + [SEMAPHORES] * 2
    if after is not None:
        operands.append(after)
        in_specs.append(ANY)
    outs = pl.pallas_call(
        body, name=name,
        in_specs=in_specs, out_specs=[SEMAPHORES] * 2 + [HBM] * n,
        out_shape=[pltpu.SemaphoreType.DMA((3 * n,)), pltpu.SemaphoreType.DMA((3 * n,))]
        + [pltpu.HBM(f.shape, f.dtype) for f in in_flight],
        input_output_aliases={i: 2 + i for i in range(n)},
        compiler_params=pltpu.CompilerParams(has_side_effects=pltpu.SideEffectType.DATAFLOW_SIDE_EFFECTING),
    )(*operands)
    return (outs[0], outs[1]), list(outs[2:])


def _gather_wait(in_flight, colwise, sems, after, name):
    n = len(in_flight)

    def body(*refs):
        ins = refs[:n]
        send_sems, recv_sems = refs[n], refs[n + 1]
        x, y, c, chips = _position()
        for a in range(n):
            for j, chip in enumerate(chips):
                k = a * 3 + j
                _to_sibling(ins[a], colwise[a], chip, c, x, y, c, send_sems.at[k], recv_sems.at[k]).wait_send()
                _to_sibling(ins[a], colwise[a], chip, 1 - c, x, y, c, send_sems.at[k], recv_sems.at[k]).wait_recv()

    operands = list(in_flight) + list(sems)
    in_specs = [HBM] * n + [SEMAPHORES] * 2
    if after is not None:
        operands.append(after)
        in_specs.append(ANY)
    outs = pl.pallas_call(
        body, name=name,
        in_specs=in_specs, out_specs=[HBM] * n,
        out_shape=[pltpu.HBM(f.shape, f.dtype) for f in in_flight],
        input_output_aliases={i: i for i in range(n)},
        compiler_params=pltpu.CompilerParams(has_side_effects=pltpu.SideEffectType.DATAFLOW_SIDE_EFFECTING),
    )(*operands)
    return list(outs)


def _exchange_copy(g_ref, land_ref, colwise, x, y, c, send_sem, recv_sem):
    return pltpu.make_async_remote_copy(
        src_ref=_window(g_ref, colwise, None, 1 - c), dst_ref=land_ref, send_sem=send_sem, recv_sem=recv_sem,
        device_id=(x, y, 1 - c), device_id_type=MESH_ID)


def _exchange_start(grads, colwise, name):
    n = len(grads)
    lands = []
    for g, cw in zip(grads, colwise):
        L, K, N = g.shape
        lands.append(lax.empty((L, K // 2, N) if cw else (L, K, N // 2), g.dtype))

    def body(*refs):
        src, land = refs[:n], refs[n:2 * n]
        send_sems, recv_sems = refs[2 * n], refs[2 * n + 1]
        x, y, c, _ = _position()
        for i in range(n):
            _exchange_copy(src[i], land[i], colwise[i], x, y, c, send_sems.at[i], recv_sems.at[i]).start()

    arrays = list(grads) + lands
    outs = pl.pallas_call(
        body, name=name,
        in_specs=[HBM] * (2 * n), out_specs=[SEMAPHORES] * 2 + [HBM] * (2 * n),
        out_shape=[pltpu.SemaphoreType.DMA((n,)), pltpu.SemaphoreType.DMA((n,))]
        + [pltpu.HBM(a.shape, a.dtype) for a in arrays],
        input_output_aliases={i: 2 + i for i in range(2 * n)},
        compiler_params=pltpu.CompilerParams(has_side_effects=pltpu.SideEffectType.DATAFLOW_SIDE_EFFECTING),
    )(*[pltpu.with_memory_space_constraint(a, pltpu.HBM) for a in arrays])
    return (outs[0], outs[1]), list(outs[2:2 + n]), list(outs[2 + n:])


def _exchange_wait(grads, lands, colwise, sems, after, name):
    n = len(grads)

    def body(*refs):
        src, land = refs[:n], refs[n:2 * n]
        send_sems, recv_sems = refs[2 * n], refs[2 * n + 1]
        x, y, c, _ = _position()
        for i in range(n):
            _exchange_copy(src[i], land[i], colwise[i], x, y, c, send_sems.at[i], recv_sems.at[i]).wait()

    arrays = list(grads) + list(lands)
    operands = arrays + list(sems)
    in_specs = [HBM] * (2 * n) + [SEMAPHORES] * 2
    if after is not None:
        operands.append(after)
        in_specs.append(ANY)
    outs = pl.pallas_call(
        body, name=name,
        in_specs=in_specs, out_specs=[HBM] * (2 * n),
        out_shape=[pltpu.HBM(a.shape, a.dtype) for a in arrays],
        input_output_aliases={i: i for i in range(2 * n)},
        compiler_params=pltpu.CompilerParams(has_side_effects=pltpu.SideEffectType.DATAFLOW_SIDE_EFFECTING),
    )(*operands)
    return list(outs[:n]), list(outs[n:])


def _scatter_copy(src_ref, land_ref, colwise, j, chip, c, send_sem, recv_sem):
    return pltpu.make_async_remote_copy(
        src_ref=_window(src_ref, colwise, 2 * chip[0] + chip[1], None), dst_ref=land_ref.at[j],
        send_sem=send_sem, recv_sem=recv_sem, device_id=(chip[0], chip[1], c), device_id_type=MESH_ID)


def _scatter_start(chip_sums, colwise, name):
    n = len(chip_sums)
    lands = []
    for g, cw in zip(chip_sums, colwise):
        L, hk, hn = g.shape
        lands.append(lax.empty((3, L, hk, hn // 4) if cw else (3, L, hk // 4, hn), g.dtype))

    def body(*refs):
        src, land = refs[:n], refs[n:2 * n]
        send_sems, recv_sems = refs[2 * n], refs[2 * n + 1]
        x, y, c, chips = _position()
        for i in range(n):
            for j, chip in enumerate(chips):
                _scatter_copy(src[i], land[i], colwise[i], j, chip, c, send_sems.at[i * 3 + j],
                              recv_sems.at[i * 3 + j]).start()

    arrays = list(chip_sums) + lands
    outs = pl.pallas_call(
        body, name=name,
        in_specs=[HBM] * (2 * n), out_specs=[SEMAPHORES] * 2 + [HBM] * (2 * n),
        out_shape=[pltpu.SemaphoreType.DMA((3 * n,)), pltpu.SemaphoreType.DMA((3 * n,))]
        + [pltpu.HBM(a.shape, a.dtype) for a in arrays],
        input_output_aliases={i: 2 + i for i in range(2 * n)},
        compiler_params=pltpu.CompilerParams(has_side_effects=pltpu.SideEffectType.DATAFLOW_SIDE_EFFECTING),
    )(*[pltpu.with_memory_space_constraint(a, pltpu.HBM) for a in arrays])
    return (outs[0], outs[1]), list(outs[2:2 + n]), list(outs[2 + n:])


def _scatter_wait(sources, lands, colwise, sems, after, name):
    n = len(sources)

    def body(*refs):
        src, land = refs[:n], refs[n:2 * n]
        send_sems, recv_sems = refs[2 * n], refs[2 * n + 1]
        x, y, c, chips = _position()
        for i in range(n):
            for j, chip in enumerate(chips):
                cp = _scatter_copy(src[i], land[i], colwise[i], j, chip, c, send_sems.at[i * 3 + j],
                                   recv_sems.at[i * 3 + j])
                cp.wait_send()
                cp.wait_recv()

    arrays = list(sources) + list(lands)
    operands = arrays + list(sems)
    in_specs = [HBM] * (2 * n) + [SEMAPHORES] * 2
    if after is not None:
        operands.append(after)
        in_specs.append(ANY)
    outs = pl.pallas_call(
        body, name=name,
        in_specs=in_specs, out_specs=[HBM] * (2 * n),
        out_shape=[pltpu.HBM(a.shape, a.dtype) for a in arrays],
        input_output_aliases={i: i for i in range(2 * n)},
        compiler_params=pltpu.CompilerParams(has_side_effects=pltpu.SideEffectType.DATAFLOW_SIDE_EFFECTING),
    )(*operands)
    return list(outs[:n]), list(outs[n:])


def _share_with_sibling(shards, colwise, name):
    n = len(shards)

    def body(*refs):
        out = refs[n:2 * n]
        send_sems, recv_sems = refs[2 * n:]
        x, y, c, _ = _position()

        def copy(i, half):
            win = _window(out[i], colwise[i], None, half)
            return pltpu.make_async_remote_copy(
                src_ref=win, dst_ref=win, send_sem=send_sems.at[i], recv_sem=recv_sems.at[i],
                device_id=(x, y, 1 - c), device_id_type=MESH_ID)

        for i in range(n):
            copy(i, c).start()
        for i in range(n):
            copy(i, 1 - c).wait_recv()
        for i in range(n):
            copy(i, c).wait_send()

    return pl.pallas_call(
        body, name=name,
        in_specs=[ANY] * n, out_specs=[ANY] * n,
        out_shape=[jax.ShapeDtypeStruct(s.shape, s.dtype) for s in shards],
        input_output_aliases={i: i for i in range(n)},
        scratch_shapes=[pltpu.SemaphoreType.DMA((n,)), pltpu.SemaphoreType.DMA((n,))],
    )(*shards)


def _chip_exchange(buf, me, x, y, c, chips, send_sems, recv_sems):
    def copy(j, chip, slot):
        return pltpu.make_async_remote_copy(
            src_ref=buf.at[me], dst_ref=buf.at[slot], send_sem=send_sems.at[j], recv_sem=recv_sems.at[j],
            device_id=(chip[0], chip[1], c), device_id_type=MESH_ID)

    for j, chip in enumerate(chips):
        copy(j, chip, me).start()
    for j, chip in enumerate(chips):
        copy(j, chip, 2 * chip[0] + chip[1]).wait_recv()
    for j, chip in enumerate(chips):
        copy(j, chip, me).wait_send()


def _gather_over_chips(pack, name):
    R, Cc = pack.shape

    def body(p_ref, o_ref, send_sems, recv_sems):
        x, y, c, chips = _position()
        me = 2 * x + y
        o_ref[me] = p_ref[...]
        _chip_exchange(o_ref, me, x, y, c, chips, send_sems, recv_sems)

    vmem = pl.BlockSpec(memory_space=pltpu.VMEM)
    return pl.pallas_call(
        body, name=name,
        in_specs=[vmem], out_specs=vmem, out_shape=jax.ShapeDtypeStruct((4, R, Cc), F32),
        scratch_shapes=[pltpu.SemaphoreType.DMA((3,)), pltpu.SemaphoreType.DMA((3,))],
    )(pack)


def _all_reduce_small(pack, name):
    R, Cc = pack.shape

    def body(p_ref, o_ref, sibling, buf, send_sems, recv_sems):
        x, y, c, chips = _position()
        me = 2 * x + y
        swap = pltpu.make_async_remote_copy(
            src_ref=p_ref, dst_ref=sibling, send_sem=send_sems.at[3], recv_sem=recv_sems.at[3],
            device_id=(x, y, 1 - c), device_id_type=MESH_ID)
        swap.start()
        swap.wait()
        buf[me] = p_ref[...] + sibling[...]
        _chip_exchange(buf, me, x, y, c, chips, send_sems, recv_sems)
        o_ref[...] = (buf[0] + buf[1]) + (buf[2] + buf[3])

    vmem = pl.BlockSpec(memory_space=pltpu.VMEM)
    return pl.pallas_call(
        body, name=name,
        in_specs=[vmem], out_specs=vmem, out_shape=jax.ShapeDtypeStruct((R, Cc), F32),
        scratch_shapes=[pltpu.VMEM((R, Cc), F32), pltpu.VMEM((4, R, Cc), F32), pltpu.SemaphoreType.DMA((4,)),
                        pltpu.SemaphoreType.DMA((4,))],
    )(pack)


def _local_forward_backward(x2, target2, S, pass_on, fetch, reduce_begin, reduce_commit, layers, final_g, tm=512):
    T, D = x2.shape
    C = D // 2
    n_heads = C // GROUP
    n_layers = len(layers)
    weights = {}
    saved = []
    xc = x2
    for li, lw in enumerate(layers):
        if li == 0:
            pass_on(0, None)
            weights.update(fetch(0, None))
        h1, qkv3, cv3 = _norm_proj(xc, lw["norm1"], weights[li, "w_in"], 0, ((3, C, F32), (3, C, BF16)), tm,
                                   min(C, 512), f"l{li}_norm_in_proj")
        if li == 0:
            pass_on(1, h1)
        o, lse, mix = _attn_fwd(qkv3, lw["attn_g"], 2, S, n_heads, f"l{li}_attn_fwd")
        pin = None
        if li == 0:
            weights.update(fetch(1, o))
            pin = pass_on(3, pass_on(2, o))
        mix = _mix_conv_fwd(cv3, lw["taps"], lw["conv_g"], mix, S, f"l{li}_mix_conv_fwd", pin)
        x_mid = _proj_residual(mix, weights[li, "w_out"], 0, xc, tm, f"l{li}_out_proj")
        if li == 0:
            weights.update(fetch(2, x_mid))
        Fd = weights[li, "ffn_up"].shape[2] // 2
        h2, up3 = _norm_proj(x_mid, lw["norm2"], weights[li, "ffn_up"], 0, ((2, Fd, BF16),), tm, 256,
                             f"l{li}_norm_ffn_up")
        if li == 0:
            weights.update(fetch(3, up3))
        act = _ffn_act_fwd(up3, lw["ffn_taps"], S, f"l{li}_ffn_act_fwd")
        pin = pass_on(li + 4, act) if li + 1 < n_layers else None
        x_out = _proj_residual(act.reshape(1, T, Fd), weights[li, "ffn_down"], 0, x_mid, tm, f"l{li}_ffn_down", pin)
        if li + 1 < n_layers:
            weights.update(fetch(li + 4, x_out))
        saved.append(dict(x_in=xc, h1=h1, qkv3=qkv3, cv3=cv3, o=o, lse=lse, mix=mix, x_mid=x_mid, h2=h2, up3=up3,
                          act=act))
        xc = x_out

    dx, d_final_g, loss_part = _final_norm_loss(xc, final_g, target2, tm, "final_norm_loss")

    small = [None] * n_layers
    started = None
    for li in reversed(range(n_layers)):
        lw, sv = layers[li], saved[li]
        w_in, w_out, ffn_up, ffn_down = (weights[li, n] for n in ("w_in", "w_out", "ffn_up", "ffn_down"))
        dxb, dact3 = _grad_through_weight(dx, ffn_down, 0, 1, Fd, BF16, tm, 256, f"l{li}_d_act", started)
        Fd = ffn_down.shape[1]
        d_ffn_down = _weight_grad(sv["act"].reshape(1, T, Fd), dxb.reshape(1, T, D), Fd // 2, D,
                                  f"l{li}_d_ffn_down")
        dup3, d_taps_g, d_taps_v = _ffn_act_bwd(sv["up3"], dact3, lw["ffn_taps"], S, f"l{li}_ffn_act_bwd")
        d_ffn_up = _weight_grad(sv["h2"].reshape(1, D, T), dup3, D, Fd // 2, f"l{li}_d_ffn_up", a_transposed=True)
        if li == 0:
            early = reduce_begin({(li, "ffn_down"): d_ffn_down, (li, "ffn_up"): d_ffn_up})
        dx_mid, d_norm2 = _grad_through_proj_norm(dup3, ffn_up, 0, sv["x_mid"], lw["norm2"], dx, tm,
                                                  f"l{li}_d_norm2")
        started = reduce_commit(early, dx_mid) if li == 0 else None
        dxmb, dmix3 = _grad_through_weight(dx_mid, w_out, 0, 2, C, F32, tm, min(C, 512), f"l{li}_d_mix", started)
        d_w_out = _weight_grad(sv["mix"], dxmb.reshape(1, T, D), min(C, 256), D, f"l{li}_d_w_out")
        dproj, d_attn_g = _attn_bwd(sv["qkv3"], sv["o"], sv["lse"], dmix3, lw["attn_g"], 6, S, n_heads,
                                    f"l{li}_attn_bwd")
        dproj, d_taps, d_conv_g = _mix_conv_bwd(sv["cv3"], dmix3, lw["taps"], lw["conv_g"], dproj, S,
                                                f"l{li}_mix_conv_bwd")
        d_w_in = _weight_grad(sv["h1"].reshape(1, D, T), dproj, D, C, f"l{li}_d_w_in", a_transposed=True)
        late = {(li, "w_out"): d_w_out, (li, "w_in"): d_w_in}
        if li > 0:
            late.update({(li, "ffn_down"): d_ffn_down, (li, "ffn_up"): d_ffn_up})
        late = reduce_begin(late)
        dx, d_norm1 = _grad_through_proj_norm(dproj, w_in, 0, sv["x_in"], lw["norm1"], dx_mid, tm,
                                              f"l{li}_d_norm1")
        started = reduce_commit(late, dx)
        small[li] = dict(norm1=d_norm1, taps=d_taps, attn_g=d_attn_g, conv_g=d_conv_g, norm2=d_norm2,
                         ffn_taps=jnp.concatenate([d_taps_g, d_taps_v], axis=1))
    return loss_part, dx, small, d_final_g


SMALL_ORDER = ("norm1", "attn_g", "conv_g", "norm2", "taps", "ffn_taps")


def _pack_small(small, d_final_g, loss_row):
    parts = [small[li][k].reshape(-1) for li in range(len(small)) for k in SMALL_ORDER]
    loss_rows = jnp.tile(loss_row.reshape(1, LANES), (8, 1))
    return jnp.concatenate(parts + [d_final_g.reshape(-1), loss_rows.reshape(-1)]).reshape(-1, LANES)


def _unpack_small(pack, small, d_final_g):
    flat = pack.reshape(-1)
    out, pos = [dict() for _ in small], 0
    for li in range(len(small)):
        for k in SMALL_ORDER:
            n = small[li][k].size
            out[li][k] = flat[pos:pos + n].reshape(small[li][k].shape)
            pos += n
    return out, flat[pos:pos + d_final_g.size], flat[pos + d_final_g.size]


def kernel(x, norm1_g, w_in, mix_conv_w, attn_out_g, conv_out_g, w_out, norm2_g, ffn_up, ffn_conv_w, ffn_down, final_norm_g, loss_target, m_norm1_g, m_w_in, m_mix_conv_w, m_attn_out_g, m_conv_out_g, m_w_out, m_norm2_g, m_ffn_up, m_ffn_conv_w, m_ffn_down, m_final_norm_g, v_norm1_g, v_w_in, v_mix_conv_w, v_attn_out_g, v_conv_out_g, v_w_out, v_norm2_g, v_ffn_up, v_ffn_conv_w, v_ffn_down, v_final_norm_g):
    Bl, S, D = x.shape
    L = w_in.shape[0]
    T = Bl * S
    shard = 2 * lax.axis_index("x") + lax.axis_index("y")
    where = jnp.stack([shard, lax.axis_index("c")]).astype(jnp.int32)
    big_names = ("w_in", "w_out", "ffn_up", "ffn_down")

    taps_w, ftaps_w = mix_conv_w.shape[2], ffn_conv_w.shape[2]
    tap_pack = _gather_over_chips(
        jnp.concatenate([mix_conv_w.reshape(-1), ffn_conv_w.reshape(-1)]).reshape(-1, LANES), "all_gather_taps")
    by_chip = tap_pack.reshape(4, -1)
    n_taps = mix_conv_w.size
    taps_full = by_chip[:, :n_taps].reshape(4, L, 3, taps_w).transpose(1, 2, 0, 3).reshape(L, 3, 4 * taps_w)
    ftaps_full = by_chip[:, n_taps:].reshape(4, L, 3, ftaps_w).transpose(1, 2, 0, 3).reshape(L, 3, 4 * ftaps_w)

    big_shards = dict(zip(big_names, (w_in, w_out, ffn_up, ffn_down)))
    col_of = dict(zip(big_names, COL_SHARDED))
    groups = [[(0, n)] for n in big_names] + [[(l, n) for n in big_names] for l in range(1, L)]
    sems, in_flight = [], {}
    all_started = tap_pack
    for first, last in ((0, 1), (1, len(groups))):
        keys = [k for g in groups[first:last] for k in g]
        new_sems, arrays = _gather_start(
            [_cast_into_full(big_shards[n], l, col_of[n], where, f"cast_{n}_{l}") for l, n in keys],
            [col_of[n] for _, n in keys], [len(g) for g in groups[first:last]], all_started,
            f"gather_start_{first}")
        sems += new_sems
        in_flight.update(zip(keys, arrays))
        all_started = arrays[-1]

    def pass_on(g, after):
        after = all_started if g == 0 else after
        sems[g], arrays = _gather_pass([in_flight[k] for k in groups[g]], [col_of[n] for _, n in groups[g]],
                                       sems[g], after, f"gather_pass_{g}")
        in_flight.update(zip(groups[g], arrays))
        return arrays[0]

    def fetch(g, after):
        done = _gather_wait([in_flight[k] for k in groups[g]], [col_of[n] for _, n in groups[g]], sems[g], after,
                            f"gather_wait_{g}")
        return dict(zip(groups[g], done))

    pending = []

    begun = []

    def reduce_begin(grads):
        g = len(begun)
        keys = list(grads)
        cols = [col_of[n] for _, n in keys]
        begun.append((g, keys, cols) + _exchange_start([grads[k] for k in keys], cols, f"exchange_start_{g}"))
        return begun[-1]

    def reduce_commit(handle, after):
        g, keys, cols, ex_sems, mine, lands = handle
        mine, others = _exchange_wait(mine, lands, cols, ex_sems, after, f"exchange_wait_{g}")
        chip_sums = [_chip_sum(m, o, cw, where, f"chip_sum_{k[1]}_{k[0]}")
                     for k, m, o, cw in zip(keys, mine, others, cols)]
        pending.append((keys, cols) + _scatter_start(chip_sums, cols, f"scatter_start_{g}"))
        return pending[-1][3][0]

    layers = [dict(norm1=norm1_g[l:l + 1], taps=taps_full[l], attn_g=attn_out_g[l:l + 1],
                   conv_g=conv_out_g[l:l + 1], norm2=norm2_g[l:l + 1], ffn_taps=ftaps_full[l]) for l in range(L)]

    loss_part, dx, small, d_final_g = _local_forward_backward(
        x.reshape(T, D), loss_target.reshape(T, D), S, pass_on, fetch, reduce_begin, reduce_commit, layers,
        final_norm_g.reshape(1, D))

    def finish_group(g, after):
        keys, cols, rs_sems, sources, lands = pending[g]
        sources, lands = _scatter_wait(sources, lands, cols, rs_sems, after, f"scatter_wait_{g}")
        for (l, n), cw, src, land in zip(keys, cols, sources, lands):
            reduced[n] = _owner_sum(src, land, cw, where, l, L, reduced[n], f"owner_sum_{n}_{l}")

    reduced = dict.fromkeys(big_names)
    last_started = pending[-1][3][0]
    for g in range(len(pending) - 1):
        finish_group(g, last_started)
    late_names = [n for n in big_names if any(n == name for _, name in pending[-1][0])]
    early_names = [n for n in big_names if n not in late_names]
    g_big = dict(zip(early_names, _share_with_sibling([reduced[n] for n in early_names],
                                                      [col_of[n] for n in early_names], "grad_share_early")))

    pack = _all_reduce_small(_pack_small(small, d_final_g, loss_part), "all_reduce_small_grads")
    g_small, g_final, loss = _unpack_small(pack, small, d_final_g)

    def stacked(key):
        return jnp.stack([g_small[l][key].reshape(g_small[l][key].shape[-2:] if key.endswith("taps") else (-1,))
                          for l in range(L)])

    g_norm1, g_attn, g_conv, g_norm2 = stacked("norm1"), stacked("attn_g"), stacked("conv_g"), stacked("norm2")
    g_taps = lax.dynamic_slice(stacked("taps"), (0, 0, shard * taps_w), (L, 3, taps_w))
    g_ftaps = lax.dynamic_slice(stacked("ffn_taps"), (0, 0, shard * ftaps_w), (L, 3, ftaps_w))

    grads_out = dict(norm1_g=g_norm1, w_in=None, mix_conv_w=g_taps, attn_out_g=g_attn, conv_out_g=g_conv,
                     w_out=None, norm2_g=g_norm2, ffn_up=None, ffn_conv_w=g_ftaps, ffn_down=None,
                     final_norm_g=g_final)
    weights = dict(norm1_g=norm1_g, w_in=w_in, mix_conv_w=mix_conv_w, attn_out_g=attn_out_g, conv_out_g=conv_out_g,
                   w_out=w_out, norm2_g=norm2_g, ffn_up=ffn_up, ffn_conv_w=ffn_conv_w, ffn_down=ffn_down,
                   final_norm_g=final_norm_g)
    ms = dict(norm1_g=m_norm1_g, w_in=m_w_in, mix_conv_w=m_mix_conv_w, attn_out_g=m_attn_out_g,
              conv_out_g=m_conv_out_g, w_out=m_w_out, norm2_g=m_norm2_g, ffn_up=m_ffn_up, ffn_conv_w=m_ffn_conv_w,
              ffn_down=m_ffn_down, final_norm_g=m_final_norm_g)
    vs = dict(norm1_g=v_norm1_g, w_in=v_w_in, mix_conv_w=v_mix_conv_w, attn_out_g=v_attn_out_g,
              conv_out_g=v_conv_out_g, w_out=v_w_out, norm2_g=v_norm2_g, ffn_up=v_ffn_up, ffn_conv_w=v_ffn_conv_w,
              ffn_down=v_ffn_down, final_norm_g=v_final_norm_g)
    names = list(weights)
    small_names = [n for n in names if n not in big_names]
    delta, new_m, new_v = {}, {}, {}

    def update_big(n):
        shp = weights[n].shape
        two_d = (shp[0] * shp[1], shp[2])
        d_, m_, v_, g_ = _adamw(weights[n].reshape(two_d), g_big[n].reshape(two_d), ms[n].reshape(two_d),
                                vs[n].reshape(two_d), f"adamw_{n}")
        delta[n], new_m[n], new_v[n], grads_out[n] = (a.reshape(shp) for a in (d_, m_, v_, g_))

    for n in early_names:
        update_big(n)
    finish_group(len(pending) - 1, delta[early_names[-1]] if early_names else None)
    g_big.update(zip(late_names, _share_with_sibling([reduced[n] for n in late_names],
                                                     [col_of[n] for n in late_names], "grad_share_late")))
    for n in late_names:
        update_big(n)

    def packed(tree):
        return jnp.concatenate([tree[n].reshape(-1) for n in small_names]).reshape(-1, LANES)

    d_, m_, v_, _ = _adamw(packed(weights), packed(grads_out), packed(ms), packed(vs), "adamw_small")
    pos = 0
    for n in small_names:
        size, shp = weights[n].size, weights[n].shape
        delta[n] = d_.reshape(-1)[pos:pos + size].reshape(shp)
        new_m[n] = m_.reshape(-1)[pos:pos + size].reshape(shp)
        new_v[n] = v_.reshape(-1)[pos:pos + size].reshape(shp)
        pos += size

    return (loss, dx.reshape(Bl, S, D), *[grads_out[n] for n in names], *[delta[n] for n in names],
            *[new_m[n] for n in names], *[new_v[n] for n in names])
```

```python
import functools
import math

import jax
import jax.numpy as jnp
from jax import lax
from jax.experimental import pallas as pl
from jax.experimental.pallas import tpu as pltpu

F32 = jnp.float32
BF16 = jnp.bfloat16
EPS = 1e-6
GROUP = 64
LANES = 128
BAND = 128
DILATIONS = (1, 4, 16)
NEG = -1e30
MIB = 1024 * 1024
MESH_ID = pl.DeviceIdType.MESH

ADAM_LR = 0.001
ADAM_B1 = 0.9
ADAM_B2 = 0.999
ADAM_EPS = 1e-08
ADAM_WD = 0.01
ADAM_STEP = 10


ANY = pl.BlockSpec(memory_space=pl.ANY)


def _in_hbm(x):
    return pltpu.with_memory_space_constraint(x, pltpu.HBM)


def _params(sem=None, vmem_mb=48):
    return pltpu.CompilerParams(dimension_semantics=sem, vmem_limit_bytes=vmem_mb * MIB)


def _nt(a, b):
    return lax.dot_general(a, b, (((1,), (1,)), ((), ())), preferred_element_type=F32)


def _tn(a, b):
    return lax.dot_general(a, b, (((0,), (0,)), ((), ())), preferred_element_type=F32)


def _seg_sum(x, is_a):
    s_a = jnp.sum(jnp.where(is_a, x, 0.0), axis=-1, keepdims=True)
    s_b = jnp.sum(jnp.where(is_a, 0.0, x), axis=-1, keepdims=True)
    return jnp.where(is_a, s_a, s_b)


def _lane_is_a():
    return lax.broadcasted_iota(jnp.int32, (1, LANES), 1) < GROUP


def _norm_proj(x, g, w3, layer, groups, tm, chunk, name):
    T, D = x.shape
    N = w3.shape[2]
    assert sum(p * c for p, c, _ in groups) == N and T % tm == 0

    def body(x_ref, g_ref, w_ref, h_ref, *out_refs):
        xv = x_ref[...]
        rstd = lax.rsqrt(jnp.mean(xv * xv, axis=-1, keepdims=True) + EPS)
        h = ((xv * rstd) * g_ref[...]).astype(BF16)
        h_ref[...] = h.T
        col = 0
        for (pieces, width, dtype), o_ref in zip(groups, out_refs):
            for p in range(pieces):
                for c0 in range(0, width, chunk):
                    acc = jnp.dot(h, w_ref[:, col + c0:col + c0 + chunk], preferred_element_type=F32)
                    o_ref[p, :, c0:c0 + chunk] = acc.astype(dtype)
                col += width

    out_shape = [jax.ShapeDtypeStruct((D, T), BF16)]
    out_specs = [pl.BlockSpec((D, tm), lambda i: (0, i))]
    for pieces, width, dtype in groups:
        assert width % chunk == 0
        out_shape.append(jax.ShapeDtypeStruct((pieces, T, width), dtype))
        out_specs.append(pl.BlockSpec((pieces, tm, width), lambda i: (0, i, 0)))
    return pl.pallas_call(
        body, grid=(T // tm,), name=name,
        in_specs=[pl.BlockSpec((tm, D), lambda i: (i, 0)),
                  pl.BlockSpec((1, D), lambda i: (0, 0)),
                  pl.BlockSpec((None, D, N), lambda i: (layer, 0, 0))],
        out_specs=out_specs, out_shape=out_shape,
        compiler_params=_params(("parallel",), 56),
    )(x, g, w3)


def _proj_residual(pieces3, w3, layer, x, tm, name, after=None):
    P, T, C = pieces3.shape
    D = w3.shape[2]

    def body(a_ref, w_ref, x_ref, *rest):
        o_ref = rest[-1]
        acc = x_ref[...]
        for p in range(P):
            acc = acc + jnp.dot(a_ref[p], w_ref[p * C:(p + 1) * C, :], preferred_element_type=F32)
        o_ref[...] = acc

    in_specs = [pl.BlockSpec((P, tm, C), lambda i: (0, i, 0)),
                pl.BlockSpec((None, P * C, D), lambda i: (layer, 0, 0)),
                pl.BlockSpec((tm, D), lambda i: (i, 0))]
    operands = [pieces3, w3, x]
    if after is not None:
        in_specs.append(ANY)
        operands.append(after)
    return pl.pallas_call(
        body, grid=(T // tm,), name=name,
        in_specs=in_specs,
        out_specs=pl.BlockSpec((tm, D), lambda i: (i, 0)),
        out_shape=jax.ShapeDtypeStruct((T, D), F32),
        compiler_params=_params(("parallel",)),
    )(*operands)


def _grad_through_weight(dy, w3, layer, pieces, width, out_dtype, tm, chunk, name, after=None):
    T, D = dy.shape

    def body(dy_ref, w_ref, *rest):
        dyb_ref, o_ref = rest[-2:]
        dyb = dy_ref[...].astype(BF16)
        dyb_ref[...] = dyb
        for p in range(pieces):
            for c0 in range(0, width, chunk):
                r0 = p * width + c0
                o_ref[p, :, c0:c0 + chunk] = _nt(dyb, w_ref[r0:r0 + chunk, :]).astype(out_dtype)

    in_specs = [pl.BlockSpec((tm, D), lambda i: (i, 0)),
                pl.BlockSpec((None, pieces * width, D), lambda i: (layer, 0, 0))]
    operands = [dy, w3]
    if after is not None:
        in_specs.append(ANY)
        operands.append(after)
    return pl.pallas_call(
        body, grid=(T // tm,), name=name,
        in_specs=in_specs,
        out_specs=[pl.BlockSpec((tm, D), lambda i: (i, 0)),
                   pl.BlockSpec((pieces, tm, width), lambda i: (0, i, 0))],
        out_shape=[jax.ShapeDtypeStruct((T, D), BF16),
                   jax.ShapeDtypeStruct((pieces, T, width), out_dtype)],
        compiler_params=_params(("parallel",)),
    )(*operands)


def _grad_through_proj_norm(dp3, w3, layer, x, g, dx_in, tm, name):
    P, T, C = dp3.shape
    D = w3.shape[1]

    def body(dp_ref, w_ref, x_ref, g_ref, dxin_ref, dx_ref, dg_ref):
        dh = _nt(dp_ref[0], w_ref[:, 0:C])
        for p in range(1, P):
            dh = dh + _nt(dp_ref[p], w_ref[:, p * C:(p + 1) * C])
        xv = x_ref[...]
        rstd = lax.rsqrt(jnp.mean(xv * xv, axis=-1, keepdims=True) + EPS)
        xn = xv * rstd
        a = dh * g_ref[...]
        dx_ref[...] = dxin_ref[...] + rstd * (a - xn * jnp.mean(a * xn, axis=-1, keepdims=True))
        part = jnp.sum(dh * xn, axis=0, keepdims=True)

        @pl.when(pl.program_id(0) == 0)
        def _():
            dg_ref[...] = part

        @pl.when(pl.program_id(0) != 0)
        def _():
            dg_ref[...] += part

    return pl.pallas_call(
        body, grid=(T // tm,), name=name,
        in_specs=[pl.BlockSpec((P, tm, C), lambda i: (0, i, 0)),
                  pl.BlockSpec((None, D, P * C), lambda i: (layer, 0, 0)),
                  pl.BlockSpec((tm, D), lambda i: (i, 0)),
                  pl.BlockSpec((1, D), lambda i: (0, 0)),
                  pl.BlockSpec((tm, D), lambda i: (i, 0))],
        out_specs=[pl.BlockSpec((tm, D), lambda i: (i, 0)),
                   pl.BlockSpec((1, D), lambda i: (0, 0))],
        out_shape=[jax.ShapeDtypeStruct((T, D), F32), jax.ShapeDtypeStruct((1, D), F32)],
        compiler_params=_params(("arbitrary",), 56),
    )(dp3, w3, x, g, dx_in)


def _weight_grad(a3, g3, ta, tg, name, a_transposed=False):
    PG, T, CG = g3.shape
    PA, CA = (a3.shape[0], a3.shape[1]) if a_transposed else (a3.shape[0], a3.shape[2])
    na, ng = CA // ta, CG // tg
    assert CA % ta == 0 and CG % tg == 0

    def body(a_ref, g_ref, o_ref):
        if a_transposed:
            part = jnp.dot(a_ref[...], g_ref[...], preferred_element_type=F32)
        else:
            part = _tn(a_ref[...], g_ref[...])
        o_ref[...] = part.astype(o_ref.dtype)

    a_spec = (pl.BlockSpec((None, ta, T), lambda i, j: (i // na, i % na, 0)) if a_transposed
              else pl.BlockSpec((None, T, ta), lambda i, j: (i // na, 0, i % na)))
    return pl.pallas_call(
        body, grid=(PA * na, PG * ng), name=name,
        in_specs=[a_spec, pl.BlockSpec((None, T, tg), lambda i, j: (j // ng, 0, j % ng))],
        out_specs=pl.BlockSpec((None, ta, tg), lambda i, j: (0, i, j)),
        out_shape=pltpu.HBM((1, PA * CA, PG * CG), BF16),
        compiler_params=_params(("parallel", "parallel"), 56),
    )(a3, g3)


def _bias_tables(bm_ref, lone_ref, pair, n_heads, S):
    ii = lax.broadcasted_iota(jnp.int32, (BAND, 2 * BAND), 0)
    jj = lax.broadcasted_iota(jnp.int32, (BAND, 2 * BAND), 1)
    dist = BAND + ii - jj
    valid = (dist >= 0) & (dist <= BAND)
    distf = dist.astype(F32)
    for hh in range(2):
        head = (2 * pair + hh + 1).astype(F32)
        slope = jnp.exp(jnp.full((1, 1), -8.0 / n_heads * math.log(2.0), F32) * head)
        for bi, d in enumerate(DILATIONS):
            table = jnp.where(valid, -(slope * d) * distf, NEG)
            bm_ref[bi, hh * BAND:(hh + 1) * BAND, :] = table
            if S // (BAND * d) == 1:
                lone_ref[bi, hh * BAND:(hh + 1) * BAND, :] = table[:, BAND:2 * BAND]


def _stack_heads(x, is_a):
    zero = jnp.zeros_like(x)
    return jnp.concatenate([jnp.where(is_a, x, zero), jnp.where(is_a, zero, x)], axis=0)


def _unstack_heads(x2, is_a):
    return jnp.where(is_a, x2[0:BAND], x2[BAND:2 * BAND])


def _gather_residues(dst_ref, src, d, S, convert):
    L = S // d
    for r in range(d):
        rows = pl.ds(r, L, stride=d) if d > 1 else slice(None)
        dst_ref[r * L:(r + 1) * L, :] = convert(src(rows))


def _block_rows(t, d, S):
    nb = S // (BAND * d)
    n = t % nb
    has_prev = jnp.minimum(n, 1)
    cur = pl.ds(pl.multiple_of(t * BAND, BAND), BAND)
    prev = pl.ds(pl.multiple_of((t - has_prev) * BAND, BAND), BAND)
    return cur, prev, has_prev


def _first_block_penalty(has_prev):
    jrow = lax.broadcasted_iota(jnp.int32, (1, 2 * BAND), 1)
    pen = jnp.where(has_prev == 0, NEG, 0.0).astype(F32)
    return jnp.where(jrow < BAND, pen, 0.0)


def _attn_fwd(qkv3, gain, mix_shape_pieces, S, n_heads, name):
    _, T, C = qkv3.shape
    B, P = T // S, C // LANES
    NBLK = S // BAND
    scale = GROUP ** -0.5
    nbr = len(DILATIONS)
    RC = 256

    def body(qkv_ref, g_ref, o_ref, lse_ref, an_ref, qs, ks, vs, op, mp, lp, ob, mb, lb, bm, bml):
        pair = pl.program_id(1)
        is_a = _lane_is_a()
        _bias_tables(bm, bml, pair, n_heads, S)

        for bi, d in enumerate(DILATIONS):
            nb = S // (BAND * d)
            _gather_residues(qs, lambda rows: qkv_ref.at[0][rows, :], d, S, lambda v: (v * scale).astype(BF16))
            _gather_residues(ks, lambda rows: qkv_ref.at[1][rows, :], d, S, lambda v: v.astype(BF16))
            _gather_residues(vs, lambda rows: qkv_ref.at[2][rows, :], d, S, lambda v: v.astype(BF16))
            o_dst, m_dst, l_dst = (ob.at[bi], mb.at[bi], lb.at[bi]) if d == 1 else (op, mp, lp)

            def block(t, carry, bi=bi, d=d, nb=nb, o_dst=o_dst, m_dst=m_dst, l_dst=l_dst):
                cur, prev, has_prev = _block_rows(t, d, S)
                q2 = _stack_heads(qs[cur, :], is_a)
                if nb == 1:
                    kc, vc = ks[cur, :], vs[cur, :]
                    s = _nt(q2, kc) + bml[bi]
                else:
                    kc = jnp.concatenate([ks[prev, :], ks[cur, :]], axis=0)
                    vc = jnp.concatenate([vs[prev, :], vs[cur, :]], axis=0)
                    s = _nt(q2, kc) + bm[bi] + _first_block_penalty(has_prev)
                m = jnp.max(s, axis=-1, keepdims=True)
                e = jnp.exp(s - m)
                l = jnp.sum(e, axis=-1, keepdims=True)
                pv = jnp.dot(e.astype(BF16), vc, preferred_element_type=F32)
                o_dst[cur, :] = _unstack_heads(pv, is_a)
                m_dst[cur, :] = _unstack_heads(m, is_a)
                l_dst[cur, :] = _unstack_heads(l, is_a)
                return carry

            lax.fori_loop(0, NBLK, block, 0, unroll=8)
            if d > 1:
                L = S // d
                for r in range(d):
                    rows = pl.ds(r, L, stride=d)
                    ob.at[bi][rows, :] = op[r * L:(r + 1) * L, :]
                    mb.at[bi][rows, :] = mp[r * L:(r + 1) * L, :]
                    lb.at[bi][rows, :] = lp[r * L:(r + 1) * L, :]

        def finish(ci, carry):
            rs = pl.ds(pl.multiple_of(ci * RC, RC), RC)
            ms = [mb[bi, rs, :] for bi in range(nbr)]
            mmax = functools.reduce(jnp.maximum, ms)
            ws = [jnp.exp(m - mmax) for m in ms]
            num = sum(ob[bi, rs, :] * ws[bi] for bi in range(nbr))
            den = sum(lb[bi, rs, :] * ws[bi] for bi in range(nbr))
            o = num / den
            o_ref[rs, :] = o
            lse_ref[rs, :] = mmax + jnp.log(den)
            rstd = lax.rsqrt(_seg_sum(o * o, is_a) * (1.0 / GROUP) + EPS)
            an_ref[rs, :] = ((o * rstd) * g_ref[...]).astype(BF16)
            return carry

        lax.fori_loop(0, S // RC, finish, 0)

    seq = pl.BlockSpec((S, LANES), lambda b, p: (b, p))
    return pl.pallas_call(
        body, grid=(B, P), name=name,
        in_specs=[pl.BlockSpec((3, S, LANES), lambda b, p: (0, b, p)),
                  pl.BlockSpec((1, LANES), lambda b, p: (0, p))],
        out_specs=[seq, seq, pl.BlockSpec((None, S, LANES), lambda b, p: (0, b, p))],
        out_shape=[jax.ShapeDtypeStruct((T, C), F32), jax.ShapeDtypeStruct((T, C), F32),
                   jax.ShapeDtypeStruct((mix_shape_pieces, T, C), BF16)],
        scratch_shapes=[pltpu.VMEM((S, LANES), BF16)] * 3 + [pltpu.VMEM((S, LANES), F32)] * 3
        + [pltpu.VMEM((nbr, S, LANES), F32)] * 3
        + [pltpu.VMEM((nbr, 2 * BAND, 2 * BAND), F32), pltpu.VMEM((nbr, 2 * BAND, BAND), F32)],
        compiler_params=_params(("parallel", "parallel")),
    )(qkv3, gain)


def _attn_bwd(qkv3, o, lse, dmix3, gain, dproj_pieces, S, n_heads, name):
    _, T, C = qkv3.shape
    B, P = T // S, C // LANES
    NBLK = S // BAND
    scale = GROUP ** -0.5
    nbr = len(DILATIONS)
    RC = 256

    def body(qkv_ref, o_ref, lse_ref, dn_ref, g_ref, dqkv_ref, dg_ref,
             do_n, dd_n, qs, ks, vs, dos, lses, dds, dqp, dkp, dvp, dqn, dkn, dvn, bm, bml):
        pair = pl.program_id(0)
        b = pl.program_id(1)
        is_a = _lane_is_a()
        _bias_tables(bm, bml, pair, n_heads, S)

        def prologue(ci, dg_acc):
            rs = pl.ds(pl.multiple_of(ci * RC, RC), RC)
            ov = o_ref[rs, :]
            dn = dn_ref[rs, :]
            rstd = lax.rsqrt(_seg_sum(ov * ov, is_a) * (1.0 / GROUP) + EPS)
            on = ov * rstd
            a = dn * g_ref[...]
            s_a = _seg_sum(a * on, is_a)
            do_n[rs, :] = rstd * (a - on * (s_a * (1.0 / GROUP)))
            dd_n[rs, :] = (EPS * s_a) * (rstd * rstd)
            zero = jnp.zeros((RC, LANES), F32)
            dqn[rs, :] = zero
            dkn[rs, :] = zero
            dvn[rs, :] = zero
            return dg_acc + jnp.sum(dn * on, axis=0, keepdims=True)

        dg_part = lax.fori_loop(0, S // RC, prologue, jnp.zeros((1, LANES), F32))

        @pl.when(b == 0)
        def _():
            dg_ref[...] = dg_part

        @pl.when(b != 0)
        def _():
            dg_ref[...] += dg_part

        for bi, d in enumerate(DILATIONS):
            nb = S // (BAND * d)
            L = S // d
            _gather_residues(qs, lambda rows: qkv_ref.at[0][rows, :], d, S, lambda v: (v * scale).astype(BF16))
            _gather_residues(ks, lambda rows: qkv_ref.at[1][rows, :], d, S, lambda v: v.astype(BF16))
            _gather_residues(vs, lambda rows: qkv_ref.at[2][rows, :], d, S, lambda v: v.astype(BF16))
            _gather_residues(dos, lambda rows: do_n[rows, :], d, S, lambda v: v.astype(BF16))
            if d == 1:
                lse_src, dd_src, dq_dst, dk_dst, dv_dst = lse_ref, dd_n, dqn, dkn, dvn
            else:
                _gather_residues(lses, lambda rows: lse_ref[rows, :], d, S, lambda v: v)
                _gather_residues(dds, lambda rows: dd_n[rows, :], d, S, lambda v: v)
                dkp[...] = jnp.zeros((S, LANES), F32)
                dvp[...] = jnp.zeros((S, LANES), F32)
                lse_src, dd_src, dq_dst, dk_dst, dv_dst = lses, dds, dqp, dkp, dvp

            def block(t, carry, bi=bi, d=d, nb=nb, lse_src=lse_src, dd_src=dd_src, dq_dst=dq_dst, dk_dst=dk_dst,
                      dv_dst=dv_dst):
                cur, prev, has_prev = _block_rows(t, d, S)
                q2 = _stack_heads(qs[cur, :], is_a)
                do2 = _stack_heads(dos[cur, :], is_a)
                lse_t = lse_src[cur, :]
                dd_t = dd_src[cur, :]
                lse2 = jnp.concatenate([lse_t[:, 0:1], lse_t[:, GROUP:GROUP + 1]], axis=0)
                dd2 = jnp.concatenate([dd_t[:, 0:1], dd_t[:, GROUP:GROUP + 1]], axis=0)
                if nb == 1:
                    kc, vc = ks[cur, :], vs[cur, :]
                    s = _nt(q2, kc) + bml[bi]
                else:
                    kc = jnp.concatenate([ks[prev, :], ks[cur, :]], axis=0)
                    vc = jnp.concatenate([vs[prev, :], vs[cur, :]], axis=0)
                    s = _nt(q2, kc) + bm[bi] + _first_block_penalty(has_prev)
                p = jnp.exp(s - lse2)
                ds = (p * (_nt(do2, vc) - dd2)).astype(BF16)
                dq = _unstack_heads(jnp.dot(ds, kc, preferred_element_type=F32), is_a)
                dk = _tn(ds, q2)
                dv = _tn(p.astype(BF16), do2)
                dq_dst[cur, :] = dq
                if nb == 1:
                    dk_dst[cur, :] += dk
                    dv_dst[cur, :] += dv
                else:
                    dk_dst[prev, :] += dk[0:BAND, :]
                    dv_dst[prev, :] += dv[0:BAND, :]
                    dk_dst[cur, :] += dk[BAND:2 * BAND, :]
                    dv_dst[cur, :] += dv[BAND:2 * BAND, :]
                return carry

            lax.fori_loop(0, NBLK, block, 0, unroll=8)
            if d > 1:
                for r in range(d):
                    rows = pl.ds(r, L, stride=d)
                    dqn[rows, :] += dqp[r * L:(r + 1) * L, :]
                    dkn[rows, :] += dkp[r * L:(r + 1) * L, :]
                    dvn[rows, :] += dvp[r * L:(r + 1) * L, :]

        dqkv_ref[0] = (dqn[...] * scale).astype(BF16)
        dqkv_ref[1] = dkn[...].astype(BF16)
        dqkv_ref[2] = dvn[...].astype(BF16)

    seq = pl.BlockSpec((S, LANES), lambda p, b: (b, p))
    f32_seq = pltpu.VMEM((S, LANES), F32)
    bf_seq = pltpu.VMEM((S, LANES), BF16)
    return pl.pallas_call(
        body, grid=(P, B), name=name,
        in_specs=[pl.BlockSpec((3, S, LANES), lambda p, b: (0, b, p)), seq, seq,
                  pl.BlockSpec((None, S, LANES), lambda p, b: (0, b, p)),
                  pl.BlockSpec((1, LANES), lambda p, b: (0, p))],
        out_specs=[pl.BlockSpec((3, S, LANES), lambda p, b: (0, b, p)),
                   pl.BlockSpec((1, LANES), lambda p, b: (0, p))],
        out_shape=[jax.ShapeDtypeStruct((dproj_pieces, T, C), BF16), jax.ShapeDtypeStruct((1, C), F32)],
        scratch_shapes=[f32_seq, f32_seq, bf_seq, bf_seq, bf_seq, bf_seq, f32_seq, f32_seq,
                        f32_seq, f32_seq, f32_seq, f32_seq, f32_seq, f32_seq,
                        pltpu.VMEM((nbr, 2 * BAND, 2 * BAND), F32), pltpu.VMEM((nbr, 2 * BAND, BAND), F32)],
        compiler_params=_params(("parallel", "arbitrary")),
    )(qkv3, o, lse, dmix3, gain)


def _delay(x, k, row):
    return jnp.where(row >= k, pltpu.roll(x, k, 0), 0.0)


def _advance(x, k, row, S):
    return jnp.where(row < S - k, pltpu.roll(x, S - k, 0), 0.0)


def _conv3(x, w, row):
    return (w[0:1, :] * _delay(x, 2, row) + w[1:2, :] * _delay(x, 1, row)) + w[2:3, :] * x


HALO = 8


CONV_ROWS = 128
FFN_LANES = 128


def _zero_halo(pad_ref, S):
    zeros = jnp.zeros((HALO, pad_ref.shape[1]), pad_ref.dtype)
    pad_ref[0:HALO, :] = zeros
    pad_ref[HALO + S:2 * HALO + S, :] = zeros


def _window_at(pad_ref, r0, shift):
    return pad_ref[HALO + r0 + shift:HALO + r0 + shift + CONV_ROWS, :]


def _conv3_at(pad_ref, w, r0):
    return ((w[0:1, :] * _window_at(pad_ref, r0, -2) + w[1:2, :] * _window_at(pad_ref, r0, -1))
            + w[2:3, :] * _window_at(pad_ref, r0, 0))


def _conv3_grads_at(dz_ref, x_ref, w, r0):
    dz, dz1, dz2 = (_window_at(dz_ref, r0, k) for k in range(3))
    x = _window_at(x_ref, r0, 0)
    dx = (w[2:3, :] * dz + w[1:2, :] * dz1) + w[0:1, :] * dz2
    parts = [jnp.sum((d * x).reshape(CONV_ROWS // 8, 8, x.shape[1]), axis=0) for d in (dz2, dz1, dz)]
    return dx, parts


def _conv3_grads(dz, x, w, row, S):
    dz1 = _advance(dz, 1, row, S)
    dz2 = _advance(dz, 2, row, S)
    dx = (w[2:3, :] * dz + w[1:2, :] * dz1) + w[0:1, :] * dz2
    dw = jnp.concatenate([jnp.sum(dz2 * x, axis=0, keepdims=True),
                          jnp.sum(dz1 * x, axis=0, keepdims=True),
                          jnp.sum(dz * x, axis=0, keepdims=True)], axis=0)
    return dx, dw


def _mix_conv_fwd(cv3, taps, gain, mix, S, name, after=None):
    _, T, C = cv3.shape
    B, P = T // S, C // LANES

    def body(cv_ref, w_ref, g_ref, mix_hbm, *rest):
        y_ref, pad_c = rest[-2:]
        del mix_hbm
        is_a = _lane_is_a()
        _zero_halo(pad_c, S)
        pad_c[HALO:HALO + S, :] = cv_ref[1].astype(F32) * cv_ref[2].astype(F32)
        w = w_ref[...]
        for r0 in range(0, S, CONV_ROWS):
            y = cv_ref[0, r0:r0 + CONV_ROWS, :].astype(F32) * _conv3_at(pad_c, w, r0)
            rstd = lax.rsqrt(_seg_sum(y * y, is_a) * (1.0 / GROUP) + EPS)
            y_ref[r0:r0 + CONV_ROWS, :] = ((y * rstd) * g_ref[...]).astype(BF16)

    in_specs = [pl.BlockSpec((3, S, LANES), lambda b, p: (0, b, p)),
                pl.BlockSpec((3, LANES), lambda b, p: (0, p)),
                pl.BlockSpec((1, LANES), lambda b, p: (0, p)),
                ANY]
    operands = [cv3, taps, gain, mix]
    if after is not None:
        in_specs.append(ANY)
        operands.append(after)
    return pl.pallas_call(
        body, grid=(B, P), name=name,
        in_specs=in_specs,
        out_specs=pl.BlockSpec((None, S, LANES), lambda b, p: (1, b, p)),
        out_shape=jax.ShapeDtypeStruct(mix.shape, mix.dtype),
        scratch_shapes=[pltpu.VMEM((S + 2 * HALO, LANES), F32)],
        input_output_aliases={3: 0},
        compiler_params=_params(("parallel", "parallel")),
    )(*operands)


def _mix_conv_bwd(cv3, dmix3, taps, gain, dproj, S, name):
    _, T, C = cv3.shape
    B, P = T // S, C // LANES

    def body(cv_ref, dn_ref, w_ref, g_ref, dproj_hbm, dcv_ref, dw_ref, dg_ref):
        del dproj_hbm
        b = pl.program_id(1)
        row = lax.broadcasted_iota(jnp.int32, (S, 1), 0)
        is_a = _lane_is_a()
        w = w_ref[...]
        gb = cv_ref[0].astype(F32)
        gc = cv_ref[1].astype(F32)
        u = cv_ref[2].astype(F32)
        c = gc * u
        z = _conv3(c, w, row)
        y = gb * z
        rstd = lax.rsqrt(_seg_sum(y * y, is_a) * (1.0 / GROUP) + EPS)
        yn = y * rstd
        dn = dn_ref[...]
        a = dn * g_ref[...]
        dy = rstd * (a - yn * (_seg_sum(a * yn, is_a) * (1.0 / GROUP)))
        dg = jnp.sum(dn * yn, axis=0, keepdims=True)
        dc, dw = _conv3_grads(dy * gb, c, w, row, S)
        dcv_ref[0] = (dy * z).astype(BF16)
        dcv_ref[1] = (dc * u).astype(BF16)
        dcv_ref[2] = (dc * gc).astype(BF16)

        @pl.when(b == 0)
        def _():
            dw_ref[...] = dw
            dg_ref[...] = dg

        @pl.when(b != 0)
        def _():
            dw_ref[...] += dw
            dg_ref[...] += dg

    return pl.pallas_call(
        body, grid=(P, B), name=name,
        in_specs=[pl.BlockSpec((3, S, LANES), lambda p, b: (0, b, p)),
                  pl.BlockSpec((None, S, LANES), lambda p, b: (1, b, p)),
                  pl.BlockSpec((3, LANES), lambda p, b: (0, p)),
                  pl.BlockSpec((1, LANES), lambda p, b: (0, p)),
                  pl.BlockSpec(memory_space=pl.ANY)],
        out_specs=[pl.BlockSpec((3, S, LANES), lambda p, b: (1, b, p)),
                   pl.BlockSpec((3, LANES), lambda p, b: (0, p)),
                   pl.BlockSpec((1, LANES), lambda p, b: (0, p))],
        out_shape=[jax.ShapeDtypeStruct(dproj.shape, dproj.dtype),
                   jax.ShapeDtypeStruct((3, C), F32), jax.ShapeDtypeStruct((1, C), F32)],
        input_output_aliases={4: 0},
        compiler_params=_params(("parallel", "arbitrary")),
    )(cv3, dmix3, taps, gain, dproj)


def _sigmoid(x):
    return 0.5 * jnp.tanh(0.5 * x) + 0.5


def _ffn_act_fwd(up3, taps, S, name):
    _, T, Fd = up3.shape
    W = FFN_LANES
    B, P = T // S, Fd // W

    def body(up_ref, wg_ref, wv_ref, act_ref, pad_g, pad_v):
        _zero_halo(pad_g, S)
        _zero_halo(pad_v, S)
        pad_g[HALO:HALO + S, :] = up_ref[0].astype(F32)
        pad_v[HALO:HALO + S, :] = up_ref[1].astype(F32)
        wg = wg_ref[...]
        wv = wv_ref[...]
        for r0 in range(0, S, CONV_ROWS):
            cg = _conv3_at(pad_g, wg, r0)
            cv = _conv3_at(pad_v, wv, r0)
            act_ref[r0:r0 + CONV_ROWS, :] = ((cg * _sigmoid(cg)) * cv).astype(BF16)

    return pl.pallas_call(
        body, grid=(B, P), name=name,
        in_specs=[pl.BlockSpec((2, S, W), lambda b, p: (0, b, p)),
                  pl.BlockSpec((3, W), lambda b, p: (0, p)),
                  pl.BlockSpec((3, W), lambda b, p: (0, P + p))],
        out_specs=pl.BlockSpec((S, W), lambda b, p: (b, p)),
        out_shape=jax.ShapeDtypeStruct((T, Fd), BF16),
        scratch_shapes=[pltpu.VMEM((S + 2 * HALO, W), F32)] * 2,
        compiler_params=_params(("parallel", "parallel")),
    )(up3, taps, taps)


def _ffn_act_bwd(up3, dact3, taps, S, name):
    _, T, Fd = up3.shape
    W = FFN_LANES
    B, P = T // S, Fd // W

    def body(up_ref, da_ref, wg_ref, wv_ref, dup_ref, dwg_ref, dwv_ref, pad_ug, pad_uv, pad_dg, pad_dv):
        b = pl.program_id(1)
        for pad in (pad_ug, pad_uv, pad_dg, pad_dv):
            _zero_halo(pad, S)
        pad_ug[HALO:HALO + S, :] = up_ref[0].astype(F32)
        pad_uv[HALO:HALO + S, :] = up_ref[1].astype(F32)
        wg = wg_ref[...]
        wv = wv_ref[...]
        for r0 in range(0, S, CONV_ROWS):
            cg = _conv3_at(pad_ug, wg, r0)
            cv = _conv3_at(pad_uv, wv, r0)
            sg = _sigmoid(cg)
            da = da_ref[r0:r0 + CONV_ROWS, :].astype(F32)
            t = da * sg
            dcv = cg * t
            pad_dg[HALO + r0:HALO + r0 + CONV_ROWS, :] = cv * ((t + dcv) - dcv * sg)
            pad_dv[HALO + r0:HALO + r0 + CONV_ROWS, :] = dcv
        sums_g = [jnp.zeros((8, W), F32)] * 3
        sums_v = [jnp.zeros((8, W), F32)] * 3
        for r0 in range(0, S, CONV_ROWS):
            dug, parts_g = _conv3_grads_at(pad_dg, pad_ug, wg, r0)
            duv, parts_v = _conv3_grads_at(pad_dv, pad_uv, wv, r0)
            dup_ref[0, r0:r0 + CONV_ROWS, :] = dug.astype(BF16)
            dup_ref[1, r0:r0 + CONV_ROWS, :] = duv.astype(BF16)
            sums_g = [a + p for a, p in zip(sums_g, parts_g)]
            sums_v = [a + p for a, p in zip(sums_v, parts_v)]
        dwg = jnp.concatenate([jnp.sum(a, axis=0, keepdims=True) for a in sums_g], axis=0)
        dwv = jnp.concatenate([jnp.sum(a, axis=0, keepdims=True) for a in sums_v], axis=0)

        @pl.when(b == 0)
        def _():
            dwg_ref[...] = dwg
            dwv_ref[...] = dwv

        @pl.when(b != 0)
        def _():
            dwg_ref[...] += dwg
            dwv_ref[...] += dwv

    tap_out = pl.BlockSpec((3, W), lambda p, b: (0, p))
    return pl.pallas_call(
        body, grid=(P, B), name=name,
        in_specs=[pl.BlockSpec((2, S, W), lambda p, b: (0, b, p)),
                  pl.BlockSpec((None, S, W), lambda p, b: (0, b, p)),
                  pl.BlockSpec((3, W), lambda p, b: (0, p)),
                  pl.BlockSpec((3, W), lambda p, b: (0, P + p))],
        out_specs=[pl.BlockSpec((2, S, W), lambda p, b: (0, b, p)), tap_out, tap_out],
        out_shape=[jax.ShapeDtypeStruct((2, T, Fd), BF16),
                   jax.ShapeDtypeStruct((3, Fd), F32), jax.ShapeDtypeStruct((3, Fd), F32)],
        scratch_shapes=[pltpu.VMEM((S + 2 * HALO, W), F32)] * 4,
        compiler_params=_params(("parallel", "arbitrary")),
    )(up3, dact3, taps, taps)


def _final_norm_loss(x, g, target, tm, name):
    T, D = x.shape

    def body(x_ref, g_ref, t_ref, dx_ref, dg_ref, loss_ref):
        xv = x_ref[...]
        rstd = lax.rsqrt(jnp.mean(xv * xv, axis=-1, keepdims=True) + EPS)
        xn = xv * rstd
        err = xn * g_ref[...] - t_ref[...]
        part = 0.5 * jnp.sum(jnp.mean(err * err, axis=-1, keepdims=True), axis=0, keepdims=True)
        dy = err * (1.0 / D)
        a = dy * g_ref[...]
        dx_ref[...] = rstd * (a - xn * jnp.mean(a * xn, axis=-1, keepdims=True))
        dg = jnp.sum(dy * xn, axis=0, keepdims=True)
        lpart = jnp.broadcast_to(part, (1, LANES))

        @pl.when(pl.program_id(0) == 0)
        def _():
            dg_ref[...] = dg
            loss_ref[...] = lpart

        @pl.when(pl.program_id(0) != 0)
        def _():
            dg_ref[...] += dg
            loss_ref[...] += lpart

    row = pl.BlockSpec((tm, D), lambda i: (i, 0))
    return pl.pallas_call(
        body, grid=(T // tm,), name=name,
        in_specs=[row, pl.BlockSpec((1, D), lambda i: (0, 0)), row],
        out_specs=[row, pl.BlockSpec((1, D), lambda i: (0, 0)), pl.BlockSpec((1, LANES), lambda i: (0, 0))],
        out_shape=[jax.ShapeDtypeStruct((T, D), F32), jax.ShapeDtypeStruct((1, D), F32),
                   jax.ShapeDtypeStruct((1, LANES), F32)],
        compiler_params=_params(("arbitrary",)),
    )(x, g, target)


def _row_tile(rows, cols, budget_elems=512 * 1024):
    tr = rows
    while tr * cols > budget_elems and tr % 32 == 0:
        tr //= 2
    return tr


def _prefetch_call(body, grid, in_specs, out_specs, out_shape, name, sem, aliases=None):
    return pl.pallas_call(
        body, name=name, out_shape=out_shape,
        grid_spec=pltpu.PrefetchScalarGridSpec(num_scalar_prefetch=1, grid=grid, in_specs=in_specs,
                                               out_specs=out_specs),
        input_output_aliases=aliases or {},
        compiler_params=_params(sem))


def _cast_into_full(w, layer, colwise, where, name):
    _, K, N = w.shape
    tr = _row_tile(K, N)
    nrb = K // tr
    full_shape = (1, K, 4 * N) if colwise else (1, 4 * K, N)

    def body(where_ref, w_ref, o_ref):
        del where_ref
        o_ref[...] = w_ref[...].astype(BF16)

    if colwise:
        out_map = lambda i, wh: (0, i, wh[0])
    else:
        out_map = lambda i, wh: (0, wh[0] * nrb + i, 0)
    return _prefetch_call(
        body, (nrb,), [pl.BlockSpec((None, tr, N), lambda i, wh: (layer, i, 0))],
        pl.BlockSpec((None, tr, N), out_map), pltpu.HBM(full_shape, BF16), name,
        ("parallel",))(where, w)


def _chip_sum(g3, other, colwise, where, name):
    L, K, N = g3.shape
    hk, hn = (K // 2, N) if colwise else (K, N // 2)
    tr = _row_tile(hk, hn)
    nrb = hk // tr

    def body(where_ref, g_ref, o_ref, s_ref):
        del where_ref
        s_ref[...] = (g_ref[...].astype(F32) + o_ref[...].astype(F32)).astype(BF16)

    if colwise:
        g_map = lambda l, i, wh: (l, wh[1] * nrb + i, 0)
    else:
        g_map = lambda l, i, wh: (l, i, wh[1])
    blk = pl.BlockSpec((None, tr, hn), lambda l, i, wh: (l, i, 0))
    return _prefetch_call(
        body, (L, nrb), [pl.BlockSpec((None, tr, hn), g_map), blk], blk,
        pltpu.HBM((L, hk, hn), BF16), name, ("parallel", "parallel"))(where, _in_hbm(g3), _in_hbm(other))


def _owner_sum(chip_sum, received, colwise, where, layer, n_layers, prev, name):
    _, hk, hn = chip_sum.shape
    pk, pn = (hk, hn // 4) if colwise else (hk // 4, hn)
    tr = _row_tile(pk, pn)
    nrb = pk // tr
    shard_shape = (n_layers, 2 * pk, pn) if colwise else (n_layers, pk, 2 * pn)

    def body(where_ref, own_ref, rec_ref, *rest):
        del where_ref
        o_ref = rest[-1]
        acc = own_ref[...].astype(F32)
        for j in range(3):
            acc = acc + rec_ref[j].astype(F32)
        o_ref[...] = acc

    if colwise:
        own_map = lambda i, wh: (0, i, wh[0])
        out_map = lambda i, wh: (layer, wh[1] * nrb + i, 0)
    else:
        own_map = lambda i, wh: (0, wh[0] * nrb + i, 0)
        out_map = lambda i, wh: (layer, i, wh[1])
    in_specs = [pl.BlockSpec((None, tr, pn), own_map),
                pl.BlockSpec((3, None, tr, pn), lambda i, wh: (0, 0, i, 0))]
    operands = [where, _in_hbm(chip_sum), _in_hbm(received)]
    if prev is not None:
        in_specs.append(ANY)
        operands.append(prev)
    return _prefetch_call(
        body, (nrb,), in_specs, pl.BlockSpec((None, tr, pn), out_map), pltpu.HBM(shard_shape, F32), name,
        ("parallel",), None if prev is None else {3: 0})(*operands)


def _adamw(w, g, m, v, name):
    R, Cc = w.shape
    tr = _row_tile(R, Cc, 256 * 1024)

    def body(w_ref, g_ref, m_ref, v_ref, d_ref, nm_ref, nv_ref, go_ref):
        gv = g_ref[...]
        go_ref[...] = gv
        nm = ADAM_B1 * m_ref[...] + (1.0 - ADAM_B1) * gv
        nv = ADAM_B2 * v_ref[...] + (1.0 - ADAM_B2) * (gv * gv)
        m_hat = nm / (1.0 - ADAM_B1 ** ADAM_STEP)
        v_hat = nv / (1.0 - ADAM_B2 ** ADAM_STEP)
        d_ref[...] = -ADAM_LR * (m_hat / (jnp.sqrt(v_hat) + ADAM_EPS) + ADAM_WD * w_ref[...])
        nm_ref[...] = nm
        nv_ref[...] = nv

    blk = pl.BlockSpec((tr, Cc), lambda i: (i, 0))
    shp = jax.ShapeDtypeStruct((R, Cc), F32)
    return pl.pallas_call(
        body, grid=(R // tr,), name=name,
        in_specs=[blk] * 4, out_specs=[blk] * 4, out_shape=[shp] * 4,
        compiler_params=_params(("parallel",)),
    )(w, g, m, v)


COL_SHARDED = (True, False, True, False)


def _position():
    x, y, c = lax.axis_index("x"), lax.axis_index("y"), lax.axis_index("c")
    chips = [(1 - x, y), (x, 1 - y), (1 - x, 1 - y)]
    return x, y, c, chips


def _span(index, size, align):
    return pl.ds(pl.multiple_of(index * size, align), size)


def _window(ref, colwise, shard, half, shards=4):
    _, K, N = ref.shape
    rows = cols = slice(None)
    if colwise:
        if half is not None:
            rows = _span(half, K // 2, 16)
        if shard is not None:
            cols = _span(shard, N // shards, LANES)
    else:
        if shard is not None:
            rows = _span(shard, K // shards, 16)
        if half is not None:
            cols = _span(half, N // 2, LANES)
    return ref.at[:, rows, cols]


HBM = pl.BlockSpec(memory_space=pltpu.HBM)
SEMAPHORES = pl.BlockSpec(memory_space=pltpu.SEMAPHORE)


def _gather_start(fulls, colwise, group_sizes, after, name):
    n = len(fulls)
    n_groups = len(group_sizes)

    n_in = n if after is None else n + 1

    def body(*refs):
        ins = refs[:n]
        sems = refs[n_in:n_in + 2 * n_groups]
        x, y, c, chips = _position()
        me = 2 * x + y
        i = 0
        for g, size in enumerate(group_sizes):
            for a in range(size):
                win = _window(ins[i], colwise[i], me, c)
                for j, chip in enumerate(chips):
                    pltpu.make_async_remote_copy(
                        src_ref=win, dst_ref=win, send_sem=sems[2 * g].at[a * 3 + j],
                        recv_sem=sems[2 * g + 1].at[a * 3 + j],
                        device_id=(chip[0], chip[1], c), device_id_type=MESH_ID).start()
                i += 1

    sem_shapes = []
    for size in group_sizes:
        sem_shapes += [pltpu.SemaphoreType.DMA((3 * size,)), pltpu.SemaphoreType.DMA((3 * size,))]
    operands = [pltpu.with_memory_space_constraint(f, pltpu.HBM) for f in fulls]
    in_specs = [HBM] * n
    if after is not None:
        operands.append(after)
        in_specs.append(ANY)
    outs = pl.pallas_call(
        body, name=name,
        in_specs=in_specs, out_specs=[SEMAPHORES] * (2 * n_groups) + [HBM] * n,
        out_shape=sem_shapes + [pltpu.HBM(f.shape, f.dtype) for f in fulls],
        input_output_aliases={i: 2 * n_groups + i for i in range(n)},
        compiler_params=pltpu.CompilerParams(has_side_effects=pltpu.SideEffectType.DATAFLOW_SIDE_EFFECTING),
    )(*operands)
    sems = [(outs[2 * g], outs[2 * g + 1]) for g in range(n_groups)]
    return sems, list(outs[2 * n_groups:])


def _to_sibling(ref, colwise, chip, half, x, y, c, send_sem, recv_sem):
    win = _window(ref, colwise, 2 * chip[0] + chip[1], half)
    return pltpu.make_async_remote_copy(
        src_ref=win, dst_ref=win, send_sem=send_sem, recv_sem=recv_sem,
        device_id=(x, y, 1 - c), device_id_type=MESH_ID)


def _gather_pass(in_flight, colwise, sems, after, name):
    n = len(in_flight)

    def body(*refs):
        ins = refs[:n]
        send_sems, recv_sems = refs[n], refs[n + 1]
        pass_send, pass_recv = refs[-2 - n], refs[-1 - n]
        x, y, c, chips = _position()
        me = 2 * x + y
        for a in range(n):
            for j, chip in enumerate(chips):
                k = a * 3 + j
                pltpu.make_async_remote_copy(
                    src_ref=_window(ins[a], colwise[a], me, c),
                    dst_ref=_window(ins[a], colwise[a], 2 * chip[0] + chip[1], c),
                    send_sem=send_sems.at[k], recv_sem=recv_sems.at[k],
                    device_id=(chip[0], chip[1], c), device_id_type=MESH_ID).wait()
                _to_sibling(ins[a], colwise[a], chip, c, x, y, c, pass_send.at[k], pass_recv.at[k]).start()

    operands = list(in_flight) + list(sems)
    in_specs = [HBM] * n + [SEMAPHORES] * 2
    if after is not None:
        operands.append(after)
        in_specs.append(ANY)
    outs = pl.pallas_call(
        body, name=name,
        in_specs=in_specs, out_specs=[SEMAPHORES] * 2 + [HBM] * n,
        out_shape=[pltpu.SemaphoreType.DMA((3 * n,)), pltpu.SemaphoreType.DMA((3 * n,))]
        + [pltpu.HBM(f.shape, f.dtype) for f in in_flight],
        input_output_aliases={i: 2 + i for i in range(n)},
        compiler_params=pltpu.CompilerParams(has_side_effects=pltpu.SideEffectType.DATAFLOW_SIDE_EFFECTING),
    )(*operands)
    return (outs[0], outs[1]), list(outs[2:])


def _gather_wait(in_flight, colwise, sems, after, name):
    n = len(in_flight)

    def body(*refs):
        ins = refs[:n]
        send_sems, recv_sems = refs[n], refs[n + 1]
        x, y, c, chips = _position()
        for a in range(n):
            for j, chip in enumerate(chips):
                k = a * 3 + j
                _to_sibling(ins[a], colwise[a], chip, c, x, y, c, send_sems.at[k], recv_sems.at[k]).wait_send()
                _to_sibling(ins[a], colwise[a], chip, 1 - c, x, y, c, send_sems.at[k], recv_sems.at[k]).wait_recv()

    operands = list(in_flight) + list(sems)
    in_specs = [HBM] * n + [SEMAPHORES] * 2
    if after is not None:
        operands.append(after)
        in_specs.append(ANY)
    outs = pl.pallas_call(
        body, name=name,
        in_specs=in_specs, out_specs=[HBM] * n,
        out_shape=[pltpu.HBM(f.shape, f.dtype) for f in in_flight],
        input_output_aliases={i: i for i in range(n)},
        compiler_params=pltpu.CompilerParams(has_side_effects=pltpu.SideEffectType.DATAFLOW_SIDE_EFFECTING),
    )(*operands)
    return list(outs)


def _exchange_copy(g_ref, land_ref, colwise, x, y, c, send_sem, recv_sem):
    return pltpu.make_async_remote_copy(
        src_ref=_window(g_ref, colwise, None, 1 - c), dst_ref=land_ref, send_sem=send_sem, recv_sem=recv_sem,
        device_id=(x, y, 1 - c), device_id_type=MESH_ID)


def _exchange_start(grads, colwise, name):
    n = len(grads)
    lands = []
    for g, cw in zip(grads, colwise):
        L, K, N = g.shape
        lands.append(lax.empty((L, K // 2, N) if cw else (L, K, N // 2), g.dtype))

    def body(*refs):
        src, land = refs[:n], refs[n:2 * n]
        send_sems, recv_sems = refs[2 * n], refs[2 * n + 1]
        x, y, c, _ = _position()
        for i in range(n):
            _exchange_copy(src[i], land[i], colwise[i], x, y, c, send_sems.at[i], recv_sems.at[i]).start()

    arrays = list(grads) + lands
    outs = pl.pallas_call(
        body, name=name,
        in_specs=[HBM] * (2 * n), out_specs=[SEMAPHORES] * 2 + [HBM] * (2 * n),
        out_shape=[pltpu.SemaphoreType.DMA((n,)), pltpu.SemaphoreType.DMA((n,))]
        + [pltpu.HBM(a.shape, a.dtype) for a in arrays],
        input_output_aliases={i: 2 + i for i in range(2 * n)},
        compiler_params=pltpu.CompilerParams(has_side_effects=pltpu.SideEffectType.DATAFLOW_SIDE_EFFECTING),
    )(*[pltpu.with_memory_space_constraint(a, pltpu.HBM) for a in arrays])
    return (outs[0], outs[1]), list(outs[2:2 + n]), list(outs[2 + n:])


def _exchange_wait(grads, lands, colwise, sems, after, name):
    n = len(grads)

    def body(*refs):
        src, land = refs[:n], refs[n:2 * n]
        send_sems, recv_sems = refs[2 * n], refs[2 * n + 1]
        x, y, c, _ = _position()
        for i in range(n):
            _exchange_copy(src[i], land[i], colwise[i], x, y, c, send_sems.at[i], recv_sems.at[i]).wait()

    arrays = list(grads) + list(lands)
    operands = arrays + list(sems)
    in_specs = [HBM] * (2 * n) + [SEMAPHORES] * 2
    if after is not None:
        operands.append(after)
        in_specs.append(ANY)
    outs = pl.pallas_call(
        body, name=name,
        in_specs=in_specs, out_specs=[HBM] * (2 * n),
        out_shape=[pltpu.HBM(a.shape, a.dtype) for a in arrays],
        input_output_aliases={i: i for i in range(2 * n)},
        compiler_params=pltpu.CompilerParams(has_side_effects=pltpu.SideEffectType.DATAFLOW_SIDE_EFFECTING),
    )(*operands)
    return list(outs[:n]), list(outs[n:])


def _scatter_copy(src_ref, land_ref, colwise, j, chip, c, send_sem, recv_sem):
    return pltpu.make_async_remote_copy(
        src_ref=_window(src_ref, colwise, 2 * chip[0] + chip[1], None), dst_ref=land_ref.at[j],
        send_sem=send_sem, recv_sem=recv_sem, device_id=(chip[0], chip[1], c), device_id_type=MESH_ID)


def _scatter_start(chip_sums, colwise, name):
    n = len(chip_sums)
    lands = []
    for g, cw in zip(chip_sums, colwise):
        L, hk, hn = g.shape
        lands.append(lax.empty((3, L, hk, hn // 4) if cw else (3, L, hk // 4, hn), g.dtype))

    def body(*refs):
        src, land = refs[:n], refs[n:2 * n]
        send_sems, recv_sems = refs[2 * n], refs[2 * n + 1]
        x, y, c, chips = _position()
        for i in range(n):
            for j, chip in enumerate(chips):
                _scatter_copy(src[i], land[i], colwise[i], j, chip, c, send_sems.at[i * 3 + j],
                              recv_sems.at[i * 3 + j]).start()

    arrays = list(chip_sums) + lands
    outs = pl.pallas_call(
        body, name=name,
        in_specs=[HBM] * (2 * n), out_specs=[SEMAPHORES] * 2 + [HBM] * (2 * n),
        out_shape=[pltpu.SemaphoreType.DMA((3 * n,)), pltpu.SemaphoreType.DMA((3 * n,))]
        + [pltpu.HBM(a.shape, a.dtype) for a in arrays],
        input_output_aliases={i: 2 + i for i in range(2 * n)},
        compiler_params=pltpu.CompilerParams(has_side_effects=pltpu.SideEffectType.DATAFLOW_SIDE_EFFECTING),
    )(*[pltpu.with_memory_space_constraint(a, pltpu.HBM) for a in arrays])
    return (outs[0], outs[1]), list(outs[2:2 + n]), list(outs[2 + n:])


def _scatter_wait(sources, lands, colwise, sems, after, name):
    n = len(sources)

    def body(*refs):
        src, land = refs[:n], refs[n:2 * n]
        send_sems, recv_sems = refs[2 * n], refs[2 * n + 1]
        x, y, c, chips = _position()
        for i in range(n):
            for j, chip in enumerate(chips):
                cp = _scatter_copy(src[i], land[i], colwise[i], j, chip, c, send_sems.at[i * 3 + j],
                                   recv_sems.at[i * 3 + j])
                cp.wait_send()
                cp.wait_recv()

    arrays = list(sources) + list(lands)
    operands = arrays + list(sems)
    in_specs = [HBM] * (2 * n) + [SEMAPHORES] * 2
    if after is not None:
        operands.append(after)
        in_specs.append(ANY)
    outs = pl.pallas_call(
        body, name=name,
        in_specs=in_specs, out_specs=[HBM] * (2 * n),
        out_shape=[pltpu.HBM(a.shape, a.dtype) for a in arrays],
        input_output_aliases={i: i for i in range(2 * n)},
        compiler_params=pltpu.CompilerParams(has_side_effects=pltpu.SideEffectType.DATAFLOW_SIDE_EFFECTING),
    )(*operands)
    return list(outs[:n]), list(outs[n:])


def _share_with_sibling(shards, colwise, name):
    n = len(shards)

    def body(*refs):
        out = refs[n:2 * n]
        send_sems, recv_sems = refs[2 * n:]
        x, y, c, _ = _position()

        def copy(i, half):
            win = _window(out[i], colwise[i], None, half)
            return pltpu.make_async_remote_copy(
                src_ref=win, dst_ref=win, send_sem=send_sems.at[i], recv_sem=recv_sems.at[i],
                device_id=(x, y, 1 - c), device_id_type=MESH_ID)

        for i in range(n):
            copy(i, c).start()
        for i in range(n):
            copy(i, 1 - c).wait_recv()
        for i in range(n):
            copy(i, c).wait_send()

    return pl.pallas_call(
        body, name=name,
        in_specs=[ANY] * n, out_specs=[ANY] * n,
        out_shape=[jax.ShapeDtypeStruct(s.shape, s.dtype) for s in shards],
        input_output_aliases={i: i for i in range(n)},
        scratch_shapes=[pltpu.SemaphoreType.DMA((n,)), pltpu.SemaphoreType.DMA((n,))],
    )(*shards)


def _chip_exchange(buf, me, x, y, c, chips, send_sems, recv_sems):
    def copy(j, chip, slot):
        return pltpu.make_async_remote_copy(
            src_ref=buf.at[me], dst_ref=buf.at[slot], send_sem=send_sems.at[j], recv_sem=recv_sems.at[j],
            device_id=(chip[0], chip[1], c), device_id_type=MESH_ID)

    for j, chip in enumerate(chips):
        copy(j, chip, me).start()
    for j, chip in enumerate(chips):
        copy(j, chip, 2 * chip[0] + chip[1]).wait_recv()
    for j, chip in enumerate(chips):
        copy(j, chip, me).wait_send()


def _gather_over_chips(pack, name):
    R, Cc = pack.shape

    def body(p_ref, o_ref, send_sems, recv_sems):
        x, y, c, chips = _position()
        me = 2 * x + y
        o_ref[me] = p_ref[...]
        _chip_exchange(o_ref, me, x, y, c, chips, send_sems, recv_sems)

    vmem = pl.BlockSpec(memory_space=pltpu.VMEM)
    return pl.pallas_call(
        body, name=name,
        in_specs=[vmem], out_specs=vmem, out_shape=jax.ShapeDtypeStruct((4, R, Cc), F32),
        scratch_shapes=[pltpu.SemaphoreType.DMA((3,)), pltpu.SemaphoreType.DMA((3,))],
    )(pack)


def _all_reduce_small(pack, name):
    R, Cc = pack.shape

    def body(p_ref, o_ref, sibling, buf, send_sems, recv_sems):
        x, y, c, chips = _position()
        me = 2 * x + y
        swap = pltpu.make_async_remote_copy(
            src_ref=p_ref, dst_ref=sibling, send_sem=send_sems.at[3], recv_sem=recv_sems.at[3],
            device_id=(x, y, 1 - c), device_id_type=MESH_ID)
        swap.start()
        swap.wait()
        buf[me] = p_ref[...] + sibling[...]
        _chip_exchange(buf, me, x, y, c, chips, send_sems, recv_sems)
        o_ref[...] = (buf[0] + buf[1]) + (buf[2] + buf[3])

    vmem = pl.BlockSpec(memory_space=pltpu.VMEM)
    return pl.pallas_call(
        body, name=name,
        in_specs=[vmem], out_specs=vmem, out_shape=jax.ShapeDtypeStruct((R, Cc), F32),
        scratch_shapes=[pltpu.VMEM((R, Cc), F32), pltpu.VMEM((4, R, Cc), F32), pltpu.SemaphoreType.DMA((4,)),
                        pltpu.SemaphoreType.DMA((4,))],
    )(pack)


def _local_forward_backward(x2, target2, S, pass_on, fetch, reduce_begin, reduce_commit, layers, final_g, tm=512):
    T, D = x2.shape
    C = D // 2
    n_heads = C // GROUP
    n_layers = len(layers)
    weights = {}
    saved = []
    xc = x2
    for li, lw in enumerate(layers):
        if li == 0:
            pass_on(0, None)
            weights.update(fetch(0, None))
        h1, qkv3, cv3 = _norm_proj(xc, lw["norm1"], weights[li, "w_in"], 0, ((3, C, F32), (3, C, BF16)), tm,
                                   min(C, 512), f"l{li}_norm_in_proj")
        if li == 0:
            pass_on(1, h1)
        o, lse, mix = _attn_fwd(qkv3, lw["attn_g"], 2, S, n_heads, f"l{li}_attn_fwd")
        pin = None
        if li == 0:
            weights.update(fetch(1, o))
            pin = pass_on(3, pass_on(2, o))
        mix = _mix_conv_fwd(cv3, lw["taps"], lw["conv_g"], mix, S, f"l{li}_mix_conv_fwd", pin)
        x_mid = _proj_residual(mix, weights[li, "w_out"], 0, xc, tm, f"l{li}_out_proj")
        if li == 0:
            weights.update(fetch(2, x_mid))
        Fd = weights[li, "ffn_up"].shape[2] // 2
        h2, up3 = _norm_proj(x_mid, lw["norm2"], weights[li, "ffn_up"], 0, ((2, Fd, BF16),), tm, 256,
                             f"l{li}_norm_ffn_up")
        if li == 0:
            weights.update(fetch(3, up3))
        act = _ffn_act_fwd(up3, lw["ffn_taps"], S, f"l{li}_ffn_act_fwd")
        pin = pass_on(li + 4, act) if li + 1 < n_layers else None
        x_out = _proj_residual(act.reshape(1, T, Fd), weights[li, "ffn_down"], 0, x_mid, tm, f"l{li}_ffn_down", pin)
        if li + 1 < n_layers:
            weights.update(fetch(li + 4, x_out))
        saved.append(dict(x_in=xc, h1=h1, qkv3=qkv3, cv3=cv3, o=o, lse=lse, mix=mix, x_mid=x_mid, h2=h2, up3=up3,
                          act=act))
        xc = x_out

    dx, d_final_g, loss_part = _final_norm_loss(xc, final_g, target2, tm, "final_norm_loss")

    small = [None] * n_layers
    started = None
    for li in reversed(range(n_layers)):
        lw, sv = layers[li], saved[li]
        w_in, w_out, ffn_up, ffn_down = (weights[li, n] for n in ("w_in", "w_out", "ffn_up", "ffn_down"))
        dxb, dact3 = _grad_through_weight(dx, ffn_down, 0, 1, Fd, BF16, tm, 256, f"l{li}_d_act", started)
        Fd = ffn_down.shape[1]
        d_ffn_down = _weight_grad(sv["act"].reshape(1, T, Fd), dxb.reshape(1, T, D), Fd // 2, D,
                                  f"l{li}_d_ffn_down")
        dup3, d_taps_g, d_taps_v = _ffn_act_bwd(sv["up3"], dact3, lw["ffn_taps"], S, f"l{li}_ffn_act_bwd")
        d_ffn_up = _weight_grad(sv["h2"].reshape(1, D, T), dup3, D, Fd // 2, f"l{li}_d_ffn_up", a_transposed=True)
        if li == 0:
            early = reduce_begin({(li, "ffn_down"): d_ffn_down, (li, "ffn_up"): d_ffn_up})
        dx_mid, d_norm2 = _grad_through_proj_norm(dup3, ffn_up, 0, sv["x_mid"], lw["norm2"], dx, tm,
                                                  f"l{li}_d_norm2")
        started = reduce_commit(early, dx_mid) if li == 0 else None
        dxmb, dmix3 = _grad_through_weight(dx_mid, w_out, 0, 2, C, F32, tm, min(C, 512), f"l{li}_d_mix", started)
        d_w_out = _weight_grad(sv["mix"], dxmb.reshape(1, T, D), min(C, 256), D, f"l{li}_d_w_out")
        dproj, d_attn_g = _attn_bwd(sv["qkv3"], sv["o"], sv["lse"], dmix3, lw["attn_g"], 6, S, n_heads,
                                    f"l{li}_attn_bwd")
        dproj, d_taps, d_conv_g = _mix_conv_bwd(sv["cv3"], dmix3, lw["taps"], lw["conv_g"], dproj, S,
                                                f"l{li}_mix_conv_bwd")
        d_w_in = _weight_grad(sv["h1"].reshape(1, D, T), dproj, D, C, f"l{li}_d_w_in", a_transposed=True)
        late = {(li, "w_out"): d_w_out, (li, "w_in"): d_w_in}
        if li > 0:
            late.update({(li, "ffn_down"): d_ffn_down, (li, "ffn_up"): d_ffn_up})
        late = reduce_begin(late)
        dx, d_norm1 = _grad_through_proj_norm(dproj, w_in, 0, sv["x_in"], lw["norm1"], dx_mid, tm,
                                              f"l{li}_d_norm1")
        started = reduce_commit(late, dx)
        small[li] = dict(norm1=d_norm1, taps=d_taps, attn_g=d_attn_g, conv_g=d_conv_g, norm2=d_norm2,
                         ffn_taps=jnp.concatenate([d_taps_g, d_taps_v], axis=1))
    return loss_part, dx, small, d_final_g


SMALL_ORDER = ("norm1", "attn_g", "conv_g", "norm2", "taps", "ffn_taps")


def _pack_small(small, d_final_g, loss_row):
    parts = [small[li][k].reshape(-1) for li in range(len(small)) for k in SMALL_ORDER]
    loss_rows = jnp.tile(loss_row.reshape(1, LANES), (8, 1))
    return jnp.concatenate(parts + [d_final_g.reshape(-1), loss_rows.reshape(-1)]).reshape(-1, LANES)


def _unpack_small(pack, small, d_final_g):
    flat = pack.reshape(-1)
    out, pos = [dict() for _ in small], 0
    for li in range(len(small)):
        for k in SMALL_ORDER:
            n = small[li][k].size
            out[li][k] = flat[pos:pos + n].reshape(small[li][k].shape)
            pos += n
    return out, flat[pos:pos + d_final_g.size], flat[pos + d_final_g.size]


def kernel(x, norm1_g, w_in, mix_conv_w, attn_out_g, conv_out_g, w_out, norm2_g, ffn_up, ffn_conv_w, ffn_down, final_norm_g, loss_target, m_norm1_g, m_w_in, m_mix_conv_w, m_attn_out_g, m_conv_out_g, m_w_out, m_norm2_g, m_ffn_up, m_ffn_conv_w, m_ffn_down, m_final_norm_g, v_norm1_g, v_w_in, v_mix_conv_w, v_attn_out_g, v_conv_out_g, v_w_out, v_norm2_g, v_ffn_up, v_ffn_conv_w, v_ffn_down, v_final_norm_g):
    Bl, S, D = x.shape
    L = w_in.shape[0]
    T = Bl * S
    shard = 2 * lax.axis_index("x") + lax.axis_index("y")
    where = jnp.stack([shard, lax.axis_index("c")]).astype(jnp.int32)
    big_names = ("w_in", "w_out", "ffn_up", "ffn_down")

    taps_w, ftaps_w = mix_conv_w.shape[2], ffn_conv_w.shape[2]
    tap_pack = _gather_over_chips(
        jnp.concatenate([mix_conv_w.reshape(-1), ffn_conv_w.reshape(-1)]).reshape(-1, LANES), "all_gather_taps")
    by_chip = tap_pack.reshape(4, -1)
    n_taps = mix_conv_w.size
    taps_full = by_chip[:, :n_taps].reshape(4, L, 3, taps_w).transpose(1, 2, 0, 3).reshape(L, 3, 4 * taps_w)
    ftaps_full = by_chip[:, n_taps:].reshape(4, L, 3, ftaps_w).transpose(1, 2, 0, 3).reshape(L, 3, 4 * ftaps_w)

    big_shards = dict(zip(big_names, (w_in, w_out, ffn_up, ffn_down)))
    col_of = dict(zip(big_names, COL_SHARDED))
    groups = [[(0, n)] for n in big_names] + [[(l, n) for n in big_names] for l in range(1, L)]
    sems, in_flight = [], {}
    all_started = tap_pack
    for first, last in ((0, 1), (1, len(groups))):
        keys = [k for g in groups[first:last] for k in g]
        new_sems, arrays = _gather_start(
            [_cast_into_full(big_shards[n], l, col_of[n], where, f"cast_{n}_{l}") for l, n in keys],
            [col_of[n] for _, n in keys], [len(g) for g in groups[first:last]], all_started,
            f"gather_start_{first}")
        sems += new_sems
        in_flight.update(zip(keys, arrays))
        all_started = arrays[-1]

    def pass_on(g, after):
        after = all_started if g == 0 else after
        sems[g], arrays = _gather_pass([in_flight[k] for k in groups[g]], [col_of[n] for _, n in groups[g]],
                                       sems[g], after, f"gather_pass_{g}")
        in_flight.update(zip(groups[g], arrays))
        return arrays[0]

    def fetch(g, after):
        done = _gather_wait([in_flight[k] for k in groups[g]], [col_of[n] for _, n in groups[g]], sems[g], after,
                            f"gather_wait_{g}")
        return dict(zip(groups[g], done))

    pending = []

    begun = []

    def reduce_begin(grads):
        g = len(begun)
        keys = list(grads)
        cols = [col_of[n] for _, n in keys]
        begun.append((g, keys, cols) + _exchange_start([grads[k] for k in keys], cols, f"exchange_start_{g}"))
        return begun[-1]

    def reduce_commit(handle, after):
        g, keys, cols, ex_sems, mine, lands = handle
        mine, others = _exchange_wait(mine, lands, cols, ex_sems, after, f"exchange_wait_{g}")
        chip_sums = [_chip_sum(m, o, cw, where, f"chip_sum_{k[1]}_{k[0]}")
                     for k, m, o, cw in zip(keys, mine, others, cols)]
        pending.append((keys, cols) + _scatter_start(chip_sums, cols, f"scatter_start_{g}"))
        return pending[-1][3][0]

    layers = [dict(norm1=norm1_g[l:l + 1], taps=taps_full[l], attn_g=attn_out_g[l:l + 1],
                   conv_g=conv_out_g[l:l + 1], norm2=norm2_g[l:l + 1], ffn_taps=ftaps_full[l]) for l in range(L)]

    loss_part, dx, small, d_final_g = _local_forward_backward(
        x.reshape(T, D), loss_target.reshape(T, D), S, pass_on, fetch, reduce_begin, reduce_commit, layers,
        final_norm_g.reshape(1, D))

    def finish_group(g, after):
        keys, cols, rs_sems, sources, lands = pending[g]
        sources, lands = _scatter_wait(sources, lands, cols, rs_sems, after, f"scatter_wait_{g}")
        for (l, n), cw, src, land in zip(keys, cols, sources, lands):
            reduced[n] = _owner_sum(src, land, cw, where, l, L, reduced[n], f"owner_sum_{n}_{l}")

    reduced = dict.fromkeys(big_names)
    last_started = pending[-1][3][0]
    for g in range(len(pending) - 1):
        finish_group(g, last_started)
    late_names = [n for n in big_names if any(n == name for _, name in pending[-1][0])]
    early_names = [n for n in big_names if n not in late_names]
    g_big = dict(zip(early_names, _share_with_sibling([reduced[n] for n in early_names],
                                                      [col_of[n] for n in early_names], "grad_share_early")))

    pack = _all_reduce_small(_pack_small(small, d_final_g, loss_part), "all_reduce_small_grads")
    g_small, g_final, loss = _unpack_small(pack, small, d_final_g)

    def stacked(key):
        return jnp.stack([g_small[l][key].reshape(g_small[l][key].shape[-2:] if key.endswith("taps") else (-1,))
                          for l in range(L)])

    g_norm1, g_attn, g_conv, g_norm2 = stacked("norm1"), stacked("attn_g"), stacked("conv_g"), stacked("norm2")
    g_taps = lax.dynamic_slice(stacked("taps"), (0, 0, shard * taps_w), (L, 3, taps_w))
    g_ftaps = lax.dynamic_slice(stacked("ffn_taps"), (0, 0, shard * ftaps_w), (L, 3, ftaps_w))

    grads_out = dict(norm1_g=g_norm1, w_in=None, mix_conv_w=g_taps, attn_out_g=g_attn, conv_out_g=g_conv,
                     w_out=None, norm2_g=g_norm2, ffn_up=None, ffn_conv_w=g_ftaps, ffn_down=None,
                     final_norm_g=g_final)
    weights = dict(norm1_g=norm1_g, w_in=w_in, mix_conv_w=mix_conv_w, attn_out_g=attn_out_g, conv_out_g=conv_out_g,
                   w_out=w_out, norm2_g=norm2_g, ffn_up=ffn_up, ffn_conv_w=ffn_conv_w, ffn_down=ffn_down,
                   final_norm_g=final_norm_g)
    ms = dict(norm1_g=m_norm1_g, w_in=m_w_in, mix_conv_w=m_mix_conv_w, attn_out_g=m_attn_out_g,
              conv_out_g=m_conv_out_g, w_out=m_w_out, norm2_g=m_norm2_g, ffn_up=m_ffn_up, ffn_conv_w=m_ffn_conv_w,
              ffn_down=m_ffn_down, final_norm_g=m_final_norm_g)
    vs = dict(norm1_g=v_norm1_g, w_in=v_w_in, mix_conv_w=v_mix_conv_w, attn_out_g=v_attn_out_g,
              conv_out_g=v_conv_out_g, w_out=v_w_out, norm2_g=v_norm2_g, ffn_up=v_ffn_up, ffn_conv_w=v_ffn_conv_w,
              ffn_down=v_ffn_down, final_norm_g=v_final_norm_g)
    names = list(weights)
    small_names = [n for n in names if n not in big_names]
    delta, new_m, new_v = {}, {}, {}

    def update_big(n):
        shp = weights[n].shape
        two_d = (shp[0] * shp[1], shp[2])
        d_, m_, v_, g_ = _adamw(weights[n].reshape(two_d), g_big[n].reshape(two_d), ms[n].reshape(two_d),
                                vs[n].reshape(two_d), f"adamw_{n}")
        delta[n], new_m[n], new_v[n], grads_out[n] = (a.reshape(shp) for a in (d_, m_, v_, g_))

    for n in early_names:
        update_big(n)
    finish_group(len(pending) - 1, delta[early_names[-1]] if early_names else None)
    g_big.update(zip(late_names, _share_with_sibling([reduced[n] for n in late_names],
                                                     [col_of[n] for n in late_names], "grad_share_late")))
    for n in late_names:
        update_big(n)

    def packed(tree):
        return jnp.concatenate([tree[n].reshape(-1) for n in small_names]).reshape(-1, LANES)

    d_, m_, v_, _ = _adamw(packed(weights), packed(grads_out), packed(ms), packed(vs), "adamw_small")
    pos = 0
    for n in small_names:
        size, shp = weights[n].size, weights[n].shape
        delta[n] = d_.reshape(-1)[pos:pos + size].reshape(shp)
        new_m[n] = m_.reshape(-1)[pos:pos + size].reshape(shp)
        new_v[n] = v_.reshape(-1)[pos:pos + size].reshape(shp)
        pos += size

    return (loss, dx.reshape(Bl, S, D), *[grads_out[n] for n in names], *[delta[n] for n in names],
            *[new_m[n] for n in names], *[new_v[n] for n in names])
```

```python
import functools
import math

import jax
import jax.numpy as jnp
from jax import lax
from jax.experimental import pallas as pl
from jax.experimental.pallas import tpu as pltpu

F32 = jnp.float32
BF16 = jnp.bfloat16
EPS = 1e-6
GROUP = 64
LANES = 128
BAND = 128
DILATIONS = (1, 4, 16)
NEG = -1e30
MIB = 1024 * 1024
MESH_ID = pl.DeviceIdType.MESH

ADAM_LR = 0.001
ADAM_B1 = 0.9
ADAM_B2 = 0.999
ADAM_EPS = 1e-08
ADAM_WD = 0.01
ADAM_STEP = 10


ANY = pl.BlockSpec(memory_space=pl.ANY)


def _in_hbm(x):
    return pltpu.with_memory_space_constraint(x, pltpu.HBM)


VMEM_LIMIT_MIB = 48
VMEM_LIMIT_RESIDENT_WEIGHT_MIB = 56
ROW_TILE = 512
DOT_CHUNK = 256
ATTN_UNROLL = 16
ATTN_ROWS = 256


def _params(sem=None, vmem_mb=VMEM_LIMIT_MIB):
    return pltpu.CompilerParams(dimension_semantics=sem, vmem_limit_bytes=vmem_mb * MIB)


def _nt(a, b):
    return lax.dot_general(a, b, (((1,), (1,)), ((), ())), preferred_element_type=F32)


def _tn(a, b):
    return lax.dot_general(a, b, (((0,), (0,)), ((), ())), preferred_element_type=F32)


def _seg_sum(x, is_a):
    s_a = jnp.sum(jnp.where(is_a, x, 0.0), axis=-1, keepdims=True)
    s_b = jnp.sum(jnp.where(is_a, 0.0, x), axis=-1, keepdims=True)
    return jnp.where(is_a, s_a, s_b)


def _lane_is_a():
    return lax.broadcasted_iota(jnp.int32, (1, LANES), 1) < GROUP


def _norm_proj(x, g, w3, groups, tm, chunk, name):
    T, D = x.shape
    N = w3.shape[2]
    assert sum(p * c for p, c, _ in groups) == N and T % tm == 0

    def body(x_ref, g_ref, w_ref, h_ref, *out_refs):
        xv = x_ref[...]
        rstd = lax.rsqrt(jnp.mean(xv * xv, axis=-1, keepdims=True) + EPS)
        h = ((xv * rstd) * g_ref[...]).astype(BF16)
        h_ref[...] = h.T
        col = 0
        for (pieces, width, dtype), o_ref in zip(groups, out_refs):
            for p in range(pieces):
                for c0 in range(0, width, chunk):
                    acc = jnp.dot(h, w_ref[:, col + c0:col + c0 + chunk], preferred_element_type=F32)
                    o_ref[p, :, c0:c0 + chunk] = acc.astype(dtype)
                col += width

    out_shape = [jax.ShapeDtypeStruct((D, T), BF16)]
    out_specs = [pl.BlockSpec((D, tm), lambda i: (0, i))]
    for pieces, width, dtype in groups:
        assert width % chunk == 0
        out_shape.append(jax.ShapeDtypeStruct((pieces, T, width), dtype))
        out_specs.append(pl.BlockSpec((pieces, tm, width), lambda i: (0, i, 0)))
    return pl.pallas_call(
        body, grid=(T // tm,), name=name,
        in_specs=[pl.BlockSpec((tm, D), lambda i: (i, 0)),
                  pl.BlockSpec((1, D), lambda i: (0, 0)),
                  pl.BlockSpec((None, D, N), lambda i: (0, 0, 0))],
        out_specs=out_specs, out_shape=out_shape,
        compiler_params=_params(("parallel",), VMEM_LIMIT_RESIDENT_WEIGHT_MIB),
    )(x, g, w3)


def _proj_residual(pieces3, w3, x, tm, name, after=None):
    P, T, C = pieces3.shape
    D = w3.shape[2]

    def body(a_ref, w_ref, x_ref, *rest):
        o_ref = rest[-1]
        acc = x_ref[...]
        for p in range(P):
            acc = acc + jnp.dot(a_ref[p], w_ref[p * C:(p + 1) * C, :], preferred_element_type=F32)
        o_ref[...] = acc

    in_specs = [pl.BlockSpec((P, tm, C), lambda i: (0, i, 0)),
                pl.BlockSpec((None, P * C, D), lambda i: (0, 0, 0)),
                pl.BlockSpec((tm, D), lambda i: (i, 0))]
    operands = [pieces3, w3, x]
    if after is not None:
        in_specs.append(ANY)
        operands.append(after)
    return pl.pallas_call(
        body, grid=(T // tm,), name=name,
        in_specs=in_specs,
        out_specs=pl.BlockSpec((tm, D), lambda i: (i, 0)),
        out_shape=jax.ShapeDtypeStruct((T, D), F32),
        compiler_params=_params(("parallel",)),
    )(*operands)


def _grad_through_weight(dy, w3, pieces, width, out_dtype, tm, chunk, name, after=None):
    T, D = dy.shape

    def body(dy_ref, w_ref, *rest):
        dyb_ref, o_ref = rest[-2:]
        dyb = dy_ref[...].astype(BF16)
        dyb_ref[...] = dyb
        for p in range(pieces):
            for c0 in range(0, width, chunk):
                r0 = p * width + c0
                o_ref[p, :, c0:c0 + chunk] = _nt(dyb, w_ref[r0:r0 + chunk, :]).astype(out_dtype)

    in_specs = [pl.BlockSpec((tm, D), lambda i: (i, 0)),
                pl.BlockSpec((None, pieces * width, D), lambda i: (0, 0, 0))]
    operands = [dy, w3]
    if after is not None:
        in_specs.append(ANY)
        operands.append(after)
    return pl.pallas_call(
        body, grid=(T // tm,), name=name,
        in_specs=in_specs,
        out_specs=[pl.BlockSpec((tm, D), lambda i: (i, 0)),
                   pl.BlockSpec((pieces, tm, width), lambda i: (0, i, 0))],
        out_shape=[jax.ShapeDtypeStruct((T, D), BF16),
                   jax.ShapeDtypeStruct((pieces, T, width), out_dtype)],
        compiler_params=_params(("parallel",)),
    )(*operands)


def _grad_through_proj_norm(dp3, w3, x, g, dx_in, tm, name):
    P, T, C = dp3.shape
    D = w3.shape[1]

    def body(dp_ref, w_ref, x_ref, g_ref, dxin_ref, dx_ref, dg_ref):
        dh = _nt(dp_ref[0], w_ref[:, 0:C])
        for p in range(1, P):
            dh = dh + _nt(dp_ref[p], w_ref[:, p * C:(p + 1) * C])
        xv = x_ref[...]
        rstd = lax.rsqrt(jnp.mean(xv * xv, axis=-1, keepdims=True) + EPS)
        xn = xv * rstd
        a = dh * g_ref[...]
        dx_ref[...] = dxin_ref[...] + rstd * (a - xn * jnp.mean(a * xn, axis=-1, keepdims=True))
        part = jnp.sum(dh * xn, axis=0, keepdims=True)

        @pl.when(pl.program_id(0) == 0)
        def _():
            dg_ref[...] = part

        @pl.when(pl.program_id(0) != 0)
        def _():
            dg_ref[...] += part

    return pl.pallas_call(
        body, grid=(T // tm,), name=name,
        in_specs=[pl.BlockSpec((P, tm, C), lambda i: (0, i, 0)),
                  pl.BlockSpec((None, D, P * C), lambda i: (0, 0, 0)),
                  pl.BlockSpec((tm, D), lambda i: (i, 0)),
                  pl.BlockSpec((1, D), lambda i: (0, 0)),
                  pl.BlockSpec((tm, D), lambda i: (i, 0))],
        out_specs=[pl.BlockSpec((tm, D), lambda i: (i, 0)),
                   pl.BlockSpec((1, D), lambda i: (0, 0))],
        out_shape=[jax.ShapeDtypeStruct((T, D), F32), jax.ShapeDtypeStruct((1, D), F32)],
        compiler_params=_params(("arbitrary",), VMEM_LIMIT_RESIDENT_WEIGHT_MIB),
    )(dp3, w3, x, g, dx_in)


def _weight_grad(a3, g3, ta, tg, name, a_transposed=False):
    PG, T, CG = g3.shape
    PA, CA = (a3.shape[0], a3.shape[1]) if a_transposed else (a3.shape[0], a3.shape[2])
    na, ng = CA // ta, CG // tg
    assert CA % ta == 0 and CG % tg == 0

    def body(a_ref, g_ref, o_ref):
        if a_transposed:
            part = jnp.dot(a_ref[...], g_ref[...], preferred_element_type=F32)
        else:
            part = _tn(a_ref[...], g_ref[...])
        o_ref[...] = part.astype(o_ref.dtype)

    a_spec = (pl.BlockSpec((None, ta, T), lambda i, j: (i // na, i % na, 0)) if a_transposed
              else pl.BlockSpec((None, T, ta), lambda i, j: (i // na, 0, i % na)))
    return pl.pallas_call(
        body, grid=(PA * na, PG * ng), name=name,
        in_specs=[a_spec, pl.BlockSpec((None, T, tg), lambda i, j: (j // ng, 0, j % ng))],
        out_specs=pl.BlockSpec((None, ta, tg), lambda i, j: (0, i, j)),
        out_shape=pltpu.HBM((1, PA * CA, PG * CG), BF16),
        compiler_params=_params(("parallel", "parallel"), VMEM_LIMIT_RESIDENT_WEIGHT_MIB),
    )(a3, g3)


def _bias_tables(bm_ref, lone_ref, pair, n_heads, S):
    ii = lax.broadcasted_iota(jnp.int32, (BAND, 2 * BAND), 0)
    jj = lax.broadcasted_iota(jnp.int32, (BAND, 2 * BAND), 1)
    dist = BAND + ii - jj
    valid = (dist >= 0) & (dist <= BAND)
    distf = dist.astype(F32)
    for hh in range(2):
        head = (2 * pair + hh + 1).astype(F32)
        slope = jnp.exp(jnp.full((1, 1), -8.0 / n_heads * math.log(2.0), F32) * head)
        for bi, d in enumerate(DILATIONS):
            table = jnp.where(valid, -(slope * d) * distf, NEG)
            bm_ref[bi, hh * BAND:(hh + 1) * BAND, :] = table
            if S // (BAND * d) == 1:
                lone_ref[bi, hh * BAND:(hh + 1) * BAND, :] = table[:, BAND:2 * BAND]


def _stack_heads(x, is_a):
    zero = jnp.zeros_like(x)
    return jnp.concatenate([jnp.where(is_a, x, zero), jnp.where(is_a, zero, x)], axis=0)


def _unstack_heads(x2, is_a):
    return jnp.where(is_a, x2[0:BAND], x2[BAND:2 * BAND])


def _gather_residues(dst_ref, src, d, S, convert):
    L = S // d
    for r in range(d):
        rows = pl.ds(r, L, stride=d) if d > 1 else slice(None)
        dst_ref[r * L:(r + 1) * L, :] = convert(src(rows))


def _block_rows(t, d, S):
    nb = S // (BAND * d)
    n = t % nb
    has_prev = jnp.minimum(n, 1)
    cur = pl.ds(pl.multiple_of(t * BAND, BAND), BAND)
    prev = pl.ds(pl.multiple_of((t - has_prev) * BAND, BAND), BAND)
    return cur, prev, has_prev


def _first_block_penalty(has_prev):
    jrow = lax.broadcasted_iota(jnp.int32, (1, 2 * BAND), 1)
    pen = jnp.where(has_prev == 0, NEG, 0.0).astype(F32)
    return jnp.where(jrow < BAND, pen, 0.0)


def _attn_fwd(qkv3, gain, mix_shape_pieces, S, n_heads, name):
    _, T, C = qkv3.shape
    B, P = T // S, C // LANES
    NBLK = S // BAND
    scale = GROUP ** -0.5
    nbr = len(DILATIONS)
    RC = ATTN_ROWS

    def body(qkv_ref, g_ref, o_ref, lse_ref, an_ref, qs, ks, vs, op, mp, lp, ob, mb, lb, bm, bml):
        pair = pl.program_id(1)
        is_a = _lane_is_a()
        _bias_tables(bm, bml, pair, n_heads, S)

        for bi, d in enumerate(DILATIONS):
            nb = S // (BAND * d)
            _gather_residues(qs, lambda rows: qkv_ref.at[0][rows, :], d, S, lambda v: (v * scale).astype(BF16))
            _gather_residues(ks, lambda rows: qkv_ref.at[1][rows, :], d, S, lambda v: v.astype(BF16))
            _gather_residues(vs, lambda rows: qkv_ref.at[2][rows, :], d, S, lambda v: v.astype(BF16))
            o_dst, m_dst, l_dst = (ob.at[bi], mb.at[bi], lb.at[bi]) if d == 1 else (op, mp, lp)

            def block(t, carry, bi=bi, d=d, nb=nb, o_dst=o_dst, m_dst=m_dst, l_dst=l_dst):
                cur, prev, has_prev = _block_rows(t, d, S)
                q2 = _stack_heads(qs[cur, :], is_a)
                if nb == 1:
                    kc, vc = ks[cur, :], vs[cur, :]
                    s = _nt(q2, kc) + bml[bi]
                else:
                    kc = jnp.concatenate([ks[prev, :], ks[cur, :]], axis=0)
                    vc = jnp.concatenate([vs[prev, :], vs[cur, :]], axis=0)
                    s = _nt(q2, kc) + bm[bi] + _first_block_penalty(has_prev)
                m = jnp.max(s, axis=-1, keepdims=True)
                e = jnp.exp(s - m)
                l = jnp.sum(e, axis=-1, keepdims=True)
                pv = jnp.dot(e.astype(BF16), vc, preferred_element_type=F32)
                o_dst[cur, :] = _unstack_heads(pv, is_a)
                m_dst[cur, :] = _unstack_heads(m, is_a)
                l_dst[cur, :] = _unstack_heads(l, is_a)
                return carry

            lax.fori_loop(0, NBLK, block, 0, unroll=ATTN_UNROLL)
            if d > 1:
                L = S // d
                for r in range(d):
                    rows = pl.ds(r, L, stride=d)
                    ob.at[bi][rows, :] = op[r * L:(r + 1) * L, :]
                    mb.at[bi][rows, :] = mp[r * L:(r + 1) * L, :]
                    lb.at[bi][rows, :] = lp[r * L:(r + 1) * L, :]

        def finish(ci, carry):
            rs = pl.ds(pl.multiple_of(ci * RC, RC), RC)
            ms = [mb[bi, rs, :] for bi in range(nbr)]
            mmax = functools.reduce(jnp.maximum, ms)
            ws = [jnp.exp(m - mmax) for m in ms]
            num = sum(ob[bi, rs, :] * ws[bi] for bi in range(nbr))
            den = sum(lb[bi, rs, :] * ws[bi] for bi in range(nbr))
            o = num / den
            o_ref[rs, :] = o
            lse_ref[rs, :] = mmax + jnp.log(den)
            rstd = lax.rsqrt(_seg_sum(o * o, is_a) * (1.0 / GROUP) + EPS)
            an_ref[rs, :] = ((o * rstd) * g_ref[...]).astype(BF16)
            return carry

        lax.fori_loop(0, S // RC, finish, 0, unroll=True)

    seq = pl.BlockSpec((S, LANES), lambda b, p: (b, p))
    return pl.pallas_call(
        body, grid=(B, P), name=name,
        in_specs=[pl.BlockSpec((3, S, LANES), lambda b, p: (0, b, p)),
                  pl.BlockSpec((1, LANES), lambda b, p: (0, p))],
        out_specs=[seq, seq, pl.BlockSpec((None, S, LANES), lambda b, p: (0, b, p))],
        out_shape=[jax.ShapeDtypeStruct((T, C), F32), jax.ShapeDtypeStruct((T, C), F32),
                   jax.ShapeDtypeStruct((mix_shape_pieces, T, C), BF16)],
        scratch_shapes=[pltpu.VMEM((S, LANES), BF16)] * 3 + [pltpu.VMEM((S, LANES), F32)] * 3
        + [pltpu.VMEM((nbr, S, LANES), F32)] * 3
        + [pltpu.VMEM((nbr, 2 * BAND, 2 * BAND), F32), pltpu.VMEM((nbr, 2 * BAND, BAND), F32)],
        compiler_params=_params(("parallel", "parallel")),
    )(qkv3, gain)


def _attn_bwd(qkv3, o, lse, dmix3, gain, dproj_pieces, S, n_heads, name):
    _, T, C = qkv3.shape
    B, P = T // S, C // LANES
    NBLK = S // BAND
    scale = GROUP ** -0.5
    nbr = len(DILATIONS)
    RC = ATTN_ROWS

    def body(qkv_ref, o_ref, lse_ref, dn_ref, g_ref, dqkv_ref, dg_ref,
             do_n, dd_n, qs, ks, vs, dos, lses, dds, dqp, dkp, dvp, dqn, dkn, dvn, bm, bml):
        pair = pl.program_id(0)
        b = pl.program_id(1)
        is_a = _lane_is_a()
        _bias_tables(bm, bml, pair, n_heads, S)

        def prologue(ci, dg_acc):
            rs = pl.ds(pl.multiple_of(ci * RC, RC), RC)
            ov = o_ref[rs, :]
            dn = dn_ref[rs, :]
            rstd = lax.rsqrt(_seg_sum(ov * ov, is_a) * (1.0 / GROUP) + EPS)
            on = ov * rstd
            a = dn * g_ref[...]
            s_a = _seg_sum(a * on, is_a)
            do_n[rs, :] = rstd * (a - on * (s_a * (1.0 / GROUP)))
            dd_n[rs, :] = (EPS * s_a) * (rstd * rstd)
            zero = jnp.zeros((RC, LANES), F32)
            dqn[rs, :] = zero
            dkn[rs, :] = zero
            dvn[rs, :] = zero
            return dg_acc + jnp.sum(dn * on, axis=0, keepdims=True)

        dg_part = lax.fori_loop(0, S // RC, prologue, jnp.zeros((1, LANES), F32), unroll=True)

        @pl.when(b == 0)
        def _():
            dg_ref[...] = dg_part

        @pl.when(b != 0)
        def _():
            dg_ref[...] += dg_part

        for bi, d in enumerate(DILATIONS):
            nb = S // (BAND * d)
            L = S // d
            _gather_residues(qs, lambda rows: qkv_ref.at[0][rows, :], d, S, lambda v: (v * scale).astype(BF16))
            _gather_residues(ks, lambda rows: qkv_ref.at[1][rows, :], d, S, lambda v: v.astype(BF16))
            _gather_residues(vs, lambda rows: qkv_ref.at[2][rows, :], d, S, lambda v: v.astype(BF16))
            _gather_residues(dos, lambda rows: do_n[rows, :], d, S, lambda v: v.astype(BF16))
            if d == 1:
                lse_src, dd_src, dq_dst, dk_dst, dv_dst = lse_ref, dd_n, dqn, dkn, dvn
            else:
                _gather_residues(lses, lambda rows: lse_ref[rows, :], d, S, lambda v: v)
                _gather_residues(dds, lambda rows: dd_n[rows, :], d, S, lambda v: v)
                dkp[...] = jnp.zeros((S, LANES), F32)
                dvp[...] = jnp.zeros((S, LANES), F32)
                lse_src, dd_src, dq_dst, dk_dst, dv_dst = lses, dds, dqp, dkp, dvp

            def block(t, carry, bi=bi, d=d, nb=nb, lse_src=lse_src, dd_src=dd_src, dq_dst=dq_dst, dk_dst=dk_dst,
                      dv_dst=dv_dst):
                cur, prev, has_prev = _block_rows(t, d, S)
                q2 = _stack_heads(qs[cur, :], is_a)
                do2 = _stack_heads(dos[cur, :], is_a)
                lse_t = lse_src[cur, :]
                dd_t = dd_src[cur, :]
                lse2 = jnp.concatenate([lse_t[:, 0:1], lse_t[:, GROUP:GROUP + 1]], axis=0)
                dd2 = jnp.concatenate([dd_t[:, 0:1], dd_t[:, GROUP:GROUP + 1]], axis=0)
                if nb == 1:
                    kc, vc = ks[cur, :], vs[cur, :]
                    s = _nt(q2, kc) + bml[bi]
                else:
                    kc = jnp.concatenate([ks[prev, :], ks[cur, :]], axis=0)
                    vc = jnp.concatenate([vs[prev, :], vs[cur, :]], axis=0)
                    s = _nt(q2, kc) + bm[bi] + _first_block_penalty(has_prev)
                p = jnp.exp(s - lse2)
                ds = (p * (_nt(do2, vc) - dd2)).astype(BF16)
                dq = _unstack_heads(jnp.dot(ds, kc, preferred_element_type=F32), is_a)
                dk = _tn(ds, q2)
                dv = _tn(p.astype(BF16), do2)
                dq_dst[cur, :] = dq
                if nb == 1:
                    dk_dst[cur, :] += dk
                    dv_dst[cur, :] += dv
                else:
                    dk_dst[prev, :] += dk[0:BAND, :]
                    dv_dst[prev, :] += dv[0:BAND, :]
                    dk_dst[cur, :] += dk[BAND:2 * BAND, :]
                    dv_dst[cur, :] += dv[BAND:2 * BAND, :]
                return carry

            lax.fori_loop(0, NBLK, block, 0, unroll=ATTN_UNROLL)
            if d > 1:
                for r in range(d):
                    rows = pl.ds(r, L, stride=d)
                    dqn[rows, :] += dqp[r * L:(r + 1) * L, :]
                    dkn[rows, :] += dkp[r * L:(r + 1) * L, :]
                    dvn[rows, :] += dvp[r * L:(r + 1) * L, :]

        dqkv_ref[0] = (dqn[...] * scale).astype(BF16)
        dqkv_ref[1] = dkn[...].astype(BF16)
        dqkv_ref[2] = dvn[...].astype(BF16)

    seq = pl.BlockSpec((S, LANES), lambda p, b: (b, p))
    f32_seq = pltpu.VMEM((S, LANES), F32)
    bf_seq = pltpu.VMEM((S, LANES), BF16)
    return pl.pallas_call(
        body, grid=(P, B), name=name,
        in_specs=[pl.BlockSpec((3, S, LANES), lambda p, b: (0, b, p)), seq, seq,
                  pl.BlockSpec((None, S, LANES), lambda p, b: (0, b, p)),
                  pl.BlockSpec((1, LANES), lambda p, b: (0, p))],
        out_specs=[pl.BlockSpec((3, S, LANES), lambda p, b: (0, b, p)),
                   pl.BlockSpec((1, LANES), lambda p, b: (0, p))],
        out_shape=[jax.ShapeDtypeStruct((dproj_pieces, T, C), BF16), jax.ShapeDtypeStruct((1, C), F32)],
        scratch_shapes=[f32_seq, f32_seq, bf_seq, bf_seq, bf_seq, bf_seq, f32_seq, f32_seq,
                        f32_seq, f32_seq, f32_seq, f32_seq, f32_seq, f32_seq,
                        pltpu.VMEM((nbr, 2 * BAND, 2 * BAND), F32), pltpu.VMEM((nbr, 2 * BAND, BAND), F32)],
        compiler_params=_params(("parallel", "arbitrary")),
    )(qkv3, o, lse, dmix3, gain)


def _delay(x, k, row):
    return jnp.where(row >= k, pltpu.roll(x, k, 0), 0.0)


def _advance(x, k, row, S):
    return jnp.where(row < S - k, pltpu.roll(x, S - k, 0), 0.0)


def _conv3(x, w, row):
    return (w[0:1, :] * _delay(x, 2, row) + w[1:2, :] * _delay(x, 1, row)) + w[2:3, :] * x


HALO = 8


CONV_ROWS = 128
FFN_LANES = 128


def _zero_halo(pad_ref, S):
    zeros = jnp.zeros((HALO, pad_ref.shape[1]), pad_ref.dtype)
    pad_ref[0:HALO, :] = zeros
    pad_ref[HALO + S:2 * HALO + S, :] = zeros


def _window_at(pad_ref, r0, shift):
    return pad_ref[HALO + r0 + shift:HALO + r0 + shift + CONV_ROWS, :]


def _conv3_at(pad_ref, w, r0):
    return ((w[0:1, :] * _window_at(pad_ref, r0, -2) + w[1:2, :] * _window_at(pad_ref, r0, -1))
            + w[2:3, :] * _window_at(pad_ref, r0, 0))


def _conv3_grads_at(dz_ref, x_ref, w, r0):
    dz, dz1, dz2 = (_window_at(dz_ref, r0, k) for k in range(3))
    x = _window_at(x_ref, r0, 0)
    dx = (w[2:3, :] * dz + w[1:2, :] * dz1) + w[0:1, :] * dz2
    parts = [jnp.sum((d * x).reshape(CONV_ROWS // 8, 8, x.shape[1]), axis=0) for d in (dz2, dz1, dz)]
    return dx, parts


def _conv3_grads(dz, x, w, row, S):
    dz1 = _advance(dz, 1, row, S)
    dz2 = _advance(dz, 2, row, S)
    dx = (w[2:3, :] * dz + w[1:2, :] * dz1) + w[0:1, :] * dz2
    dw = jnp.concatenate([jnp.sum(dz2 * x, axis=0, keepdims=True),
                          jnp.sum(dz1 * x, axis=0, keepdims=True),
                          jnp.sum(dz * x, axis=0, keepdims=True)], axis=0)
    return dx, dw


def _mix_conv_fwd(cv3, taps, gain, mix, S, name, after=None):
    _, T, C = cv3.shape
    B, P = T // S, C // LANES

    def body(cv_ref, w_ref, g_ref, mix_hbm, *rest):
        y_ref, pad_c = rest[-2:]
        del mix_hbm
        is_a = _lane_is_a()
        _zero_halo(pad_c, S)
        pad_c[HALO:HALO + S, :] = cv_ref[1].astype(F32) * cv_ref[2].astype(F32)
        w = w_ref[...]
        for r0 in range(0, S, CONV_ROWS):
            y = cv_ref[0, r0:r0 + CONV_ROWS, :].astype(F32) * _conv3_at(pad_c, w, r0)
            rstd = lax.rsqrt(_seg_sum(y * y, is_a) * (1.0 / GROUP) + EPS)
            y_ref[r0:r0 + CONV_ROWS, :] = ((y * rstd) * g_ref[...]).astype(BF16)

    in_specs = [pl.BlockSpec((3, S, LANES), lambda b, p: (0, b, p)),
                pl.BlockSpec((3, LANES), lambda b, p: (0, p)),
                pl.BlockSpec((1, LANES), lambda b, p: (0, p)),
                ANY]
    operands = [cv3, taps, gain, mix]
    if after is not None:
        in_specs.append(ANY)
        operands.append(after)
    return pl.pallas_call(
        body, grid=(B, P), name=name,
        in_specs=in_specs,
        out_specs=pl.BlockSpec((None, S, LANES), lambda b, p: (1, b, p)),
        out_shape=jax.ShapeDtypeStruct(mix.shape, mix.dtype),
        scratch_shapes=[pltpu.VMEM((S + 2 * HALO, LANES), F32)],
        input_output_aliases={3: 0},
        compiler_params=_params(("parallel", "parallel")),
    )(*operands)


def _mix_conv_bwd(cv3, dmix3, taps, gain, dproj, S, name):
    _, T, C = cv3.shape
    B, P = T // S, C // LANES

    def body(cv_ref, dn_ref, w_ref, g_ref, dproj_hbm, dcv_ref, dw_ref, dg_ref):
        del dproj_hbm
        b = pl.program_id(1)
        row = lax.broadcasted_iota(jnp.int32, (S, 1), 0)
        is_a = _lane_is_a()
        w = w_ref[...]
        gb = cv_ref[0].astype(F32)
        gc = cv_ref[1].astype(F32)
        u = cv_ref[2].astype(F32)
        c = gc * u
        z = _conv3(c, w, row)
        y = gb * z
        rstd = lax.rsqrt(_seg_sum(y * y, is_a) * (1.0 / GROUP) + EPS)
        yn = y * rstd
        dn = dn_ref[...]
        a = dn * g_ref[...]
        dy = rstd * (a - yn * (_seg_sum(a * yn, is_a) * (1.0 / GROUP)))
        dg = jnp.sum(dn * yn, axis=0, keepdims=True)
        dc, dw = _conv3_grads(dy * gb, c, w, row, S)
        dcv_ref[0] = (dy * z).astype(BF16)
        dcv_ref[1] = (dc * u).astype(BF16)
        dcv_ref[2] = (dc * gc).astype(BF16)

        @pl.when(b == 0)
        def _():
            dw_ref[...] = dw
            dg_ref[...] = dg

        @pl.when(b != 0)
        def _():
            dw_ref[...] += dw
            dg_ref[...] += dg

    return pl.pallas_call(
        body, grid=(P, B), name=name,
        in_specs=[pl.BlockSpec((3, S, LANES), lambda p, b: (0, b, p)),
                  pl.BlockSpec((None, S, LANES), lambda p, b: (1, b, p)),
                  pl.BlockSpec((3, LANES), lambda p, b: (0, p)),
                  pl.BlockSpec((1, LANES), lambda p, b: (0, p)),
                  pl.BlockSpec(memory_space=pl.ANY)],
        out_specs=[pl.BlockSpec((3, S, LANES), lambda p, b: (1, b, p)),
                   pl.BlockSpec((3, LANES), lambda p, b: (0, p)),
                   pl.BlockSpec((1, LANES), lambda p, b: (0, p))],
        out_shape=[jax.ShapeDtypeStruct(dproj.shape, dproj.dtype),
                   jax.ShapeDtypeStruct((3, C), F32), jax.ShapeDtypeStruct((1, C), F32)],
        input_output_aliases={4: 0},
        compiler_params=_params(("parallel", "arbitrary")),
    )(cv3, dmix3, taps, gain, dproj)


def _sigmoid(x):
    return 0.5 * jnp.tanh(0.5 * x) + 0.5


def _ffn_act_fwd(up3, taps, S, name):
    _, T, Fd = up3.shape
    W = FFN_LANES
    B, P = T // S, Fd // W

    def body(up_ref, wg_ref, wv_ref, act_ref, pad_g, pad_v):
        _zero_halo(pad_g, S)
        _zero_halo(pad_v, S)
        pad_g[HALO:HALO + S, :] = up_ref[0].astype(F32)
        pad_v[HALO:HALO + S, :] = up_ref[1].astype(F32)
        wg = wg_ref[...]
        wv = wv_ref[...]
        for r0 in range(0, S, CONV_ROWS):
            cg = _conv3_at(pad_g, wg, r0)
            cv = _conv3_at(pad_v, wv, r0)
            act_ref[r0:r0 + CONV_ROWS, :] = ((cg * _sigmoid(cg)) * cv).astype(BF16)

    return pl.pallas_call(
        body, grid=(B, P), name=name,
        in_specs=[pl.BlockSpec((2, S, W), lambda b, p: (0, b, p)),
                  pl.BlockSpec((3, W), lambda b, p: (0, p)),
                  pl.BlockSpec((3, W), lambda b, p: (0, P + p))],
        out_specs=pl.BlockSpec((S, W), lambda b, p: (b, p)),
        out_shape=jax.ShapeDtypeStruct((T, Fd), BF16),
        scratch_shapes=[pltpu.VMEM((S + 2 * HALO, W), F32)] * 2,
        compiler_params=_params(("parallel", "parallel")),
    )(up3, taps, taps)


def _ffn_act_bwd(up3, dact3, taps, S, name):
    _, T, Fd = up3.shape
    W = FFN_LANES
    B, P = T // S, Fd // W

    def body(up_ref, da_ref, wg_ref, wv_ref, dup_ref, dwg_ref, dwv_ref, pad_ug, pad_uv, pad_dg, pad_dv):
        b = pl.program_id(1)
        for pad in (pad_ug, pad_uv, pad_dg, pad_dv):
            _zero_halo(pad, S)
        pad_ug[HALO:HALO + S, :] = up_ref[0].astype(F32)
        pad_uv[HALO:HALO + S, :] = up_ref[1].astype(F32)
        wg = wg_ref[...]
        wv = wv_ref[...]
        for r0 in range(0, S, CONV_ROWS):
            cg = _conv3_at(pad_ug, wg, r0)
            cv = _conv3_at(pad_uv, wv, r0)
            sg = _sigmoid(cg)
            da = da_ref[r0:r0 + CONV_ROWS, :].astype(F32)
            t = da * sg
            dcv = cg * t
            pad_dg[HALO + r0:HALO + r0 + CONV_ROWS, :] = cv * ((t + dcv) - dcv * sg)
            pad_dv[HALO + r0:HALO + r0 + CONV_ROWS, :] = dcv
        sums_g = [jnp.zeros((8, W), F32)] * 3
        sums_v = [jnp.zeros((8, W), F32)] * 3
        for r0 in range(0, S, CONV_ROWS):
            dug, parts_g = _conv3_grads_at(pad_dg, pad_ug, wg, r0)
            duv, parts_v = _conv3_grads_at(pad_dv, pad_uv, wv, r0)
            dup_ref[0, r0:r0 + CONV_ROWS, :] = dug.astype(BF16)
            dup_ref[1, r0:r0 + CONV_ROWS, :] = duv.astype(BF16)
            sums_g = [a + p for a, p in zip(sums_g, parts_g)]
            sums_v = [a + p for a, p in zip(sums_v, parts_v)]
        dwg = jnp.concatenate([jnp.sum(a, axis=0, keepdims=True) for a in sums_g], axis=0)
        dwv = jnp.concatenate([jnp.sum(a, axis=0, keepdims=True) for a in sums_v], axis=0)

        @pl.when(b == 0)
        def _():
            dwg_ref[...] = dwg
            dwv_ref[...] = dwv

        @pl.when(b != 0)
        def _():
            dwg_ref[...] += dwg
            dwv_ref[...] += dwv

    tap_out = pl.BlockSpec((3, W), lambda p, b: (0, p))
    return pl.pallas_call(
        body, grid=(P, B), name=name,
        in_specs=[pl.BlockSpec((2, S, W), lambda p, b: (0, b, p)),
                  pl.BlockSpec((None, S, W), lambda p, b: (0, b, p)),
                  pl.BlockSpec((3, W), lambda p, b: (0, p)),
                  pl.BlockSpec((3, W), lambda p, b: (0, P + p))],
        out_specs=[pl.BlockSpec((2, S, W), lambda p, b: (0, b, p)), tap_out, tap_out],
        out_shape=[jax.ShapeDtypeStruct((2, T, Fd), BF16),
                   jax.ShapeDtypeStruct((3, Fd), F32), jax.ShapeDtypeStruct((3, Fd), F32)],
        scratch_shapes=[pltpu.VMEM((S + 2 * HALO, W), F32)] * 4,
        compiler_params=_params(("parallel", "arbitrary")),
    )(up3, dact3, taps, taps)


def _final_norm_loss(x, g, target, tm, name):
    T, D = x.shape

    def body(x_ref, g_ref, t_ref, dx_ref, dg_ref, loss_ref):
        xv = x_ref[...]
        rstd = lax.rsqrt(jnp.mean(xv * xv, axis=-1, keepdims=True) + EPS)
        xn = xv * rstd
        err = xn * g_ref[...] - t_ref[...]
        part = 0.5 * jnp.sum(jnp.mean(err * err, axis=-1, keepdims=True), axis=0, keepdims=True)
        dy = err * (1.0 / D)
        a = dy * g_ref[...]
        dx_ref[...] = rstd * (a - xn * jnp.mean(a * xn, axis=-1, keepdims=True))
        dg = jnp.sum(dy * xn, axis=0, keepdims=True)
        lpart = jnp.broadcast_to(part, (1, LANES))

        @pl.when(pl.program_id(0) == 0)
        def _():
            dg_ref[...] = dg
            loss_ref[...] = lpart

        @pl.when(pl.program_id(0) != 0)
        def _():
            dg_ref[...] += dg
            loss_ref[...] += lpart

    row = pl.BlockSpec((tm, D), lambda i: (i, 0))
    return pl.pallas_call(
        body, grid=(T // tm,), name=name,
        in_specs=[row, pl.BlockSpec((1, D), lambda i: (0, 0)), row],
        out_specs=[row, pl.BlockSpec((1, D), lambda i: (0, 0)), pl.BlockSpec((1, LANES), lambda i: (0, 0))],
        out_shape=[jax.ShapeDtypeStruct((T, D), F32), jax.ShapeDtypeStruct((1, D), F32),
                   jax.ShapeDtypeStruct((1, LANES), F32)],
        compiler_params=_params(("arbitrary",)),
    )(x, g, target)


def _row_tile(rows, cols, budget_elems=512 * 1024):
    tr = rows
    while tr * cols > budget_elems and tr % 32 == 0:
        tr //= 2
    return tr


def _prefetch_call(body, grid, in_specs, out_specs, out_shape, name, sem, aliases=None):
    return pl.pallas_call(
        body, name=name, out_shape=out_shape,
        grid_spec=pltpu.PrefetchScalarGridSpec(num_scalar_prefetch=1, grid=grid, in_specs=in_specs,
                                               out_specs=out_specs),
        input_output_aliases=aliases or {},
        compiler_params=_params(sem))


def _cast_into_full(w, layer, colwise, where, name):
    _, K, N = w.shape
    tr = _row_tile(K, N)
    nrb = K // tr
    full_shape = (1, K, 4 * N) if colwise else (1, 4 * K, N)

    def body(where_ref, w_ref, o_ref):
        del where_ref
        o_ref[...] = w_ref[...].astype(BF16)

    if colwise:
        out_map = lambda i, wh: (0, i, wh[0])
    else:
        out_map = lambda i, wh: (0, wh[0] * nrb + i, 0)
    return _prefetch_call(
        body, (nrb,), [pl.BlockSpec((None, tr, N), lambda i, wh: (layer, i, 0))],
        pl.BlockSpec((None, tr, N), out_map), pltpu.HBM(full_shape, BF16), name,
        ("parallel",))(where, w)


def _chip_sum(g3, other, colwise, where, name):
    L, K, N = g3.shape
    hk, hn = (K // 2, N) if colwise else (K, N // 2)
    tr = _row_tile(hk, hn)
    nrb = hk // tr

    def body(where_ref, g_ref, o_ref, s_ref):
        del where_ref
        s_ref[...] = (g_ref[...].astype(F32) + o_ref[...].astype(F32)).astype(BF16)

    if colwise:
        g_map = lambda l, i, wh: (l, wh[1] * nrb + i, 0)
    else:
        g_map = lambda l, i, wh: (l, i, wh[1])
    blk = pl.BlockSpec((None, tr, hn), lambda l, i, wh: (l, i, 0))
    return _prefetch_call(
        body, (L, nrb), [pl.BlockSpec((None, tr, hn), g_map), blk], blk,
        pltpu.HBM((L, hk, hn), BF16), name, ("parallel", "parallel"))(where, _in_hbm(g3), _in_hbm(other))


def _owner_sum(chip_sum, received, colwise, where, layer, n_layers, prev, name):
    _, hk, hn = chip_sum.shape
    pk, pn = (hk, hn // 4) if colwise else (hk // 4, hn)
    tr = _row_tile(pk, pn)
    nrb = pk // tr
    shard_shape = (n_layers, 2 * pk, pn) if colwise else (n_layers, pk, 2 * pn)

    def body(where_ref, own_ref, rec_ref, *rest):
        del where_ref
        o_ref = rest[-1]
        acc = own_ref[...].astype(F32)
        for j in range(3):
            acc = acc + rec_ref[j].astype(F32)
        o_ref[...] = acc

    if colwise:
        own_map = lambda i, wh: (0, i, wh[0])
        out_map = lambda i, wh: (layer, wh[1] * nrb + i, 0)
    else:
        own_map = lambda i, wh: (0, wh[0] * nrb + i, 0)
        out_map = lambda i, wh: (layer, i, wh[1])
    in_specs = [pl.BlockSpec((None, tr, pn), own_map),
                pl.BlockSpec((3, None, tr, pn), lambda i, wh: (0, 0, i, 0))]
    operands = [where, _in_hbm(chip_sum), _in_hbm(received)]
    if prev is not None:
        in_specs.append(ANY)
        operands.append(prev)
    return _prefetch_call(
        body, (nrb,), in_specs, pl.BlockSpec((None, tr, pn), out_map), pltpu.HBM(shard_shape, F32), name,
        ("parallel",), None if prev is None else {3: 0})(*operands)


def _adamw(w, g, m, v, name):
    R, Cc = w.shape
    tr = _row_tile(R, Cc, 256 * 1024)

    def body(w_ref, g_ref, m_ref, v_ref, d_ref, nm_ref, nv_ref, go_ref):
        gv = g_ref[...]
        go_ref[...] = gv
        nm = ADAM_B1 * m_ref[...] + (1.0 - ADAM_B1) * gv
        nv = ADAM_B2 * v_ref[...] + (1.0 - ADAM_B2) * (gv * gv)
        m_hat = nm / (1.0 - ADAM_B1 ** ADAM_STEP)
        v_hat = nv / (1.0 - ADAM_B2 ** ADAM_STEP)
        d_ref[...] = -ADAM_LR * (m_hat / (jnp.sqrt(v_hat) + ADAM_EPS) + ADAM_WD * w_ref[...])
        nm_ref[...] = nm
        nv_ref[...] = nv

    blk = pl.BlockSpec((tr, Cc), lambda i: (i, 0))
    shp = jax.ShapeDtypeStruct((R, Cc), F32)
    return pl.pallas_call(
        body, grid=(R // tr,), name=name,
        in_specs=[blk] * 4, out_specs=[blk] * 4, out_shape=[shp] * 4,
        compiler_params=_params(("parallel",)),
    )(w, g, m, v)


COL_SHARDED = (True, False, True, False)


def _position():
    x, y, c = lax.axis_index("x"), lax.axis_index("y"), lax.axis_index("c")
    chips = [(1 - x, y), (x, 1 - y), (1 - x, 1 - y)]
    return x, y, c, chips


def _span(index, size, align):
    return pl.ds(pl.multiple_of(index * size, align), size)


def _window(ref, colwise, shard, half, shards=4):
    _, K, N = ref.shape
    rows = cols = slice(None)
    if colwise:
        if half is not None:
            rows = _span(half, K // 2, 16)
        if shard is not None:
            cols = _span(shard, N // shards, LANES)
    else:
        if shard is not None:
            rows = _span(shard, K // shards, 16)
        if half is not None:
            cols = _span(half, N // 2, LANES)
    return ref.at[:, rows, cols]


HBM = pl.BlockSpec(memory_space=pltpu.HBM)
SEMAPHORES = pl.BlockSpec(memory_space=pltpu.SEMAPHORE)


def _gather_start(fulls, colwise, group_sizes, after, name):
    n = len(fulls)
    n_groups = len(group_sizes)

    n_in = n if after is None else n + 1

    def body(*refs):
        ins = refs[:n]
        sems = refs[n_in:n_in + 2 * n_groups]
        x, y, c, chips = _position()
        me = 2 * x + y
        i = 0
        for g, size in enumerate(group_sizes):
            for a in range(size):
                win = _window(ins[i], colwise[i], me, c)
                for j, chip in enumerate(chips):
                    pltpu.make_async_remote_copy(
                        src_ref=win, dst_ref=win, send_sem=sems[2 * g].at[a * 3 + j],
                        recv_sem=sems[2 * g + 1].at[a * 3 + j],
                        device_id=(chip[0], chip[1], c), device_id_type=MESH_ID).start()
                i += 1

    sem_shapes = []
    for size in group_sizes:
        sem_shapes += [pltpu.SemaphoreType.DMA((3 * size,)), pltpu.SemaphoreType.DMA((3 * size,))]
    operands = [pltpu.with_memory_space_constraint(f, pltpu.HBM) for f in fulls]
    in_specs = [HBM] * n
    if after is not None:
        operands.append(after)
        in_specs.append(ANY)
    outs = pl.pallas_call(
        body, name=name,
        in_specs=in_specs, out_specs=[SEMAPHORES] * (2 * n_groups) + [HBM] * n,
        out_shape=sem_shapes + [pltpu.HBM(f.shape, f.dtype) for f in fulls],
        input_output_aliases={i: 2 * n_groups + i for i in range(n)},
        compiler_params=pltpu.CompilerParams(has_side_effects=pltpu.SideEffectType.DATAFLOW_SIDE_EFFECTING),
    )(*operands)
    sems = [(outs[2 * g], outs[2 * g + 1]) for g in range(n_groups)]
    return sems, list(outs[2 * n_groups:])


def _to_sibling(ref, colwise, chip, half, x, y, c, send_sem, recv_sem):
    win = _window(ref, colwise, 2 * chip[0] + chip[1], half)
    return pltpu.make_async_remote_copy(
        src_ref=win, dst_ref=win, send_sem=send_sem, recv_sem=recv_sem,
        device_id=(x, y, 1 - c), device_id_type=MESH_ID)


def _gather_pass(in_flight, colwise, sems, after, name):
    n = len(in_flight)

    def body(*refs):
        ins = refs[:n]
        send_sems, recv_sems = refs[n], refs[n + 1]
        pass_send, pass_recv = refs[-2 - n], refs[-1 - n]
        x, y, c, chips = _position()
        me = 2 * x + y
        for a in range(n):
            for j, chip in enumerate(chips):
                k = a * 3 + j
                pltpu.make_async_remote_copy(
                    src_ref=_window(ins[a], colwise[a], me, c),
                    dst_ref=_window(ins[a], colwise[a], 2 * chip[0] + chip[1], c),
                    send_sem=send_sems.at[k], recv_sem=recv_sems.at[k],
                    device_id=(chip[0], chip[1], c), device_id_type=MESH_ID).wait()
                _to_sibling(ins[a], colwise[a], chip, c, x, y, c, pass_send.at[k], pass_recv.at[k]).start()

    operands = list(in_flight) + list(sems)
    in_specs = [HBM] * n + [SEMAPHORES] * 2
    if after is not None:
        operands.append(after)
        in_specs.append(ANY)
    outs = pl.pallas_call(
        body, name=name,
        in_specs=in_specs, out_specs=[SEMAPHORES] * 2 + [HBM] * n,
        out_shape=[pltpu.SemaphoreType.DMA((3 * n,)), pltpu.SemaphoreType.DMA((3 * n,))]
        + [pltpu.HBM(f.shape, f.dtype) for f in in_flight],
        input_output_aliases={i: 2 + i for i in range(n)},
        compiler_params=pltpu.CompilerParams(has_side_effects=pltpu.SideEffectType.DATAFLOW_SIDE_EFFECTING),
    )(*operands)
    return (outs[0], outs[1]), list(outs[2:])


def _gather_wait(in_flight, colwise, sems, after, name):
    n = len(in_flight)

    def body(*refs):
        ins = refs[:n]
        send_sems, recv_sems = refs[n], refs[n + 1]
        x, y, c, chips = _position()
        for a in range(n):
            for j, chip in enumerate(chips):
                k = a * 3 + j
                _to_sibling(ins[a], colwise[a], chip, c, x, y, c, send_sems.at[k], recv_sems.at[k]).wait_send()
                _to_sibling(ins[a], colwise[a], chip, 1 - c, x, y, c, send_sems.at[k], recv_sems.at[k]).wait_recv()

    operands = list(in_flight) + list(sems)
    in_specs = [HBM] * n + [SEMAPHORES] * 2
    if after is not None:
        operands.append(after)
        in_specs.append(ANY)
    outs = pl.pallas_call(
        body, name=name,
        in_specs=in_specs, out_specs=[HBM] * n,
        out_shape=[pltpu.HBM(f.shape, f.dtype) for f in in_flight],
        input_output_aliases={i: i for i in range(n)},
        compiler_params=pltpu.CompilerParams(has_side_effects=pltpu.SideEffectType.DATAFLOW_SIDE_EFFECTING),
    )(*operands)
    return list(outs)


def _exchange_copy(g_ref, land_ref, colwise, x, y, c, send_sem, recv_sem):
    return pltpu.make_async_remote_copy(
        src_ref=_window(g_ref, colwise, None, 1 - c), dst_ref=land_ref, send_sem=send_sem, recv_sem=recv_sem,
        device_id=(x, y, 1 - c), device_id_type=MESH_ID)


def _exchange_start(grads, colwise, name):
    n = len(grads)
    lands = []
    for g, cw in zip(grads, colwise):
        L, K, N = g.shape
        lands.append(lax.empty((L, K // 2, N) if cw else (L, K, N // 2), g.dtype))

    def body(*refs):
        src, land = refs[:n], refs[n:2 * n]
        send_sems, recv_sems = refs[2 * n], refs[2 * n + 1]
        x, y, c, _ = _position()
        for i in range(n):
            _exchange_copy(src[i], land[i], colwise[i], x, y, c, send_sems.at[i], recv_sems.at[i]).start()

    arrays = list(grads) + lands
    outs = pl.pallas_call(
        body, name=name,
        in_specs=[HBM] * (2 * n), out_specs=[SEMAPHORES] * 2 + [HBM] * (2 * n),
        out_shape=[pltpu.SemaphoreType.DMA((n,)), pltpu.SemaphoreType.DMA((n,))]
        + [pltpu.HBM(a.shape, a.dtype) for a in arrays],
        input_output_aliases={i: 2 + i for i in range(2 * n)},
        compiler_params=pltpu.CompilerParams(has_side_effects=pltpu.SideEffectType.DATAFLOW_SIDE_EFFECTING),
    )(*[pltpu.with_memory_space_constraint(a, pltpu.HBM) for a in arrays])
    return (outs[0], outs[1]), list(outs[2:2 + n]), list(outs[2 + n:])


def _exchange_wait(grads, lands, colwise, sems, after, name):
    n = len(grads)

    def body(*refs):
        src, land = refs[:n], refs[n:2 * n]
        send_sems, recv_sems = refs[2 * n], refs[2 * n + 1]
        x, y, c, _ = _position()
        for i in range(n):
            _exchange_copy(src[i], land[i], colwise[i], x, y, c, send_sems.at[i], recv_sems.at[i]).wait()

    arrays = list(grads) + list(lands)
    operands = arrays + list(sems)
    in_specs = [HBM] * (2 * n) + [SEMAPHORES] * 2
    if after is not None:
        operands.append(after)
        in_specs.append(ANY)
    outs = pl.pallas_call(
        body, name=name,
        in_specs=in_specs, out_specs=[HBM] * (2 * n),
        out_shape=[pltpu.HBM(a.shape, a.dtype) for a in arrays],
        input_output_aliases={i: i for i in range(2 * n)},
        compiler_params=pltpu.CompilerParams(has_side_effects=pltpu.SideEffectType.DATAFLOW_SIDE_EFFECTING),
    )(*operands)
    return list(outs[:n]), list(outs[n:])


def _scatter_copy(src_ref, land_ref, colwise, j, chip, c, send_sem, recv_sem):
    return pltpu.make_async_remote_copy(
        src_ref=_window(src_ref, colwise, 2 * chip[0] + chip[1], None), dst_ref=land_ref.at[j],
        send_sem=send_sem, recv_sem=recv_sem, device_id=(chip[0], chip[1], c), device_id_type=MESH_ID)


def _scatter_start(chip_sums, colwise, name):
    n = len(chip_sums)
    lands = []
    for g, cw in zip(chip_sums, colwise):
        L, hk, hn = g.shape
        lands.append(lax.empty((3, L, hk, hn // 4) if cw else (3, L, hk // 4, hn), g.dtype))

    def body(*refs):
        src, land = refs[:n], refs[n:2 * n]
        send_sems, recv_sems = refs[2 * n], refs[2 * n + 1]
        x, y, c, chips = _position()
        for i in range(n):
            for j, chip in enumerate(chips):
                _scatter_copy(src[i], land[i], colwise[i], j, chip, c, send_sems.at[i * 3 + j],
                              recv_sems.at[i * 3 + j]).start()

    arrays = list(chip_sums) + lands
    outs = pl.pallas_call(
        body, name=name,
        in_specs=[HBM] * (2 * n), out_specs=[SEMAPHORES] * 2 + [HBM] * (2 * n),
        out_shape=[pltpu.SemaphoreType.DMA((3 * n,)), pltpu.SemaphoreType.DMA((3 * n,))]
        + [pltpu.HBM(a.shape, a.dtype) for a in arrays],
        input_output_aliases={i: 2 + i for i in range(2 * n)},
        compiler_params=pltpu.CompilerParams(has_side_effects=pltpu.SideEffectType.DATAFLOW_SIDE_EFFECTING),
    )(*[pltpu.with_memory_space_constraint(a, pltpu.HBM) for a in arrays])
    return (outs[0], outs[1]), list(outs[2:2 + n]), list(outs[2 + n:])


def _scatter_wait(sources, lands, colwise, sems, after, name):
    n = len(sources)

    def body(*refs):
        src, land = refs[:n], refs[n:2 * n]
        send_sems, recv_sems = refs[2 * n], refs[2 * n + 1]
        x, y, c, chips = _position()
        for i in range(n):
            for j, chip in enumerate(chips):
                cp = _scatter_copy(src[i], land[i], colwise[i], j, chip, c, send_sems.at[i * 3 + j],
                                   recv_sems.at[i * 3 + j])
                cp.wait_send()
                cp.wait_recv()

    arrays = list(sources) + list(lands)
    operands = arrays + list(sems)
    in_specs = [HBM] * (2 * n) + [SEMAPHORES] * 2
    if after is not None:
        operands.append(after)
        in_specs.append(ANY)
    outs = pl.pallas_call(
        body, name=name,
        in_specs=in_specs, out_specs=[HBM] * (2 * n),
        out_shape=[pltpu.HBM(a.shape, a.dtype) for a in arrays],
        input_output_aliases={i: i for i in range(2 * n)},
        compiler_params=pltpu.CompilerParams(has_side_effects=pltpu.SideEffectType.DATAFLOW_SIDE_EFFECTING),
    )(*operands)
    return list(outs[:n]), list(outs[n:])


def _share_with_sibling(shards, colwise, name):
    n = len(shards)

    def body(*refs):
        out = refs[n:2 * n]
        send_sems, recv_sems = refs[2 * n:]
        x, y, c, _ = _position()

        def copy(i, half):
            win = _window(out[i], colwise[i], None, half)
            return pltpu.make_async_remote_copy(
                src_ref=win, dst_ref=win, send_sem=send_sems.at[i], recv_sem=recv_sems.at[i],
                device_id=(x, y, 1 - c), device_id_type=MESH_ID)

        for i in range(n):
            copy(i, c).start()
        for i in range(n):
            copy(i, 1 - c).wait_recv()
        for i in range(n):
            copy(i, c).wait_send()

    return pl.pallas_call(
        body, name=name,
        in_specs=[ANY] * n, out_specs=[ANY] * n,
        out_shape=[jax.ShapeDtypeStruct(s.shape, s.dtype) for s in shards],
        input_output_aliases={i: i for i in range(n)},
        scratch_shapes=[pltpu.SemaphoreType.DMA((n,)), pltpu.SemaphoreType.DMA((n,))],
    )(*shards)


def _chip_exchange(buf, me, x, y, c, chips, send_sems, recv_sems):
    def copy(j, chip, slot):
        return pltpu.make_async_remote_copy(
            src_ref=buf.at[me], dst_ref=buf.at[slot], send_sem=send_sems.at[j], recv_sem=recv_sems.at[j],
            device_id=(chip[0], chip[1], c), device_id_type=MESH_ID)

    for j, chip in enumerate(chips):
        copy(j, chip, me).start()
    for j, chip in enumerate(chips):
        copy(j, chip, 2 * chip[0] + chip[1]).wait_recv()
    for j, chip in enumerate(chips):
        copy(j, chip, me).wait_send()


def _gather_over_chips(pack, name):
    R, Cc = pack.shape

    def body(p_ref, o_ref, send_sems, recv_sems):
        x, y, c, chips = _position()
        me = 2 * x + y
        o_ref[me] = p_ref[...]
        _chip_exchange(o_ref, me, x, y, c, chips, send_sems, recv_sems)

    vmem = pl.BlockSpec(memory_space=pltpu.VMEM)
    return pl.pallas_call(
        body, name=name,
        in_specs=[vmem], out_specs=vmem, out_shape=jax.ShapeDtypeStruct((4, R, Cc), F32),
        scratch_shapes=[pltpu.SemaphoreType.DMA((3,)), pltpu.SemaphoreType.DMA((3,))],
    )(pack)


def _all_reduce_small(pack, name):
    R, Cc = pack.shape

    def body(p_ref, o_ref, sibling, buf, send_sems, recv_sems):
        x, y, c, chips = _position()
        me = 2 * x + y
        swap = pltpu.make_async_remote_copy(
            src_ref=p_ref, dst_ref=sibling, send_sem=send_sems.at[3], recv_sem=recv_sems.at[3],
            device_id=(x, y, 1 - c), device_id_type=MESH_ID)
        swap.start()
        swap.wait()
        buf[me] = p_ref[...] + sibling[...]
        _chip_exchange(buf, me, x, y, c, chips, send_sems, recv_sems)
        o_ref[...] = (buf[0] + buf[1]) + (buf[2] + buf[3])

    vmem = pl.BlockSpec(memory_space=pltpu.VMEM)
    return pl.pallas_call(
        body, name=name,
        in_specs=[vmem], out_specs=vmem, out_shape=jax.ShapeDtypeStruct((R, Cc), F32),
        scratch_shapes=[pltpu.VMEM((R, Cc), F32), pltpu.VMEM((4, R, Cc), F32), pltpu.SemaphoreType.DMA((4,)),
                        pltpu.SemaphoreType.DMA((4,))],
    )(pack)


def _local_forward_backward(x2, target2, S, pass_on, fetch, reduce_begin, reduce_commit, layers, final_g,
                            tm=ROW_TILE):
    T, D = x2.shape
    C = D // 2
    n_heads = C // GROUP
    n_layers = len(layers)
    weights = {}
    saved = []
    xc = x2
    for li, lw in enumerate(layers):
        if li == 0:
            pass_on(0, None)
            weights.update(fetch(0, None))
        h1, qkv3, cv3 = _norm_proj(xc, lw["norm1"], weights[li, "w_in"], ((3, C, F32), (3, C, BF16)), tm,
                                   min(C, 512), f"l{li}_norm_in_proj")
        if li == 0:
            pass_on(1, h1)
        o, lse, mix = _attn_fwd(qkv3, lw["attn_g"], 2, S, n_heads, f"l{li}_attn_fwd")
        pin = None
        if li == 0:
            weights.update(fetch(1, o))
            pin = pass_on(3, pass_on(2, o))
        mix = _mix_conv_fwd(cv3, lw["taps"], lw["conv_g"], mix, S, f"l{li}_mix_conv_fwd", pin)
        x_mid = _proj_residual(mix, weights[li, "w_out"], xc, tm, f"l{li}_out_proj")
        if li == 0:
            weights.update(fetch(2, x_mid))
        Fd = weights[li, "ffn_up"].shape[2] // 2
        h2, up3 = _norm_proj(x_mid, lw["norm2"], weights[li, "ffn_up"], ((2, Fd, BF16),), tm, DOT_CHUNK,
                             f"l{li}_norm_ffn_up")
        if li == 0:
            weights.update(fetch(3, up3))
        act = _ffn_act_fwd(up3, lw["ffn_taps"], S, f"l{li}_ffn_act_fwd")
        pin = pass_on(li + 4, act) if li + 1 < n_layers else None
        x_out = _proj_residual(act.reshape(1, T, Fd), weights[li, "ffn_down"], x_mid, tm, f"l{li}_ffn_down", pin)
        if li + 1 < n_layers:
            weights.update(fetch(li + 4, x_out))
        saved.append(dict(x_in=xc, h1=h1, qkv3=qkv3, cv3=cv3, o=o, lse=lse, mix=mix, x_mid=x_mid, h2=h2, up3=up3,
                          act=act))
        xc = x_out

    dx, d_final_g, loss_part = _final_norm_loss(xc, final_g, target2, tm, "final_norm_loss")

    small = [None] * n_layers
    started = None
    for li in reversed(range(n_layers)):
        lw, sv = layers[li], saved[li]
        w_in, w_out, ffn_up, ffn_down = (weights[li, n] for n in ("w_in", "w_out", "ffn_up", "ffn_down"))
        dxb, dact3 = _grad_through_weight(dx, ffn_down, 1, Fd, BF16, tm, DOT_CHUNK, f"l{li}_d_act", started)
        Fd = ffn_down.shape[1]
        d_ffn_down = _weight_grad(sv["act"].reshape(1, T, Fd), dxb.reshape(1, T, D), Fd // 2, D,
                                  f"l{li}_d_ffn_down")
        dup3, d_taps_g, d_taps_v = _ffn_act_bwd(sv["up3"], dact3, lw["ffn_taps"], S, f"l{li}_ffn_act_bwd")
        d_ffn_up = _weight_grad(sv["h2"].reshape(1, D, T), dup3, D, Fd // 2, f"l{li}_d_ffn_up", a_transposed=True)
        if li == 0:
            early = reduce_begin({(li, "ffn_down"): d_ffn_down, (li, "ffn_up"): d_ffn_up})
        dx_mid, d_norm2 = _grad_through_proj_norm(dup3, ffn_up, sv["x_mid"], lw["norm2"], dx, tm,
                                                  f"l{li}_d_norm2")
        started = reduce_commit(early, dx_mid) if li == 0 else None
        dxmb, dmix3 = _grad_through_weight(dx_mid, w_out, 2, C, F32, tm, min(C, 512), f"l{li}_d_mix", started)
        d_w_out = _weight_grad(sv["mix"], dxmb.reshape(1, T, D), min(C, 256), D, f"l{li}_d_w_out")
        dproj, d_attn_g = _attn_bwd(sv["qkv3"], sv["o"], sv["lse"], dmix3, lw["attn_g"], 6, S, n_heads,
                                    f"l{li}_attn_bwd")
        dproj, d_taps, d_conv_g = _mix_conv_bwd(sv["cv3"], dmix3, lw["taps"], lw["conv_g"], dproj, S,
                                                f"l{li}_mix_conv_bwd")
        d_w_in = _weight_grad(sv["h1"].reshape(1, D, T), dproj, D, C, f"l{li}_d_w_in", a_transposed=True)
        late = {(li, "w_out"): d_w_out, (li, "w_in"): d_w_in}
        if li > 0:
            late.update({(li, "ffn_down"): d_ffn_down, (li, "ffn_up"): d_ffn_up})
        late = reduce_begin(late)
        dx, d_norm1 = _grad_through_proj_norm(dproj, w_in, sv["x_in"], lw["norm1"], dx_mid, tm,
                                              f"l{li}_d_norm1")
        started = reduce_commit(late, dx)
        small[li] = dict(norm1=d_norm1, taps=d_taps, attn_g=d_attn_g, conv_g=d_conv_g, norm2=d_norm2,
                         ffn_taps=jnp.concatenate([d_taps_g, d_taps_v], axis=1))
    return loss_part, dx, small, d_final_g


SMALL_ORDER = ("norm1", "attn_g", "conv_g", "norm2", "taps", "ffn_taps")


def _pack_small(small, d_final_g, loss_row):
    parts = [small[li][k].reshape(-1) for li in range(len(small)) for k in SMALL_ORDER]
    loss_rows = jnp.tile(loss_row.reshape(1, LANES), (8, 1))
    return jnp.concatenate(parts + [d_final_g.reshape(-1), loss_rows.reshape(-1)]).reshape(-1, LANES)


def _unpack_small(pack, small, d_final_g):
    flat = pack.reshape(-1)
    out, pos = [dict() for _ in small], 0
    for li in range(len(small)):
        for k in SMALL_ORDER:
            n = small[li][k].size
            out[li][k] = flat[pos:pos + n].reshape(small[li][k].shape)
            pos += n
    return out, flat[pos:pos + d_final_g.size], flat[pos + d_final_g.size]


def kernel(x, norm1_g, w_in, mix_conv_w, attn_out_g, conv_out_g, w_out, norm2_g, ffn_up, ffn_conv_w, ffn_down, final_norm_g, loss_target, m_norm1_g, m_w_in, m_mix_conv_w, m_attn_out_g, m_conv_out_g, m_w_out, m_norm2_g, m_ffn_up, m_ffn_conv_w, m_ffn_down, m_final_norm_g, v_norm1_g, v_w_in, v_mix_conv_w, v_attn_out_g, v_conv_out_g, v_w_out, v_norm2_g, v_ffn_up, v_ffn_conv_w, v_ffn_down, v_final_norm_g):
    Bl, S, D = x.shape
    L = w_in.shape[0]
    T = Bl * S
    shard = 2 * lax.axis_index("x") + lax.axis_index("y")
    where = jnp.stack([shard, lax.axis_index("c")]).astype(jnp.int32)
    big_names = ("w_in", "w_out", "ffn_up", "ffn_down")

    taps_w, ftaps_w = mix_conv_w.shape[2], ffn_conv_w.shape[2]
    tap_pack = _gather_over_chips(
        jnp.concatenate([mix_conv_w.reshape(-1), ffn_conv_w.reshape(-1)]).reshape(-1, LANES), "all_gather_taps")
    by_chip = tap_pack.reshape(4, -1)
    n_taps = mix_conv_w.size
    taps_full = by_chip[:, :n_taps].reshape(4, L, 3, taps_w).transpose(1, 2, 0, 3).reshape(L, 3, 4 * taps_w)
    ftaps_full = by_chip[:, n_taps:].reshape(4, L, 3, ftaps_w).transpose(1, 2, 0, 3).reshape(L, 3, 4 * ftaps_w)

    big_shards = dict(zip(big_names, (w_in, w_out, ffn_up, ffn_down)))
    col_of = dict(zip(big_names, COL_SHARDED))
    groups = [[(0, n)] for n in big_names] + [[(l, n) for n in big_names] for l in range(1, L)]
    sems, in_flight = [], {}
    all_started = tap_pack
    for first, last in ((0, 1), (1, len(groups))):
        keys = [k for g in groups[first:last] for k in g]
        new_sems, arrays = _gather_start(
            [_cast_into_full(big_shards[n], l, col_of[n], where, f"cast_{n}_{l}") for l, n in keys],
            [col_of[n] for _, n in keys], [len(g) for g in groups[first:last]], all_started,
            f"gather_start_{first}")
        sems += new_sems
        in_flight.update(zip(keys, arrays))
        all_started = arrays[-1]

    def pass_on(g, after):
        after = all_started if g == 0 else after
        sems[g], arrays = _gather_pass([in_flight[k] for k in groups[g]], [col_of[n] for _, n in groups[g]],
                                       sems[g], after, f"gather_pass_{g}")
        in_flight.update(zip(groups[g], arrays))
        return arrays[0]

    def fetch(g, after):
        done = _gather_wait([in_flight[k] for k in groups[g]], [col_of[n] for _, n in groups[g]], sems[g], after,
                            f"gather_wait_{g}")
        return dict(zip(groups[g], done))

    pending = []

    begun = []

    def reduce_begin(grads):
        g = len(begun)
        keys = list(grads)
        cols = [col_of[n] for _, n in keys]
        begun.append((g, keys, cols) + _exchange_start([grads[k] for k in keys], cols, f"exchange_start_{g}"))
        return begun[-1]

    def reduce_commit(handle, after):
        g, keys, cols, ex_sems, mine, lands = handle
        mine, others = _exchange_wait(mine, lands, cols, ex_sems, after, f"exchange_wait_{g}")
        chip_sums = [_chip_sum(m, o, cw, where, f"chip_sum_{k[1]}_{k[0]}")
                     for k, m, o, cw in zip(keys, mine, others, cols)]
        pending.append((keys, cols) + _scatter_start(chip_sums, cols, f"scatter_start_{g}"))
        return pending[-1][3][0]

    layers = [dict(norm1=norm1_g[l:l + 1], taps=taps_full[l], attn_g=attn_out_g[l:l + 1],
                   conv_g=conv_out_g[l:l + 1], norm2=norm2_g[l:l + 1], ffn_taps=ftaps_full[l]) for l in range(L)]

    loss_part, dx, small, d_final_g = _local_forward_backward(
        x.reshape(T, D), loss_target.reshape(T, D), S, pass_on, fetch, reduce_begin, reduce_commit, layers,
        final_norm_g.reshape(1, D))

    def finish_group(g, after):
        keys, cols, rs_sems, sources, lands = pending[g]
        sources, lands = _scatter_wait(sources, lands, cols, rs_sems, after, f"scatter_wait_{g}")
        for (l, n), cw, src, land in zip(keys, cols, sources, lands):
            reduced[n] = _owner_sum(src, land, cw, where, l, L, reduced[n], f"owner_sum_{n}_{l}")

    reduced = dict.fromkeys(big_names)
    last_started = pending[-1][3][0]
    for g in range(len(pending) - 1):
        finish_group(g, last_started)
    late_names = [n for n in big_names if any(n == name for _, name in pending[-1][0])]
    early_names = [n for n in big_names if n not in late_names]
    g_big = dict(zip(early_names, _share_with_sibling([reduced[n] for n in early_names],
                                                      [col_of[n] for n in early_names], "grad_share_early")))

    pack = _all_reduce_small(_pack_small(small, d_final_g, loss_part), "all_reduce_small_grads")
    g_small, g_final, loss = _unpack_small(pack, small, d_final_g)

    def stacked(key):
        return jnp.stack([g_small[l][key].reshape(g_small[l][key].shape[-2:] if key.endswith("taps") else (-1,))
                          for l in range(L)])

    g_norm1, g_attn, g_conv, g_norm2 = stacked("norm1"), stacked("attn_g"), stacked("conv_g"), stacked("norm2")
    g_taps = lax.dynamic_slice(stacked("taps"), (0, 0, shard * taps_w), (L, 3, taps_w))
    g_ftaps = lax.dynamic_slice(stacked("ffn_taps"), (0, 0, shard * ftaps_w), (L, 3, ftaps_w))

    grads_out = dict(norm1_g=g_norm1, w_in=None, mix_conv_w=g_taps, attn_out_g=g_attn, conv_out_g=g_conv,
                     w_out=None, norm2_g=g_norm2, ffn_up=None, ffn_conv_w=g_ftaps, ffn_down=None,
                     final_norm_g=g_final)
    weights = dict(norm1_g=norm1_g, w_in=w_in, mix_conv_w=mix_conv_w, attn_out_g=attn_out_g, conv_out_g=conv_out_g,
                   w_out=w_out, norm2_g=norm2_g, ffn_up=ffn_up, ffn_conv_w=ffn_conv_w, ffn_down=ffn_down,
                   final_norm_g=final_norm_g)
    ms = dict(norm1_g=m_norm1_g, w_in=m_w_in, mix_conv_w=m_mix_conv_w, attn_out_g=m_attn_out_g,
              conv_out_g=m_conv_out_g, w_out=m_w_out, norm2_g=m_norm2_g, ffn_up=m_ffn_up, ffn_conv_w=m_ffn_conv_w,
              ffn_down=m_ffn_down, final_norm_g=m_final_norm_g)
    vs = dict(norm1_g=v_norm1_g, w_in=v_w_in, mix_conv_w=v_mix_conv_w, attn_out_g=v_attn_out_g,
              conv_out_g=v_conv_out_g, w_out=v_w_out, norm2_g=v_norm2_g, ffn_up=v_ffn_up, ffn_conv_w=v_ffn_conv_w,
              ffn_down=v_ffn_down, final_norm_g=v_final_norm_g)
    names = list(weights)
    small_names = [n for n in names if n not in big_names]
    delta, new_m, new_v = {}, {}, {}

    def update_big(n):
        shp = weights[n].shape
        two_d = (shp[0] * shp[1], shp[2])
        d_, m_, v_, g_ = _adamw(weights[n].reshape(two_d), g_big[n].reshape(two_d), ms[n].reshape(two_d),
                                vs[n].reshape(two_d), f"adamw_{n}")
        delta[n], new_m[n], new_v[n], grads_out[n] = (a.reshape(shp) for a in (d_, m_, v_, g_))

    for n in early_names:
        update_big(n)
    finish_group(len(pending) - 1, delta[early_names[-1]] if early_names else None)
    g_big.update(zip(late_names, _share_with_sibling([reduced[n] for n in late_names],
                                                     [col_of[n] for n in late_names], "grad_share_late")))
    for n in late_names:
        update_big(n)

    def packed(tree):
        return jnp.concatenate([tree[n].reshape(-1) for n in small_names]).reshape(-1, LANES)

    d_, m_, v_, _ = _adamw(packed(weights), packed(grads_out), packed(ms), packed(vs), "adamw_small")
    pos = 0
    for n in small_names:
        size, shp = weights[n].size, weights[n].shape
        delta[n] = d_.reshape(-1)[pos:pos + size].reshape(shp)
        new_m[n] = m_.reshape(-1)[pos:pos + size].reshape(shp)
        new_v[n] = v_.reshape(-1)[pos:pos + size].reshape(shp)
        pos += size

    return (loss, dx.reshape(Bl, S, D), *[grads_out[n] for n in names], *[delta[n] for n in names],
            *[new_m[n] for n in names], *[new_v[n] for n in names])
```

```python
import functools
import math

import jax
import jax.numpy as jnp
from jax import lax
from jax.experimental import pallas as pl
from jax.experimental.pallas import tpu as pltpu

F32 = jnp.float32
BF16 = jnp.bfloat16
EPS = 1e-6
GROUP = 64
LANES = 128
BAND = 128
DILATIONS = (1, 4, 16)
NEG = -1e30
MIB = 1024 * 1024
MESH_ID = pl.DeviceIdType.MESH

ADAM_LR = 0.001
ADAM_B1 = 0.9
ADAM_B2 = 0.999
ADAM_EPS = 1e-08
ADAM_WD = 0.01
ADAM_STEP = 10


ANY = pl.BlockSpec(memory_space=pl.ANY)


def _in_hbm(x):
    return pltpu.with_memory_space_constraint(x, pltpu.HBM)


VMEM_LIMIT_MIB = 48
VMEM_LIMIT_RESIDENT_WEIGHT_MIB = 56
ROW_TILE = 512
DOT_CHUNK = 256
ATTN_UNROLL = 16
ATTN_ROWS = 256


def _params(sem=None, vmem_mb=VMEM_LIMIT_MIB):
    return pltpu.CompilerParams(dimension_semantics=sem, vmem_limit_bytes=vmem_mb * MIB)


def _nt(a, b):
    return lax.dot_general(a, b, (((1,), (1,)), ((), ())), preferred_element_type=F32)


def _tn(a, b):
    return lax.dot_general(a, b, (((0,), (0,)), ((), ())), preferred_element_type=F32)


def _seg_sum(x, is_a):
    s_a = jnp.sum(jnp.where(is_a, x, 0.0), axis=-1, keepdims=True)
    s_b = jnp.sum(jnp.where(is_a, 0.0, x), axis=-1, keepdims=True)
    return jnp.where(is_a, s_a, s_b)


def _lane_is_a():
    return lax.broadcasted_iota(jnp.int32, (1, LANES), 1) < GROUP


def _norm_proj(x, g, w3, groups, tm, chunk, name):
    T, D = x.shape
    N = w3.shape[2]
    assert sum(p * c for p, c, _ in groups) == N and T % tm == 0

    def body(x_ref, g_ref, w_ref, h_ref, *out_refs):
        xv = x_ref[...]
        rstd = lax.rsqrt(jnp.mean(xv * xv, axis=-1, keepdims=True) + EPS)
        h = ((xv * rstd) * g_ref[...]).astype(BF16)
        h_ref[...] = h.T
        col = 0
        for (pieces, width, dtype), o_ref in zip(groups, out_refs):
            for p in range(pieces):
                for c0 in range(0, width, chunk):
                    acc = jnp.dot(h, w_ref[:, col + c0:col + c0 + chunk], preferred_element_type=F32)
                    o_ref[p, :, c0:c0 + chunk] = acc.astype(dtype)
                col += width

    out_shape = [jax.ShapeDtypeStruct((D, T), BF16)]
    out_specs = [pl.BlockSpec((D, tm), lambda i: (0, i))]
    for pieces, width, dtype in groups:
        assert width % chunk == 0
        out_shape.append(jax.ShapeDtypeStruct((pieces, T, width), dtype))
        out_specs.append(pl.BlockSpec((pieces, tm, width), lambda i: (0, i, 0)))
    return pl.pallas_call(
        body, grid=(T // tm,), name=name,
        in_specs=[pl.BlockSpec((tm, D), lambda i: (i, 0)),
                  pl.BlockSpec((1, D), lambda i: (0, 0)),
                  pl.BlockSpec((None, D, N), lambda i: (0, 0, 0))],
        out_specs=out_specs, out_shape=out_shape,
        compiler_params=_params(("parallel",), VMEM_LIMIT_RESIDENT_WEIGHT_MIB),
    )(x, g, w3)


def _proj_residual(pieces3, w3, x, tm, name, after=None):
    P, T, C = pieces3.shape
    D = w3.shape[2]

    def body(a_ref, w_ref, x_ref, *rest):
        o_ref = rest[-1]
        acc = x_ref[...]
        for p in range(P):
            acc = acc + jnp.dot(a_ref[p], w_ref[p * C:(p + 1) * C, :], preferred_element_type=F32)
        o_ref[...] = acc

    in_specs = [pl.BlockSpec((P, tm, C), lambda i: (0, i, 0)),
                pl.BlockSpec((None, P * C, D), lambda i: (0, 0, 0)),
                pl.BlockSpec((tm, D), lambda i: (i, 0))]
    operands = [pieces3, w3, x]
    if after is not None:
        in_specs.append(ANY)
        operands.append(after)
    return pl.pallas_call(
        body, grid=(T // tm,), name=name,
        in_specs=in_specs,
        out_specs=pl.BlockSpec((tm, D), lambda i: (i, 0)),
        out_shape=jax.ShapeDtypeStruct((T, D), F32),
        compiler_params=_params(("parallel",)),
    )(*operands)


def _grad_through_weight(dy, w3, pieces, width, out_dtype, tm, chunk, name, after=None):
    T, D = dy.shape

    def body(dy_ref, w_ref, *rest):
        dyb_ref, o_ref = rest[-2:]
        dyb = dy_ref[...].astype(BF16)
        dyb_ref[...] = dyb
        for p in range(pieces):
            for c0 in range(0, width, chunk):
                r0 = p * width + c0
                o_ref[p, :, c0:c0 + chunk] = _nt(dyb, w_ref[r0:r0 + chunk, :]).astype(out_dtype)

    in_specs = [pl.BlockSpec((tm, D), lambda i: (i, 0)),
                pl.BlockSpec((None, pieces * width, D), lambda i: (0, 0, 0))]
    operands = [dy, w3]
    if after is not None:
        in_specs.append(ANY)
        operands.append(after)
    return pl.pallas_call(
        body, grid=(T // tm,), name=name,
        in_specs=in_specs,
        out_specs=[pl.BlockSpec((tm, D), lambda i: (i, 0)),
                   pl.BlockSpec((pieces, tm, width), lambda i: (0, i, 0))],
        out_shape=[jax.ShapeDtypeStruct((T, D), BF16),
                   jax.ShapeDtypeStruct((pieces, T, width), out_dtype)],
        compiler_params=_params(("parallel",)),
    )(*operands)


def _grad_through_proj_norm(dp3, w3, x, g, dx_in, tm, name):
    P, T, C = dp3.shape
    D = w3.shape[1]

    def body(dp_ref, w_ref, x_ref, g_ref, dxin_ref, dx_ref, dg_ref):
        dh = _nt(dp_ref[0], w_ref[:, 0:C])
        for p in range(1, P):
            dh = dh + _nt(dp_ref[p], w_ref[:, p * C:(p + 1) * C])
        xv = x_ref[...]
        rstd = lax.rsqrt(jnp.mean(xv * xv, axis=-1, keepdims=True) + EPS)
        xn = xv * rstd
        a = dh * g_ref[...]
        dx_ref[...] = dxin_ref[...] + rstd * (a - xn * jnp.mean(a * xn, axis=-1, keepdims=True))
        part = jnp.sum(dh * xn, axis=0, keepdims=True)

        @pl.when(pl.program_id(0) == 0)
        def _():
            dg_ref[...] = part

        @pl.when(pl.program_id(0) != 0)
        def _():
            dg_ref[...] += part

    return pl.pallas_call(
        body, grid=(T // tm,), name=name,
        in_specs=[pl.BlockSpec((P, tm, C), lambda i: (0, i, 0)),
                  pl.BlockSpec((None, D, P * C), lambda i: (0, 0, 0)),
                  pl.BlockSpec((tm, D), lambda i: (i, 0)),
                  pl.BlockSpec((1, D), lambda i: (0, 0)),
                  pl.BlockSpec((tm, D), lambda i: (i, 0))],
        out_specs=[pl.BlockSpec((tm, D), lambda i: (i, 0)),
                   pl.BlockSpec((1, D), lambda i: (0, 0))],
        out_shape=[jax.ShapeDtypeStruct((T, D), F32), jax.ShapeDtypeStruct((1, D), F32)],
        compiler_params=_params(("arbitrary",), VMEM_LIMIT_RESIDENT_WEIGHT_MIB),
    )(dp3, w3, x, g, dx_in)


def _weight_grad(a3, g3, ta, tg, name, a_transposed=False):
    PG, T, CG = g3.shape
    PA, CA = (a3.shape[0], a3.shape[1]) if a_transposed else (a3.shape[0], a3.shape[2])
    na, ng = CA // ta, CG // tg
    assert CA % ta == 0 and CG % tg == 0

    def body(a_ref, g_ref, o_ref):
        if a_transposed:
            part = jnp.dot(a_ref[...], g_ref[...], preferred_element_type=F32)
        else:
            part = _tn(a_ref[...], g_ref[...])
        o_ref[...] = part.astype(o_ref.dtype)

    a_spec = (pl.BlockSpec((None, ta, T), lambda i, j: (i // na, i % na, 0)) if a_transposed
              else pl.BlockSpec((None, T, ta), lambda i, j: (i // na, 0, i % na)))
    return pl.pallas_call(
        body, grid=(PA * na, PG * ng), name=name,
        in_specs=[a_spec, pl.BlockSpec((None, T, tg), lambda i, j: (j // ng, 0, j % ng))],
        out_specs=pl.BlockSpec((None, ta, tg), lambda i, j: (0, i, j)),
        out_shape=pltpu.HBM((1, PA * CA, PG * CG), BF16),
        compiler_params=_params(("parallel", "parallel"), VMEM_LIMIT_RESIDENT_WEIGHT_MIB),
    )(a3, g3)


def _bias_tables(bm_ref, lone_ref, pair, n_heads, S):
    ii = lax.broadcasted_iota(jnp.int32, (BAND, 2 * BAND), 0)
    jj = lax.broadcasted_iota(jnp.int32, (BAND, 2 * BAND), 1)
    dist = BAND + ii - jj
    valid = (dist >= 0) & (dist <= BAND)
    distf = dist.astype(F32)
    for hh in range(2):
        head = (2 * pair + hh + 1).astype(F32)
        slope = jnp.exp(jnp.full((1, 1), -8.0 / n_heads * math.log(2.0), F32) * head)
        for bi, d in enumerate(DILATIONS):
            table = jnp.where(valid, -(slope * d) * distf, NEG)
            bm_ref[bi, hh * BAND:(hh + 1) * BAND, :] = table
            if S // (BAND * d) == 1:
                lone_ref[bi, hh * BAND:(hh + 1) * BAND, :] = table[:, BAND:2 * BAND]


def _stack_heads(x, is_a):
    zero = jnp.zeros_like(x)
    return jnp.concatenate([jnp.where(is_a, x, zero), jnp.where(is_a, zero, x)], axis=0)


def _unstack_heads(x2, is_a):
    return jnp.where(is_a, x2[0:BAND], x2[BAND:2 * BAND])


def _gather_residues(dst_ref, src, d, S, convert):
    L = S // d
    for r in range(d):
        rows = pl.ds(r, L, stride=d) if d > 1 else slice(None)
        dst_ref[r * L:(r + 1) * L, :] = convert(src(rows))


def _block_rows(t, d, S):
    nb = S // (BAND * d)
    n = t % nb
    has_prev = jnp.minimum(n, 1)
    cur = pl.ds(pl.multiple_of(t * BAND, BAND), BAND)
    prev = pl.ds(pl.multiple_of((t - has_prev) * BAND, BAND), BAND)
    return cur, prev, has_prev


def _first_block_penalty(has_prev):
    jrow = lax.broadcasted_iota(jnp.int32, (1, 2 * BAND), 1)
    pen = jnp.where(has_prev == 0, NEG, 0.0).astype(F32)
    return jnp.where(jrow < BAND, pen, 0.0)


def _attn_fwd(qkv3, gain, mix_shape_pieces, S, n_heads, name):
    _, T, C = qkv3.shape
    B, P = T // S, C // LANES
    NBLK = S // BAND
    scale = GROUP ** -0.5
    nbr = len(DILATIONS)
    RC = ATTN_ROWS

    def body(qkv_ref, g_ref, o_ref, lse_ref, an_ref, qs, ks, vs, op, mp, lp, ob, mb, lb, bm, bml):
        pair = pl.program_id(1)
        is_a = _lane_is_a()
        _bias_tables(bm, bml, pair, n_heads, S)

        for bi, d in enumerate(DILATIONS):
            nb = S // (BAND * d)
            _gather_residues(qs, lambda rows: qkv_ref.at[0][rows, :], d, S, lambda v: (v * scale).astype(BF16))
            _gather_residues(ks, lambda rows: qkv_ref.at[1][rows, :], d, S, lambda v: v.astype(BF16))
            _gather_residues(vs, lambda rows: qkv_ref.at[2][rows, :], d, S, lambda v: v.astype(BF16))
            o_dst, m_dst, l_dst = (ob.at[bi], mb.at[bi], lb.at[bi]) if d == 1 else (op, mp, lp)

            def block(t, carry, bi=bi, d=d, nb=nb, o_dst=o_dst, m_dst=m_dst, l_dst=l_dst):
                cur, prev, has_prev = _block_rows(t, d, S)
                q2 = _stack_heads(qs[cur, :], is_a)
                if nb == 1:
                    kc, vc = ks[cur, :], vs[cur, :]
                    s = _nt(q2, kc) + bml[bi]
                else:
                    kc = jnp.concatenate([ks[prev, :], ks[cur, :]], axis=0)
                    vc = jnp.concatenate([vs[prev, :], vs[cur, :]], axis=0)
                    s = _nt(q2, kc) + bm[bi] + _first_block_penalty(has_prev)
                m = jnp.max(s, axis=-1, keepdims=True)
                e = jnp.exp(s - m)
                l = jnp.sum(e, axis=-1, keepdims=True)
                pv = jnp.dot(e.astype(BF16), vc, preferred_element_type=F32)
                o_dst[cur, :] = _unstack_heads(pv, is_a)
                m_dst[cur, :] = _unstack_heads(m, is_a)
                l_dst[cur, :] = _unstack_heads(l, is_a)
                return carry

            lax.fori_loop(0, NBLK, block, 0, unroll=ATTN_UNROLL)
            if d > 1:
                L = S // d
                for r in range(d):
                    rows = pl.ds(r, L, stride=d)
                    ob.at[bi][rows, :] = op[r * L:(r + 1) * L, :]
                    mb.at[bi][rows, :] = mp[r * L:(r + 1) * L, :]
                    lb.at[bi][rows, :] = lp[r * L:(r + 1) * L, :]

        def finish(ci, carry):
            rs = pl.ds(pl.multiple_of(ci * RC, RC), RC)
            ms = [mb[bi, rs, :] for bi in range(nbr)]
            mmax = functools.reduce(jnp.maximum, ms)
            ws = [jnp.exp(m - mmax) for m in ms]
            num = sum(ob[bi, rs, :] * ws[bi] for bi in range(nbr))
            den = sum(lb[bi, rs, :] * ws[bi] for bi in range(nbr))
            o = num / den
            o_ref[rs, :] = o
            lse_ref[rs, :] = mmax + jnp.log(den)
            rstd = lax.rsqrt(_seg_sum(o * o, is_a) * (1.0 / GROUP) + EPS)
            an_ref[rs, :] = ((o * rstd) * g_ref[...]).astype(BF16)
            return carry

        lax.fori_loop(0, S // RC, finish, 0, unroll=True)

    seq = pl.BlockSpec((S, LANES), lambda b, p: (b, p))
    return pl.pallas_call(
        body, grid=(B, P), name=name,
        in_specs=[pl.BlockSpec((3, S, LANES), lambda b, p: (0, b, p)),
                  pl.BlockSpec((1, LANES), lambda b, p: (0, p))],
        out_specs=[seq, seq, pl.BlockSpec((None, S, LANES), lambda b, p: (0, b, p))],
        out_shape=[jax.ShapeDtypeStruct((T, C), F32), jax.ShapeDtypeStruct((T, C), F32),
                   jax.ShapeDtypeStruct((mix_shape_pieces, T, C), BF16)],
        scratch_shapes=[pltpu.VMEM((S, LANES), BF16)] * 3 + [pltpu.VMEM((S, LANES), F32)] * 3
        + [pltpu.VMEM((nbr, S, LANES), F32)] * 3
        + [pltpu.VMEM((nbr, 2 * BAND, 2 * BAND), F32), pltpu.VMEM((nbr, 2 * BAND, BAND), F32)],
        compiler_params=_params(("parallel", "parallel")),
    )(qkv3, gain)


def _attn_bwd(qkv3, o, lse, dmix3, gain, dproj_pieces, S, n_heads, name):
    _, T, C = qkv3.shape
    B, P = T // S, C // LANES
    NBLK = S // BAND
    scale = GROUP ** -0.5
    nbr = len(DILATIONS)
    RC = ATTN_ROWS

    def body(qkv_ref, o_ref, lse_ref, dn_ref, g_ref, dqkv_ref, dg_ref,
             do_n, dd_n, qs, ks, vs, dos, lses, dds, dqp, dkp, dvp, dqn, dkn, dvn, bm, bml):
        pair = pl.program_id(0)
        b = pl.program_id(1)
        is_a = _lane_is_a()
        _bias_tables(bm, bml, pair, n_heads, S)

        def prologue(ci, dg_acc):
            rs = pl.ds(pl.multiple_of(ci * RC, RC), RC)
            ov = o_ref[rs, :]
            dn = dn_ref[rs, :]
            rstd = lax.rsqrt(_seg_sum(ov * ov, is_a) * (1.0 / GROUP) + EPS)
            on = ov * rstd
            a = dn * g_ref[...]
            s_a = _seg_sum(a * on, is_a)
            do_n[rs, :] = rstd * (a - on * (s_a * (1.0 / GROUP)))
            dd_n[rs, :] = (EPS * s_a) * (rstd * rstd)
            zero = jnp.zeros((RC, LANES), F32)
            dqn[rs, :] = zero
            dkn[rs, :] = zero
            dvn[rs, :] = zero
            return dg_acc + jnp.sum(dn * on, axis=0, keepdims=True)

        dg_part = lax.fori_loop(0, S // RC, prologue, jnp.zeros((1, LANES), F32), unroll=True)

        @pl.when(b == 0)
        def _():
            dg_ref[...] = dg_part

        @pl.when(b != 0)
        def _():
            dg_ref[...] += dg_part

        for bi, d in enumerate(DILATIONS):
            nb = S // (BAND * d)
            L = S // d
            _gather_residues(qs, lambda rows: qkv_ref.at[0][rows, :], d, S, lambda v: (v * scale).astype(BF16))
            _gather_residues(ks, lambda rows: qkv_ref.at[1][rows, :], d, S, lambda v: v.astype(BF16))
            _gather_residues(vs, lambda rows: qkv_ref.at[2][rows, :], d, S, lambda v: v.astype(BF16))
            _gather_residues(dos, lambda rows: do_n[rows, :], d, S, lambda v: v.astype(BF16))
            if d == 1:
                lse_src, dd_src, dq_dst, dk_dst, dv_dst = lse_ref, dd_n, dqn, dkn, dvn
            else:
                _gather_residues(lses, lambda rows: lse_ref[rows, :], d, S, lambda v: v)
                _gather_residues(dds, lambda rows: dd_n[rows, :], d, S, lambda v: v)
                dkp[...] = jnp.zeros((S, LANES), F32)
                dvp[...] = jnp.zeros((S, LANES), F32)
                lse_src, dd_src, dq_dst, dk_dst, dv_dst = lses, dds, dqp, dkp, dvp

            def block(t, carry, bi=bi, d=d, nb=nb, lse_src=lse_src, dd_src=dd_src, dq_dst=dq_dst, dk_dst=dk_dst,
                      dv_dst=dv_dst):
                cur, prev, has_prev = _block_rows(t, d, S)
                q2 = _stack_heads(qs[cur, :], is_a)
                do2 = _stack_heads(dos[cur, :], is_a)
                lse_t = lse_src[cur, :]
                dd_t = dd_src[cur, :]
                lse2 = jnp.concatenate([lse_t[:, 0:1], lse_t[:, GROUP:GROUP + 1]], axis=0)
                dd2 = jnp.concatenate([dd_t[:, 0:1], dd_t[:, GROUP:GROUP + 1]], axis=0)
                if nb == 1:
                    kc, vc = ks[cur, :], vs[cur, :]
                    s = _nt(q2, kc) + bml[bi]
                else:
                    kc = jnp.concatenate([ks[prev, :], ks[cur, :]], axis=0)
                    vc = jnp.concatenate([vs[prev, :], vs[cur, :]], axis=0)
                    s = _nt(q2, kc) + bm[bi] + _first_block_penalty(has_prev)
                p = jnp.exp(s - lse2)
                ds = (p * (_nt(do2, vc) - dd2)).astype(BF16)
                dq = _unstack_heads(jnp.dot(ds, kc, preferred_element_type=F32), is_a)
                dk = _tn(ds, q2)
                dv = _tn(p.astype(BF16), do2)
                dq_dst[cur, :] = dq
                if nb == 1:
                    dk_dst[cur, :] += dk
                    dv_dst[cur, :] += dv
                else:
                    dk_dst[prev, :] += dk[0:BAND, :]
                    dv_dst[prev, :] += dv[0:BAND, :]
                    dk_dst[cur, :] += dk[BAND:2 * BAND, :]
                    dv_dst[cur, :] += dv[BAND:2 * BAND, :]
                return carry

            lax.fori_loop(0, NBLK, block, 0, unroll=ATTN_UNROLL)
            if d > 1:
                for r in range(d):
                    rows = pl.ds(r, L, stride=d)
                    dqn[rows, :] += dqp[r * L:(r + 1) * L, :]
                    dkn[rows, :] += dkp[r * L:(r + 1) * L, :]
                    dvn[rows, :] += dvp[r * L:(r + 1) * L, :]

        dqkv_ref[0] = (dqn[...] * scale).astype(BF16)
        dqkv_ref[1] = dkn[...].astype(BF16)
        dqkv_ref[2] = dvn[...].astype(BF16)

    seq = pl.BlockSpec((S, LANES), lambda p, b: (b, p))
    f32_seq = pltpu.VMEM((S, LANES), F32)
    bf_seq = pltpu.VMEM((S, LANES), BF16)
    return pl.pallas_call(
        body, grid=(P, B), name=name,
        in_specs=[pl.BlockSpec((3, S, LANES), lambda p, b: (0, b, p)), seq, seq,
                  pl.BlockSpec((None, S, LANES), lambda p, b: (0, b, p)),
                  pl.BlockSpec((1, LANES), lambda p, b: (0, p))],
        out_specs=[pl.BlockSpec((3, S, LANES), lambda p, b: (0, b, p)),
                   pl.BlockSpec((1, LANES), lambda p, b: (0, p))],
        out_shape=[jax.ShapeDtypeStruct((dproj_pieces, T, C), BF16), jax.ShapeDtypeStruct((1, C), F32)],
        scratch_shapes=[f32_seq, f32_seq, bf_seq, bf_seq, bf_seq, bf_seq, f32_seq, f32_seq,
                        f32_seq, f32_seq, f32_seq, f32_seq, f32_seq, f32_seq,
                        pltpu.VMEM((nbr, 2 * BAND, 2 * BAND), F32), pltpu.VMEM((nbr, 2 * BAND, BAND), F32)],
        compiler_params=_params(("parallel", "arbitrary")),
    )(qkv3, o, lse, dmix3, gain)


def _delay(x, k, row):
    return jnp.where(row >= k, pltpu.roll(x, k, 0), 0.0)


def _advance(x, k, row, S):
    return jnp.where(row < S - k, pltpu.roll(x, S - k, 0), 0.0)


def _conv3(x, w, row):
    return (w[0:1, :] * _delay(x, 2, row) + w[1:2, :] * _delay(x, 1, row)) + w[2:3, :] * x


HALO = 8


CONV_ROWS = 64
FFN_LANES = 128


def _zero_halo(pad_ref, S):
    zeros = jnp.zeros((HALO, pad_ref.shape[1]), pad_ref.dtype)
    pad_ref[0:HALO, :] = zeros
    pad_ref[HALO + S:2 * HALO + S, :] = zeros


def _window_at(pad_ref, r0, shift):
    return pad_ref[HALO + r0 + shift:HALO + r0 + shift + CONV_ROWS, :]


def _conv3_at(pad_ref, w, r0):
    return ((w[0:1, :] * _window_at(pad_ref, r0, -2) + w[1:2, :] * _window_at(pad_ref, r0, -1))
            + w[2:3, :] * _window_at(pad_ref, r0, 0))


def _conv3_grads_at(dz_ref, x_ref, w, r0):
    dz, dz1, dz2 = (_window_at(dz_ref, r0, k) for k in range(3))
    x = _window_at(x_ref, r0, 0)
    dx = (w[2:3, :] * dz + w[1:2, :] * dz1) + w[0:1, :] * dz2
    parts = [jnp.sum((d * x).reshape(CONV_ROWS // 8, 8, x.shape[1]), axis=0) for d in (dz2, dz1, dz)]
    return dx, parts


def _conv3_grads(dz, x, w, row, S):
    dz1 = _advance(dz, 1, row, S)
    dz2 = _advance(dz, 2, row, S)
    dx = (w[2:3, :] * dz + w[1:2, :] * dz1) + w[0:1, :] * dz2
    dw = jnp.concatenate([jnp.sum(dz2 * x, axis=0, keepdims=True),
                          jnp.sum(dz1 * x, axis=0, keepdims=True),
                          jnp.sum(dz * x, axis=0, keepdims=True)], axis=0)
    return dx, dw


def _mix_conv_fwd(cv3, taps, gain, mix, S, name, after=None):
    _, T, C = cv3.shape
    B, P = T // S, C // LANES

    def body(cv_ref, w_ref, g_ref, mix_hbm, *rest):
        y_ref, pad_c = rest[-2:]
        del mix_hbm
        is_a = _lane_is_a()
        _zero_halo(pad_c, S)
        pad_c[HALO:HALO + S, :] = cv_ref[1].astype(F32) * cv_ref[2].astype(F32)
        w = w_ref[...]
        for r0 in range(0, S, CONV_ROWS):
            y = cv_ref[0, r0:r0 + CONV_ROWS, :].astype(F32) * _conv3_at(pad_c, w, r0)
            rstd = lax.rsqrt(_seg_sum(y * y, is_a) * (1.0 / GROUP) + EPS)
            y_ref[r0:r0 + CONV_ROWS, :] = ((y * rstd) * g_ref[...]).astype(BF16)

    in_specs = [pl.BlockSpec((3, S, LANES), lambda b, p: (0, b, p)),
                pl.BlockSpec((3, LANES), lambda b, p: (0, p)),
                pl.BlockSpec((1, LANES), lambda b, p: (0, p)),
                ANY]
    operands = [cv3, taps, gain, mix]
    if after is not None:
        in_specs.append(ANY)
        operands.append(after)
    return pl.pallas_call(
        body, grid=(B, P), name=name,
        in_specs=in_specs,
        out_specs=pl.BlockSpec((None, S, LANES), lambda b, p: (1, b, p)),
        out_shape=jax.ShapeDtypeStruct(mix.shape, mix.dtype),
        scratch_shapes=[pltpu.VMEM((S + 2 * HALO, LANES), F32)],
        input_output_aliases={3: 0},
        compiler_params=_params(("parallel", "parallel")),
    )(*operands)


def _mix_conv_bwd(cv3, dmix3, taps, gain, dproj, S, name):
    _, T, C = cv3.shape
    B, P = T // S, C // LANES

    def body(cv_ref, dn_ref, w_ref, g_ref, dproj_hbm, dcv_ref, dw_ref, dg_ref):
        del dproj_hbm
        b = pl.program_id(1)
        row = lax.broadcasted_iota(jnp.int32, (S, 1), 0)
        is_a = _lane_is_a()
        w = w_ref[...]
        gb = cv_ref[0].astype(F32)
        gc = cv_ref[1].astype(F32)
        u = cv_ref[2].astype(F32)
        c = gc * u
        z = _conv3(c, w, row)
        y = gb * z
        rstd = lax.rsqrt(_seg_sum(y * y, is_a) * (1.0 / GROUP) + EPS)
        yn = y * rstd
        dn = dn_ref[...]
        a = dn * g_ref[...]
        dy = rstd * (a - yn * (_seg_sum(a * yn, is_a) * (1.0 / GROUP)))
        dg = jnp.sum(dn * yn, axis=0, keepdims=True)
        dc, dw = _conv3_grads(dy * gb, c, w, row, S)
        dcv_ref[0] = (dy * z).astype(BF16)
        dcv_ref[1] = (dc * u).astype(BF16)
        dcv_ref[2] = (dc * gc).astype(BF16)

        @pl.when(b == 0)
        def _():
            dw_ref[...] = dw
            dg_ref[...] = dg

        @pl.when(b != 0)
        def _():
            dw_ref[...] += dw
            dg_ref[...] += dg

    return pl.pallas_call(
        body, grid=(P, B), name=name,
        in_specs=[pl.BlockSpec((3, S, LANES), lambda p, b: (0, b, p)),
                  pl.BlockSpec((None, S, LANES), lambda p, b: (1, b, p)),
                  pl.BlockSpec((3, LANES), lambda p, b: (0, p)),
                  pl.BlockSpec((1, LANES), lambda p, b: (0, p)),
                  pl.BlockSpec(memory_space=pl.ANY)],
        out_specs=[pl.BlockSpec((3, S, LANES), lambda p, b: (1, b, p)),
                   pl.BlockSpec((3, LANES), lambda p, b: (0, p)),
                   pl.BlockSpec((1, LANES), lambda p, b: (0, p))],
        out_shape=[jax.ShapeDtypeStruct(dproj.shape, dproj.dtype),
                   jax.ShapeDtypeStruct((3, C), F32), jax.ShapeDtypeStruct((1, C), F32)],
        input_output_aliases={4: 0},
        compiler_params=_params(("parallel", "arbitrary")),
    )(cv3, dmix3, taps, gain, dproj)


def _sigmoid(x):
    return 0.5 * jnp.tanh(0.5 * x) + 0.5


def _ffn_act_fwd(up3, taps, S, name):
    _, T, Fd = up3.shape
    W = FFN_LANES
    B, P = T // S, Fd // W

    def body(up_ref, wg_ref, wv_ref, act_ref, pad_g, pad_v):
        _zero_halo(pad_g, S)
        _zero_halo(pad_v, S)
        pad_g[HALO:HALO + S, :] = up_ref[0].astype(F32)
        pad_v[HALO:HALO + S, :] = up_ref[1].astype(F32)
        wg = wg_ref[...]
        wv = wv_ref[...]
        for r0 in range(0, S, CONV_ROWS):
            cg = _conv3_at(pad_g, wg, r0)
            cv = _conv3_at(pad_v, wv, r0)
            act_ref[r0:r0 + CONV_ROWS, :] = ((cg * _sigmoid(cg)) * cv).astype(BF16)

    return pl.pallas_call(
        body, grid=(B, P), name=name,
        in_specs=[pl.BlockSpec((2, S, W), lambda b, p: (0, b, p)),
                  pl.BlockSpec((3, W), lambda b, p: (0, p)),
                  pl.BlockSpec((3, W), lambda b, p: (0, P + p))],
        out_specs=pl.BlockSpec((S, W), lambda b, p: (b, p)),
        out_shape=jax.ShapeDtypeStruct((T, Fd), BF16),
        scratch_shapes=[pltpu.VMEM((S + 2 * HALO, W), F32)] * 2,
        compiler_params=_params(("parallel", "parallel")),
    )(up3, taps, taps)


def _ffn_act_bwd(up3, dact3, taps, S, name):
    _, T, Fd = up3.shape
    W = FFN_LANES
    B, P = T // S, Fd // W

    def body(up_ref, da_ref, wg_ref, wv_ref, dup_ref, dwg_ref, dwv_ref, pad_ug, pad_uv, pad_dg, pad_dv):
        b = pl.program_id(1)
        for pad in (pad_ug, pad_uv, pad_dg, pad_dv):
            _zero_halo(pad, S)
        pad_ug[HALO:HALO + S, :] = up_ref[0].astype(F32)
        pad_uv[HALO:HALO + S, :] = up_ref[1].astype(F32)
        wg = wg_ref[...]
        wv = wv_ref[...]
        for r0 in range(0, S, CONV_ROWS):
            cg = _conv3_at(pad_ug, wg, r0)
            cv = _conv3_at(pad_uv, wv, r0)
            sg = _sigmoid(cg)
            da = da_ref[r0:r0 + CONV_ROWS, :].astype(F32)
            t = da * sg
            dcv = cg * t
            pad_dg[HALO + r0:HALO + r0 + CONV_ROWS, :] = cv * ((t + dcv) - dcv * sg)
            pad_dv[HALO + r0:HALO + r0 + CONV_ROWS, :] = dcv
        sums_g = [jnp.zeros((8, W), F32)] * 3
        sums_v = [jnp.zeros((8, W), F32)] * 3
        for r0 in range(0, S, CONV_ROWS):
            dug, parts_g = _conv3_grads_at(pad_dg, pad_ug, wg, r0)
            duv, parts_v = _conv3_grads_at(pad_dv, pad_uv, wv, r0)
            dup_ref[0, r0:r0 + CONV_ROWS, :] = dug.astype(BF16)
            dup_ref[1, r0:r0 + CONV_ROWS, :] = duv.astype(BF16)
            sums_g = [a + p for a, p in zip(sums_g, parts_g)]
            sums_v = [a + p for a, p in zip(sums_v, parts_v)]
        dwg = jnp.concatenate([jnp.sum(a, axis=0, keepdims=True) for a in sums_g], axis=0)
        dwv = jnp.concatenate([jnp.sum(a, axis=0, keepdims=True) for a in sums_v], axis=0)

        @pl.when(b == 0)
        def _():
            dwg_ref[...] = dwg
            dwv_ref[...] = dwv

        @pl.when(b != 0)
        def _():
            dwg_ref[...] += dwg
            dwv_ref[...] += dwv

    tap_out = pl.BlockSpec((3, W), lambda p, b: (0, p))
    return pl.pallas_call(
        body, grid=(P, B), name=name,
        in_specs=[pl.BlockSpec((2, S, W), lambda p, b: (0, b, p)),
                  pl.BlockSpec((None, S, W), lambda p, b: (0, b, p)),
                  pl.BlockSpec((3, W), lambda p, b: (0, p)),
                  pl.BlockSpec((3, W), lambda p, b: (0, P + p))],
        out_specs=[pl.BlockSpec((2, S, W), lambda p, b: (0, b, p)), tap_out, tap_out],
        out_shape=[jax.ShapeDtypeStruct((2, T, Fd), BF16),
                   jax.ShapeDtypeStruct((3, Fd), F32), jax.ShapeDtypeStruct((3, Fd), F32)],
        scratch_shapes=[pltpu.VMEM((S + 2 * HALO, W), F32)] * 4,
        compiler_params=_params(("parallel", "arbitrary")),
    )(up3, dact3, taps, taps)


def _final_norm_loss(x, g, target, tm, name):
    T, D = x.shape

    def body(x_ref, g_ref, t_ref, dx_ref, dg_ref, loss_ref):
        xv = x_ref[...]
        rstd = lax.rsqrt(jnp.mean(xv * xv, axis=-1, keepdims=True) + EPS)
        xn = xv * rstd
        err = xn * g_ref[...] - t_ref[...]
        part = 0.5 * jnp.sum(jnp.mean(err * err, axis=-1, keepdims=True), axis=0, keepdims=True)
        dy = err * (1.0 / D)
        a = dy * g_ref[...]
        dx_ref[...] = rstd * (a - xn * jnp.mean(a * xn, axis=-1, keepdims=True))
        dg = jnp.sum(dy * xn, axis=0, keepdims=True)
        lpart = jnp.broadcast_to(part, (1, LANES))

        @pl.when(pl.program_id(0) == 0)
        def _():
            dg_ref[...] = dg
            loss_ref[...] = lpart

        @pl.when(pl.program_id(0) != 0)
        def _():
            dg_ref[...] += dg
            loss_ref[...] += lpart

    row = pl.BlockSpec((tm, D), lambda i: (i, 0))
    return pl.pallas_call(
        body, grid=(T // tm,), name=name,
        in_specs=[row, pl.BlockSpec((1, D), lambda i: (0, 0)), row],
        out_specs=[row, pl.BlockSpec((1, D), lambda i: (0, 0)), pl.BlockSpec((1, LANES), lambda i: (0, 0))],
        out_shape=[jax.ShapeDtypeStruct((T, D), F32), jax.ShapeDtypeStruct((1, D), F32),
                   jax.ShapeDtypeStruct((1, LANES), F32)],
        compiler_params=_params(("arbitrary",)),
    )(x, g, target)


def _row_tile(rows, cols, budget_elems=512 * 1024):
    tr = rows
    while tr * cols > budget_elems and tr % 32 == 0:
        tr //= 2
    return tr


def _prefetch_call(body, grid, in_specs, out_specs, out_shape, name, sem, aliases=None):
    return pl.pallas_call(
        body, name=name, out_shape=out_shape,
        grid_spec=pltpu.PrefetchScalarGridSpec(num_scalar_prefetch=1, grid=grid, in_specs=in_specs,
                                               out_specs=out_specs),
        input_output_aliases=aliases or {},
        compiler_params=_params(sem))


def _cast_into_full(w, layer, colwise, where, name):
    _, K, N = w.shape
    tr = _row_tile(K, N)
    nrb = K // tr
    full_shape = (1, K, 4 * N) if colwise else (1, 4 * K, N)

    def body(where_ref, w_ref, o_ref):
        del where_ref
        o_ref[...] = w_ref[...].astype(BF16)

    if colwise:
        out_map = lambda i, wh: (0, i, wh[0])
    else:
        out_map = lambda i, wh: (0, wh[0] * nrb + i, 0)
    return _prefetch_call(
        body, (nrb,), [pl.BlockSpec((None, tr, N), lambda i, wh: (layer, i, 0))],
        pl.BlockSpec((None, tr, N), out_map), pltpu.HBM(full_shape, BF16), name,
        ("parallel",))(where, w)


def _chip_sum(g3, other, colwise, where, name):
    L, K, N = g3.shape
    hk, hn = (K // 2, N) if colwise else (K, N // 2)
    tr = _row_tile(hk, hn)
    nrb = hk // tr

    def body(where_ref, g_ref, o_ref, s_ref):
        del where_ref
        s_ref[...] = (g_ref[...].astype(F32) + o_ref[...].astype(F32)).astype(BF16)

    if colwise:
        g_map = lambda l, i, wh: (l, wh[1] * nrb + i, 0)
    else:
        g_map = lambda l, i, wh: (l, i, wh[1])
    blk = pl.BlockSpec((None, tr, hn), lambda l, i, wh: (l, i, 0))
    return _prefetch_call(
        body, (L, nrb), [pl.BlockSpec((None, tr, hn), g_map), blk], blk,
        pltpu.HBM((L, hk, hn), BF16), name, ("parallel", "parallel"))(where, _in_hbm(g3), _in_hbm(other))


def _owner_sum(chip_sum, received, colwise, where, layer, n_layers, prev, name):
    _, hk, hn = chip_sum.shape
    pk, pn = (hk, hn // 4) if colwise else (hk // 4, hn)
    tr = _row_tile(pk, pn)
    nrb = pk // tr
    shard_shape = (n_layers, 2 * pk, pn) if colwise else (n_layers, pk, 2 * pn)

    def body(where_ref, own_ref, rec_ref, *rest):
        del where_ref
        o_ref = rest[-1]
        acc = own_ref[...].astype(F32)
        for j in range(3):
            acc = acc + rec_ref[j].astype(F32)
        o_ref[...] = acc

    if colwise:
        own_map = lambda i, wh: (0, i, wh[0])
        out_map = lambda i, wh: (layer, wh[1] * nrb + i, 0)
    else:
        own_map = lambda i, wh: (0, wh[0] * nrb + i, 0)
        out_map = lambda i, wh: (layer, i, wh[1])
    in_specs = [pl.BlockSpec((None, tr, pn), own_map),
                pl.BlockSpec((3, None, tr, pn), lambda i, wh: (0, 0, i, 0))]
    operands = [where, _in_hbm(chip_sum), _in_hbm(received)]
    if prev is not None:
        in_specs.append(ANY)
        operands.append(prev)
    return _prefetch_call(
        body, (nrb,), in_specs, pl.BlockSpec((None, tr, pn), out_map), pltpu.HBM(shard_shape, F32), name,
        ("parallel",), None if prev is None else {3: 0})(*operands)


def _adamw(w, g, m, v, name):
    R, Cc = w.shape
    tr = _row_tile(R, Cc, 256 * 1024)

    def body(w_ref, g_ref, m_ref, v_ref, d_ref, nm_ref, nv_ref, go_ref):
        gv = g_ref[...]
        go_ref[...] = gv
        nm = ADAM_B1 * m_ref[...] + (1.0 - ADAM_B1) * gv
        nv = ADAM_B2 * v_ref[...] + (1.0 - ADAM_B2) * (gv * gv)
        m_hat = nm / (1.0 - ADAM_B1 ** ADAM_STEP)
        v_hat = nv / (1.0 - ADAM_B2 ** ADAM_STEP)
        d_ref[...] = -ADAM_LR * (m_hat / (jnp.sqrt(v_hat) + ADAM_EPS) + ADAM_WD * w_ref[...])
        nm_ref[...] = nm
        nv_ref[...] = nv

    blk = pl.BlockSpec((tr, Cc), lambda i: (i, 0))
    shp = jax.ShapeDtypeStruct((R, Cc), F32)
    return pl.pallas_call(
        body, grid=(R // tr,), name=name,
        in_specs=[blk] * 4, out_specs=[blk] * 4, out_shape=[shp] * 4,
        compiler_params=_params(("parallel",)),
    )(w, g, m, v)


COL_SHARDED = (True, False, True, False)


def _position():
    x, y, c = lax.axis_index("x"), lax.axis_index("y"), lax.axis_index("c")
    chips = [(1 - x, y), (x, 1 - y), (1 - x, 1 - y)]
    return x, y, c, chips


def _span(index, size, align):
    return pl.ds(pl.multiple_of(index * size, align), size)


def _window(ref, colwise, shard, half, shards=4):
    _, K, N = ref.shape
    rows = cols = slice(None)
    if colwise:
        if half is not None:
            rows = _span(half, K // 2, 16)
        if shard is not None:
            cols = _span(shard, N // shards, LANES)
    else:
        if shard is not None:
            rows = _span(shard, K // shards, 16)
        if half is not None:
            cols = _span(half, N // 2, LANES)
    return ref.at[:, rows, cols]


HBM = pl.BlockSpec(memory_space=pltpu.HBM)
SEMAPHORES = pl.BlockSpec(memory_space=pltpu.SEMAPHORE)


def _gather_start(fulls, colwise, group_sizes, after, name):
    n = len(fulls)
    n_groups = len(group_sizes)

    n_in = n if after is None else n + 1

    def body(*refs):
        ins = refs[:n]
        sems = refs[n_in:n_in + 2 * n_groups]
        x, y, c, chips = _position()
        me = 2 * x + y
        i = 0
        for g, size in enumerate(group_sizes):
            for a in range(size):
                win = _window(ins[i], colwise[i], me, c)
                for j, chip in enumerate(chips):
                    pltpu.make_async_remote_copy(
                        src_ref=win, dst_ref=win, send_sem=sems[2 * g].at[a * 3 + j],
                        recv_sem=sems[2 * g + 1].at[a * 3 + j],
                        device_id=(chip[0], chip[1], c), device_id_type=MESH_ID).start()
                i += 1

    sem_shapes = []
    for size in group_sizes:
        sem_shapes += [pltpu.SemaphoreType.DMA((3 * size,)), pltpu.SemaphoreType.DMA((3 * size,))]
    operands = [pltpu.with_memory_space_constraint(f, pltpu.HBM) for f in fulls]
    in_specs = [HBM] * n
    if after is not None:
        operands.append(after)
        in_specs.append(ANY)
    outs = pl.pallas_call(
        body, name=name,
        in_specs=in_specs, out_specs=[SEMAPHORES] * (2 * n_groups) + [HBM] * n,
        out_shape=sem_shapes + [pltpu.HBM(f.shape, f.dtype) for f in fulls],
        input_output_aliases={i: 2 * n_groups + i for i in range(n)},
        compiler_params=pltpu.CompilerParams(has_side_effects=pltpu.SideEffectType.DATAFLOW_SIDE_EFFECTING),
    )(*operands)
    sems = [(outs[2 * g], outs[2 * g + 1]) for g in range(n_groups)]
    return sems, list(outs[2 * n_groups:])


def _to_sibling(ref, colwise, chip, half, x, y, c, send_sem, recv_sem):
    win = _window(ref, colwise, 2 * chip[0] + chip[1], half)
    return pltpu.make_async_remote_copy(
        src_ref=win, dst_ref=win, send_sem=send_sem, recv_sem=recv_sem,
        device_id=(x, y, 1 - c), device_id_type=MESH_ID)


def _gather_pass(in_flight, colwise, sems, after, name):
    n = len(in_flight)

    def body(*refs):
        ins = refs[:n]
        send_sems, recv_sems = refs[n], refs[n + 1]
        pass_send, pass_recv = refs[-2 - n], refs[-1 - n]
        x, y, c, chips = _position()
        me = 2 * x + y
        for a in range(n):
            for j, chip in enumerate(chips):
                k = a * 3 + j
                pltpu.make_async_remote_copy(
                    src_ref=_window(ins[a], colwise[a], me, c),
                    dst_ref=_window(ins[a], colwise[a], 2 * chip[0] + chip[1], c),
                    send_sem=send_sems.at[k], recv_sem=recv_sems.at[k],
                    device_id=(chip[0], chip[1], c), device_id_type=MESH_ID).wait()
                _to_sibling(ins[a], colwise[a], chip, c, x, y, c, pass_send.at[k], pass_recv.at[k]).start()

    operands = list(in_flight) + list(sems)
    in_specs = [HBM] * n + [SEMAPHORES] * 2
    if after is not None:
        operands.append(after)
        in_specs.append(ANY)
    outs = pl.pallas_call(
        body, name=name,
        in_specs=in_specs, out_specs=[SEMAPHORES] * 2 + [HBM] * n,
        out_shape=[pltpu.SemaphoreType.DMA((3 * n,)), pltpu.SemaphoreType.DMA((3 * n,))]
        + [pltpu.HBM(f.shape, f.dtype) for f in in_flight],
        input_output_aliases={i: 2 + i for i in range(n)},
        compiler_params=pltpu.CompilerParams(has_side_effects=pltpu.SideEffectType.DATAFLOW_SIDE_EFFECTING),
    )(*operands)
    return (outs[0], outs[1]), list(outs[2:])


def _gather_wait(in_flight, colwise, sems, after, name):
    n = len(in_flight)

    def body(*refs):
        ins = refs[:n]
        send_sems, recv_sems = refs[n], refs[n + 1]
        x, y, c, chips = _position()
        for a in range(n):
            for j, chip in enumerate(chips):
                k = a * 3 + j
                _to_sibling(ins[a], colwise[a], chip, c, x, y, c, send_sems.at[k], recv_sems.at[k]).wait_send()
                _to_sibling(ins[a], colwise[a], chip, 1 - c, x, y, c, send_sems.at[k], recv_sems.at[k]).wait_recv()

    operands = list(in_flight) + list(sems)
    in_specs = [HBM] * n + [SEMAPHORES] * 2
    if after is not None:
        operands.append(after)
        in_specs.append(ANY)
    outs = pl.pallas_call(
        body, name=name,
        in_specs=in_specs, out_specs=[HBM] * n,
        out_shape=[pltpu.HBM(f.shape, f.dtype) for f in in_flight],
        input_output_aliases={i: i for i in range(n)},
        compiler_params=pltpu.CompilerParams(has_side_effects=pltpu.SideEffectType.DATAFLOW_SIDE_EFFECTING),
    )(*operands)
    return list(outs)


def _exchange_copy(g_ref, land_ref, colwise, x, y, c, send_sem, recv_sem):
    return pltpu.make_async_remote_copy(
        src_ref=_window(g_ref, colwise, None, 1 - c), dst_ref=land_ref, send_sem=send_sem, recv_sem=recv_sem,
        device_id=(x, y, 1 - c), device_id_type=MESH_ID)


def _exchange_start(grads, colwise, name):
    n = len(grads)
    lands = []
    for g, cw in zip(grads, colwise):
        L, K, N = g.shape
        lands.append(lax.empty((L, K // 2, N) if cw else (L, K, N // 2), g.dtype))

    def body(*refs):
        src, land = refs[:n], refs[n:2 * n]
        send_sems, recv_sems = refs[2 * n], refs[2 * n + 1]
        x, y, c, _ = _position()
        for i in range(n):
            _exchange_copy(src[i], land[i], colwise[i], x, y, c, send_sems.at[i], recv_sems.at[i]).start()

    arrays = list(grads) + lands
    outs = pl.pallas_call(
        body, name=name,
        in_specs=[HBM] * (2 * n), out_specs=[SEMAPHORES] * 2 + [HBM] * (2 * n),
        out_shape=[pltpu.SemaphoreType.DMA((n,)), pltpu.SemaphoreType.DMA((n,))]
        + [pltpu.HBM(a.shape, a.dtype) for a in arrays],
        input_output_aliases={i: 2 + i for i in range(2 * n)},
        compiler_params=pltpu.CompilerParams(has_side_effects=pltpu.SideEffectType.DATAFLOW_SIDE_EFFECTING),
    )(*[pltpu.with_memory_space_constraint(a, pltpu.HBM) for a in arrays])
    return (outs[0], outs[1]), list(outs[2:2 + n]), list(outs[2 + n:])


def _exchange_wait(grads, lands, colwise, sems, after, name):
    n = len(grads)

    def body(*refs):
        src, land = refs[:n], refs[n:2 * n]
        send_sems, recv_sems = refs[2 * n], refs[2 * n + 1]
        x, y, c, _ = _position()
        for i in range(n):
            _exchange_copy(src[i], land[i], colwise[i], x, y, c, send_sems.at[i], recv_sems.at[i]).wait()

    arrays = list(grads) + list(lands)
    operands = arrays + list(sems)
    in_specs = [HBM] * (2 * n) + [SEMAPHORES] * 2
    if after is not None:
        operands.append(after)
        in_specs.append(ANY)
    outs = pl.pallas_call(
        body, name=name,
        in_specs=in_specs, out_specs=[HBM] * (2 * n),
        out_shape=[pltpu.HBM(a.shape, a.dtype) for a in arrays],
        input_output_aliases={i: i for i in range(2 * n)},
        compiler_params=pltpu.CompilerParams(has_side_effects=pltpu.SideEffectType.DATAFLOW_SIDE_EFFECTING),
    )(*operands)
    return list(outs[:n]), list(outs[n:])


def _scatter_copy(src_ref, land_ref, colwise, j, chip, c, send_sem, recv_sem):
    return pltpu.make_async_remote_copy(
        src_ref=_window(src_ref, colwise, 2 * chip[0] + chip[1], None), dst_ref=land_ref.at[j],
        send_sem=send_sem, recv_sem=recv_sem, device_id=(chip[0], chip[1], c), device_id_type=MESH_ID)


def _scatter_start(chip_sums, colwise, name):
    n = len(chip_sums)
    lands = []
    for g, cw in zip(chip_sums, colwise):
        L, hk, hn = g.shape
        lands.append(lax.empty((3, L, hk, hn // 4) if cw else (3, L, hk // 4, hn), g.dtype))

    def body(*refs):
        src, land = refs[:n], refs[n:2 * n]
        send_sems, recv_sems = refs[2 * n], refs[2 * n + 1]
        x, y, c, chips = _position()
        for i in range(n):
            for j, chip in enumerate(chips):
                _scatter_copy(src[i], land[i], colwise[i], j, chip, c, send_sems.at[i * 3 + j],
                              recv_sems.at[i * 3 + j]).start()

    arrays = list(chip_sums) + lands
    outs = pl.pallas_call(
        body, name=name,
        in_specs=[HBM] * (2 * n), out_specs=[SEMAPHORES] * 2 + [HBM] * (2 * n),
        out_shape=[pltpu.SemaphoreType.DMA((3 * n,)), pltpu.SemaphoreType.DMA((3 * n,))]
        + [pltpu.HBM(a.shape, a.dtype) for a in arrays],
        input_output_aliases={i: 2 + i for i in range(2 * n)},
        compiler_params=pltpu.CompilerParams(has_side_effects=pltpu.SideEffectType.DATAFLOW_SIDE_EFFECTING),
    )(*[pltpu.with_memory_space_constraint(a, pltpu.HBM) for a in arrays])
    return (outs[0], outs[1]), list(outs[2:2 + n]), list(outs[2 + n:])


def _scatter_wait(sources, lands, colwise, sems, after, name):
    n = len(sources)

    def body(*refs):
        src, land = refs[:n], refs[n:2 * n]
        send_sems, recv_sems = refs[2 * n], refs[2 * n + 1]
        x, y, c, chips = _position()
        for i in range(n):
            for j, chip in enumerate(chips):
                cp = _scatter_copy(src[i], land[i], colwise[i], j, chip, c, send_sems.at[i * 3 + j],
                                   recv_sems.at[i * 3 + j])
                cp.wait_send()
                cp.wait_recv()

    arrays = list(sources) + list(lands)
    operands = arrays + list(sems)
    in_specs = [HBM] * (2 * n) + [SEMAPHORES] * 2
    if after is not None:
        operands.append(after)
        in_specs.append(ANY)
    outs = pl.pallas_call(
        body, name=name,
        in_specs=in_specs, out_specs=[HBM] * (2 * n),
        out_shape=[pltpu.HBM(a.shape, a.dtype) for a in arrays],
        input_output_aliases={i: i for i in range(2 * n)},
        compiler_params=pltpu.CompilerParams(has_side_effects=pltpu.SideEffectType.DATAFLOW_SIDE_EFFECTING),
    )(*operands)
    return list(outs[:n]), list(outs[n:])


def _share_with_sibling(shards, colwise, name):
    n = len(shards)

    def body(*refs):
        out = refs[n:2 * n]
        send_sems, recv_sems = refs[2 * n:]
        x, y, c, _ = _position()

        def copy(i, half):
            win = _window(out[i], colwise[i], None, half)
            return pltpu.make_async_remote_copy(
                src_ref=win, dst_ref=win, send_sem=send_sems.at[i], recv_sem=recv_sems.at[i],
                device_id=(x, y, 1 - c), device_id_type=MESH_ID)

        for i in range(n):
            copy(i, c).start()
        for i in range(n):
            copy(i, 1 - c).wait_recv()
        for i in range(n):
            copy(i, c).wait_send()

    return pl.pallas_call(
        body, name=name,
        in_specs=[ANY] * n, out_specs=[ANY] * n,
        out_shape=[jax.ShapeDtypeStruct(s.shape, s.dtype) for s in shards],
        input_output_aliases={i: i for i in range(n)},
        scratch_shapes=[pltpu.SemaphoreType.DMA((n,)), pltpu.SemaphoreType.DMA((n,))],
    )(*shards)


def _chip_exchange(buf, me, x, y, c, chips, send_sems, recv_sems):
    def copy(j, chip, slot):
        return pltpu.make_async_remote_copy(
            src_ref=buf.at[me], dst_ref=buf.at[slot], send_sem=send_sems.at[j], recv_sem=recv_sems.at[j],
            device_id=(chip[0], chip[1], c), device_id_type=MESH_ID)

    for j, chip in enumerate(chips):
        copy(j, chip, me).start()
    for j, chip in enumerate(chips):
        copy(j, chip, 2 * chip[0] + chip[1]).wait_recv()
    for j, chip in enumerate(chips):
        copy(j, chip, me).wait_send()


def _gather_over_chips(pack, name):
    R, Cc = pack.shape

    def body(p_ref, o_ref, send_sems, recv_sems):
        x, y, c, chips = _position()
        me = 2 * x + y
        o_ref[me] = p_ref[...]
        _chip_exchange(o_ref, me, x, y, c, chips, send_sems, recv_sems)

    vmem = pl.BlockSpec(memory_space=pltpu.VMEM)
    return pl.pallas_call(
        body, name=name,
        in_specs=[vmem], out_specs=vmem, out_shape=jax.ShapeDtypeStruct((4, R, Cc), F32),
        scratch_shapes=[pltpu.SemaphoreType.DMA((3,)), pltpu.SemaphoreType.DMA((3,))],
    )(pack)


def _all_reduce_small(pack, name):
    R, Cc = pack.shape

    def body(p_ref, o_ref, sibling, buf, send_sems, recv_sems):
        x, y, c, chips = _position()
        me = 2 * x + y
        swap = pltpu.make_async_remote_copy(
            src_ref=p_ref, dst_ref=sibling, send_sem=send_sems.at[3], recv_sem=recv_sems.at[3],
            device_id=(x, y, 1 - c), device_id_type=MESH_ID)
        swap.start()
        swap.wait()
        buf[me] = p_ref[...] + sibling[...]
        _chip_exchange(buf, me, x, y, c, chips, send_sems, recv_sems)
        o_ref[...] = (buf[0] + buf[1]) + (buf[2] + buf[3])

    vmem = pl.BlockSpec(memory_space=pltpu.VMEM)
    return pl.pallas_call(
        body, name=name,
        in_specs=[vmem], out_specs=vmem, out_shape=jax.ShapeDtypeStruct((R, Cc), F32),
        scratch_shapes=[pltpu.VMEM((R, Cc), F32), pltpu.VMEM((4, R, Cc), F32), pltpu.SemaphoreType.DMA((4,)),
                        pltpu.SemaphoreType.DMA((4,))],
    )(pack)


def _local_forward_backward(x2, target2, S, pass_on, fetch, reduce_begin, reduce_commit, layers, final_g,
                            tm=ROW_TILE):
    T, D = x2.shape
    C = D // 2
    n_heads = C // GROUP
    n_layers = len(layers)
    weights = {}
    saved = []
    xc = x2
    for li, lw in enumerate(layers):
        if li == 0:
            pass_on(0, None)
            weights.update(fetch(0, None))
        h1, qkv3, cv3 = _norm_proj(xc, lw["norm1"], weights[li, "w_in"], ((3, C, F32), (3, C, BF16)), tm,
                                   min(C, 512), f"l{li}_norm_in_proj")
        if li == 0:
            pass_on(1, h1)
        o, lse, mix = _attn_fwd(qkv3, lw["attn_g"], 2, S, n_heads, f"l{li}_attn_fwd")
        pin = None
        if li == 0:
            weights.update(fetch(1, o))
            pin = pass_on(3, pass_on(2, o))
        mix = _mix_conv_fwd(cv3, lw["taps"], lw["conv_g"], mix, S, f"l{li}_mix_conv_fwd", pin)
        x_mid = _proj_residual(mix, weights[li, "w_out"], xc, tm, f"l{li}_out_proj")
        if li == 0:
            weights.update(fetch(2, x_mid))
        Fd = weights[li, "ffn_up"].shape[2] // 2
        h2, up3 = _norm_proj(x_mid, lw["norm2"], weights[li, "ffn_up"], ((2, Fd, BF16),), tm, DOT_CHUNK,
                             f"l{li}_norm_ffn_up")
        if li == 0:
            weights.update(fetch(3, up3))
        act = _ffn_act_fwd(up3, lw["ffn_taps"], S, f"l{li}_ffn_act_fwd")
        pin = pass_on(li + 4, act) if li + 1 < n_layers else None
        x_out = _proj_residual(act.reshape(1, T, Fd), weights[li, "ffn_down"], x_mid, tm, f"l{li}_ffn_down", pin)
        if li + 1 < n_layers:
            weights.update(fetch(li + 4, x_out))
        saved.append(dict(x_in=xc, h1=h1, qkv3=qkv3, cv3=cv3, o=o, lse=lse, mix=mix, x_mid=x_mid, h2=h2, up3=up3,
                          act=act))
        xc = x_out

    dx, d_final_g, loss_part = _final_norm_loss(xc, final_g, target2, tm, "final_norm_loss")

    small = [None] * n_layers
    started = None
    for li in reversed(range(n_layers)):
        lw, sv = layers[li], saved[li]
        w_in, w_out, ffn_up, ffn_down = (weights[li, n] for n in ("w_in", "w_out", "ffn_up", "ffn_down"))
        dxb, dact3 = _grad_through_weight(dx, ffn_down, 1, Fd, BF16, tm, DOT_CHUNK, f"l{li}_d_act", started)
        Fd = ffn_down.shape[1]
        d_ffn_down = _weight_grad(sv["act"].reshape(1, T, Fd), dxb.reshape(1, T, D), Fd // 2, D,
                                  f"l{li}_d_ffn_down")
        dup3, d_taps_g, d_taps_v = _ffn_act_bwd(sv["up3"], dact3, lw["ffn_taps"], S, f"l{li}_ffn_act_bwd")
        d_ffn_up = _weight_grad(sv["h2"].reshape(1, D, T), dup3, D, Fd // 2, f"l{li}_d_ffn_up", a_transposed=True)
        if li == 0:
            early = reduce_begin({(li, "ffn_down"): d_ffn_down, (li, "ffn_up"): d_ffn_up})
        dx_mid, d_norm2 = _grad_through_proj_norm(dup3, ffn_up, sv["x_mid"], lw["norm2"], dx, tm,
                                                  f"l{li}_d_norm2")
        started = reduce_commit(early, dx_mid) if li == 0 else None
        dxmb, dmix3 = _grad_through_weight(dx_mid, w_out, 2, C, F32, tm, min(C, 512), f"l{li}_d_mix", started)
        d_w_out = _weight_grad(sv["mix"], dxmb.reshape(1, T, D), min(C, 256), D, f"l{li}_d_w_out")
        dproj, d_attn_g = _attn_bwd(sv["qkv3"], sv["o"], sv["lse"], dmix3, lw["attn_g"], 6, S, n_heads,
                                    f"l{li}_attn_bwd")
        dproj, d_taps, d_conv_g = _mix_conv_bwd(sv["cv3"], dmix3, lw["taps"], lw["conv_g"], dproj, S,
                                                f"l{li}_mix_conv_bwd")
        d_w_in = _weight_grad(sv["h1"].reshape(1, D, T), dproj, D, C, f"l{li}_d_w_in", a_transposed=True)
        late = {(li, "w_out"): d_w_out, (li, "w_in"): d_w_in}
        if li > 0:
            late.update({(li, "ffn_down"): d_ffn_down, (li, "ffn_up"): d_ffn_up})
        late = reduce_begin(late)
        dx, d_norm1 = _grad_through_proj_norm(dproj, w_in, sv["x_in"], lw["norm1"], dx_mid, tm,
                                              f"l{li}_d_norm1")
        started = reduce_commit(late, dx)
        small[li] = dict(norm1=d_norm1, taps=d_taps, attn_g=d_attn_g, conv_g=d_conv_g, norm2=d_norm2,
                         ffn_taps=jnp.concatenate([d_taps_g, d_taps_v], axis=1))
    return loss_part, dx, small, d_final_g


SMALL_ORDER = ("norm1", "attn_g", "conv_g", "norm2", "taps", "ffn_taps")


def _pack_small(small, d_final_g, loss_row):
    parts = [small[li][k].reshape(-1) for li in range(len(small)) for k in SMALL_ORDER]
    loss_rows = jnp.tile(loss_row.reshape(1, LANES), (8, 1))
    return jnp.concatenate(parts + [d_final_g.reshape(-1), loss_rows.reshape(-1)]).reshape(-1, LANES)


def _unpack_small(pack, small, d_final_g):
    flat = pack.reshape(-1)
    out, pos = [dict() for _ in small], 0
    for li in range(len(small)):
        for k in SMALL_ORDER:
            n = small[li][k].size
            out[li][k] = flat[pos:pos + n].reshape(small[li][k].shape)
            pos += n
    return out, flat[pos:pos + d_final_g.size], flat[pos + d_final_g.size]


def kernel(x, norm1_g, w_in, mix_conv_w, attn_out_g, conv_out_g, w_out, norm2_g, ffn_up, ffn_conv_w, ffn_down, final_norm_g, loss_target, m_norm1_g, m_w_in, m_mix_conv_w, m_attn_out_g, m_conv_out_g, m_w_out, m_norm2_g, m_ffn_up, m_ffn_conv_w, m_ffn_down, m_final_norm_g, v_norm1_g, v_w_in, v_mix_conv_w, v_attn_out_g, v_conv_out_g, v_w_out, v_norm2_g, v_ffn_up, v_ffn_conv_w, v_ffn_down, v_final_norm_g):
    Bl, S, D = x.shape
    L = w_in.shape[0]
    T = Bl * S
    shard = 2 * lax.axis_index("x") + lax.axis_index("y")
    where = jnp.stack([shard, lax.axis_index("c")]).astype(jnp.int32)
    big_names = ("w_in", "w_out", "ffn_up", "ffn_down")

    taps_w, ftaps_w = mix_conv_w.shape[2], ffn_conv_w.shape[2]
    tap_pack = _gather_over_chips(
        jnp.concatenate([mix_conv_w.reshape(-1), ffn_conv_w.reshape(-1)]).reshape(-1, LANES), "all_gather_taps")
    by_chip = tap_pack.reshape(4, -1)
    n_taps = mix_conv_w.size
    taps_full = by_chip[:, :n_taps].reshape(4, L, 3, taps_w).transpose(1, 2, 0, 3).reshape(L, 3, 4 * taps_w)
    ftaps_full = by_chip[:, n_taps:].reshape(4, L, 3, ftaps_w).transpose(1, 2, 0, 3).reshape(L, 3, 4 * ftaps_w)

    big_shards = dict(zip(big_names, (w_in, w_out, ffn_up, ffn_down)))
    col_of = dict(zip(big_names, COL_SHARDED))
    groups = [[(0, n)] for n in big_names] + [[(l, n) for n in big_names] for l in range(1, L)]
    sems, in_flight = [], {}
    all_started = tap_pack
    for first, last in ((0, 1), (1, len(groups))):
        keys = [k for g in groups[first:last] for k in g]
        new_sems, arrays = _gather_start(
            [_cast_into_full(big_shards[n], l, col_of[n], where, f"cast_{n}_{l}") for l, n in keys],
            [col_of[n] for _, n in keys], [len(g) for g in groups[first:last]], all_started,
            f"gather_start_{first}")
        sems += new_sems
        in_flight.update(zip(keys, arrays))
        all_started = arrays[-1]

    def pass_on(g, after):
        after = all_started if g == 0 else after
        sems[g], arrays = _gather_pass([in_flight[k] for k in groups[g]], [col_of[n] for _, n in groups[g]],
                                       sems[g], after, f"gather_pass_{g}")
        in_flight.update(zip(groups[g], arrays))
        return arrays[0]

    def fetch(g, after):
        done = _gather_wait([in_flight[k] for k in groups[g]], [col_of[n] for _, n in groups[g]], sems[g], after,
                            f"gather_wait_{g}")
        return dict(zip(groups[g], done))

    pending = []

    begun = []

    def reduce_begin(grads):
        g = len(begun)
        keys = list(grads)
        cols = [col_of[n] for _, n in keys]
        begun.append((g, keys, cols) + _exchange_start([grads[k] for k in keys], cols, f"exchange_start_{g}"))
        return begun[-1]

    def reduce_commit(handle, after):
        g, keys, cols, ex_sems, mine, lands = handle
        mine, others = _exchange_wait(mine, lands, cols, ex_sems, after, f"exchange_wait_{g}")
        chip_sums = [_chip_sum(m, o, cw, where, f"chip_sum_{k[1]}_{k[0]}")
                     for k, m, o, cw in zip(keys, mine, others, cols)]
        pending.append((keys, cols) + _scatter_start(chip_sums, cols, f"scatter_start_{g}"))
        return pending[-1][3][0]

    layers = [dict(norm1=norm1_g[l:l + 1], taps=taps_full[l], attn_g=attn_out_g[l:l + 1],
                   conv_g=conv_out_g[l:l + 1], norm2=norm2_g[l:l + 1], ffn_taps=ftaps_full[l]) for l in range(L)]

    loss_part, dx, small, d_final_g = _local_forward_backward(
        x.reshape(T, D), loss_target.reshape(T, D), S, pass_on, fetch, reduce_begin, reduce_commit, layers,
        final_norm_g.reshape(1, D))

    def finish_group(g, after):
        keys, cols, rs_sems, sources, lands = pending[g]
        sources, lands = _scatter_wait(sources, lands, cols, rs_sems, after, f"scatter_wait_{g}")
        for (l, n), cw, src, land in zip(keys, cols, sources, lands):
            reduced[n] = _owner_sum(src, land, cw, where, l, L, reduced[n], f"owner_sum_{n}_{l}")

    reduced = dict.fromkeys(big_names)
    last_started = pending[-1][3][0]
    for g in range(len(pending) - 1):
        finish_group(g, last_started)
    late_names = [n for n in big_names if any(n == name for _, name in pending[-1][0])]
    early_names = [n for n in big_names if n not in late_names]
    g_big = dict(zip(early_names, _share_with_sibling([reduced[n] for n in early_names],
                                                      [col_of[n] for n in early_names], "grad_share_early")))

    pack = _all_reduce_small(_pack_small(small, d_final_g, loss_part), "all_reduce_small_grads")
    g_small, g_final, loss = _unpack_small(pack, small, d_final_g)

    def stacked(key):
        return jnp.stack([g_small[l][key].reshape(g_small[l][key].shape[-2:] if key.endswith("taps") else (-1,))
                          for l in range(L)])

    g_norm1, g_attn, g_conv, g_norm2 = stacked("norm1"), stacked("attn_g"), stacked("conv_g"), stacked("norm2")
    g_taps = lax.dynamic_slice(stacked("taps"), (0, 0, shard * taps_w), (L, 3, taps_w))
    g_ftaps = lax.dynamic_slice(stacked("ffn_taps"), (0, 0, shard * ftaps_w), (L, 3, ftaps_w))

    grads_out = dict(norm1_g=g_norm1, w_in=None, mix_conv_w=g_taps, attn_out_g=g_attn, conv_out_g=g_conv,
                     w_out=None, norm2_g=g_norm2, ffn_up=None, ffn_conv_w=g_ftaps, ffn_down=None,
                     final_norm_g=g_final)
    weights = dict(norm1_g=norm1_g, w_in=w_in, mix_conv_w=mix_conv_w, attn_out_g=attn_out_g, conv_out_g=conv_out_g,
                   w_out=w_out, norm2_g=norm2_g, ffn_up=ffn_up, ffn_conv_w=ffn_conv_w, ffn_down=ffn_down,
                   final_norm_g=final_norm_g)
    ms = dict(norm1_g=m_norm1_g, w_in=m_w_in, mix_conv_w=m_mix_conv_w, attn_out_g=m_attn_out_g,
              conv_out_g=m_conv_out_g, w_out=m_w_out, norm2_g=m_norm2_g, ffn_up=m_ffn_up, ffn_conv_w=m_ffn_conv_w,
              ffn_down=m_ffn_down, final_norm_g=m_final_norm_g)
    vs = dict(norm1_g=v_norm1_g, w_in=v_w_in, mix_conv_w=v_mix_conv_w, attn_out_g=v_attn_out_g,
              conv_out_g=v_conv_out_g, w_out=v_w_out, norm2_g=v_norm2_g, ffn_up=v_ffn_up, ffn_conv_w=v_ffn_conv_w,
              ffn_down=v_ffn_down, final_norm_g=v_final_norm_g)
    names = list(weights)
    small_names = [n for n in names if n not in big_names]
    delta, new_m, new_v = {}, {}, {}

    def update_big(n):
        shp = weights[n].shape
        two_d = (shp[0] * shp[1], shp[2])
        d_, m_, v_, g_ = _adamw(weights[n].reshape(two_d), g_big[n].reshape(two_d), ms[n].reshape(two_d),
                                vs[n].reshape(two_d), f"adamw_{n}")
        delta[n], new_m[n], new_v[n], grads_out[n] = (a.reshape(shp) for a in (d_, m_, v_, g_))

    for n in early_names:
        update_big(n)
    finish_group(len(pending) - 1, delta[early_names[-1]] if early_names else None)
    g_big.update(zip(late_names, _share_with_sibling([reduced[n] for n in late_names],
                                                     [col_of[n] for n in late_names], "grad_share_late")))
    for n in late_names:
        update_big(n)

    def packed(tree):
        return jnp.concatenate([tree[n].reshape(-1) for n in small_names]).reshape(-1, LANES)

    d_, m_, v_, _ = _adamw(packed(weights), packed(grads_out), packed(ms), packed(vs), "adamw_small")
    pos = 0
    for n in small_names:
        size, shp = weights[n].size, weights[n].shape
        delta[n] = d_.reshape(-1)[pos:pos + size].reshape(shp)
        new_m[n] = m_.reshape(-1)[pos:pos + size].reshape(shp)
        new_v[n] = v_.reshape(-1)[pos:pos + size].reshape(shp)
        pos += size

    return (loss, dx.reshape(Bl, S, D), *[grads_out[n] for n in names], *[delta[n] for n in names],
            *[new_m[n] for n in names], *[new_v[n] for n in names])
```

```python
import functools
import math

import jax
import jax.numpy as jnp
from jax import lax
from jax.experimental import pallas as pl
from jax.experimental.pallas import tpu as pltpu

F32 = jnp.float32
BF16 = jnp.bfloat16
EPS = 1e-6
GROUP = 64
LANES = 128
BAND = 128
DILATIONS = (1, 4, 16)
NEG = -1e30
MIB = 1024 * 1024
MESH_ID = pl.DeviceIdType.MESH

ADAM_LR = 0.001
ADAM_B1 = 0.9
ADAM_B2 = 0.999
ADAM_EPS = 1e-08
ADAM_WD = 0.01
ADAM_STEP = 10


ANY = pl.BlockSpec(memory_space=pl.ANY)


def _in_hbm(x):
    return pltpu.with_memory_space_constraint(x, pltpu.HBM)


VMEM_LIMIT_MIB = 48
VMEM_LIMIT_RESIDENT_WEIGHT_MIB = 56
ROW_TILE = 512
DOT_CHUNK = 256
ATTN_UNROLL = 16
ATTN_ROWS = 256


def _params(sem=None, vmem_mb=VMEM_LIMIT_MIB):
    return pltpu.CompilerParams(dimension_semantics=sem, vmem_limit_bytes=vmem_mb * MIB)


def _nt(a, b):
    return lax.dot_general(a, b, (((1,), (1,)), ((), ())), preferred_element_type=F32)


def _tn(a, b):
    return lax.dot_general(a, b, (((0,), (0,)), ((), ())), preferred_element_type=F32)


def _seg_sum(x, is_a):
    s_a = jnp.sum(jnp.where(is_a, x, 0.0), axis=-1, keepdims=True)
    s_b = jnp.sum(jnp.where(is_a, 0.0, x), axis=-1, keepdims=True)
    return jnp.where(is_a, s_a, s_b)


def _lane_is_a():
    return lax.broadcasted_iota(jnp.int32, (1, LANES), 1) < GROUP


def _norm_proj(x, g, w3, groups, tm, chunk, name):
    T, D = x.shape
    N = w3.shape[2]
    assert sum(p * c for p, c, _ in groups) == N and T % tm == 0

    def body(x_ref, g_ref, w_ref, h_ref, *out_refs):
        xv = x_ref[...]
        rstd = lax.rsqrt(jnp.mean(xv * xv, axis=-1, keepdims=True) + EPS)
        h = ((xv * rstd) * g_ref[...]).astype(BF16)
        h_ref[...] = h.T
        col = 0
        for (pieces, width, dtype), o_ref in zip(groups, out_refs):
            for p in range(pieces):
                for c0 in range(0, width, chunk):
                    acc = jnp.dot(h, w_ref[:, col + c0:col + c0 + chunk], preferred_element_type=F32)
                    o_ref[p, :, c0:c0 + chunk] = acc.astype(dtype)
                col += width

    out_shape = [jax.ShapeDtypeStruct((D, T), BF16)]
    out_specs = [pl.BlockSpec((D, tm), lambda i: (0, i))]
    for pieces, width, dtype in groups:
        assert width % chunk == 0
        out_shape.append(jax.ShapeDtypeStruct((pieces, T, width), dtype))
        out_specs.append(pl.BlockSpec((pieces, tm, width), lambda i: (0, i, 0)))
    return pl.pallas_call(
        body, grid=(T // tm,), name=name,
        in_specs=[pl.BlockSpec((tm, D), lambda i: (i, 0)),
                  pl.BlockSpec((1, D), lambda i: (0, 0)),
                  pl.BlockSpec((None, D, N), lambda i: (0, 0, 0))],
        out_specs=out_specs, out_shape=out_shape,
        compiler_params=_params(("parallel",), VMEM_LIMIT_RESIDENT_WEIGHT_MIB),
    )(x, g, w3)


def _proj_residual(pieces3, w3, x, tm, name, after=None):
    P, T, C = pieces3.shape
    D = w3.shape[2]

    def body(a_ref, w_ref, x_ref, *rest):
        o_ref = rest[-1]
        acc = x_ref[...]
        for p in range(P):
            acc = acc + jnp.dot(a_ref[p], w_ref[p * C:(p + 1) * C, :], preferred_element_type=F32)
        o_ref[...] = acc

    in_specs = [pl.BlockSpec((P, tm, C), lambda i: (0, i, 0)),
                pl.BlockSpec((None, P * C, D), lambda i: (0, 0, 0)),
                pl.BlockSpec((tm, D), lambda i: (i, 0))]
    operands = [pieces3, w3, x]
    if after is not None:
        in_specs.append(ANY)
        operands.append(after)
    return pl.pallas_call(
        body, grid=(T // tm,), name=name,
        in_specs=in_specs,
        out_specs=pl.BlockSpec((tm, D), lambda i: (i, 0)),
        out_shape=jax.ShapeDtypeStruct((T, D), F32),
        compiler_params=_params(("parallel",)),
    )(*operands)


def _grad_through_weight(dy, w3, pieces, width, out_dtype, tm, chunk, name, after=None):
    T, D = dy.shape

    def body(dy_ref, w_ref, *rest):
        dyb_ref, o_ref = rest[-2:]
        dyb = dy_ref[...].astype(BF16)
        dyb_ref[...] = dyb
        for p in range(pieces):
            for c0 in range(0, width, chunk):
                r0 = p * width + c0
                o_ref[p, :, c0:c0 + chunk] = _nt(dyb, w_ref[r0:r0 + chunk, :]).astype(out_dtype)

    in_specs = [pl.BlockSpec((tm, D), lambda i: (i, 0)),
                pl.BlockSpec((None, pieces * width, D), lambda i: (0, 0, 0))]
    operands = [dy, w3]
    if after is not None:
        in_specs.append(ANY)
        operands.append(after)
    return pl.pallas_call(
        body, grid=(T // tm,), name=name,
        in_specs=in_specs,
        out_specs=[pl.BlockSpec((tm, D), lambda i: (i, 0)),
                   pl.BlockSpec((pieces, tm, width), lambda i: (0, i, 0))],
        out_shape=[jax.ShapeDtypeStruct((T, D), BF16),
                   jax.ShapeDtypeStruct((pieces, T, width), out_dtype)],
        compiler_params=_params(("parallel",)),
    )(*operands)


def _grad_through_proj_norm(dp3, w3, x, g, dx_in, tm, name):
    P, T, C = dp3.shape
    D = w3.shape[1]

    def body(dp_ref, w_ref, x_ref, g_ref, dxin_ref, dx_ref, dg_ref):
        dh = _nt(dp_ref[0], w_ref[:, 0:C])
        for p in range(1, P):
            dh = dh + _nt(dp_ref[p], w_ref[:, p * C:(p + 1) * C])
        xv = x_ref[...]
        rstd = lax.rsqrt(jnp.mean(xv * xv, axis=-1, keepdims=True) + EPS)
        xn = xv * rstd
        a = dh * g_ref[...]
        dx_ref[...] = dxin_ref[...] + rstd * (a - xn * jnp.mean(a * xn, axis=-1, keepdims=True))
        part = jnp.sum(dh * xn, axis=0, keepdims=True)

        @pl.when(pl.program_id(0) == 0)
        def _():
            dg_ref[...] = part

        @pl.when(pl.program_id(0) != 0)
        def _():
            dg_ref[...] += part

    return pl.pallas_call(
        body, grid=(T // tm,), name=name,
        in_specs=[pl.BlockSpec((P, tm, C), lambda i: (0, i, 0)),
                  pl.BlockSpec((None, D, P * C), lambda i: (0, 0, 0)),
                  pl.BlockSpec((tm, D), lambda i: (i, 0)),
                  pl.BlockSpec((1, D), lambda i: (0, 0)),
                  pl.BlockSpec((tm, D), lambda i: (i, 0))],
        out_specs=[pl.BlockSpec((tm, D), lambda i: (i, 0)),
                   pl.BlockSpec((1, D), lambda i: (0, 0))],
        out_shape=[jax.ShapeDtypeStruct((T, D), F32), jax.ShapeDtypeStruct((1, D), F32)],
        compiler_params=_params(("arbitrary",), VMEM_LIMIT_RESIDENT_WEIGHT_MIB),
    )(dp3, w3, x, g, dx_in)


def _weight_grad(a3, g3, ta, tg, name, a_transposed=False):
    PG, T, CG = g3.shape
    PA, CA = (a3.shape[0], a3.shape[1]) if a_transposed else (a3.shape[0], a3.shape[2])
    na, ng = CA // ta, CG // tg
    assert CA % ta == 0 and CG % tg == 0

    def body(a_ref, g_ref, o_ref):
        if a_transposed:
            part = jnp.dot(a_ref[...], g_ref[...], preferred_element_type=F32)
        else:
            part = _tn(a_ref[...], g_ref[...])
        o_ref[...] = part.astype(o_ref.dtype)

    a_spec = (pl.BlockSpec((None, ta, T), lambda i, j: (i // na, i % na, 0)) if a_transposed
              else pl.BlockSpec((None, T, ta), lambda i, j: (i // na, 0, i % na)))
    return pl.pallas_call(
        body, grid=(PA * na, PG * ng), name=name,
        in_specs=[a_spec, pl.BlockSpec((None, T, tg), lambda i, j: (j // ng, 0, j % ng))],
        out_specs=pl.BlockSpec((None, ta, tg), lambda i, j: (0, i, j)),
        out_shape=pltpu.HBM((1, PA * CA, PG * CG), BF16),
        compiler_params=_params(("parallel", "parallel"), VMEM_LIMIT_RESIDENT_WEIGHT_MIB),
    )(a3, g3)


def _bias_tables(bm_ref, lone_ref, pair, n_heads, S):
    ii = lax.broadcasted_iota(jnp.int32, (BAND, 2 * BAND), 0)
    jj = lax.broadcasted_iota(jnp.int32, (BAND, 2 * BAND), 1)
    dist = BAND + ii - jj
    valid = (dist >= 0) & (dist <= BAND)
    distf = dist.astype(F32)
    for hh in range(2):
        head = (2 * pair + hh + 1).astype(F32)
        slope = jnp.exp(jnp.full((1, 1), -8.0 / n_heads * math.log(2.0), F32) * head)
        for bi, d in enumerate(DILATIONS):
            table = jnp.where(valid, -(slope * d) * distf, NEG)
            bm_ref[bi, hh * BAND:(hh + 1) * BAND, :] = table
            if S // (BAND * d) == 1:
                lone_ref[bi, hh * BAND:(hh + 1) * BAND, :] = table[:, BAND:2 * BAND]


def _stack_heads(x, is_a):
    zero = jnp.zeros_like(x)
    return jnp.concatenate([jnp.where(is_a, x, zero), jnp.where(is_a, zero, x)], axis=0)


def _unstack_heads(x2, is_a):
    return jnp.where(is_a, x2[0:BAND], x2[BAND:2 * BAND])


def _gather_residues(dst_ref, src, d, S, convert):
    L = S // d
    for r in range(d):
        rows = pl.ds(r, L, stride=d) if d > 1 else slice(None)
        dst_ref[r * L:(r + 1) * L, :] = convert(src(rows))


def _block_rows(t, d, S):
    nb = S // (BAND * d)
    n = t % nb
    has_prev = jnp.minimum(n, 1)
    cur = pl.ds(pl.multiple_of(t * BAND, BAND), BAND)
    prev = pl.ds(pl.multiple_of((t - has_prev) * BAND, BAND), BAND)
    return cur, prev, has_prev


def _first_block_penalty(has_prev):
    jrow = lax.broadcasted_iota(jnp.int32, (1, 2 * BAND), 1)
    pen = jnp.where(has_prev == 0, NEG, 0.0).astype(F32)
    return jnp.where(jrow < BAND, pen, 0.0)


def _attn_fwd(qkv3, gain, mix_shape_pieces, S, n_heads, name):
    _, T, C = qkv3.shape
    B, P = T // S, C // LANES
    NBLK = S // BAND
    scale = GROUP ** -0.5
    nbr = len(DILATIONS)
    RC = ATTN_ROWS

    def body(qkv_ref, g_ref, o_ref, lse_ref, an_ref, qs, ks, vs, op, mp, lp, ob, mb, lb, bm, bml):
        pair = pl.program_id(1)
        is_a = _lane_is_a()
        _bias_tables(bm, bml, pair, n_heads, S)

        for bi, d in enumerate(DILATIONS):
            nb = S // (BAND * d)
            _gather_residues(qs, lambda rows: qkv_ref.at[0][rows, :], d, S, lambda v: (v * scale).astype(BF16))
            _gather_residues(ks, lambda rows: qkv_ref.at[1][rows, :], d, S, lambda v: v.astype(BF16))
            _gather_residues(vs, lambda rows: qkv_ref.at[2][rows, :], d, S, lambda v: v.astype(BF16))
            o_dst, m_dst, l_dst = (ob.at[bi], mb.at[bi], lb.at[bi]) if d == 1 else (op, mp, lp)

            def block(t, carry, bi=bi, d=d, nb=nb, o_dst=o_dst, m_dst=m_dst, l_dst=l_dst):
                cur, prev, has_prev = _block_rows(t, d, S)
                q2 = _stack_heads(qs[cur, :], is_a)
                if nb == 1:
                    kc, vc = ks[cur, :], vs[cur, :]
                    s = _nt(q2, kc) + bml[bi]
                else:
                    kc = jnp.concatenate([ks[prev, :], ks[cur, :]], axis=0)
                    vc = jnp.concatenate([vs[prev, :], vs[cur, :]], axis=0)
                    s = _nt(q2, kc) + bm[bi] + _first_block_penalty(has_prev)
                m = jnp.max(s, axis=-1, keepdims=True)
                e = jnp.exp(s - m)
                l = jnp.sum(e, axis=-1, keepdims=True)
                pv = jnp.dot(e.astype(BF16), vc, preferred_element_type=F32)
                o_dst[cur, :] = _unstack_heads(pv, is_a)
                m_dst[cur, :] = _unstack_heads(m, is_a)
                l_dst[cur, :] = _unstack_heads(l, is_a)
                return carry

            lax.fori_loop(0, NBLK, block, 0, unroll=ATTN_UNROLL)
            if d > 1:
                L = S // d
                for r in range(d):
                    rows = pl.ds(r, L, stride=d)
                    ob.at[bi][rows, :] = op[r * L:(r + 1) * L, :]
                    mb.at[bi][rows, :] = mp[r * L:(r + 1) * L, :]
                    lb.at[bi][rows, :] = lp[r * L:(r + 1) * L, :]

        def finish(ci, carry):
            rs = pl.ds(pl.multiple_of(ci * RC, RC), RC)
            ms = [mb[bi, rs, :] for bi in range(nbr)]
            mmax = functools.reduce(jnp.maximum, ms)
            ws = [jnp.exp(m - mmax) for m in ms]
            num = sum(ob[bi, rs, :] * ws[bi] for bi in range(nbr))
            den = sum(lb[bi, rs, :] * ws[bi] for bi in range(nbr))
            o = num / den
            o_ref[rs, :] = o
            lse_ref[rs, :] = mmax + jnp.log(den)
            rstd = lax.rsqrt(_seg_sum(o * o, is_a) * (1.0 / GROUP) + EPS)
            an_ref[rs, :] = ((o * rstd) * g_ref[...]).astype(BF16)
            return carry

        lax.fori_loop(0, S // RC, finish, 0, unroll=True)

    seq = pl.BlockSpec((S, LANES), lambda b, p: (b, p))
    return pl.pallas_call(
        body, grid=(B, P), name=name,
        in_specs=[pl.BlockSpec((3, S, LANES), lambda b, p: (0, b, p)),
                  pl.BlockSpec((1, LANES), lambda b, p: (0, p))],
        out_specs=[seq, seq, pl.BlockSpec((None, S, LANES), lambda b, p: (0, b, p))],
        out_shape=[jax.ShapeDtypeStruct((T, C), F32), jax.ShapeDtypeStruct((T, C), F32),
                   jax.ShapeDtypeStruct((mix_shape_pieces, T, C), BF16)],
        scratch_shapes=[pltpu.VMEM((S, LANES), BF16)] * 3 + [pltpu.VMEM((S, LANES), F32)] * 3
        + [pltpu.VMEM((nbr, S, LANES), F32)] * 3
        + [pltpu.VMEM((nbr, 2 * BAND, 2 * BAND), F32), pltpu.VMEM((nbr, 2 * BAND, BAND), F32)],
        compiler_params=_params(("parallel", "parallel")),
    )(qkv3, gain)


def _attn_bwd(qkv3, o, lse, dmix3, gain, dproj_pieces, S, n_heads, name):
    _, T, C = qkv3.shape
    B, P = T // S, C // LANES
    NBLK = S // BAND
    scale = GROUP ** -0.5
    nbr = len(DILATIONS)
    RC = ATTN_ROWS

    def body(qkv_ref, o_ref, lse_ref, dn_ref, g_ref, dqkv_ref, dg_ref,
             do_n, dd_n, qs, ks, vs, dos, lses, dds, dqp, dkp, dvp, dqn, dkn, dvn, bm, bml):
        pair = pl.program_id(0)
        b = pl.program_id(1)
        is_a = _lane_is_a()
        _bias_tables(bm, bml, pair, n_heads, S)

        def prologue(ci, dg_acc):
            rs = pl.ds(pl.multiple_of(ci * RC, RC), RC)
            ov = o_ref[rs, :]
            dn = dn_ref[rs, :]
            rstd = lax.rsqrt(_seg_sum(ov * ov, is_a) * (1.0 / GROUP) + EPS)
            on = ov * rstd
            a = dn * g_ref[...]
            s_a = _seg_sum(a * on, is_a)
            do_n[rs, :] = rstd * (a - on * (s_a * (1.0 / GROUP)))
            dd_n[rs, :] = (EPS * s_a) * (rstd * rstd)
            zero = jnp.zeros((RC, LANES), F32)
            dqn[rs, :] = zero
            dkn[rs, :] = zero
            dvn[rs, :] = zero
            return dg_acc + jnp.sum(dn * on, axis=0, keepdims=True)

        dg_part = lax.fori_loop(0, S // RC, prologue, jnp.zeros((1, LANES), F32), unroll=True)

        @pl.when(b == 0)
        def _():
            dg_ref[...] = dg_part

        @pl.when(b != 0)
        def _():
            dg_ref[...] += dg_part

        for bi, d in enumerate(DILATIONS):
            nb = S // (BAND * d)
            L = S // d
            _gather_residues(qs, lambda rows: qkv_ref.at[0][rows, :], d, S, lambda v: (v * scale).astype(BF16))
            _gather_residues(ks, lambda rows: qkv_ref.at[1][rows, :], d, S, lambda v: v.astype(BF16))
            _gather_residues(vs, lambda rows: qkv_ref.at[2][rows, :], d, S, lambda v: v.astype(BF16))
            _gather_residues(dos, lambda rows: do_n[rows, :], d, S, lambda v: v.astype(BF16))
            if d == 1:
                lse_src, dd_src, dq_dst, dk_dst, dv_dst = lse_ref, dd_n, dqn, dkn, dvn
            else:
                _gather_residues(lses, lambda rows: lse_ref[rows, :], d, S, lambda v: v)
                _gather_residues(dds, lambda rows: dd_n[rows, :], d, S, lambda v: v)
                dkp[...] = jnp.zeros((S, LANES), F32)
                dvp[...] = jnp.zeros((S, LANES), F32)
                lse_src, dd_src, dq_dst, dk_dst, dv_dst = lses, dds, dqp, dkp, dvp

            def block(t, carry, bi=bi, d=d, nb=nb, lse_src=lse_src, dd_src=dd_src, dq_dst=dq_dst, dk_dst=dk_dst,
                      dv_dst=dv_dst):
                cur, prev, has_prev = _block_rows(t, d, S)
                q2 = _stack_heads(qs[cur, :], is_a)
                do2 = _stack_heads(dos[cur, :], is_a)
                lse_t = lse_src[cur, :]
                dd_t = dd_src[cur, :]
                lse2 = jnp.concatenate([lse_t[:, 0:1], lse_t[:, GROUP:GROUP + 1]], axis=0)
                dd2 = jnp.concatenate([dd_t[:, 0:1], dd_t[:, GROUP:GROUP + 1]], axis=0)
                if nb == 1:
                    kc, vc = ks[cur, :], vs[cur, :]
                    s = _nt(q2, kc) + bml[bi]
                else:
                    kc = jnp.concatenate([ks[prev, :], ks[cur, :]], axis=0)
                    vc = jnp.concatenate([vs[prev, :], vs[cur, :]], axis=0)
                    s = _nt(q2, kc) + bm[bi] + _first_block_penalty(has_prev)
                p = jnp.exp(s - lse2)
                ds = (p * (_nt(do2, vc) - dd2)).astype(BF16)
                dq = _unstack_heads(jnp.dot(ds, kc, preferred_element_type=F32), is_a)
                dk = _tn(ds, q2)
                dv = _tn(p.astype(BF16), do2)
                dq_dst[cur, :] = dq
                if nb == 1:
                    dk_dst[cur, :] += dk
                    dv_dst[cur, :] += dv
                else:
                    dk_dst[prev, :] += dk[0:BAND, :]
                    dv_dst[prev, :] += dv[0:BAND, :]
                    dk_dst[cur, :] += dk[BAND:2 * BAND, :]
                    dv_dst[cur, :] += dv[BAND:2 * BAND, :]
                return carry

            lax.fori_loop(0, NBLK, block, 0, unroll=ATTN_UNROLL)
            if d > 1:
                for r in range(d):
                    rows = pl.ds(r, L, stride=d)
                    dqn[rows, :] += dqp[r * L:(r + 1) * L, :]
                    dkn[rows, :] += dkp[r * L:(r + 1) * L, :]
                    dvn[rows, :] += dvp[r * L:(r + 1) * L, :]

        dqkv_ref[0] = (dqn[...] * scale).astype(BF16)
        dqkv_ref[1] = dkn[...].astype(BF16)
        dqkv_ref[2] = dvn[...].astype(BF16)

    seq = pl.BlockSpec((S, LANES), lambda p, b: (b, p))
    f32_seq = pltpu.VMEM((S, LANES), F32)
    bf_seq = pltpu.VMEM((S, LANES), BF16)
    return pl.pallas_call(
        body, grid=(P, B), name=name,
        in_specs=[pl.BlockSpec((3, S, LANES), lambda p, b: (0, b, p)), seq, seq,
                  pl.BlockSpec((None, S, LANES), lambda p, b: (0, b, p)),
                  pl.BlockSpec((1, LANES), lambda p, b: (0, p))],
        out_specs=[pl.BlockSpec((3, S, LANES), lambda p, b: (0, b, p)),
                   pl.BlockSpec((1, LANES), lambda p, b: (0, p))],
        out_shape=[jax.ShapeDtypeStruct((dproj_pieces, T, C), BF16), jax.ShapeDtypeStruct((1, C), F32)],
        scratch_shapes=[f32_seq, f32_seq, bf_seq, bf_seq, bf_seq, bf_seq, f32_seq, f32_seq,
                        f32_seq, f32_seq, f32_seq, f32_seq, f32_seq, f32_seq,
                        pltpu.VMEM((nbr, 2 * BAND, 2 * BAND), F32), pltpu.VMEM((nbr, 2 * BAND, BAND), F32)],
        compiler_params=_params(("parallel", "arbitrary")),
    )(qkv3, o, lse, dmix3, gain)


def _delay(x, k, row):
    return jnp.where(row >= k, pltpu.roll(x, k, 0), 0.0)


def _advance(x, k, row, S):
    return jnp.where(row < S - k, pltpu.roll(x, S - k, 0), 0.0)


def _conv3(x, w, row):
    return (w[0:1, :] * _delay(x, 2, row) + w[1:2, :] * _delay(x, 1, row)) + w[2:3, :] * x


HALO = 8


CONV_ROWS = 64
FFN_LANES = 128


def _zero_halo(pad_ref, S):
    zeros = jnp.zeros((HALO, pad_ref.shape[1]), pad_ref.dtype)
    pad_ref[0:HALO, :] = zeros
    pad_ref[HALO + S:2 * HALO + S, :] = zeros


def _window_at(pad_ref, r0, shift):
    return pad_ref[HALO + r0 + shift:HALO + r0 + shift + CONV_ROWS, :]


def _conv3_at(pad_ref, w, r0):
    return ((w[0:1, :] * _window_at(pad_ref, r0, -2) + w[1:2, :] * _window_at(pad_ref, r0, -1))
            + w[2:3, :] * _window_at(pad_ref, r0, 0))


def _conv3_grads_at(dz_ref, x_ref, w, r0):
    dz, dz1, dz2 = (_window_at(dz_ref, r0, k) for k in range(3))
    x = _window_at(x_ref, r0, 0)
    dx = (w[2:3, :] * dz + w[1:2, :] * dz1) + w[0:1, :] * dz2
    parts = [jnp.sum((d * x).reshape(CONV_ROWS // 8, 8, x.shape[1]), axis=0) for d in (dz2, dz1, dz)]
    return dx, parts


def _conv3_grads(dz, x, w, row, S):
    dz1 = _advance(dz, 1, row, S)
    dz2 = _advance(dz, 2, row, S)
    dx = (w[2:3, :] * dz + w[1:2, :] * dz1) + w[0:1, :] * dz2
    dw = jnp.concatenate([jnp.sum(dz2 * x, axis=0, keepdims=True),
                          jnp.sum(dz1 * x, axis=0, keepdims=True),
                          jnp.sum(dz * x, axis=0, keepdims=True)], axis=0)
    return dx, dw


def _mix_conv_fwd(cv3, taps, gain, mix, S, name, after=None):
    _, T, C = cv3.shape
    B, P = T // S, C // LANES

    def body(cv_ref, w_ref, g_ref, mix_hbm, *rest):
        y_ref, pad_c = rest[-2:]
        del mix_hbm
        is_a = _lane_is_a()
        _zero_halo(pad_c, S)
        pad_c[HALO:HALO + S, :] = cv_ref[1].astype(F32) * cv_ref[2].astype(F32)
        w = w_ref[...]
        for r0 in range(0, S, CONV_ROWS):
            y = cv_ref[0, r0:r0 + CONV_ROWS, :].astype(F32) * _conv3_at(pad_c, w, r0)
            rstd = lax.rsqrt(_seg_sum(y * y, is_a) * (1.0 / GROUP) + EPS)
            y_ref[r0:r0 + CONV_ROWS, :] = ((y * rstd) * g_ref[...]).astype(BF16)

    in_specs = [pl.BlockSpec((3, S, LANES), lambda b, p: (0, b, p)),
                pl.BlockSpec((3, LANES), lambda b, p: (0, p)),
                pl.BlockSpec((1, LANES), lambda b, p: (0, p)),
                ANY]
    operands = [cv3, taps, gain, mix]
    if after is not None:
        in_specs.append(ANY)
        operands.append(after)
    return pl.pallas_call(
        body, grid=(B, P), name=name,
        in_specs=in_specs,
        out_specs=pl.BlockSpec((None, S, LANES), lambda b, p: (1, b, p)),
        out_shape=jax.ShapeDtypeStruct(mix.shape, mix.dtype),
        scratch_shapes=[pltpu.VMEM((S + 2 * HALO, LANES), F32)],
        input_output_aliases={3: 0},
        compiler_params=_params(("parallel", "parallel")),
    )(*operands)


def _mix_conv_bwd(cv3, dmix3, taps, gain, dproj, S, name):
    _, T, C = cv3.shape
    B, P = T // S, C // LANES

    def body(cv_ref, dn_ref, w_ref, g_ref, dproj_hbm, dcv_ref, dw_ref, dg_ref):
        del dproj_hbm
        b = pl.program_id(1)
        row = lax.broadcasted_iota(jnp.int32, (S, 1), 0)
        is_a = _lane_is_a()
        w = w_ref[...]
        gb = cv_ref[0].astype(F32)
        gc = cv_ref[1].astype(F32)
        u = cv_ref[2].astype(F32)
        c = gc * u
        z = _conv3(c, w, row)
        y = gb * z
        rstd = lax.rsqrt(_seg_sum(y * y, is_a) * (1.0 / GROUP) + EPS)
        yn = y * rstd
        dn = dn_ref[...]
        a = dn * g_ref[...]
        dy = rstd * (a - yn * (_seg_sum(a * yn, is_a) * (1.0 / GROUP)))
        dg = jnp.sum(dn * yn, axis=0, keepdims=True)
        dc, dw = _conv3_grads(dy * gb, c, w, row, S)
        dcv_ref[0] = (dy * z).astype(BF16)
        dcv_ref[1] = (dc * u).astype(BF16)
        dcv_ref[2] = (dc * gc).astype(BF16)

        @pl.when(b == 0)
        def _():
            dw_ref[...] = dw
            dg_ref[...] = dg

        @pl.when(b != 0)
        def _():
            dw_ref[...] += dw
            dg_ref[...] += dg

    return pl.pallas_call(
        body, grid=(P, B), name=name,
        in_specs=[pl.BlockSpec((3, S, LANES), lambda p, b: (0, b, p)),
                  pl.BlockSpec((None, S, LANES), lambda p, b: (1, b, p)),
                  pl.BlockSpec((3, LANES), lambda p, b: (0, p)),
                  pl.BlockSpec((1, LANES), lambda p, b: (0, p)),
                  pl.BlockSpec(memory_space=pl.ANY)],
        out_specs=[pl.BlockSpec((3, S, LANES), lambda p, b: (1, b, p)),
                   pl.BlockSpec((3, LANES), lambda p, b: (0, p)),
                   pl.BlockSpec((1, LANES), lambda p, b: (0, p))],
        out_shape=[jax.ShapeDtypeStruct(dproj.shape, dproj.dtype),
                   jax.ShapeDtypeStruct((3, C), F32), jax.ShapeDtypeStruct((1, C), F32)],
        input_output_aliases={4: 0},
        compiler_params=_params(("parallel", "arbitrary")),
    )(cv3, dmix3, taps, gain, dproj)


def _sigmoid(x):
    return 0.5 * jnp.tanh(0.5 * x) + 0.5


def _ffn_act_fwd(up3, taps, S, name):
    _, T, Fd = up3.shape
    W = FFN_LANES
    B, P = T // S, Fd // W

    def body(up_ref, wg_ref, wv_ref, act_ref, pad_g, pad_v):
        _zero_halo(pad_g, S)
        _zero_halo(pad_v, S)
        pad_g[HALO:HALO + S, :] = up_ref[0].astype(F32)
        pad_v[HALO:HALO + S, :] = up_ref[1].astype(F32)
        wg = wg_ref[...]
        wv = wv_ref[...]
        for r0 in range(0, S, CONV_ROWS):
            cg = _conv3_at(pad_g, wg, r0)
            cv = _conv3_at(pad_v, wv, r0)
            act_ref[r0:r0 + CONV_ROWS, :] = ((cg * _sigmoid(cg)) * cv).astype(BF16)

    return pl.pallas_call(
        body, grid=(B, P), name=name,
        in_specs=[pl.BlockSpec((2, S, W), lambda b, p: (0, b, p)),
                  pl.BlockSpec((3, W), lambda b, p: (0, p)),
                  pl.BlockSpec((3, W), lambda b, p: (0, P + p))],
        out_specs=pl.BlockSpec((S, W), lambda b, p: (b, p)),
        out_shape=jax.ShapeDtypeStruct((T, Fd), BF16),
        scratch_shapes=[pltpu.VMEM((S + 2 * HALO, W), F32)] * 2,
        compiler_params=_params(("parallel", "parallel")),
    )(up3, taps, taps)


def _ffn_act_bwd(up3, dact3, taps, S, name):
    _, T, Fd = up3.shape
    W = FFN_LANES
    B, P = T // S, Fd // W

    def body(up_ref, da_ref, wg_ref, wv_ref, dup_ref, dwg_ref, dwv_ref, pad_ug, pad_uv, pad_dg, pad_dv):
        b = pl.program_id(1)
        for pad in (pad_ug, pad_uv, pad_dg, pad_dv):
            _zero_halo(pad, S)
        pad_ug[HALO:HALO + S, :] = up_ref[0].astype(F32)
        pad_uv[HALO:HALO + S, :] = up_ref[1].astype(F32)
        wg = wg_ref[...]
        wv = wv_ref[...]
        for r0 in range(0, S, CONV_ROWS):
            cg = _conv3_at(pad_ug, wg, r0)
            cv = _conv3_at(pad_uv, wv, r0)
            sg = _sigmoid(cg)
            da = da_ref[r0:r0 + CONV_ROWS, :].astype(F32)
            t = da * sg
            dcv = cg * t
            pad_dg[HALO + r0:HALO + r0 + CONV_ROWS, :] = cv * ((t + dcv) - dcv * sg)
            pad_dv[HALO + r0:HALO + r0 + CONV_ROWS, :] = dcv
        sums_g = [jnp.zeros((8, W), F32)] * 3
        sums_v = [jnp.zeros((8, W), F32)] * 3
        for r0 in range(0, S, CONV_ROWS):
            dug, parts_g = _conv3_grads_at(pad_dg, pad_ug, wg, r0)
            duv, parts_v = _conv3_grads_at(pad_dv, pad_uv, wv, r0)
            dup_ref[0, r0:r0 + CONV_ROWS, :] = dug.astype(BF16)
            dup_ref[1, r0:r0 + CONV_ROWS, :] = duv.astype(BF16)
            sums_g = [a + p for a, p in zip(sums_g, parts_g)]
            sums_v = [a + p for a, p in zip(sums_v, parts_v)]
        dwg = jnp.concatenate([jnp.sum(a, axis=0, keepdims=True) for a in sums_g], axis=0)
        dwv = jnp.concatenate([jnp.sum(a, axis=0, keepdims=True) for a in sums_v], axis=0)

        @pl.when(b == 0)
        def _():
            dwg_ref[...] = dwg
            dwv_ref[...] = dwv

        @pl.when(b != 0)
        def _():
            dwg_ref[...] += dwg
            dwv_ref[...] += dwv

    tap_out = pl.BlockSpec((3, W), lambda p, b: (0, p))
    return pl.pallas_call(
        body, grid=(P, B), name=name,
        in_specs=[pl.BlockSpec((2, S, W), lambda p, b: (0, b, p)),
                  pl.BlockSpec((None, S, W), lambda p, b: (0, b, p)),
                  pl.BlockSpec((3, W), lambda p, b: (0, p)),
                  pl.BlockSpec((3, W), lambda p, b: (0, P + p))],
        out_specs=[pl.BlockSpec((2, S, W), lambda p, b: (0, b, p)), tap_out, tap_out],
        out_shape=[jax.ShapeDtypeStruct((2, T, Fd), BF16),
                   jax.ShapeDtypeStruct((3, Fd), F32), jax.ShapeDtypeStruct((3, Fd), F32)],
        scratch_shapes=[pltpu.VMEM((S + 2 * HALO, W), F32)] * 4,
        compiler_params=_params(("parallel", "arbitrary")),
    )(up3, dact3, taps, taps)


def _final_norm_loss(x, g, target, tm, name):
    T, D = x.shape

    def body(x_ref, g_ref, t_ref, dx_ref, dg_ref, loss_ref):
        xv = x_ref[...]
        rstd = lax.rsqrt(jnp.mean(xv * xv, axis=-1, keepdims=True) + EPS)
        xn = xv * rstd
        err = xn * g_ref[...] - t_ref[...]
        part = 0.5 * jnp.sum(jnp.mean(err * err, axis=-1, keepdims=True), axis=0, keepdims=True)
        dy = err * (1.0 / D)
        a = dy * g_ref[...]
        dx_ref[...] = rstd * (a - xn * jnp.mean(a * xn, axis=-1, keepdims=True))
        dg = jnp.sum(dy * xn, axis=0, keepdims=True)
        lpart = jnp.broadcast_to(part, (1, LANES))

        @pl.when(pl.program_id(0) == 0)
        def _():
            dg_ref[...] = dg
            loss_ref[...] = lpart

        @pl.when(pl.program_id(0) != 0)
        def _():
            dg_ref[...] += dg
            loss_ref[...] += lpart

    row = pl.BlockSpec((tm, D), lambda i: (i, 0))
    return pl.pallas_call(
        body, grid=(T // tm,), name=name,
        in_specs=[row, pl.BlockSpec((1, D), lambda i: (0, 0)), row],
        out_specs=[row, pl.BlockSpec((1, D), lambda i: (0, 0)), pl.BlockSpec((1, LANES), lambda i: (0, 0))],
        out_shape=[jax.ShapeDtypeStruct((T, D), F32), jax.ShapeDtypeStruct((1, D), F32),
                   jax.ShapeDtypeStruct((1, LANES), F32)],
        compiler_params=_params(("arbitrary",)),
    )(x, g, target)


def _row_tile(rows, cols, budget_elems=512 * 1024):
    tr = rows
    while tr * cols > budget_elems and tr % 32 == 0:
        tr //= 2
    return tr


def _prefetch_call(body, grid, in_specs, out_specs, out_shape, name, sem, aliases=None):
    return pl.pallas_call(
        body, name=name, out_shape=out_shape,
        grid_spec=pltpu.PrefetchScalarGridSpec(num_scalar_prefetch=1, grid=grid, in_specs=in_specs,
                                               out_specs=out_specs),
        input_output_aliases=aliases or {},
        compiler_params=_params(sem))


def _cast_into_full(w, layer, colwise, where, name):
    _, K, N = w.shape
    tr = _row_tile(K, N)
    nrb = K // tr
    full_shape = (1, K, 4 * N) if colwise else (1, 4 * K, N)

    def body(where_ref, w_ref, o_ref):
        del where_ref
        o_ref[...] = w_ref[...].astype(BF16)

    if colwise:
        out_map = lambda i, wh: (0, i, wh[0])
    else:
        out_map = lambda i, wh: (0, wh[0] * nrb + i, 0)
    return _prefetch_call(
        body, (nrb,), [pl.BlockSpec((None, tr, N), lambda i, wh: (layer, i, 0))],
        pl.BlockSpec((None, tr, N), out_map), pltpu.HBM(full_shape, BF16), name,
        ("parallel",))(where, w)


def _chip_sum(g3, other, colwise, where, name):
    L, K, N = g3.shape
    hk, hn = (K // 2, N) if colwise else (K, N // 2)
    tr = _row_tile(hk, hn)
    nrb = hk // tr

    def body(where_ref, g_ref, o_ref, s_ref):
        del where_ref
        s_ref[...] = (g_ref[...].astype(F32) + o_ref[...].astype(F32)).astype(BF16)

    if colwise:
        g_map = lambda l, i, wh: (l, wh[1] * nrb + i, 0)
    else:
        g_map = lambda l, i, wh: (l, i, wh[1])
    blk = pl.BlockSpec((None, tr, hn), lambda l, i, wh: (l, i, 0))
    return _prefetch_call(
        body, (L, nrb), [pl.BlockSpec((None, tr, hn), g_map), blk], blk,
        pltpu.HBM((L, hk, hn), BF16), name, ("parallel", "parallel"))(where, _in_hbm(g3), _in_hbm(other))


def _owner_sum(chip_sum, received, colwise, where, layer, n_layers, prev, name):
    _, hk, hn = chip_sum.shape
    pk, pn = (hk, hn // 4) if colwise else (hk // 4, hn)
    tr = _row_tile(pk, pn)
    nrb = pk // tr
    shard_shape = (n_layers, 2 * pk, pn) if colwise else (n_layers, pk, 2 * pn)

    def body(where_ref, own_ref, rec_ref, *rest):
        del where_ref
        o_ref = rest[-1]
        acc = own_ref[...].astype(F32)
        for j in range(3):
            acc = acc + rec_ref[j].astype(F32)
        o_ref[...] = acc

    if colwise:
        own_map = lambda i, wh: (0, i, wh[0])
        out_map = lambda i, wh: (layer, wh[1] * nrb + i, 0)
    else:
        own_map = lambda i, wh: (0, wh[0] * nrb + i, 0)
        out_map = lambda i, wh: (layer, i, wh[1])
    in_specs = [pl.BlockSpec((None, tr, pn), own_map),
                pl.BlockSpec((3, None, tr, pn), lambda i, wh: (0, 0, i, 0))]
    operands = [where, _in_hbm(chip_sum), _in_hbm(received)]
    if prev is not None:
        in_specs.append(ANY)
        operands.append(prev)
    return _prefetch_call(
        body, (nrb,), in_specs, pl.BlockSpec((None, tr, pn), out_map), pltpu.HBM(shard_shape, F32), name,
        ("parallel",), None if prev is None else {3: 0})(*operands)


def _adamw(w, g, m, v, name):
    R, Cc = w.shape
    tr = _row_tile(R, Cc, 256 * 1024)

    def body(w_ref, g_ref, m_ref, v_ref, d_ref, nm_ref, nv_ref, go_ref):
        gv = g_ref[...]
        go_ref[...] = gv
        nm = ADAM_B1 * m_ref[...] + (1.0 - ADAM_B1) * gv
        nv = ADAM_B2 * v_ref[...] + (1.0 - ADAM_B2) * (gv * gv)
        m_hat = nm / (1.0 - ADAM_B1 ** ADAM_STEP)
        v_hat = nv / (1.0 - ADAM_B2 ** ADAM_STEP)
        d_ref[...] = -ADAM_LR * (m_hat / (jnp.sqrt(v_hat) + ADAM_EPS) + ADAM_WD * w_ref[...])
        nm_ref[...] = nm
        nv_ref[...] = nv

    blk = pl.BlockSpec((tr, Cc), lambda i: (i, 0))
    shp = jax.ShapeDtypeStruct((R, Cc), F32)
    return pl.pallas_call(
        body, grid=(R // tr,), name=name,
        in_specs=[blk] * 4, out_specs=[blk] * 4, out_shape=[shp] * 4,
        compiler_params=_params(("parallel",)),
    )(w, g, m, v)


COL_SHARDED = (True, False, True, False)


def _position():
    x, y, c = lax.axis_index("x"), lax.axis_index("y"), lax.axis_index("c")
    chips = [(1 - x, y), (x, 1 - y), (1 - x, 1 - y)]
    return x, y, c, chips


def _span(index, size, align):
    return pl.ds(pl.multiple_of(index * size, align), size)


def _window(ref, colwise, shard, half, shards=4):
    _, K, N = ref.shape
    rows = cols = slice(None)
    if colwise:
        if half is not None:
            rows = _span(half, K // 2, 16)
        if shard is not None:
            cols = _span(shard, N // shards, LANES)
    else:
        if shard is not None:
            rows = _span(shard, K // shards, 16)
        if half is not None:
            cols = _span(half, N // 2, LANES)
    return ref.at[:, rows, cols]


HBM = pl.BlockSpec(memory_space=pltpu.HBM)
SEMAPHORES = pl.BlockSpec(memory_space=pltpu.SEMAPHORE)


def _gather_start(fulls, colwise, group_sizes, after, name):
    n = len(fulls)
    n_groups = len(group_sizes)

    n_in = n if after is None else n + 1

    def body(*refs):
        ins = refs[:n]
        sems = refs[n_in:n_in + 2 * n_groups]
        x, y, c, chips = _position()
        me = 2 * x + y
        i = 0
        for g, size in enumerate(group_sizes):
            for a in range(size):
                win = _window(ins[i], colwise[i], me, c)
                for j, chip in enumerate(chips):
                    pltpu.make_async_remote_copy(
                        src_ref=win, dst_ref=win, send_sem=sems[2 * g].at[a * 3 + j],
                        recv_sem=sems[2 * g + 1].at[a * 3 + j],
                        device_id=(chip[0], chip[1], c), device_id_type=MESH_ID).start()
                i += 1

    sem_shapes = []
    for size in group_sizes:
        sem_shapes += [pltpu.SemaphoreType.DMA((3 * size,)), pltpu.SemaphoreType.DMA((3 * size,))]
    operands = [pltpu.with_memory_space_constraint(f, pltpu.HBM) for f in fulls]
    in_specs = [HBM] * n
    if after is not None:
        operands.append(after)
        in_specs.append(ANY)
    outs = pl.pallas_call(
        body, name=name,
        in_specs=in_specs, out_specs=[SEMAPHORES] * (2 * n_groups) + [HBM] * n,
        out_shape=sem_shapes + [pltpu.HBM(f.shape, f.dtype) for f in fulls],
        input_output_aliases={i: 2 * n_groups + i for i in range(n)},
        compiler_params=pltpu.CompilerParams(has_side_effects=pltpu.SideEffectType.DATAFLOW_SIDE_EFFECTING),
    )(*operands)
    sems = [(outs[2 * g], outs[2 * g + 1]) for g in range(n_groups)]
    return sems, list(outs[2 * n_groups:])


def _to_sibling(ref, colwise, chip, half, x, y, c, send_sem, recv_sem):
    win = _window(ref, colwise, 2 * chip[0] + chip[1], half)
    return pltpu.make_async_remote_copy(
        src_ref=win, dst_ref=win, send_sem=send_sem, recv_sem=recv_sem,
        device_id=(x, y, 1 - c), device_id_type=MESH_ID)


def _gather_pass(in_flight, colwise, sems, after, name):
    n = len(in_flight)

    def body(*refs):
        ins = refs[:n]
        send_sems, recv_sems = refs[n], refs[n + 1]
        pass_send, pass_recv = refs[-2 - n], refs[-1 - n]
        x, y, c, chips = _position()
        me = 2 * x + y
        for a in range(n):
            for j, chip in enumerate(chips):
                k = a * 3 + j
                pltpu.make_async_remote_copy(
                    src_ref=_window(ins[a], colwise[a], me, c),
                    dst_ref=_window(ins[a], colwise[a], 2 * chip[0] + chip[1], c),
                    send_sem=send_sems.at[k], recv_sem=recv_sems.at[k],
                    device_id=(chip[0], chip[1], c), device_id_type=MESH_ID).wait()
                _to_sibling(ins[a], colwise[a], chip, c, x, y, c, pass_send.at[k], pass_recv.at[k]).start()

    operands = list(in_flight) + list(sems)
    in_specs = [HBM] * n + [SEMAPHORES] * 2
    if after is not None:
        operands.append(after)
        in_specs.append(ANY)
    outs = pl.pallas_call(
        body, name=name,
        in_specs=in_specs, out_specs=[SEMAPHORES] * 2 + [HBM] * n,
        out_shape=[pltpu.SemaphoreType.DMA((3 * n,)), pltpu.SemaphoreType.DMA((3 * n,))]
        + [pltpu.HBM(f.shape, f.dtype) for f in in_flight],
        input_output_aliases={i: 2 + i for i in range(n)},
        compiler_params=pltpu.CompilerParams(has_side_effects=pltpu.SideEffectType.DATAFLOW_SIDE_EFFECTING),
    )(*operands)
    return (outs[0], outs[1]), list(outs[2:])


def _gather_wait(in_flight, colwise, sems, after, name):
    n = len(in_flight)

    def body(*refs):
        ins = refs[:n]
        send_sems, recv_sems = refs[n], refs[n + 1]
        x, y, c, chips = _position()
        for a in range(n):
            for j, chip in enumerate(chips):
                k = a * 3 + j
                _to_sibling(ins[a], colwise[a], chip, c, x, y, c, send_sems.at[k], recv_sems.at[k]).wait_send()
                _to_sibling(ins[a], colwise[a], chip, 1 - c, x, y, c, send_sems.at[k], recv_sems.at[k]).wait_recv()

    operands = list(in_flight) + list(sems)
    in_specs = [HBM] * n + [SEMAPHORES] * 2
    if after is not None:
        operands.append(after)
        in_specs.append(ANY)
    outs = pl.pallas_call(
        body, name=name,
        in_specs=in_specs, out_specs=[HBM] * n,
        out_shape=[pltpu.HBM(f.shape, f.dtype) for f in in_flight],
        input_output_aliases={i: i for i in range(n)},
        compiler_params=pltpu.CompilerParams(has_side_effects=pltpu.SideEffectType.DATAFLOW_SIDE_EFFECTING),
    )(*operands)
    return list(outs)


def _exchange_copy(g_ref, land_ref, colwise, x, y, c, send_sem, recv_sem):
    return pltpu.make_async_remote_copy(
        src_ref=_window(g_ref, colwise, None, 1 - c), dst_ref=land_ref, send_sem=send_sem, recv_sem=recv_sem,
        device_id=(x, y, 1 - c), device_id_type=MESH_ID)


def _exchange_start(grads, colwise, name):
    n = len(grads)
    lands = []
    for g, cw in zip(grads, colwise):
        L, K, N = g.shape
        lands.append(lax.empty((L, K // 2, N) if cw else (L, K, N // 2), g.dtype))

    def body(*refs):
        src, land = refs[:n], refs[n:2 * n]
        send_sems, recv_sems = refs[2 * n], refs[2 * n + 1]
        x, y, c, _ = _position()
        for i in range(n):
            _exchange_copy(src[i], land[i], colwise[i], x, y, c, send_sems.at[i], recv_sems.at[i]).start()

    arrays = list(grads) + lands
    outs = pl.pallas_call(
        body, name=name,
        in_specs=[HBM] * (2 * n), out_specs=[SEMAPHORES] * 2 + [HBM] * (2 * n),
        out_shape=[pltpu.SemaphoreType.DMA((n,)), pltpu.SemaphoreType.DMA((n,))]
        + [pltpu.HBM(a.shape, a.dtype) for a in arrays],
        input_output_aliases={i: 2 + i for i in range(2 * n)},
        compiler_params=pltpu.CompilerParams(has_side_effects=pltpu.SideEffectType.DATAFLOW_SIDE_EFFECTING),
    )(*[pltpu.with_memory_space_constraint(a, pltpu.HBM) for a in arrays])
    return (outs[0], outs[1]), list(outs[2:2 + n]), list(outs[2 + n:])


def _exchange_wait(grads, lands, colwise, sems, after, name):
    n = len(grads)

    def body(*refs):
        src, land = refs[:n], refs[n:2 * n]
        send_sems, recv_sems = refs[2 * n], refs[2 * n + 1]
        x, y, c, _ = _position()
        for i in range(n):
            _exchange_copy(src[i], land[i], colwise[i], x, y, c, send_sems.at[i], recv_sems.at[i]).wait()

    arrays = list(grads) + list(lands)
    operands = arrays + list(sems)
    in_specs = [HBM] * (2 * n) + [SEMAPHORES] * 2
    if after is not None:
        operands.append(after)
        in_specs.append(ANY)
    outs = pl.pallas_call(
        body, name=name,
        in_specs=in_specs, out_specs=[HBM] * (2 * n),
        out_shape=[pltpu.HBM(a.shape, a.dtype) for a in arrays],
        input_output_aliases={i: i for i in range(2 * n)},
        compiler_params=pltpu.CompilerParams(has_side_effects=pltpu.SideEffectType.DATAFLOW_SIDE_EFFECTING),
    )(*operands)
    return list(outs[:n]), list(outs[n:])


def _scatter_copy(src_ref, land_ref, colwise, j, chip, c, send_sem, recv_sem):
    return pltpu.make_async_remote_copy(
        src_ref=_window(src_ref, colwise, 2 * chip[0] + chip[1], None), dst_ref=land_ref.at[j],
        send_sem=send_sem, recv_sem=recv_sem, device_id=(chip[0], chip[1], c), device_id_type=MESH_ID)


def _scatter_start(chip_sums, colwise, name):
    n = len(chip_sums)
    lands = []
    for g, cw in zip(chip_sums, colwise):
        L, hk, hn = g.shape
        lands.append(lax.empty((3, L, hk, hn // 4) if cw else (3, L, hk // 4, hn), g.dtype))

    def body(*refs):
        src, land = refs[:n], refs[n:2 * n]
        send_sems, recv_sems = refs[2 * n], refs[2 * n + 1]
        x, y, c, chips = _position()
        for i in range(n):
            for j, chip in enumerate(chips):
                _scatter_copy(src[i], land[i], colwise[i], j, chip, c, send_sems.at[i * 3 + j],
                              recv_sems.at[i * 3 + j]).start()

    arrays = list(chip_sums) + lands
    outs = pl.pallas_call(
        body, name=name,
        in_specs=[HBM] * (2 * n), out_specs=[SEMAPHORES] * 2 + [HBM] * (2 * n),
        out_shape=[pltpu.SemaphoreType.DMA((3 * n,)), pltpu.SemaphoreType.DMA((3 * n,))]
        + [pltpu.HBM(a.shape, a.dtype) for a in arrays],
        input_output_aliases={i: 2 + i for i in range(2 * n)},
        compiler_params=pltpu.CompilerParams(has_side_effects=pltpu.SideEffectType.DATAFLOW_SIDE_EFFECTING),
    )(*[pltpu.with_memory_space_constraint(a, pltpu.HBM) for a in arrays])
    return (outs[0], outs[1]), list(outs[2:2 + n]), list(outs[2 + n:])


def _scatter_wait(sources, lands, colwise, sems, after, name):
    n = len(sources)

    def body(*refs):
        src, land = refs[:n], refs[n:2 * n]
        send_sems, recv_sems = refs[2 * n], refs[2 * n + 1]
        x, y, c, chips = _position()
        for i in range(n):
            for j, chip in enumerate(chips):
                cp = _scatter_copy(src[i], land[i], colwise[i], j, chip, c, send_sems.at[i * 3 + j],
                                   recv_sems.at[i * 3 + j])
                cp.wait_send()
                cp.wait_recv()

    arrays = list(sources) + list(lands)
    operands = arrays + list(sems)
    in_specs = [HBM] * (2 * n) + [SEMAPHORES] * 2
    if after is not None:
        operands.append(after)
        in_specs.append(ANY)
    outs = pl.pallas_call(
        body, name=name,
        in_specs=in_specs, out_specs=[HBM] * (2 * n),
        out_shape=[pltpu.HBM(a.shape, a.dtype) for a in arrays],
        input_output_aliases={i: i for i in range(2 * n)},
        compiler_params=pltpu.CompilerParams(has_side_effects=pltpu.SideEffectType.DATAFLOW_SIDE_EFFECTING),
    )(*operands)
    return list(outs[:n]), list(outs[n:])


def _share_with_sibling(shards, colwise, name):
    n = len(shards)

    def body(*refs):
        out = refs[n:2 * n]
        send_sems, recv_sems = refs[2 * n:]
        x, y, c, _ = _position()

        def copy(i, half):
            win = _window(out[i], colwise[i], None, half)
            return pltpu.make_async_remote_copy(
                src_ref=win, dst_ref=win, send_sem=send_sems.at[i], recv_sem=recv_sems.at[i],
                device_id=(x, y, 1 - c), device_id_type=MESH_ID)

        for i in range(n):
            copy(i, c).start()
        for i in range(n):
            copy(i, 1 - c).wait_recv()
        for i in range(n):
            copy(i, c).wait_send()

    return pl.pallas_call(
        body, name=name,
        in_specs=[ANY] * n, out_specs=[ANY] * n,
        out_shape=[jax.ShapeDtypeStruct(s.shape, s.dtype) for s in shards],
        input_output_aliases={i: i for i in range(n)},
        scratch_shapes=[pltpu.SemaphoreType.DMA((n,)), pltpu.SemaphoreType.DMA((n,))],
    )(*shards)


def _chip_exchange(buf, me, x, y, c, chips, send_sems, recv_sems):
    def copy(j, chip, slot):
        return pltpu.make_async_remote_copy(
            src_ref=buf.at[me], dst_ref=buf.at[slot], send_sem=send_sems.at[j], recv_sem=recv_sems.at[j],
            device_id=(chip[0], chip[1], c), device_id_type=MESH_ID)

    for j, chip in enumerate(chips):
        copy(j, chip, me).start()
    for j, chip in enumerate(chips):
        copy(j, chip, 2 * chip[0] + chip[1]).wait_recv()
    for j, chip in enumerate(chips):
        copy(j, chip, me).wait_send()


def _gather_over_chips(pack, name):
    R, Cc = pack.shape

    def body(p_ref, o_ref, send_sems, recv_sems):
        x, y, c, chips = _position()
        me = 2 * x + y
        o_ref[me] = p_ref[...]
        _chip_exchange(o_ref, me, x, y, c, chips, send_sems, recv_sems)

    vmem = pl.BlockSpec(memory_space=pltpu.VMEM)
    return pl.pallas_call(
        body, name=name,
        in_specs=[vmem], out_specs=vmem, out_shape=jax.ShapeDtypeStruct((4, R, Cc), F32),
        scratch_shapes=[pltpu.SemaphoreType.DMA((3,)), pltpu.SemaphoreType.DMA((3,))],
    )(pack)


def _all_reduce_small(pack, name):
    R, Cc = pack.shape

    def body(p_ref, o_ref, sibling, buf, send_sems, recv_sems):
        x, y, c, chips = _position()
        me = 2 * x + y
        swap = pltpu.make_async_remote_copy(
            src_ref=p_ref, dst_ref=sibling, send_sem=send_sems.at[3], recv_sem=recv_sems.at[3],
            device_id=(x, y, 1 - c), device_id_type=MESH_ID)
        swap.start()
        swap.wait()
        buf[me] = p_ref[...] + sibling[...]
        _chip_exchange(buf, me, x, y, c, chips, send_sems, recv_sems)
        o_ref[...] = (buf[0] + buf[1]) + (buf[2] + buf[3])

    vmem = pl.BlockSpec(memory_space=pltpu.VMEM)
    return pl.pallas_call(
        body, name=name,
        in_specs=[vmem], out_specs=vmem, out_shape=jax.ShapeDtypeStruct((R, Cc), F32),
        scratch_shapes=[pltpu.VMEM((R, Cc), F32), pltpu.VMEM((4, R, Cc), F32), pltpu.SemaphoreType.DMA((4,)),
                        pltpu.SemaphoreType.DMA((4,))],
    )(pack)


def _local_forward_backward(x2, target2, S, pass_on, fetch, reduce_begin, reduce_commit, layers, final_g,
                            tm=ROW_TILE):
    T, D = x2.shape
    C = D // 2
    n_heads = C // GROUP
    n_layers = len(layers)
    weights = {}
    saved = []
    xc = x2
    for li, lw in enumerate(layers):
        if li == 0:
            pass_on(0, None)
            weights.update(fetch(0, None))
        h1, qkv3, cv3 = _norm_proj(xc, lw["norm1"], weights[li, "w_in"], ((3, C, F32), (3, C, BF16)), tm,
                                   min(C, 512), f"l{li}_norm_in_proj")
        if li == 0:
            pass_on(1, h1)
        o, lse, mix = _attn_fwd(qkv3, lw["attn_g"], 2, S, n_heads, f"l{li}_attn_fwd")
        pin = None
        if li == 0:
            weights.update(fetch(1, o))
            pin = pass_on(3, pass_on(2, o))
        mix = _mix_conv_fwd(cv3, lw["taps"], lw["conv_g"], mix, S, f"l{li}_mix_conv_fwd", pin)
        x_mid = _proj_residual(mix, weights[li, "w_out"], xc, tm, f"l{li}_out_proj")
        if li == 0:
            weights.update(fetch(2, x_mid))
        Fd = weights[li, "ffn_up"].shape[2] // 2
        h2, up3 = _norm_proj(x_mid, lw["norm2"], weights[li, "ffn_up"], ((2, Fd, BF16),), tm, DOT_CHUNK,
                             f"l{li}_norm_ffn_up")
        if li == 0:
            weights.update(fetch(3, up3))
        act = _ffn_act_fwd(up3, lw["ffn_taps"], S, f"l{li}_ffn_act_fwd")
        pin = pass_on(li + 4, act) if li + 1 < n_layers else None
        x_out = _proj_residual(act.reshape(1, T, Fd), weights[li, "ffn_down"], x_mid, tm, f"l{li}_ffn_down", pin)
        if li + 1 < n_layers:
            weights.update(fetch(li + 4, x_out))
        saved.append(dict(x_in=xc, h1=h1, qkv3=qkv3, cv3=cv3, o=o, lse=lse, mix=mix, x_mid=x_mid, h2=h2, up3=up3,
                          act=act))
        xc = x_out

    dx, d_final_g, loss_part = _final_norm_loss(xc, final_g, target2, tm, "final_norm_loss")

    small = [None] * n_layers
    started = None
    for li in reversed(range(n_layers)):
        lw, sv = layers[li], saved[li]
        w_in, w_out, ffn_up, ffn_down = (weights[li, n] for n in ("w_in", "w_out", "ffn_up", "ffn_down"))
        dxb, dact3 = _grad_through_weight(dx, ffn_down, 1, Fd, BF16, tm, DOT_CHUNK, f"l{li}_d_act", started)
        Fd = ffn_down.shape[1]
        d_ffn_down = _weight_grad(sv["act"].reshape(1, T, Fd), dxb.reshape(1, T, D), DOT_CHUNK, D,
                                  f"l{li}_d_ffn_down")
        dup3, d_taps_g, d_taps_v = _ffn_act_bwd(sv["up3"], dact3, lw["ffn_taps"], S, f"l{li}_ffn_act_bwd")
        d_ffn_up = _weight_grad(sv["h2"].reshape(1, D, T), dup3, D, DOT_CHUNK, f"l{li}_d_ffn_up", a_transposed=True)
        if li == 0:
            early = reduce_begin({(li, "ffn_down"): d_ffn_down, (li, "ffn_up"): d_ffn_up})
        dx_mid, d_norm2 = _grad_through_proj_norm(dup3, ffn_up, sv["x_mid"], lw["norm2"], dx, tm,
                                                  f"l{li}_d_norm2")
        started = reduce_commit(early, dx_mid) if li == 0 else None
        dxmb, dmix3 = _grad_through_weight(dx_mid, w_out, 2, C, F32, tm, min(C, 512), f"l{li}_d_mix", started)
        d_w_out = _weight_grad(sv["mix"], dxmb.reshape(1, T, D), min(C, 256), D, f"l{li}_d_w_out")
        dproj, d_attn_g = _attn_bwd(sv["qkv3"], sv["o"], sv["lse"], dmix3, lw["attn_g"], 6, S, n_heads,
                                    f"l{li}_attn_bwd")
        dproj, d_taps, d_conv_g = _mix_conv_bwd(sv["cv3"], dmix3, lw["taps"], lw["conv_g"], dproj, S,
                                                f"l{li}_mix_conv_bwd")
        d_w_in = _weight_grad(sv["h1"].reshape(1, D, T), dproj, D, C, f"l{li}_d_w_in", a_transposed=True)
        late = {(li, "w_out"): d_w_out, (li, "w_in"): d_w_in}
        if li > 0:
            late.update({(li, "ffn_down"): d_ffn_down, (li, "ffn_up"): d_ffn_up})
        late = reduce_begin(late)
        dx, d_norm1 = _grad_through_proj_norm(dproj, w_in, sv["x_in"], lw["norm1"], dx_mid, tm,
                                              f"l{li}_d_norm1")
        started = reduce_commit(late, dx)
        small[li] = dict(norm1=d_norm1, taps=d_taps, attn_g=d_attn_g, conv_g=d_conv_g, norm2=d_norm2,
                         ffn_taps=jnp.concatenate([d_taps_g, d_taps_v], axis=1))
    return loss_part, dx, small, d_final_g


SMALL_ORDER = ("norm1", "attn_g", "conv_g", "norm2", "taps", "ffn_taps")


def _pack_small(small, d_final_g, loss_row):
    parts = [small[li][k].reshape(-1) for li in range(len(small)) for k in SMALL_ORDER]
    loss_rows = jnp.tile(loss_row.reshape(1, LANES), (8, 1))
    return jnp.concatenate(parts + [d_final_g.reshape(-1), loss_rows.reshape(-1)]).reshape(-1, LANES)


def _unpack_small(pack, small, d_final_g):
    flat = pack.reshape(-1)
    out, pos = [dict() for _ in small], 0
    for li in range(len(small)):
        for k in SMALL_ORDER:
            n = small[li][k].size
            out[li][k] = flat[pos:pos + n].reshape(small[li][k].shape)
            pos += n
    return out, flat[pos:pos + d_final_g.size], flat[pos + d_final_g.size]


def kernel(x, norm1_g, w_in, mix_conv_w, attn_out_g, conv_out_g, w_out, norm2_g, ffn_up, ffn_conv_w, ffn_down, final_norm_g, loss_target, m_norm1_g, m_w_in, m_mix_conv_w, m_attn_out_g, m_conv_out_g, m_w_out, m_norm2_g, m_ffn_up, m_ffn_conv_w, m_ffn_down, m_final_norm_g, v_norm1_g, v_w_in, v_mix_conv_w, v_attn_out_g, v_conv_out_g, v_w_out, v_norm2_g, v_ffn_up, v_ffn_conv_w, v_ffn_down, v_final_norm_g):
    Bl, S, D = x.shape
    L = w_in.shape[0]
    T = Bl * S
    shard = 2 * lax.axis_index("x") + lax.axis_index("y")
    where = jnp.stack([shard, lax.axis_index("c")]).astype(jnp.int32)
    big_names = ("w_in", "w_out", "ffn_up", "ffn_down")

    taps_w, ftaps_w = mix_conv_w.shape[2], ffn_conv_w.shape[2]
    tap_pack = _gather_over_chips(
        jnp.concatenate([mix_conv_w.reshape(-1), ffn_conv_w.reshape(-1)]).reshape(-1, LANES), "all_gather_taps")
    by_chip = tap_pack.reshape(4, -1)
    n_taps = mix_conv_w.size
    taps_full = by_chip[:, :n_taps].reshape(4, L, 3, taps_w).transpose(1, 2, 0, 3).reshape(L, 3, 4 * taps_w)
    ftaps_full = by_chip[:, n_taps:].reshape(4, L, 3, ftaps_w).transpose(1, 2, 0, 3).reshape(L, 3, 4 * ftaps_w)

    big_shards = dict(zip(big_names, (w_in, w_out, ffn_up, ffn_down)))
    col_of = dict(zip(big_names, COL_SHARDED))
    groups = [[(0, n)] for n in big_names] + [[(l, n) for n in big_names] for l in range(1, L)]
    sems, in_flight = [], {}
    all_started = tap_pack
    for first, last in ((0, 1), (1, len(groups))):
        keys = [k for g in groups[first:last] for k in g]
        new_sems, arrays = _gather_start(
            [_cast_into_full(big_shards[n], l, col_of[n], where, f"cast_{n}_{l}") for l, n in keys],
            [col_of[n] for _, n in keys], [len(g) for g in groups[first:last]], all_started,
            f"gather_start_{first}")
        sems += new_sems
        in_flight.update(zip(keys, arrays))
        all_started = arrays[-1]

    def pass_on(g, after):
        after = all_started if g == 0 else after
        sems[g], arrays = _gather_pass([in_flight[k] for k in groups[g]], [col_of[n] for _, n in groups[g]],
                                       sems[g], after, f"gather_pass_{g}")
        in_flight.update(zip(groups[g], arrays))
        return arrays[0]

    def fetch(g, after):
        done = _gather_wait([in_flight[k] for k in groups[g]], [col_of[n] for _, n in groups[g]], sems[g], after,
                            f"gather_wait_{g}")
        return dict(zip(groups[g], done))

    pending = []

    begun = []

    def reduce_begin(grads):
        g = len(begun)
        keys = list(grads)
        cols = [col_of[n] for _, n in keys]
        begun.append((g, keys, cols) + _exchange_start([grads[k] for k in keys], cols, f"exchange_start_{g}"))
        return begun[-1]

    def reduce_commit(handle, after):
        g, keys, cols, ex_sems, mine, lands = handle
        mine, others = _exchange_wait(mine, lands, cols, ex_sems, after, f"exchange_wait_{g}")
        chip_sums = [_chip_sum(m, o, cw, where, f"chip_sum_{k[1]}_{k[0]}")
                     for k, m, o, cw in zip(keys, mine, others, cols)]
        pending.append((keys, cols) + _scatter_start(chip_sums, cols, f"scatter_start_{g}"))
        return pending[-1][3][0]

    layers = [dict(norm1=norm1_g[l:l + 1], taps=taps_full[l], attn_g=attn_out_g[l:l + 1],
                   conv_g=conv_out_g[l:l + 1], norm2=norm2_g[l:l + 1], ffn_taps=ftaps_full[l]) for l in range(L)]

    loss_part, dx, small, d_final_g = _local_forward_backward(
        x.reshape(T, D), loss_target.reshape(T, D), S, pass_on, fetch, reduce_begin, reduce_commit, layers,
        final_norm_g.reshape(1, D))

    def finish_group(g, after):
        keys, cols, rs_sems, sources, lands = pending[g]
        sources, lands = _scatter_wait(sources, lands, cols, rs_sems, after, f"scatter_wait_{g}")
        for (l, n), cw, src, land in zip(keys, cols, sources, lands):
            reduced[n] = _owner_sum(src, land, cw, where, l, L, reduced[n], f"owner_sum_{n}_{l}")

    reduced = dict.fromkeys(big_names)
    last_started = pending[-1][3][0]
    for g in range(len(pending) - 1):
        finish_group(g, last_started)
    late_names = [n for n in big_names if any(n == name for _, name in pending[-1][0])]
    early_names = [n for n in big_names if n not in late_names]
    g_big = dict(zip(early_names, _share_with_sibling([reduced[n] for n in early_names],
                                                      [col_of[n] for n in early_names], "grad_share_early")))

    pack = _all_reduce_small(_pack_small(small, d_final_g, loss_part), "all_reduce_small_grads")
    g_small, g_final, loss = _unpack_small(pack, small, d_final_g)

    def stacked(key):
        return jnp.stack([g_small[l][key].reshape(g_small[l][key].shape[-2:] if key.endswith("taps") else (-1,))
                          for l in range(L)])

    g_norm1, g_attn, g_conv, g_norm2 = stacked("norm1"), stacked("attn_g"), stacked("conv_g"), stacked("norm2")
    g_taps = lax.dynamic_slice(stacked("taps"), (0, 0, shard * taps_w), (L, 3, taps_w))
    g_ftaps = lax.dynamic_slice(stacked("ffn_taps"), (0, 0, shard * ftaps_w), (L, 3, ftaps_w))

    grads_out = dict(norm1_g=g_norm1, w_in=None, mix_conv_w=g_taps, attn_out_g=g_attn, conv_out_g=g_conv,
                     w_out=None, norm2_g=g_norm2, ffn_up=None, ffn_conv_w=g_ftaps, ffn_down=None,
                     final_norm_g=g_final)
    weights = dict(norm1_g=norm1_g, w_in=w_in, mix_conv_w=mix_conv_w, attn_out_g=attn_out_g, conv_out_g=conv_out_g,
                   w_out=w_out, norm2_g=norm2_g, ffn_up=ffn_up, ffn_conv_w=ffn_conv_w, ffn_down=ffn_down,
                   final_norm_g=final_norm_g)
    ms = dict(norm1_g=m_norm1_g, w_in=m_w_in, mix_conv_w=m_mix_conv_w, attn_out_g=m_attn_out_g,
              conv_out_g=m_conv_out_g, w_out=m_w_out, norm2_g=m_norm2_g, ffn_up=m_ffn_up, ffn_conv_w=m_ffn_conv_w,
              ffn_down=m_ffn_down, final_norm_g=m_final_norm_g)
    vs = dict(norm1_g=v_norm1_g, w_in=v_w_in, mix_conv_w=v_mix_conv_w, attn_out_g=v_attn_out_g,
              conv_out_g=v_conv_out_g, w_out=v_w_out, norm2_g=v_norm2_g, ffn_up=v_ffn_up, ffn_conv_w=v_ffn_conv_w,
              ffn_down=v_ffn_down, final_norm_g=v_final_norm_g)
    names = list(weights)
    small_names = [n for n in names if n not in big_names]
    delta, new_m, new_v = {}, {}, {}

    def update_big(n):
        shp = weights[n].shape
        two_d = (shp[0] * shp[1], shp[2])
        d_, m_, v_, g_ = _adamw(weights[n].reshape(two_d), g_big[n].reshape(two_d), ms[n].reshape(two_d),
                                vs[n].reshape(two_d), f"adamw_{n}")
        delta[n], new_m[n], new_v[n], grads_out[n] = (a.reshape(shp) for a in (d_, m_, v_, g_))

    for n in early_names:
        update_big(n)
    finish_group(len(pending) - 1, delta[early_names[-1]] if early_names else None)
    g_big.update(zip(late_names, _share_with_sibling([reduced[n] for n in late_names],
                                                     [col_of[n] for n in late_names], "grad_share_late")))
    for n in late_names:
        update_big(n)

    def packed(tree):
        return jnp.concatenate([tree[n].reshape(-1) for n in small_names]).reshape(-1, LANES)

    d_, m_, v_, _ = _adamw(packed(weights), packed(grads_out), packed(ms), packed(vs), "adamw_small")
    pos = 0
    for n in small_names:
        size, shp = weights[n].size, weights[n].shape
        delta[n] = d_.reshape(-1)[pos:pos + size].reshape(shp)
        new_m[n] = m_.reshape(-1)[pos:pos + size].reshape(shp)
        new_v[n] = v_.reshape(-1)[pos:pos + size].reshape(shp)
        pos += size

    return (loss, dx.reshape(Bl, S, D), *[grads_out[n] for n in names], *[delta[n] for n in names],
            *[new_m[n] for n in names], *[new_v[n] for n in names])
```

```python
import functools
import math

import jax
import jax.numpy as jnp
from jax import lax
from jax.experimental import pallas as pl
from jax.experimental.pallas import tpu as pltpu

F32 = jnp.float32
BF16 = jnp.bfloat16
EPS = 1e-6
GROUP = 64
LANES = 128
BAND = 128
DILATIONS = (1, 4, 16)
NEG = -1e30
MIB = 1024 * 1024
MESH_ID = pl.DeviceIdType.MESH

ADAM_LR = 0.001
ADAM_B1 = 0.9
ADAM_B2 = 0.999
ADAM_EPS = 1e-08
ADAM_WD = 0.01
ADAM_STEP = 10


ANY = pl.BlockSpec(memory_space=pl.ANY)


def _in_hbm(x):
    return pltpu.with_memory_space_constraint(x, pltpu.HBM)


VMEM_LIMIT_MIB = 48
VMEM_LIMIT_RESIDENT_WEIGHT_MIB = 56
ROW_TILE = 512
DOT_CHUNK = 256
ATTN_UNROLL = 16
ATTN_ROWS = 256


def _params(sem=None, vmem_mb=VMEM_LIMIT_MIB):
    return pltpu.CompilerParams(dimension_semantics=sem, vmem_limit_bytes=vmem_mb * MIB)


def _nt(a, b):
    return lax.dot_general(a, b, (((1,), (1,)), ((), ())), preferred_element_type=F32)


def _tn(a, b):
    return lax.dot_general(a, b, (((0,), (0,)), ((), ())), preferred_element_type=F32)


def _seg_sum(x, is_a):
    s_a = jnp.sum(jnp.where(is_a, x, 0.0), axis=-1, keepdims=True)
    s_b = jnp.sum(jnp.where(is_a, 0.0, x), axis=-1, keepdims=True)
    return jnp.where(is_a, s_a, s_b)


def _lane_is_a():
    return lax.broadcasted_iota(jnp.int32, (1, LANES), 1) < GROUP


def _norm_proj(x, g, w3, groups, tm, chunk, name):
    T, D = x.shape
    N = w3.shape[2]
    assert sum(p * c for p, c, _ in groups) == N and T % tm == 0

    def body(x_ref, g_ref, w_ref, h_ref, *out_refs):
        xv = x_ref[...]
        rstd = lax.rsqrt(jnp.mean(xv * xv, axis=-1, keepdims=True) + EPS)
        h = ((xv * rstd) * g_ref[...]).astype(BF16)
        h_ref[...] = h.T
        col = 0
        for (pieces, width, dtype), o_ref in zip(groups, out_refs):
            for p in range(pieces):
                for c0 in range(0, width, chunk):
                    acc = jnp.dot(h, w_ref[:, col + c0:col + c0 + chunk], preferred_element_type=F32)
                    o_ref[p, :, c0:c0 + chunk] = acc.astype(dtype)
                col += width

    out_shape = [jax.ShapeDtypeStruct((D, T), BF16)]
    out_specs = [pl.BlockSpec((D, tm), lambda i: (0, i))]
    for pieces, width, dtype in groups:
        assert width % chunk == 0
        out_shape.append(jax.ShapeDtypeStruct((pieces, T, width), dtype))
        out_specs.append(pl.BlockSpec((pieces, tm, width), lambda i: (0, i, 0)))
    return pl.pallas_call(
        body, grid=(T // tm,), name=name,
        in_specs=[pl.BlockSpec((tm, D), lambda i: (i, 0)),
                  pl.BlockSpec((1, D), lambda i: (0, 0)),
                  pl.BlockSpec((None, D, N), lambda i: (0, 0, 0))],
        out_specs=out_specs, out_shape=out_shape,
        compiler_params=_params(("parallel",), VMEM_LIMIT_RESIDENT_WEIGHT_MIB),
    )(x, g, w3)


def _proj_residual(pieces3, w3, x, tm, name, after=None):
    P, T, C = pieces3.shape
    D = w3.shape[2]

    def body(a_ref, w_ref, x_ref, *rest):
        o_ref = rest[-1]
        acc = x_ref[...]
        for p in range(P):
            acc = acc + jnp.dot(a_ref[p], w_ref[p * C:(p + 1) * C, :], preferred_element_type=F32)
        o_ref[...] = acc

    in_specs = [pl.BlockSpec((P, tm, C), lambda i: (0, i, 0)),
                pl.BlockSpec((None, P * C, D), lambda i: (0, 0, 0)),
                pl.BlockSpec((tm, D), lambda i: (i, 0))]
    operands = [pieces3, w3, x]
    if after is not None:
        in_specs.append(ANY)
        operands.append(after)
    return pl.pallas_call(
        body, grid=(T // tm,), name=name,
        in_specs=in_specs,
        out_specs=pl.BlockSpec((tm, D), lambda i: (i, 0)),
        out_shape=jax.ShapeDtypeStruct((T, D), F32),
        compiler_params=_params(("parallel",)),
    )(*operands)


def _grad_through_weight(dy, w3, pieces, width, out_dtype, tm, chunk, name, after=None):
    T, D = dy.shape

    def body(dy_ref, w_ref, *rest):
        dyb_ref, o_ref = rest[-2:]
        dyb = dy_ref[...].astype(BF16)
        dyb_ref[...] = dyb
        for p in range(pieces):
            for c0 in range(0, width, chunk):
                r0 = p * width + c0
                o_ref[p, :, c0:c0 + chunk] = _nt(dyb, w_ref[r0:r0 + chunk, :]).astype(out_dtype)

    in_specs = [pl.BlockSpec((tm, D), lambda i: (i, 0)),
                pl.BlockSpec((None, pieces * width, D), lambda i: (0, 0, 0))]
    operands = [dy, w3]
    if after is not None:
        in_specs.append(ANY)
        operands.append(after)
    return pl.pallas_call(
        body, grid=(T // tm,), name=name,
        in_specs=in_specs,
        out_specs=[pl.BlockSpec((tm, D), lambda i: (i, 0)),
                   pl.BlockSpec((pieces, tm, width), lambda i: (0, i, 0))],
        out_shape=[jax.ShapeDtypeStruct((T, D), BF16),
                   jax.ShapeDtypeStruct((pieces, T, width), out_dtype)],
        compiler_params=_params(("parallel",)),
    )(*operands)


def _grad_through_proj_norm(dp3, w3, x, g, dx_in, tm, name):
    P, T, C = dp3.shape
    D = w3.shape[1]

    def body(dp_ref, w_ref, x_ref, g_ref, dxin_ref, dx_ref, dg_ref):
        dh = _nt(dp_ref[0], w_ref[:, 0:C])
        for p in range(1, P):
            dh = dh + _nt(dp_ref[p], w_ref[:, p * C:(p + 1) * C])
        xv = x_ref[...]
        rstd = lax.rsqrt(jnp.mean(xv * xv, axis=-1, keepdims=True) + EPS)
        xn = xv * rstd
        a = dh * g_ref[...]
        dx_ref[...] = dxin_ref[...] + rstd * (a - xn * jnp.mean(a * xn, axis=-1, keepdims=True))
        part = jnp.sum(dh * xn, axis=0, keepdims=True)

        @pl.when(pl.program_id(0) == 0)
        def _():
            dg_ref[...] = part

        @pl.when(pl.program_id(0) != 0)
        def _():
            dg_ref[...] += part

    return pl.pallas_call(
        body, grid=(T // tm,), name=name,
        in_specs=[pl.BlockSpec((P, tm, C), lambda i: (0, i, 0)),
                  pl.BlockSpec((None, D, P * C), lambda i: (0, 0, 0)),
                  pl.BlockSpec((tm, D), lambda i: (i, 0)),
                  pl.BlockSpec((1, D), lambda i: (0, 0)),
                  pl.BlockSpec((tm, D), lambda i: (i, 0))],
        out_specs=[pl.BlockSpec((tm, D), lambda i: (i, 0)),
                   pl.BlockSpec((1, D), lambda i: (0, 0))],
        out_shape=[jax.ShapeDtypeStruct((T, D), F32), jax.ShapeDtypeStruct((1, D), F32)],
        compiler_params=_params(("arbitrary",), VMEM_LIMIT_RESIDENT_WEIGHT_MIB),
    )(dp3, w3, x, g, dx_in)


def _weight_grad(a3, g3, ta, tg, name, a_transposed=False):
    PG, T, CG = g3.shape
    PA, CA = (a3.shape[0], a3.shape[1]) if a_transposed else (a3.shape[0], a3.shape[2])
    na, ng = CA // ta, CG // tg
    assert CA % ta == 0 and CG % tg == 0

    def body(a_ref, g_ref, o_ref):
        if a_transposed:
            part = jnp.dot(a_ref[...], g_ref[...], preferred_element_type=F32)
        else:
            part = _tn(a_ref[...], g_ref[...])
        o_ref[...] = part.astype(o_ref.dtype)

    a_spec = (pl.BlockSpec((None, ta, T), lambda i, j: (i // na, i % na, 0)) if a_transposed
              else pl.BlockSpec((None, T, ta), lambda i, j: (i // na, 0, i % na)))
    return pl.pallas_call(
        body, grid=(PA * na, PG * ng), name=name,
        in_specs=[a_spec, pl.BlockSpec((None, T, tg), lambda i, j: (j // ng, 0, j % ng))],
        out_specs=pl.BlockSpec((None, ta, tg), lambda i, j: (0, i, j)),
        out_shape=pltpu.HBM((1, PA * CA, PG * CG), BF16),
        compiler_params=_params(("parallel", "parallel"), VMEM_LIMIT_RESIDENT_WEIGHT_MIB),
    )(a3, g3)


def _bias_tables(bm_ref, lone_ref, pair, n_heads, S):
    ii = lax.broadcasted_iota(jnp.int32, (BAND, 2 * BAND), 0)
    jj = lax.broadcasted_iota(jnp.int32, (BAND, 2 * BAND), 1)
    dist = BAND + ii - jj
    valid = (dist >= 0) & (dist <= BAND)
    distf = dist.astype(F32)
    for hh in range(2):
        head = (2 * pair + hh + 1).astype(F32)
        slope = jnp.exp(jnp.full((1, 1), -8.0 / n_heads * math.log(2.0), F32) * head)
        for bi, d in enumerate(DILATIONS):
            table = jnp.where(valid, -(slope * d) * distf, NEG)
            bm_ref[bi, hh * BAND:(hh + 1) * BAND, :] = table
            if S // (BAND * d) == 1:
                lone_ref[bi, hh * BAND:(hh + 1) * BAND, :] = table[:, BAND:2 * BAND]


def _stack_heads(x, is_a):
    zero = jnp.zeros_like(x)
    return jnp.concatenate([jnp.where(is_a, x, zero), jnp.where(is_a, zero, x)], axis=0)


def _unstack_heads(x2, is_a):
    return jnp.where(is_a, x2[0:BAND], x2[BAND:2 * BAND])


def _gather_residues(dst_ref, src, d, S, convert):
    L = S // d
    for r in range(d):
        rows = pl.ds(r, L, stride=d) if d > 1 else slice(None)
        dst_ref[r * L:(r + 1) * L, :] = convert(src(rows))


def _block_rows(t, d, S):
    nb = S // (BAND * d)
    n = t % nb
    has_prev = jnp.minimum(n, 1)
    cur = pl.ds(pl.multiple_of(t * BAND, BAND), BAND)
    prev = pl.ds(pl.multiple_of((t - has_prev) * BAND, BAND), BAND)
    return cur, prev, has_prev


def _first_block_penalty(has_prev):
    jrow = lax.broadcasted_iota(jnp.int32, (1, 2 * BAND), 1)
    pen = jnp.where(has_prev == 0, NEG, 0.0).astype(F32)
    return jnp.where(jrow < BAND, pen, 0.0)


def _attn_fwd(qkv3, gain, mix_shape_pieces, S, n_heads, name):
    _, T, C = qkv3.shape
    B, P = T // S, C // LANES
    NBLK = S // BAND
    scale = GROUP ** -0.5
    nbr = len(DILATIONS)
    RC = ATTN_ROWS

    def body(qkv_ref, g_ref, o_ref, lse_ref, an_ref, qs, ks, vs, op, mp, lp, ob, mb, lb, bm, bml):
        pair = pl.program_id(1)
        is_a = _lane_is_a()
        _bias_tables(bm, bml, pair, n_heads, S)

        for bi, d in enumerate(DILATIONS):
            nb = S // (BAND * d)
            _gather_residues(qs, lambda rows: qkv_ref.at[0][rows, :], d, S, lambda v: (v * scale).astype(BF16))
            _gather_residues(ks, lambda rows: qkv_ref.at[1][rows, :], d, S, lambda v: v.astype(BF16))
            _gather_residues(vs, lambda rows: qkv_ref.at[2][rows, :], d, S, lambda v: v.astype(BF16))
            o_dst, m_dst, l_dst = (ob.at[bi], mb.at[bi], lb.at[bi]) if d == 1 else (op, mp, lp)

            def block(t, carry, bi=bi, d=d, nb=nb, o_dst=o_dst, m_dst=m_dst, l_dst=l_dst):
                cur, prev, has_prev = _block_rows(t, d, S)
                q2 = _stack_heads(qs[cur, :], is_a)
                if nb == 1:
                    kc, vc = ks[cur, :], vs[cur, :]
                    s = _nt(q2, kc) + bml[bi]
                else:
                    kc = jnp.concatenate([ks[prev, :], ks[cur, :]], axis=0)
                    vc = jnp.concatenate([vs[prev, :], vs[cur, :]], axis=0)
                    s = _nt(q2, kc) + bm[bi] + _first_block_penalty(has_prev)
                m = jnp.max(s, axis=-1, keepdims=True)
                e = jnp.exp(s - m)
                l = jnp.sum(e, axis=-1, keepdims=True)
                pv = jnp.dot(e.astype(BF16), vc, preferred_element_type=F32)
                o_dst[cur, :] = _unstack_heads(pv, is_a)
                m_dst[cur, :] = _unstack_heads(m, is_a)
                l_dst[cur, :] = _unstack_heads(l, is_a)
                return carry

            lax.fori_loop(0, NBLK, block, 0, unroll=ATTN_UNROLL)
            if d > 1:
                L = S // d
                for r in range(d):
                    rows = pl.ds(r, L, stride=d)
                    ob.at[bi][rows, :] = op[r * L:(r + 1) * L, :]
                    mb.at[bi][rows, :] = mp[r * L:(r + 1) * L, :]
                    lb.at[bi][rows, :] = lp[r * L:(r + 1) * L, :]

        def finish(ci, carry):
            rs = pl.ds(pl.multiple_of(ci * RC, RC), RC)
            ms = [mb[bi, rs, :] for bi in range(nbr)]
            mmax = functools.reduce(jnp.maximum, ms)
            ws = [jnp.exp(m - mmax) for m in ms]
            num = sum(ob[bi, rs, :] * ws[bi] for bi in range(nbr))
            den = sum(lb[bi, rs, :] * ws[bi] for bi in range(nbr))
            o = num / den
            o_ref[rs, :] = o
            lse_ref[rs, :] = mmax + jnp.log(den)
            rstd = lax.rsqrt(_seg_sum(o * o, is_a) * (1.0 / GROUP) + EPS)
            an_ref[rs, :] = ((o * rstd) * g_ref[...]).astype(BF16)
            return carry

        lax.fori_loop(0, S // RC, finish, 0, unroll=True)

    seq = pl.BlockSpec((S, LANES), lambda b, p: (b, p))
    return pl.pallas_call(
        body, grid=(B, P), name=name,
        in_specs=[pl.BlockSpec((3, S, LANES), lambda b, p: (0, b, p)),
                  pl.BlockSpec((1, LANES), lambda b, p: (0, p))],
        out_specs=[seq, seq, pl.BlockSpec((None, S, LANES), lambda b, p: (0, b, p))],
        out_shape=[jax.ShapeDtypeStruct((T, C), F32), jax.ShapeDtypeStruct((T, C), F32),
                   jax.ShapeDtypeStruct((mix_shape_pieces, T, C), BF16)],
        scratch_shapes=[pltpu.VMEM((S, LANES), BF16)] * 3 + [pltpu.VMEM((S, LANES), F32)] * 3
        + [pltpu.VMEM((nbr, S, LANES), F32)] * 3
        + [pltpu.VMEM((nbr, 2 * BAND, 2 * BAND), F32), pltpu.VMEM((nbr, 2 * BAND, BAND), F32)],
        compiler_params=_params(("parallel", "parallel")),
    )(qkv3, gain)


def _attn_bwd(qkv3, o, lse, dmix3, gain, dproj_pieces, S, n_heads, name):
    _, T, C = qkv3.shape
    B, P = T // S, C // LANES
    NBLK = S // BAND
    scale = GROUP ** -0.5
    nbr = len(DILATIONS)
    RC = ATTN_ROWS

    def body(qkv_ref, o_ref, lse_ref, dn_ref, g_ref, dqkv_ref, dg_ref,
             do_n, dd_n, qs, ks, vs, dos, lses, dds, dqp, dkp, dvp, dqn, dkn, dvn, bm, bml):
        pair = pl.program_id(0)
        b = pl.program_id(1)
        is_a = _lane_is_a()
        _bias_tables(bm, bml, pair, n_heads, S)

        def prologue(ci, dg_acc):
            rs = pl.ds(pl.multiple_of(ci * RC, RC), RC)
            ov = o_ref[rs, :]
            dn = dn_ref[rs, :]
            rstd = lax.rsqrt(_seg_sum(ov * ov, is_a) * (1.0 / GROUP) + EPS)
            on = ov * rstd
            a = dn * g_ref[...]
            s_a = _seg_sum(a * on, is_a)
            do_n[rs, :] = rstd * (a - on * (s_a * (1.0 / GROUP)))
            dd_n[rs, :] = (EPS * s_a) * (rstd * rstd)
            zero = jnp.zeros((RC, LANES), F32)
            dqn[rs, :] = zero
            dkn[rs, :] = zero
            dvn[rs, :] = zero
            return dg_acc + jnp.sum(dn * on, axis=0, keepdims=True)

        dg_part = lax.fori_loop(0, S // RC, prologue, jnp.zeros((1, LANES), F32), unroll=True)

        @pl.when(b == 0)
        def _():
            dg_ref[...] = dg_part

        @pl.when(b != 0)
        def _():
            dg_ref[...] += dg_part

        for bi, d in enumerate(DILATIONS):
            nb = S // (BAND * d)
            L = S // d
            _gather_residues(qs, lambda rows: qkv_ref.at[0][rows, :], d, S, lambda v: (v * scale).astype(BF16))
            _gather_residues(ks, lambda rows: qkv_ref.at[1][rows, :], d, S, lambda v: v.astype(BF16))
            _gather_residues(vs, lambda rows: qkv_ref.at[2][rows, :], d, S, lambda v: v.astype(BF16))
            _gather_residues(dos, lambda rows: do_n[rows, :], d, S, lambda v: v.astype(BF16))
            if d == 1:
                lse_src, dd_src, dq_dst, dk_dst, dv_dst = lse_ref, dd_n, dqn, dkn, dvn
            else:
                _gather_residues(lses, lambda rows: lse_ref[rows, :], d, S, lambda v: v)
                _gather_residues(dds, lambda rows: dd_n[rows, :], d, S, lambda v: v)
                dkp[...] = jnp.zeros((S, LANES), F32)
                dvp[...] = jnp.zeros((S, LANES), F32)
                lse_src, dd_src, dq_dst, dk_dst, dv_dst = lses, dds, dqp, dkp, dvp

            def block(t, carry, bi=bi, d=d, nb=nb, lse_src=lse_src, dd_src=dd_src, dq_dst=dq_dst, dk_dst=dk_dst,
                      dv_dst=dv_dst):
                cur, prev, has_prev = _block_rows(t, d, S)
                q2 = _stack_heads(qs[cur, :], is_a)
                do2 = _stack_heads(dos[cur, :], is_a)
                lse_t = lse_src[cur, :]
                dd_t = dd_src[cur, :]
                lse2 = jnp.concatenate([lse_t[:, 0:1], lse_t[:, GROUP:GROUP + 1]], axis=0)
                dd2 = jnp.concatenate([dd_t[:, 0:1], dd_t[:, GROUP:GROUP + 1]], axis=0)
                if nb == 1:
                    kc, vc = ks[cur, :], vs[cur, :]
                    s = _nt(q2, kc) + bml[bi]
                else:
                    kc = jnp.concatenate([ks[prev, :], ks[cur, :]], axis=0)
                    vc = jnp.concatenate([vs[prev, :], vs[cur, :]], axis=0)
                    s = _nt(q2, kc) + bm[bi] + _first_block_penalty(has_prev)
                p = jnp.exp(s - lse2)
                ds = (p * (_nt(do2, vc) - dd2)).astype(BF16)
                dq = _unstack_heads(jnp.dot(ds, kc, preferred_element_type=F32), is_a)
                dk = _tn(ds, q2)
                dv = _tn(p.astype(BF16), do2)
                dq_dst[cur, :] = dq
                if nb == 1:
                    dk_dst[cur, :] += dk
                    dv_dst[cur, :] += dv
                else:
                    dk_dst[prev, :] += dk[0:BAND, :]
                    dv_dst[prev, :] += dv[0:BAND, :]
                    dk_dst[cur, :] += dk[BAND:2 * BAND, :]
                    dv_dst[cur, :] += dv[BAND:2 * BAND, :]
                return carry

            lax.fori_loop(0, NBLK, block, 0, unroll=ATTN_UNROLL)
            if d > 1:
                for r in range(d):
                    rows = pl.ds(r, L, stride=d)
                    dqn[rows, :] += dqp[r * L:(r + 1) * L, :]
                    dkn[rows, :] += dkp[r * L:(r + 1) * L, :]
                    dvn[rows, :] += dvp[r * L:(r + 1) * L, :]

        dqkv_ref[0] = (dqn[...] * scale).astype(BF16)
        dqkv_ref[1] = dkn[...].astype(BF16)
        dqkv_ref[2] = dvn[...].astype(BF16)

    seq = pl.BlockSpec((S, LANES), lambda p, b: (b, p))
    f32_seq = pltpu.VMEM((S, LANES), F32)
    bf_seq = pltpu.VMEM((S, LANES), BF16)
    return pl.pallas_call(
        body, grid=(P, B), name=name,
        in_specs=[pl.BlockSpec((3, S, LANES), lambda p, b: (0, b, p)), seq, seq,
                  pl.BlockSpec((None, S, LANES), lambda p, b: (0, b, p)),
                  pl.BlockSpec((1, LANES), lambda p, b: (0, p))],
        out_specs=[pl.BlockSpec((3, S, LANES), lambda p, b: (0, b, p)),
                   pl.BlockSpec((1, LANES), lambda p, b: (0, p))],
        out_shape=[jax.ShapeDtypeStruct((dproj_pieces, T, C), BF16), jax.ShapeDtypeStruct((1, C), F32)],
        scratch_shapes=[f32_seq, f32_seq, bf_seq, bf_seq, bf_seq, bf_seq, f32_seq, f32_seq,
                        f32_seq, f32_seq, f32_seq, f32_seq, f32_seq, f32_seq,
                        pltpu.VMEM((nbr, 2 * BAND, 2 * BAND), F32), pltpu.VMEM((nbr, 2 * BAND, BAND), F32)],
        compiler_params=_params(("parallel", "arbitrary")),
    )(qkv3, o, lse, dmix3, gain)


def _delay(x, k, row):
    return jnp.where(row >= k, pltpu.roll(x, k, 0), 0.0)


def _advance(x, k, row, S):
    return jnp.where(row < S - k, pltpu.roll(x, S - k, 0), 0.0)


def _conv3(x, w, row):
    return (w[0:1, :] * _delay(x, 2, row) + w[1:2, :] * _delay(x, 1, row)) + w[2:3, :] * x


HALO = 8


CONV_ROWS = 64
FFN_LANES = 128
IN_BUFFERS = 3


def _zero_halo(pad_ref, S):
    zeros = jnp.zeros((HALO, pad_ref.shape[1]), pad_ref.dtype)
    pad_ref[0:HALO, :] = zeros
    pad_ref[HALO + S:2 * HALO + S, :] = zeros


def _window_at(pad_ref, r0, shift):
    return pad_ref[HALO + r0 + shift:HALO + r0 + shift + CONV_ROWS, :]


def _conv3_at(pad_ref, w, r0):
    return ((w[0:1, :] * _window_at(pad_ref, r0, -2) + w[1:2, :] * _window_at(pad_ref, r0, -1))
            + w[2:3, :] * _window_at(pad_ref, r0, 0))


def _conv3_grads_at(dz_ref, x_ref, w, r0):
    dz, dz1, dz2 = (_window_at(dz_ref, r0, k) for k in range(3))
    x = _window_at(x_ref, r0, 0)
    dx = (w[2:3, :] * dz + w[1:2, :] * dz1) + w[0:1, :] * dz2
    parts = [jnp.sum((d * x).reshape(CONV_ROWS // 8, 8, x.shape[1]), axis=0) for d in (dz2, dz1, dz)]
    return dx, parts


def _conv3_grads(dz, x, w, row, S):
    dz1 = _advance(dz, 1, row, S)
    dz2 = _advance(dz, 2, row, S)
    dx = (w[2:3, :] * dz + w[1:2, :] * dz1) + w[0:1, :] * dz2
    dw = jnp.concatenate([jnp.sum(dz2 * x, axis=0, keepdims=True),
                          jnp.sum(dz1 * x, axis=0, keepdims=True),
                          jnp.sum(dz * x, axis=0, keepdims=True)], axis=0)
    return dx, dw


def _mix_conv_fwd(cv3, taps, gain, mix, S, name, after=None):
    _, T, C = cv3.shape
    B, P = T // S, C // LANES

    def body(cv_ref, w_ref, g_ref, mix_hbm, *rest):
        y_ref, pad_c = rest[-2:]
        del mix_hbm
        is_a = _lane_is_a()
        _zero_halo(pad_c, S)
        pad_c[HALO:HALO + S, :] = cv_ref[1].astype(F32) * cv_ref[2].astype(F32)
        w = w_ref[...]
        for r0 in range(0, S, CONV_ROWS):
            y = cv_ref[0, r0:r0 + CONV_ROWS, :].astype(F32) * _conv3_at(pad_c, w, r0)
            rstd = lax.rsqrt(_seg_sum(y * y, is_a) * (1.0 / GROUP) + EPS)
            y_ref[r0:r0 + CONV_ROWS, :] = ((y * rstd) * g_ref[...]).astype(BF16)

    in_specs = [pl.BlockSpec((3, S, LANES), lambda b, p: (0, b, p)),
                pl.BlockSpec((3, LANES), lambda b, p: (0, p)),
                pl.BlockSpec((1, LANES), lambda b, p: (0, p)),
                ANY]
    operands = [cv3, taps, gain, mix]
    if after is not None:
        in_specs.append(ANY)
        operands.append(after)
    return pl.pallas_call(
        body, grid=(B, P), name=name,
        in_specs=in_specs,
        out_specs=pl.BlockSpec((None, S, LANES), lambda b, p: (1, b, p)),
        out_shape=jax.ShapeDtypeStruct(mix.shape, mix.dtype),
        scratch_shapes=[pltpu.VMEM((S + 2 * HALO, LANES), F32)],
        input_output_aliases={3: 0},
        compiler_params=_params(("parallel", "parallel")),
    )(*operands)


def _mix_conv_bwd(cv3, dmix3, taps, gain, dproj, S, name):
    _, T, C = cv3.shape
    B, P = T // S, C // LANES

    def body(cv_ref, dn_ref, w_ref, g_ref, dproj_hbm, dcv_ref, dw_ref, dg_ref):
        del dproj_hbm
        b = pl.program_id(1)
        row = lax.broadcasted_iota(jnp.int32, (S, 1), 0)
        is_a = _lane_is_a()
        w = w_ref[...]
        gb = cv_ref[0].astype(F32)
        gc = cv_ref[1].astype(F32)
        u = cv_ref[2].astype(F32)
        c = gc * u
        z = _conv3(c, w, row)
        y = gb * z
        rstd = lax.rsqrt(_seg_sum(y * y, is_a) * (1.0 / GROUP) + EPS)
        yn = y * rstd
        dn = dn_ref[...]
        a = dn * g_ref[...]
        dy = rstd * (a - yn * (_seg_sum(a * yn, is_a) * (1.0 / GROUP)))
        dg = jnp.sum(dn * yn, axis=0, keepdims=True)
        dc, dw = _conv3_grads(dy * gb, c, w, row, S)
        dcv_ref[0] = (dy * z).astype(BF16)
        dcv_ref[1] = (dc * u).astype(BF16)
        dcv_ref[2] = (dc * gc).astype(BF16)

        @pl.when(b == 0)
        def _():
            dw_ref[...] = dw
            dg_ref[...] = dg

        @pl.when(b != 0)
        def _():
            dw_ref[...] += dw
            dg_ref[...] += dg

    return pl.pallas_call(
        body, grid=(P, B), name=name,
        in_specs=[pl.BlockSpec((3, S, LANES), lambda p, b: (0, b, p)),
                  pl.BlockSpec((None, S, LANES), lambda p, b: (1, b, p)),
                  pl.BlockSpec((3, LANES), lambda p, b: (0, p)),
                  pl.BlockSpec((1, LANES), lambda p, b: (0, p)),
                  pl.BlockSpec(memory_space=pl.ANY)],
        out_specs=[pl.BlockSpec((3, S, LANES), lambda p, b: (1, b, p)),
                   pl.BlockSpec((3, LANES), lambda p, b: (0, p)),
                   pl.BlockSpec((1, LANES), lambda p, b: (0, p))],
        out_shape=[jax.ShapeDtypeStruct(dproj.shape, dproj.dtype),
                   jax.ShapeDtypeStruct((3, C), F32), jax.ShapeDtypeStruct((1, C), F32)],
        input_output_aliases={4: 0},
        compiler_params=_params(("parallel", "arbitrary")),
    )(cv3, dmix3, taps, gain, dproj)


def _sigmoid(x):
    return 0.5 * jnp.tanh(0.5 * x) + 0.5


def _ffn_act_fwd(up3, taps, S, name):
    _, T, Fd = up3.shape
    W = FFN_LANES
    B, P = T // S, Fd // W
    n_steps = B * P
    taps3 = taps.reshape(3, 2 * P, W).transpose(1, 0, 2)

    def body(up_hbm, w_ref, act_hbm, inbuf, outbuf, in_sems, out_sems, pad_g, pad_v):
        def block_of(i):
            rows = pl.ds(pl.multiple_of((i // P) * S, S), S)
            cols = pl.ds(pl.multiple_of((i % P) * W, W), W)
            return rows, cols

        def fetch(i, slot):
            rows, cols = block_of(i)
            return pltpu.make_async_copy(up_hbm.at[:, rows, cols], inbuf.at[slot], in_sems.at[slot])

        def write_back(i, slot):
            rows, cols = block_of(i)
            return pltpu.make_async_copy(outbuf.at[slot], act_hbm.at[rows, cols], out_sems.at[slot])

        _zero_halo(pad_g, S)
        _zero_halo(pad_v, S)
        for k in range(IN_BUFFERS):
            fetch(k, k).start()

        def step(i, carry):
            slot = i % IN_BUFFERS
            oslot = i % 2
            fetch(i, slot).wait()
            pad_g[HALO:HALO + S, :] = inbuf[slot, 0].astype(F32)
            pad_v[HALO:HALO + S, :] = inbuf[slot, 1].astype(F32)

            @pl.when(i + IN_BUFFERS < n_steps)
            def _():
                fetch(i + IN_BUFFERS, slot).start()

            @pl.when(i >= 2)
            def _():
                write_back(i - 2, oslot).wait()

            wg = w_ref[i % P]
            wv = w_ref[P + i % P]
            for r0 in range(0, S, CONV_ROWS):
                cg = _conv3_at(pad_g, wg, r0)
                cv = _conv3_at(pad_v, wv, r0)
                outbuf[oslot, r0:r0 + CONV_ROWS, :] = ((cg * _sigmoid(cg)) * cv).astype(BF16)
            write_back(i, oslot).start()
            return carry

        lax.fori_loop(0, n_steps, step, 0)
        write_back(n_steps - 2, n_steps % 2).wait()
        write_back(n_steps - 1, (n_steps - 1) % 2).wait()

    return pl.pallas_call(
        body, name=name,
        in_specs=[ANY, pl.BlockSpec(memory_space=pltpu.VMEM)],
        out_specs=ANY,
        out_shape=jax.ShapeDtypeStruct((T, Fd), BF16),
        scratch_shapes=[pltpu.VMEM((IN_BUFFERS, 2, S, W), BF16), pltpu.VMEM((2, S, W), BF16),
                        pltpu.SemaphoreType.DMA((IN_BUFFERS,)), pltpu.SemaphoreType.DMA((2,)),
                        pltpu.VMEM((S + 2 * HALO, W), F32), pltpu.VMEM((S + 2 * HALO, W), F32)],
        compiler_params=_params(),
    )(up3, taps3)


def _ffn_act_bwd(up3, dact3, taps, S, name):
    _, T, Fd = up3.shape
    W = FFN_LANES
    B, P = T // S, Fd // W

    def body(up_ref, da_ref, wg_ref, wv_ref, dup_ref, dwg_ref, dwv_ref, pad_ug, pad_uv, pad_dg, pad_dv):
        b = pl.program_id(1)
        for pad in (pad_ug, pad_uv, pad_dg, pad_dv):
            _zero_halo(pad, S)
        pad_ug[HALO:HALO + S, :] = up_ref[0].astype(F32)
        pad_uv[HALO:HALO + S, :] = up_ref[1].astype(F32)
        wg = wg_ref[...]
        wv = wv_ref[...]
        for r0 in range(0, S, CONV_ROWS):
            cg = _conv3_at(pad_ug, wg, r0)
            cv = _conv3_at(pad_uv, wv, r0)
            sg = _sigmoid(cg)
            da = da_ref[r0:r0 + CONV_ROWS, :].astype(F32)
            t = da * sg
            dcv = cg * t
            pad_dg[HALO + r0:HALO + r0 + CONV_ROWS, :] = cv * ((t + dcv) - dcv * sg)
            pad_dv[HALO + r0:HALO + r0 + CONV_ROWS, :] = dcv
        sums_g = [jnp.zeros((8, W), F32)] * 3
        sums_v = [jnp.zeros((8, W), F32)] * 3
        for r0 in range(0, S, CONV_ROWS):
            dug, parts_g = _conv3_grads_at(pad_dg, pad_ug, wg, r0)
            duv, parts_v = _conv3_grads_at(pad_dv, pad_uv, wv, r0)
            dup_ref[0, r0:r0 + CONV_ROWS, :] = dug.astype(BF16)
            dup_ref[1, r0:r0 + CONV_ROWS, :] = duv.astype(BF16)
            sums_g = [a + p for a, p in zip(sums_g, parts_g)]
            sums_v = [a + p for a, p in zip(sums_v, parts_v)]
        dwg = jnp.concatenate([jnp.sum(a, axis=0, keepdims=True) for a in sums_g], axis=0)
        dwv = jnp.concatenate([jnp.sum(a, axis=0, keepdims=True) for a in sums_v], axis=0)

        @pl.when(b == 0)
        def _():
            dwg_ref[...] = dwg
            dwv_ref[...] = dwv

        @pl.when(b != 0)
        def _():
            dwg_ref[...] += dwg
            dwv_ref[...] += dwv

    tap_out = pl.BlockSpec((3, W), lambda p, b: (0, p))
    return pl.pallas_call(
        body, grid=(P, B), name=name,
        in_specs=[pl.BlockSpec((2, S, W), lambda p, b: (0, b, p)),
                  pl.BlockSpec((None, S, W), lambda p, b: (0, b, p)),
                  pl.BlockSpec((3, W), lambda p, b: (0, p)),
                  pl.BlockSpec((3, W), lambda p, b: (0, P + p))],
        out_specs=[pl.BlockSpec((2, S, W), lambda p, b: (0, b, p)), tap_out, tap_out],
        out_shape=[jax.ShapeDtypeStruct((2, T, Fd), BF16),
                   jax.ShapeDtypeStruct((3, Fd), F32), jax.ShapeDtypeStruct((3, Fd), F32)],
        scratch_shapes=[pltpu.VMEM((S + 2 * HALO, W), F32)] * 4,
        compiler_params=_params(("parallel", "arbitrary")),
    )(up3, dact3, taps, taps)


def _final_norm_loss(x, g, target, tm, name):
    T, D = x.shape

    def body(x_ref, g_ref, t_ref, dx_ref, dg_ref, loss_ref):
        xv = x_ref[...]
        rstd = lax.rsqrt(jnp.mean(xv * xv, axis=-1, keepdims=True) + EPS)
        xn = xv * rstd
        err = xn * g_ref[...] - t_ref[...]
        part = 0.5 * jnp.sum(jnp.mean(err * err, axis=-1, keepdims=True), axis=0, keepdims=True)
        dy = err * (1.0 / D)
        a = dy * g_ref[...]
        dx_ref[...] = rstd * (a - xn * jnp.mean(a * xn, axis=-1, keepdims=True))
        dg = jnp.sum(dy * xn, axis=0, keepdims=True)
        lpart = jnp.broadcast_to(part, (1, LANES))

        @pl.when(pl.program_id(0) == 0)
        def _():
            dg_ref[...] = dg
            loss_ref[...] = lpart

        @pl.when(pl.program_id(0) != 0)
        def _():
            dg_ref[...] += dg
            loss_ref[...] += lpart

    row = pl.BlockSpec((tm, D), lambda i: (i, 0))
    return pl.pallas_call(
        body, grid=(T // tm,), name=name,
        in_specs=[row, pl.BlockSpec((1, D), lambda i: (0, 0)), row],
        out_specs=[row, pl.BlockSpec((1, D), lambda i: (0, 0)), pl.BlockSpec((1, LANES), lambda i: (0, 0))],
        out_shape=[jax.ShapeDtypeStruct((T, D), F32), jax.ShapeDtypeStruct((1, D), F32),
                   jax.ShapeDtypeStruct((1, LANES), F32)],
        compiler_params=_params(("arbitrary",)),
    )(x, g, target)


def _row_tile(rows, cols, budget_elems=512 * 1024):
    tr = rows
    while tr * cols > budget_elems and tr % 32 == 0:
        tr //= 2
    return tr


def _prefetch_call(body, grid, in_specs, out_specs, out_shape, name, sem, aliases=None):
    return pl.pallas_call(
        body, name=name, out_shape=out_shape,
        grid_spec=pltpu.PrefetchScalarGridSpec(num_scalar_prefetch=1, grid=grid, in_specs=in_specs,
                                               out_specs=out_specs),
        input_output_aliases=aliases or {},
        compiler_params=_params(sem))


def _cast_into_full(w, layer, colwise, where, name):
    _, K, N = w.shape
    tr = _row_tile(K, N)
    nrb = K // tr
    full_shape = (1, K, 4 * N) if colwise else (1, 4 * K, N)

    def body(where_ref, w_ref, o_ref):
        del where_ref
        o_ref[...] = w_ref[...].astype(BF16)

    if colwise:
        out_map = lambda i, wh: (0, i, wh[0])
    else:
        out_map = lambda i, wh: (0, wh[0] * nrb + i, 0)
    return _prefetch_call(
        body, (nrb,), [pl.BlockSpec((None, tr, N), lambda i, wh: (layer, i, 0))],
        pl.BlockSpec((None, tr, N), out_map), pltpu.HBM(full_shape, BF16), name,
        ("parallel",))(where, w)


def _chip_sum(g3, other, colwise, where, name):
    L, K, N = g3.shape
    hk, hn = (K // 2, N) if colwise else (K, N // 2)
    tr = _row_tile(hk, hn)
    nrb = hk // tr

    def body(where_ref, g_ref, o_ref, s_ref):
        del where_ref
        s_ref[...] = (g_ref[...].astype(F32) + o_ref[...].astype(F32)).astype(BF16)

    if colwise:
        g_map = lambda l, i, wh: (l, wh[1] * nrb + i, 0)
    else:
        g_map = lambda l, i, wh: (l, i, wh[1])
    blk = pl.BlockSpec((None, tr, hn), lambda l, i, wh: (l, i, 0))
    return _prefetch_call(
        body, (L, nrb), [pl.BlockSpec((None, tr, hn), g_map), blk], blk,
        pltpu.HBM((L, hk, hn), BF16), name, ("parallel", "parallel"))(where, _in_hbm(g3), _in_hbm(other))


def _owner_sum(chip_sum, received, colwise, where, layer, n_layers, prev, name):
    _, hk, hn = chip_sum.shape
    pk, pn = (hk, hn // 4) if colwise else (hk // 4, hn)
    tr = _row_tile(pk, pn)
    nrb = pk // tr
    shard_shape = (n_layers, 2 * pk, pn) if colwise else (n_layers, pk, 2 * pn)

    def body(where_ref, own_ref, rec_ref, *rest):
        del where_ref
        o_ref = rest[-1]
        acc = own_ref[...].astype(F32)
        for j in range(3):
            acc = acc + rec_ref[j].astype(F32)
        o_ref[...] = acc

    if colwise:
        own_map = lambda i, wh: (0, i, wh[0])
        out_map = lambda i, wh: (layer, wh[1] * nrb + i, 0)
    else:
        own_map = lambda i, wh: (0, wh[0] * nrb + i, 0)
        out_map = lambda i, wh: (layer, i, wh[1])
    in_specs = [pl.BlockSpec((None, tr, pn), own_map),
                pl.BlockSpec((3, None, tr, pn), lambda i, wh: (0, 0, i, 0))]
    operands = [where, _in_hbm(chip_sum), _in_hbm(received)]
    if prev is not None:
        in_specs.append(ANY)
        operands.append(prev)
    return _prefetch_call(
        body, (nrb,), in_specs, pl.BlockSpec((None, tr, pn), out_map), pltpu.HBM(shard_shape, F32), name,
        ("parallel",), None if prev is None else {3: 0})(*operands)


def _adamw(w, g, m, v, name):
    R, Cc = w.shape
    tr = _row_tile(R, Cc, 256 * 1024)

    def body(w_ref, g_ref, m_ref, v_ref, d_ref, nm_ref, nv_ref, go_ref):
        gv = g_ref[...]
        go_ref[...] = gv
        nm = ADAM_B1 * m_ref[...] + (1.0 - ADAM_B1) * gv
        nv = ADAM_B2 * v_ref[...] + (1.0 - ADAM_B2) * (gv * gv)
        m_hat = nm / (1.0 - ADAM_B1 ** ADAM_STEP)
        v_hat = nv / (1.0 - ADAM_B2 ** ADAM_STEP)
        d_ref[...] = -ADAM_LR * (m_hat / (jnp.sqrt(v_hat) + ADAM_EPS) + ADAM_WD * w_ref[...])
        nm_ref[...] = nm
        nv_ref[...] = nv

    blk = pl.BlockSpec((tr, Cc), lambda i: (i, 0))
    shp = jax.ShapeDtypeStruct((R, Cc), F32)
    return pl.pallas_call(
        body, grid=(R // tr,), name=name,
        in_specs=[blk] * 4, out_specs=[blk] * 4, out_shape=[shp] * 4,
        compiler_params=_params(("parallel",)),
    )(w, g, m, v)


COL_SHARDED = (True, False, True, False)


def _position():
    x, y, c = lax.axis_index("x"), lax.axis_index("y"), lax.axis_index("c")
    chips = [(1 - x, y), (x, 1 - y), (1 - x, 1 - y)]
    return x, y, c, chips


def _span(index, size, align):
    return pl.ds(pl.multiple_of(index * size, align), size)


def _window(ref, colwise, shard, half, shards=4):
    _, K, N = ref.shape
    rows = cols = slice(None)
    if colwise:
        if half is not None:
            rows = _span(half, K // 2, 16)
        if shard is not None:
            cols = _span(shard, N // shards, LANES)
    else:
        if shard is not None:
            rows = _span(shard, K // shards, 16)
        if half is not None:
            cols = _span(half, N // 2, LANES)
    return ref.at[:, rows, cols]


HBM = pl.BlockSpec(memory_space=pltpu.HBM)
SEMAPHORES = pl.BlockSpec(memory_space=pltpu.SEMAPHORE)


def _gather_start(fulls, colwise, group_sizes, after, name):
    n = len(fulls)
    n_groups = len(group_sizes)

    n_in = n if after is None else n + 1

    def body(*refs):
        ins = refs[:n]
        sems = refs[n_in:n_in + 2 * n_groups]
        x, y, c, chips = _position()
        me = 2 * x + y
        i = 0
        for g, size in enumerate(group_sizes):
            for a in range(size):
                win = _window(ins[i], colwise[i], me, c)
                for j, chip in enumerate(chips):
                    pltpu.make_async_remote_copy(
                        src_ref=win, dst_ref=win, send_sem=sems[2 * g].at[a * 3 + j],
                        recv_sem=sems[2 * g + 1].at[a * 3 + j],
                        device_id=(chip[0], chip[1], c), device_id_type=MESH_ID).start()
                i += 1

    sem_shapes = []
    for size in group_sizes:
        sem_shapes += [pltpu.SemaphoreType.DMA((3 * size,)), pltpu.SemaphoreType.DMA((3 * size,))]
    operands = [pltpu.with_memory_space_constraint(f, pltpu.HBM) for f in fulls]
    in_specs = [HBM] * n
    if after is not None:
        operands.append(after)
        in_specs.append(ANY)
    outs = pl.pallas_call(
        body, name=name,
        in_specs=in_specs, out_specs=[SEMAPHORES] * (2 * n_groups) + [HBM] * n,
        out_shape=sem_shapes + [pltpu.HBM(f.shape, f.dtype) for f in fulls],
        input_output_aliases={i: 2 * n_groups + i for i in range(n)},
        compiler_params=pltpu.CompilerParams(has_side_effects=pltpu.SideEffectType.DATAFLOW_SIDE_EFFECTING),
    )(*operands)
    sems = [(outs[2 * g], outs[2 * g + 1]) for g in range(n_groups)]
    return sems, list(outs[2 * n_groups:])


def _to_sibling(ref, colwise, chip, half, x, y, c, send_sem, recv_sem):
    win = _window(ref, colwise, 2 * chip[0] + chip[1], half)
    return pltpu.make_async_remote_copy(
        src_ref=win, dst_ref=win, send_sem=send_sem, recv_sem=recv_sem,
        device_id=(x, y, 1 - c), device_id_type=MESH_ID)


def _gather_pass(in_flight, colwise, sems, after, name):
    n = len(in_flight)

    def body(*refs):
        ins = refs[:n]
        send_sems, recv_sems = refs[n], refs[n + 1]
        pass_send, pass_recv = refs[-2 - n], refs[-1 - n]
        x, y, c, chips = _position()
        me = 2 * x + y
        for a in range(n):
            for j, chip in enumerate(chips):
                k = a * 3 + j
                pltpu.make_async_remote_copy(
                    src_ref=_window(ins[a], colwise[a], me, c),
                    dst_ref=_window(ins[a], colwise[a], 2 * chip[0] + chip[1], c),
                    send_sem=send_sems.at[k], recv_sem=recv_sems.at[k],
                    device_id=(chip[0], chip[1], c), device_id_type=MESH_ID).wait()
                _to_sibling(ins[a], colwise[a], chip, c, x, y, c, pass_send.at[k], pass_recv.at[k]).start()

    operands = list(in_flight) + list(sems)
    in_specs = [HBM] * n + [SEMAPHORES] * 2
    if after is not None:
        operands.append(after)
        in_specs.append(ANY)
    outs = pl.pallas_call(
        body, name=name,
        in_specs=in_specs, out_specs=[SEMAPHORES] * 2 + [HBM] * n,
        out_shape=[pltpu.SemaphoreType.DMA((3 * n,)), pltpu.SemaphoreType.DMA((3 * n,))]
        + [pltpu.HBM(f.shape, f.dtype) for f in in_flight],
        input_output_aliases={i: 2 + i for i in range(n)},
        compiler_params=pltpu.CompilerParams(has_side_effects=pltpu.SideEffectType.DATAFLOW_SIDE_EFFECTING),
    )(*operands)
    return (outs[0], outs[1]), list(outs[2:])


def _gather_wait(in_flight, colwise, sems, after, name):
    n = len(in_flight)

    def body(*refs):
        ins = refs[:n]
        send_sems, recv_sems = refs[n], refs[n + 1]
        x, y, c, chips = _position()
        for a in range(n):
            for j, chip in enumerate(chips):
                k = a * 3 + j
                _to_sibling(ins[a], colwise[a], chip, c, x, y, c, send_sems.at[k], recv_sems.at[k]).wait_send()
                _to_sibling(ins[a], colwise[a], chip, 1 - c, x, y, c, send_sems.at[k], recv_sems.at[k]).wait_recv()

    operands = list(in_flight) + list(sems)
    in_specs = [HBM] * n + [SEMAPHORES] * 2
    if after is not None:
        operands.append(after)
        in_specs.append(ANY)
    outs = pl.pallas_call(
        body, name=name,
        in_specs=in_specs, out_specs=[HBM] * n,
        out_shape=[pltpu.HBM(f.shape, f.dtype) for f in in_flight],
        input_output_aliases={i: i for i in range(n)},
        compiler_params=pltpu.CompilerParams(has_side_effects=pltpu.SideEffectType.DATAFLOW_SIDE_EFFECTING),
    )(*operands)
    return list(outs)


def _exchange_copy(g_ref, land_ref, colwise, x, y, c, send_sem, recv_sem):
    return pltpu.make_async_remote_copy(
        src_ref=_window(g_ref, colwise, None, 1 - c), dst_ref=land_ref, send_sem=send_sem, recv_sem=recv_sem,
        device_id=(x, y, 1 - c), device_id_type=MESH_ID)


def _exchange_start(grads, colwise, name):
    n = len(grads)
    lands = []
    for g, cw in zip(grads, colwise):
        L, K, N = g.shape
        lands.append(lax.empty((L, K // 2, N) if cw else (L, K, N // 2), g.dtype))

    def body(*refs):
        src, land = refs[:n], refs[n:2 * n]
        send_sems, recv_sems = refs[2 * n], refs[2 * n + 1]
        x, y, c, _ = _position()
        for i in range(n):
            _exchange_copy(src[i], land[i], colwise[i], x, y, c, send_sems.at[i], recv_sems.at[i]).start()

    arrays = list(grads) + lands
    outs = pl.pallas_call(
        body, name=name,
        in_specs=[HBM] * (2 * n), out_specs=[SEMAPHORES] * 2 + [HBM] * (2 * n),
        out_shape=[pltpu.SemaphoreType.DMA((n,)), pltpu.SemaphoreType.DMA((n,))]
        + [pltpu.HBM(a.shape, a.dtype) for a in arrays],
        input_output_aliases={i: 2 + i for i in range(2 * n)},
        compiler_params=pltpu.CompilerParams(has_side_effects=pltpu.SideEffectType.DATAFLOW_SIDE_EFFECTING),
    )(*[pltpu.with_memory_space_constraint(a, pltpu.HBM) for a in arrays])
    return (outs[0], outs[1]), list(outs[2:2 + n]), list(outs[2 + n:])


def _exchange_wait(grads, lands, colwise, sems, after, name):
    n = len(grads)

    def body(*refs):
        src, land = refs[:n], refs[n:2 * n]
        send_sems, recv_sems = refs[2 * n], refs[2 * n + 1]
        x, y, c, _ = _position()
        for i in range(n):
            _exchange_copy(src[i], land[i], colwise[i], x, y, c, send_sems.at[i], recv_sems.at[i]).wait()

    arrays = list(grads) + list(lands)
    operands = arrays + list(sems)
    in_specs = [HBM] * (2 * n) + [SEMAPHORES] * 2
    if after is not None:
        operands.append(after)
        in_specs.append(ANY)
    outs = pl.pallas_call(
        body, name=name,
        in_specs=in_specs, out_specs=[HBM] * (2 * n),
        out_shape=[pltpu.HBM(a.shape, a.dtype) for a in arrays],
        input_output_aliases={i: i for i in range(2 * n)},
        compiler_params=pltpu.CompilerParams(has_side_effects=pltpu.SideEffectType.DATAFLOW_SIDE_EFFECTING),
    )(*operands)
    return list(outs[:n]), list(outs[n:])


def _scatter_copy(src_ref, land_ref, colwise, j, chip, c, send_sem, recv_sem):
    return pltpu.make_async_remote_copy(
        src_ref=_window(src_ref, colwise, 2 * chip[0] + chip[1], None), dst_ref=land_ref.at[j],
        send_sem=send_sem, recv_sem=recv_sem, device_id=(chip[0], chip[1], c), device_id_type=MESH_ID)


def _scatter_start(chip_sums, colwise, name):
    n = len(chip_sums)
    lands = []
    for g, cw in zip(chip_sums, colwise):
        L, hk, hn = g.shape
        lands.append(lax.empty((3, L, hk, hn // 4) if cw else (3, L, hk // 4, hn), g.dtype))

    def body(*refs):
        src, land = refs[:n], refs[n:2 * n]
        send_sems, recv_sems = refs[2 * n], refs[2 * n + 1]
        x, y, c, chips = _position()
        for i in range(n):
            for j, chip in enumerate(chips):
                _scatter_copy(src[i], land[i], colwise[i], j, chip, c, send_sems.at[i * 3 + j],
                              recv_sems.at[i * 3 + j]).start()

    arrays = list(chip_sums) + lands
    outs = pl.pallas_call(
        body, name=name,
        in_specs=[HBM] * (2 * n), out_specs=[SEMAPHORES] * 2 + [HBM] * (2 * n),
        out_shape=[pltpu.SemaphoreType.DMA((3 * n,)), pltpu.SemaphoreType.DMA((3 * n,))]
        + [pltpu.HBM(a.shape, a.dtype) for a in arrays],
        input_output_aliases={i: 2 + i for i in range(2 * n)},
        compiler_params=pltpu.CompilerParams(has_side_effects=pltpu.SideEffectType.DATAFLOW_SIDE_EFFECTING),
    )(*[pltpu.with_memory_space_constraint(a, pltpu.HBM) for a in arrays])
    return (outs[0], outs[1]), list(outs[2:2 + n]), list(outs[2 + n:])


def _scatter_wait(sources, lands, colwise, sems, after, name):
    n = len(sources)

    def body(*refs):
        src, land = refs[:n], refs[n:2 * n]
        send_sems, recv_sems = refs[2 * n], refs[2 * n + 1]
        x, y, c, chips = _position()
        for i in range(n):
            for j, chip in enumerate(chips):
                cp = _scatter_copy(src[i], land[i], colwise[i], j, chip, c, send_sems.at[i * 3 + j],
                                   recv_sems.at[i * 3 + j])
                cp.wait_send()
                cp.wait_recv()

    arrays = list(sources) + list(lands)
    operands = arrays + list(sems)
    in_specs = [HBM] * (2 * n) + [SEMAPHORES] * 2
    if after is not None:
        operands.append(after)
        in_specs.append(ANY)
    outs = pl.pallas_call(
        body, name=name,
        in_specs=in_specs, out_specs=[HBM] * (2 * n),
        out_shape=[pltpu.HBM(a.shape, a.dtype) for a in arrays],
        input_output_aliases={i: i for i in range(2 * n)},
        compiler_params=pltpu.CompilerParams(has_side_effects=pltpu.SideEffectType.DATAFLOW_SIDE_EFFECTING),
    )(*operands)
    return list(outs[:n]), list(outs[n:])


def _share_with_sibling(shards, colwise, name):
    n = len(shards)

    def body(*refs):
        out = refs[n:2 * n]
        send_sems, recv_sems = refs[2 * n:]
        x, y, c, _ = _position()

        def copy(i, half):
            win = _window(out[i], colwise[i], None, half)
            return pltpu.make_async_remote_copy(
                src_ref=win, dst_ref=win, send_sem=send_sems.at[i], recv_sem=recv_sems.at[i],
                device_id=(x, y, 1 - c), device_id_type=MESH_ID)

        for i in range(n):
            copy(i, c).start()
        for i in range(n):
            copy(i, 1 - c).wait_recv()
        for i in range(n):
            copy(i, c).wait_send()

    return pl.pallas_call(
        body, name=name,
        in_specs=[ANY] * n, out_specs=[ANY] * n,
        out_shape=[jax.ShapeDtypeStruct(s.shape, s.dtype) for s in shards],
        input_output_aliases={i: i for i in range(n)},
        scratch_shapes=[pltpu.SemaphoreType.DMA((n,)), pltpu.SemaphoreType.DMA((n,))],
    )(*shards)


def _chip_exchange(buf, me, x, y, c, chips, send_sems, recv_sems):
    def copy(j, chip, slot):
        return pltpu.make_async_remote_copy(
            src_ref=buf.at[me], dst_ref=buf.at[slot], send_sem=send_sems.at[j], recv_sem=recv_sems.at[j],
            device_id=(chip[0], chip[1], c), device_id_type=MESH_ID)

    for j, chip in enumerate(chips):
        copy(j, chip, me).start()
    for j, chip in enumerate(chips):
        copy(j, chip, 2 * chip[0] + chip[1]).wait_recv()
    for j, chip in enumerate(chips):
        copy(j, chip, me).wait_send()


def _gather_over_chips(pack, name):
    R, Cc = pack.shape

    def body(p_ref, o_ref, send_sems, recv_sems):
        x, y, c, chips = _position()
        me = 2 * x + y
        o_ref[me] = p_ref[...]
        _chip_exchange(o_ref, me, x, y, c, chips, send_sems, recv_sems)

    vmem = pl.BlockSpec(memory_space=pltpu.VMEM)
    return pl.pallas_call(
        body, name=name,
        in_specs=[vmem], out_specs=vmem, out_shape=jax.ShapeDtypeStruct((4, R, Cc), F32),
        scratch_shapes=[pltpu.SemaphoreType.DMA((3,)), pltpu.SemaphoreType.DMA((3,))],
    )(pack)


def _all_reduce_small(pack, name):
    R, Cc = pack.shape

    def body(p_ref, o_ref, sibling, buf, send_sems, recv_sems):
        x, y, c, chips = _position()
        me = 2 * x + y
        swap = pltpu.make_async_remote_copy(
            src_ref=p_ref, dst_ref=sibling, send_sem=send_sems.at[3], recv_sem=recv_sems.at[3],
            device_id=(x, y, 1 - c), device_id_type=MESH_ID)
        swap.start()
        swap.wait()
        buf[me] = p_ref[...] + sibling[...]
        _chip_exchange(buf, me, x, y, c, chips, send_sems, recv_sems)
        o_ref[...] = (buf[0] + buf[1]) + (buf[2] + buf[3])

    vmem = pl.BlockSpec(memory_space=pltpu.VMEM)
    return pl.pallas_call(
        body, name=name,
        in_specs=[vmem], out_specs=vmem, out_shape=jax.ShapeDtypeStruct((R, Cc), F32),
        scratch_shapes=[pltpu.VMEM((R, Cc), F32), pltpu.VMEM((4, R, Cc), F32), pltpu.SemaphoreType.DMA((4,)),
                        pltpu.SemaphoreType.DMA((4,))],
    )(pack)


def _local_forward_backward(x2, target2, S, pass_on, fetch, reduce_begin, reduce_commit, layers, final_g,
                            tm=ROW_TILE):
    T, D = x2.shape
    C = D // 2
    n_heads = C // GROUP
    n_layers = len(layers)
    weights = {}
    saved = []
    xc = x2
    for li, lw in enumerate(layers):
        if li == 0:
            pass_on(0, None)
            weights.update(fetch(0, None))
        h1, qkv3, cv3 = _norm_proj(xc, lw["norm1"], weights[li, "w_in"], ((3, C, F32), (3, C, BF16)), tm,
                                   min(C, 512), f"l{li}_norm_in_proj")
        if li == 0:
            pass_on(1, h1)
        o, lse, mix = _attn_fwd(qkv3, lw["attn_g"], 2, S, n_heads, f"l{li}_attn_fwd")
        pin = None
        if li == 0:
            weights.update(fetch(1, o))
            pin = pass_on(3, pass_on(2, o))
        mix = _mix_conv_fwd(cv3, lw["taps"], lw["conv_g"], mix, S, f"l{li}_mix_conv_fwd", pin)
        x_mid = _proj_residual(mix, weights[li, "w_out"], xc, tm, f"l{li}_out_proj")
        if li == 0:
            weights.update(fetch(2, x_mid))
        Fd = weights[li, "ffn_up"].shape[2] // 2
        h2, up3 = _norm_proj(x_mid, lw["norm2"], weights[li, "ffn_up"], ((2, Fd, BF16),), tm, DOT_CHUNK,
                             f"l{li}_norm_ffn_up")
        if li == 0:
            weights.update(fetch(3, up3))
        act = _ffn_act_fwd(up3, lw["ffn_taps"], S, f"l{li}_ffn_act_fwd")
        pin = pass_on(li + 4, act) if li + 1 < n_layers else None
        x_out = _proj_residual(act.reshape(1, T, Fd), weights[li, "ffn_down"], x_mid, tm, f"l{li}_ffn_down", pin)
        if li + 1 < n_layers:
            weights.update(fetch(li + 4, x_out))
        saved.append(dict(x_in=xc, h1=h1, qkv3=qkv3, cv3=cv3, o=o, lse=lse, mix=mix, x_mid=x_mid, h2=h2, up3=up3,
                          act=act))
        xc = x_out

    dx, d_final_g, loss_part = _final_norm_loss(xc, final_g, target2, tm, "final_norm_loss")

    small = [None] * n_layers
    started = None
    for li in reversed(range(n_layers)):
        lw, sv = layers[li], saved[li]
        w_in, w_out, ffn_up, ffn_down = (weights[li, n] for n in ("w_in", "w_out", "ffn_up", "ffn_down"))
        dxb, dact3 = _grad_through_weight(dx, ffn_down, 1, Fd, BF16, tm, DOT_CHUNK, f"l{li}_d_act", started)
        Fd = ffn_down.shape[1]
        d_ffn_down = _weight_grad(sv["act"].reshape(1, T, Fd), dxb.reshape(1, T, D), DOT_CHUNK, D,
                                  f"l{li}_d_ffn_down")
        dup3, d_taps_g, d_taps_v = _ffn_act_bwd(sv["up3"], dact3, lw["ffn_taps"], S, f"l{li}_ffn_act_bwd")
        d_ffn_up = _weight_grad(sv["h2"].reshape(1, D, T), dup3, D, DOT_CHUNK, f"l{li}_d_ffn_up", a_transposed=True)
        if li == 0:
            early = reduce_begin({(li, "ffn_down"): d_ffn_down, (li, "ffn_up"): d_ffn_up})
        dx_mid, d_norm2 = _grad_through_proj_norm(dup3, ffn_up, sv["x_mid"], lw["norm2"], dx, tm,
                                                  f"l{li}_d_norm2")
        started = reduce_commit(early, dx_mid) if li == 0 else None
        dxmb, dmix3 = _grad_through_weight(dx_mid, w_out, 2, C, F32, tm, min(C, 512), f"l{li}_d_mix", started)
        d_w_out = _weight_grad(sv["mix"], dxmb.reshape(1, T, D), min(C, 256), D, f"l{li}_d_w_out")
        dproj, d_attn_g = _attn_bwd(sv["qkv3"], sv["o"], sv["lse"], dmix3, lw["attn_g"], 6, S, n_heads,
                                    f"l{li}_attn_bwd")
        dproj, d_taps, d_conv_g = _mix_conv_bwd(sv["cv3"], dmix3, lw["taps"], lw["conv_g"], dproj, S,
                                                f"l{li}_mix_conv_bwd")
        d_w_in = _weight_grad(sv["h1"].reshape(1, D, T), dproj, D, C, f"l{li}_d_w_in", a_transposed=True)
        late = {(li, "w_out"): d_w_out, (li, "w_in"): d_w_in}
        if li > 0:
            late.update({(li, "ffn_down"): d_ffn_down, (li, "ffn_up"): d_ffn_up})
        late = reduce_begin(late)
        dx, d_norm1 = _grad_through_proj_norm(dproj, w_in, sv["x_in"], lw["norm1"], dx_mid, tm,
                                              f"l{li}_d_norm1")
        started = reduce_commit(late, dx)
        small[li] = dict(norm1=d_norm1, taps=d_taps, attn_g=d_attn_g, conv_g=d_conv_g, norm2=d_norm2,
                         ffn_taps=jnp.concatenate([d_taps_g, d_taps_v], axis=1))
    return loss_part, dx, small, d_final_g


SMALL_ORDER = ("norm1", "attn_g", "conv_g", "norm2", "taps", "ffn_taps")


def _pack_small(small, d_final_g, loss_row):
    parts = [small[li][k].reshape(-1) for li in range(len(small)) for k in SMALL_ORDER]
    loss_rows = jnp.tile(loss_row.reshape(1, LANES), (8, 1))
    return jnp.concatenate(parts + [d_final_g.reshape(-1), loss_rows.reshape(-1)]).reshape(-1, LANES)


def _unpack_small(pack, small, d_final_g):
    flat = pack.reshape(-1)
    out, pos = [dict() for _ in small], 0
    for li in range(len(small)):
        for k in SMALL_ORDER:
            n = small[li][k].size
            out[li][k] = flat[pos:pos + n].reshape(small[li][k].shape)
            pos += n
    return out, flat[pos:pos + d_final_g.size], flat[pos + d_final_g.size]


def kernel(x, norm1_g, w_in, mix_conv_w, attn_out_g, conv_out_g, w_out, norm2_g, ffn_up, ffn_conv_w, ffn_down, final_norm_g, loss_target, m_norm1_g, m_w_in, m_mix_conv_w, m_attn_out_g, m_conv_out_g, m_w_out, m_norm2_g, m_ffn_up, m_ffn_conv_w, m_ffn_down, m_final_norm_g, v_norm1_g, v_w_in, v_mix_conv_w, v_attn_out_g, v_conv_out_g, v_w_out, v_norm2_g, v_ffn_up, v_ffn_conv_w, v_ffn_down, v_final_norm_g):
    Bl, S, D = x.shape
    L = w_in.shape[0]
    T = Bl * S
    shard = 2 * lax.axis_index("x") + lax.axis_index("y")
    where = jnp.stack([shard, lax.axis_index("c")]).astype(jnp.int32)
    big_names = ("w_in", "w_out", "ffn_up", "ffn_down")

    taps_w, ftaps_w = mix_conv_w.shape[2], ffn_conv_w.shape[2]
    tap_pack = _gather_over_chips(
        jnp.concatenate([mix_conv_w.reshape(-1), ffn_conv_w.reshape(-1)]).reshape(-1, LANES), "all_gather_taps")
    by_chip = tap_pack.reshape(4, -1)
    n_taps = mix_conv_w.size
    taps_full = by_chip[:, :n_taps].reshape(4, L, 3, taps_w).transpose(1, 2, 0, 3).reshape(L, 3, 4 * taps_w)
    ftaps_full = by_chip[:, n_taps:].reshape(4, L, 3, ftaps_w).transpose(1, 2, 0, 3).reshape(L, 3, 4 * ftaps_w)

    big_shards = dict(zip(big_names, (w_in, w_out, ffn_up, ffn_down)))
    col_of = dict(zip(big_names, COL_SHARDED))
    groups = [[(0, n)] for n in big_names] + [[(l, n) for n in big_names] for l in range(1, L)]
    sems, in_flight = [], {}
    all_started = tap_pack
    for first, last in ((0, 1), (1, len(groups))):
        keys = [k for g in groups[first:last] for k in g]
        new_sems, arrays = _gather_start(
            [_cast_into_full(big_shards[n], l, col_of[n], where, f"cast_{n}_{l}") for l, n in keys],
            [col_of[n] for _, n in keys], [len(g) for g in groups[first:last]], all_started,
            f"gather_start_{first}")
        sems += new_sems
        in_flight.update(zip(keys, arrays))
        all_started = arrays[-1]

    def pass_on(g, after):
        after = all_started if g == 0 else after
        sems[g], arrays = _gather_pass([in_flight[k] for k in groups[g]], [col_of[n] for _, n in groups[g]],
                                       sems[g], after, f"gather_pass_{g}")
        in_flight.update(zip(groups[g], arrays))
        return arrays[0]

    def fetch(g, after):
        done = _gather_wait([in_flight[k] for k in groups[g]], [col_of[n] for _, n in groups[g]], sems[g], after,
                            f"gather_wait_{g}")
        return dict(zip(groups[g], done))

    pending = []

    begun = []

    def reduce_begin(grads):
        g = len(begun)
        keys = list(grads)
        cols = [col_of[n] for _, n in keys]
        begun.append((g, keys, cols) + _exchange_start([grads[k] for k in keys], cols, f"exchange_start_{g}"))
        return begun[-1]

    def reduce_commit(handle, after):
        g, keys, cols, ex_sems, mine, lands = handle
        mine, others = _exchange_wait(mine, lands, cols, ex_sems, after, f"exchange_wait_{g}")
        chip_sums = [_chip_sum(m, o, cw, where, f"chip_sum_{k[1]}_{k[0]}")
                     for k, m, o, cw in zip(keys, mine, others, cols)]
        pending.append((keys, cols) + _scatter_start(chip_sums, cols, f"scatter_start_{g}"))
        return pending[-1][3][0]

    layers = [dict(norm1=norm1_g[l:l + 1], taps=taps_full[l], attn_g=attn_out_g[l:l + 1],
                   conv_g=conv_out_g[l:l + 1], norm2=norm2_g[l:l + 1], ffn_taps=ftaps_full[l]) for l in range(L)]

    loss_part, dx, small, d_final_g = _local_forward_backward(
        x.reshape(T, D), loss_target.reshape(T, D), S, pass_on, fetch, reduce_begin, reduce_commit, layers,
        final_norm_g.reshape(1, D))

    def finish_group(g, after):
        keys, cols, rs_sems, sources, lands = pending[g]
        sources, lands = _scatter_wait(sources, lands, cols, rs_sems, after, f"scatter_wait_{g}")
        for (l, n), cw, src, land in zip(keys, cols, sources, lands):
            reduced[n] = _owner_sum(src, land, cw, where, l, L, reduced[n], f"owner_sum_{n}_{l}")

    reduced = dict.fromkeys(big_names)
    last_started = pending[-1][3][0]
    for g in range(len(pending) - 1):
        finish_group(g, last_started)
    late_names = [n for n in big_names if any(n == name for _, name in pending[-1][0])]
    early_names = [n for n in big_names if n not in late_names]
    g_big = dict(zip(early_names, _share_with_sibling([reduced[n] for n in early_names],
                                                      [col_of[n] for n in early_names], "grad_share_early")))

    pack = _all_reduce_small(_pack_small(small, d_final_g, loss_part), "all_reduce_small_grads")
    g_small, g_final, loss = _unpack_small(pack, small, d_final_g)

    def stacked(key):
        return jnp.stack([g_small[l][key].reshape(g_small[l][key].shape[-2:] if key.endswith("taps") else (-1,))
                          for l in range(L)])

    g_norm1, g_attn, g_conv, g_norm2 = stacked("norm1"), stacked("attn_g"), stacked("conv_g"), stacked("norm2")
    g_taps = lax.dynamic_slice(stacked("taps"), (0, 0, shard * taps_w), (L, 3, taps_w))
    g_ftaps = lax.dynamic_slice(stacked("ffn_taps"), (0, 0, shard * ftaps_w), (L, 3, ftaps_w))

    grads_out = dict(norm1_g=g_norm1, w_in=None, mix_conv_w=g_taps, attn_out_g=g_attn, conv_out_g=g_conv,
                     w_out=None, norm2_g=g_norm2, ffn_up=None, ffn_conv_w=g_ftaps, ffn_down=None,
                     final_norm_g=g_final)
    weights = dict(norm1_g=norm1_g, w_in=w_in, mix_conv_w=mix_conv_w, attn_out_g=attn_out_g, conv_out_g=conv_out_g,
                   w_out=w_out, norm2_g=norm2_g, ffn_up=ffn_up, ffn_conv_w=ffn_conv_w, ffn_down=ffn_down,
                   final_norm_g=final_norm_g)
    ms = dict(norm1_g=m_norm1_g, w_in=m_w_in, mix_conv_w=m_mix_conv_w, attn_out_g=m_attn_out_g,
              conv_out_g=m_conv_out_g, w_out=m_w_out, norm2_g=m_norm2_g, ffn_up=m_ffn_up, ffn_conv_w=m_ffn_conv_w,
              ffn_down=m_ffn_down, final_norm_g=m_final_norm_g)
    vs = dict(norm1_g=v_norm1_g, w_in=v_w_in, mix_conv_w=v_mix_conv_w, attn_out_g=v_attn_out_g,
              conv_out_g=v_conv_out_g, w_out=v_w_out, norm2_g=v_norm2_g, ffn_up=v_ffn_up, ffn_conv_w=v_ffn_conv_w,
              ffn_down=v_ffn_down, final_norm_g=v_final_norm_g)
    names = list(weights)
    small_names = [n for n in names if n not in big_names]
    delta, new_m, new_v = {}, {}, {}

    def update_big(n):
        shp = weights[n].shape
        two_d = (shp[0] * shp[1], shp[2])
        d_, m_, v_, g_ = _adamw(weights[n].reshape(two_d), g_big[n].reshape(two_d), ms[n].reshape(two_d),
                                vs[n].reshape(two_d), f"adamw_{n}")
        delta[n], new_m[n], new_v[n], grads_out[n] = (a.reshape(shp) for a in (d_, m_, v_, g_))

    for n in early_names:
        update_big(n)
    finish_group(len(pending) - 1, delta[early_names[-1]] if early_names else None)
    g_big.update(zip(late_names, _share_with_sibling([reduced[n] for n in late_names],
                                                     [col_of[n] for n in late_names], "grad_share_late")))
    for n in late_names:
        update_big(n)

    def packed(tree):
        return jnp.concatenate([tree[n].reshape(-1) for n in small_names]).reshape(-1, LANES)

    d_, m_, v_, _ = _adamw(packed(weights), packed(grads_out), packed(ms), packed(vs), "adamw_small")
    pos = 0
    for n in small_names:
        size, shp = weights[n].size, weights[n].shape
        delta[n] = d_.reshape(-1)[pos:pos + size].reshape(shp)
        new_m[n] = m_.reshape(-1)[pos:pos + size].reshape(shp)
        new_v[n] = v_.reshape(-1)[pos:pos + size].reshape(shp)
        pos += size

    return (loss, dx.reshape(Bl, S, D), *[grads_out[n] for n in names], *[delta[n] for n in names],
            *[new_m[n] for n in names], *[new_v[n] for n in names])
```

```python
import functools
import math

import jax
import jax.numpy as jnp
from jax import lax
from jax.experimental import pallas as pl
from jax.experimental.pallas import tpu as pltpu

F32 = jnp.float32
BF16 = jnp.bfloat16
EPS = 1e-6
GROUP = 64
LANES = 128
BAND = 128
DILATIONS = (1, 4, 16)
NEG = -1e30
MIB = 1024 * 1024
MESH_ID = pl.DeviceIdType.MESH

ADAM_LR = 0.001
ADAM_B1 = 0.9
ADAM_B2 = 0.999
ADAM_EPS = 1e-08
ADAM_WD = 0.01
ADAM_STEP = 10


ANY = pl.BlockSpec(memory_space=pl.ANY)


def _in_hbm(x):
    return pltpu.with_memory_space_constraint(x, pltpu.HBM)


VMEM_LIMIT_MIB = 48
VMEM_LIMIT_RESIDENT_WEIGHT_MIB = 56
ROW_TILE = 512
DOT_CHUNK = 256
ATTN_UNROLL = 16
ATTN_ROWS = 256


def _params(sem=None, vmem_mb=VMEM_LIMIT_MIB):
    return pltpu.CompilerParams(dimension_semantics=sem, vmem_limit_bytes=vmem_mb * MIB)


def _nt(a, b):
    return lax.dot_general(a, b, (((1,), (1,)), ((), ())), preferred_element_type=F32)


def _tn(a, b):
    return lax.dot_general(a, b, (((0,), (0,)), ((), ())), preferred_element_type=F32)


def _seg_sum(x, is_a):
    s_a = jnp.sum(jnp.where(is_a, x, 0.0), axis=-1, keepdims=True)
    s_b = jnp.sum(jnp.where(is_a, 0.0, x), axis=-1, keepdims=True)
    return jnp.where(is_a, s_a, s_b)


def _lane_is_a():
    return lax.broadcasted_iota(jnp.int32, (1, LANES), 1) < GROUP


def _norm_proj(x, g, w3, groups, tm, chunk, name):
    T, D = x.shape
    N = w3.shape[2]
    assert sum(p * c for p, c, _ in groups) == N and T % tm == 0

    def body(x_ref, g_ref, w_ref, h_ref, *out_refs):
        xv = x_ref[...]
        rstd = lax.rsqrt(jnp.mean(xv * xv, axis=-1, keepdims=True) + EPS)
        h = ((xv * rstd) * g_ref[...]).astype(BF16)
        h_ref[...] = h.T
        col = 0
        for (pieces, width, dtype), o_ref in zip(groups, out_refs):
            for p in range(pieces):
                for c0 in range(0, width, chunk):
                    acc = jnp.dot(h, w_ref[:, col + c0:col + c0 + chunk], preferred_element_type=F32)
                    o_ref[p, :, c0:c0 + chunk] = acc.astype(dtype)
                col += width

    out_shape = [jax.ShapeDtypeStruct((D, T), BF16)]
    out_specs = [pl.BlockSpec((D, tm), lambda i: (0, i))]
    for pieces, width, dtype in groups:
        assert width % chunk == 0
        out_shape.append(jax.ShapeDtypeStruct((pieces, T, width), dtype))
        out_specs.append(pl.BlockSpec((pieces, tm, width), lambda i: (0, i, 0)))
    return pl.pallas_call(
        body, grid=(T // tm,), name=name,
        in_specs=[pl.BlockSpec((tm, D), lambda i: (i, 0)),
                  pl.BlockSpec((1, D), lambda i: (0, 0)),
                  pl.BlockSpec((None, D, N), lambda i: (0, 0, 0))],
        out_specs=out_specs, out_shape=out_shape,
        compiler_params=_params(("parallel",), VMEM_LIMIT_RESIDENT_WEIGHT_MIB),
    )(x, g, w3)


def _proj_residual(pieces3, w3, x, tm, name, after=None):
    P, T, C = pieces3.shape
    D = w3.shape[2]

    def body(a_ref, w_ref, x_ref, *rest):
        o_ref = rest[-1]
        acc = x_ref[...]
        for p in range(P):
            acc = acc + jnp.dot(a_ref[p], w_ref[p * C:(p + 1) * C, :], preferred_element_type=F32)
        o_ref[...] = acc

    in_specs = [pl.BlockSpec((P, tm, C), lambda i: (0, i, 0)),
                pl.BlockSpec((None, P * C, D), lambda i: (0, 0, 0)),
                pl.BlockSpec((tm, D), lambda i: (i, 0))]
    operands = [pieces3, w3, x]
    if after is not None:
        in_specs.append(ANY)
        operands.append(after)
    return pl.pallas_call(
        body, grid=(T // tm,), name=name,
        in_specs=in_specs,
        out_specs=pl.BlockSpec((tm, D), lambda i: (i, 0)),
        out_shape=jax.ShapeDtypeStruct((T, D), F32),
        compiler_params=_params(("parallel",)),
    )(*operands)


def _grad_through_weight(dy, w3, pieces, width, out_dtype, tm, chunk, name, after=None):
    T, D = dy.shape

    def body(dy_ref, w_ref, *rest):
        dyb_ref, o_ref = rest[-2:]
        dyb = dy_ref[...].astype(BF16)
        dyb_ref[...] = dyb
        for p in range(pieces):
            for c0 in range(0, width, chunk):
                r0 = p * width + c0
                o_ref[p, :, c0:c0 + chunk] = _nt(dyb, w_ref[r0:r0 + chunk, :]).astype(out_dtype)

    in_specs = [pl.BlockSpec((tm, D), lambda i: (i, 0)),
                pl.BlockSpec((None, pieces * width, D), lambda i: (0, 0, 0))]
    operands = [dy, w3]
    if after is not None:
        in_specs.append(ANY)
        operands.append(after)
    return pl.pallas_call(
        body, grid=(T // tm,), name=name,
        in_specs=in_specs,
        out_specs=[pl.BlockSpec((tm, D), lambda i: (i, 0)),
                   pl.BlockSpec((pieces, tm, width), lambda i: (0, i, 0))],
        out_shape=[jax.ShapeDtypeStruct((T, D), BF16),
                   jax.ShapeDtypeStruct((pieces, T, width), out_dtype)],
        compiler_params=_params(("parallel",)),
    )(*operands)


def _grad_through_proj_norm(dp3, w3, x, g, dx_in, tm, name):
    P, T, C = dp3.shape
    D = w3.shape[1]

    def body(dp_ref, w_ref, x_ref, g_ref, dxin_ref, dx_ref, dg_ref):
        dh = _nt(dp_ref[0], w_ref[:, 0:C])
        for p in range(1, P):
            dh = dh + _nt(dp_ref[p], w_ref[:, p * C:(p + 1) * C])
        xv = x_ref[...]
        rstd = lax.rsqrt(jnp.mean(xv * xv, axis=-1, keepdims=True) + EPS)
        xn = xv * rstd
        a = dh * g_ref[...]
        dx_ref[...] = dxin_ref[...] + rstd * (a - xn * jnp.mean(a * xn, axis=-1, keepdims=True))
        part = jnp.sum(dh * xn, axis=0, keepdims=True)

        @pl.when(pl.program_id(0) == 0)
        def _():
            dg_ref[...] = part

        @pl.when(pl.program_id(0) != 0)
        def _():
            dg_ref[...] += part

    return pl.pallas_call(
        body, grid=(T // tm,), name=name,
        in_specs=[pl.BlockSpec((P, tm, C), lambda i: (0, i, 0)),
                  pl.BlockSpec((None, D, P * C), lambda i: (0, 0, 0)),
                  pl.BlockSpec((tm, D), lambda i: (i, 0)),
                  pl.BlockSpec((1, D), lambda i: (0, 0)),
                  pl.BlockSpec((tm, D), lambda i: (i, 0))],
        out_specs=[pl.BlockSpec((tm, D), lambda i: (i, 0)),
                   pl.BlockSpec((1, D), lambda i: (0, 0))],
        out_shape=[jax.ShapeDtypeStruct((T, D), F32), jax.ShapeDtypeStruct((1, D), F32)],
        compiler_params=_params(("arbitrary",), VMEM_LIMIT_RESIDENT_WEIGHT_MIB),
    )(dp3, w3, x, g, dx_in)


def _weight_grad(a3, g3, ta, tg, name, a_transposed=False):
    PG, T, CG = g3.shape
    PA, CA = (a3.shape[0], a3.shape[1]) if a_transposed else (a3.shape[0], a3.shape[2])
    na, ng = CA // ta, CG // tg
    assert CA % ta == 0 and CG % tg == 0

    def body(a_ref, g_ref, o_ref):
        if a_transposed:
            part = jnp.dot(a_ref[...], g_ref[...], preferred_element_type=F32)
        else:
            part = _tn(a_ref[...], g_ref[...])
        o_ref[...] = part.astype(o_ref.dtype)

    a_spec = (pl.BlockSpec((None, ta, T), lambda i, j: (i // na, i % na, 0)) if a_transposed
              else pl.BlockSpec((None, T, ta), lambda i, j: (i // na, 0, i % na)))
    return pl.pallas_call(
        body, grid=(PA * na, PG * ng), name=name,
        in_specs=[a_spec, pl.BlockSpec((None, T, tg), lambda i, j: (j // ng, 0, j % ng))],
        out_specs=pl.BlockSpec((None, ta, tg), lambda i, j: (0, i, j)),
        out_shape=pltpu.HBM((1, PA * CA, PG * CG), BF16),
        compiler_params=_params(("parallel", "parallel"), VMEM_LIMIT_RESIDENT_WEIGHT_MIB),
    )(a3, g3)


def _bias_tables(bm_ref, lone_ref, pair, n_heads, S):
    ii = lax.broadcasted_iota(jnp.int32, (BAND, 2 * BAND), 0)
    jj = lax.broadcasted_iota(jnp.int32, (BAND, 2 * BAND), 1)
    dist = BAND + ii - jj
    valid = (dist >= 0) & (dist <= BAND)
    distf = dist.astype(F32)
    for hh in range(2):
        head = (2 * pair + hh + 1).astype(F32)
        slope = jnp.exp(jnp.full((1, 1), -8.0 / n_heads * math.log(2.0), F32) * head)
        for bi, d in enumerate(DILATIONS):
            table = jnp.where(valid, -(slope * d) * distf, NEG)
            bm_ref[bi, hh * BAND:(hh + 1) * BAND, :] = table
            if S // (BAND * d) == 1:
                lone_ref[bi, hh * BAND:(hh + 1) * BAND, :] = table[:, BAND:2 * BAND]


def _stack_heads(x, is_a):
    zero = jnp.zeros_like(x)
    return jnp.concatenate([jnp.where(is_a, x, zero), jnp.where(is_a, zero, x)], axis=0)


def _unstack_heads(x2, is_a):
    return jnp.where(is_a, x2[0:BAND], x2[BAND:2 * BAND])


def _gather_residues(dst_ref, src, d, S, convert):
    L = S // d
    for r in range(d):
        rows = pl.ds(r, L, stride=d) if d > 1 else slice(None)
        dst_ref[r * L:(r + 1) * L, :] = convert(src(rows))


def _block_rows(t, d, S):
    nb = S // (BAND * d)
    n = t % nb
    has_prev = jnp.minimum(n, 1)
    cur = pl.ds(pl.multiple_of(t * BAND, BAND), BAND)
    prev = pl.ds(pl.multiple_of((t - has_prev) * BAND, BAND), BAND)
    return cur, prev, has_prev


def _first_block_penalty(has_prev):
    jrow = lax.broadcasted_iota(jnp.int32, (1, 2 * BAND), 1)
    pen = jnp.where(has_prev == 0, NEG, 0.0).astype(F32)
    return jnp.where(jrow < BAND, pen, 0.0)


def _attn_fwd(qkv3, gain, mix_shape_pieces, S, n_heads, name):
    _, T, C = qkv3.shape
    B, P = T // S, C // LANES
    NBLK = S // BAND
    scale = GROUP ** -0.5
    nbr = len(DILATIONS)
    RC = ATTN_ROWS

    def body(qkv_ref, g_ref, o_ref, lse_ref, an_ref, qs, ks, vs, op, mp, lp, ob, mb, lb, bm, bml):
        pair = pl.program_id(1)
        is_a = _lane_is_a()
        _bias_tables(bm, bml, pair, n_heads, S)

        for bi, d in enumerate(DILATIONS):
            nb = S // (BAND * d)
            _gather_residues(qs, lambda rows: qkv_ref.at[0][rows, :], d, S, lambda v: (v * scale).astype(BF16))
            _gather_residues(ks, lambda rows: qkv_ref.at[1][rows, :], d, S, lambda v: v.astype(BF16))
            _gather_residues(vs, lambda rows: qkv_ref.at[2][rows, :], d, S, lambda v: v.astype(BF16))
            o_dst, m_dst, l_dst = (ob.at[bi], mb.at[bi], lb.at[bi]) if d == 1 else (op, mp, lp)

            def block(t, carry, bi=bi, d=d, nb=nb, o_dst=o_dst, m_dst=m_dst, l_dst=l_dst):
                cur, prev, has_prev = _block_rows(t, d, S)
                q2 = _stack_heads(qs[cur, :], is_a)
                if nb == 1:
                    kc, vc = ks[cur, :], vs[cur, :]
                    s = _nt(q2, kc) + bml[bi]
                else:
                    kc = jnp.concatenate([ks[prev, :], ks[cur, :]], axis=0)
                    vc = jnp.concatenate([vs[prev, :], vs[cur, :]], axis=0)
                    s = _nt(q2, kc) + bm[bi] + _first_block_penalty(has_prev)
                m = jnp.max(s, axis=-1, keepdims=True)
                e = jnp.exp(s - m)
                l = jnp.sum(e, axis=-1, keepdims=True)
                pv = jnp.dot(e.astype(BF16), vc, preferred_element_type=F32)
                o_dst[cur, :] = _unstack_heads(pv, is_a)
                m_dst[cur, :] = _unstack_heads(m, is_a)
                l_dst[cur, :] = _unstack_heads(l, is_a)
                return carry

            lax.fori_loop(0, NBLK, block, 0, unroll=ATTN_UNROLL)
            if d > 1:
                L = S // d
                for r in range(d):
                    rows = pl.ds(r, L, stride=d)
                    ob.at[bi][rows, :] = op[r * L:(r + 1) * L, :]
                    mb.at[bi][rows, :] = mp[r * L:(r + 1) * L, :]
                    lb.at[bi][rows, :] = lp[r * L:(r + 1) * L, :]

        def finish(ci, carry):
            rs = pl.ds(pl.multiple_of(ci * RC, RC), RC)
            ms = [mb[bi, rs, :] for bi in range(nbr)]
            mmax = functools.reduce(jnp.maximum, ms)
            ws = [jnp.exp(m - mmax) for m in ms]
            num = sum(ob[bi, rs, :] * ws[bi] for bi in range(nbr))
            den = sum(lb[bi, rs, :] * ws[bi] for bi in range(nbr))
            o = num / den
            o_ref[rs, :] = o
            lse_ref[rs, :] = mmax + jnp.log(den)
            rstd = lax.rsqrt(_seg_sum(o * o, is_a) * (1.0 / GROUP) + EPS)
            an_ref[rs, :] = ((o * rstd) * g_ref[...]).astype(BF16)
            return carry

        lax.fori_loop(0, S // RC, finish, 0, unroll=True)

    seq = pl.BlockSpec((S, LANES), lambda b, p: (b, p))
    return pl.pallas_call(
        body, grid=(B, P), name=name,
        in_specs=[pl.BlockSpec((3, S, LANES), lambda b, p: (0, b, p)),
                  pl.BlockSpec((1, LANES), lambda b, p: (0, p))],
        out_specs=[seq, seq, pl.BlockSpec((None, S, LANES), lambda b, p: (0, b, p))],
        out_shape=[jax.ShapeDtypeStruct((T, C), F32), jax.ShapeDtypeStruct((T, C), F32),
                   jax.ShapeDtypeStruct((mix_shape_pieces, T, C), BF16)],
        scratch_shapes=[pltpu.VMEM((S, LANES), BF16)] * 3 + [pltpu.VMEM((S, LANES), F32)] * 3
        + [pltpu.VMEM((nbr, S, LANES), F32)] * 3
        + [pltpu.VMEM((nbr, 2 * BAND, 2 * BAND), F32), pltpu.VMEM((nbr, 2 * BAND, BAND), F32)],
        compiler_params=_params(("parallel", "parallel")),
    )(qkv3, gain)


def _attn_bwd(qkv3, o, lse, dmix3, gain, dproj_pieces, S, n_heads, name):
    _, T, C = qkv3.shape
    B, P = T // S, C // LANES
    NBLK = S // BAND
    scale = GROUP ** -0.5
    nbr = len(DILATIONS)
    RC = ATTN_ROWS

    def body(qkv_ref, o_ref, lse_ref, dn_ref, g_ref, dqkv_ref, dg_ref,
             do_n, dd_n, qs, ks, vs, dos, lses, dds, dqp, dkp, dvp, dqn, dkn, dvn, bm, bml):
        pair = pl.program_id(0)
        b = pl.program_id(1)
        is_a = _lane_is_a()
        _bias_tables(bm, bml, pair, n_heads, S)

        def prologue(ci, dg_acc):
            rs = pl.ds(pl.multiple_of(ci * RC, RC), RC)
            ov = o_ref[rs, :]
            dn = dn_ref[rs, :]
            rstd = lax.rsqrt(_seg_sum(ov * ov, is_a) * (1.0 / GROUP) + EPS)
            on = ov * rstd
            a = dn * g_ref[...]
            s_a = _seg_sum(a * on, is_a)
            do_n[rs, :] = rstd * (a - on * (s_a * (1.0 / GROUP)))
            dd_n[rs, :] = (EPS * s_a) * (rstd * rstd)
            zero = jnp.zeros((RC, LANES), F32)
            dqn[rs, :] = zero
            dkn[rs, :] = zero
            dvn[rs, :] = zero
            return dg_acc + jnp.sum(dn * on, axis=0, keepdims=True)

        dg_part = lax.fori_loop(0, S // RC, prologue, jnp.zeros((1, LANES), F32), unroll=True)

        @pl.when(b == 0)
        def _():
            dg_ref[...] = dg_part

        @pl.when(b != 0)
        def _():
            dg_ref[...] += dg_part

        for bi, d in enumerate(DILATIONS):
            nb = S // (BAND * d)
            L = S // d
            _gather_residues(qs, lambda rows: qkv_ref.at[0][rows, :], d, S, lambda v: (v * scale).astype(BF16))
            _gather_residues(ks, lambda rows: qkv_ref.at[1][rows, :], d, S, lambda v: v.astype(BF16))
            _gather_residues(vs, lambda rows: qkv_ref.at[2][rows, :], d, S, lambda v: v.astype(BF16))
            _gather_residues(dos, lambda rows: do_n[rows, :], d, S, lambda v: v.astype(BF16))
            if d == 1:
                lse_src, dd_src, dq_dst, dk_dst, dv_dst = lse_ref, dd_n, dqn, dkn, dvn
            else:
                _gather_residues(lses, lambda rows: lse_ref[rows, :], d, S, lambda v: v)
                _gather_residues(dds, lambda rows: dd_n[rows, :], d, S, lambda v: v)
                dkp[...] = jnp.zeros((S, LANES), F32)
                dvp[...] = jnp.zeros((S, LANES), F32)
                lse_src, dd_src, dq_dst, dk_dst, dv_dst = lses, dds, dqp, dkp, dvp

            def block(t, carry, bi=bi, d=d, nb=nb, lse_src=lse_src, dd_src=dd_src, dq_dst=dq_dst, dk_dst=dk_dst,
                      dv_dst=dv_dst):
                cur, prev, has_prev = _block_rows(t, d, S)
                q2 = _stack_heads(qs[cur, :], is_a)
                do2 = _stack_heads(dos[cur, :], is_a)
                lse_t = lse_src[cur, :]
                dd_t = dd_src[cur, :]
                lse2 = jnp.concatenate([lse_t[:, 0:1], lse_t[:, GROUP:GROUP + 1]], axis=0)
                dd2 = jnp.concatenate([dd_t[:, 0:1], dd_t[:, GROUP:GROUP + 1]], axis=0)
                if nb == 1:
                    kc, vc = ks[cur, :], vs[cur, :]
                    s = _nt(q2, kc) + bml[bi]
                else:
                    kc = jnp.concatenate([ks[prev, :], ks[cur, :]], axis=0)
                    vc = jnp.concatenate([vs[prev, :], vs[cur, :]], axis=0)
                    s = _nt(q2, kc) + bm[bi] + _first_block_penalty(has_prev)
                p = jnp.exp(s - lse2)
                ds = (p * (_nt(do2, vc) - dd2)).astype(BF16)
                dq = _unstack_heads(jnp.dot(ds, kc, preferred_element_type=F32), is_a)
                dk = _tn(ds, q2)
                dv = _tn(p.astype(BF16), do2)
                dq_dst[cur, :] = dq
                if nb == 1:
                    dk_dst[cur, :] += dk
                    dv_dst[cur, :] += dv
                else:
                    dk_dst[prev, :] += dk[0:BAND, :]
                    dv_dst[prev, :] += dv[0:BAND, :]
                    dk_dst[cur, :] += dk[BAND:2 * BAND, :]
                    dv_dst[cur, :] += dv[BAND:2 * BAND, :]
                return carry

            lax.fori_loop(0, NBLK, block, 0, unroll=ATTN_UNROLL)
            if d > 1:
                for r in range(d):
                    rows = pl.ds(r, L, stride=d)
                    dqn[rows, :] += dqp[r * L:(r + 1) * L, :]
                    dkn[rows, :] += dkp[r * L:(r + 1) * L, :]
                    dvn[rows, :] += dvp[r * L:(r + 1) * L, :]

        dqkv_ref[0] = (dqn[...] * scale).astype(BF16)
        dqkv_ref[1] = dkn[...].astype(BF16)
        dqkv_ref[2] = dvn[...].astype(BF16)

    seq = pl.BlockSpec((S, LANES), lambda p, b: (b, p))
    f32_seq = pltpu.VMEM((S, LANES), F32)
    bf_seq = pltpu.VMEM((S, LANES), BF16)
    return pl.pallas_call(
        body, grid=(P, B), name=name,
        in_specs=[pl.BlockSpec((3, S, LANES), lambda p, b: (0, b, p)), seq, seq,
                  pl.BlockSpec((None, S, LANES), lambda p, b: (0, b, p)),
                  pl.BlockSpec((1, LANES), lambda p, b: (0, p))],
        out_specs=[pl.BlockSpec((3, S, LANES), lambda p, b: (0, b, p)),
                   pl.BlockSpec((1, LANES), lambda p, b: (0, p))],
        out_shape=[jax.ShapeDtypeStruct((dproj_pieces, T, C), BF16), jax.ShapeDtypeStruct((1, C), F32)],
        scratch_shapes=[f32_seq, f32_seq, bf_seq, bf_seq, bf_seq, bf_seq, f32_seq, f32_seq,
                        f32_seq, f32_seq, f32_seq, f32_seq, f32_seq, f32_seq,
                        pltpu.VMEM((nbr, 2 * BAND, 2 * BAND), F32), pltpu.VMEM((nbr, 2 * BAND, BAND), F32)],
        compiler_params=_params(("parallel", "arbitrary")),
    )(qkv3, o, lse, dmix3, gain)


def _delay(x, k, row):
    return jnp.where(row >= k, pltpu.roll(x, k, 0), 0.0)


def _advance(x, k, row, S):
    return jnp.where(row < S - k, pltpu.roll(x, S - k, 0), 0.0)


def _conv3(x, w, row):
    return (w[0:1, :] * _delay(x, 2, row) + w[1:2, :] * _delay(x, 1, row)) + w[2:3, :] * x


HALO = 8


CONV_ROWS = 64
FFN_LANES = 128
IN_BUFFERS = 3


def _zero_halo(pad_ref, S):
    zeros = jnp.zeros((HALO, pad_ref.shape[1]), pad_ref.dtype)
    pad_ref[0:HALO, :] = zeros
    pad_ref[HALO + S:2 * HALO + S, :] = zeros


def _window_at(pad_ref, r0, shift):
    return pad_ref[HALO + r0 + shift:HALO + r0 + shift + CONV_ROWS, :]


def _conv3_at(pad_ref, w, r0):
    return ((w[0:1, :] * _window_at(pad_ref, r0, -2) + w[1:2, :] * _window_at(pad_ref, r0, -1))
            + w[2:3, :] * _window_at(pad_ref, r0, 0))


def _conv3_grads_at(dz_ref, x_ref, w, r0):
    dz, dz1, dz2 = (_window_at(dz_ref, r0, k) for k in range(3))
    x = _window_at(x_ref, r0, 0)
    dx = (w[2:3, :] * dz + w[1:2, :] * dz1) + w[0:1, :] * dz2
    parts = [jnp.sum((d * x).reshape(CONV_ROWS // 8, 8, x.shape[1]), axis=0) for d in (dz2, dz1, dz)]
    return dx, parts


def _conv3_grads(dz, x, w, row, S):
    dz1 = _advance(dz, 1, row, S)
    dz2 = _advance(dz, 2, row, S)
    dx = (w[2:3, :] * dz + w[1:2, :] * dz1) + w[0:1, :] * dz2
    dw = jnp.concatenate([jnp.sum(dz2 * x, axis=0, keepdims=True),
                          jnp.sum(dz1 * x, axis=0, keepdims=True),
                          jnp.sum(dz * x, axis=0, keepdims=True)], axis=0)
    return dx, dw


def _mix_conv_fwd(cv3, taps, gain, mix, S, name, after=None):
    _, T, C = cv3.shape
    B, P = T // S, C // LANES

    def body(cv_ref, w_ref, g_ref, mix_hbm, *rest):
        y_ref, pad_c = rest[-2:]
        del mix_hbm
        is_a = _lane_is_a()
        _zero_halo(pad_c, S)
        pad_c[HALO:HALO + S, :] = cv_ref[1].astype(F32) * cv_ref[2].astype(F32)
        w = w_ref[...]
        for r0 in range(0, S, CONV_ROWS):
            y = cv_ref[0, r0:r0 + CONV_ROWS, :].astype(F32) * _conv3_at(pad_c, w, r0)
            rstd = lax.rsqrt(_seg_sum(y * y, is_a) * (1.0 / GROUP) + EPS)
            y_ref[r0:r0 + CONV_ROWS, :] = ((y * rstd) * g_ref[...]).astype(BF16)

    in_specs = [pl.BlockSpec((3, S, LANES), lambda b, p: (0, b, p)),
                pl.BlockSpec((3, LANES), lambda b, p: (0, p)),
                pl.BlockSpec((1, LANES), lambda b, p: (0, p)),
                ANY]
    operands = [cv3, taps, gain, mix]
    if after is not None:
        in_specs.append(ANY)
        operands.append(after)
    return pl.pallas_call(
        body, grid=(B, P), name=name,
        in_specs=in_specs,
        out_specs=pl.BlockSpec((None, S, LANES), lambda b, p: (1, b, p)),
        out_shape=jax.ShapeDtypeStruct(mix.shape, mix.dtype),
        scratch_shapes=[pltpu.VMEM((S + 2 * HALO, LANES), F32)],
        input_output_aliases={3: 0},
        compiler_params=_params(("parallel", "parallel")),
    )(*operands)


def _mix_conv_bwd(cv3, dmix3, taps, gain, dproj, S, name):
    _, T, C = cv3.shape
    B, P = T // S, C // LANES

    def body(cv_ref, dn_ref, w_ref, g_ref, dproj_hbm, dcv_ref, dw_ref, dg_ref):
        del dproj_hbm
        b = pl.program_id(1)
        row = lax.broadcasted_iota(jnp.int32, (S, 1), 0)
        is_a = _lane_is_a()
        w = w_ref[...]
        gb = cv_ref[0].astype(F32)
        gc = cv_ref[1].astype(F32)
        u = cv_ref[2].astype(F32)
        c = gc * u
        z = _conv3(c, w, row)
        y = gb * z
        rstd = lax.rsqrt(_seg_sum(y * y, is_a) * (1.0 / GROUP) + EPS)
        yn = y * rstd
        dn = dn_ref[...]
        a = dn * g_ref[...]
        dy = rstd * (a - yn * (_seg_sum(a * yn, is_a) * (1.0 / GROUP)))
        dg = jnp.sum(dn * yn, axis=0, keepdims=True)
        dc, dw = _conv3_grads(dy * gb, c, w, row, S)
        dcv_ref[0] = (dy * z).astype(BF16)
        dcv_ref[1] = (dc * u).astype(BF16)
        dcv_ref[2] = (dc * gc).astype(BF16)

        @pl.when(b == 0)
        def _():
            dw_ref[...] = dw
            dg_ref[...] = dg

        @pl.when(b != 0)
        def _():
            dw_ref[...] += dw
            dg_ref[...] += dg

    return pl.pallas_call(
        body, grid=(P, B), name=name,
        in_specs=[pl.BlockSpec((3, S, LANES), lambda p, b: (0, b, p)),
                  pl.BlockSpec((None, S, LANES), lambda p, b: (1, b, p)),
                  pl.BlockSpec((3, LANES), lambda p, b: (0, p)),
                  pl.BlockSpec((1, LANES), lambda p, b: (0, p)),
                  pl.BlockSpec(memory_space=pl.ANY)],
        out_specs=[pl.BlockSpec((3, S, LANES), lambda p, b: (1, b, p)),
                   pl.BlockSpec((3, LANES), lambda p, b: (0, p)),
                   pl.BlockSpec((1, LANES), lambda p, b: (0, p))],
        out_shape=[jax.ShapeDtypeStruct(dproj.shape, dproj.dtype),
                   jax.ShapeDtypeStruct((3, C), F32), jax.ShapeDtypeStruct((1, C), F32)],
        input_output_aliases={4: 0},
        compiler_params=_params(("parallel", "arbitrary")),
    )(cv3, dmix3, taps, gain, dproj)


def _sigmoid(x):
    return 0.5 * jnp.tanh(0.5 * x) + 0.5


def _ffn_act_fwd(up3, taps, S, name):
    _, T, Fd = up3.shape
    W = FFN_LANES
    B, P = T // S, Fd // W
    n_steps = B * P
    taps3 = taps.reshape(3, 2 * P, W).transpose(1, 0, 2)

    def body(up_hbm, w_ref, act_hbm, inbuf, outbuf, in_sems, out_sems, pad_g, pad_v):
        def block_of(i):
            rows = pl.ds(pl.multiple_of((i // P) * S, S), S)
            cols = pl.ds(pl.multiple_of((i % P) * W, W), W)
            return rows, cols

        def fetch(i, slot):
            rows, cols = block_of(i)
            return pltpu.make_async_copy(up_hbm.at[:, rows, cols], inbuf.at[slot], in_sems.at[slot])

        def write_back(i, slot):
            rows, cols = block_of(i)
            return pltpu.make_async_copy(outbuf.at[slot], act_hbm.at[rows, cols], out_sems.at[slot])

        _zero_halo(pad_g, S)
        _zero_halo(pad_v, S)
        for k in range(IN_BUFFERS):
            fetch(k, k).start()

        def step(i, carry):
            slot = i % IN_BUFFERS
            oslot = i % 2
            fetch(i, slot).wait()
            pad_g[HALO:HALO + S, :] = inbuf[slot, 0].astype(F32)
            pad_v[HALO:HALO + S, :] = inbuf[slot, 1].astype(F32)

            @pl.when(i + IN_BUFFERS < n_steps)
            def _():
                fetch(i + IN_BUFFERS, slot).start()

            @pl.when(i >= 2)
            def _():
                write_back(i - 2, oslot).wait()

            wg = w_ref[i % P]
            wv = w_ref[P + i % P]
            for r0 in range(0, S, CONV_ROWS):
                cg = _conv3_at(pad_g, wg, r0)
                cv = _conv3_at(pad_v, wv, r0)
                outbuf[oslot, r0:r0 + CONV_ROWS, :] = ((cg * _sigmoid(cg)) * cv).astype(BF16)
            write_back(i, oslot).start()
            return carry

        lax.fori_loop(0, n_steps, step, 0)
        write_back(n_steps - 2, n_steps % 2).wait()
        write_back(n_steps - 1, (n_steps - 1) % 2).wait()

    return pl.pallas_call(
        body, name=name,
        in_specs=[ANY, pl.BlockSpec(memory_space=pltpu.VMEM)],
        out_specs=ANY,
        out_shape=jax.ShapeDtypeStruct((T, Fd), BF16),
        scratch_shapes=[pltpu.VMEM((IN_BUFFERS, 2, S, W), BF16), pltpu.VMEM((2, S, W), BF16),
                        pltpu.SemaphoreType.DMA((IN_BUFFERS,)), pltpu.SemaphoreType.DMA((2,)),
                        pltpu.VMEM((S + 2 * HALO, W), F32), pltpu.VMEM((S + 2 * HALO, W), F32)],
        compiler_params=_params(),
    )(up3, taps3)


def _ffn_act_bwd(up3, dact3, taps, S, name):
    _, T, Fd = up3.shape
    W = FFN_LANES
    B, P = T // S, Fd // W
    n_steps = B * P
    taps3 = taps.reshape(3, 2 * P, W).transpose(1, 0, 2)

    def body(up_hbm, da_hbm, w_ref, dup_hbm, dwg_ref, dwv_ref, upbuf, dabuf, outbuf, up_sems, da_sems, out_sems,
             pad_ug, pad_uv, pad_dg, pad_dv):
        def block_of(i):
            rows = pl.ds(pl.multiple_of((i % B) * S, S), S)
            cols = pl.ds(pl.multiple_of((i // B) * W, W), W)
            return rows, cols

        def fetches(i, slot):
            rows, cols = block_of(i)
            return (pltpu.make_async_copy(up_hbm.at[:, rows, cols], upbuf.at[slot], up_sems.at[slot]),
                    pltpu.make_async_copy(da_hbm.at[0, rows, cols], dabuf.at[slot], da_sems.at[slot]))

        def write_back(i, slot):
            rows, cols = block_of(i)
            return pltpu.make_async_copy(outbuf.at[slot], dup_hbm.at[:, rows, cols], out_sems.at[slot])

        for pad in (pad_ug, pad_uv, pad_dg, pad_dv):
            _zero_halo(pad, S)
        for k in range(IN_BUFFERS):
            for cp in fetches(k, k):
                cp.start()

        def step(i, carry):
            slot = i % IN_BUFFERS
            oslot = i % 2
            p = i // B
            for cp in fetches(i, slot):
                cp.wait()
            pad_ug[HALO:HALO + S, :] = upbuf[slot, 0].astype(F32)
            pad_uv[HALO:HALO + S, :] = upbuf[slot, 1].astype(F32)
            wg = w_ref[p]
            wv = w_ref[P + p]
            for r0 in range(0, S, CONV_ROWS):
                cg = _conv3_at(pad_ug, wg, r0)
                cv = _conv3_at(pad_uv, wv, r0)
                sg = _sigmoid(cg)
                da = dabuf[slot, r0:r0 + CONV_ROWS, :].astype(F32)
                t = da * sg
                dcv = cg * t
                pad_dg[HALO + r0:HALO + r0 + CONV_ROWS, :] = cv * ((t + dcv) - dcv * sg)
                pad_dv[HALO + r0:HALO + r0 + CONV_ROWS, :] = dcv

            @pl.when(i + IN_BUFFERS < n_steps)
            def _():
                for cp in fetches(i + IN_BUFFERS, slot):
                    cp.start()

            @pl.when(i >= 2)
            def _():
                write_back(i - 2, oslot).wait()

            sums_g = [jnp.zeros((8, W), F32)] * 3
            sums_v = [jnp.zeros((8, W), F32)] * 3
            for r0 in range(0, S, CONV_ROWS):
                dug, parts_g = _conv3_grads_at(pad_dg, pad_ug, wg, r0)
                duv, parts_v = _conv3_grads_at(pad_dv, pad_uv, wv, r0)
                outbuf[oslot, 0, r0:r0 + CONV_ROWS, :] = dug.astype(BF16)
                outbuf[oslot, 1, r0:r0 + CONV_ROWS, :] = duv.astype(BF16)
                sums_g = [a + q for a, q in zip(sums_g, parts_g)]
                sums_v = [a + q for a, q in zip(sums_v, parts_v)]
            write_back(i, oslot).start()
            dwg = jnp.concatenate([jnp.sum(a, axis=0, keepdims=True) for a in sums_g], axis=0)
            dwv = jnp.concatenate([jnp.sum(a, axis=0, keepdims=True) for a in sums_v], axis=0)

            @pl.when(i % B == 0)
            def _():
                dwg_ref[p] = dwg
                dwv_ref[p] = dwv

            @pl.when(i % B != 0)
            def _():
                dwg_ref[p] += dwg
                dwv_ref[p] += dwv

            return carry

        lax.fori_loop(0, n_steps, step, 0)
        write_back(n_steps - 2, n_steps % 2).wait()
        write_back(n_steps - 1, (n_steps - 1) % 2).wait()

    vmem = pl.BlockSpec(memory_space=pltpu.VMEM)
    dup3, dwg3, dwv3 = pl.pallas_call(
        body, name=name,
        in_specs=[ANY, ANY, vmem],
        out_specs=[ANY, vmem, vmem],
        out_shape=[jax.ShapeDtypeStruct((2, T, Fd), BF16),
                   jax.ShapeDtypeStruct((P, 3, W), F32), jax.ShapeDtypeStruct((P, 3, W), F32)],
        scratch_shapes=[pltpu.VMEM((IN_BUFFERS, 2, S, W), BF16), pltpu.VMEM((IN_BUFFERS, S, W), BF16),
                        pltpu.VMEM((2, 2, S, W), BF16),
                        pltpu.SemaphoreType.DMA((IN_BUFFERS,)), pltpu.SemaphoreType.DMA((IN_BUFFERS,)),
                        pltpu.SemaphoreType.DMA((2,))]
        + [pltpu.VMEM((S + 2 * HALO, W), F32)] * 4,
        compiler_params=_params(),
    )(up3, dact3, taps3)
    return dup3, dwg3.transpose(1, 0, 2).reshape(3, Fd), dwv3.transpose(1, 0, 2).reshape(3, Fd)


def _final_norm_loss(x, g, target, tm, name):
    T, D = x.shape

    def body(x_ref, g_ref, t_ref, dx_ref, dg_ref, loss_ref):
        xv = x_ref[...]
        rstd = lax.rsqrt(jnp.mean(xv * xv, axis=-1, keepdims=True) + EPS)
        xn = xv * rstd
        err = xn * g_ref[...] - t_ref[...]
        part = 0.5 * jnp.sum(jnp.mean(err * err, axis=-1, keepdims=True), axis=0, keepdims=True)
        dy = err * (1.0 / D)
        a = dy * g_ref[...]
        dx_ref[...] = rstd * (a - xn * jnp.mean(a * xn, axis=-1, keepdims=True))
        dg = jnp.sum(dy * xn, axis=0, keepdims=True)
        lpart = jnp.broadcast_to(part, (1, LANES))

        @pl.when(pl.program_id(0) == 0)
        def _():
            dg_ref[...] = dg
            loss_ref[...] = lpart

        @pl.when(pl.program_id(0) != 0)
        def _():
            dg_ref[...] += dg
            loss_ref[...] += lpart

    row = pl.BlockSpec((tm, D), lambda i: (i, 0))
    return pl.pallas_call(
        body, grid=(T // tm,), name=name,
        in_specs=[row, pl.BlockSpec((1, D), lambda i: (0, 0)), row],
        out_specs=[row, pl.BlockSpec((1, D), lambda i: (0, 0)), pl.BlockSpec((1, LANES), lambda i: (0, 0))],
        out_shape=[jax.ShapeDtypeStruct((T, D), F32), jax.ShapeDtypeStruct((1, D), F32),
                   jax.ShapeDtypeStruct((1, LANES), F32)],
        compiler_params=_params(("arbitrary",)),
    )(x, g, target)


def _row_tile(rows, cols, budget_elems=512 * 1024):
    tr = rows
    while tr * cols > budget_elems and tr % 32 == 0:
        tr //= 2
    return tr


def _prefetch_call(body, grid, in_specs, out_specs, out_shape, name, sem, aliases=None):
    return pl.pallas_call(
        body, name=name, out_shape=out_shape,
        grid_spec=pltpu.PrefetchScalarGridSpec(num_scalar_prefetch=1, grid=grid, in_specs=in_specs,
                                               out_specs=out_specs),
        input_output_aliases=aliases or {},
        compiler_params=_params(sem))


def _cast_into_full(w, layer, colwise, where, name):
    _, K, N = w.shape
    tr = _row_tile(K, N)
    nrb = K // tr
    full_shape = (1, K, 4 * N) if colwise else (1, 4 * K, N)

    def body(where_ref, w_ref, o_ref):
        del where_ref
        o_ref[...] = w_ref[...].astype(BF16)

    if colwise:
        out_map = lambda i, wh: (0, i, wh[0])
    else:
        out_map = lambda i, wh: (0, wh[0] * nrb + i, 0)
    return _prefetch_call(
        body, (nrb,), [pl.BlockSpec((None, tr, N), lambda i, wh: (layer, i, 0))],
        pl.BlockSpec((None, tr, N), out_map), pltpu.HBM(full_shape, BF16), name,
        ("parallel",))(where, w)


def _chip_sum(g3, other, colwise, where, name):
    L, K, N = g3.shape
    hk, hn = (K // 2, N) if colwise else (K, N // 2)
    tr = _row_tile(hk, hn)
    nrb = hk // tr

    def body(where_ref, g_ref, o_ref, s_ref):
        del where_ref
        s_ref[...] = (g_ref[...].astype(F32) + o_ref[...].astype(F32)).astype(BF16)

    if colwise:
        g_map = lambda l, i, wh: (l, wh[1] * nrb + i, 0)
    else:
        g_map = lambda l, i, wh: (l, i, wh[1])
    blk = pl.BlockSpec((None, tr, hn), lambda l, i, wh: (l, i, 0))
    return _prefetch_call(
        body, (L, nrb), [pl.BlockSpec((None, tr, hn), g_map), blk], blk,
        pltpu.HBM((L, hk, hn), BF16), name, ("parallel", "parallel"))(where, _in_hbm(g3), _in_hbm(other))


def _owner_sum(chip_sum, received, colwise, where, layer, n_layers, prev, name):
    _, hk, hn = chip_sum.shape
    pk, pn = (hk, hn // 4) if colwise else (hk // 4, hn)
    tr = _row_tile(pk, pn)
    nrb = pk // tr
    shard_shape = (n_layers, 2 * pk, pn) if colwise else (n_layers, pk, 2 * pn)

    def body(where_ref, own_ref, rec_ref, *rest):
        del where_ref
        o_ref = rest[-1]
        acc = own_ref[...].astype(F32)
        for j in range(3):
            acc = acc + rec_ref[j].astype(F32)
        o_ref[...] = acc

    if colwise:
        own_map = lambda i, wh: (0, i, wh[0])
        out_map = lambda i, wh: (layer, wh[1] * nrb + i, 0)
    else:
        own_map = lambda i, wh: (0, wh[0] * nrb + i, 0)
        out_map = lambda i, wh: (layer, i, wh[1])
    in_specs = [pl.BlockSpec((None, tr, pn), own_map),
                pl.BlockSpec((3, None, tr, pn), lambda i, wh: (0, 0, i, 0))]
    operands = [where, _in_hbm(chip_sum), _in_hbm(received)]
    if prev is not None:
        in_specs.append(ANY)
        operands.append(prev)
    return _prefetch_call(
        body, (nrb,), in_specs, pl.BlockSpec((None, tr, pn), out_map), pltpu.HBM(shard_shape, F32), name,
        ("parallel",), None if prev is None else {3: 0})(*operands)


def _adamw(w, g, m, v, name):
    R, Cc = w.shape
    tr = _row_tile(R, Cc, 256 * 1024)

    def body(w_ref, g_ref, m_ref, v_ref, d_ref, nm_ref, nv_ref, go_ref):
        gv = g_ref[...]
        go_ref[...] = gv
        nm = ADAM_B1 * m_ref[...] + (1.0 - ADAM_B1) * gv
        nv = ADAM_B2 * v_ref[...] + (1.0 - ADAM_B2) * (gv * gv)
        m_hat = nm / (1.0 - ADAM_B1 ** ADAM_STEP)
        v_hat = nv / (1.0 - ADAM_B2 ** ADAM_STEP)
        d_ref[...] = -ADAM_LR * (m_hat / (jnp.sqrt(v_hat) + ADAM_EPS) + ADAM_WD * w_ref[...])
        nm_ref[...] = nm
        nv_ref[...] = nv

    blk = pl.BlockSpec((tr, Cc), lambda i: (i, 0))
    shp = jax.ShapeDtypeStruct((R, Cc), F32)
    return pl.pallas_call(
        body, grid=(R // tr,), name=name,
        in_specs=[blk] * 4, out_specs=[blk] * 4, out_shape=[shp] * 4,
        compiler_params=_params(("parallel",)),
    )(w, g, m, v)


COL_SHARDED = (True, False, True, False)


def _position():
    x, y, c = lax.axis_index("x"), lax.axis_index("y"), lax.axis_index("c")
    chips = [(1 - x, y), (x, 1 - y), (1 - x, 1 - y)]
    return x, y, c, chips


def _span(index, size, align):
    return pl.ds(pl.multiple_of(index * size, align), size)


def _window(ref, colwise, shard, half, shards=4):
    _, K, N = ref.shape
    rows = cols = slice(None)
    if colwise:
        if half is not None:
            rows = _span(half, K // 2, 16)
        if shard is not None:
            cols = _span(shard, N // shards, LANES)
    else:
        if shard is not None:
            rows = _span(shard, K // shards, 16)
        if half is not None:
            cols = _span(half, N // 2, LANES)
    return ref.at[:, rows, cols]


HBM = pl.BlockSpec(memory_space=pltpu.HBM)
SEMAPHORES = pl.BlockSpec(memory_space=pltpu.SEMAPHORE)


def _gather_start(fulls, colwise, group_sizes, after, name):
    n = len(fulls)
    n_groups = len(group_sizes)

    n_in = n if after is None else n + 1

    def body(*refs):
        ins = refs[:n]
        sems = refs[n_in:n_in + 2 * n_groups]
        x, y, c, chips = _position()
        me = 2 * x + y
        i = 0
        for g, size in enumerate(group_sizes):
            for a in range(size):
                win = _window(ins[i], colwise[i], me, c)
                for j, chip in enumerate(chips):
                    pltpu.make_async_remote_copy(
                        src_ref=win, dst_ref=win, send_sem=sems[2 * g].at[a * 3 + j],
                        recv_sem=sems[2 * g + 1].at[a * 3 + j],
                        device_id=(chip[0], chip[1], c), device_id_type=MESH_ID).start()
                i += 1

    sem_shapes = []
    for size in group_sizes:
        sem_shapes += [pltpu.SemaphoreType.DMA((3 * size,)), pltpu.SemaphoreType.DMA((3 * size,))]
    operands = [pltpu.with_memory_space_constraint(f, pltpu.HBM) for f in fulls]
    in_specs = [HBM] * n
    if after is not None:
        operands.append(after)
        in_specs.append(ANY)
    outs = pl.pallas_call(
        body, name=name,
        in_specs=in_specs, out_specs=[SEMAPHORES] * (2 * n_groups) + [HBM] * n,
        out_shape=sem_shapes + [pltpu.HBM(f.shape, f.dtype) for f in fulls],
        input_output_aliases={i: 2 * n_groups + i for i in range(n)},
        compiler_params=pltpu.CompilerParams(has_side_effects=pltpu.SideEffectType.DATAFLOW_SIDE_EFFECTING),
    )(*operands)
    sems = [(outs[2 * g], outs[2 * g + 1]) for g in range(n_groups)]
    return sems, list(outs[2 * n_groups:])


def _to_sibling(ref, colwise, chip, half, x, y, c, send_sem, recv_sem):
    win = _window(ref, colwise, 2 * chip[0] + chip[1], half)
    return pltpu.make_async_remote_copy(
        src_ref=win, dst_ref=win, send_sem=send_sem, recv_sem=recv_sem,
        device_id=(x, y, 1 - c), device_id_type=MESH_ID)


def _gather_pass(in_flight, colwise, sems, after, name):
    n = len(in_flight)

    def body(*refs):
        ins = refs[:n]
        send_sems, recv_sems = refs[n], refs[n + 1]
        pass_send, pass_recv = refs[-2 - n], refs[-1 - n]
        x, y, c, chips = _position()
        me = 2 * x + y
        for a in range(n):
            for j, chip in enumerate(chips):
                k = a * 3 + j
                pltpu.make_async_remote_copy(
                    src_ref=_window(ins[a], colwise[a], me, c),
                    dst_ref=_window(ins[a], colwise[a], 2 * chip[0] + chip[1], c),
                    send_sem=send_sems.at[k], recv_sem=recv_sems.at[k],
                    device_id=(chip[0], chip[1], c), device_id_type=MESH_ID).wait()
                _to_sibling(ins[a], colwise[a], chip, c, x, y, c, pass_send.at[k], pass_recv.at[k]).start()

    operands = list(in_flight) + list(sems)
    in_specs = [HBM] * n + [SEMAPHORES] * 2
    if after is not None:
        operands.append(after)
        in_specs.append(ANY)
    outs = pl.pallas_call(
        body, name=name,
        in_specs=in_specs, out_specs=[SEMAPHORES] * 2 + [HBM] * n,
        out_shape=[pltpu.SemaphoreType.DMA((3 * n,)), pltpu.SemaphoreType.DMA((3 * n,))]
        + [pltpu.HBM(f.shape, f.dtype) for f in in_flight],
        input_output_aliases={i: 2 + i for i in range(n)},
        compiler_params=pltpu.CompilerParams(has_side_effects=pltpu.SideEffectType.DATAFLOW_SIDE_EFFECTING),
    )(*operands)
    return (outs[0], outs[1]), list(outs[2:])


def _gather_wait(in_flight, colwise, sems, after, name):
    n = len(in_flight)

    def body(*refs):
        ins = refs[:n]
        send_sems, recv_sems = refs[n], refs[n + 1]
        x, y, c, chips = _position()
        for a in range(n):
            for j, chip in enumerate(chips):
                k = a * 3 + j
                _to_sibling(ins[a], colwise[a], chip, c, x, y, c, send_sems.at[k], recv_sems.at[k]).wait_send()
                _to_sibling(ins[a], colwise[a], chip, 1 - c, x, y, c, send_sems.at[k], recv_sems.at[k]).wait_recv()

    operands = list(in_flight) + list(sems)
    in_specs = [HBM] * n + [SEMAPHORES] * 2
    if after is not None:
        operands.append(after)
        in_specs.append(ANY)
    outs = pl.pallas_call(
        body, name=name,
        in_specs=in_specs, out_specs=[HBM] * n,
        out_shape=[pltpu.HBM(f.shape, f.dtype) for f in in_flight],
        input_output_aliases={i: i for i in range(n)},
        compiler_params=pltpu.CompilerParams(has_side_effects=pltpu.SideEffectType.DATAFLOW_SIDE_EFFECTING),
    )(*operands)
    return list(outs)


def _exchange_copy(g_ref, land_ref, colwise, x, y, c, send_sem, recv_sem):
    return pltpu.make_async_remote_copy(
        src_ref=_window(g_ref, colwise, None, 1 - c), dst_ref=land_ref, send_sem=send_sem, recv_sem=recv_sem,
        device_id=(x, y, 1 - c), device_id_type=MESH_ID)


def _exchange_start(grads, colwise, name):
    n = len(grads)
    lands = []
    for g, cw in zip(grads, colwise):
        L, K, N = g.shape
        lands.append(lax.empty((L, K // 2, N) if cw else (L, K, N // 2), g.dtype))

    def body(*refs):
        src, land = refs[:n], refs[n:2 * n]
        send_sems, recv_sems = refs[2 * n], refs[2 * n + 1]
        x, y, c, _ = _position()
        for i in range(n):
            _exchange_copy(src[i], land[i], colwise[i], x, y, c, send_sems.at[i], recv_sems.at[i]).start()

    arrays = list(grads) + lands
    outs = pl.pallas_call(
        body, name=name,
        in_specs=[HBM] * (2 * n), out_specs=[SEMAPHORES] * 2 + [HBM] * (2 * n),
        out_shape=[pltpu.SemaphoreType.DMA((n,)), pltpu.SemaphoreType.DMA((n,))]
        + [pltpu.HBM(a.shape, a.dtype) for a in arrays],
        input_output_aliases={i: 2 + i for i in range(2 * n)},
        compiler_params=pltpu.CompilerParams(has_side_effects=pltpu.SideEffectType.DATAFLOW_SIDE_EFFECTING),
    )(*[pltpu.with_memory_space_constraint(a, pltpu.HBM) for a in arrays])
    return (outs[0], outs[1]), list(outs[2:2 + n]), list(outs[2 + n:])


def _exchange_wait(grads, lands, colwise, sems, after, name):
    n = len(grads)

    def body(*refs):
        src, land = refs[:n], refs[n:2 * n]
        send_sems, recv_sems = refs[2 * n], refs[2 * n + 1]
        x, y, c, _ = _position()
        for i in range(n):
            _exchange_copy(src[i], land[i], colwise[i], x, y, c, send_sems.at[i], recv_sems.at[i]).wait()

    arrays = list(grads) + list(lands)
    operands = arrays + list(sems)
    in_specs = [HBM] * (2 * n) + [SEMAPHORES] * 2
    if after is not None:
        operands.append(after)
        in_specs.append(ANY)
    outs = pl.pallas_call(
        body, name=name,
        in_specs=in_specs, out_specs=[HBM] * (2 * n),
        out_shape=[pltpu.HBM(a.shape, a.dtype) for a in arrays],
        input_output_aliases={i: i for i in range(2 * n)},
        compiler_params=pltpu.CompilerParams(has_side_effects=pltpu.SideEffectType.DATAFLOW_SIDE_EFFECTING),
    )(*operands)
    return list(outs[:n]), list(outs[n:])


def _scatter_copy(src_ref, land_ref, colwise, j, chip, c, send_sem, recv_sem):
    return pltpu.make_async_remote_copy(
        src_ref=_window(src_ref, colwise, 2 * chip[0] + chip[1], None), dst_ref=land_ref.at[j],
        send_sem=send_sem, recv_sem=recv_sem, device_id=(chip[0], chip[1], c), device_id_type=MESH_ID)


def _scatter_start(chip_sums, colwise, name):
    n = len(chip_sums)
    lands = []
    for g, cw in zip(chip_sums, colwise):
        L, hk, hn = g.shape
        lands.append(lax.empty((3, L, hk, hn // 4) if cw else (3, L, hk // 4, hn), g.dtype))

    def body(*refs):
        src, land = refs[:n], refs[n:2 * n]
        send_sems, recv_sems = refs[2 * n], refs[2 * n + 1]
        x, y, c, chips = _position()
        for i in range(n):
            for j, chip in enumerate(chips):
                _scatter_copy(src[i], land[i], colwise[i], j, chip, c, send_sems.at[i * 3 + j],
                              recv_sems.at[i * 3 + j]).start()

    arrays = list(chip_sums) + lands
    outs = pl.pallas_call(
        body, name=name,
        in_specs=[HBM] * (2 * n), out_specs=[SEMAPHORES] * 2 + [HBM] * (2 * n),
        out_shape=[pltpu.SemaphoreType.DMA((3 * n,)), pltpu.SemaphoreType.DMA((3 * n,))]
        + [pltpu.HBM(a.shape, a.dtype) for a in arrays],
        input_output_aliases={i: 2 + i for i in range(2 * n)},
        compiler_params=pltpu.CompilerParams(has_side_effects=pltpu.SideEffectType.DATAFLOW_SIDE_EFFECTING),
    )(*[pltpu.with_memory_space_constraint(a, pltpu.HBM) for a in arrays])
    return (outs[0], outs[1]), list(outs[2:2 + n]), list(outs[2 + n:])


def _scatter_wait(sources, lands, colwise, sems, after, name):
    n = len(sources)

    def body(*refs):
        src, land = refs[:n], refs[n:2 * n]
        send_sems, recv_sems = refs[2 * n], refs[2 * n + 1]
        x, y, c, chips = _position()
        for i in range(n):
            for j, chip in enumerate(chips):
                cp = _scatter_copy(src[i], land[i], colwise[i], j, chip, c, send_sems.at[i * 3 + j],
                                   recv_sems.at[i * 3 + j])
                cp.wait_send()
                cp.wait_recv()

    arrays = list(sources) + list(lands)
    operands = arrays + list(sems)
    in_specs = [HBM] * (2 * n) + [SEMAPHORES] * 2
    if after is not None:
        operands.append(after)
        in_specs.append(ANY)
    outs = pl.pallas_call(
        body, name=name,
        in_specs=in_specs, out_specs=[HBM] * (2 * n),
        out_shape=[pltpu.HBM(a.shape, a.dtype) for a in arrays],
        input_output_aliases={i: i for i in range(2 * n)},
        compiler_params=pltpu.CompilerParams(has_side_effects=pltpu.SideEffectType.DATAFLOW_SIDE_EFFECTING),
    )(*operands)
    return list(outs[:n]), list(outs[n:])


def _share_with_sibling(shards, colwise, name):
    n = len(shards)

    def body(*refs):
        out = refs[n:2 * n]
        send_sems, recv_sems = refs[2 * n:]
        x, y, c, _ = _position()

        def copy(i, half):
            win = _window(out[i], colwise[i], None, half)
            return pltpu.make_async_remote_copy(
                src_ref=win, dst_ref=win, send_sem=send_sems.at[i], recv_sem=recv_sems.at[i],
                device_id=(x, y, 1 - c), device_id_type=MESH_ID)

        for i in range(n):
            copy(i, c).start()
        for i in range(n):
            copy(i, 1 - c).wait_recv()
        for i in range(n):
            copy(i, c).wait_send()

    return pl.pallas_call(
        body, name=name,
        in_specs=[ANY] * n, out_specs=[ANY] * n,
        out_shape=[jax.ShapeDtypeStruct(s.shape, s.dtype) for s in shards],
        input_output_aliases={i: i for i in range(n)},
        scratch_shapes=[pltpu.SemaphoreType.DMA((n,)), pltpu.SemaphoreType.DMA((n,))],
    )(*shards)


def _chip_exchange(buf, me, x, y, c, chips, send_sems, recv_sems):
    def copy(j, chip, slot):
        return pltpu.make_async_remote_copy(
            src_ref=buf.at[me], dst_ref=buf.at[slot], send_sem=send_sems.at[j], recv_sem=recv_sems.at[j],
            device_id=(chip[0], chip[1], c), device_id_type=MESH_ID)

    for j, chip in enumerate(chips):
        copy(j, chip, me).start()
    for j, chip in enumerate(chips):
        copy(j, chip, 2 * chip[0] + chip[1]).wait_recv()
    for j, chip in enumerate(chips):
        copy(j, chip, me).wait_send()


def _gather_over_chips(pack, name):
    R, Cc = pack.shape

    def body(p_ref, o_ref, send_sems, recv_sems):
        x, y, c, chips = _position()
        me = 2 * x + y
        o_ref[me] = p_ref[...]
        _chip_exchange(o_ref, me, x, y, c, chips, send_sems, recv_sems)

    vmem = pl.BlockSpec(memory_space=pltpu.VMEM)
    return pl.pallas_call(
        body, name=name,
        in_specs=[vmem], out_specs=vmem, out_shape=jax.ShapeDtypeStruct((4, R, Cc), F32),
        scratch_shapes=[pltpu.SemaphoreType.DMA((3,)), pltpu.SemaphoreType.DMA((3,))],
    )(pack)


def _all_reduce_small(pack, name):
    R, Cc = pack.shape

    def body(p_ref, o_ref, sibling, buf, send_sems, recv_sems):
        x, y, c, chips = _position()
        me = 2 * x + y
        swap = pltpu.make_async_remote_copy(
            src_ref=p_ref, dst_ref=sibling, send_sem=send_sems.at[3], recv_sem=recv_sems.at[3],
            device_id=(x, y, 1 - c), device_id_type=MESH_ID)
        swap.start()
        swap.wait()
        buf[me] = p_ref[...] + sibling[...]
        _chip_exchange(buf, me, x, y, c, chips, send_sems, recv_sems)
        o_ref[...] = (buf[0] + buf[1]) + (buf[2] + buf[3])

    vmem = pl.BlockSpec(memory_space=pltpu.VMEM)
    return pl.pallas_call(
        body, name=name,
        in_specs=[vmem], out_specs=vmem, out_shape=jax.ShapeDtypeStruct((R, Cc), F32),
        scratch_shapes=[pltpu.VMEM((R, Cc), F32), pltpu.VMEM((4, R, Cc), F32), pltpu.SemaphoreType.DMA((4,)),
                        pltpu.SemaphoreType.DMA((4,))],
    )(pack)


def _local_forward_backward(x2, target2, S, pass_on, fetch, reduce_begin, reduce_commit, layers, final_g,
                            tm=ROW_TILE):
    T, D = x2.shape
    C = D // 2
    n_heads = C // GROUP
    n_layers = len(layers)
    weights = {}
    saved = []
    xc = x2
    for li, lw in enumerate(layers):
        if li == 0:
            pass_on(0, None)
            weights.update(fetch(0, None))
        h1, qkv3, cv3 = _norm_proj(xc, lw["norm1"], weights[li, "w_in"], ((3, C, F32), (3, C, BF16)), tm,
                                   min(C, 512), f"l{li}_norm_in_proj")
        if li == 0:
            pass_on(1, h1)
        o, lse, mix = _attn_fwd(qkv3, lw["attn_g"], 2, S, n_heads, f"l{li}_attn_fwd")
        pin = None
        if li == 0:
            weights.update(fetch(1, o))
            pin = pass_on(3, pass_on(2, o))
        mix = _mix_conv_fwd(cv3, lw["taps"], lw["conv_g"], mix, S, f"l{li}_mix_conv_fwd", pin)
        x_mid = _proj_residual(mix, weights[li, "w_out"], xc, tm, f"l{li}_out_proj")
        if li == 0:
            weights.update(fetch(2, x_mid))
        Fd = weights[li, "ffn_up"].shape[2] // 2
        h2, up3 = _norm_proj(x_mid, lw["norm2"], weights[li, "ffn_up"], ((2, Fd, BF16),), tm, DOT_CHUNK,
                             f"l{li}_norm_ffn_up")
        if li == 0:
            weights.update(fetch(3, up3))
        act = _ffn_act_fwd(up3, lw["ffn_taps"], S, f"l{li}_ffn_act_fwd")
        pin = pass_on(li + 4, act) if li + 1 < n_layers else None
        x_out = _proj_residual(act.reshape(1, T, Fd), weights[li, "ffn_down"], x_mid, tm, f"l{li}_ffn_down", pin)
        if li + 1 < n_layers:
            weights.update(fetch(li + 4, x_out))
        saved.append(dict(x_in=xc, h1=h1, qkv3=qkv3, cv3=cv3, o=o, lse=lse, mix=mix, x_mid=x_mid, h2=h2, up3=up3,
                          act=act))
        xc = x_out

    dx, d_final_g, loss_part = _final_norm_loss(xc, final_g, target2, tm, "final_norm_loss")

    small = [None] * n_layers
    started = None
    for li in reversed(range(n_layers)):
        lw, sv = layers[li], saved[li]
        w_in, w_out, ffn_up, ffn_down = (weights[li, n] for n in ("w_in", "w_out", "ffn_up", "ffn_down"))
        dxb, dact3 = _grad_through_weight(dx, ffn_down, 1, Fd, BF16, tm, DOT_CHUNK, f"l{li}_d_act", started)
        Fd = ffn_down.shape[1]
        d_ffn_down = _weight_grad(sv["act"].reshape(1, T, Fd), dxb.reshape(1, T, D), DOT_CHUNK, D,
                                  f"l{li}_d_ffn_down")
        dup3, d_taps_g, d_taps_v = _ffn_act_bwd(sv["up3"], dact3, lw["ffn_taps"], S, f"l{li}_ffn_act_bwd")
        d_ffn_up = _weight_grad(sv["h2"].reshape(1, D, T), dup3, D, DOT_CHUNK, f"l{li}_d_ffn_up", a_transposed=True)
        if li == 0:
            early = reduce_begin({(li, "ffn_down"): d_ffn_down, (li, "ffn_up"): d_ffn_up})
        dx_mid, d_norm2 = _grad_through_proj_norm(dup3, ffn_up, sv["x_mid"], lw["norm2"], dx, tm,
                                                  f"l{li}_d_norm2")
        started = reduce_commit(early, dx_mid) if li == 0 else None
        dxmb, dmix3 = _grad_through_weight(dx_mid, w_out, 2, C, F32, tm, min(C, 512), f"l{li}_d_mix", started)
        d_w_out = _weight_grad(sv["mix"], dxmb.reshape(1, T, D), min(C, 256), D, f"l{li}_d_w_out")
        dproj, d_attn_g = _attn_bwd(sv["qkv3"], sv["o"], sv["lse"], dmix3, lw["attn_g"], 6, S, n_heads,
                                    f"l{li}_attn_bwd")
        dproj, d_taps, d_conv_g = _mix_conv_bwd(sv["cv3"], dmix3, lw["taps"], lw["conv_g"], dproj, S,
                                                f"l{li}_mix_conv_bwd")
        d_w_in = _weight_grad(sv["h1"].reshape(1, D, T), dproj, D, C, f"l{li}_d_w_in", a_transposed=True)
        late = {(li, "w_out"): d_w_out, (li, "w_in"): d_w_in}
        if li > 0:
            late.update({(li, "ffn_down"): d_ffn_down, (li, "ffn_up"): d_ffn_up})
        late = reduce_begin(late)
        dx, d_norm1 = _grad_through_proj_norm(dproj, w_in, sv["x_in"], lw["norm1"], dx_mid, tm,
                                              f"l{li}_d_norm1")
        started = reduce_commit(late, dx)
        small[li] = dict(norm1=d_norm1, taps=d_taps, attn_g=d_attn_g, conv_g=d_conv_g, norm2=d_norm2,
                         ffn_taps=jnp.concatenate([d_taps_g, d_taps_v], axis=1))
    return loss_part, dx, small, d_final_g


SMALL_ORDER = ("norm1", "attn_g", "conv_g", "norm2", "taps", "ffn_taps")


def _pack_small(small, d_final_g, loss_row):
    parts = [small[li][k].reshape(-1) for li in range(len(small)) for k in SMALL_ORDER]
    loss_rows = jnp.tile(loss_row.reshape(1, LANES), (8, 1))
    return jnp.concatenate(parts + [d_final_g.reshape(-1), loss_rows.reshape(-1)]).reshape(-1, LANES)


def _unpack_small(pack, small, d_final_g):
    flat = pack.reshape(-1)
    out, pos = [dict() for _ in small], 0
    for li in range(len(small)):
        for k in SMALL_ORDER:
            n = small[li][k].size
            out[li][k] = flat[pos:pos + n].reshape(small[li][k].shape)
            pos += n
    return out, flat[pos:pos + d_final_g.size], flat[pos + d_final_g.size]


def kernel(x, norm1_g, w_in, mix_conv_w, attn_out_g, conv_out_g, w_out, norm2_g, ffn_up, ffn_conv_w, ffn_down, final_norm_g, loss_target, m_norm1_g, m_w_in, m_mix_conv_w, m_attn_out_g, m_conv_out_g, m_w_out, m_norm2_g, m_ffn_up, m_ffn_conv_w, m_ffn_down, m_final_norm_g, v_norm1_g, v_w_in, v_mix_conv_w, v_attn_out_g, v_conv_out_g, v_w_out, v_norm2_g, v_ffn_up, v_ffn_conv_w, v_ffn_down, v_final_norm_g):
    Bl, S, D = x.shape
    L = w_in.shape[0]
    T = Bl * S
    shard = 2 * lax.axis_index("x") + lax.axis_index("y")
    where = jnp.stack([shard, lax.axis_index("c")]).astype(jnp.int32)
    big_names = ("w_in", "w_out", "ffn_up", "ffn_down")

    taps_w, ftaps_w = mix_conv_w.shape[2], ffn_conv_w.shape[2]
    tap_pack = _gather_over_chips(
        jnp.concatenate([mix_conv_w.reshape(-1), ffn_conv_w.reshape(-1)]).reshape(-1, LANES), "all_gather_taps")
    by_chip = tap_pack.reshape(4, -1)
    n_taps = mix_conv_w.size
    taps_full = by_chip[:, :n_taps].reshape(4, L, 3, taps_w).transpose(1, 2, 0, 3).reshape(L, 3, 4 * taps_w)
    ftaps_full = by_chip[:, n_taps:].reshape(4, L, 3, ftaps_w).transpose(1, 2, 0, 3).reshape(L, 3, 4 * ftaps_w)

    big_shards = dict(zip(big_names, (w_in, w_out, ffn_up, ffn_down)))
    col_of = dict(zip(big_names, COL_SHARDED))
    groups = [[(0, n)] for n in big_names] + [[(l, n) for n in big_names] for l in range(1, L)]
    sems, in_flight = [], {}
    all_started = tap_pack
    for first, last in ((0, 1), (1, len(groups))):
        keys = [k for g in groups[first:last] for k in g]
        new_sems, arrays = _gather_start(
            [_cast_into_full(big_shards[n], l, col_of[n], where, f"cast_{n}_{l}") for l, n in keys],
            [col_of[n] for _, n in keys], [len(g) for g in groups[first:last]], all_started,
            f"gather_start_{first}")
        sems += new_sems
        in_flight.update(zip(keys, arrays))
        all_started = arrays[-1]

    def pass_on(g, after):
        after = all_started if g == 0 else after
        sems[g], arrays = _gather_pass([in_flight[k] for k in groups[g]], [col_of[n] for _, n in groups[g]],
                                       sems[g], after, f"gather_pass_{g}")
        in_flight.update(zip(groups[g], arrays))
        return arrays[0]

    def fetch(g, after):
        done = _gather_wait([in_flight[k] for k in groups[g]], [col_of[n] for _, n in groups[g]], sems[g], after,
                            f"gather_wait_{g}")
        return dict(zip(groups[g], done))

    pending = []

    begun = []

    def reduce_begin(grads):
        g = len(begun)
        keys = list(grads)
        cols = [col_of[n] for _, n in keys]
        begun.append((g, keys, cols) + _exchange_start([grads[k] for k in keys], cols, f"exchange_start_{g}"))
        return begun[-1]

    def reduce_commit(handle, after):
        g, keys, cols, ex_sems, mine, lands = handle
        mine, others = _exchange_wait(mine, lands, cols, ex_sems, after, f"exchange_wait_{g}")
        chip_sums = [_chip_sum(m, o, cw, where, f"chip_sum_{k[1]}_{k[0]}")
                     for k, m, o, cw in zip(keys, mine, others, cols)]
        pending.append((keys, cols) + _scatter_start(chip_sums, cols, f"scatter_start_{g}"))
        return pending[-1][3][0]

    layers = [dict(norm1=norm1_g[l:l + 1], taps=taps_full[l], attn_g=attn_out_g[l:l + 1],
                   conv_g=conv_out_g[l:l + 1], norm2=norm2_g[l:l + 1], ffn_taps=ftaps_full[l]) for l in range(L)]

    loss_part, dx, small, d_final_g = _local_forward_backward(
        x.reshape(T, D), loss_target.reshape(T, D), S, pass_on, fetch, reduce_begin, reduce_commit, layers,
        final_norm_g.reshape(1, D))

    def finish_group(g, after):
        keys, cols, rs_sems, sources, lands = pending[g]
        sources, lands = _scatter_wait(sources, lands, cols, rs_sems, after, f"scatter_wait_{g}")
        for (l, n), cw, src, land in zip(keys, cols, sources, lands):
            reduced[n] = _owner_sum(src, land, cw, where, l, L, reduced[n], f"owner_sum_{n}_{l}")

    reduced = dict.fromkeys(big_names)
    last_started = pending[-1][3][0]
    for g in range(len(pending) - 1):
        finish_group(g, last_started)
    late_names = [n for n in big_names if any(n == name for _, name in pending[-1][0])]
    early_names = [n for n in big_names if n not in late_names]
    g_big = dict(zip(early_names, _share_with_sibling([reduced[n] for n in early_names],
                                                      [col_of[n] for n in early_names], "grad_share_early")))

    pack = _all_reduce_small(_pack_small(small, d_final_g, loss_part), "all_reduce_small_grads")
    g_small, g_final, loss = _unpack_small(pack, small, d_final_g)

    def stacked(key):
        return jnp.stack([g_small[l][key].reshape(g_small[l][key].shape[-2:] if key.endswith("taps") else (-1,))
                          for l in range(L)])

    g_norm1, g_attn, g_conv, g_norm2 = stacked("norm1"), stacked("attn_g"), stacked("conv_g"), stacked("norm2")
    g_taps = lax.dynamic_slice(stacked("taps"), (0, 0, shard * taps_w), (L, 3, taps_w))
    g_ftaps = lax.dynamic_slice(stacked("ffn_taps"), (0, 0, shard * ftaps_w), (L, 3, ftaps_w))

    grads_out = dict(norm1_g=g_norm1, w_in=None, mix_conv_w=g_taps, attn_out_g=g_attn, conv_out_g=g_conv,
                     w_out=None, norm2_g=g_norm2, ffn_up=None, ffn_conv_w=g_ftaps, ffn_down=None,
                     final_norm_g=g_final)
    weights = dict(norm1_g=norm1_g, w_in=w_in, mix_conv_w=mix_conv_w, attn_out_g=attn_out_g, conv_out_g=conv_out_g,
                   w_out=w_out, norm2_g=norm2_g, ffn_up=ffn_up, ffn_conv_w=ffn_conv_w, ffn_down=ffn_down,
                   final_norm_g=final_norm_g)
    ms = dict(norm1_g=m_norm1_g, w_in=m_w_in, mix_conv_w=m_mix_conv_w, attn_out_g=m_attn_out_g,
              conv_out_g=m_conv_out_g, w_out=m_w_out, norm2_g=m_norm2_g, ffn_up=m_ffn_up, ffn_conv_w=m_ffn_conv_w,
              ffn_down=m_ffn_down, final_norm_g=m_final_norm_g)
    vs = dict(norm1_g=v_norm1_g, w_in=v_w_in, mix_conv_w=v_mix_conv_w, attn_out_g=v_attn_out_g,
              conv_out_g=v_conv_out_g, w_out=v_w_out, norm2_g=v_norm2_g, ffn_up=v_ffn_up, ffn_conv_w=v_ffn_conv_w,
              ffn_down=v_ffn_down, final_norm_g=v_final_norm_g)
    names = list(weights)
    small_names = [n for n in names if n not in big_names]
    delta, new_m, new_v = {}, {}, {}

    def update_big(n):
        shp = weights[n].shape
        two_d = (shp[0] * shp[1], shp[2])
        d_, m_, v_, g_ = _adamw(weights[n].reshape(two_d), g_big[n].reshape(two_d), ms[n].reshape(two_d),
                                vs[n].reshape(two_d), f"adamw_{n}")
        delta[n], new_m[n], new_v[n], grads_out[n] = (a.reshape(shp) for a in (d_, m_, v_, g_))

    for n in early_names:
        update_big(n)
    finish_group(len(pending) - 1, delta[early_names[-1]] if early_names else None)
    g_big.update(zip(late_names, _share_with_sibling([reduced[n] for n in late_names],
                                                     [col_of[n] for n in late_names], "grad_share_late")))
    for n in late_names:
        update_big(n)

    def packed(tree):
        return jnp.concatenate([tree[n].reshape(-1) for n in small_names]).reshape(-1, LANES)

    d_, m_, v_, _ = _adamw(packed(weights), packed(grads_out), packed(ms), packed(vs), "adamw_small")
    pos = 0
    for n in small_names:
        size, shp = weights[n].size, weights[n].shape
        delta[n] = d_.reshape(-1)[pos:pos + size].reshape(shp)
        new_m[n] = m_.reshape(-1)[pos:pos + size].reshape(shp)
        new_v[n] = v_.reshape(-1)[pos:pos + size].reshape(shp)
        pos += size

    return (loss, dx.reshape(Bl, S, D), *[grads_out[n] for n in names], *[delta[n] for n in names],
            *[new_m[n] for n in names], *[new_v[n] for n in names])
```
